```python
import jax, jax.numpy as jnp
from jax import lax
import numpy as np

D_MODEL = 1024
BATCH = 8
SEQ = 8192
DEPTH = 1

HEAD_DIM = 128
ATT_HEADS = 4
DILATED_PAIRS = ((128, 1), (512, 4), (2048, 16))
N_DIL_GROUPS = len(DILATED_PAIRS)
Q_W = N_DIL_GROUPS * ATT_HEADS * HEAD_DIM
KV_W = ATT_HEADS * HEAD_DIM
POOL_WINDOWS = (2, 4, 8, 16)
N_POOL_GROUPS = len(POOL_WINDOWS)
POOL_GROUP_DIM = 128
POOL_W = N_POOL_GROUPS * POOL_GROUP_DIM
MIX_W = KV_W + POOL_W
IN_W = Q_W + 2 * KV_W + POOL_W
BLOCK = 128
ROT_DIM = HEAD_DIM // 4
ROT_HALF = ROT_DIM // 2
ROPE_THETA = 500000.0
N_MEM = 256
X_HEADS = 4
X_W = X_HEADS * HEAD_DIM
D_FF = ((8 * D_MODEL // 3 + 255) // 256) * 256
EPS = 1e-6
NEG_INF = -1e30

kernel_name = "hybrid_pool_dilated_attn_block"


def rms_norm(x, g):
    xf = x.astype(jnp.float32)
    y = xf * lax.rsqrt(jnp.mean(xf * xf, axis=-1, keepdims=True) + EPS)
    return (y * g.astype(jnp.float32)).astype(x.dtype)


def rope_partial(t, cos, sin):
    ex = tuple(range(2, t.ndim - 1))
    c = jnp.expand_dims(cos, ex)
    s = jnp.expand_dims(sin, ex)
    tr = t[..., :ROT_DIM].astype(jnp.float32)
    x1, x2 = tr[..., :ROT_HALF], tr[..., ROT_HALF:]
    rot = jnp.concatenate([x1 * c - x2 * s, x2 * c + x1 * s], axis=-1).astype(t.dtype)
    return jnp.concatenate([rot, t[..., ROT_DIM:]], axis=-1)


def dilated_branch(q, k, v, window, dilation):
    B, S, H, D = q.shape
    n_back = window // dilation
    L = S // dilation
    nb = -(-L // BLOCK)
    Lp = nb * BLOCK

    def to_sub(t):
        t = t.astype(jnp.float32).reshape(B, L, dilation, H, D).transpose(0, 2, 3, 1, 4)
        return jnp.pad(t, ((0, 0), (0, 0), (0, 0), (0, Lp - L), (0, 0)))

    def windows(t):
        tp = jnp.pad(t, ((0, 0), (0, 0), (0, 0), (BLOCK, 0), (0, 0)))
        tp = tp.reshape(B, dilation, H, nb + 1, BLOCK, D)
        return jnp.concatenate([tp[:, :, :, :-1], tp[:, :, :, 1:]], axis=4)

    qb = to_sub(q).reshape(B, dilation, H, nb, BLOCK, D)
    kw = windows(to_sub(k))
    vw = windows(to_sub(v))
    s = jnp.einsum('bdhnqc,bdhnkc->bdhnqk', qb, kw)
    qi = jnp.arange(BLOCK)[:, None]
    kj = jnp.arange(2 * BLOCK)[None, :]
    delta = qi + BLOCK - kj
    key_idx = jnp.arange(nb)[:, None, None] * BLOCK - BLOCK + kj[None]
    valid = (delta >= 0) & (delta <= n_back) & (key_idx >= 0)
    s = jnp.where(valid, s, NEG_INF)
    m = jnp.max(s, axis=-1)
    p = jnp.exp(s - m[..., None])
    l = jnp.sum(p, axis=-1)
    acc = jnp.einsum('bdhnqk,bdhnkc->bdhnqc', p, vw)
    acc = acc.reshape(B, dilation, H, Lp, D)[:, :, :, :L].transpose(0, 3, 1, 2, 4).reshape(B, S, H, D)
    m = m.reshape(B, dilation, H, Lp)[..., :L].transpose(0, 3, 1, 2).reshape(B, S, H)
    l = l.reshape(B, dilation, H, Lp)[..., :L].transpose(0, 3, 1, 2).reshape(B, S, H)
    return acc, m, l


def pool_mixer(u, pool_w, pool_scale):
    B, S, _ = u.shape
    uf = u.astype(jnp.float32).reshape(B, S, N_POOL_GROUPS, POOL_GROUP_DIM)
    c = jnp.cumsum(uf, axis=1)
    t = jnp.arange(S)
    outs = []
    for g, w in enumerate(POOL_WINDOWS):
        cg = c[:, :, g]
        shifted = jnp.pad(cg, ((0, 0), (w, 0), (0, 0)))[:, :S]
        count = jnp.minimum(t + 1, w).astype(jnp.float32)[None, :, None]
        outs.append((cg - shifted) / count - uf[:, :, g])
    d = jnp.stack(outs, axis=2).astype(u.dtype)
    y = jnp.einsum('bsgc,gce->bsge', d, pool_w).reshape(B, S, POOL_W)
    return y * pool_scale


def parallel_mixer(xn, cos, sin, w_in, q_norm_g, k_norm_g, pool_w, pool_scale, w_out):
    B, S, _ = xn.shape
    proj = xn @ w_in
    q, k, v, u = jnp.split(proj, [Q_W, Q_W + KV_W, Q_W + 2 * KV_W], axis=-1)
    q = q.reshape(B, S, N_DIL_GROUPS, ATT_HEADS, HEAD_DIM)
    k = k.reshape(B, S, ATT_HEADS, HEAD_DIM)
    v = v.reshape(B, S, ATT_HEADS, HEAD_DIM)
    q = rope_partial(rms_norm(q, q_norm_g), cos, sin) * (HEAD_DIM ** -0.5)
    k = rope_partial(rms_norm(k, k_norm_g), cos, sin)
    accs, ms, ls = [], [], []
    for g, (window, dilation) in enumerate(DILATED_PAIRS):
        a, m, l = dilated_branch(q[:, :, g], k, v, window, dilation)
        accs.append(a); ms.append(m); ls.append(l)
    ms = jnp.stack(ms)
    wts = jnp.exp(ms - jnp.max(ms, axis=0, keepdims=True))
    num = jnp.sum(wts[..., None] * jnp.stack(accs), axis=0)
    den = jnp.sum(wts * jnp.stack(ls), axis=0)
    attn = (num / den[..., None]).astype(xn.dtype).reshape(B, S, KV_W)
    pooled = pool_mixer(u, pool_w, pool_scale)
    return jnp.concatenate([attn, pooled], axis=-1) @ w_out


def memory_cross_attention(hn, mem_n, w_cq, w_ckv, cq_norm_g, ck_norm_g, w_co):
    B, S, _ = hn.shape
    M = mem_n.shape[1]
    q = (hn @ w_cq).reshape(B, S, X_HEADS, HEAD_DIM)
    k, v = jnp.split(mem_n @ w_ckv, 2, axis=-1)
    k = k.reshape(B, M, X_HEADS, HEAD_DIM)
    v = v.reshape(B, M, X_HEADS, HEAD_DIM)
    q = rms_norm(q, cq_norm_g).astype(jnp.float32) * (HEAD_DIM ** -0.5)
    k = rms_norm(k, ck_norm_g).astype(jnp.float32)
    p = jax.nn.softmax(jnp.einsum('bshd,bmhd->bhsm', q, k), axis=-1)
    o = jnp.einsum('bhsm,bmhd->bshd', p, v.astype(jnp.float32)).astype(hn.dtype)
    return o.reshape(B, S, X_W) @ w_co


def swiglu_ffn(hn, w_gate_up, w_down):
    g, u = jnp.split(hn @ w_gate_up, 2, axis=-1)
    return (jax.nn.silu(g) * u) @ w_down


def _fwd_setup_inputs(seed: int = 0) -> dict:
    key = jax.random.key(seed)
    ks = jax.random.split(key, 24)
    f32 = jnp.float32

    def w(k, shape, fan_in):
        return jax.random.normal(k, shape, f32) * (fan_in ** -0.5)

    def gain(k, shape):
        return 1.0 + 0.02 * jax.random.normal(k, shape, f32)

    x = jax.random.normal(ks[0], (BATCH, SEQ, D_MODEL), f32)
    mem = jax.random.normal(ks[1], (BATCH, N_MEM, D_MODEL), f32)
    offset = jax.random.randint(ks[2], (BATCH, 1), 0, 4096, dtype=jnp.int32)
    positions = jnp.arange(SEQ, dtype=jnp.int32)[None, :] + offset
    return {
        "x": x,
        "mem": mem,
        "positions": positions,
        "mix_norm_g": gain(ks[3], (DEPTH, D_MODEL)),
        "w_in": w(ks[4], (DEPTH, D_MODEL, IN_W), D_MODEL),
        "q_norm_g": gain(ks[5], (DEPTH, HEAD_DIM)),
        "k_norm_g": gain(ks[6], (DEPTH, HEAD_DIM)),
        "pool_w": w(ks[7], (DEPTH, N_POOL_GROUPS, POOL_GROUP_DIM, POOL_GROUP_DIM), POOL_GROUP_DIM),
        "pool_scale": gain(ks[8], (DEPTH, POOL_W)),
        "w_out": w(ks[9], (DEPTH, MIX_W, D_MODEL), MIX_W),
        "cross_norm_g": gain(ks[10], (DEPTH, D_MODEL)),
        "mem_norm_g": gain(ks[11], (DEPTH, D_MODEL)),
        "w_cq": w(ks[12], (DEPTH, D_MODEL, X_W), D_MODEL),
        "w_ckv": w(ks[13], (DEPTH, D_MODEL, 2 * X_W), D_MODEL),
        "cq_norm_g": gain(ks[14], (DEPTH, HEAD_DIM)),
        "ck_norm_g": gain(ks[15], (DEPTH, HEAD_DIM)),
        "w_co": w(ks[16], (DEPTH, X_W, D_MODEL), X_W),
        "ffn_norm_g": gain(ks[17], (DEPTH, D_MODEL)),
        "w_gate_up": w(ks[18], (DEPTH, D_MODEL, 2 * D_FF), D_MODEL),
        "w_down": w(ks[19], (DEPTH, D_FF, D_MODEL), D_FF),
    }


def _fwd_reference(x, mem, positions, mix_norm_g, w_in, q_norm_g, k_norm_g, pool_w, pool_scale, w_out,
              cross_norm_g, mem_norm_g, w_cq, w_ckv, cq_norm_g, ck_norm_g, w_co,
              ffn_norm_g, w_gate_up, w_down):
    inv_freq = ROPE_THETA ** (-jnp.arange(0, ROT_DIM, 2, dtype=jnp.float32) / ROT_DIM)
    ang = positions.astype(jnp.float32)[..., None] * inv_freq
    cos, sin = jnp.cos(ang), jnp.sin(ang)
    h = x
    for layer in range(DEPTH):
        h = h + parallel_mixer(rms_norm(h, mix_norm_g[layer]), cos, sin, w_in[layer],
                               q_norm_g[layer], k_norm_g[layer], pool_w[layer],
                               pool_scale[layer], w_out[layer])
        h = h + memory_cross_attention(rms_norm(h, cross_norm_g[layer]),
                                       rms_norm(mem, mem_norm_g[layer]), w_cq[layer],
                                       w_ckv[layer], cq_norm_g[layer], ck_norm_g[layer],
                                       w_co[layer])
        h = h + swiglu_ffn(rms_norm(h, ffn_norm_g[layer]), w_gate_up[layer], w_down[layer])
    return h


import jax as _jax
import jax.numpy as _jnp

TWIN_FORMAT = 'train_step'
FWD_PARAMS = ['x', 'mem', 'positions', 'mix_norm_g', 'w_in', 'q_norm_g', 'k_norm_g', 'pool_w', 'pool_scale', 'w_out', 'cross_norm_g', 'mem_norm_g', 'w_cq', 'w_ckv', 'cq_norm_g', 'ck_norm_g', 'w_co', 'ffn_norm_g', 'w_gate_up', 'w_down']
TWIN_WEIGHTS = ['mix_norm_g', 'w_in', 'q_norm_g', 'k_norm_g', 'pool_w', 'pool_scale', 'w_out', 'cross_norm_g', 'mem_norm_g', 'w_cq', 'w_ckv', 'cq_norm_g', 'ck_norm_g', 'w_co', 'ffn_norm_g', 'w_gate_up', 'w_down']
TWIN_DIFF_INPUT = 'x'
TWIN_INPUTS = ['x', 'mem', 'positions', 'mix_norm_g', 'w_in', 'q_norm_g', 'k_norm_g', 'pool_w', 'pool_scale', 'w_out', 'cross_norm_g', 'mem_norm_g', 'w_cq', 'w_ckv', 'cq_norm_g', 'ck_norm_g', 'w_co', 'ffn_norm_g', 'w_gate_up', 'w_down', 'loss_target', 'm_mix_norm_g', 'm_w_in', 'm_q_norm_g', 'm_k_norm_g', 'm_pool_w', 'm_pool_scale', 'm_w_out', 'm_cross_norm_g', 'm_mem_norm_g', 'm_w_cq', 'm_w_ckv', 'm_cq_norm_g', 'm_ck_norm_g', 'm_w_co', 'm_ffn_norm_g', 'm_w_gate_up', 'm_w_down', 'v_mix_norm_g', 'v_w_in', 'v_q_norm_g', 'v_k_norm_g', 'v_pool_w', 'v_pool_scale', 'v_w_out', 'v_cross_norm_g', 'v_mem_norm_g', 'v_w_cq', 'v_w_ckv', 'v_cq_norm_g', 'v_ck_norm_g', 'v_w_co', 'v_ffn_norm_g', 'v_w_gate_up', 'v_w_down']
TWIN_OUTPUTS = ['loss', 'grad_x', 'grad_mix_norm_g', 'grad_w_in', 'grad_q_norm_g', 'grad_k_norm_g', 'grad_pool_w', 'grad_pool_scale', 'grad_w_out', 'grad_cross_norm_g', 'grad_mem_norm_g', 'grad_w_cq', 'grad_w_ckv', 'grad_cq_norm_g', 'grad_ck_norm_g', 'grad_w_co', 'grad_ffn_norm_g', 'grad_w_gate_up', 'grad_w_down', 'delta_mix_norm_g', 'delta_w_in', 'delta_q_norm_g', 'delta_k_norm_g', 'delta_pool_w', 'delta_pool_scale', 'delta_w_out', 'delta_cross_norm_g', 'delta_mem_norm_g', 'delta_w_cq', 'delta_w_ckv', 'delta_cq_norm_g', 'delta_ck_norm_g', 'delta_w_co', 'delta_ffn_norm_g', 'delta_w_gate_up', 'delta_w_down', 'new_m_mix_norm_g', 'new_m_w_in', 'new_m_q_norm_g', 'new_m_k_norm_g', 'new_m_pool_w', 'new_m_pool_scale', 'new_m_w_out', 'new_m_cross_norm_g', 'new_m_mem_norm_g', 'new_m_w_cq', 'new_m_w_ckv', 'new_m_cq_norm_g', 'new_m_ck_norm_g', 'new_m_w_co', 'new_m_ffn_norm_g', 'new_m_w_gate_up', 'new_m_w_down', 'new_v_mix_norm_g', 'new_v_w_in', 'new_v_q_norm_g', 'new_v_k_norm_g', 'new_v_pool_w', 'new_v_pool_scale', 'new_v_w_out', 'new_v_cross_norm_g', 'new_v_mem_norm_g', 'new_v_w_cq', 'new_v_w_ckv', 'new_v_cq_norm_g', 'new_v_ck_norm_g', 'new_v_w_co', 'new_v_ffn_norm_g', 'new_v_w_gate_up', 'new_v_w_down']
TWIN_LEAF_KINDS = {'loss': 'loss', 'grad_x': 'grad_x', 'grad_mix_norm_g': 'grad_w', 'grad_w_in': 'grad_w', 'grad_q_norm_g': 'grad_w', 'grad_k_norm_g': 'grad_w', 'grad_pool_w': 'grad_w', 'grad_pool_scale': 'grad_w', 'grad_w_out': 'grad_w', 'grad_cross_norm_g': 'grad_w', 'grad_mem_norm_g': 'grad_w', 'grad_w_cq': 'grad_w', 'grad_w_ckv': 'grad_w', 'grad_cq_norm_g': 'grad_w', 'grad_ck_norm_g': 'grad_w', 'grad_w_co': 'grad_w', 'grad_ffn_norm_g': 'grad_w', 'grad_w_gate_up': 'grad_w', 'grad_w_down': 'grad_w', 'delta_mix_norm_g': 'delta_w', 'delta_w_in': 'delta_w', 'delta_q_norm_g': 'delta_w', 'delta_k_norm_g': 'delta_w', 'delta_pool_w': 'delta_w', 'delta_pool_scale': 'delta_w', 'delta_w_out': 'delta_w', 'delta_cross_norm_g': 'delta_w', 'delta_mem_norm_g': 'delta_w', 'delta_w_cq': 'delta_w', 'delta_w_ckv': 'delta_w', 'delta_cq_norm_g': 'delta_w', 'delta_ck_norm_g': 'delta_w', 'delta_w_co': 'delta_w', 'delta_ffn_norm_g': 'delta_w', 'delta_w_gate_up': 'delta_w', 'delta_w_down': 'delta_w', 'new_m_mix_norm_g': 'new_m', 'new_m_w_in': 'new_m', 'new_m_q_norm_g': 'new_m', 'new_m_k_norm_g': 'new_m', 'new_m_pool_w': 'new_m', 'new_m_pool_scale': 'new_m', 'new_m_w_out': 'new_m', 'new_m_cross_norm_g': 'new_m', 'new_m_mem_norm_g': 'new_m', 'new_m_w_cq': 'new_m', 'new_m_w_ckv': 'new_m', 'new_m_cq_norm_g': 'new_m', 'new_m_ck_norm_g': 'new_m', 'new_m_w_co': 'new_m', 'new_m_ffn_norm_g': 'new_m', 'new_m_w_gate_up': 'new_m', 'new_m_w_down': 'new_m', 'new_v_mix_norm_g': 'new_v', 'new_v_w_in': 'new_v', 'new_v_q_norm_g': 'new_v', 'new_v_k_norm_g': 'new_v', 'new_v_pool_w': 'new_v', 'new_v_pool_scale': 'new_v', 'new_v_w_out': 'new_v', 'new_v_cross_norm_g': 'new_v', 'new_v_mem_norm_g': 'new_v', 'new_v_w_cq': 'new_v', 'new_v_w_ckv': 'new_v', 'new_v_cq_norm_g': 'new_v', 'new_v_ck_norm_g': 'new_v', 'new_v_w_co': 'new_v', 'new_v_ffn_norm_g': 'new_v', 'new_v_w_gate_up': 'new_v', 'new_v_w_down': 'new_v'}


def _forward(args):
    return _fwd_reference(*[args[k] for k in FWD_PARAMS])


def _output_shape():
    def fwd():
        inp = _fwd_setup_inputs(0)
        return _fwd_reference(*[inp[k] for k in FWD_PARAMS])
    out = _jax.eval_shape(fwd)
    return out.shape, out.dtype

N_MICROBATCH = 1
ADAM_LR = 0.001
ADAM_B1 = 0.9
ADAM_B2 = 0.999
ADAM_EPS = 1e-08
ADAM_WD = 0.01
ADAM_STEP = 10
PER_EXAMPLE_BATCH_AXIS = {'x': 0, 'mem': 0, 'positions': 0, 'loss_target': 0}
SHARED_INPUTS = []
_WEIGHT_DTYPES = {'mix_norm_g': _jnp.float32, 'w_in': _jnp.float32, 'q_norm_g': _jnp.float32, 'k_norm_g': _jnp.float32, 'pool_w': _jnp.float32, 'pool_scale': _jnp.float32, 'w_out': _jnp.float32, 'cross_norm_g': _jnp.float32, 'mem_norm_g': _jnp.float32, 'w_cq': _jnp.float32, 'w_ckv': _jnp.float32, 'cq_norm_g': _jnp.float32, 'ck_norm_g': _jnp.float32, 'w_co': _jnp.float32, 'ffn_norm_g': _jnp.float32, 'w_gate_up': _jnp.float32, 'w_down': _jnp.float32}
MOMENT_SCALE = {'mix_norm_g': 2.397026e+01, 'w_in': 9.029435e-01, 'q_norm_g': 1.175451e+00, 'k_norm_g': 1.175538e+00, 'pool_w': 4.802749e+00, 'pool_scale': 5.043227e+01, 'w_out': 1.724357e+00, 'cross_norm_g': 9.006283e-02, 'mem_norm_g': 6.995091e-01, 'w_cq': 1.291245e-01, 'w_ckv': 2.125386e-01, 'cq_norm_g': 5.018193e+00, 'ck_norm_g': 5.038251e+00, 'w_co': 1.884509e-01, 'ffn_norm_g': 4.954080e+01, 'w_gate_up': 3.377617e-01, 'w_down': 5.539344e-01}


def _to_microbatches(a, axis):
    t = _jnp.moveaxis(a, axis, 0)
    t = t.reshape((N_MICROBATCH, t.shape[0] // N_MICROBATCH) + t.shape[1:])
    return _jnp.moveaxis(t, 1, axis + 1)


def setup_inputs(seed: int = 0) -> dict:
    inp = _fwd_setup_inputs(seed)
    key = _jax.random.fold_in(_jax.random.key(seed), 7919)
    shape, _ = _output_shape()
    out = dict(inp)
    out["loss_target"] = _jax.random.normal(_jax.random.fold_in(key, 0), shape, _jnp.float32)
    for i, name in enumerate(TWIN_WEIGHTS):
        w = inp[name].astype(_jnp.float32)
        if MOMENT_SCALE is None:
            s = _jnp.sqrt(_jnp.mean(_jnp.square(w)) + 1e-30)
        else:
            s = MOMENT_SCALE[name]
        km, kv = _jax.random.split(_jax.random.fold_in(key, i + 1))
        out[name] = w
        out["m_" + name] = s * _jax.random.normal(km, w.shape, _jnp.float32)
        out["v_" + name] = (s * s) * _jax.random.uniform(kv, w.shape, _jnp.float32, 0.5, 1.5)
    if N_MICROBATCH > 1:
        for name, axis in PER_EXAMPLE_BATCH_AXIS.items():
            out[name] = _to_microbatches(out[name], axis)
    return {'x': out['x'], 'mem': out['mem'], 'positions': out['positions'], 'mix_norm_g': out['mix_norm_g'], 'w_in': out['w_in'], 'q_norm_g': out['q_norm_g'], 'k_norm_g': out['k_norm_g'], 'pool_w': out['pool_w'], 'pool_scale': out['pool_scale'], 'w_out': out['w_out'], 'cross_norm_g': out['cross_norm_g'], 'mem_norm_g': out['mem_norm_g'], 'w_cq': out['w_cq'], 'w_ckv': out['w_ckv'], 'cq_norm_g': out['cq_norm_g'], 'ck_norm_g': out['ck_norm_g'], 'w_co': out['w_co'], 'ffn_norm_g': out['ffn_norm_g'], 'w_gate_up': out['w_gate_up'], 'w_down': out['w_down'], 'loss_target': out['loss_target'], 'm_mix_norm_g': out['m_mix_norm_g'], 'm_w_in': out['m_w_in'], 'm_q_norm_g': out['m_q_norm_g'], 'm_k_norm_g': out['m_k_norm_g'], 'm_pool_w': out['m_pool_w'], 'm_pool_scale': out['m_pool_scale'], 'm_w_out': out['m_w_out'], 'm_cross_norm_g': out['m_cross_norm_g'], 'm_mem_norm_g': out['m_mem_norm_g'], 'm_w_cq': out['m_w_cq'], 'm_w_ckv': out['m_w_ckv'], 'm_cq_norm_g': out['m_cq_norm_g'], 'm_ck_norm_g': out['m_ck_norm_g'], 'm_w_co': out['m_w_co'], 'm_ffn_norm_g': out['m_ffn_norm_g'], 'm_w_gate_up': out['m_w_gate_up'], 'm_w_down': out['m_w_down'], 'v_mix_norm_g': out['v_mix_norm_g'], 'v_w_in': out['v_w_in'], 'v_q_norm_g': out['v_q_norm_g'], 'v_k_norm_g': out['v_k_norm_g'], 'v_pool_w': out['v_pool_w'], 'v_pool_scale': out['v_pool_scale'], 'v_w_out': out['v_w_out'], 'v_cross_norm_g': out['v_cross_norm_g'], 'v_mem_norm_g': out['v_mem_norm_g'], 'v_w_cq': out['v_w_cq'], 'v_w_ckv': out['v_w_ckv'], 'v_cq_norm_g': out['v_cq_norm_g'], 'v_ck_norm_g': out['v_ck_norm_g'], 'v_w_co': out['v_w_co'], 'v_ffn_norm_g': out['v_ffn_norm_g'], 'v_w_gate_up': out['v_w_gate_up'], 'v_w_down': out['v_w_down']}


def _loss(weights, diff, rest, loss_target):
    with _jax.named_scope("forward"):
        args = {**rest, TWIN_DIFF_INPUT: diff, **{k: w.astype(_WEIGHT_DTYPES[k]) for k, w in weights.items()}}
        y = _forward(args)
    with _jax.named_scope("loss_head"):
        err = _jnp.square(y.astype(_jnp.float32) - loss_target)
        return 0.5 * _jnp.sum(_jnp.mean(err, axis=-1)) if err.ndim else 0.5 * err


def _adamw(w, g, m, v):
    m = ADAM_B1 * m + (1.0 - ADAM_B1) * g
    v = ADAM_B2 * v + (1.0 - ADAM_B2) * _jnp.square(g)
    m_hat = m / (1.0 - ADAM_B1 ** ADAM_STEP)
    v_hat = v / (1.0 - ADAM_B2 ** ADAM_STEP)
    delta = -ADAM_LR * (m_hat / (_jnp.sqrt(v_hat) + ADAM_EPS) + ADAM_WD * w)
    return delta, m, v


def reference(x, mem, positions, mix_norm_g, w_in, q_norm_g, k_norm_g, pool_w, pool_scale, w_out, cross_norm_g, mem_norm_g, w_cq, w_ckv, cq_norm_g, ck_norm_g, w_co, ffn_norm_g, w_gate_up, w_down, loss_target, m_mix_norm_g, m_w_in, m_q_norm_g, m_k_norm_g, m_pool_w, m_pool_scale, m_w_out, m_cross_norm_g, m_mem_norm_g, m_w_cq, m_w_ckv, m_cq_norm_g, m_ck_norm_g, m_w_co, m_ffn_norm_g, m_w_gate_up, m_w_down, v_mix_norm_g, v_w_in, v_q_norm_g, v_k_norm_g, v_pool_w, v_pool_scale, v_w_out, v_cross_norm_g, v_mem_norm_g, v_w_cq, v_w_ckv, v_cq_norm_g, v_ck_norm_g, v_w_co, v_ffn_norm_g, v_w_gate_up, v_w_down):
    given = dict(x=x, mem=mem, positions=positions, mix_norm_g=mix_norm_g, w_in=w_in, q_norm_g=q_norm_g, k_norm_g=k_norm_g, pool_w=pool_w, pool_scale=pool_scale, w_out=w_out, cross_norm_g=cross_norm_g, mem_norm_g=mem_norm_g, w_cq=w_cq, w_ckv=w_ckv, cq_norm_g=cq_norm_g, ck_norm_g=ck_norm_g, w_co=w_co, ffn_norm_g=ffn_norm_g, w_gate_up=w_gate_up, w_down=w_down, loss_target=loss_target, m_mix_norm_g=m_mix_norm_g, m_w_in=m_w_in, m_q_norm_g=m_q_norm_g, m_k_norm_g=m_k_norm_g, m_pool_w=m_pool_w, m_pool_scale=m_pool_scale, m_w_out=m_w_out, m_cross_norm_g=m_cross_norm_g, m_mem_norm_g=m_mem_norm_g, m_w_cq=m_w_cq, m_w_ckv=m_w_ckv, m_cq_norm_g=m_cq_norm_g, m_ck_norm_g=m_ck_norm_g, m_w_co=m_w_co, m_ffn_norm_g=m_ffn_norm_g, m_w_gate_up=m_w_gate_up, m_w_down=m_w_down, v_mix_norm_g=v_mix_norm_g, v_w_in=v_w_in, v_q_norm_g=v_q_norm_g, v_k_norm_g=v_k_norm_g, v_pool_w=v_pool_w, v_pool_scale=v_pool_scale, v_w_out=v_w_out, v_cross_norm_g=v_cross_norm_g, v_mem_norm_g=v_mem_norm_g, v_w_cq=v_w_cq, v_w_ckv=v_w_ckv, v_cq_norm_g=v_cq_norm_g, v_ck_norm_g=v_ck_norm_g, v_w_co=v_w_co, v_ffn_norm_g=v_ffn_norm_g, v_w_gate_up=v_w_gate_up, v_w_down=v_w_down)
    weights = {n: given[n] for n in TWIN_WEIGHTS}
    shared = {n: given[n] for n in SHARED_INPUTS}
    per_example = {n: given[n] for n in ['x', 'mem', 'positions']}
    grad_fn = _jax.value_and_grad(_loss, argnums=(0, 1))

    def one_microbatch(ex, loss_target):
        ex = dict(ex)
        diff = ex.pop(TWIN_DIFF_INPUT)
        return grad_fn(weights, diff, {**shared, **ex}, loss_target)

    if N_MICROBATCH == 1:
        loss, (grad_w, grad_x) = one_microbatch(per_example, given["loss_target"])
    else:
        def body(carry, xs):
            loss_sum, grad_sum = carry
            l_k, (gw_k, gx_k) = one_microbatch(xs[0], xs[1])
            with _jax.named_scope("update"):
                return (loss_sum + l_k, _jax.tree.map(_jnp.add, grad_sum, gw_k)), gx_k

        init = (_jnp.zeros((), _jnp.float32), _jax.tree.map(_jnp.zeros_like, weights))
        (loss, grad_w), grad_x = _jax.lax.scan(body, init, (per_example, given["loss_target"]))
    with _jax.named_scope("update"):
        delta_w, new_m, new_v = {}, {}, {}
        for n in TWIN_WEIGHTS:
            delta_w[n], new_m[n], new_v[n] = _adamw(weights[n], grad_w[n], given["m_" + n], given["v_" + n])
    return (loss, grad_x, *[grad_w[n] for n in TWIN_WEIGHTS], *[delta_w[n] for n in TWIN_WEIGHTS],
            *[new_m[n] for n in TWIN_WEIGHTS], *[new_v[n] for n in TWIN_WEIGHTS])
```

```python
import functools

import jax
import jax.numpy as jnp
from jax import lax
from jax.experimental import pallas as pl
from jax.experimental.pallas import tpu as pltpu

F32 = jnp.float32
BF16 = jnp.bfloat16
MESH = pl.DeviceIdType.MESH
ANY = pl.BlockSpec(memory_space=pl.ANY)

D_MODEL = 1024
HEAD = 128
N_HEADS = 4
DILATIONS = (1, 4, 16)
BLOCK = 128
Q_W = 1536
KV_W = 512
POOL_W = 512
POOL_WINDOWS = (2, 4, 8, 16)
POOL_HALO = 16
IN_W = 3072
ROT_DIM = 32
ROT_HALF = 16
ROPE_THETA = 500000.0
X_W = 512
D_FF = 2816
EPS = 1e-6
NEG_INF = -1e30
SCALE = HEAD ** -0.5
N_CHIPS = 4

ADAM_LR = 0.001
ADAM_B1 = 0.9
ADAM_B2 = 0.999
ADAM_EPS = 1e-08
ADAM_WD = 0.01
ADAM_STEP = 10

VMEM_BYTES_V7X = 64 * 2 ** 20
VMEM_LIMIT_MAX = 56 * 2 ** 20
VMEM_LIMIT_MIN = 24 * 2 ** 20


def _nbytes(shape, dtype):
    n = 1
    for s in shape:
        n *= s
    return n * jnp.dtype(dtype).itemsize


def _cparams(n_axes, block_bytes, scratch_bytes=0):
    est = 2 * (2 * block_bytes + scratch_bytes)
    lim = int(min(VMEM_LIMIT_MAX, max(VMEM_LIMIT_MIN, est)))
    return pltpu.CompilerParams(dimension_semantics=("arbitrary",) * n_axes, vmem_limit_bytes=lim)


def _bf(v):
    return v.astype(BF16)


def _dot(a, b):
    return jnp.dot(a, b, preferred_element_type=F32)


def _dot_nt(a, b):
    return lax.dot_general(a, b, (((1,), (1,)), ((), ())), preferred_element_type=F32)


def _dot_tn(a, b):
    return lax.dot_general(a, b, (((0,), (0,)), ((), ())), preferred_element_type=F32)


def _rstd(v):
    return lax.rsqrt(jnp.mean(v * v, axis=-1, keepdims=True) + EPS)


def _norm_bwd(dy, xv, g):
    r = _rstd(xv)
    xh = xv * r
    dxh = dy * g
    dx = r * (dxh - xh * jnp.mean(dxh * xh, axis=-1, keepdims=True))
    return dx, xh


def _mm_nn(name, a, w3, *, tm, tn, norm_g=None, residual=None, out_dtype=F32):
    M, K = a.shape
    P, Kw, C = w3.shape
    assert Kw == K and C % tn == 0 and M % tm == 0
    N = P * C
    cpt = C // tn
    has_norm = norm_g is not None
    has_res = residual is not None
    staged = has_norm or a.dtype != BF16

    def body(*refs):
        refs = list(refs)
        a_ref = refs.pop(0)
        w_ref = refs.pop(0)
        g_ref = refs.pop(0) if has_norm else None
        r_ref = refs.pop(0) if has_res else None
        o_ref = refs.pop(0)
        xn_ref = refs.pop(0) if has_norm else None
        ab_ref = refs.pop(0) if staged else None
        if staged:
            @pl.when(pl.program_id(1) == 0)
            def _():
                av = a_ref[...].astype(F32)
                if has_norm:
                    av = av * _rstd(av) * g_ref[...]
                ab_ref[...] = _bf(av)
                if has_norm:
                    xn_ref[...] = _bf(av)
            ab = ab_ref[...]
        else:
            ab = a_ref[...]
        acc = _dot(ab, w_ref[...])
        if has_res:
            acc = acc + r_ref[...]
        o_ref[...] = acc.astype(o_ref.dtype)

    in_specs = [pl.BlockSpec((tm, K), lambda i, j: (i, 0)),
                pl.BlockSpec((None, K, tn), lambda i, j: (j // cpt, 0, j % cpt))]
    args = [a, w3]
    if has_norm:
        in_specs.append(pl.BlockSpec((1, K), lambda i, j: (0, 0)))
        args.append(norm_g)
    if has_res:
        in_specs.append(pl.BlockSpec((tm, tn), lambda i, j: (i, j)))
        args.append(residual)
    out_shape = [jax.ShapeDtypeStruct((M, N), out_dtype)]
    out_specs = [pl.BlockSpec((tm, tn), lambda i, j: (i, j))]
    if has_norm:
        out_shape.append(jax.ShapeDtypeStruct((M, K), BF16))
        out_specs.append(pl.BlockSpec((tm, K), lambda i, j: (i, 0)))
    scratch = [pltpu.VMEM((tm, K), BF16)] if staged else []
    blk = (_nbytes((tm, K), a.dtype) + _nbytes((K, tn), BF16) + 2 * _nbytes((tm, tn), F32)
           + _nbytes((tm, K), BF16))
    res = pl.pallas_call(
        body, name=name, grid=(M // tm, N // tn), in_specs=in_specs, out_specs=out_specs,
        out_shape=out_shape, scratch_shapes=scratch,
        compiler_params=_cparams(2, blk, _nbytes((tm, K), BF16)))(*args)
    return res if has_norm else res[0]


def _ffn_up(name, h, wgu3, g, *, tm):
    M, K = h.shape
    P, _, C = wgu3.shape
    half = P // 2

    def body(h_ref, wg_ref, wu_ref, g_ref, act_ref, gu_ref, xn_ref, ab_ref):
        @pl.when(pl.program_id(1) == 0)
        def _():
            hv = h_ref[...]
            xn = _bf(hv * _rstd(hv) * g_ref[...])
            ab_ref[...] = xn
            xn_ref[...] = xn
        ab = ab_ref[...]
        gate = _dot(ab, wg_ref[...])
        up = _dot(ab, wu_ref[...])
        act_ref[...] = _bf(gate * (1.0 / (1.0 + jnp.exp(-gate))) * up)
        gu_ref[0] = _bf(gate)
        gu_ref[1] = _bf(up)

    blk = (_nbytes((tm, K), F32) + 2 * _nbytes((K, C), BF16) + 3 * _nbytes((tm, C), BF16)
           + 2 * _nbytes((tm, C), F32) + _nbytes((tm, K), BF16))
    return pl.pallas_call(
        body, name=name, grid=(M // tm, half),
        in_specs=[pl.BlockSpec((tm, K), lambda i, j: (i, 0)),
                  pl.BlockSpec((None, K, C), lambda i, j: (j, 0, 0)),
                  pl.BlockSpec((None, K, C), lambda i, j: (j + half, 0, 0)),
                  pl.BlockSpec((1, K), lambda i, j: (0, 0))],
        out_specs=[pl.BlockSpec((tm, C), lambda i, j: (i, j)),
                   pl.BlockSpec((2, tm, C), lambda i, j: (0, i, j)),
                   pl.BlockSpec((tm, K), lambda i, j: (i, 0))],
        out_shape=[jax.ShapeDtypeStruct((M, half * C), BF16),
                   jax.ShapeDtypeStruct((2, M, half * C), BF16),
                   jax.ShapeDtypeStruct((M, K), BF16)],
        scratch_shapes=[pltpu.VMEM((tm, K), BF16)],
        compiler_params=_cparams(2, blk, _nbytes((tm, K), BF16)))(h, wgu3, wgu3, g)


def _ffn_down_loss(name, act, wd3, h, target, *, tm):
    M, K = act.shape
    _, _, N = wd3.shape

    def body(a_ref, w_ref, h_ref, t_ref, dy_ref, ls_ref):
        err = _dot(a_ref[...], w_ref[...]) + h_ref[...] - t_ref[...]
        dy_ref[...] = err * (1.0 / N)

        @pl.when(pl.program_id(0) == 0)
        def _():
            ls_ref[...] = jnp.zeros_like(ls_ref)
        ls_ref[...] += jnp.sum(err * err, axis=0, keepdims=True)

    blk = _nbytes((tm, K), BF16) + _nbytes((K, N), BF16) + 4 * _nbytes((tm, N), F32)
    return pl.pallas_call(
        body, name=name, grid=(M // tm,),
        in_specs=[pl.BlockSpec((tm, K), lambda i: (i, 0)),
                  pl.BlockSpec((None, K, N), lambda i: (0, 0, 0)),
                  pl.BlockSpec((tm, N), lambda i: (i, 0)),
                  pl.BlockSpec((tm, N), lambda i: (i, 0))],
        out_specs=[pl.BlockSpec((tm, N), lambda i: (i, 0)),
                   pl.BlockSpec((1, N), lambda i: (0, 0))],
        out_shape=[jax.ShapeDtypeStruct((M, N), F32), jax.ShapeDtypeStruct((1, N), F32)],
        compiler_params=_cparams(1, blk))(act, wd3, h, target)


def _mm_nt(name, a, a_spec, a_blk, w3, w_spec, w_blk, nsteps, M, Ko, *, tm, out_dtype):
    def body(a_ref, w_ref, o_ref, acc_ref):
        p = pl.program_id(1)

        @pl.when(p == 0)
        def _():
            acc_ref[...] = jnp.zeros_like(acc_ref)
        acc_ref[...] += _dot_nt(_bf(a_ref[...]), w_ref[...])

        @pl.when(p == nsteps - 1)
        def _():
            o_ref[...] = acc_ref[...].astype(o_ref.dtype)

    blk = _nbytes(a_blk, a.dtype) + _nbytes(w_blk, BF16) + _nbytes((tm, Ko), F32)
    return pl.pallas_call(
        body, name=name, grid=(M // tm, nsteps), in_specs=[a_spec, w_spec],
        out_specs=pl.BlockSpec((tm, Ko), lambda i, p: (i, 0)),
        out_shape=jax.ShapeDtypeStruct((M, Ko), out_dtype),
        scratch_shapes=[pltpu.VMEM((tm, Ko), F32)],
        compiler_params=_cparams(2, blk, _nbytes((tm, Ko), F32)))(a, w3)


def _mm_nt_normbwd(name, a, a_spec, a_blk, w3, w_spec, w_blk, nsteps, h, g, dres, *, tm):
    M, D = h.shape
    has_res = dres is not None

    def body(*refs):
        refs = list(refs)
        a_ref, w_ref, h_ref, g_ref = refs[:4]
        r_ref = refs[4] if has_res else None
        dx_ref, dg_ref, acc_ref = refs[-3:]
        i = pl.program_id(0)
        p = pl.program_id(1)

        @pl.when(p == 0)
        def _():
            acc_ref[...] = jnp.zeros_like(acc_ref)
        acc_ref[...] += _dot_nt(_bf(a_ref[...]), w_ref[...])

        @pl.when(p == nsteps - 1)
        def _():
            dhn = acc_ref[...]
            dx, xh = _norm_bwd(dhn, h_ref[...], g_ref[...])
            if has_res:
                dx = dx + r_ref[...]
            dx_ref[...] = dx

            @pl.when(i == 0)
            def _():
                dg_ref[...] = jnp.zeros_like(dg_ref)
            dg_ref[...] += jnp.sum(dhn * xh, axis=0, keepdims=True)

    in_specs = [a_spec, w_spec, pl.BlockSpec((tm, D), lambda i, p: (i, 0)),
                pl.BlockSpec((1, D), lambda i, p: (0, 0))]
    args = [a, w3, h, g]
    if has_res:
        in_specs.append(pl.BlockSpec((tm, D), lambda i, p: (i, 0)))
        args.append(dres)
    blk = _nbytes(a_blk, a.dtype) + _nbytes(w_blk, BF16) + 5 * _nbytes((tm, D), F32)
    return pl.pallas_call(
        body, name=name, grid=(M // tm, nsteps), in_specs=in_specs,
        out_specs=[pl.BlockSpec((tm, D), lambda i, p: (i, 0)), pl.BlockSpec((1, D), lambda i, p: (0, 0))],
        out_shape=[jax.ShapeDtypeStruct((M, D), F32), jax.ShapeDtypeStruct((1, D), F32)],
        scratch_shapes=[pltpu.VMEM((tm, D), F32)],
        compiler_params=_cparams(2, blk, _nbytes((tm, D), F32)))(*args)


def _ffn_down_bwd(name, dy, wd3, gu, *, tm, tn):
    M, D = dy.shape
    _, F, _ = wd3.shape

    def body(dy_ref, w_ref, gu_ref, o_ref, ab_ref):
        @pl.when(pl.program_id(1) == 0)
        def _():
            ab_ref[...] = _bf(dy_ref[...])
        da = _dot_nt(ab_ref[...], w_ref[...])
        gate = gu_ref[0].astype(F32)
        up = gu_ref[1].astype(F32)
        sig = 1.0 / (1.0 + jnp.exp(-gate))
        silu = gate * sig
        o_ref[0] = _bf(da * up * (sig + silu * (1.0 - sig)))
        o_ref[1] = _bf(da * silu)

    blk = (_nbytes((tm, D), F32) + _nbytes((tn, D), BF16) + 4 * _nbytes((tm, tn), BF16)
           + 4 * _nbytes((tm, tn), F32))
    return pl.pallas_call(
        body, name=name, grid=(M // tm, F // tn),
        in_specs=[pl.BlockSpec((tm, D), lambda i, j: (i, 0)),
                  pl.BlockSpec((None, tn, D), lambda i, j: (0, j, 0)),
                  pl.BlockSpec((2, tm, tn), lambda i, j: (0, i, j))],
        out_specs=pl.BlockSpec((2, tm, tn), lambda i, j: (0, i, j)),
        out_shape=jax.ShapeDtypeStruct((2, M, F), BF16),
        scratch_shapes=[pltpu.VMEM((tm, D), BF16)],
        compiler_params=_cparams(2, blk, _nbytes((tm, D), BF16)))(dy, wd3, gu)


def _mm_tn(name, a, a_spec, a_blk, b, b_spec, b_blk, out3, o_spec, o_blk, grid):
    def body(a_ref, b_ref, o_ref):
        @pl.when(pl.program_id(2) == 0)
        def _():
            o_ref[...] = jnp.zeros_like(o_ref)
        o_ref[...] += _dot_tn(_bf(a_ref[...]), _bf(b_ref[...]))

    blk = _nbytes(a_blk, a.dtype) + _nbytes(b_blk, b.dtype) + 2 * _nbytes(o_blk, F32)
    return pl.pallas_call(
        body, name=name, grid=grid, in_specs=[a_spec, b_spec], out_specs=o_spec,
        out_shape=jax.ShapeDtypeStruct(out3, F32),
        compiler_params=_cparams(3, blk))(a, b)


def _swap_halves(v, lane, masked):
    up = pltpu.roll(v, HEAD - ROT_HALF, 1)
    down = pltpu.roll(v, ROT_HALF, 1)
    rest = jnp.where(lane < ROT_DIM, down, 0.0) if masked else down
    return jnp.where(lane < ROT_HALF, up, rest)


def _qk_prep(name, proj, ctab, stab, qg, kg, *, tm):
    S = proj.shape[0]
    width = Q_W + KV_W

    def body(p_ref, c_ref, s_ref, qg_ref, kg_ref, q_ref, k_ref):
        lane = lax.broadcasted_iota(jnp.int32, (tm, HEAD), 1)
        cv = c_ref[...]
        sv = s_ref[...]

        def prep(t, g, scale):
            n = t * _rstd(t) * g
            return (n * cv + _swap_halves(n, lane, False) * sv) * scale

        for j in range(Q_W // HEAD):
            q_ref[:, j * HEAD:(j + 1) * HEAD] = prep(p_ref[:, j * HEAD:(j + 1) * HEAD], qg_ref[...], SCALE)
        for j in range(KV_W // HEAD):
            k_ref[:, j * HEAD:(j + 1) * HEAD] = prep(p_ref[:, Q_W + j * HEAD:Q_W + (j + 1) * HEAD], kg_ref[...], 1.0)

    blk = 2 * _nbytes((tm, width), F32) + 2 * _nbytes((tm, HEAD), F32)
    return pl.pallas_call(
        body, name=name, grid=(S // tm,),
        in_specs=[pl.BlockSpec((tm, width), lambda i: (i, 0)),
                  pl.BlockSpec((tm, HEAD), lambda i: (i, 0)),
                  pl.BlockSpec((tm, HEAD), lambda i: (i, 0)),
                  pl.BlockSpec((1, HEAD), lambda i: (0, 0)),
                  pl.BlockSpec((1, HEAD), lambda i: (0, 0))],
        out_specs=[pl.BlockSpec((tm, Q_W), lambda i: (i, 0)), pl.BlockSpec((tm, KV_W), lambda i: (i, 0))],
        out_shape=[jax.ShapeDtypeStruct((S, Q_W), F32), jax.ShapeDtypeStruct((S, KV_W), F32)],
        compiler_params=_cparams(1, blk))(proj, ctab, stab, qg, kg)


def _rows(ref, r, dil):
    if dil == 1:
        return ref[...]
    return ref[pl.ds(r, BLOCK, stride=dil), :]


def _set_rows(ref, r, dil, val):
    if dil == 1:
        ref[...] = val
    else:
        ref[pl.ds(r, BLOCK, stride=dil), :] = val


def _band_mask(first):
    qi = lax.broadcasted_iota(jnp.int32, (BLOCK, 2 * BLOCK), 0)
    kj = lax.broadcasted_iota(jnp.int32, (BLOCK, 2 * BLOCK), 1)
    return (kj >= qi) & (kj <= qi + BLOCK) & ((kj >= BLOCK) | jnp.logical_not(first))


def _attn_fwd(name, qn, kn, proj, grp):
    S = qn.shape[0]
    dil = DILATIONS[grp]
    bt = BLOCK * dil
    nb = S // bt
    vcol = (Q_W + KV_W) // HEAD

    def body(q_ref, kp_ref, kc_ref, vp_ref, vc_ref, o_ref, l_ref):
        valid = _band_mask(pl.program_id(1) == 0)
        for r in range(dil):
            q = _bf(_rows(q_ref, r, dil))
            k2 = _bf(jnp.concatenate([_rows(kp_ref, r, dil), _rows(kc_ref, r, dil)], axis=0))
            v2 = _bf(jnp.concatenate([_rows(vp_ref, r, dil), _rows(vc_ref, r, dil)], axis=0))
            s = jnp.where(valid, _dot_nt(q, k2), NEG_INF)
            m = jnp.max(s, axis=-1, keepdims=True)
            p = jnp.exp(s - m)
            l = jnp.sum(p, axis=-1, keepdims=True)
            acc = _dot(_bf(p), v2)
            _set_rows(o_ref, r, dil, acc / l)
            _set_rows(l_ref, r, dil, jnp.broadcast_to(m + jnp.log(l), (BLOCK, HEAD)))

    prev = lambda n: jnp.maximum(n - 1, 0)
    blk = 7 * _nbytes((bt, HEAD), F32)
    return pl.pallas_call(
        body, name=name, grid=(N_HEADS, nb),
        in_specs=[pl.BlockSpec((bt, HEAD), lambda h, n: (n, grp * N_HEADS + h)),
                  pl.BlockSpec((bt, HEAD), lambda h, n: (prev(n), h)),
                  pl.BlockSpec((bt, HEAD), lambda h, n: (n, h)),
                  pl.BlockSpec((bt, HEAD), lambda h, n: (prev(n), vcol + h)),
                  pl.BlockSpec((bt, HEAD), lambda h, n: (n, vcol + h))],
        out_specs=[pl.BlockSpec((bt, HEAD), lambda h, n: (n, h)), pl.BlockSpec((bt, HEAD), lambda h, n: (n, h))],
        out_shape=[jax.ShapeDtypeStruct((S, KV_W), F32), jax.ShapeDtypeStruct((S, KV_W), F32)],
        compiler_params=_cparams(2, blk))(qn, kn, kn, proj, proj)


def _attn_bwd(name, qn, kn, proj, dattn, lse, delta, grp):
    S = qn.shape[0]
    dil = DILATIONS[grp]
    bt = BLOCK * dil
    nb = S // bt
    vcol = (Q_W + KV_W) // HEAD

    def body(q_ref, kp_ref, kc_ref, vp_ref, vc_ref, do_ref, l_ref, d_ref, dq_ref, dk_ref, dv_ref, ck_ref, cv_ref):
        n = pl.program_id(1)
        valid = _band_mask(n == 0)

        @pl.when(n < nb)
        def _():
            for r in range(dil):
                q = _bf(_rows(q_ref, r, dil))
                k2 = _bf(jnp.concatenate([_rows(kp_ref, r, dil), _rows(kc_ref, r, dil)], axis=0))
                v2 = _bf(jnp.concatenate([_rows(vp_ref, r, dil), _rows(vc_ref, r, dil)], axis=0))
                do = _bf(_rows(do_ref, r, dil))
                lse_r = _rows(l_ref, r, dil)[:, :1]
                del_r = _rows(d_ref, r, dil)[:, :1]
                s = jnp.where(valid, _dot_nt(q, k2), NEG_INF)
                p = jnp.exp(s - lse_r)
                ds = _bf(p * (_dot_nt(do, v2) - del_r))
                _set_rows(dq_ref, r, dil, _dot(ds, k2))
                dk2 = _dot_tn(ds, q)
                dv2 = _dot_tn(_bf(p), do)

                @pl.when(n > 0)
                def _():
                    _set_rows(dk_ref, r, dil, ck_ref[r] + dk2[:BLOCK])
                    _set_rows(dv_ref, r, dil, cv_ref[r] + dv2[:BLOCK])
                ck_ref[r] = dk2[BLOCK:]
                cv_ref[r] = dv2[BLOCK:]

        @pl.when(n == nb)
        def _():
            for r in range(dil):
                _set_rows(dk_ref, r, dil, ck_ref[r])
                _set_rows(dv_ref, r, dil, cv_ref[r])

    cur = lambda n: jnp.minimum(n, nb - 1)
    prev = lambda n: jnp.clip(n - 1, 0, nb - 1)
    blk = 11 * _nbytes((bt, HEAD), F32)
    return pl.pallas_call(
        body, name=name, grid=(N_HEADS, nb + 1),
        in_specs=[pl.BlockSpec((bt, HEAD), lambda h, n: (cur(n), grp * N_HEADS + h)),
                  pl.BlockSpec((bt, HEAD), lambda h, n: (prev(n), h)),
                  pl.BlockSpec((bt, HEAD), lambda h, n: (cur(n), h)),
                  pl.BlockSpec((bt, HEAD), lambda h, n: (prev(n), vcol + h)),
                  pl.BlockSpec((bt, HEAD), lambda h, n: (cur(n), vcol + h)),
                  pl.BlockSpec((bt, HEAD), lambda h, n: (cur(n), h)),
                  pl.BlockSpec((bt, HEAD), lambda h, n: (cur(n), h)),
                  pl.BlockSpec((bt, HEAD), lambda h, n: (cur(n), h))],
        out_specs=[pl.BlockSpec((bt, HEAD), lambda h, n: (cur(n), h)),
                   pl.BlockSpec((bt, HEAD), lambda h, n: (prev(n), h)),
                   pl.BlockSpec((bt, HEAD), lambda h, n: (prev(n), h))],
        out_shape=[jax.ShapeDtypeStruct((S, KV_W), F32)] * 3,
        scratch_shapes=[pltpu.VMEM((dil, BLOCK, HEAD), F32), pltpu.VMEM((dil, BLOCK, HEAD), F32)],
        compiler_params=_cparams(2, blk, 2 * _nbytes((bt, HEAD), F32)))(
            qn, kn, kn, proj, proj, dattn, lse, delta)


def _window_sum(v, n_doublings, back):
    rows = v.shape[0]
    step = 1
    for _ in range(n_doublings):
        v = v + pltpu.roll(v, step if back else rows - step, 0)
        step *= 2
    return v


def _mix_post(name, outs, lses, proj, pool_w, pool_scale, *, tm):
    S = proj.shape[0]
    ucol = (IN_W - POOL_W) // POOL_W
    hpt = tm // POOL_HALO

    def body(o0, o1, o2, l0, l1, l2, u_ref, uh_ref, pw_ref, ps_ref, mix_ref, lse_ref, dp_ref):
        i = pl.program_id(0)
        for h in range(N_HEADS):
            sl = slice(h * HEAD, (h + 1) * HEAD)
            ls = [l0[:, sl], l1[:, sl], l2[:, sl]]
            m = jnp.maximum(jnp.maximum(ls[0], ls[1]), ls[2])
            ws = [jnp.exp(v - m) for v in ls]
            den = ws[0] + ws[1] + ws[2]
            num = ws[0] * o0[:, sl] + ws[1] * o1[:, sl] + ws[2] * o2[:, sl]
            mix_ref[:, sl] = _bf(num / den)
            lse_ref[:, sl] = m + jnp.log(den)
        halo = jnp.where(i == 0, 0.0, uh_ref[...])
        t = lax.broadcasted_iota(jnp.int32, (tm + POOL_HALO, HEAD), 0) + (i * tm - POOL_HALO)
        for g, w in enumerate(POOL_WINDOWS):
            sl = slice(g * HEAD, (g + 1) * HEAD)
            ub = jnp.concatenate([halo[:, sl], u_ref[:, sl]], axis=0)
            cnt = jnp.minimum(t + 1, w).astype(F32)
            d = (_window_sum(ub, g + 1, True) / cnt - ub)[POOL_HALO:]
            db = _bf(d)
            dp_ref[:, sl] = db
            mix_ref[:, KV_W + g * HEAD:KV_W + (g + 1) * HEAD] = _bf(_dot(db, pw_ref[g]) * ps_ref[:, sl])

    tile = pl.BlockSpec((tm, KV_W), lambda i: (i, 0))
    blk = 9 * _nbytes((tm, KV_W), F32) + _nbytes((tm, 2 * KV_W), BF16)
    return pl.pallas_call(
        body, name=name, grid=(S // tm,),
        in_specs=[tile] * 6 + [
            pl.BlockSpec((tm, POOL_W), lambda i: (i, ucol)),
            pl.BlockSpec((POOL_HALO, POOL_W), lambda i: (jnp.maximum(i * hpt - 1, 0), ucol)),
            pl.BlockSpec((len(POOL_WINDOWS), HEAD, HEAD), lambda i: (0, 0, 0)),
            pl.BlockSpec((1, POOL_W), lambda i: (0, 0))],
        out_specs=[pl.BlockSpec((tm, 2 * KV_W), lambda i: (i, 0)), tile, tile],
        out_shape=[jax.ShapeDtypeStruct((S, 2 * KV_W), BF16), jax.ShapeDtypeStruct((S, KV_W), F32),
                   jax.ShapeDtypeStruct((S, POOL_W), BF16)],
        compiler_params=_cparams(1, blk))(*outs, *lses, proj, proj, pool_w, pool_scale)


def _mix_bwd(name, dmix, mix, dpool, pool_w, pool_scale, *, tm):
    S = dmix.shape[0]
    hpt = tm // POOL_HALO
    last_halo = S // POOL_HALO - 1
    n_tiles = S // tm

    def body(dm_ref, dh_ref, at_ref, dp_ref, pw_ref, ps_ref, da_ref, dl_ref, du_ref, gw_ref, gs_ref):
        i = pl.program_id(0)

        @pl.when(i == 0)
        def _():
            gw_ref[...] = jnp.zeros_like(gw_ref)
            gs_ref[...] = jnp.zeros_like(gs_ref)

        for h in range(N_HEADS):
            sl = slice(h * HEAD, (h + 1) * HEAD)
            da = dm_ref[:, sl]
            da_ref[:, sl] = da
            dl_ref[:, sl] = jnp.broadcast_to(
                jnp.sum(da * at_ref[:, sl].astype(F32), axis=-1, keepdims=True), (tm, HEAD))
        halo = jnp.where(i == n_tiles - 1, 0.0, dh_ref[...])
        t = lax.broadcasted_iota(jnp.int32, (tm + POOL_HALO, HEAD), 0) + i * tm
        for g, w in enumerate(POOL_WINDOWS):
            sl = slice(g * HEAD, (g + 1) * HEAD)
            dy = jnp.concatenate([dm_ref[:, KV_W + g * HEAD:KV_W + (g + 1) * HEAD], halo[:, sl]], axis=0)
            dys = _bf(dy * ps_ref[:, sl])
            dd = _dot_nt(dys, pw_ref[g])
            cnt = jnp.minimum(t + 1, w).astype(F32)
            du_ref[:, sl] = (_window_sum(dd / cnt, g + 1, False) - dd)[:tm]
            db = dp_ref[:, sl]
            gw_ref[g] += _dot_tn(db, dys[:tm])
            gs_ref[:, sl] += jnp.sum(dy[:tm] * _dot(db, pw_ref[g]), axis=0, keepdims=True)

    tile = pl.BlockSpec((tm, KV_W), lambda i: (i, 0))
    blk = _nbytes((tm, 2 * KV_W), F32) + 6 * _nbytes((tm, KV_W), F32)
    return pl.pallas_call(
        body, name=name, grid=(n_tiles,),
        in_specs=[pl.BlockSpec((tm, 2 * KV_W), lambda i: (i, 0)),
                  pl.BlockSpec((POOL_HALO, POOL_W), lambda i: (jnp.minimum((i + 1) * hpt, last_halo), 1)),
                  tile, tile,
                  pl.BlockSpec((len(POOL_WINDOWS), HEAD, HEAD), lambda i: (0, 0, 0)),
                  pl.BlockSpec((1, POOL_W), lambda i: (0, 0))],
        out_specs=[tile, tile, tile,
                   pl.BlockSpec((len(POOL_WINDOWS), HEAD, HEAD), lambda i: (0, 0, 0)),
                   pl.BlockSpec((1, POOL_W), lambda i: (0, 0))],
        out_shape=[jax.ShapeDtypeStruct((S, KV_W), F32)] * 3 + [
            jax.ShapeDtypeStruct((len(POOL_WINDOWS), HEAD, HEAD), F32), jax.ShapeDtypeStruct((1, POOL_W), F32)],
        compiler_params=_cparams(1, blk))(dmix, dmix, mix, dpool, pool_w, pool_scale)


def _qkv_bwd(name, dqs, dks, dvs, du, proj, ctab, stab, qg, kg, *, tm):
    S = proj.shape[0]
    width = Q_W + KV_W

    def body(dq0, dq1, dq2, dk0, dk1, dk2, dv0, dv1, dv2, du_ref, p_ref, c_ref, s_ref, qg_ref, kg_ref,
             dp_ref, gq_ref, gk_ref):
        @pl.when(pl.program_id(0) == 0)
        def _():
            gq_ref[...] = jnp.zeros_like(gq_ref)
            gk_ref[...] = jnp.zeros_like(gk_ref)

        lane = lax.broadcasted_iota(jnp.int32, (tm, HEAD), 1)
        cv = c_ref[...]
        sv = s_ref[...]

        def back(dy, t, g, scale):
            dy = dy * scale
            dn = dy * cv + _swap_halves(dy * sv, lane, True)
            dt, xh = _norm_bwd(dn, t, g)
            return dt, jnp.sum(dn * xh, axis=0, keepdims=True)

        dqr = (dq0, dq1, dq2)
        gq = jnp.zeros((1, HEAD), F32)
        for j in range(Q_W // HEAD):
            grp, h = divmod(j, N_HEADS)
            dt, gj = back(dqr[grp][:, h * HEAD:(h + 1) * HEAD], p_ref[:, j * HEAD:(j + 1) * HEAD], qg_ref[...], SCALE)
            dp_ref[:, j * HEAD:(j + 1) * HEAD] = _bf(dt)
            gq = gq + gj
        gq_ref[...] += gq
        gk = jnp.zeros((1, HEAD), F32)
        for h in range(N_HEADS):
            sl = slice(h * HEAD, (h + 1) * HEAD)
            dt, gj = back(dk0[:, sl] + dk1[:, sl] + dk2[:, sl], p_ref[:, Q_W + h * HEAD:Q_W + (h + 1) * HEAD],
                          kg_ref[...], 1.0)
            dp_ref[:, Q_W + h * HEAD:Q_W + (h + 1) * HEAD] = _bf(dt)
            gk = gk + gj
        gk_ref[...] += gk
        dp_ref[:, width:width + KV_W] = _bf(dv0[...] + dv1[...] + dv2[...])
        dp_ref[:, width + KV_W:] = _bf(du_ref[...])

    tile = pl.BlockSpec((tm, KV_W), lambda i: (i, 0))
    vec = pl.BlockSpec((1, HEAD), lambda i: (0, 0))
    rot = pl.BlockSpec((tm, HEAD), lambda i: (i, 0))
    blk = 10 * _nbytes((tm, KV_W), F32) + _nbytes((tm, width), F32) + _nbytes((tm, IN_W), BF16)
    return pl.pallas_call(
        body, name=name, grid=(S // tm,),
        in_specs=[tile] * 10 + [pl.BlockSpec((tm, width), lambda i: (i, 0)), rot, rot, vec, vec],
        out_specs=[pl.BlockSpec((tm, IN_W), lambda i: (i, 0)), vec, vec],
        out_shape=[jax.ShapeDtypeStruct((S, IN_W), BF16), jax.ShapeDtypeStruct((1, HEAD), F32),
                   jax.ShapeDtypeStruct((1, HEAD), F32)],
        compiler_params=_cparams(1, blk))(*dqs, *dks, *dvs, du, proj, ctab, stab, qg, kg)


def _cross_heads(q_ref, kv_ref, qg, kg, h):
    sl = slice(h * HEAD, (h + 1) * HEAD)
    qr = q_ref[:, sl]
    kr = kv_ref[:, sl]
    qh = qr * _rstd(qr) * qg * SCALE
    kh = kr * _rstd(kr) * kg
    vh = kv_ref[:, X_W + h * HEAD:X_W + (h + 1) * HEAD]
    return qr, _bf(qh), _bf(kh), _bf(vh)


def _cross_fwd(name, qraw, kv, qg, kg, *, tm):
    S = qraw.shape[0]
    M = kv.shape[0]

    def body(q_ref, kv_ref, qg_ref, kg_ref, o_ref):
        for h in range(N_HEADS):
            _, qh, kh, vh = _cross_heads(q_ref, kv_ref, qg_ref[...], kg_ref[...], h)
            s = _dot_nt(qh, kh)
            p = jnp.exp(s - jnp.max(s, axis=-1, keepdims=True))
            l = jnp.sum(p, axis=-1, keepdims=True)
            o_ref[:, h * HEAD:(h + 1) * HEAD] = _bf(_dot(_bf(p), vh) / l)

    vec = pl.BlockSpec((1, HEAD), lambda i: (0, 0))
    blk = 2 * _nbytes((tm, X_W), F32) + _nbytes((M, 2 * X_W), F32) + 4 * _nbytes((tm, M), F32)
    return pl.pallas_call(
        body, name=name, grid=(S // tm,),
        in_specs=[pl.BlockSpec((tm, X_W), lambda i: (i, 0)), pl.BlockSpec((M, 2 * X_W), lambda i: (0, 0)), vec, vec],
        out_specs=pl.BlockSpec((tm, X_W), lambda i: (i, 0)),
        out_shape=jax.ShapeDtypeStruct((S, X_W), BF16),
        compiler_params=_cparams(1, blk))(qraw, kv, qg, kg)


def _cross_bwd(name, do, qraw, kv, qg, kg, *, tm):
    S = qraw.shape[0]
    M = kv.shape[0]

    def body(do_ref, q_ref, kv_ref, qg_ref, kg_ref, dq_ref, dk_ref, dv_ref, gq_ref):
        @pl.when(pl.program_id(0) == 0)
        def _():
            dk_ref[...] = jnp.zeros_like(dk_ref)
            dv_ref[...] = jnp.zeros_like(dv_ref)
            gq_ref[...] = jnp.zeros_like(gq_ref)

        gq = jnp.zeros((1, HEAD), F32)
        for h in range(N_HEADS):
            sl = slice(h * HEAD, (h + 1) * HEAD)
            qr, qh, kh, vh = _cross_heads(q_ref, kv_ref, qg_ref[...], kg_ref[...], h)
            doh = _bf(do_ref[:, sl])
            s = _dot_nt(qh, kh)
            p = jnp.exp(s - jnp.max(s, axis=-1, keepdims=True))
            p = p / jnp.sum(p, axis=-1, keepdims=True)
            pb = _bf(p)
            dp = _dot_nt(doh, vh)
            ds = _bf(p * (dp - jnp.sum(dp * p, axis=-1, keepdims=True)))
            dv_ref[:, sl] += _dot_tn(pb, doh)
            dk_ref[:, sl] += _dot_tn(ds, qh)
            dn = _dot(ds, kh) * SCALE
            dt, xh = _norm_bwd(dn, qr, qg_ref[...])
            dq_ref[:, sl] = _bf(dt)
            gq = gq + jnp.sum(dn * xh, axis=0, keepdims=True)
        gq_ref[...] += gq

    vec = pl.BlockSpec((1, HEAD), lambda i: (0, 0))
    acc = pl.BlockSpec((M, X_W), lambda i: (0, 0))
    blk = 3 * _nbytes((tm, X_W), F32) + 3 * _nbytes((M, 2 * X_W), F32) + 6 * _nbytes((tm, M), F32)
    return pl.pallas_call(
        body, name=name, grid=(S // tm,),
        in_specs=[pl.BlockSpec((tm, X_W), lambda i: (i, 0)), pl.BlockSpec((tm, X_W), lambda i: (i, 0)),
                  pl.BlockSpec((M, 2 * X_W), lambda i: (0, 0)), vec, vec],
        out_specs=[pl.BlockSpec((tm, X_W), lambda i: (i, 0)), acc, acc, vec],
        out_shape=[jax.ShapeDtypeStruct((S, X_W), BF16), jax.ShapeDtypeStruct((M, X_W), F32),
                   jax.ShapeDtypeStruct((M, X_W), F32), jax.ShapeDtypeStruct((1, HEAD), F32)],
        compiler_params=_cparams(1, blk))(do, qraw, kv, qg, kg)


def _cross_kv_bwd(name, dkn, dv, kv, kg):
    M = kv.shape[0]

    def body(dk_ref, dv_ref, kv_ref, kg_ref, o_ref, g_ref):
        gk = jnp.zeros((1, HEAD), F32)
        for h in range(N_HEADS):
            sl = slice(h * HEAD, (h + 1) * HEAD)
            dn = dk_ref[:, sl]
            dt, xh = _norm_bwd(dn, kv_ref[:, sl], kg_ref[...])
            o_ref[:, sl] = _bf(dt)
            gk = gk + jnp.sum(dn * xh, axis=0, keepdims=True)
        o_ref[:, X_W:] = _bf(dv_ref[...])
        g_ref[...] = gk

    full = lambda shape: pl.BlockSpec(shape, lambda i: (0,) * len(shape))
    return pl.pallas_call(
        body, name=name, grid=(1,),
        in_specs=[full((M, X_W)), full((M, X_W)), full((M, 2 * X_W)), full((1, HEAD))],
        out_specs=[full((M, 2 * X_W)), full((1, HEAD))],
        out_shape=[jax.ShapeDtypeStruct((M, 2 * X_W), BF16), jax.ShapeDtypeStruct((1, HEAD), F32)],
        compiler_params=_cparams(1, 6 * _nbytes((M, 2 * X_W), F32)))(dkn, dv, kv, kg)


def _rope_tables(positions):
    inv_freq = ROPE_THETA ** (-jnp.arange(0, ROT_DIM, 2, dtype=F32) / ROT_DIM)
    ang = positions.astype(F32)[:, None] * inv_freq
    cos, sin = jnp.cos(ang), jnp.sin(ang)
    S = positions.shape[0]
    ctab = jnp.concatenate([cos, cos, jnp.ones((S, HEAD - ROT_DIM), F32)], axis=-1)
    stab = jnp.concatenate([-sin, sin, jnp.zeros((S, HEAD - ROT_DIM), F32)], axis=-1)
    return ctab, stab


def _local_step(x, mem, positions, target, wf, sm, *, tm=512):
    S, D = x.shape
    M = mem.shape[0]
    ctab, stab = _rope_tables(positions)
    w_in = wf["w_in"]
    w_out = wf["w_out"].reshape(1, 2 * KV_W, D)
    w_cq = wf["w_cq"].reshape(1, D, X_W)
    w_ckv = wf["w_ckv"].reshape(1, D, 2 * X_W)
    w_co = wf["w_co"]
    w_gu = wf["w_gate_up"]
    w_down = wf["w_down"].reshape(1, D_FF, D)
    pool_w_b = _bf(sm["pool_w"])
    cin = w_in.shape[2]
    cco = w_co.shape[2]
    cgu = w_gu.shape[2]

    proj, xn1 = _mm_nn("in_proj", x, w_in, tm=tm, tn=cin, norm_g=sm["mix_norm_g"])
    qn, kn = _qk_prep("qk_prep", proj, ctab, stab, sm["q_norm_g"], sm["k_norm_g"], tm=tm)
    outs, lses = [], []
    for grp in range(len(DILATIONS)):
        o, l = _attn_fwd(f"attn_fwd{grp}", qn, kn, proj, grp)
        outs.append(o)
        lses.append(l)
    mix, lse, dpool = _mix_post("mix_post", outs, lses, proj, pool_w_b, sm["pool_scale"], tm=tm)
    h1 = _mm_nn("out_proj", mix, w_out, tm=tm, tn=D, residual=x)
    cq_raw, hn2 = _mm_nn("cq_proj", h1, w_cq, tm=tm, tn=X_W, norm_g=sm["cross_norm_g"])
    kv, mem_n = _mm_nn("ckv_proj", mem, w_ckv, tm=M, tn=2 * X_W, norm_g=sm["mem_norm_g"])
    xo = _cross_fwd("cross_fwd", cq_raw, kv, sm["cq_norm_g"], sm["ck_norm_g"], tm=tm)
    h2 = _mm_nn("co_proj", xo, w_co, tm=tm, tn=cco, residual=h1)
    act, gu, hn3 = _ffn_up("ffn_up", h2, w_gu, sm["ffn_norm_g"], tm=tm)
    dy, lsum = _ffn_down_loss("ffn_down_loss", act, w_down, h2, target, tm=tm)
    loss = 0.5 * jnp.sum(lsum) / D

    nS = S // tm
    dgu = _ffn_down_bwd("ffn_down_bwd", dy, w_down, gu, tm=tm, tn=cgu)
    g_down = _mm_tn("g_w_down", act, pl.BlockSpec((tm, cgu), lambda r, c, s: (s, r)), (tm, cgu),
                    dy, pl.BlockSpec((tm, D), lambda r, c, s: (s, 0)), (tm, D),
                    (1, D_FF, D), pl.BlockSpec((None, cgu, D), lambda r, c, s: (0, r, 0)), (cgu, D),
                    (D_FF // cgu, 1, nS))
    dh2, g_ffn_norm = _mm_nt_normbwd(
        "ffn_up_bwd", dgu, pl.BlockSpec((None, tm, cgu), lambda i, p: (p // 2, i, p % 2)), (tm, cgu),
        w_gu, pl.BlockSpec((None, D, cgu), lambda i, p: (p, 0, 0)), (D, cgu), 4, h2, sm["ffn_norm_g"], dy, tm=tm)
    g_gu = _mm_tn("g_w_gate_up", hn3, pl.BlockSpec((tm, D), lambda r, c, s: (s, 0)), (tm, D),
                  dgu, pl.BlockSpec((None, tm, cgu), lambda r, c, s: (c // 2, s, c % 2)), (tm, cgu),
                  (4, D, cgu), pl.BlockSpec((None, D, cgu), lambda r, c, s: (c, 0, 0)), (D, cgu),
                  (1, 4, nS))

    dxo = _mm_nt("co_proj_bwd", dh2, pl.BlockSpec((tm, cco), lambda i, p: (i, p)), (tm, cco),
                 w_co, pl.BlockSpec((None, X_W, cco), lambda i, p: (p, 0, 0)), (X_W, cco), N_CHIPS, S, X_W,
                 tm=tm, out_dtype=BF16)
    g_co = _mm_tn("g_w_co", xo, pl.BlockSpec((tm, X_W), lambda r, c, s: (s, 0)), (tm, X_W),
                  dh2, pl.BlockSpec((tm, cco), lambda r, c, s: (s, c)), (tm, cco),
                  (N_CHIPS, X_W, cco), pl.BlockSpec((None, X_W, cco), lambda r, c, s: (c, 0, 0)), (X_W, cco),
                  (1, N_CHIPS, nS))
    dcq, dkn, dvm, g_cq_norm = _cross_bwd("cross_bwd", dxo, cq_raw, kv, sm["cq_norm_g"], sm["ck_norm_g"], tm=tm)
    dkv, g_ck_norm = _cross_kv_bwd("cross_kv_bwd", dkn, dvm, kv, sm["ck_norm_g"])
    dh1, g_cross_norm = _mm_nt_normbwd(
        "cq_proj_bwd", dcq, pl.BlockSpec((tm, X_W), lambda i, p: (i, 0)), (tm, X_W),
        w_cq, pl.BlockSpec((None, D, X_W), lambda i, p: (0, 0, 0)), (D, X_W), 1, h1, sm["cross_norm_g"], dh2, tm=tm)
    g_cq = _mm_tn("g_w_cq", hn2, pl.BlockSpec((tm, D), lambda r, c, s: (s, 0)), (tm, D),
                  dcq, pl.BlockSpec((tm, X_W), lambda r, c, s: (s, 0)), (tm, X_W),
                  (1, D, X_W), pl.BlockSpec((None, D, X_W), lambda r, c, s: (0, 0, 0)), (D, X_W), (1, 1, nS))
    _, g_mem_norm = _mm_nt_normbwd(
        "ckv_proj_bwd", dkv, pl.BlockSpec((M, 2 * X_W), lambda i, p: (0, 0)), (M, 2 * X_W),
        w_ckv, pl.BlockSpec((None, D, 2 * X_W), lambda i, p: (0, 0, 0)), (D, 2 * X_W), 1, mem, sm["mem_norm_g"],
        None, tm=M)
    g_ckv = _mm_tn("g_w_ckv", mem_n, pl.BlockSpec((M, D), lambda r, c, s: (0, 0)), (M, D),
                   dkv, pl.BlockSpec((M, 2 * X_W), lambda r, c, s: (0, 0)), (M, 2 * X_W),
                   (1, D, 2 * X_W), pl.BlockSpec((None, D, 2 * X_W), lambda r, c, s: (0, 0, 0)), (D, 2 * X_W),
                   (1, 1, 1))

    dmix = _mm_nt("out_proj_bwd", dh1, pl.BlockSpec((tm, D), lambda i, p: (i, 0)), (tm, D),
                  w_out, pl.BlockSpec((None, 2 * KV_W, D), lambda i, p: (0, 0, 0)), (2 * KV_W, D), 1, S, 2 * KV_W,
                  tm=tm, out_dtype=F32)
    g_out = _mm_tn("g_w_out", mix, pl.BlockSpec((tm, 2 * KV_W), lambda r, c, s: (s, 0)), (tm, 2 * KV_W),
                   dh1, pl.BlockSpec((tm, D), lambda r, c, s: (s, 0)), (tm, D),
                   (1, 2 * KV_W, D), pl.BlockSpec((None, 2 * KV_W, D), lambda r, c, s: (0, 0, 0)), (2 * KV_W, D),
                   (1, 1, nS))
    dattn, delta, du, g_pool_w, g_pool_scale = _mix_bwd("mix_bwd", dmix, mix, dpool, pool_w_b, sm["pool_scale"], tm=tm)
    dqs, dks, dvs = [], [], []
    for grp in range(len(DILATIONS)):
        dq, dk, dv = _attn_bwd(f"attn_bwd{grp}", qn, kn, proj, dattn, lse, delta, grp)
        dqs.append(dq)
        dks.append(dk)
        dvs.append(dv)
    dproj, g_q_norm, g_k_norm = _qkv_bwd("qkv_bwd", dqs, dks, dvs, du, proj, ctab, stab,
                                         sm["q_norm_g"], sm["k_norm_g"], tm=tm)
    dx, g_mix_norm = _mm_nt_normbwd(
        "in_proj_bwd", dproj, pl.BlockSpec((tm, cin), lambda i, p: (i, p)), (tm, cin),
        w_in, pl.BlockSpec((None, D, cin), lambda i, p: (p, 0, 0)), (D, cin), N_CHIPS, x, sm["mix_norm_g"], dh1, tm=tm)
    g_in = _mm_tn("g_w_in", xn1, pl.BlockSpec((tm, D), lambda r, c, s: (s, 0)), (tm, D),
                  dproj, pl.BlockSpec((tm, cin), lambda r, c, s: (s, c)), (tm, cin),
                  (N_CHIPS, D, cin), pl.BlockSpec((None, D, cin), lambda r, c, s: (c, 0, 0)), (D, cin),
                  (1, N_CHIPS, nS))

    big = {
        "w_in": g_in,
        "w_out": g_out.reshape(N_CHIPS, 2 * KV_W // N_CHIPS, D),
        "w_cq": g_cq.reshape(N_CHIPS, D // N_CHIPS, X_W),
        "w_ckv": g_ckv.reshape(N_CHIPS, D // N_CHIPS, 2 * X_W),
        "w_co": g_co,
        "w_gate_up": g_gu,
        "w_down": g_down.reshape(N_CHIPS, D_FF // N_CHIPS, D),
    }
    small = {
        "mix_norm_g": g_mix_norm, "q_norm_g": g_q_norm, "k_norm_g": g_k_norm, "pool_w": g_pool_w,
        "pool_scale": g_pool_scale, "cross_norm_g": g_cross_norm, "mem_norm_g": g_mem_norm,
        "cq_norm_g": g_cq_norm, "ck_norm_g": g_ck_norm, "ffn_norm_g": g_ffn_norm,
    }
    return loss, dx, big, small


BIG = ("w_in", "w_out", "w_cq", "w_ckv", "w_co", "w_gate_up", "w_down")
SMALL = ("mix_norm_g", "q_norm_g", "k_norm_g", "pool_w", "pool_scale", "cross_norm_g", "mem_norm_g",
         "cq_norm_g", "ck_norm_g", "ffn_norm_g")
WEIGHTS = ("mix_norm_g", "w_in", "q_norm_g", "k_norm_g", "pool_w", "pool_scale", "w_out", "cross_norm_g",
           "mem_norm_g", "w_cq", "w_ckv", "cq_norm_g", "ck_norm_g", "w_co", "ffn_norm_g", "w_gate_up", "w_down")


def _place():
    x, y, c = lax.axis_index("x"), lax.axis_index("y"), lax.axis_index("c")
    other_chips = [(1 - x, y), (x, 1 - y), (1 - x, 1 - y)]
    return x, y, c, other_chips


def _gather_weights(shards):
    n = len(shards)
    shapes = [s.shape for s in shards]

    def body(*refs):
        ins, outs = refs[:n], refs[n:2 * n]
        send, recv, lsem = refs[2 * n:]
        x, y, c, chips = _place()
        me = 2 * x + y
        sib = (x, y, 1 - c)

        def half(i, ref, which):
            hr = shapes[i][0] // 2
            return ref.at[pl.ds(which * hr, hr), :]

        def copy(i, k, src, dst, to):
            return pltpu.make_async_remote_copy(src_ref=src, dst_ref=dst, send_sem=send.at[i, k], recv_sem=recv.at[i, k],
                                                device_id=to, device_id_type=MESH)

        started = []
        local = []
        for i in range(n):
            lc = pltpu.make_async_copy(ins[i], outs[i].at[me], lsem.at[i])
            lc.start()
            local.append(lc)
            for j, (cx, cy) in enumerate(chips):
                cp = copy(i, j, half(i, ins[i], c), half(i, outs[i].at[me], c), (cx, cy, c))
                cp.start()
                started.append(cp)
        for i in range(n):
            for j, (cx, cy) in enumerate(chips):
                piece = half(i, outs[i].at[2 * cx + cy], c)
                copy(i, j, piece, piece, (cx, cy, c)).wait_recv()
                fw = copy(i, 3 + j, piece, piece, sib)
                fw.start()
                started.append(fw)
        for i in range(n):
            for j, (cx, cy) in enumerate(chips):
                piece = half(i, outs[i].at[2 * cx + cy], 1 - c)
                copy(i, 3 + j, piece, piece, sib).wait_recv()
        for cp in started:
            cp.wait_send()
        for lc in local:
            lc.wait()

    return pl.pallas_call(
        body, name="gather_weights", in_specs=[ANY] * n, out_specs=[ANY] * n,
        out_shape=[jax.ShapeDtypeStruct((N_CHIPS,) + s, BF16) for s in shapes],
        scratch_shapes=[pltpu.SemaphoreType.DMA((n, 6)), pltpu.SemaphoreType.DMA((n, 6)), pltpu.SemaphoreType.DMA((n,))],
    )(*shards)


def _swap_core_halves(grads):
    n = len(grads)
    shapes = [g.shape for g in grads]

    def body(*refs):
        ins, outs = refs[:n], refs[n:2 * n]
        send, recv = refs[2 * n:]
        x, y, c, _ = _place()
        cps = []
        for i in range(n):
            hr = shapes[i][1] // 2
            cp = pltpu.make_async_remote_copy(
                src_ref=ins[i].at[:, pl.ds((1 - c) * hr, hr), :], dst_ref=outs[i], send_sem=send.at[i],
                recv_sem=recv.at[i], device_id=(x, y, 1 - c), device_id_type=MESH)
            cp.start()
            cps.append(cp)
        for cp in cps:
            cp.wait()

    return pl.pallas_call(
        body, name="swap_core_halves", in_specs=[ANY] * n, out_specs=[ANY] * n,
        out_shape=[jax.ShapeDtypeStruct((s[0], s[1] // 2, s[2]), F32) for s in shapes],
        scratch_shapes=[pltpu.SemaphoreType.DMA((n,)), pltpu.SemaphoreType.DMA((n,))],
    )(*grads)


def _add_core_halves(name, g, t, c_arr):
    P, R, C = g.shape
    hr = R // 2

    def body(c_ref, g_ref, t_ref, o_ref):
        o_ref[...] = g_ref[...] + t_ref[...]

    return pl.pallas_call(
        body, name=name,
        grid_spec=pltpu.PrefetchScalarGridSpec(
            num_scalar_prefetch=1, grid=(P,),
            in_specs=[pl.BlockSpec((None, hr, C), lambda p, c: (p, c[0], 0)),
                      pl.BlockSpec((None, hr, C), lambda p, c: (p, 0, 0))],
            out_specs=pl.BlockSpec((None, hr, C), lambda p, c: (p, 0, 0))),
        out_shape=jax.ShapeDtypeStruct((P, hr, C), F32),
        compiler_params=_cparams(1, 3 * _nbytes((hr, C), F32)))(c_arr, g, t)


def _exchange_chip_sums(sums):
    n = len(sums)
    shapes = [s.shape for s in sums]

    def body(*refs):
        ins, outs = refs[:n], refs[n:2 * n]
        send, recv = refs[2 * n:]
        x, y, c, chips = _place()
        cps = []
        for i in range(n):
            for j, (cx, cy) in enumerate(chips):
                cp = pltpu.make_async_remote_copy(
                    src_ref=ins[i].at[2 * cx + cy], dst_ref=outs[i].at[j], send_sem=send.at[i, j],
                    recv_sem=recv.at[i, j], device_id=(cx, cy, c), device_id_type=MESH)
                cp.start()
                cps.append(cp)
        for cp in cps:
            cp.wait()

    return pl.pallas_call(
        body, name="exchange_chip_sums", in_specs=[ANY] * n, out_specs=[ANY] * n,
        out_shape=[jax.ShapeDtypeStruct((N_CHIPS - 1,) + s[1:], F32) for s in shapes],
        scratch_shapes=[pltpu.SemaphoreType.DMA((n, 3)), pltpu.SemaphoreType.DMA((n, 3))],
    )(*sums)


def _sum_chips(name, own, got, k_arr):
    P, hr, C = own.shape

    def body(k_ref, o_ref, g_ref, r_ref):
        r_ref[...] = ((o_ref[...] + g_ref[0]) + g_ref[1]) + g_ref[2]

    return pl.pallas_call(
        body, name=name,
        grid_spec=pltpu.PrefetchScalarGridSpec(
            num_scalar_prefetch=1, grid=(1,),
            in_specs=[pl.BlockSpec((None, hr, C), lambda i, k: (k[0], 0, 0)),
                      pl.BlockSpec((N_CHIPS - 1, hr, C), lambda i, k: (0, 0, 0))],
            out_specs=pl.BlockSpec((hr, C), lambda i, k: (0, 0))),
        out_shape=jax.ShapeDtypeStruct((hr, C), F32),
        compiler_params=_cparams(1, 5 * _nbytes((hr, C), F32)))(k_arr, own, got)


def _join_core_halves(halves):
    n = len(halves)
    shapes = [h.shape for h in halves]

    def body(*refs):
        ins, outs = refs[:n], refs[n:2 * n]
        send, recv, lsem = refs[2 * n:]
        x, y, c, _ = _place()
        cps = []
        for i in range(n):
            hr = shapes[i][0]
            mine = outs[i].at[pl.ds(c * hr, hr), :]
            lc = pltpu.make_async_copy(ins[i], mine, lsem.at[i])
            lc.start()
            cp = pltpu.make_async_remote_copy(src_ref=ins[i], dst_ref=mine, send_sem=send.at[i], recv_sem=recv.at[i],
                                              device_id=(x, y, 1 - c), device_id_type=MESH)
            cp.start()
            cps += [lc, cp]
        for cp in cps:
            cp.wait()

    return pl.pallas_call(
        body, name="join_core_halves", in_specs=[ANY] * n, out_specs=[ANY] * n,
        out_shape=[jax.ShapeDtypeStruct((2 * s[0], s[1]), F32) for s in shapes],
        scratch_shapes=[pltpu.SemaphoreType.DMA((n,)), pltpu.SemaphoreType.DMA((n,)), pltpu.SemaphoreType.DMA((n,))],
    )(*halves)


def _allreduce_small(v):
    R, C = v.shape
    n_dev = 8
    flips = [(dx, dy, dc) for dx in (0, 1) for dy in (0, 1) for dc in (0, 1)][1:]

    def body(v_ref, o_ref, slots, send, recv):
        x, y, c, _ = _place()
        me = 4 * x + 2 * y + c
        cps = []
        for q, (dx, dy, dc) in enumerate(flips):
            to = (x + dx - 2 * x * dx, y + dy - 2 * y * dy, c + dc - 2 * c * dc)
            cp = pltpu.make_async_remote_copy(src_ref=v_ref, dst_ref=slots.at[me], send_sem=send.at[q],
                                              recv_sem=recv.at[q], device_id=to, device_id_type=MESH)
            cp.start()
            cps.append(cp)
        slots[me] = v_ref[...]
        for cp in cps:
            cp.wait_recv()
        acc = slots[0]
        for d in range(1, n_dev):
            acc = acc + slots[d]
        o_ref[...] = acc
        for cp in cps:
            cp.wait_send()

    return pl.pallas_call(
        body, name="allreduce_small",
        in_specs=[pl.BlockSpec(memory_space=pltpu.VMEM)], out_specs=pl.BlockSpec(memory_space=pltpu.VMEM),
        out_shape=jax.ShapeDtypeStruct((R, C), F32),
        scratch_shapes=[pltpu.VMEM((n_dev, R, C), F32), pltpu.SemaphoreType.DMA((n_dev - 1,)),
                        pltpu.SemaphoreType.DMA((n_dev - 1,))],
    )(v)


def _adamw(name, w, g, m, v, *, tr):
    R, C = w.shape

    def body(w_ref, g_ref, m_ref, v_ref, d_ref, nm_ref, nv_ref):
        gv = g_ref[...]
        nm = ADAM_B1 * m_ref[...] + (1.0 - ADAM_B1) * gv
        nv = ADAM_B2 * v_ref[...] + (1.0 - ADAM_B2) * (gv * gv)
        m_hat = nm / (1.0 - ADAM_B1 ** ADAM_STEP)
        v_hat = nv / (1.0 - ADAM_B2 ** ADAM_STEP)
        d_ref[...] = -ADAM_LR * (m_hat / (jnp.sqrt(v_hat) + ADAM_EPS) + ADAM_WD * w_ref[...])
        nm_ref[...] = nm
        nv_ref[...] = nv

    tile = pl.BlockSpec((tr, C), lambda i: (i, 0))
    return pl.pallas_call(
        body, name=name, grid=(R // tr,), in_specs=[tile] * 4, out_specs=[tile] * 3,
        out_shape=[jax.ShapeDtypeStruct((R, C), F32)] * 3,
        compiler_params=_cparams(1, 7 * _nbytes((tr, C), F32)))(w, g, m, v)


def _pack_small(d):
    parts = []
    for name in SMALL:
        a = d[name].reshape(-1, HEAD)
        pad = (-a.shape[0]) % 8
        parts.append(jnp.pad(a, ((0, pad), (0, 0))))
    return jnp.concatenate(parts, axis=0)


def _unpack_small(packed, like):
    out = {}
    row = 0
    for name in SMALL:
        shape = like[name].shape
        rows = like[name].size // HEAD
        out[name] = packed[row:row + rows].reshape(shape)
        row += rows + (-rows) % 8
    return out


def kernel(x, mem, positions, mix_norm_g, w_in, q_norm_g, k_norm_g, pool_w, pool_scale, w_out, cross_norm_g, mem_norm_g, w_cq, w_ckv, cq_norm_g, ck_norm_g, w_co, ffn_norm_g, w_gate_up, w_down, loss_target, m_mix_norm_g, m_w_in, m_q_norm_g, m_k_norm_g, m_pool_w, m_pool_scale, m_w_out, m_cross_norm_g, m_mem_norm_g, m_w_cq, m_w_ckv, m_cq_norm_g, m_ck_norm_g, m_w_co, m_ffn_norm_g, m_w_gate_up, m_w_down, v_mix_norm_g, v_w_in, v_q_norm_g, v_k_norm_g, v_pool_w, v_pool_scale, v_w_out, v_cross_norm_g, v_mem_norm_g, v_w_cq, v_w_ckv, v_cq_norm_g, v_ck_norm_g, v_w_co, v_ffn_norm_g, v_w_gate_up, v_w_down):
    w = dict(mix_norm_g=mix_norm_g, w_in=w_in, q_norm_g=q_norm_g, k_norm_g=k_norm_g, pool_w=pool_w,
             pool_scale=pool_scale, w_out=w_out, cross_norm_g=cross_norm_g, mem_norm_g=mem_norm_g, w_cq=w_cq,
             w_ckv=w_ckv, cq_norm_g=cq_norm_g, ck_norm_g=ck_norm_g, w_co=w_co, ffn_norm_g=ffn_norm_g,
             w_gate_up=w_gate_up, w_down=w_down)
    m = dict(mix_norm_g=m_mix_norm_g, w_in=m_w_in, q_norm_g=m_q_norm_g, k_norm_g=m_k_norm_g, pool_w=m_pool_w,
             pool_scale=m_pool_scale, w_out=m_w_out, cross_norm_g=m_cross_norm_g, mem_norm_g=m_mem_norm_g,
             w_cq=m_w_cq, w_ckv=m_w_ckv, cq_norm_g=m_cq_norm_g, ck_norm_g=m_ck_norm_g, w_co=m_w_co,
             ffn_norm_g=m_ffn_norm_g, w_gate_up=m_w_gate_up, w_down=m_w_down)
    v = dict(mix_norm_g=v_mix_norm_g, w_in=v_w_in, q_norm_g=v_q_norm_g, k_norm_g=v_k_norm_g, pool_w=v_pool_w,
             pool_scale=v_pool_scale, w_out=v_w_out, cross_norm_g=v_cross_norm_g, mem_norm_g=v_mem_norm_g,
             w_cq=v_w_cq, w_ckv=v_w_ckv, cq_norm_g=v_cq_norm_g, ck_norm_g=v_ck_norm_g, w_co=v_w_co,
             ffn_norm_g=v_ffn_norm_g, w_gate_up=v_w_gate_up, w_down=v_w_down)

    gathered = _gather_weights([_bf(w[k][0]) for k in BIG])
    wf = dict(zip(BIG, gathered))
    sm = {k: (w[k][0] if k == "pool_w" else w[k]) for k in SMALL}

    loss_part, dx, gbig, gsmall = _local_step(x[0], mem[0], positions[0], loss_target[0], wf, sm)
    loss = lax.psum(loss_part, ("x", "y", "c"))

    c_arr = lax.axis_index("c").astype(jnp.int32).reshape(1)
    k_arr = (2 * lax.axis_index("x") + lax.axis_index("y")).astype(jnp.int32).reshape(1)
    glist = [gbig[k] for k in BIG]
    from_sibling = _swap_core_halves(glist)
    chip_sums = [_add_core_halves(f"add_halves_{k}", g, t, c_arr) for k, g, t in zip(BIG, glist, from_sibling)]
    from_chips = _exchange_chip_sums(chip_sums)
    halves = [_sum_chips(f"sum_chips_{k}", s, t, k_arr) for k, s, t in zip(BIG, chip_sums, from_chips)]
    gshard = dict(zip(BIG, _join_core_halves(halves)))

    gsm = _unpack_small(_allreduce_small(_pack_small(gsmall)), sm)

    grads, deltas, new_m, new_v = {}, {}, {}, {}
    for k in BIG:
        shard = w[k][0]
        tr = shard.shape[0] // 4
        d, nm, nv = _adamw(f"adamw_{k}", shard, gshard[k], m[k][0], v[k][0], tr=tr)
        grads[k], deltas[k], new_m[k], new_v[k] = gshard[k][None], d[None], nm[None], nv[None]
    smw = {k: (w[k][0] if k == "pool_w" else w[k]) for k in SMALL}
    smm = {k: (m[k][0] if k == "pool_w" else m[k]) for k in SMALL}
    smv = {k: (v[k][0] if k == "pool_w" else v[k]) for k in SMALL}
    pw, pg, pm, pv = _pack_small(smw), _pack_small(gsm), _pack_small(smm), _pack_small(smv)
    d, nm, nv = _adamw("adamw_small", pw, pg, pm, pv, tr=pw.shape[0])
    for dst, packed in ((deltas, d), (new_m, nm), (new_v, nv)):
        un = _unpack_small(packed, sm)
        for k in SMALL:
            dst[k] = un[k].reshape(w[k].shape)
    for k in SMALL:
        grads[k] = gsm[k].reshape(w[k].shape)

    return (loss, dx[None], *[grads[k] for k in WEIGHTS], *[deltas[k] for k in WEIGHTS],
            *[new_m[k] for k in WEIGHTS], *[new_v[k] for k in WEIGHTS])
```

```python
import functools

import jax
import jax.numpy as jnp
from jax import lax
from jax.experimental import pallas as pl
from jax.experimental.pallas import tpu as pltpu

F32 = jnp.float32
BF16 = jnp.bfloat16
MESH = pl.DeviceIdType.MESH
ANY = pl.BlockSpec(memory_space=pl.ANY)

D_MODEL = 1024
HEAD = 128
N_HEADS = 4
DILATIONS = (1, 4, 16)
BLOCK = 128
Q_W = 1536
KV_W = 512
POOL_W = 512
POOL_WINDOWS = (2, 4, 8, 16)
POOL_HALO = 16
IN_W = 3072
ROT_DIM = 32
ROT_HALF = 16
ROPE_THETA = 500000.0
X_W = 512
D_FF = 2816
EPS = 1e-6
NEG_INF = -1e30
SCALE = HEAD ** -0.5
N_CHIPS = 4

ADAM_LR = 0.001
ADAM_B1 = 0.9
ADAM_B2 = 0.999
ADAM_EPS = 1e-08
ADAM_WD = 0.01
ADAM_STEP = 10

VMEM_BYTES_V7X = 64 * 2 ** 20
VMEM_LIMIT_MAX = 56 * 2 ** 20
VMEM_LIMIT_MIN = 24 * 2 ** 20


def _nbytes(shape, dtype):
    n = 1
    for s in shape:
        n *= s
    return n * jnp.dtype(dtype).itemsize


def _cparams(n_axes, block_bytes, scratch_bytes=0):
    est = 2 * (2 * block_bytes + scratch_bytes)
    lim = int(min(VMEM_LIMIT_MAX, max(VMEM_LIMIT_MIN, est)))
    return pltpu.CompilerParams(dimension_semantics=("arbitrary",) * n_axes, vmem_limit_bytes=lim)


def _bf(v):
    return v.astype(BF16)


def _dot(a, b):
    return jnp.dot(a, b, preferred_element_type=F32)


def _dot_nt(a, b):
    return lax.dot_general(a, b, (((1,), (1,)), ((), ())), preferred_element_type=F32)


def _dot_tn(a, b):
    return lax.dot_general(a, b, (((0,), (0,)), ((), ())), preferred_element_type=F32)


def _rstd(v):
    return lax.rsqrt(jnp.mean(v * v, axis=-1, keepdims=True) + EPS)


def _norm_bwd(dy, xv, g):
    r = _rstd(xv)
    xh = xv * r
    dxh = dy * g
    dx = r * (dxh - xh * jnp.mean(dxh * xh, axis=-1, keepdims=True))
    return dx, xh


def _mm_nn(name, a, w3, *, tm, norm_g=None, residual=None, out_dtype=F32):
    M, K = a.shape
    P, Kw, C = w3.shape
    assert Kw == K and M % tm == 0
    N = P * C
    has_norm = norm_g is not None
    has_res = residual is not None

    def body(*refs):
        refs = list(refs)
        a_ref = refs.pop(0)
        w_ref = refs.pop(0)
        g_ref = refs.pop(0) if has_norm else None
        r_ref = refs.pop(0) if has_res else None
        o_ref = refs.pop(0)
        xn_ref = refs.pop(0) if has_norm else None
        if has_norm:
            av = a_ref[...].astype(F32)
            ab = _bf(av * _rstd(av) * g_ref[...])
            xn_ref[...] = ab
        else:
            ab = _bf(a_ref[...])
        for p in range(P):
            acc = _dot(ab, w_ref[p])
            if has_res:
                acc = acc + r_ref[:, p * C:(p + 1) * C]
            o_ref[:, p * C:(p + 1) * C] = acc.astype(o_ref.dtype)

    in_specs = [pl.BlockSpec((tm, K), lambda i: (i, 0)), pl.BlockSpec((P, K, C), lambda i: (0, 0, 0))]
    args = [a, w3]
    if has_norm:
        in_specs.append(pl.BlockSpec((1, K), lambda i: (0, 0)))
        args.append(norm_g)
    if has_res:
        in_specs.append(pl.BlockSpec((tm, N), lambda i: (i, 0)))
        args.append(residual)
    out_shape = [jax.ShapeDtypeStruct((M, N), out_dtype)]
    out_specs = [pl.BlockSpec((tm, N), lambda i: (i, 0))]
    if has_norm:
        out_shape.append(jax.ShapeDtypeStruct((M, K), BF16))
        out_specs.append(pl.BlockSpec((tm, K), lambda i: (i, 0)))
    blk = (_nbytes((tm, K), a.dtype) + _nbytes((P, K, C), BF16) + 2 * _nbytes((tm, N), F32)
           + _nbytes((tm, K), BF16))
    res = pl.pallas_call(
        body, name=name, grid=(M // tm,), in_specs=in_specs, out_specs=out_specs,
        out_shape=out_shape, compiler_params=_cparams(1, blk))(*args)
    return res if has_norm else res[0]


def _ffn_up(name, h, wgu3, g, *, tm):
    M, K = h.shape
    P, _, C = wgu3.shape
    half = P // 2

    def body(h_ref, wg_ref, wu_ref, g_ref, act_ref, gu_ref, xn_ref, ab_ref):
        @pl.when(pl.program_id(1) == 0)
        def _():
            hv = h_ref[...]
            xn = _bf(hv * _rstd(hv) * g_ref[...])
            ab_ref[...] = xn
            xn_ref[...] = xn
        ab = ab_ref[...]
        gate = _dot(ab, wg_ref[...])
        up = _dot(ab, wu_ref[...])
        act_ref[...] = _bf(gate * (1.0 / (1.0 + jnp.exp(-gate))) * up)
        gu_ref[0] = _bf(gate)
        gu_ref[1] = _bf(up)

    blk = (_nbytes((tm, K), F32) + 2 * _nbytes((K, C), BF16) + 3 * _nbytes((tm, C), BF16)
           + 2 * _nbytes((tm, C), F32) + _nbytes((tm, K), BF16))
    return pl.pallas_call(
        body, name=name, grid=(M // tm, half),
        in_specs=[pl.BlockSpec((tm, K), lambda i, j: (i, 0)),
                  pl.BlockSpec((None, K, C), lambda i, j: (j, 0, 0)),
                  pl.BlockSpec((None, K, C), lambda i, j: (j + half, 0, 0)),
                  pl.BlockSpec((1, K), lambda i, j: (0, 0))],
        out_specs=[pl.BlockSpec((tm, C), lambda i, j: (i, j)),
                   pl.BlockSpec((2, tm, C), lambda i, j: (0, i, j)),
                   pl.BlockSpec((tm, K), lambda i, j: (i, 0))],
        out_shape=[jax.ShapeDtypeStruct((M, half * C), BF16),
                   jax.ShapeDtypeStruct((2, M, half * C), BF16),
                   jax.ShapeDtypeStruct((M, K), BF16)],
        scratch_shapes=[pltpu.VMEM((tm, K), BF16)],
        compiler_params=_cparams(2, blk, _nbytes((tm, K), BF16)))(h, wgu3, wgu3, g)


def _ffn_down_loss(name, act, wd3, h, target, *, tm):
    M, K = act.shape
    _, _, N = wd3.shape

    def body(a_ref, w_ref, h_ref, t_ref, dy_ref, ls_ref):
        err = _dot(a_ref[...], w_ref[...]) + h_ref[...] - t_ref[...]
        dy_ref[...] = err * (1.0 / N)

        @pl.when(pl.program_id(0) == 0)
        def _():
            ls_ref[...] = jnp.zeros_like(ls_ref)
        ls_ref[...] += jnp.sum(err * err, axis=0, keepdims=True)

    blk = _nbytes((tm, K), BF16) + _nbytes((K, N), BF16) + 4 * _nbytes((tm, N), F32)
    return pl.pallas_call(
        body, name=name, grid=(M // tm,),
        in_specs=[pl.BlockSpec((tm, K), lambda i: (i, 0)),
                  pl.BlockSpec((None, K, N), lambda i: (0, 0, 0)),
                  pl.BlockSpec((tm, N), lambda i: (i, 0)),
                  pl.BlockSpec((tm, N), lambda i: (i, 0))],
        out_specs=[pl.BlockSpec((tm, N), lambda i: (i, 0)),
                   pl.BlockSpec((1, N), lambda i: (0, 0))],
        out_shape=[jax.ShapeDtypeStruct((M, N), F32), jax.ShapeDtypeStruct((1, N), F32)],
        compiler_params=_cparams(1, blk))(act, wd3, h, target)


def _nt_pieces(a_ref, w_ref):
    P, _, C = w_ref.shape
    acc = _dot_nt(_bf(a_ref[:, 0:C]), w_ref[0])
    for p in range(1, P):
        acc = acc + _dot_nt(_bf(a_ref[:, p * C:(p + 1) * C]), w_ref[p])
    return acc


def _mm_nt(name, a, w3, *, tm, out_dtype):
    M, N = a.shape
    P, Ko, C = w3.shape
    assert N == P * C and M % tm == 0

    def body(a_ref, w_ref, o_ref):
        o_ref[...] = _nt_pieces(a_ref, w_ref).astype(o_ref.dtype)

    blk = _nbytes((tm, N), a.dtype) + _nbytes((P, Ko, C), BF16) + 2 * _nbytes((tm, Ko), F32)
    return pl.pallas_call(
        body, name=name, grid=(M // tm,),
        in_specs=[pl.BlockSpec((tm, N), lambda i: (i, 0)), pl.BlockSpec((P, Ko, C), lambda i: (0, 0, 0))],
        out_specs=pl.BlockSpec((tm, Ko), lambda i: (i, 0)),
        out_shape=jax.ShapeDtypeStruct((M, Ko), out_dtype),
        compiler_params=_cparams(1, blk))(a, w3)


def _mm_nt_normbwd(name, a, w3, h, g, dres, *, tm):
    M, D = h.shape
    P, Ko, C = w3.shape
    N = a.shape[1]
    assert N == P * C and Ko == D and M % tm == 0
    has_res = dres is not None

    def body(*refs):
        a_ref, w_ref, h_ref, g_ref = refs[:4]
        r_ref = refs[4] if has_res else None
        dx_ref, dg_ref = refs[-2:]
        dhn = _nt_pieces(a_ref, w_ref)
        dx, xh = _norm_bwd(dhn, h_ref[...], g_ref[...])
        if has_res:
            dx = dx + r_ref[...]
        dx_ref[...] = dx

        @pl.when(pl.program_id(0) == 0)
        def _():
            dg_ref[...] = jnp.zeros_like(dg_ref)
        dg_ref[...] += jnp.sum(dhn * xh, axis=0, keepdims=True)

    row = pl.BlockSpec((tm, D), lambda i: (i, 0))
    vec = pl.BlockSpec((1, D), lambda i: (0, 0))
    in_specs = [pl.BlockSpec((tm, N), lambda i: (i, 0)), pl.BlockSpec((P, Ko, C), lambda i: (0, 0, 0)), row, vec]
    args = [a, w3, h, g]
    if has_res:
        in_specs.append(row)
        args.append(dres)
    blk = _nbytes((tm, N), a.dtype) + _nbytes((P, Ko, C), BF16) + 5 * _nbytes((tm, D), F32)
    return pl.pallas_call(
        body, name=name, grid=(M // tm,), in_specs=in_specs, out_specs=[row, vec],
        out_shape=[jax.ShapeDtypeStruct((M, D), F32), jax.ShapeDtypeStruct((1, D), F32)],
        compiler_params=_cparams(1, blk))(*args)


def _mm_nt_normbwd_steps(name, a, a_spec, a_blk, w3, w_spec, w_blk, nsteps, h, g, dres, *, tm):
    M, D = h.shape
    has_res = dres is not None

    def body(*refs):
        refs = list(refs)
        a_ref, w_ref, h_ref, g_ref = refs[:4]
        r_ref = refs[4] if has_res else None
        dx_ref, dg_ref, acc_ref = refs[-3:]
        i = pl.program_id(0)
        p = pl.program_id(1)

        @pl.when(p == 0)
        def _():
            acc_ref[...] = jnp.zeros_like(acc_ref)
        acc_ref[...] += _dot_nt(_bf(a_ref[...]), w_ref[...])

        @pl.when(p == nsteps - 1)
        def _():
            dhn = acc_ref[...]
            dx, xh = _norm_bwd(dhn, h_ref[...], g_ref[...])
            if has_res:
                dx = dx + r_ref[...]
            dx_ref[...] = dx

            @pl.when(i == 0)
            def _():
                dg_ref[...] = jnp.zeros_like(dg_ref)
            dg_ref[...] += jnp.sum(dhn * xh, axis=0, keepdims=True)

    in_specs = [a_spec, w_spec, pl.BlockSpec((tm, D), lambda i, p: (i, 0)),
                pl.BlockSpec((1, D), lambda i, p: (0, 0))]
    args = [a, w3, h, g]
    if has_res:
        in_specs.append(pl.BlockSpec((tm, D), lambda i, p: (i, 0)))
        args.append(dres)
    blk = _nbytes(a_blk, a.dtype) + _nbytes(w_blk, BF16) + 5 * _nbytes((tm, D), F32)
    return pl.pallas_call(
        body, name=name, grid=(M // tm, nsteps), in_specs=in_specs,
        out_specs=[pl.BlockSpec((tm, D), lambda i, p: (i, 0)), pl.BlockSpec((1, D), lambda i, p: (0, 0))],
        out_shape=[jax.ShapeDtypeStruct((M, D), F32), jax.ShapeDtypeStruct((1, D), F32)],
        scratch_shapes=[pltpu.VMEM((tm, D), F32)],
        compiler_params=_cparams(2, blk, _nbytes((tm, D), F32)))(*args)


def _ffn_down_bwd(name, dy, wd3, gu, *, tm, tn):
    M, D = dy.shape
    _, F, _ = wd3.shape

    def body(dy_ref, w_ref, gu_ref, o_ref, ab_ref):
        @pl.when(pl.program_id(1) == 0)
        def _():
            ab_ref[...] = _bf(dy_ref[...])
        da = _dot_nt(ab_ref[...], w_ref[...])
        gate = gu_ref[0].astype(F32)
        up = gu_ref[1].astype(F32)
        sig = 1.0 / (1.0 + jnp.exp(-gate))
        silu = gate * sig
        o_ref[0] = _bf(da * up * (sig + silu * (1.0 - sig)))
        o_ref[1] = _bf(da * silu)

    blk = (_nbytes((tm, D), F32) + _nbytes((tn, D), BF16) + 4 * _nbytes((tm, tn), BF16)
           + 4 * _nbytes((tm, tn), F32))
    return pl.pallas_call(
        body, name=name, grid=(M // tm, F // tn),
        in_specs=[pl.BlockSpec((tm, D), lambda i, j: (i, 0)),
                  pl.BlockSpec((None, tn, D), lambda i, j: (0, j, 0)),
                  pl.BlockSpec((2, tm, tn), lambda i, j: (0, i, j))],
        out_specs=pl.BlockSpec((2, tm, tn), lambda i, j: (0, i, j)),
        out_shape=jax.ShapeDtypeStruct((2, M, F), BF16),
        scratch_shapes=[pltpu.VMEM((tm, D), BF16)],
        compiler_params=_cparams(2, blk, _nbytes((tm, D), BF16)))(dy, wd3, gu)


def _mm_tn(name, a, a_spec, a_blk, b, b_spec, b_blk, out3, o_spec, o_blk, grid):
    def body(a_ref, b_ref, o_ref):
        @pl.when(pl.program_id(2) == 0)
        def _():
            o_ref[...] = jnp.zeros_like(o_ref)
        o_ref[...] += _dot_tn(_bf(a_ref[...]), _bf(b_ref[...]))

    blk = _nbytes(a_blk, a.dtype) + _nbytes(b_blk, b.dtype) + 2 * _nbytes(o_blk, F32)
    return pl.pallas_call(
        body, name=name, grid=grid, in_specs=[a_spec, b_spec], out_specs=o_spec,
        out_shape=jax.ShapeDtypeStruct(out3, F32),
        compiler_params=_cparams(3, blk))(a, b)


def _swap_halves(v, lane, masked):
    up = pltpu.roll(v, HEAD - ROT_HALF, 1)
    down = pltpu.roll(v, ROT_HALF, 1)
    rest = jnp.where(lane < ROT_DIM, down, 0.0) if masked else down
    return jnp.where(lane < ROT_HALF, up, rest)


def _qk_prep(name, proj, ctab, stab, qg, kg, *, tm):
    S = proj.shape[0]
    width = Q_W + KV_W

    def body(p_ref, c_ref, s_ref, qg_ref, kg_ref, q_ref, k_ref):
        lane = lax.broadcasted_iota(jnp.int32, (tm, HEAD), 1)
        cv = c_ref[...]
        sv = s_ref[...]

        def prep(t, g, scale):
            n = t * _rstd(t) * g
            return (n * cv + _swap_halves(n, lane, False) * sv) * scale

        for j in range(Q_W // HEAD):
            q_ref[:, j * HEAD:(j + 1) * HEAD] = prep(p_ref[:, j * HEAD:(j + 1) * HEAD], qg_ref[...], SCALE)
        for j in range(KV_W // HEAD):
            k_ref[:, j * HEAD:(j + 1) * HEAD] = prep(p_ref[:, Q_W + j * HEAD:Q_W + (j + 1) * HEAD], kg_ref[...], 1.0)

    blk = 2 * _nbytes((tm, width), F32) + 2 * _nbytes((tm, HEAD), F32)
    return pl.pallas_call(
        body, name=name, grid=(S // tm,),
        in_specs=[pl.BlockSpec((tm, width), lambda i: (i, 0)),
                  pl.BlockSpec((tm, HEAD), lambda i: (i, 0)),
                  pl.BlockSpec((tm, HEAD), lambda i: (i, 0)),
                  pl.BlockSpec((1, HEAD), lambda i: (0, 0)),
                  pl.BlockSpec((1, HEAD), lambda i: (0, 0))],
        out_specs=[pl.BlockSpec((tm, Q_W), lambda i: (i, 0)), pl.BlockSpec((tm, KV_W), lambda i: (i, 0))],
        out_shape=[jax.ShapeDtypeStruct((S, Q_W), F32), jax.ShapeDtypeStruct((S, KV_W), F32)],
        compiler_params=_cparams(1, blk))(proj, ctab, stab, qg, kg)


ATTN_STEP = 2048


def _row_idx(start, dil):
    return pl.ds(start, BLOCK) if dil == 1 else pl.ds(start, BLOCK, stride=dil)


def _rows(ref, start, dil):
    return ref[_row_idx(start, dil), :]


def _set_rows(ref, start, dil, val):
    ref[_row_idx(start, dil), :] = val


def _add_rows(ref, start, dil, val):
    idx = _row_idx(start, dil)
    ref[idx, :] = ref[idx, :] + val


def _band_mask(first):
    qi = lax.broadcasted_iota(jnp.int32, (BLOCK, 2 * BLOCK), 0)
    kj = lax.broadcasted_iota(jnp.int32, (BLOCK, 2 * BLOCK), 1)
    band = (kj >= qi) & (kj <= qi + BLOCK)
    if first is None:
        return band
    return band & ((kj >= BLOCK) | jnp.logical_not(first))


def _attn_geometry(S, grp):
    dil = DILATIONS[grp]
    bt = BLOCK * dil
    assert S % ATTN_STEP == 0 and ATTN_STEP % bt == 0
    return dil, bt, ATTN_STEP // bt, S // ATTN_STEP


def _attn_fwd(name, qn, kn, proj, grp):
    S = qn.shape[0]
    dil, bt, nsub, nsb = _attn_geometry(S, grp)
    vcol = (Q_W + KV_W) // HEAD

    def body(q_ref, kp_ref, kc_ref, vp_ref, vc_ref, o_ref, l_ref):
        valid_first = _band_mask(pl.program_id(1) == 0)
        valid_inner = _band_mask(None)
        for b in range(nsub):
            for r in range(dil):
                at = b * bt + r
                if b == 0:
                    kprev, vprev, valid = _rows(kp_ref, r, dil), _rows(vp_ref, r, dil), valid_first
                else:
                    kprev, vprev, valid = _rows(kc_ref, at - bt, dil), _rows(vc_ref, at - bt, dil), valid_inner
                q = _bf(_rows(q_ref, at, dil))
                k2 = _bf(jnp.concatenate([kprev, _rows(kc_ref, at, dil)], axis=0))
                v2 = _bf(jnp.concatenate([vprev, _rows(vc_ref, at, dil)], axis=0))
                s = jnp.where(valid, _dot_nt(q, k2), NEG_INF)
                m = jnp.max(s, axis=-1, keepdims=True)
                p = jnp.exp(s - m)
                l = jnp.sum(p, axis=-1, keepdims=True)
                acc = _dot(_bf(p), v2)
                _set_rows(o_ref, at, dil, acc / l)
                _set_rows(l_ref, at, dil, jnp.broadcast_to(m + jnp.log(l), (BLOCK, HEAD)))

    prev = lambda n: jnp.maximum(n * nsub - 1, 0)
    big = lambda col: pl.BlockSpec((ATTN_STEP, HEAD), lambda h, n: (n, col(h)))
    tail = lambda col: pl.BlockSpec((bt, HEAD), lambda h, n: (prev(n), col(h)))
    blk = 5 * _nbytes((ATTN_STEP, HEAD), F32) + 2 * _nbytes((bt, HEAD), F32)
    return pl.pallas_call(
        body, name=name, grid=(N_HEADS, nsb),
        in_specs=[big(lambda h: grp * N_HEADS + h), tail(lambda h: h), big(lambda h: h),
                  tail(lambda h: vcol + h), big(lambda h: vcol + h)],
        out_specs=[big(lambda h: h), big(lambda h: h)],
        out_shape=[jax.ShapeDtypeStruct((S, KV_W), F32), jax.ShapeDtypeStruct((S, KV_W), F32)],
        compiler_params=_cparams(2, blk))(qn, kn, kn, proj, proj)


def _attn_bwd(name, qn, kn, proj, dattn, lse, delta, grp):
    S = qn.shape[0]
    dil, bt, nsub, nsb = _attn_geometry(S, grp)
    vcol = (Q_W + KV_W) // HEAD

    def body(q_ref, kp_ref, kc_ref, vp_ref, vc_ref, do_ref, l_ref, d_ref, dq_ref, dk_ref, dv_ref, ck_ref, cv_ref):
        n = pl.program_id(1)
        par = n % 2

        @pl.when(n < nsb)
        def _():
            valid_first = _band_mask(n == 0)
            valid_inner = _band_mask(None)
            ck, cv = ck_ref.at[par], cv_ref.at[par]
            pk, pv = ck_ref.at[1 - par], cv_ref.at[1 - par]
            for b in range(nsub):
                for r in range(dil):
                    at = b * bt + r
                    if b == 0:
                        kprev, vprev, valid = _rows(kp_ref, r, dil), _rows(vp_ref, r, dil), valid_first
                    else:
                        kprev, vprev, valid = _rows(kc_ref, at - bt, dil), _rows(vc_ref, at - bt, dil), valid_inner
                    q = _bf(_rows(q_ref, at, dil))
                    k2 = _bf(jnp.concatenate([kprev, _rows(kc_ref, at, dil)], axis=0))
                    v2 = _bf(jnp.concatenate([vprev, _rows(vc_ref, at, dil)], axis=0))
                    do = _bf(_rows(do_ref, at, dil))
                    lse_r = _rows(l_ref, at, dil)[:, :1]
                    del_r = _rows(d_ref, at, dil)[:, :1]
                    s = jnp.where(valid, _dot_nt(q, k2), NEG_INF)
                    p = jnp.exp(s - lse_r)
                    ds = _bf(p * (_dot_nt(do, v2) - del_r))
                    _set_rows(dq_ref, at, dil, _dot(ds, k2))
                    dk2 = _dot_tn(ds, q)
                    dv2 = _dot_tn(_bf(p), do)
                    _set_rows(ck, at, dil, dk2[BLOCK:])
                    _set_rows(cv, at, dil, dv2[BLOCK:])
                    if b > 0:
                        _add_rows(ck, at - bt, dil, dk2[:BLOCK])
                        _add_rows(cv, at - bt, dil, dv2[:BLOCK])
                    else:
                        @pl.when(n > 0)
                        def _():
                            _add_rows(pk, (nsub - 1) * bt + r, dil, dk2[:BLOCK])
                            _add_rows(pv, (nsub - 1) * bt + r, dil, dv2[:BLOCK])

        @pl.when(n > 0)
        def _():
            dk_ref[...] = ck_ref[1 - par]
            dv_ref[...] = cv_ref[1 - par]

    cur = lambda n: jnp.minimum(n, nsb - 1)
    prev = lambda n: jnp.maximum(n - 1, 0)
    tail_at = lambda n: jnp.maximum(cur(n) * nsub - 1, 0)
    big = lambda col: pl.BlockSpec((ATTN_STEP, HEAD), lambda h, n: (cur(n), col(h)))
    tail = lambda col: pl.BlockSpec((bt, HEAD), lambda h, n: (tail_at(n), col(h)))
    late = pl.BlockSpec((ATTN_STEP, HEAD), lambda h, n: (prev(n), h))
    blk = 9 * _nbytes((ATTN_STEP, HEAD), F32) + 2 * _nbytes((bt, HEAD), F32)
    return pl.pallas_call(
        body, name=name, grid=(N_HEADS, nsb + 1),
        in_specs=[big(lambda h: grp * N_HEADS + h), tail(lambda h: h), big(lambda h: h),
                  tail(lambda h: vcol + h), big(lambda h: vcol + h),
                  big(lambda h: h), big(lambda h: h), big(lambda h: h)],
        out_specs=[big(lambda h: h), late, late],
        out_shape=[jax.ShapeDtypeStruct((S, KV_W), F32)] * 3,
        scratch_shapes=[pltpu.VMEM((2, ATTN_STEP, HEAD), F32), pltpu.VMEM((2, ATTN_STEP, HEAD), F32)],
        compiler_params=_cparams(2, blk, 4 * _nbytes((ATTN_STEP, HEAD), F32)))(
            qn, kn, kn, proj, proj, dattn, lse, delta)


def _window_sum(v, n_doublings, back):
    rows = v.shape[0]
    step = 1
    for _ in range(n_doublings):
        v = v + pltpu.roll(v, step if back else rows - step, 0)
        step *= 2
    return v


def _mix_post(name, outs, lses, proj, pool_w, pool_scale, *, tm):
    S = proj.shape[0]
    ucol = (IN_W - POOL_W) // POOL_W
    hpt = tm // POOL_HALO

    def body(o0, o1, o2, l0, l1, l2, u_ref, uh_ref, pw_ref, ps_ref, mix_ref, lse_ref, dp_ref):
        i = pl.program_id(0)
        for h in range(N_HEADS):
            sl = slice(h * HEAD, (h + 1) * HEAD)
            ls = [l0[:, sl], l1[:, sl], l2[:, sl]]
            m = jnp.maximum(jnp.maximum(ls[0], ls[1]), ls[2])
            ws = [jnp.exp(v - m) for v in ls]
            den = ws[0] + ws[1] + ws[2]
            num = ws[0] * o0[:, sl] + ws[1] * o1[:, sl] + ws[2] * o2[:, sl]
            mix_ref[:, sl] = _bf(num / den)
            lse_ref[:, sl] = m + jnp.log(den)
        halo = jnp.where(i == 0, 0.0, uh_ref[...])
        t = lax.broadcasted_iota(jnp.int32, (tm + POOL_HALO, HEAD), 0) + (i * tm - POOL_HALO)
        for g, w in enumerate(POOL_WINDOWS):
            sl = slice(g * HEAD, (g + 1) * HEAD)
            ub = jnp.concatenate([halo[:, sl], u_ref[:, sl]], axis=0)
            cnt = jnp.minimum(t + 1, w).astype(F32)
            d = (_window_sum(ub, g + 1, True) / cnt - ub)[POOL_HALO:]
            db = _bf(d)
            dp_ref[:, sl] = db
            mix_ref[:, KV_W + g * HEAD:KV_W + (g + 1) * HEAD] = _bf(_dot(db, pw_ref[g]) * ps_ref[:, sl])

    tile = pl.BlockSpec((tm, KV_W), lambda i: (i, 0))
    blk = 9 * _nbytes((tm, KV_W), F32) + _nbytes((tm, 2 * KV_W), BF16)
    return pl.pallas_call(
        body, name=name, grid=(S // tm,),
        in_specs=[tile] * 6 + [
            pl.BlockSpec((tm, POOL_W), lambda i: (i, ucol)),
            pl.BlockSpec((POOL_HALO, POOL_W), lambda i: (jnp.maximum(i * hpt - 1, 0), ucol)),
            pl.BlockSpec((len(POOL_WINDOWS), HEAD, HEAD), lambda i: (0, 0, 0)),
            pl.BlockSpec((1, POOL_W), lambda i: (0, 0))],
        out_specs=[pl.BlockSpec((tm, 2 * KV_W), lambda i: (i, 0)), tile, tile],
        out_shape=[jax.ShapeDtypeStruct((S, 2 * KV_W), BF16), jax.ShapeDtypeStruct((S, KV_W), F32),
                   jax.ShapeDtypeStruct((S, POOL_W), BF16)],
        compiler_params=_cparams(1, blk))(*outs, *lses, proj, proj, pool_w, pool_scale)


def _mix_bwd(name, dmix, mix, dpool, pool_w, pool_scale, *, tm):
    S = dmix.shape[0]
    hpt = tm // POOL_HALO
    last_halo = S // POOL_HALO - 1
    n_tiles = S // tm

    def body(dm_ref, dh_ref, at_ref, dp_ref, pw_ref, ps_ref, da_ref, dl_ref, du_ref, gw_ref, gs_ref):
        i = pl.program_id(0)

        @pl.when(i == 0)
        def _():
            gw_ref[...] = jnp.zeros_like(gw_ref)
            gs_ref[...] = jnp.zeros_like(gs_ref)

        for h in range(N_HEADS):
            sl = slice(h * HEAD, (h + 1) * HEAD)
            da = dm_ref[:, sl]
            da_ref[:, sl] = da
            dl_ref[:, sl] = jnp.broadcast_to(
                jnp.sum(da * at_ref[:, sl].astype(F32), axis=-1, keepdims=True), (tm, HEAD))
        halo = jnp.where(i == n_tiles - 1, 0.0, dh_ref[...])
        t = lax.broadcasted_iota(jnp.int32, (tm + POOL_HALO, HEAD), 0) + i * tm
        for g, w in enumerate(POOL_WINDOWS):
            sl = slice(g * HEAD, (g + 1) * HEAD)
            dy = jnp.concatenate([dm_ref[:, KV_W + g * HEAD:KV_W + (g + 1) * HEAD], halo[:, sl]], axis=0)
            dys = _bf(dy * ps_ref[:, sl])
            dd = _dot_nt(dys, pw_ref[g])
            cnt = jnp.minimum(t + 1, w).astype(F32)
            du_ref[:, sl] = (_window_sum(dd / cnt, g + 1, False) - dd)[:tm]
            db = dp_ref[:, sl]
            gw_ref[g] += _dot_tn(db, dys[:tm])
            gs_ref[:, sl] += jnp.sum(dy[:tm] * _dot(db, pw_ref[g]), axis=0, keepdims=True)

    tile = pl.BlockSpec((tm, KV_W), lambda i: (i, 0))
    blk = _nbytes((tm, 2 * KV_W), F32) + 6 * _nbytes((tm, KV_W), F32)
    return pl.pallas_call(
        body, name=name, grid=(n_tiles,),
        in_specs=[pl.BlockSpec((tm, 2 * KV_W), lambda i: (i, 0)),
                  pl.BlockSpec((POOL_HALO, POOL_W), lambda i: (jnp.minimum((i + 1) * hpt, last_halo), 1)),
                  tile, tile,
                  pl.BlockSpec((len(POOL_WINDOWS), HEAD, HEAD), lambda i: (0, 0, 0)),
                  pl.BlockSpec((1, POOL_W), lambda i: (0, 0))],
        out_specs=[tile, tile, tile,
                   pl.BlockSpec((len(POOL_WINDOWS), HEAD, HEAD), lambda i: (0, 0, 0)),
                   pl.BlockSpec((1, POOL_W), lambda i: (0, 0))],
        out_shape=[jax.ShapeDtypeStruct((S, KV_W), F32)] * 3 + [
            jax.ShapeDtypeStruct((len(POOL_WINDOWS), HEAD, HEAD), F32), jax.ShapeDtypeStruct((1, POOL_W), F32)],
        compiler_params=_cparams(1, blk))(dmix, dmix, mix, dpool, pool_w, pool_scale)


def _qkv_bwd(name, dqs, dks, dvs, du, proj, ctab, stab, qg, kg, *, tm):
    S = proj.shape[0]
    width = Q_W + KV_W

    def body(dq0, dq1, dq2, dk0, dk1, dk2, dv0, dv1, dv2, du_ref, p_ref, c_ref, s_ref, qg_ref, kg_ref,
             dp_ref, gq_ref, gk_ref):
        @pl.when(pl.program_id(0) == 0)
        def _():
            gq_ref[...] = jnp.zeros_like(gq_ref)
            gk_ref[...] = jnp.zeros_like(gk_ref)

        lane = lax.broadcasted_iota(jnp.int32, (tm, HEAD), 1)
        cv = c_ref[...]
        sv = s_ref[...]

        def back(dy, t, g, scale):
            dy = dy * scale
            dn = dy * cv + _swap_halves(dy * sv, lane, True)
            dt, xh = _norm_bwd(dn, t, g)
            return dt, jnp.sum(dn * xh, axis=0, keepdims=True)

        dqr = (dq0, dq1, dq2)
        gq = jnp.zeros((1, HEAD), F32)
        for j in range(Q_W // HEAD):
            grp, h = divmod(j, N_HEADS)
            dt, gj = back(dqr[grp][:, h * HEAD:(h + 1) * HEAD], p_ref[:, j * HEAD:(j + 1) * HEAD], qg_ref[...], SCALE)
            dp_ref[:, j * HEAD:(j + 1) * HEAD] = _bf(dt)
            gq = gq + gj
        gq_ref[...] += gq
        gk = jnp.zeros((1, HEAD), F32)
        for h in range(N_HEADS):
            sl = slice(h * HEAD, (h + 1) * HEAD)
            dt, gj = back(dk0[:, sl] + dk1[:, sl] + dk2[:, sl], p_ref[:, Q_W + h * HEAD:Q_W + (h + 1) * HEAD],
                          kg_ref[...], 1.0)
            dp_ref[:, Q_W + h * HEAD:Q_W + (h + 1) * HEAD] = _bf(dt)
            gk = gk + gj
        gk_ref[...] += gk
        dp_ref[:, width:width + KV_W] = _bf(dv0[...] + dv1[...] + dv2[...])
        dp_ref[:, width + KV_W:] = _bf(du_ref[...])

    tile = pl.BlockSpec((tm, KV_W), lambda i: (i, 0))
    vec = pl.BlockSpec((1, HEAD), lambda i: (0, 0))
    rot = pl.BlockSpec((tm, HEAD), lambda i: (i, 0))
    blk = 10 * _nbytes((tm, KV_W), F32) + _nbytes((tm, width), F32) + _nbytes((tm, IN_W), BF16)
    return pl.pallas_call(
        body, name=name, grid=(S // tm,),
        in_specs=[tile] * 10 + [pl.BlockSpec((tm, width), lambda i: (i, 0)), rot, rot, vec, vec],
        out_specs=[pl.BlockSpec((tm, IN_W), lambda i: (i, 0)), vec, vec],
        out_shape=[jax.ShapeDtypeStruct((S, IN_W), BF16), jax.ShapeDtypeStruct((1, HEAD), F32),
                   jax.ShapeDtypeStruct((1, HEAD), F32)],
        compiler_params=_cparams(1, blk))(*dqs, *dks, *dvs, du, proj, ctab, stab, qg, kg)


def _cross_heads(q_ref, kv_ref, qg, kg, h):
    sl = slice(h * HEAD, (h + 1) * HEAD)
    qr = q_ref[:, sl]
    kr = kv_ref[:, sl]
    qh = qr * _rstd(qr) * qg * SCALE
    kh = kr * _rstd(kr) * kg
    vh = kv_ref[:, X_W + h * HEAD:X_W + (h + 1) * HEAD]
    return qr, _bf(qh), _bf(kh), _bf(vh)


def _cross_fwd(name, qraw, kv, qg, kg, *, tm):
    S = qraw.shape[0]
    M = kv.shape[0]

    def body(q_ref, kv_ref, qg_ref, kg_ref, o_ref):
        for h in range(N_HEADS):
            _, qh, kh, vh = _cross_heads(q_ref, kv_ref, qg_ref[...], kg_ref[...], h)
            s = _dot_nt(qh, kh)
            p = jnp.exp(s - jnp.max(s, axis=-1, keepdims=True))
            l = jnp.sum(p, axis=-1, keepdims=True)
            o_ref[:, h * HEAD:(h + 1) * HEAD] = _bf(_dot(_bf(p), vh) / l)

    vec = pl.BlockSpec((1, HEAD), lambda i: (0, 0))
    blk = 2 * _nbytes((tm, X_W), F32) + _nbytes((M, 2 * X_W), F32) + 4 * _nbytes((tm, M), F32)
    return pl.pallas_call(
        body, name=name, grid=(S // tm,),
        in_specs=[pl.BlockSpec((tm, X_W), lambda i: (i, 0)), pl.BlockSpec((M, 2 * X_W), lambda i: (0, 0)), vec, vec],
        out_specs=pl.BlockSpec((tm, X_W), lambda i: (i, 0)),
        out_shape=jax.ShapeDtypeStruct((S, X_W), BF16),
        compiler_params=_cparams(1, blk))(qraw, kv, qg, kg)


def _cross_bwd(name, do, qraw, kv, qg, kg, *, tm):
    S = qraw.shape[0]
    M = kv.shape[0]

    def body(do_ref, q_ref, kv_ref, qg_ref, kg_ref, dq_ref, dk_ref, dv_ref, gq_ref):
        @pl.when(pl.program_id(0) == 0)
        def _():
            dk_ref[...] = jnp.zeros_like(dk_ref)
            dv_ref[...] = jnp.zeros_like(dv_ref)
            gq_ref[...] = jnp.zeros_like(gq_ref)

        gq = jnp.zeros((1, HEAD), F32)
        for h in range(N_HEADS):
            sl = slice(h * HEAD, (h + 1) * HEAD)
            qr, qh, kh, vh = _cross_heads(q_ref, kv_ref, qg_ref[...], kg_ref[...], h)
            doh = _bf(do_ref[:, sl])
            s = _dot_nt(qh, kh)
            p = jnp.exp(s - jnp.max(s, axis=-1, keepdims=True))
            p = p / jnp.sum(p, axis=-1, keepdims=True)
            pb = _bf(p)
            dp = _dot_nt(doh, vh)
            ds = _bf(p * (dp - jnp.sum(dp * p, axis=-1, keepdims=True)))
            dv_ref[:, sl] += _dot_tn(pb, doh)
            dk_ref[:, sl] += _dot_tn(ds, qh)
            dn = _dot(ds, kh) * SCALE
            dt, xh = _norm_bwd(dn, qr, qg_ref[...])
            dq_ref[:, sl] = _bf(dt)
            gq = gq + jnp.sum(dn * xh, axis=0, keepdims=True)
        gq_ref[...] += gq

    vec = pl.BlockSpec((1, HEAD), lambda i: (0, 0))
    acc = pl.BlockSpec((M, X_W), lambda i: (0, 0))
    blk = 3 * _nbytes((tm, X_W), F32) + 3 * _nbytes((M, 2 * X_W), F32) + 6 * _nbytes((tm, M), F32)
    return pl.pallas_call(
        body, name=name, grid=(S // tm,),
        in_specs=[pl.BlockSpec((tm, X_W), lambda i: (i, 0)), pl.BlockSpec((tm, X_W), lambda i: (i, 0)),
                  pl.BlockSpec((M, 2 * X_W), lambda i: (0, 0)), vec, vec],
        out_specs=[pl.BlockSpec((tm, X_W), lambda i: (i, 0)), acc, acc, vec],
        out_shape=[jax.ShapeDtypeStruct((S, X_W), BF16), jax.ShapeDtypeStruct((M, X_W), F32),
                   jax.ShapeDtypeStruct((M, X_W), F32), jax.ShapeDtypeStruct((1, HEAD), F32)],
        compiler_params=_cparams(1, blk))(do, qraw, kv, qg, kg)


def _cross_kv_bwd(name, dkn, dv, kv, kg):
    M = kv.shape[0]

    def body(dk_ref, dv_ref, kv_ref, kg_ref, o_ref, g_ref):
        gk = jnp.zeros((1, HEAD), F32)
        for h in range(N_HEADS):
            sl = slice(h * HEAD, (h + 1) * HEAD)
            dn = dk_ref[:, sl]
            dt, xh = _norm_bwd(dn, kv_ref[:, sl], kg_ref[...])
            o_ref[:, sl] = _bf(dt)
            gk = gk + jnp.sum(dn * xh, axis=0, keepdims=True)
        o_ref[:, X_W:] = _bf(dv_ref[...])
        g_ref[...] = gk

    full = lambda shape: pl.BlockSpec(shape, lambda i: (0,) * len(shape))
    return pl.pallas_call(
        body, name=name, grid=(1,),
        in_specs=[full((M, X_W)), full((M, X_W)), full((M, 2 * X_W)), full((1, HEAD))],
        out_specs=[full((M, 2 * X_W)), full((1, HEAD))],
        out_shape=[jax.ShapeDtypeStruct((M, 2 * X_W), BF16), jax.ShapeDtypeStruct((1, HEAD), F32)],
        compiler_params=_cparams(1, 6 * _nbytes((M, 2 * X_W), F32)))(dkn, dv, kv, kg)


def _rope_tables(positions):
    inv_freq = ROPE_THETA ** (-jnp.arange(0, ROT_DIM, 2, dtype=F32) / ROT_DIM)
    ang = positions.astype(F32)[:, None] * inv_freq
    cos, sin = jnp.cos(ang), jnp.sin(ang)
    S = positions.shape[0]
    ctab = jnp.concatenate([cos, cos, jnp.ones((S, HEAD - ROT_DIM), F32)], axis=-1)
    stab = jnp.concatenate([-sin, sin, jnp.zeros((S, HEAD - ROT_DIM), F32)], axis=-1)
    return ctab, stab


def _local_step(x, mem, positions, target, wf, sm, *, tm=512):
    S, D = x.shape
    M = mem.shape[0]
    ctab, stab = _rope_tables(positions)
    w_in = wf["w_in"]
    w_out = wf["w_out"].reshape(1, 2 * KV_W, D)
    w_cq = wf["w_cq"].reshape(1, D, X_W)
    w_ckv = wf["w_ckv"].reshape(1, D, 2 * X_W)
    w_co = wf["w_co"]
    w_gu = wf["w_gate_up"]
    w_down = wf["w_down"].reshape(1, D_FF, D)
    pool_w_b = _bf(sm["pool_w"])
    cin = w_in.shape[2]
    cco = w_co.shape[2]
    cgu = w_gu.shape[2]

    proj, xn1 = _mm_nn("in_proj", x, w_in, tm=tm, norm_g=sm["mix_norm_g"])
    qn, kn = _qk_prep("qk_prep", proj, ctab, stab, sm["q_norm_g"], sm["k_norm_g"], tm=tm)
    outs, lses = [], []
    for grp in range(len(DILATIONS)):
        o, l = _attn_fwd(f"attn_fwd{grp}", qn, kn, proj, grp)
        outs.append(o)
        lses.append(l)
    mix, lse, dpool = _mix_post("mix_post", outs, lses, proj, pool_w_b, sm["pool_scale"], tm=tm)
    h1 = _mm_nn("out_proj", mix, w_out, tm=tm, residual=x)
    cq_raw, hn2 = _mm_nn("cq_proj", h1, w_cq, tm=tm, norm_g=sm["cross_norm_g"])
    kv, mem_n = _mm_nn("ckv_proj", mem, w_ckv, tm=M, norm_g=sm["mem_norm_g"])
    xo = _cross_fwd("cross_fwd", cq_raw, kv, sm["cq_norm_g"], sm["ck_norm_g"], tm=tm)
    h2 = _mm_nn("co_proj", xo, w_co, tm=tm, residual=h1)
    act, gu, hn3 = _ffn_up("ffn_up", h2, w_gu, sm["ffn_norm_g"], tm=tm)
    dy, lsum = _ffn_down_loss("ffn_down_loss", act, w_down, h2, target, tm=tm)
    loss = 0.5 * jnp.sum(lsum) / D

    nS = S // tm
    dgu = _ffn_down_bwd("ffn_down_bwd", dy, w_down, gu, tm=tm, tn=cgu)
    g_down = _mm_tn("g_w_down", act, pl.BlockSpec((tm, cgu), lambda r, c, s: (s, r)), (tm, cgu),
                    dy, pl.BlockSpec((tm, D), lambda r, c, s: (s, 0)), (tm, D),
                    (1, D_FF, D), pl.BlockSpec((None, cgu, D), lambda r, c, s: (0, r, 0)), (cgu, D),
                    (D_FF // cgu, 1, nS))
    dh2, g_ffn_norm = _mm_nt_normbwd_steps(
        "ffn_up_bwd", dgu, pl.BlockSpec((None, tm, cgu), lambda i, p: (p // 2, i, p % 2)), (tm, cgu),
        w_gu, pl.BlockSpec((None, D, cgu), lambda i, p: (p, 0, 0)), (D, cgu), 4, h2, sm["ffn_norm_g"], dy, tm=tm)
    g_gu = _mm_tn("g_w_gate_up", hn3, pl.BlockSpec((tm, D), lambda r, c, s: (s, 0)), (tm, D),
                  dgu, pl.BlockSpec((None, tm, cgu), lambda r, c, s: (c // 2, s, c % 2)), (tm, cgu),
                  (4, D, cgu), pl.BlockSpec((None, D, cgu), lambda r, c, s: (c, 0, 0)), (D, cgu),
                  (1, 4, nS))

    dxo = _mm_nt("co_proj_bwd", dh2, w_co, tm=tm, out_dtype=BF16)
    g_co = _mm_tn("g_w_co", xo, pl.BlockSpec((tm, X_W), lambda r, c, s: (s, 0)), (tm, X_W),
                  dh2, pl.BlockSpec((tm, cco), lambda r, c, s: (s, c)), (tm, cco),
                  (N_CHIPS, X_W, cco), pl.BlockSpec((None, X_W, cco), lambda r, c, s: (c, 0, 0)), (X_W, cco),
                  (1, N_CHIPS, nS))
    dcq, dkn, dvm, g_cq_norm = _cross_bwd("cross_bwd", dxo, cq_raw, kv, sm["cq_norm_g"], sm["ck_norm_g"], tm=tm)
    dkv, g_ck_norm = _cross_kv_bwd("cross_kv_bwd", dkn, dvm, kv, sm["ck_norm_g"])
    dh1, g_cross_norm = _mm_nt_normbwd("cq_proj_bwd", dcq, w_cq, h1, sm["cross_norm_g"], dh2, tm=tm)
    g_cq = _mm_tn("g_w_cq", hn2, pl.BlockSpec((tm, D), lambda r, c, s: (s, 0)), (tm, D),
                  dcq, pl.BlockSpec((tm, X_W), lambda r, c, s: (s, 0)), (tm, X_W),
                  (1, D, X_W), pl.BlockSpec((None, D, X_W), lambda r, c, s: (0, 0, 0)), (D, X_W), (1, 1, nS))
    _, g_mem_norm = _mm_nt_normbwd("ckv_proj_bwd", dkv, w_ckv, mem, sm["mem_norm_g"], None, tm=M)
    g_ckv = _mm_tn("g_w_ckv", mem_n, pl.BlockSpec((M, D), lambda r, c, s: (0, 0)), (M, D),
                   dkv, pl.BlockSpec((M, 2 * X_W), lambda r, c, s: (0, 0)), (M, 2 * X_W),
                   (1, D, 2 * X_W), pl.BlockSpec((None, D, 2 * X_W), lambda r, c, s: (0, 0, 0)), (D, 2 * X_W),
                   (1, 1, 1))

    dmix = _mm_nt("out_proj_bwd", dh1, w_out, tm=tm, out_dtype=F32)
    g_out = _mm_tn("g_w_out", mix, pl.BlockSpec((tm, 2 * KV_W), lambda r, c, s: (s, 0)), (tm, 2 * KV_W),
                   dh1, pl.BlockSpec((tm, D), lambda r, c, s: (s, 0)), (tm, D),
                   (1, 2 * KV_W, D), pl.BlockSpec((None, 2 * KV_W, D), lambda r, c, s: (0, 0, 0)), (2 * KV_W, D),
                   (1, 1, nS))
    dattn, delta, du, g_pool_w, g_pool_scale = _mix_bwd("mix_bwd", dmix, mix, dpool, pool_w_b, sm["pool_scale"], tm=tm)
    dqs, dks, dvs = [], [], []
    for grp in range(len(DILATIONS)):
        dq, dk, dv = _attn_bwd(f"attn_bwd{grp}", qn, kn, proj, dattn, lse, delta, grp)
        dqs.append(dq)
        dks.append(dk)
        dvs.append(dv)
    dproj, g_q_norm, g_k_norm = _qkv_bwd("qkv_bwd", dqs, dks, dvs, du, proj, ctab, stab,
                                         sm["q_norm_g"], sm["k_norm_g"], tm=tm)
    dx, g_mix_norm = _mm_nt_normbwd("in_proj_bwd", dproj, w_in, x, sm["mix_norm_g"], dh1, tm=tm)
    g_in = _mm_tn("g_w_in", xn1, pl.BlockSpec((tm, D), lambda r, c, s: (s, 0)), (tm, D),
                  dproj, pl.BlockSpec((tm, cin), lambda r, c, s: (s, c)), (tm, cin),
                  (N_CHIPS, D, cin), pl.BlockSpec((None, D, cin), lambda r, c, s: (c, 0, 0)), (D, cin),
                  (1, N_CHIPS, nS))

    big = {
        "w_in": g_in,
        "w_out": g_out.reshape(N_CHIPS, 2 * KV_W // N_CHIPS, D),
        "w_cq": g_cq.reshape(N_CHIPS, D // N_CHIPS, X_W),
        "w_ckv": g_ckv.reshape(N_CHIPS, D // N_CHIPS, 2 * X_W),
        "w_co": g_co,
        "w_gate_up": g_gu,
        "w_down": g_down.reshape(N_CHIPS, D_FF // N_CHIPS, D),
    }
    small = {
        "mix_norm_g": g_mix_norm, "q_norm_g": g_q_norm, "k_norm_g": g_k_norm, "pool_w": g_pool_w,
        "pool_scale": g_pool_scale, "cross_norm_g": g_cross_norm, "mem_norm_g": g_mem_norm,
        "cq_norm_g": g_cq_norm, "ck_norm_g": g_ck_norm, "ffn_norm_g": g_ffn_norm,
    }
    return loss, dx, big, small


BIG = ("w_in", "w_out", "w_cq", "w_ckv", "w_co", "w_gate_up", "w_down")
SMALL = ("mix_norm_g", "q_norm_g", "k_norm_g", "pool_w", "pool_scale", "cross_norm_g", "mem_norm_g",
         "cq_norm_g", "ck_norm_g", "ffn_norm_g")
WEIGHTS = ("mix_norm_g", "w_in", "q_norm_g", "k_norm_g", "pool_w", "pool_scale", "w_out", "cross_norm_g",
           "mem_norm_g", "w_cq", "w_ckv", "cq_norm_g", "ck_norm_g", "w_co", "ffn_norm_g", "w_gate_up", "w_down")


def _place():
    x, y, c = lax.axis_index("x"), lax.axis_index("y"), lax.axis_index("c")
    other_chips = [(1 - x, y), (x, 1 - y), (1 - x, 1 - y)]
    return x, y, c, other_chips


def _cast_piece(name, w, k_arr):
    R, C = w.shape
    hr = R // 2

    def body(k_ref, w_ref, o_ref):
        o_ref[...] = _bf(w_ref[...])

    return pl.pallas_call(
        body, name=name,
        grid_spec=pltpu.PrefetchScalarGridSpec(
            num_scalar_prefetch=1, grid=(2,),
            in_specs=[pl.BlockSpec((hr, C), lambda i, k: (i, 0))],
            out_specs=pl.BlockSpec((None, hr, C), lambda i, k: (k[0], i, 0))),
        out_shape=jax.ShapeDtypeStruct((N_CHIPS, R, C), BF16),
        compiler_params=_cparams(1, 2 * _nbytes((hr, C), F32)))(k_arr, w)


def _gather_weights(bufs):
    n = len(bufs)
    shapes = [b.shape[1:] for b in bufs]

    def body(*refs):
        outs = refs[n:2 * n]
        send, recv = refs[2 * n:]
        x, y, c, chips = _place()
        me = 2 * x + y
        sib = (x, y, 1 - c)

        def half(i, piece, which):
            hr = shapes[i][0] // 2
            return outs[i].at[piece, pl.ds(which * hr, hr), :]

        def copy(i, k, ref, to):
            return pltpu.make_async_remote_copy(src_ref=ref, dst_ref=ref, send_sem=send.at[i, k], recv_sem=recv.at[i, k],
                                                device_id=to, device_id_type=MESH)

        started = []
        for i in range(n):
            for j, (cx, cy) in enumerate(chips):
                cp = copy(i, j, half(i, me, c), (cx, cy, c))
                cp.start()
                started.append(cp)
        for i in range(n):
            for j, (cx, cy) in enumerate(chips):
                piece = half(i, 2 * cx + cy, c)
                copy(i, j, piece, (cx, cy, c)).wait_recv()
                fw = copy(i, 3 + j, piece, sib)
                fw.start()
                started.append(fw)
        for i in range(n):
            for j, (cx, cy) in enumerate(chips):
                copy(i, 3 + j, half(i, 2 * cx + cy, 1 - c), sib).wait_recv()
        for cp in started:
            cp.wait_send()

    return pl.pallas_call(
        body, name="gather_weights", in_specs=[ANY] * n, out_specs=[ANY] * n,
        out_shape=[jax.ShapeDtypeStruct(b.shape, BF16) for b in bufs],
        input_output_aliases={i: i for i in range(n)},
        scratch_shapes=[pltpu.SemaphoreType.DMA((n, 6)), pltpu.SemaphoreType.DMA((n, 6))],
    )(*bufs)


def _swap_core_halves(grads):
    n = len(grads)
    shapes = [g.shape for g in grads]

    def body(*refs):
        ins, outs = refs[:n], refs[n:2 * n]
        send, recv = refs[2 * n:]
        x, y, c, _ = _place()
        cps = []
        for i in range(n):
            hr = shapes[i][1] // 2
            cp = pltpu.make_async_remote_copy(
                src_ref=ins[i].at[:, pl.ds((1 - c) * hr, hr), :], dst_ref=outs[i], send_sem=send.at[i],
                recv_sem=recv.at[i], device_id=(x, y, 1 - c), device_id_type=MESH)
            cp.start()
            cps.append(cp)
        for cp in cps:
            cp.wait()

    return pl.pallas_call(
        body, name="swap_core_halves", in_specs=[ANY] * n, out_specs=[ANY] * n,
        out_shape=[jax.ShapeDtypeStruct((s[0], s[1] // 2, s[2]), F32) for s in shapes],
        scratch_shapes=[pltpu.SemaphoreType.DMA((n,)), pltpu.SemaphoreType.DMA((n,))],
    )(*grads)


def _add_core_halves(name, g, t, c_arr):
    P, R, C = g.shape
    hr = R // 2

    def body(c_ref, g_ref, t_ref, o_ref, ob_ref):
        tot = g_ref[...] + t_ref[...]
        o_ref[...] = tot
        ob_ref[...] = _bf(tot)

    piece = pl.BlockSpec((None, hr, C), lambda p, c: (p, 0, 0))
    return pl.pallas_call(
        body, name=name,
        grid_spec=pltpu.PrefetchScalarGridSpec(
            num_scalar_prefetch=1, grid=(P,),
            in_specs=[pl.BlockSpec((None, hr, C), lambda p, c: (p, c[0], 0)), piece],
            out_specs=[piece, piece]),
        out_shape=[jax.ShapeDtypeStruct((P, hr, C), F32), jax.ShapeDtypeStruct((P, hr, C), BF16)],
        compiler_params=_cparams(1, 4 * _nbytes((hr, C), F32)))(c_arr, g, t)


def _exchange_chip_sums(sums):
    n = len(sums)
    shapes = [s.shape for s in sums]

    def body(*refs):
        ins, outs = refs[:n], refs[n:2 * n]
        send, recv = refs[2 * n:]
        x, y, c, chips = _place()
        cps = []
        for i in range(n):
            for j, (cx, cy) in enumerate(chips):
                cp = pltpu.make_async_remote_copy(
                    src_ref=ins[i].at[2 * cx + cy], dst_ref=outs[i].at[j], send_sem=send.at[i, j],
                    recv_sem=recv.at[i, j], device_id=(cx, cy, c), device_id_type=MESH)
                cp.start()
                cps.append(cp)
        for cp in cps:
            cp.wait()

    return pl.pallas_call(
        body, name="exchange_chip_sums", in_specs=[ANY] * n, out_specs=[ANY] * n,
        out_shape=[jax.ShapeDtypeStruct((N_CHIPS - 1,) + s[1:], BF16) for s in shapes],
        scratch_shapes=[pltpu.SemaphoreType.DMA((n, 3)), pltpu.SemaphoreType.DMA((n, 3))],
    )(*sums)


def _sum_chips(name, own, got, kc_arr):
    P, hr, C = own.shape

    def body(kc_ref, o_ref, g_ref, r_ref):
        r_ref[...] = ((o_ref[...] + g_ref[0].astype(F32)) + g_ref[1].astype(F32)) + g_ref[2].astype(F32)

    return pl.pallas_call(
        body, name=name,
        grid_spec=pltpu.PrefetchScalarGridSpec(
            num_scalar_prefetch=1, grid=(1,),
            in_specs=[pl.BlockSpec((None, hr, C), lambda i, kc: (kc[0], 0, 0)),
                      pl.BlockSpec((N_CHIPS - 1, hr, C), lambda i, kc: (0, 0, 0))],
            out_specs=pl.BlockSpec((hr, C), lambda i, kc: (kc[1], 0))),
        out_shape=jax.ShapeDtypeStruct((2 * hr, C), F32),
        compiler_params=_cparams(1, 5 * _nbytes((hr, C), F32)))(kc_arr, own, got)


def _join_core_halves(shards):
    n = len(shards)
    shapes = [s.shape for s in shards]

    def body(*refs):
        outs = refs[n:2 * n]
        send, recv = refs[2 * n:]
        x, y, c, _ = _place()
        cps = []
        for i in range(n):
            hr = shapes[i][0] // 2
            mine = outs[i].at[pl.ds(c * hr, hr), :]
            cp = pltpu.make_async_remote_copy(src_ref=mine, dst_ref=mine, send_sem=send.at[i], recv_sem=recv.at[i],
                                              device_id=(x, y, 1 - c), device_id_type=MESH)
            cp.start()
            cps.append(cp)
        for cp in cps:
            cp.wait()

    return pl.pallas_call(
        body, name="join_core_halves", in_specs=[ANY] * n, out_specs=[ANY] * n,
        out_shape=[jax.ShapeDtypeStruct(s, F32) for s in shapes],
        input_output_aliases={i: i for i in range(n)},
        scratch_shapes=[pltpu.SemaphoreType.DMA((n,)), pltpu.SemaphoreType.DMA((n,))],
    )(*shards)


def _allreduce_small(v):
    R, C = v.shape
    n_dev = 8
    flips = [(dx, dy, dc) for dx in (0, 1) for dy in (0, 1) for dc in (0, 1)][1:]

    def body(v_ref, o_ref, slots, send, recv):
        x, y, c, _ = _place()
        me = 4 * x + 2 * y + c
        cps = []
        for q, (dx, dy, dc) in enumerate(flips):
            to = (x + dx - 2 * x * dx, y + dy - 2 * y * dy, c + dc - 2 * c * dc)
            cp = pltpu.make_async_remote_copy(src_ref=v_ref, dst_ref=slots.at[me], send_sem=send.at[q],
                                              recv_sem=recv.at[q], device_id=to, device_id_type=MESH)
            cp.start()
            cps.append(cp)
        slots[me] = v_ref[...]
        for cp in cps:
            cp.wait_recv()
        acc = slots[0]
        for d in range(1, n_dev):
            acc = acc + slots[d]
        o_ref[...] = acc
        for cp in cps:
            cp.wait_send()

    return pl.pallas_call(
        body, name="allreduce_small",
        in_specs=[pl.BlockSpec(memory_space=pltpu.VMEM)], out_specs=pl.BlockSpec(memory_space=pltpu.VMEM),
        out_shape=jax.ShapeDtypeStruct((R, C), F32),
        scratch_shapes=[pltpu.VMEM((n_dev, R, C), F32), pltpu.SemaphoreType.DMA((n_dev - 1,)),
                        pltpu.SemaphoreType.DMA((n_dev - 1,))],
    )(v)


def _adamw(name, w, g, m, v, *, tr):
    R, C = w.shape

    def body(w_ref, g_ref, m_ref, v_ref, d_ref, nm_ref, nv_ref):
        gv = g_ref[...]
        nm = ADAM_B1 * m_ref[...] + (1.0 - ADAM_B1) * gv
        nv = ADAM_B2 * v_ref[...] + (1.0 - ADAM_B2) * (gv * gv)
        m_hat = nm / (1.0 - ADAM_B1 ** ADAM_STEP)
        v_hat = nv / (1.0 - ADAM_B2 ** ADAM_STEP)
        d_ref[...] = -ADAM_LR * (m_hat / (jnp.sqrt(v_hat) + ADAM_EPS) + ADAM_WD * w_ref[...])
        nm_ref[...] = nm
        nv_ref[...] = nv

    tile = pl.BlockSpec((tr, C), lambda i: (i, 0))
    return pl.pallas_call(
        body, name=name, grid=(R // tr,), in_specs=[tile] * 4, out_specs=[tile] * 3,
        out_shape=[jax.ShapeDtypeStruct((R, C), F32)] * 3,
        compiler_params=_cparams(1, 7 * _nbytes((tr, C), F32)))(w, g, m, v)


def _pack_small(d):
    parts = []
    for name in SMALL:
        a = d[name].reshape(-1, HEAD)
        pad = (-a.shape[0]) % 8
        parts.append(jnp.pad(a, ((0, pad), (0, 0))))
    return jnp.concatenate(parts, axis=0)


def _unpack_small(packed, like):
    out = {}
    row = 0
    for name in SMALL:
        shape = like[name].shape
        rows = like[name].size // HEAD
        out[name] = packed[row:row + rows].reshape(shape)
        row += rows + (-rows) % 8
    return out


def kernel(x, mem, positions, mix_norm_g, w_in, q_norm_g, k_norm_g, pool_w, pool_scale, w_out, cross_norm_g, mem_norm_g, w_cq, w_ckv, cq_norm_g, ck_norm_g, w_co, ffn_norm_g, w_gate_up, w_down, loss_target, m_mix_norm_g, m_w_in, m_q_norm_g, m_k_norm_g, m_pool_w, m_pool_scale, m_w_out, m_cross_norm_g, m_mem_norm_g, m_w_cq, m_w_ckv, m_cq_norm_g, m_ck_norm_g, m_w_co, m_ffn_norm_g, m_w_gate_up, m_w_down, v_mix_norm_g, v_w_in, v_q_norm_g, v_k_norm_g, v_pool_w, v_pool_scale, v_w_out, v_cross_norm_g, v_mem_norm_g, v_w_cq, v_w_ckv, v_cq_norm_g, v_ck_norm_g, v_w_co, v_ffn_norm_g, v_w_gate_up, v_w_down):
    w = dict(mix_norm_g=mix_norm_g, w_in=w_in, q_norm_g=q_norm_g, k_norm_g=k_norm_g, pool_w=pool_w,
             pool_scale=pool_scale, w_out=w_out, cross_norm_g=cross_norm_g, mem_norm_g=mem_norm_g, w_cq=w_cq,
             w_ckv=w_ckv, cq_norm_g=cq_norm_g, ck_norm_g=ck_norm_g, w_co=w_co, ffn_norm_g=ffn_norm_g,
             w_gate_up=w_gate_up, w_down=w_down)
    m = dict(mix_norm_g=m_mix_norm_g, w_in=m_w_in, q_norm_g=m_q_norm_g, k_norm_g=m_k_norm_g, pool_w=m_pool_w,
             pool_scale=m_pool_scale, w_out=m_w_out, cross_norm_g=m_cross_norm_g, mem_norm_g=m_mem_norm_g,
             w_cq=m_w_cq, w_ckv=m_w_ckv, cq_norm_g=m_cq_norm_g, ck_norm_g=m_ck_norm_g, w_co=m_w_co,
             ffn_norm_g=m_ffn_norm_g, w_gate_up=m_w_gate_up, w_down=m_w_down)
    v = dict(mix_norm_g=v_mix_norm_g, w_in=v_w_in, q_norm_g=v_q_norm_g, k_norm_g=v_k_norm_g, pool_w=v_pool_w,
             pool_scale=v_pool_scale, w_out=v_w_out, cross_norm_g=v_cross_norm_g, mem_norm_g=v_mem_norm_g,
             w_cq=v_w_cq, w_ckv=v_w_ckv, cq_norm_g=v_cq_norm_g, ck_norm_g=v_ck_norm_g, w_co=v_w_co,
             ffn_norm_g=v_ffn_norm_g, w_gate_up=v_w_gate_up, w_down=v_w_down)

    c_arr = lax.axis_index("c").astype(jnp.int32).reshape(1)
    k_arr = (2 * lax.axis_index("x") + lax.axis_index("y")).astype(jnp.int32).reshape(1)
    kc_arr = jnp.concatenate([k_arr, c_arr])
    gathered = _gather_weights([_cast_piece(f"cast_{k}", w[k][0], k_arr) for k in BIG])
    wf = dict(zip(BIG, gathered))
    sm = {k: (w[k][0] if k == "pool_w" else w[k]) for k in SMALL}

    loss_part, dx, gbig, gsmall = _local_step(x[0], mem[0], positions[0], loss_target[0], wf, sm)
    loss = lax.psum(loss_part, ("x", "y", "c"))

    glist = [gbig[k] for k in BIG]
    from_sibling = _swap_core_halves(glist)
    chip_sums = [_add_core_halves(f"add_halves_{k}", g, t, c_arr) for k, g, t in zip(BIG, glist, from_sibling)]
    from_chips = _exchange_chip_sums([s16 for _, s16 in chip_sums])
    halves = [_sum_chips(f"sum_chips_{k}", s32, t, kc_arr) for k, (s32, _), t in zip(BIG, chip_sums, from_chips)]
    gshard = dict(zip(BIG, _join_core_halves(halves)))

    gsm = _unpack_small(_allreduce_small(_pack_small(gsmall)), sm)

    grads, deltas, new_m, new_v = {}, {}, {}, {}
    for k in BIG:
        shard = w[k][0]
        tr = shard.shape[0] // 4
        d, nm, nv = _adamw(f"adamw_{k}", shard, gshard[k], m[k][0], v[k][0], tr=tr)
        grads[k], deltas[k], new_m[k], new_v[k] = gshard[k][None], d[None], nm[None], nv[None]
    smw = {k: (w[k][0] if k == "pool_w" else w[k]) for k in SMALL}
    smm = {k: (m[k][0] if k == "pool_w" else m[k]) for k in SMALL}
    smv = {k: (v[k][0] if k == "pool_w" else v[k]) for k in SMALL}
    pw, pg, pm, pv = _pack_small(smw), _pack_small(gsm), _pack_small(smm), _pack_small(smv)
    d, nm, nv = _adamw("adamw_small", pw, pg, pm, pv, tr=pw.shape[0])
    for dst, packed in ((deltas, d), (new_m, nm), (new_v, nv)):
        un = _unpack_small(packed, sm)
        for k in SMALL:
            dst[k] = un[k].reshape(w[k].shape)
    for k in SMALL:
        grads[k] = gsm[k].reshape(w[k].shape)

    return (loss, dx[None], *[grads[k] for k in WEIGHTS], *[deltas[k] for k in WEIGHTS],
            *[new_m[k] for k in WEIGHTS], *[new_v[k] for k in WEIGHTS])
```

```python
import functools

import jax
import jax.numpy as jnp
from jax import lax
from jax.experimental import pallas as pl
from jax.experimental.pallas import tpu as pltpu

F32 = jnp.float32
BF16 = jnp.bfloat16
MESH = pl.DeviceIdType.MESH
ANY = pl.BlockSpec(memory_space=pl.ANY)

D_MODEL = 1024
HEAD = 128
N_HEADS = 4
DILATIONS = (1, 4, 16)
BLOCK = 128
Q_W = 1536
KV_W = 512
POOL_W = 512
POOL_WINDOWS = (2, 4, 8, 16)
POOL_HALO = 16
IN_W = 3072
ROT_DIM = 32
ROT_HALF = 16
ROPE_THETA = 500000.0
X_W = 512
D_FF = 2816
EPS = 1e-6
NEG_INF = -1e30
SCALE = HEAD ** -0.5
N_CHIPS = 4

ADAM_LR = 0.001
ADAM_B1 = 0.9
ADAM_B2 = 0.999
ADAM_EPS = 1e-08
ADAM_WD = 0.01
ADAM_STEP = 10

VMEM_BYTES_V7X = 64 * 2 ** 20
VMEM_LIMIT_MAX = 56 * 2 ** 20
VMEM_LIMIT_MIN = 24 * 2 ** 20


def _nbytes(shape, dtype):
    n = 1
    for s in shape:
        n *= s
    return n * jnp.dtype(dtype).itemsize


def _cparams(n_axes, block_bytes, scratch_bytes=0):
    est = 2 * (2 * block_bytes + scratch_bytes)
    lim = int(min(VMEM_LIMIT_MAX, max(VMEM_LIMIT_MIN, est)))
    return pltpu.CompilerParams(dimension_semantics=("arbitrary",) * n_axes, vmem_limit_bytes=lim)


def _bf(v):
    return v.astype(BF16)


def _dot(a, b):
    return jnp.dot(a, b, preferred_element_type=F32)


def _dot_nt(a, b):
    return lax.dot_general(a, b, (((1,), (1,)), ((), ())), preferred_element_type=F32)


def _dot_tn(a, b):
    return lax.dot_general(a, b, (((0,), (0,)), ((), ())), preferred_element_type=F32)


def _rstd(v):
    return lax.rsqrt(jnp.mean(v * v, axis=-1, keepdims=True) + EPS)


def _norm_bwd(dy, xv, g):
    r = _rstd(xv)
    xh = xv * r
    dxh = dy * g
    dx = r * (dxh - xh * jnp.mean(dxh * xh, axis=-1, keepdims=True))
    return dx, xh


def _place():
    x, y, c = lax.axis_index("x"), lax.axis_index("y"), lax.axis_index("c")
    other_chips = [(1 - x, y), (x, 1 - y), (1 - x, 1 - y)]
    return x, y, c, other_chips


class _Plan:
    ins = ()
    out_shapes = ()
    aliases = {}
    sem_shapes = ()

    def copies(self, ins, outs, sems):
        raise NotImplementedError

    def begin(self, ins, outs, sems):
        for cp in self.copies(ins, outs, sems):
            cp.start()

    def finish(self, ins, outs, sems):
        for cp in self.copies(ins, outs, sems):
            cp.wait()


class _GatherPlan(_Plan):
    def __init__(self, bufs):
        n = len(bufs)
        self.ins = list(bufs)
        self.out_shapes = [jax.ShapeDtypeStruct(b.shape, b.dtype) for b in bufs]
        self.aliases = {i: i for i in range(n)}
        self.sem_shapes = [pltpu.SemaphoreType.DMA((n, 6)), pltpu.SemaphoreType.DMA((n, 6))]

    def _parts(self, outs, sems):
        send, recv = sems
        x, y, c, chips = _place()

        def half(i, piece, which):
            hr = outs[i].shape[1] // 2
            return outs[i].at[piece, pl.ds(which * hr, hr), :]

        def copy(i, k, ref, to):
            return pltpu.make_async_remote_copy(src_ref=ref, dst_ref=ref, send_sem=send.at[i, k], recv_sem=recv.at[i, k],
                                                device_id=to, device_id_type=MESH)

        return x, y, c, chips, half, copy

    def begin(self, ins, outs, sems):
        x, y, c, chips, half, copy = self._parts(outs, sems)
        for i in range(len(outs)):
            for j, (cx, cy) in enumerate(chips):
                copy(i, j, half(i, 2 * x + y, c), (cx, cy, c)).start()

    def finish(self, ins, outs, sems):
        x, y, c, chips, half, copy = self._parts(outs, sems)
        sib = (x, y, 1 - c)
        n = len(outs)
        for i in range(n):
            for j, (cx, cy) in enumerate(chips):
                piece = half(i, 2 * cx + cy, c)
                copy(i, j, piece, (cx, cy, c)).wait_recv()
                copy(i, 3 + j, piece, sib).start()
        for i in range(n):
            for j, (cx, cy) in enumerate(chips):
                copy(i, 3 + j, half(i, 2 * cx + cy, 1 - c), sib).wait_recv()
        for i in range(n):
            for j, (cx, cy) in enumerate(chips):
                copy(i, j, half(i, 2 * x + y, c), (cx, cy, c)).wait_send()
                copy(i, 3 + j, half(i, 2 * cx + cy, c), sib).wait_send()


class _SwapPlan(_Plan):
    def __init__(self, grads):
        n = len(grads)
        self.ins = list(grads)
        self.out_shapes = [jax.ShapeDtypeStruct((g.shape[0], g.shape[1] // 2, g.shape[2]), g.dtype) for g in grads]
        self.sem_shapes = [pltpu.SemaphoreType.DMA((n,)), pltpu.SemaphoreType.DMA((n,))]

    def copies(self, ins, outs, sems):
        send, recv = sems
        x, y, c, _ = _place()
        cps = []
        for i in range(len(ins)):
            hr = ins[i].shape[1] // 2
            cps.append(pltpu.make_async_remote_copy(
                src_ref=ins[i].at[:, pl.ds((1 - c) * hr, hr), :], dst_ref=outs[i], send_sem=send.at[i],
                recv_sem=recv.at[i], device_id=(x, y, 1 - c), device_id_type=MESH))
        return cps


class _ExchangePlan(_Plan):
    def __init__(self, sums):
        n = len(sums)
        self.ins = list(sums)
        self.out_shapes = [jax.ShapeDtypeStruct((N_CHIPS - 1,) + s.shape[1:], s.dtype) for s in sums]
        self.sem_shapes = [pltpu.SemaphoreType.DMA((n, 3)), pltpu.SemaphoreType.DMA((n, 3))]

    def copies(self, ins, outs, sems):
        send, recv = sems
        x, y, c, chips = _place()
        cps = []
        for i in range(len(ins)):
            for j, (cx, cy) in enumerate(chips):
                cps.append(pltpu.make_async_remote_copy(
                    src_ref=ins[i].at[2 * cx + cy], dst_ref=outs[i].at[j], send_sem=send.at[i, j],
                    recv_sem=recv.at[i, j], device_id=(cx, cy, c), device_id_type=MESH))
        return cps


class _JoinPlan(_Plan):
    def __init__(self, shards):
        n = len(shards)
        self.ins = list(shards)
        self.out_shapes = [jax.ShapeDtypeStruct(s.shape, s.dtype) for s in shards]
        self.aliases = {i: i for i in range(n)}
        self.sem_shapes = [pltpu.SemaphoreType.DMA((n,)), pltpu.SemaphoreType.DMA((n,))]

    def copies(self, ins, outs, sems):
        send, recv = sems
        x, y, c, _ = _place()
        cps = []
        for i in range(len(outs)):
            hr = outs[i].shape[0] // 2
            mine = outs[i].at[pl.ds(c * hr, hr), :]
            cps.append(pltpu.make_async_remote_copy(src_ref=mine, dst_ref=mine, send_sem=send.at[i], recv_sem=recv.at[i],
                                                    device_id=(x, y, 1 - c), device_id_type=MESH))
        return cps


def _run_plan(name, plan):
    n_in, n_out = len(plan.ins), len(plan.out_shapes)

    def body(*refs):
        ins, outs, sems = refs[:n_in], refs[n_in:n_in + n_out], refs[n_in + n_out:]
        plan.begin(ins, outs, sems)
        plan.finish(ins, outs, sems)

    return pl.pallas_call(
        body, name=name, in_specs=[ANY] * n_in, out_specs=[ANY] * n_out, out_shape=list(plan.out_shapes),
        input_output_aliases=dict(plan.aliases), scratch_shapes=list(plan.sem_shapes))(*plan.ins)


def _pcall(body, *, name, grid, in_specs, out_specs, out_shape, params, args, scratch_shapes=(), plan=None):
    in_specs, out_specs, out_shape, scratch = list(in_specs), list(out_specs), list(out_shape), list(scratch_shapes)
    if plan is None:
        res = pl.pallas_call(body, name=name, grid=grid, in_specs=in_specs, out_specs=out_specs, out_shape=out_shape,
                             scratch_shapes=scratch, compiler_params=params)(*args)
        return list(res), []
    ni, no, ns = len(in_specs), len(out_specs), len(scratch)
    pi, po = len(plan.ins), len(plan.out_shapes)

    def wrapped(*refs):
        ins, pins = refs[:ni], refs[ni:ni + pi]
        outs, pouts = refs[ni + pi:ni + pi + no], refs[ni + pi + no:ni + pi + no + po]
        scr, psems = refs[ni + pi + no + po:ni + pi + no + po + ns], refs[ni + pi + no + po + ns:]
        ids = [pl.program_id(a) for a in range(len(grid))]
        first = functools.reduce(jnp.logical_and, [i == 0 for i in ids])
        last = functools.reduce(jnp.logical_and, [i == g - 1 for i, g in zip(ids, grid)])

        @pl.when(first)
        def _():
            plan.begin(pins, pouts, psems)

        body(*ins, *outs, *scr)

        @pl.when(last)
        def _():
            plan.finish(pins, pouts, psems)

    res = pl.pallas_call(
        wrapped, name=name, grid=grid, in_specs=in_specs + [ANY] * pi, out_specs=out_specs + [ANY] * po,
        out_shape=out_shape + list(plan.out_shapes), scratch_shapes=scratch + list(plan.sem_shapes),
        input_output_aliases={ni + a: no + b for a, b in plan.aliases.items()},
        compiler_params=params)(*args, *plan.ins)
    return list(res[:no]), list(res[no:])


def _mm_nn(name, a, w3, *, tm, norm_g=None, residual=None, out_dtype=F32, plan=None):
    M, K = a.shape
    P, Kw, C = w3.shape
    assert Kw == K and M % tm == 0
    N = P * C
    has_norm = norm_g is not None
    has_res = residual is not None

    def body(*refs):
        refs = list(refs)
        a_ref = refs.pop(0)
        w_ref = refs.pop(0)
        g_ref = refs.pop(0) if has_norm else None
        r_ref = refs.pop(0) if has_res else None
        o_ref = refs.pop(0)
        xn_ref = refs.pop(0) if has_norm else None
        if has_norm:
            av = a_ref[...].astype(F32)
            ab = _bf(av * _rstd(av) * g_ref[...])
            xn_ref[...] = ab
        else:
            ab = _bf(a_ref[...])
        for p in range(P):
            acc = _dot(ab, w_ref[p])
            if has_res:
                acc = acc + r_ref[:, p * C:(p + 1) * C]
            o_ref[:, p * C:(p + 1) * C] = acc.astype(o_ref.dtype)

    in_specs = [pl.BlockSpec((tm, K), lambda i: (i, 0)), pl.BlockSpec((P, K, C), lambda i: (0, 0, 0))]
    args = [a, w3]
    if has_norm:
        in_specs.append(pl.BlockSpec((1, K), lambda i: (0, 0)))
        args.append(norm_g)
    if has_res:
        in_specs.append(pl.BlockSpec((tm, N), lambda i: (i, 0)))
        args.append(residual)
    out_shape = [jax.ShapeDtypeStruct((M, N), out_dtype)]
    out_specs = [pl.BlockSpec((tm, N), lambda i: (i, 0))]
    if has_norm:
        out_shape.append(jax.ShapeDtypeStruct((M, K), BF16))
        out_specs.append(pl.BlockSpec((tm, K), lambda i: (i, 0)))
    blk = (_nbytes((tm, K), a.dtype) + _nbytes((P, K, C), BF16) + 2 * _nbytes((tm, N), F32)
           + _nbytes((tm, K), BF16))
    res, got = _pcall(body, name=name, grid=(M // tm,), in_specs=in_specs, out_specs=out_specs, out_shape=out_shape,
                      params=_cparams(1, blk), args=args, plan=plan)
    res = res if has_norm else res[0]
    return res if plan is None else (res, got)


def _ffn_up(name, h, wgu3, g, *, tm):
    M, K = h.shape
    P, _, C = wgu3.shape
    half = P // 2

    def body(h_ref, wg_ref, wu_ref, g_ref, act_ref, gu_ref, xn_ref, ab_ref):
        @pl.when(pl.program_id(1) == 0)
        def _():
            hv = h_ref[...]
            xn = _bf(hv * _rstd(hv) * g_ref[...])
            ab_ref[...] = xn
            xn_ref[...] = xn
        ab = ab_ref[...]
        gate = _dot(ab, wg_ref[...])
        up = _dot(ab, wu_ref[...])
        act_ref[...] = _bf(gate * (1.0 / (1.0 + jnp.exp(-gate))) * up)
        gu_ref[0] = _bf(gate)
        gu_ref[1] = _bf(up)

    blk = (_nbytes((tm, K), F32) + 2 * _nbytes((K, C), BF16) + 3 * _nbytes((tm, C), BF16)
           + 2 * _nbytes((tm, C), F32) + _nbytes((tm, K), BF16))
    return pl.pallas_call(
        body, name=name, grid=(M // tm, half),
        in_specs=[pl.BlockSpec((tm, K), lambda i, j: (i, 0)),
                  pl.BlockSpec((None, K, C), lambda i, j: (j, 0, 0)),
                  pl.BlockSpec((None, K, C), lambda i, j: (j + half, 0, 0)),
                  pl.BlockSpec((1, K), lambda i, j: (0, 0))],
        out_specs=[pl.BlockSpec((tm, C), lambda i, j: (i, j)),
                   pl.BlockSpec((2, tm, C), lambda i, j: (0, i, j)),
                   pl.BlockSpec((tm, K), lambda i, j: (i, 0))],
        out_shape=[jax.ShapeDtypeStruct((M, half * C), BF16),
                   jax.ShapeDtypeStruct((2, M, half * C), BF16),
                   jax.ShapeDtypeStruct((M, K), BF16)],
        scratch_shapes=[pltpu.VMEM((tm, K), BF16)],
        compiler_params=_cparams(2, blk, _nbytes((tm, K), BF16)))(h, wgu3, wgu3, g)


def _ffn_down_loss(name, act, wd3, h, target, *, tm):
    M, K = act.shape
    _, _, N = wd3.shape

    def body(a_ref, w_ref, h_ref, t_ref, dy_ref, ls_ref):
        err = _dot(a_ref[...], w_ref[...]) + h_ref[...] - t_ref[...]
        dy_ref[...] = err * (1.0 / N)

        @pl.when(pl.program_id(0) == 0)
        def _():
            ls_ref[...] = jnp.zeros_like(ls_ref)
        ls_ref[...] += jnp.sum(err * err, axis=0, keepdims=True)

    blk = _nbytes((tm, K), BF16) + _nbytes((K, N), BF16) + 4 * _nbytes((tm, N), F32)
    return pl.pallas_call(
        body, name=name, grid=(M // tm,),
        in_specs=[pl.BlockSpec((tm, K), lambda i: (i, 0)),
                  pl.BlockSpec((None, K, N), lambda i: (0, 0, 0)),
                  pl.BlockSpec((tm, N), lambda i: (i, 0)),
                  pl.BlockSpec((tm, N), lambda i: (i, 0))],
        out_specs=[pl.BlockSpec((tm, N), lambda i: (i, 0)),
                   pl.BlockSpec((1, N), lambda i: (0, 0))],
        out_shape=[jax.ShapeDtypeStruct((M, N), F32), jax.ShapeDtypeStruct((1, N), F32)],
        compiler_params=_cparams(1, blk))(act, wd3, h, target)


def _nt_pieces(a_ref, w_ref):
    P, _, C = w_ref.shape
    acc = _dot_nt(_bf(a_ref[:, 0:C]), w_ref[0])
    for p in range(1, P):
        acc = acc + _dot_nt(_bf(a_ref[:, p * C:(p + 1) * C]), w_ref[p])
    return acc


def _mm_nt(name, a, w3, *, tm, out_dtype):
    M, N = a.shape
    P, Ko, C = w3.shape
    assert N == P * C and M % tm == 0

    def body(a_ref, w_ref, o_ref):
        o_ref[...] = _nt_pieces(a_ref, w_ref).astype(o_ref.dtype)

    blk = _nbytes((tm, N), a.dtype) + _nbytes((P, Ko, C), BF16) + 2 * _nbytes((tm, Ko), F32)
    return pl.pallas_call(
        body, name=name, grid=(M // tm,),
        in_specs=[pl.BlockSpec((tm, N), lambda i: (i, 0)), pl.BlockSpec((P, Ko, C), lambda i: (0, 0, 0))],
        out_specs=pl.BlockSpec((tm, Ko), lambda i: (i, 0)),
        out_shape=jax.ShapeDtypeStruct((M, Ko), out_dtype),
        compiler_params=_cparams(1, blk))(a, w3)


def _mm_nt_normbwd(name, a, w3, h, g, dres, *, tm, plan=None):
    M, D = h.shape
    P, Ko, C = w3.shape
    N = a.shape[1]
    assert N == P * C and Ko == D and M % tm == 0
    has_res = dres is not None

    def body(*refs):
        a_ref, w_ref, h_ref, g_ref = refs[:4]
        r_ref = refs[4] if has_res else None
        dx_ref, dg_ref = refs[-2:]
        dhn = _nt_pieces(a_ref, w_ref)
        dx, xh = _norm_bwd(dhn, h_ref[...], g_ref[...])
        if has_res:
            dx = dx + r_ref[...]
        dx_ref[...] = dx

        @pl.when(pl.program_id(0) == 0)
        def _():
            dg_ref[...] = jnp.zeros_like(dg_ref)
        dg_ref[...] += jnp.sum(dhn * xh, axis=0, keepdims=True)

    row = pl.BlockSpec((tm, D), lambda i: (i, 0))
    vec = pl.BlockSpec((1, D), lambda i: (0, 0))
    in_specs = [pl.BlockSpec((tm, N), lambda i: (i, 0)), pl.BlockSpec((P, Ko, C), lambda i: (0, 0, 0)), row, vec]
    args = [a, w3, h, g]
    if has_res:
        in_specs.append(row)
        args.append(dres)
    blk = _nbytes((tm, N), a.dtype) + _nbytes((P, Ko, C), BF16) + 5 * _nbytes((tm, D), F32)
    res, got = _pcall(body, name=name, grid=(M // tm,), in_specs=in_specs, out_specs=[row, vec],
                      out_shape=[jax.ShapeDtypeStruct((M, D), F32), jax.ShapeDtypeStruct((1, D), F32)],
                      params=_cparams(1, blk), args=args, plan=plan)
    return res if plan is None else (res, got)


def _mm_nt_normbwd_steps(name, a, a_spec, a_blk, w3, w_spec, w_blk, nsteps, h, g, dres, *, tm):
    M, D = h.shape
    has_res = dres is not None

    def body(*refs):
        refs = list(refs)
        a_ref, w_ref, h_ref, g_ref = refs[:4]
        r_ref = refs[4] if has_res else None
        dx_ref, dg_ref, acc_ref = refs[-3:]
        i = pl.program_id(0)
        p = pl.program_id(1)

        @pl.when(p == 0)
        def _():
            acc_ref[...] = jnp.zeros_like(acc_ref)
        acc_ref[...] += _dot_nt(_bf(a_ref[...]), w_ref[...])

        @pl.when(p == nsteps - 1)
        def _():
            dhn = acc_ref[...]
            dx, xh = _norm_bwd(dhn, h_ref[...], g_ref[...])
            if has_res:
                dx = dx + r_ref[...]
            dx_ref[...] = dx

            @pl.when(i == 0)
            def _():
                dg_ref[...] = jnp.zeros_like(dg_ref)
            dg_ref[...] += jnp.sum(dhn * xh, axis=0, keepdims=True)

    in_specs = [a_spec, w_spec, pl.BlockSpec((tm, D), lambda i, p: (i, 0)),
                pl.BlockSpec((1, D), lambda i, p: (0, 0))]
    args = [a, w3, h, g]
    if has_res:
        in_specs.append(pl.BlockSpec((tm, D), lambda i, p: (i, 0)))
        args.append(dres)
    blk = _nbytes(a_blk, a.dtype) + _nbytes(w_blk, BF16) + 5 * _nbytes((tm, D), F32)
    return pl.pallas_call(
        body, name=name, grid=(M // tm, nsteps), in_specs=in_specs,
        out_specs=[pl.BlockSpec((tm, D), lambda i, p: (i, 0)), pl.BlockSpec((1, D), lambda i, p: (0, 0))],
        out_shape=[jax.ShapeDtypeStruct((M, D), F32), jax.ShapeDtypeStruct((1, D), F32)],
        scratch_shapes=[pltpu.VMEM((tm, D), F32)],
        compiler_params=_cparams(2, blk, _nbytes((tm, D), F32)))(*args)


def _ffn_down_bwd(name, dy, wd3, gu, *, tm, tn):
    M, D = dy.shape
    _, F, _ = wd3.shape

    def body(dy_ref, w_ref, gu_ref, o_ref, ab_ref):
        @pl.when(pl.program_id(1) == 0)
        def _():
            ab_ref[...] = _bf(dy_ref[...])
        da = _dot_nt(ab_ref[...], w_ref[...])
        gate = gu_ref[0].astype(F32)
        up = gu_ref[1].astype(F32)
        sig = 1.0 / (1.0 + jnp.exp(-gate))
        silu = gate * sig
        o_ref[0] = _bf(da * up * (sig + silu * (1.0 - sig)))
        o_ref[1] = _bf(da * silu)

    blk = (_nbytes((tm, D), F32) + _nbytes((tn, D), BF16) + 4 * _nbytes((tm, tn), BF16)
           + 4 * _nbytes((tm, tn), F32))
    return pl.pallas_call(
        body, name=name, grid=(M // tm, F // tn),
        in_specs=[pl.BlockSpec((tm, D), lambda i, j: (i, 0)),
                  pl.BlockSpec((None, tn, D), lambda i, j: (0, j, 0)),
                  pl.BlockSpec((2, tm, tn), lambda i, j: (0, i, j))],
        out_specs=pl.BlockSpec((2, tm, tn), lambda i, j: (0, i, j)),
        out_shape=jax.ShapeDtypeStruct((2, M, F), BF16),
        scratch_shapes=[pltpu.VMEM((tm, D), BF16)],
        compiler_params=_cparams(2, blk, _nbytes((tm, D), BF16)))(dy, wd3, gu)


def _mm_tn(name, a, a_spec, a_blk, b, b_spec, b_blk, out3, o_spec, o_blk, grid):
    def body(a_ref, b_ref, o_ref):
        @pl.when(pl.program_id(2) == 0)
        def _():
            o_ref[...] = jnp.zeros_like(o_ref)
        o_ref[...] += _dot_tn(_bf(a_ref[...]), _bf(b_ref[...]))

    blk = _nbytes(a_blk, a.dtype) + _nbytes(b_blk, b.dtype) + 2 * _nbytes(o_blk, F32)
    return pl.pallas_call(
        body, name=name, grid=grid, in_specs=[a_spec, b_spec], out_specs=o_spec,
        out_shape=jax.ShapeDtypeStruct(out3, F32),
        compiler_params=_cparams(3, blk))(a, b)


def _swap_halves(v, lane, masked):
    up = pltpu.roll(v, HEAD - ROT_HALF, 1)
    down = pltpu.roll(v, ROT_HALF, 1)
    rest = jnp.where(lane < ROT_DIM, down, 0.0) if masked else down
    return jnp.where(lane < ROT_HALF, up, rest)


def _qk_prep(name, proj, ctab, stab, qg, kg, *, tm, plan=None):
    S = proj.shape[0]
    width = Q_W + KV_W

    def body(p_ref, c_ref, s_ref, qg_ref, kg_ref, q_ref, k_ref):
        lane = lax.broadcasted_iota(jnp.int32, (tm, HEAD), 1)
        cv = c_ref[...]
        sv = s_ref[...]

        def prep(t, g, scale):
            n = t * _rstd(t) * g
            return (n * cv + _swap_halves(n, lane, False) * sv) * scale

        for j in range(Q_W // HEAD):
            q_ref[:, j * HEAD:(j + 1) * HEAD] = prep(p_ref[:, j * HEAD:(j + 1) * HEAD], qg_ref[...], SCALE)
        for j in range(KV_W // HEAD):
            k_ref[:, j * HEAD:(j + 1) * HEAD] = prep(p_ref[:, Q_W + j * HEAD:Q_W + (j + 1) * HEAD], kg_ref[...], 1.0)

    blk = 2 * _nbytes((tm, width), F32) + 2 * _nbytes((tm, HEAD), F32)
    res, got = _pcall(
        body, name=name, grid=(S // tm,),
        in_specs=[pl.BlockSpec((tm, width), lambda i: (i, 0)),
                  pl.BlockSpec((tm, HEAD), lambda i: (i, 0)),
                  pl.BlockSpec((tm, HEAD), lambda i: (i, 0)),
                  pl.BlockSpec((1, HEAD), lambda i: (0, 0)),
                  pl.BlockSpec((1, HEAD), lambda i: (0, 0))],
        out_specs=[pl.BlockSpec((tm, Q_W), lambda i: (i, 0)), pl.BlockSpec((tm, KV_W), lambda i: (i, 0))],
        out_shape=[jax.ShapeDtypeStruct((S, Q_W), F32), jax.ShapeDtypeStruct((S, KV_W), F32)],
        params=_cparams(1, blk), args=[proj, ctab, stab, qg, kg], plan=plan)
    return res if plan is None else (res, got)


ATTN_STEP = 2048


def _row_idx(start, dil):
    return pl.ds(start, BLOCK) if dil == 1 else pl.ds(start, BLOCK, stride=dil)


def _rows(ref, start, dil):
    return ref[_row_idx(start, dil), :]


def _set_rows(ref, start, dil, val):
    ref[_row_idx(start, dil), :] = val


def _add_rows(ref, start, dil, val):
    idx = _row_idx(start, dil)
    ref[idx, :] = ref[idx, :] + val


def _band_mask(first):
    qi = lax.broadcasted_iota(jnp.int32, (BLOCK, 2 * BLOCK), 0)
    kj = lax.broadcasted_iota(jnp.int32, (BLOCK, 2 * BLOCK), 1)
    band = (kj >= qi) & (kj <= qi + BLOCK)
    if first is None:
        return band
    return band & ((kj >= BLOCK) | jnp.logical_not(first))


def _attn_geometry(S, grp):
    dil = DILATIONS[grp]
    bt = BLOCK * dil
    assert S % ATTN_STEP == 0 and ATTN_STEP % bt == 0
    return dil, bt, ATTN_STEP // bt, S // ATTN_STEP


def _attn_fwd(name, qn, kn, proj, grp, plan=None):
    S = qn.shape[0]
    dil, bt, nsub, nsb = _attn_geometry(S, grp)
    vcol = (Q_W + KV_W) // HEAD

    def body(q_ref, kp_ref, kc_ref, vp_ref, vc_ref, o_ref, l_ref):
        valid_first = _band_mask(pl.program_id(1) == 0)
        valid_inner = _band_mask(None)
        for b in range(nsub):
            for r in range(dil):
                at = b * bt + r
                if b == 0:
                    kprev, vprev, valid = _rows(kp_ref, r, dil), _rows(vp_ref, r, dil), valid_first
                else:
                    kprev, vprev, valid = _rows(kc_ref, at - bt, dil), _rows(vc_ref, at - bt, dil), valid_inner
                q = _bf(_rows(q_ref, at, dil))
                k2 = _bf(jnp.concatenate([kprev, _rows(kc_ref, at, dil)], axis=0))
                v2 = _bf(jnp.concatenate([vprev, _rows(vc_ref, at, dil)], axis=0))
                s = jnp.where(valid, _dot_nt(q, k2), NEG_INF)
                m = jnp.max(s, axis=-1, keepdims=True)
                p = jnp.exp(s - m)
                l = jnp.sum(p, axis=-1, keepdims=True)
                acc = _dot(_bf(p), v2)
                _set_rows(o_ref, at, dil, acc / l)
                _set_rows(l_ref, at, dil, jnp.broadcast_to(m + jnp.log(l), (BLOCK, HEAD)))

    prev = lambda n: jnp.maximum(n * nsub - 1, 0)
    big = lambda col: pl.BlockSpec((ATTN_STEP, HEAD), lambda h, n: (n, col(h)))
    tail = lambda col: pl.BlockSpec((bt, HEAD), lambda h, n: (prev(n), col(h)))
    blk = 5 * _nbytes((ATTN_STEP, HEAD), F32) + 2 * _nbytes((bt, HEAD), F32)
    res, got = _pcall(
        body, name=name, grid=(N_HEADS, nsb),
        in_specs=[big(lambda h: grp * N_HEADS + h), tail(lambda h: h), big(lambda h: h),
                  tail(lambda h: vcol + h), big(lambda h: vcol + h)],
        out_specs=[big(lambda h: h), big(lambda h: h)],
        out_shape=[jax.ShapeDtypeStruct((S, KV_W), F32), jax.ShapeDtypeStruct((S, KV_W), F32)],
        params=_cparams(2, blk), args=[qn, kn, kn, proj, proj], plan=plan)
    return res if plan is None else (res, got)


def _attn_bwd(name, qn, kn, proj, dattn, lse, delta, grp, plan=None):
    S = qn.shape[0]
    dil, bt, nsub, nsb = _attn_geometry(S, grp)
    vcol = (Q_W + KV_W) // HEAD

    def body(q_ref, kp_ref, kc_ref, vp_ref, vc_ref, do_ref, l_ref, d_ref, dq_ref, dk_ref, dv_ref, ck_ref, cv_ref):
        n = pl.program_id(1)
        par = n % 2

        @pl.when(n < nsb)
        def _():
            valid_first = _band_mask(n == 0)
            valid_inner = _band_mask(None)
            ck, cv = ck_ref.at[par], cv_ref.at[par]
            pk, pv = ck_ref.at[1 - par], cv_ref.at[1 - par]
            for b in range(nsub):
                for r in range(dil):
                    at = b * bt + r
                    if b == 0:
                        kprev, vprev, valid = _rows(kp_ref, r, dil), _rows(vp_ref, r, dil), valid_first
                    else:
                        kprev, vprev, valid = _rows(kc_ref, at - bt, dil), _rows(vc_ref, at - bt, dil), valid_inner
                    q = _bf(_rows(q_ref, at, dil))
                    k2 = _bf(jnp.concatenate([kprev, _rows(kc_ref, at, dil)], axis=0))
                    v2 = _bf(jnp.concatenate([vprev, _rows(vc_ref, at, dil)], axis=0))
                    do = _bf(_rows(do_ref, at, dil))
                    lse_r = _rows(l_ref, at, dil)[:, :1]
                    del_r = _rows(d_ref, at, dil)[:, :1]
                    s = jnp.where(valid, _dot_nt(q, k2), NEG_INF)
                    p = jnp.exp(s - lse_r)
                    ds = _bf(p * (_dot_nt(do, v2) - del_r))
                    _set_rows(dq_ref, at, dil, _dot(ds, k2))
                    dk2 = _dot_tn(ds, q)
                    dv2 = _dot_tn(_bf(p), do)
                    _set_rows(ck, at, dil, dk2[BLOCK:])
                    _set_rows(cv, at, dil, dv2[BLOCK:])
                    if b > 0:
                        _add_rows(ck, at - bt, dil, dk2[:BLOCK])
                        _add_rows(cv, at - bt, dil, dv2[:BLOCK])
                    else:
                        @pl.when(n > 0)
                        def _():
                            _add_rows(pk, (nsub - 1) * bt + r, dil, dk2[:BLOCK])
                            _add_rows(pv, (nsub - 1) * bt + r, dil, dv2[:BLOCK])

        @pl.when(n > 0)
        def _():
            dk_ref[...] = ck_ref[1 - par]
            dv_ref[...] = cv_ref[1 - par]

    cur = lambda n: jnp.minimum(n, nsb - 1)
    prev = lambda n: jnp.maximum(n - 1, 0)
    tail_at = lambda n: jnp.maximum(cur(n) * nsub - 1, 0)
    big = lambda col: pl.BlockSpec((ATTN_STEP, HEAD), lambda h, n: (cur(n), col(h)))
    tail = lambda col: pl.BlockSpec((bt, HEAD), lambda h, n: (tail_at(n), col(h)))
    late = pl.BlockSpec((ATTN_STEP, HEAD), lambda h, n: (prev(n), h))
    blk = 9 * _nbytes((ATTN_STEP, HEAD), F32) + 2 * _nbytes((bt, HEAD), F32)
    res, got = _pcall(
        body, name=name, grid=(N_HEADS, nsb + 1),
        in_specs=[big(lambda h: grp * N_HEADS + h), tail(lambda h: h), big(lambda h: h),
                  tail(lambda h: vcol + h), big(lambda h: vcol + h),
                  big(lambda h: h), big(lambda h: h), big(lambda h: h)],
        out_specs=[big(lambda h: h), late, late],
        out_shape=[jax.ShapeDtypeStruct((S, KV_W), F32)] * 3,
        scratch_shapes=[pltpu.VMEM((2, ATTN_STEP, HEAD), F32), pltpu.VMEM((2, ATTN_STEP, HEAD), F32)],
        params=_cparams(2, blk, 4 * _nbytes((ATTN_STEP, HEAD), F32)),
        args=[qn, kn, kn, proj, proj, dattn, lse, delta], plan=plan)
    return res if plan is None else (res, got)


def _window_sum(v, n_doublings, back):
    rows = v.shape[0]
    step = 1
    for _ in range(n_doublings):
        v = v + pltpu.roll(v, step if back else rows - step, 0)
        step *= 2
    return v


def _mix_post(name, outs, lses, proj, pool_w, pool_scale, *, tm):
    S = proj.shape[0]
    ucol = (IN_W - POOL_W) // POOL_W
    hpt = tm // POOL_HALO

    def body(o0, o1, o2, l0, l1, l2, u_ref, uh_ref, pw_ref, ps_ref, mix_ref, lse_ref, dp_ref):
        i = pl.program_id(0)
        for h in range(N_HEADS):
            sl = slice(h * HEAD, (h + 1) * HEAD)
            ls = [l0[:, sl], l1[:, sl], l2[:, sl]]
            m = jnp.maximum(jnp.maximum(ls[0], ls[1]), ls[2])
            ws = [jnp.exp(v - m) for v in ls]
            den = ws[0] + ws[1] + ws[2]
            num = ws[0] * o0[:, sl] + ws[1] * o1[:, sl] + ws[2] * o2[:, sl]
            mix_ref[:, sl] = _bf(num / den)
            lse_ref[:, sl] = m + jnp.log(den)
        halo = jnp.where(i == 0, 0.0, uh_ref[...])
        t = lax.broadcasted_iota(jnp.int32, (tm + POOL_HALO, HEAD), 0) + (i * tm - POOL_HALO)
        for g, w in enumerate(POOL_WINDOWS):
            sl = slice(g * HEAD, (g + 1) * HEAD)
            ub = jnp.concatenate([halo[:, sl], u_ref[:, sl]], axis=0)
            cnt = jnp.minimum(t + 1, w).astype(F32)
            d = (_window_sum(ub, g + 1, True) / cnt - ub)[POOL_HALO:]
            db = _bf(d)
            dp_ref[:, sl] = db
            mix_ref[:, KV_W + g * HEAD:KV_W + (g + 1) * HEAD] = _bf(_dot(db, pw_ref[g]) * ps_ref[:, sl])

    tile = pl.BlockSpec((tm, KV_W), lambda i: (i, 0))
    blk = 9 * _nbytes((tm, KV_W), F32) + _nbytes((tm, 2 * KV_W), BF16)
    return pl.pallas_call(
        body, name=name, grid=(S // tm,),
        in_specs=[tile] * 6 + [
            pl.BlockSpec((tm, POOL_W), lambda i: (i, ucol)),
            pl.BlockSpec((POOL_HALO, POOL_W), lambda i: (jnp.maximum(i * hpt - 1, 0), ucol)),
            pl.BlockSpec((len(POOL_WINDOWS), HEAD, HEAD), lambda i: (0, 0, 0)),
            pl.BlockSpec((1, POOL_W), lambda i: (0, 0))],
        out_specs=[pl.BlockSpec((tm, 2 * KV_W), lambda i: (i, 0)), tile, tile],
        out_shape=[jax.ShapeDtypeStruct((S, 2 * KV_W), BF16), jax.ShapeDtypeStruct((S, KV_W), F32),
                   jax.ShapeDtypeStruct((S, POOL_W), BF16)],
        compiler_params=_cparams(1, blk))(*outs, *lses, proj, proj, pool_w, pool_scale)


def _mix_bwd(name, dmix, mix, dpool, pool_w, pool_scale, *, tm, plan=None):
    S = dmix.shape[0]
    hpt = tm // POOL_HALO
    last_halo = S // POOL_HALO - 1
    n_tiles = S // tm

    def body(dm_ref, dh_ref, at_ref, dp_ref, pw_ref, ps_ref, da_ref, dl_ref, du_ref, gw_ref, gs_ref):
        i = pl.program_id(0)

        @pl.when(i == 0)
        def _():
            gw_ref[...] = jnp.zeros_like(gw_ref)
            gs_ref[...] = jnp.zeros_like(gs_ref)

        for h in range(N_HEADS):
            sl = slice(h * HEAD, (h + 1) * HEAD)
            da = dm_ref[:, sl]
            da_ref[:, sl] = da
            dl_ref[:, sl] = jnp.broadcast_to(
                jnp.sum(da * at_ref[:, sl].astype(F32), axis=-1, keepdims=True), (tm, HEAD))
        halo = jnp.where(i == n_tiles - 1, 0.0, dh_ref[...])
        t = lax.broadcasted_iota(jnp.int32, (tm + POOL_HALO, HEAD), 0) + i * tm
        for g, w in enumerate(POOL_WINDOWS):
            sl = slice(g * HEAD, (g + 1) * HEAD)
            dy = jnp.concatenate([dm_ref[:, KV_W + g * HEAD:KV_W + (g + 1) * HEAD], halo[:, sl]], axis=0)
            dys = _bf(dy * ps_ref[:, sl])
            dd = _dot_nt(dys, pw_ref[g])
            cnt = jnp.minimum(t + 1, w).astype(F32)
            du_ref[:, sl] = (_window_sum(dd / cnt, g + 1, False) - dd)[:tm]
            db = dp_ref[:, sl]
            gw_ref[g] += _dot_tn(db, dys[:tm])
            gs_ref[:, sl] += jnp.sum(dy[:tm] * _dot(db, pw_ref[g]), axis=0, keepdims=True)

    tile = pl.BlockSpec((tm, KV_W), lambda i: (i, 0))
    blk = _nbytes((tm, 2 * KV_W), F32) + 6 * _nbytes((tm, KV_W), F32)
    res, got = _pcall(
        body, name=name, grid=(n_tiles,),
        in_specs=[pl.BlockSpec((tm, 2 * KV_W), lambda i: (i, 0)),
                  pl.BlockSpec((POOL_HALO, POOL_W), lambda i: (jnp.minimum((i + 1) * hpt, last_halo), 1)),
                  tile, tile,
                  pl.BlockSpec((len(POOL_WINDOWS), HEAD, HEAD), lambda i: (0, 0, 0)),
                  pl.BlockSpec((1, POOL_W), lambda i: (0, 0))],
        out_specs=[tile, tile, tile,
                   pl.BlockSpec((len(POOL_WINDOWS), HEAD, HEAD), lambda i: (0, 0, 0)),
                   pl.BlockSpec((1, POOL_W), lambda i: (0, 0))],
        out_shape=[jax.ShapeDtypeStruct((S, KV_W), F32)] * 3 + [
            jax.ShapeDtypeStruct((len(POOL_WINDOWS), HEAD, HEAD), F32), jax.ShapeDtypeStruct((1, POOL_W), F32)],
        params=_cparams(1, blk), args=[dmix, dmix, mix, dpool, pool_w, pool_scale], plan=plan)
    return res if plan is None else (res, got)


def _qkv_bwd(name, dqs, dks, dvs, du, proj, ctab, stab, qg, kg, *, tm, plan=None):
    S = proj.shape[0]
    width = Q_W + KV_W

    def body(dq0, dq1, dq2, dk0, dk1, dk2, dv0, dv1, dv2, du_ref, p_ref, c_ref, s_ref, qg_ref, kg_ref,
             dp_ref, gq_ref, gk_ref):
        @pl.when(pl.program_id(0) == 0)
        def _():
            gq_ref[...] = jnp.zeros_like(gq_ref)
            gk_ref[...] = jnp.zeros_like(gk_ref)

        lane = lax.broadcasted_iota(jnp.int32, (tm, HEAD), 1)
        cv = c_ref[...]
        sv = s_ref[...]

        def back(dy, t, g, scale):
            dy = dy * scale
            dn = dy * cv + _swap_halves(dy * sv, lane, True)
            dt, xh = _norm_bwd(dn, t, g)
            return dt, jnp.sum(dn * xh, axis=0, keepdims=True)

        dqr = (dq0, dq1, dq2)
        gq = jnp.zeros((1, HEAD), F32)
        for j in range(Q_W // HEAD):
            grp, h = divmod(j, N_HEADS)
            dt, gj = back(dqr[grp][:, h * HEAD:(h + 1) * HEAD], p_ref[:, j * HEAD:(j + 1) * HEAD], qg_ref[...], SCALE)
            dp_ref[:, j * HEAD:(j + 1) * HEAD] = _bf(dt)
            gq = gq + gj
        gq_ref[...] += gq
        gk = jnp.zeros((1, HEAD), F32)
        for h in range(N_HEADS):
            sl = slice(h * HEAD, (h + 1) * HEAD)
            dt, gj = back(dk0[:, sl] + dk1[:, sl] + dk2[:, sl], p_ref[:, Q_W + h * HEAD:Q_W + (h + 1) * HEAD],
                          kg_ref[...], 1.0)
            dp_ref[:, Q_W + h * HEAD:Q_W + (h + 1) * HEAD] = _bf(dt)
            gk = gk + gj
        gk_ref[...] += gk
        dp_ref[:, width:width + KV_W] = _bf(dv0[...] + dv1[...] + dv2[...])
        dp_ref[:, width + KV_W:] = _bf(du_ref[...])

    tile = pl.BlockSpec((tm, KV_W), lambda i: (i, 0))
    vec = pl.BlockSpec((1, HEAD), lambda i: (0, 0))
    rot = pl.BlockSpec((tm, HEAD), lambda i: (i, 0))
    blk = 10 * _nbytes((tm, KV_W), F32) + _nbytes((tm, width), F32) + _nbytes((tm, IN_W), BF16)
    res, got = _pcall(
        body, name=name, grid=(S // tm,),
        in_specs=[tile] * 10 + [pl.BlockSpec((tm, width), lambda i: (i, 0)), rot, rot, vec, vec],
        out_specs=[pl.BlockSpec((tm, IN_W), lambda i: (i, 0)), vec, vec],
        out_shape=[jax.ShapeDtypeStruct((S, IN_W), BF16), jax.ShapeDtypeStruct((1, HEAD), F32),
                   jax.ShapeDtypeStruct((1, HEAD), F32)],
        params=_cparams(1, blk), args=[*dqs, *dks, *dvs, du, proj, ctab, stab, qg, kg], plan=plan)
    return res if plan is None else (res, got)


def _cross_heads(q_ref, kv_ref, qg, kg, h):
    sl = slice(h * HEAD, (h + 1) * HEAD)
    qr = q_ref[:, sl]
    kr = kv_ref[:, sl]
    qh = qr * _rstd(qr) * qg * SCALE
    kh = kr * _rstd(kr) * kg
    vh = kv_ref[:, X_W + h * HEAD:X_W + (h + 1) * HEAD]
    return qr, _bf(qh), _bf(kh), _bf(vh)


def _cross_fwd(name, qraw, kv, qg, kg, *, tm):
    S = qraw.shape[0]
    M = kv.shape[0]

    def body(q_ref, kv_ref, qg_ref, kg_ref, o_ref):
        for h in range(N_HEADS):
            _, qh, kh, vh = _cross_heads(q_ref, kv_ref, qg_ref[...], kg_ref[...], h)
            s = _dot_nt(qh, kh)
            p = jnp.exp(s - jnp.max(s, axis=-1, keepdims=True))
            l = jnp.sum(p, axis=-1, keepdims=True)
            o_ref[:, h * HEAD:(h + 1) * HEAD] = _bf(_dot(_bf(p), vh) / l)

    vec = pl.BlockSpec((1, HEAD), lambda i: (0, 0))
    blk = 2 * _nbytes((tm, X_W), F32) + _nbytes((M, 2 * X_W), F32) + 4 * _nbytes((tm, M), F32)
    return pl.pallas_call(
        body, name=name, grid=(S // tm,),
        in_specs=[pl.BlockSpec((tm, X_W), lambda i: (i, 0)), pl.BlockSpec((M, 2 * X_W), lambda i: (0, 0)), vec, vec],
        out_specs=pl.BlockSpec((tm, X_W), lambda i: (i, 0)),
        out_shape=jax.ShapeDtypeStruct((S, X_W), BF16),
        compiler_params=_cparams(1, blk))(qraw, kv, qg, kg)


def _cross_bwd(name, do, qraw, kv, qg, kg, *, tm, plan=None):
    S = qraw.shape[0]
    M = kv.shape[0]

    def body(do_ref, q_ref, kv_ref, qg_ref, kg_ref, dq_ref, dk_ref, dv_ref, gq_ref):
        @pl.when(pl.program_id(0) == 0)
        def _():
            dk_ref[...] = jnp.zeros_like(dk_ref)
            dv_ref[...] = jnp.zeros_like(dv_ref)
            gq_ref[...] = jnp.zeros_like(gq_ref)

        gq = jnp.zeros((1, HEAD), F32)
        for h in range(N_HEADS):
            sl = slice(h * HEAD, (h + 1) * HEAD)
            qr, qh, kh, vh = _cross_heads(q_ref, kv_ref, qg_ref[...], kg_ref[...], h)
            doh = _bf(do_ref[:, sl])
            s = _dot_nt(qh, kh)
            p = jnp.exp(s - jnp.max(s, axis=-1, keepdims=True))
            p = p / jnp.sum(p, axis=-1, keepdims=True)
            pb = _bf(p)
            dp = _dot_nt(doh, vh)
            ds = _bf(p * (dp - jnp.sum(dp * p, axis=-1, keepdims=True)))
            dv_ref[:, sl] += _dot_tn(pb, doh)
            dk_ref[:, sl] += _dot_tn(ds, qh)
            dn = _dot(ds, kh) * SCALE
            dt, xh = _norm_bwd(dn, qr, qg_ref[...])
            dq_ref[:, sl] = _bf(dt)
            gq = gq + jnp.sum(dn * xh, axis=0, keepdims=True)
        gq_ref[...] += gq

    vec = pl.BlockSpec((1, HEAD), lambda i: (0, 0))
    acc = pl.BlockSpec((M, X_W), lambda i: (0, 0))
    blk = 3 * _nbytes((tm, X_W), F32) + 3 * _nbytes((M, 2 * X_W), F32) + 6 * _nbytes((tm, M), F32)
    res, got = _pcall(
        body, name=name, grid=(S // tm,),
        in_specs=[pl.BlockSpec((tm, X_W), lambda i: (i, 0)), pl.BlockSpec((tm, X_W), lambda i: (i, 0)),
                  pl.BlockSpec((M, 2 * X_W), lambda i: (0, 0)), vec, vec],
        out_specs=[pl.BlockSpec((tm, X_W), lambda i: (i, 0)), acc, acc, vec],
        out_shape=[jax.ShapeDtypeStruct((S, X_W), BF16), jax.ShapeDtypeStruct((M, X_W), F32),
                   jax.ShapeDtypeStruct((M, X_W), F32), jax.ShapeDtypeStruct((1, HEAD), F32)],
        params=_cparams(1, blk), args=[do, qraw, kv, qg, kg], plan=plan)
    return res if plan is None else (res, got)


def _cross_kv_bwd(name, dkn, dv, kv, kg):
    M = kv.shape[0]

    def body(dk_ref, dv_ref, kv_ref, kg_ref, o_ref, g_ref):
        gk = jnp.zeros((1, HEAD), F32)
        for h in range(N_HEADS):
            sl = slice(h * HEAD, (h + 1) * HEAD)
            dn = dk_ref[:, sl]
            dt, xh = _norm_bwd(dn, kv_ref[:, sl], kg_ref[...])
            o_ref[:, sl] = _bf(dt)
            gk = gk + jnp.sum(dn * xh, axis=0, keepdims=True)
        o_ref[:, X_W:] = _bf(dv_ref[...])
        g_ref[...] = gk

    full = lambda shape: pl.BlockSpec(shape, lambda i: (0,) * len(shape))
    return pl.pallas_call(
        body, name=name, grid=(1,),
        in_specs=[full((M, X_W)), full((M, X_W)), full((M, 2 * X_W)), full((1, HEAD))],
        out_specs=[full((M, 2 * X_W)), full((1, HEAD))],
        out_shape=[jax.ShapeDtypeStruct((M, 2 * X_W), BF16), jax.ShapeDtypeStruct((1, HEAD), F32)],
        compiler_params=_cparams(1, 6 * _nbytes((M, 2 * X_W), F32)))(dkn, dv, kv, kg)


def _rope_tables(positions):
    inv_freq = ROPE_THETA ** (-jnp.arange(0, ROT_DIM, 2, dtype=F32) / ROT_DIM)
    ang = positions.astype(F32)[:, None] * inv_freq
    cos, sin = jnp.cos(ang), jnp.sin(ang)
    S = positions.shape[0]
    ctab = jnp.concatenate([cos, cos, jnp.ones((S, HEAD - ROT_DIM), F32)], axis=-1)
    stab = jnp.concatenate([-sin, sin, jnp.zeros((S, HEAD - ROT_DIM), F32)], axis=-1)
    return ctab, stab


GATHER_BEHIND_IN_PROJ = ("w_out", "w_cq", "w_ckv", "w_co")
FFN_WEIGHTS = ("w_gate_up", "w_down")


def _hosted(fn, *args, plan=None, **kw):
    if plan is None:
        return fn(*args, **kw), []
    return fn(*args, plan=plan, **kw)


def _local_step(x, mem, positions, target, wb, sm, *, tm=512, place=None):
    S, D = x.shape
    M = mem.shape[0]
    dist = place is not None
    wb = dict(wb)
    ctab, stab = _rope_tables(positions)
    pool_w_b = _bf(sm["pool_w"])

    def gather(names):
        return _GatherPlan([wb[k] for k in names]) if dist else None

    if dist:
        wb["w_in"], = _run_plan("gather_w_in", gather(["w_in"]))
    w_in = wb["w_in"]
    (proj, xn1), got = _hosted(_mm_nn, "in_proj", x, w_in, tm=tm, norm_g=sm["mix_norm_g"],
                               plan=gather(GATHER_BEHIND_IN_PROJ))
    wb.update(zip(GATHER_BEHIND_IN_PROJ, got))
    (qn, kn), got = _hosted(_qk_prep, "qk_prep", proj, ctab, stab, sm["q_norm_g"], sm["k_norm_g"], tm=tm,
                            plan=gather(["w_gate_up"]))
    wb.update(zip(["w_gate_up"], got))
    outs, lses = [], []
    for grp in range(len(DILATIONS)):
        last = grp == len(DILATIONS) - 1
        (o, l), got = _hosted(_attn_fwd, f"attn_fwd{grp}", qn, kn, proj, grp, plan=gather(["w_down"]) if last else None)
        wb.update(zip(["w_down"], got))
        outs.append(o)
        lses.append(l)
    w_out = wb["w_out"].reshape(1, 2 * KV_W, D)
    w_cq = wb["w_cq"].reshape(1, D, X_W)
    w_ckv = wb["w_ckv"].reshape(1, D, 2 * X_W)
    w_co = wb["w_co"]
    w_gu = wb["w_gate_up"]
    w_down = wb["w_down"].reshape(1, D_FF, D)
    cin = w_in.shape[2]
    cco = w_co.shape[2]
    cgu = w_gu.shape[2]
    mix, lse, dpool = _mix_post("mix_post", outs, lses, proj, pool_w_b, sm["pool_scale"], tm=tm)
    h1 = _mm_nn("out_proj", mix, w_out, tm=tm, residual=x)
    cq_raw, hn2 = _mm_nn("cq_proj", h1, w_cq, tm=tm, norm_g=sm["cross_norm_g"])
    kv, mem_n = _mm_nn("ckv_proj", mem, w_ckv, tm=M, norm_g=sm["mem_norm_g"])
    xo = _cross_fwd("cross_fwd", cq_raw, kv, sm["cq_norm_g"], sm["ck_norm_g"], tm=tm)
    h2 = _mm_nn("co_proj", xo, w_co, tm=tm, residual=h1)
    act, gu, hn3 = _ffn_up("ffn_up", h2, w_gu, sm["ffn_norm_g"], tm=tm)
    dy, lsum = _ffn_down_loss("ffn_down_loss", act, w_down, h2, target, tm=tm)
    loss = 0.5 * jnp.sum(lsum) / D

    nS = S // tm
    dgu = _ffn_down_bwd("ffn_down_bwd", dy, w_down, gu, tm=tm, tn=cgu)
    g_down = _mm_tn("g_w_down", act, pl.BlockSpec((tm, cgu), lambda r, c, s: (s, r)), (tm, cgu),
                    dy, pl.BlockSpec((tm, D), lambda r, c, s: (s, 0)), (tm, D),
                    (1, D_FF, D), pl.BlockSpec((None, cgu, D), lambda r, c, s: (0, r, 0)), (cgu, D),
                    (D_FF // cgu, 1, nS))
    dh2, g_ffn_norm = _mm_nt_normbwd_steps(
        "ffn_up_bwd", dgu, pl.BlockSpec((None, tm, cgu), lambda i, p: (p // 2, i, p % 2)), (tm, cgu),
        w_gu, pl.BlockSpec((None, D, cgu), lambda i, p: (p, 0, 0)), (D, cgu), 4, h2, sm["ffn_norm_g"], dy, tm=tm)
    g_gu = _mm_tn("g_w_gate_up", hn3, pl.BlockSpec((tm, D), lambda r, c, s: (s, 0)), (tm, D),
                  dgu, pl.BlockSpec((None, tm, cgu), lambda r, c, s: (c // 2, s, c % 2)), (tm, cgu),
                  (4, D, cgu), pl.BlockSpec((None, D, cgu), lambda r, c, s: (c, 0, 0)), (D, cgu),
                  (1, 4, nS))

    dxo = _mm_nt("co_proj_bwd", dh2, w_co, tm=tm, out_dtype=BF16)
    g_co = _mm_tn("g_w_co", xo, pl.BlockSpec((tm, X_W), lambda r, c, s: (s, 0)), (tm, X_W),
                  dh2, pl.BlockSpec((tm, cco), lambda r, c, s: (s, c)), (tm, cco),
                  (N_CHIPS, X_W, cco), pl.BlockSpec((None, X_W, cco), lambda r, c, s: (c, 0, 0)), (X_W, cco),
                  (1, N_CHIPS, nS))
    full = {"w_gate_up": g_gu, "w_down": g_down.reshape(N_CHIPS, D_FF // N_CHIPS, D)}
    sums = {}

    def swap(names):
        return _SwapPlan([full[k] for k in names]) if dist else None

    def add_halves(names, from_sibling):
        for k, t in zip(names, from_sibling):
            sums[k] = _add_core_halves(f"add_halves_{k}", full[k], t, place[0])

    def exchange(names):
        return _ExchangePlan([sums[k][1] for k in names]) if dist else None

    def sum_chips(names, from_chips):
        return [_sum_chips(f"sum_chips_{k}", sums[k][0], t, place[1]) for k, t in zip(names, from_chips)]

    (dcq, dkn, dvm, g_cq_norm), got = _hosted(_cross_bwd, "cross_bwd", dxo, cq_raw, kv, sm["cq_norm_g"],
                                              sm["ck_norm_g"], tm=tm, plan=swap(FFN_WEIGHTS))
    add_halves(FFN_WEIGHTS, got)
    dkv, g_ck_norm = _cross_kv_bwd("cross_kv_bwd", dkn, dvm, kv, sm["ck_norm_g"])
    dh1, g_cross_norm = _mm_nt_normbwd("cq_proj_bwd", dcq, w_cq, h1, sm["cross_norm_g"], dh2, tm=tm)
    g_cq = _mm_tn("g_w_cq", hn2, pl.BlockSpec((tm, D), lambda r, c, s: (s, 0)), (tm, D),
                  dcq, pl.BlockSpec((tm, X_W), lambda r, c, s: (s, 0)), (tm, X_W),
                  (1, D, X_W), pl.BlockSpec((None, D, X_W), lambda r, c, s: (0, 0, 0)), (D, X_W), (1, 1, nS))
    _, g_mem_norm = _mm_nt_normbwd("ckv_proj_bwd", dkv, w_ckv, mem, sm["mem_norm_g"], None, tm=M)
    g_ckv = _mm_tn("g_w_ckv", mem_n, pl.BlockSpec((M, D), lambda r, c, s: (0, 0)), (M, D),
                   dkv, pl.BlockSpec((M, 2 * X_W), lambda r, c, s: (0, 0)), (M, 2 * X_W),
                   (1, D, 2 * X_W), pl.BlockSpec((None, D, 2 * X_W), lambda r, c, s: (0, 0, 0)), (D, 2 * X_W),
                   (1, 1, 1))

    dmix = _mm_nt("out_proj_bwd", dh1, w_out, tm=tm, out_dtype=F32)
    g_out = _mm_tn("g_w_out", mix, pl.BlockSpec((tm, 2 * KV_W), lambda r, c, s: (s, 0)), (tm, 2 * KV_W),
                   dh1, pl.BlockSpec((tm, D), lambda r, c, s: (s, 0)), (tm, D),
                   (1, 2 * KV_W, D), pl.BlockSpec((None, 2 * KV_W, D), lambda r, c, s: (0, 0, 0)), (2 * KV_W, D),
                   (1, 1, nS))
    full.update({
        "w_out": g_out.reshape(N_CHIPS, 2 * KV_W // N_CHIPS, D),
        "w_cq": g_cq.reshape(N_CHIPS, D // N_CHIPS, X_W),
        "w_ckv": g_ckv.reshape(N_CHIPS, D // N_CHIPS, 2 * X_W),
        "w_co": g_co,
    })
    mixer = GATHER_BEHIND_IN_PROJ
    (dattn, delta, du, g_pool_w, g_pool_scale), got = _hosted(
        _mix_bwd, "mix_bwd", dmix, mix, dpool, pool_w_b, sm["pool_scale"], tm=tm, plan=swap(mixer))
    add_halves(mixer, got)
    behind_attn = (None, mixer, FFN_WEIGHTS)
    halves = {}
    dqs, dks, dvs = [], [], []
    for grp in range(len(DILATIONS)):
        names = behind_attn[grp]
        (dq, dk, dv), got = _hosted(_attn_bwd, f"attn_bwd{grp}", qn, kn, proj, dattn, lse, delta, grp,
                                    plan=exchange(names) if names else None)
        if dist and names:
            halves.update(zip(names, sum_chips(names, got)))
        dqs.append(dq)
        dks.append(dk)
        dvs.append(dv)
    joined = mixer + FFN_WEIGHTS
    (dproj, g_q_norm, g_k_norm), got = _hosted(
        _qkv_bwd, "qkv_bwd", dqs, dks, dvs, du, proj, ctab, stab, sm["q_norm_g"], sm["k_norm_g"], tm=tm,
        plan=_JoinPlan([halves[k] for k in joined]) if dist else None)
    shards = dict(zip(joined, got))
    full["w_in"] = _mm_tn("g_w_in", xn1, pl.BlockSpec((tm, D), lambda r, c, s: (s, 0)), (tm, D),
                          dproj, pl.BlockSpec((tm, cin), lambda r, c, s: (s, c)), (tm, cin),
                          (N_CHIPS, D, cin), pl.BlockSpec((None, D, cin), lambda r, c, s: (c, 0, 0)), (D, cin),
                          (1, N_CHIPS, nS))
    (dx, g_mix_norm), got = _hosted(_mm_nt_normbwd, "in_proj_bwd", dproj, w_in, x, sm["mix_norm_g"], dh1, tm=tm,
                                    plan=swap(["w_in"]))
    if dist:
        add_halves(["w_in"], got)
        half, = sum_chips(["w_in"], _run_plan("exchange_w_in", exchange(["w_in"])))
        shards["w_in"], = _run_plan("join_w_in", _JoinPlan([half]))
    big = shards if dist else full
    small = {
        "mix_norm_g": g_mix_norm, "q_norm_g": g_q_norm, "k_norm_g": g_k_norm, "pool_w": g_pool_w,
        "pool_scale": g_pool_scale, "cross_norm_g": g_cross_norm, "mem_norm_g": g_mem_norm,
        "cq_norm_g": g_cq_norm, "ck_norm_g": g_ck_norm, "ffn_norm_g": g_ffn_norm,
    }
    return loss, dx, big, small


BIG = ("w_in", "w_out", "w_cq", "w_ckv", "w_co", "w_gate_up", "w_down")
SMALL = ("mix_norm_g", "q_norm_g", "k_norm_g", "pool_w", "pool_scale", "cross_norm_g", "mem_norm_g",
         "cq_norm_g", "ck_norm_g", "ffn_norm_g")
WEIGHTS = ("mix_norm_g", "w_in", "q_norm_g", "k_norm_g", "pool_w", "pool_scale", "w_out", "cross_norm_g",
           "mem_norm_g", "w_cq", "w_ckv", "cq_norm_g", "ck_norm_g", "w_co", "ffn_norm_g", "w_gate_up", "w_down")


def _cast_piece(name, w, k_arr):
    R, C = w.shape
    hr = R // 2

    def body(k_ref, w_ref, o_ref):
        o_ref[...] = _bf(w_ref[...])

    return pl.pallas_call(
        body, name=name,
        grid_spec=pltpu.PrefetchScalarGridSpec(
            num_scalar_prefetch=1, grid=(2,),
            in_specs=[pl.BlockSpec((hr, C), lambda i, k: (i, 0))],
            out_specs=pl.BlockSpec((None, hr, C), lambda i, k: (k[0], i, 0))),
        out_shape=jax.ShapeDtypeStruct((N_CHIPS, R, C), BF16),
        compiler_params=_cparams(1, 2 * _nbytes((hr, C), F32)))(k_arr, w)


def _add_core_halves(name, g, t, c_arr):
    P, R, C = g.shape
    hr = R // 2

    def body(c_ref, g_ref, t_ref, o_ref, ob_ref):
        tot = g_ref[...] + t_ref[...]
        o_ref[...] = tot
        ob_ref[...] = _bf(tot)

    piece = pl.BlockSpec((None, hr, C), lambda p, c: (p, 0, 0))
    return pl.pallas_call(
        body, name=name,
        grid_spec=pltpu.PrefetchScalarGridSpec(
            num_scalar_prefetch=1, grid=(P,),
            in_specs=[pl.BlockSpec((None, hr, C), lambda p, c: (p, c[0], 0)), piece],
            out_specs=[piece, piece]),
        out_shape=[jax.ShapeDtypeStruct((P, hr, C), F32), jax.ShapeDtypeStruct((P, hr, C), BF16)],
        compiler_params=_cparams(1, 4 * _nbytes((hr, C), F32)))(c_arr, g, t)


def _sum_chips(name, own, got, kc_arr):
    P, hr, C = own.shape

    def body(kc_ref, o_ref, g_ref, r_ref):
        r_ref[...] = ((o_ref[...] + g_ref[0].astype(F32)) + g_ref[1].astype(F32)) + g_ref[2].astype(F32)

    return pl.pallas_call(
        body, name=name,
        grid_spec=pltpu.PrefetchScalarGridSpec(
            num_scalar_prefetch=1, grid=(1,),
            in_specs=[pl.BlockSpec((None, hr, C), lambda i, kc: (kc[0], 0, 0)),
                      pl.BlockSpec((N_CHIPS - 1, hr, C), lambda i, kc: (0, 0, 0))],
            out_specs=pl.BlockSpec((hr, C), lambda i, kc: (kc[1], 0))),
        out_shape=jax.ShapeDtypeStruct((2 * hr, C), F32),
        compiler_params=_cparams(1, 5 * _nbytes((hr, C), F32)))(kc_arr, own, got)


def _allreduce_small(v):
    R, C = v.shape
    n_dev = 8
    flips = [(dx, dy, dc) for dx in (0, 1) for dy in (0, 1) for dc in (0, 1)][1:]

    def body(v_ref, o_ref, slots, send, recv):
        x, y, c, _ = _place()
        me = 4 * x + 2 * y + c
        cps = []
        for q, (dx, dy, dc) in enumerate(flips):
            to = (x + dx - 2 * x * dx, y + dy - 2 * y * dy, c + dc - 2 * c * dc)
            cp = pltpu.make_async_remote_copy(src_ref=v_ref, dst_ref=slots.at[me], send_sem=send.at[q],
                                              recv_sem=recv.at[q], device_id=to, device_id_type=MESH)
            cp.start()
            cps.append(cp)
        slots[me] = v_ref[...]
        for cp in cps:
            cp.wait_recv()
        acc = slots[0]
        for d in range(1, n_dev):
            acc = acc + slots[d]
        o_ref[...] = acc
        for cp in cps:
            cp.wait_send()

    return pl.pallas_call(
        body, name="allreduce_small",
        in_specs=[pl.BlockSpec(memory_space=pltpu.VMEM)], out_specs=pl.BlockSpec(memory_space=pltpu.VMEM),
        out_shape=jax.ShapeDtypeStruct((R, C), F32),
        scratch_shapes=[pltpu.VMEM((n_dev, R, C), F32), pltpu.SemaphoreType.DMA((n_dev - 1,)),
                        pltpu.SemaphoreType.DMA((n_dev - 1,))],
    )(v)


def _adamw(name, w, g, m, v, *, tr):
    R, C = w.shape

    def body(w_ref, g_ref, m_ref, v_ref, d_ref, nm_ref, nv_ref):
        gv = g_ref[...]
        nm = ADAM_B1 * m_ref[...] + (1.0 - ADAM_B1) * gv
        nv = ADAM_B2 * v_ref[...] + (1.0 - ADAM_B2) * (gv * gv)
        m_hat = nm / (1.0 - ADAM_B1 ** ADAM_STEP)
        v_hat = nv / (1.0 - ADAM_B2 ** ADAM_STEP)
        d_ref[...] = -ADAM_LR * (m_hat / (jnp.sqrt(v_hat) + ADAM_EPS) + ADAM_WD * w_ref[...])
        nm_ref[...] = nm
        nv_ref[...] = nv

    tile = pl.BlockSpec((tr, C), lambda i: (i, 0))
    return pl.pallas_call(
        body, name=name, grid=(R // tr,), in_specs=[tile] * 4, out_specs=[tile] * 3,
        out_shape=[jax.ShapeDtypeStruct((R, C), F32)] * 3,
        compiler_params=_cparams(1, 7 * _nbytes((tr, C), F32)))(w, g, m, v)


def _pack_small(d):
    parts = []
    for name in SMALL:
        a = d[name].reshape(-1, HEAD)
        pad = (-a.shape[0]) % 8
        parts.append(jnp.pad(a, ((0, pad), (0, 0))))
    return jnp.concatenate(parts, axis=0)


def _unpack_small(packed, like):
    out = {}
    row = 0
    for name in SMALL:
        shape = like[name].shape
        rows = like[name].size // HEAD
        out[name] = packed[row:row + rows].reshape(shape)
        row += rows + (-rows) % 8
    return out


def kernel(x, mem, positions, mix_norm_g, w_in, q_norm_g, k_norm_g, pool_w, pool_scale, w_out, cross_norm_g, mem_norm_g, w_cq, w_ckv, cq_norm_g, ck_norm_g, w_co, ffn_norm_g, w_gate_up, w_down, loss_target, m_mix_norm_g, m_w_in, m_q_norm_g, m_k_norm_g, m_pool_w, m_pool_scale, m_w_out, m_cross_norm_g, m_mem_norm_g, m_w_cq, m_w_ckv, m_cq_norm_g, m_ck_norm_g, m_w_co, m_ffn_norm_g, m_w_gate_up, m_w_down, v_mix_norm_g, v_w_in, v_q_norm_g, v_k_norm_g, v_pool_w, v_pool_scale, v_w_out, v_cross_norm_g, v_mem_norm_g, v_w_cq, v_w_ckv, v_cq_norm_g, v_ck_norm_g, v_w_co, v_ffn_norm_g, v_w_gate_up, v_w_down):
    w = dict(mix_norm_g=mix_norm_g, w_in=w_in, q_norm_g=q_norm_g, k_norm_g=k_norm_g, pool_w=pool_w,
             pool_scale=pool_scale, w_out=w_out, cross_norm_g=cross_norm_g, mem_norm_g=mem_norm_g, w_cq=w_cq,
             w_ckv=w_ckv, cq_norm_g=cq_norm_g, ck_norm_g=ck_norm_g, w_co=w_co, ffn_norm_g=ffn_norm_g,
             w_gate_up=w_gate_up, w_down=w_down)
    m = dict(mix_norm_g=m_mix_norm_g, w_in=m_w_in, q_norm_g=m_q_norm_g, k_norm_g=m_k_norm_g, pool_w=m_pool_w,
             pool_scale=m_pool_scale, w_out=m_w_out, cross_norm_g=m_cross_norm_g, mem_norm_g=m_mem_norm_g,
             w_cq=m_w_cq, w_ckv=m_w_ckv, cq_norm_g=m_cq_norm_g, ck_norm_g=m_ck_norm_g, w_co=m_w_co,
             ffn_norm_g=m_ffn_norm_g, w_gate_up=m_w_gate_up, w_down=m_w_down)
    v = dict(mix_norm_g=v_mix_norm_g, w_in=v_w_in, q_norm_g=v_q_norm_g, k_norm_g=v_k_norm_g, pool_w=v_pool_w,
             pool_scale=v_pool_scale, w_out=v_w_out, cross_norm_g=v_cross_norm_g, mem_norm_g=v_mem_norm_g,
             w_cq=v_w_cq, w_ckv=v_w_ckv, cq_norm_g=v_cq_norm_g, ck_norm_g=v_ck_norm_g, w_co=v_w_co,
             ffn_norm_g=v_ffn_norm_g, w_gate_up=v_w_gate_up, w_down=v_w_down)

    c_arr = lax.axis_index("c").astype(jnp.int32).reshape(1)
    k_arr = (2 * lax.axis_index("x") + lax.axis_index("y")).astype(jnp.int32).reshape(1)
    kc_arr = jnp.concatenate([k_arr, c_arr])
    wb = {k: _cast_piece(f"cast_{k}", w[k][0], k_arr) for k in BIG}
    sm = {k: (w[k][0] if k == "pool_w" else w[k]) for k in SMALL}
    loss_part, dx, gshard, gsmall = _local_step(x[0], mem[0], positions[0], loss_target[0], wb, sm,
                                                place=(c_arr, kc_arr))
    loss = lax.psum(loss_part, ("x", "y", "c"))

    gsm = _unpack_small(_allreduce_small(_pack_small(gsmall)), sm)

    grads, deltas, new_m, new_v = {}, {}, {}, {}
    for k in BIG:
        shard = w[k][0]
        tr = shard.shape[0] // 4
        d, nm, nv = _adamw(f"adamw_{k}", shard, gshard[k], m[k][0], v[k][0], tr=tr)
        grads[k], deltas[k], new_m[k], new_v[k] = gshard[k][None], d[None], nm[None], nv[None]
    smw = {k: (w[k][0] if k == "pool_w" else w[k]) for k in SMALL}
    smm = {k: (m[k][0] if k == "pool_w" else m[k]) for k in SMALL}
    smv = {k: (v[k][0] if k == "pool_w" else v[k]) for k in SMALL}
    pw, pg, pm, pv = _pack_small(smw), _pack_small(gsm), _pack_small(smm), _pack_small(smv)
    d, nm, nv = _adamw("adamw_small", pw, pg, pm, pv, tr=pw.shape[0])
    for dst, packed in ((deltas, d), (new_m, nm), (new_v, nv)):
        un = _unpack_small(packed, sm)
        for k in SMALL:
            dst[k] = un[k].reshape(w[k].shape)
    for k in SMALL:
        grads[k] = gsm[k].reshape(w[k].shape)

    return (loss, dx[None], *[grads[k] for k in WEIGHTS], *[deltas[k] for k in WEIGHTS],
            *[new_m[k] for k in WEIGHTS], *[new_v[k] for k in WEIGHTS])
```

```python
import functools

import jax
import jax.numpy as jnp
from jax import lax
from jax.experimental import pallas as pl
from jax.experimental.pallas import tpu as pltpu

F32 = jnp.float32
BF16 = jnp.bfloat16
MESH = pl.DeviceIdType.MESH
ANY = pl.BlockSpec(memory_space=pl.ANY)

D_MODEL = 1024
HEAD = 128
N_HEADS = 4
DILATIONS = (1, 4, 16)
BLOCK = 128
Q_W = 1536
KV_W = 512
POOL_W = 512
POOL_WINDOWS = (2, 4, 8, 16)
POOL_HALO = 16
IN_W = 3072
ROT_DIM = 32
ROT_HALF = 16
ROPE_THETA = 500000.0
X_W = 512
D_FF = 2816
EPS = 1e-6
NEG_INF = -1e30
SCALE = HEAD ** -0.5
N_CHIPS = 4

ADAM_LR = 0.001
ADAM_B1 = 0.9
ADAM_B2 = 0.999
ADAM_EPS = 1e-08
ADAM_WD = 0.01
ADAM_STEP = 10

VMEM_BYTES_V7X = 64 * 2 ** 20
VMEM_LIMIT_MAX = 56 * 2 ** 20
VMEM_LIMIT_MIN = 24 * 2 ** 20


def _nbytes(shape, dtype):
    n = 1
    for s in shape:
        n *= s
    return n * jnp.dtype(dtype).itemsize


def _cparams(n_axes, block_bytes, scratch_bytes=0):
    est = 2 * (2 * block_bytes + scratch_bytes)
    lim = int(min(VMEM_LIMIT_MAX, max(VMEM_LIMIT_MIN, est)))
    return pltpu.CompilerParams(dimension_semantics=("arbitrary",) * n_axes, vmem_limit_bytes=lim)


def _bf(v):
    return v.astype(BF16)


def _dot(a, b):
    return jnp.dot(a, b, preferred_element_type=F32)


def _dot_nt(a, b):
    return lax.dot_general(a, b, (((1,), (1,)), ((), ())), preferred_element_type=F32)


def _dot_tn(a, b):
    return lax.dot_general(a, b, (((0,), (0,)), ((), ())), preferred_element_type=F32)


def _rstd(v):
    return lax.rsqrt(jnp.mean(v * v, axis=-1, keepdims=True) + EPS)


def _norm_bwd(dy, xv, g):
    r = _rstd(xv)
    xh = xv * r
    dxh = dy * g
    dx = r * (dxh - xh * jnp.mean(dxh * xh, axis=-1, keepdims=True))
    return dx, xh


def _place():
    x, y, c = lax.axis_index("x"), lax.axis_index("y"), lax.axis_index("c")
    other_chips = [(1 - x, y), (x, 1 - y), (1 - x, 1 - y)]
    return x, y, c, other_chips


class _Plan:
    ins = ()
    out_shapes = ()
    aliases = {}
    sem_shapes = ()

    def copies(self, ins, outs, sems):
        raise NotImplementedError

    def begin(self, ins, outs, sems):
        for cp in self.copies(ins, outs, sems):
            cp.start()

    def finish(self, ins, outs, sems):
        for cp in self.copies(ins, outs, sems):
            cp.wait()


class _GatherPlan(_Plan):
    def __init__(self, bufs):
        n = len(bufs)
        self.ins = list(bufs)
        self.out_shapes = [_sds(b.shape, b.dtype) for b in bufs]
        self.aliases = {i: i for i in range(n)}
        self.sem_shapes = [pltpu.SemaphoreType.DMA((n, 6)), pltpu.SemaphoreType.DMA((n, 6))]

    def _parts(self, outs, sems):
        send, recv = sems
        x, y, c, chips = _place()

        def half(i, piece, which):
            hr = outs[i].shape[1] // 2
            return outs[i].at[piece, pl.ds(which * hr, hr), :]

        def copy(i, k, ref, to):
            return pltpu.make_async_remote_copy(src_ref=ref, dst_ref=ref, send_sem=send.at[i, k], recv_sem=recv.at[i, k],
                                                device_id=to, device_id_type=MESH)

        return x, y, c, chips, half, copy

    def begin(self, ins, outs, sems):
        x, y, c, chips, half, copy = self._parts(outs, sems)
        for i in range(len(outs)):
            for j, (cx, cy) in enumerate(chips):
                copy(i, j, half(i, 2 * x + y, c), (cx, cy, c)).start()

    def finish(self, ins, outs, sems):
        x, y, c, chips, half, copy = self._parts(outs, sems)
        sib = (x, y, 1 - c)
        n = len(outs)
        for i in range(n):
            for j, (cx, cy) in enumerate(chips):
                piece = half(i, 2 * cx + cy, c)
                copy(i, j, piece, (cx, cy, c)).wait_recv()
                copy(i, 3 + j, piece, sib).start()
        for i in range(n):
            for j, (cx, cy) in enumerate(chips):
                copy(i, 3 + j, half(i, 2 * cx + cy, 1 - c), sib).wait_recv()
        for i in range(n):
            for j, (cx, cy) in enumerate(chips):
                copy(i, j, half(i, 2 * x + y, c), (cx, cy, c)).wait_send()
                copy(i, 3 + j, half(i, 2 * cx + cy, c), sib).wait_send()


class _SwapPlan(_Plan):
    def __init__(self, grads):
        n = len(grads)
        self.ins = list(grads)
        self.out_shapes = [_sds((g.shape[0], g.shape[1] // 2, g.shape[2]), g.dtype) for g in grads]
        self.sem_shapes = [pltpu.SemaphoreType.DMA((n,)), pltpu.SemaphoreType.DMA((n,))]

    def copies(self, ins, outs, sems):
        send, recv = sems
        x, y, c, _ = _place()
        cps = []
        for i in range(len(ins)):
            hr = ins[i].shape[1] // 2
            cps.append(pltpu.make_async_remote_copy(
                src_ref=ins[i].at[:, pl.ds((1 - c) * hr, hr), :], dst_ref=outs[i], send_sem=send.at[i],
                recv_sem=recv.at[i], device_id=(x, y, 1 - c), device_id_type=MESH))
        return cps


class _ExchangePlan(_Plan):
    def __init__(self, sums):
        n = len(sums)
        self.ins = list(sums)
        self.out_shapes = [_sds((N_CHIPS - 1,) + s.shape[1:], s.dtype) for s in sums]
        self.sem_shapes = [pltpu.SemaphoreType.DMA((n, 3)), pltpu.SemaphoreType.DMA((n, 3))]

    def copies(self, ins, outs, sems):
        send, recv = sems
        x, y, c, chips = _place()
        cps = []
        for i in range(len(ins)):
            for j, (cx, cy) in enumerate(chips):
                cps.append(pltpu.make_async_remote_copy(
                    src_ref=ins[i].at[2 * cx + cy], dst_ref=outs[i].at[j], send_sem=send.at[i, j],
                    recv_sem=recv.at[i, j], device_id=(cx, cy, c), device_id_type=MESH))
        return cps


class _JoinPlan(_Plan):
    def __init__(self, shards):
        n = len(shards)
        self.ins = list(shards)
        self.out_shapes = [_sds(s.shape, s.dtype) for s in shards]
        self.aliases = {i: i for i in range(n)}
        self.sem_shapes = [pltpu.SemaphoreType.DMA((n,)), pltpu.SemaphoreType.DMA((n,))]

    def copies(self, ins, outs, sems):
        send, recv = sems
        x, y, c, _ = _place()
        cps = []
        for i in range(len(outs)):
            hr = outs[i].shape[0] // 2
            mine = outs[i].at[pl.ds(c * hr, hr), :]
            cps.append(pltpu.make_async_remote_copy(src_ref=mine, dst_ref=mine, send_sem=send.at[i], recv_sem=recv.at[i],
                                                    device_id=(x, y, 1 - c), device_id_type=MESH))
        return cps


def _run_plan(name, plan):
    n_in, n_out = len(plan.ins), len(plan.out_shapes)

    def body(*refs):
        ins, outs, sems = refs[:n_in], refs[n_in:n_in + n_out], refs[n_in + n_out:]
        plan.begin(ins, outs, sems)
        plan.finish(ins, outs, sems)

    return pl.pallas_call(
        body, name=name, in_specs=[ANY] * n_in, out_specs=[ANY] * n_out, out_shape=list(plan.out_shapes),
        input_output_aliases=dict(plan.aliases), scratch_shapes=list(plan.sem_shapes))(*plan.ins)


HBM_PIN_BYTES = 1 << 20


def _sds(shape, dtype):
    if _nbytes(shape, dtype) >= HBM_PIN_BYTES:
        return pltpu.HBM(shape, dtype)
    return jax.ShapeDtypeStruct(shape, dtype)


def _pin(args):
    return [pltpu.with_memory_space_constraint(a, pltpu.HBM) if _nbytes(a.shape, a.dtype) >= HBM_PIN_BYTES else a
            for a in args]


def _pcall(body, *, name, grid, in_specs, out_specs, out_shape, params, args, scratch_shapes=(), plan=None):
    in_specs, out_specs, out_shape, scratch = list(in_specs), list(out_specs), list(out_shape), list(scratch_shapes)
    args = _pin(args)
    if plan is None:
        res = pl.pallas_call(body, name=name, grid=grid, in_specs=in_specs, out_specs=out_specs, out_shape=out_shape,
                             scratch_shapes=scratch, compiler_params=params)(*args)
        return list(res), []
    ni, no, ns = len(in_specs), len(out_specs), len(scratch)
    pi, po = len(plan.ins), len(plan.out_shapes)

    def wrapped(*refs):
        ins, pins = refs[:ni], refs[ni:ni + pi]
        outs, pouts = refs[ni + pi:ni + pi + no], refs[ni + pi + no:ni + pi + no + po]
        scr, psems = refs[ni + pi + no + po:ni + pi + no + po + ns], refs[ni + pi + no + po + ns:]
        ids = [pl.program_id(a) for a in range(len(grid))]
        first = functools.reduce(jnp.logical_and, [i == 0 for i in ids])
        last = functools.reduce(jnp.logical_and, [i == g - 1 for i, g in zip(ids, grid)])

        @pl.when(first)
        def _():
            plan.begin(pins, pouts, psems)

        body(*ins, *outs, *scr)

        @pl.when(last)
        def _():
            plan.finish(pins, pouts, psems)

    res = pl.pallas_call(
        wrapped, name=name, grid=grid, in_specs=in_specs + [ANY] * pi, out_specs=out_specs + [ANY] * po,
        out_shape=out_shape + list(plan.out_shapes), scratch_shapes=scratch + list(plan.sem_shapes),
        input_output_aliases={ni + a: no + b for a, b in plan.aliases.items()},
        compiler_params=params)(*args, *plan.ins)
    return list(res[:no]), list(res[no:])


def _mm_nn(name, a, w3, *, tm, norm_g=None, residual=None, out_dtype=F32, plan=None):
    M, K = a.shape
    P, Kw, C = w3.shape
    assert Kw == K and M % tm == 0
    N = P * C
    has_norm = norm_g is not None
    has_res = residual is not None

    def body(*refs):
        refs = list(refs)
        a_ref = refs.pop(0)
        w_ref = refs.pop(0)
        g_ref = refs.pop(0) if has_norm else None
        r_ref = refs.pop(0) if has_res else None
        o_ref = refs.pop(0)
        xn_ref = refs.pop(0) if has_norm else None
        if has_norm:
            av = a_ref[...].astype(F32)
            ab = _bf(av * _rstd(av) * g_ref[...])
            xn_ref[...] = ab
        else:
            ab = _bf(a_ref[...])
        for p in range(P):
            acc = _dot(ab, w_ref[p])
            if has_res:
                acc = acc + r_ref[:, p * C:(p + 1) * C]
            o_ref[:, p * C:(p + 1) * C] = acc.astype(o_ref.dtype)

    in_specs = [pl.BlockSpec((tm, K), lambda i: (i, 0)), pl.BlockSpec((P, K, C), lambda i: (0, 0, 0))]
    args = [a, w3]
    if has_norm:
        in_specs.append(pl.BlockSpec((1, K), lambda i: (0, 0)))
        args.append(norm_g)
    if has_res:
        in_specs.append(pl.BlockSpec((tm, N), lambda i: (i, 0)))
        args.append(residual)
    out_shape = [_sds((M, N), out_dtype)]
    out_specs = [pl.BlockSpec((tm, N), lambda i: (i, 0))]
    if has_norm:
        out_shape.append(_sds((M, K), BF16))
        out_specs.append(pl.BlockSpec((tm, K), lambda i: (i, 0)))
    blk = (_nbytes((tm, K), a.dtype) + _nbytes((P, K, C), BF16) + 2 * _nbytes((tm, N), F32)
           + _nbytes((tm, K), BF16))
    res, got = _pcall(body, name=name, grid=(M // tm,), in_specs=in_specs, out_specs=out_specs, out_shape=out_shape,
                      params=_cparams(1, blk), args=args, plan=plan)
    res = res if has_norm else res[0]
    return res if plan is None else (res, got)


def _ffn_up(name, h, wgu3, g, *, tm):
    M, K = h.shape
    P, _, C = wgu3.shape
    half = P // 2

    def body(h_ref, wg_ref, wu_ref, g_ref, act_ref, gu_ref, xn_ref, ab_ref):
        @pl.when(pl.program_id(1) == 0)
        def _():
            hv = h_ref[...]
            xn = _bf(hv * _rstd(hv) * g_ref[...])
            ab_ref[...] = xn
            xn_ref[...] = xn
        ab = ab_ref[...]
        gate = _dot(ab, wg_ref[...])
        up = _dot(ab, wu_ref[...])
        act_ref[...] = _bf(gate * (1.0 / (1.0 + jnp.exp(-gate))) * up)
        gu_ref[0] = _bf(gate)
        gu_ref[1] = _bf(up)

    blk = (_nbytes((tm, K), F32) + 2 * _nbytes((K, C), BF16) + 3 * _nbytes((tm, C), BF16)
           + 2 * _nbytes((tm, C), F32) + _nbytes((tm, K), BF16))
    return pl.pallas_call(
        body, name=name, grid=(M // tm, half),
        in_specs=[pl.BlockSpec((tm, K), lambda i, j: (i, 0)),
                  pl.BlockSpec((None, K, C), lambda i, j: (j, 0, 0)),
                  pl.BlockSpec((None, K, C), lambda i, j: (j + half, 0, 0)),
                  pl.BlockSpec((1, K), lambda i, j: (0, 0))],
        out_specs=[pl.BlockSpec((tm, C), lambda i, j: (i, j)),
                   pl.BlockSpec((2, tm, C), lambda i, j: (0, i, j)),
                   pl.BlockSpec((tm, K), lambda i, j: (i, 0))],
        out_shape=[_sds((M, half * C), BF16),
                   _sds((2, M, half * C), BF16),
                   _sds((M, K), BF16)],
        scratch_shapes=[pltpu.VMEM((tm, K), BF16)],
        compiler_params=_cparams(2, blk, _nbytes((tm, K), BF16)))(*_pin([h, wgu3, wgu3, g]))


def _ffn_down_loss(name, act, wd3, h, target, *, tm):
    M, K = act.shape
    _, _, N = wd3.shape

    def body(a_ref, w_ref, h_ref, t_ref, dy_ref, ls_ref):
        err = _dot(a_ref[...], w_ref[...]) + h_ref[...] - t_ref[...]
        dy_ref[...] = err * (1.0 / N)

        @pl.when(pl.program_id(0) == 0)
        def _():
            ls_ref[...] = jnp.zeros_like(ls_ref)
        ls_ref[...] += jnp.sum(err * err, axis=0, keepdims=True)

    blk = _nbytes((tm, K), BF16) + _nbytes((K, N), BF16) + 4 * _nbytes((tm, N), F32)
    return pl.pallas_call(
        body, name=name, grid=(M // tm,),
        in_specs=[pl.BlockSpec((tm, K), lambda i: (i, 0)),
                  pl.BlockSpec((None, K, N), lambda i: (0, 0, 0)),
                  pl.BlockSpec((tm, N), lambda i: (i, 0)),
                  pl.BlockSpec((tm, N), lambda i: (i, 0))],
        out_specs=[pl.BlockSpec((tm, N), lambda i: (i, 0)),
                   pl.BlockSpec((1, N), lambda i: (0, 0))],
        out_shape=[_sds((M, N), F32), _sds((1, N), F32)],
        compiler_params=_cparams(1, blk))(*_pin([act, wd3, h, target]))


def _nt_pieces(a_ref, w_ref):
    P, _, C = w_ref.shape
    acc = _dot_nt(_bf(a_ref[:, 0:C]), w_ref[0])
    for p in range(1, P):
        acc = acc + _dot_nt(_bf(a_ref[:, p * C:(p + 1) * C]), w_ref[p])
    return acc


def _mm_nt(name, a, w3, *, tm, out_dtype):
    M, N = a.shape
    P, Ko, C = w3.shape
    assert N == P * C and M % tm == 0

    def body(a_ref, w_ref, o_ref):
        o_ref[...] = _nt_pieces(a_ref, w_ref).astype(o_ref.dtype)

    blk = _nbytes((tm, N), a.dtype) + _nbytes((P, Ko, C), BF16) + 2 * _nbytes((tm, Ko), F32)
    return pl.pallas_call(
        body, name=name, grid=(M // tm,),
        in_specs=[pl.BlockSpec((tm, N), lambda i: (i, 0)), pl.BlockSpec((P, Ko, C), lambda i: (0, 0, 0))],
        out_specs=pl.BlockSpec((tm, Ko), lambda i: (i, 0)),
        out_shape=_sds((M, Ko), out_dtype),
        compiler_params=_cparams(1, blk))(*_pin([a, w3]))


def _mm_nt_normbwd(name, a, w3, h, g, dres, *, tm, plan=None):
    M, D = h.shape
    P, Ko, C = w3.shape
    N = a.shape[1]
    assert N == P * C and Ko == D and M % tm == 0
    has_res = dres is not None

    def body(*refs):
        a_ref, w_ref, h_ref, g_ref = refs[:4]
        r_ref = refs[4] if has_res else None
        dx_ref, dg_ref = refs[-2:]
        dhn = _nt_pieces(a_ref, w_ref)
        dx, xh = _norm_bwd(dhn, h_ref[...], g_ref[...])
        if has_res:
            dx = dx + r_ref[...]
        dx_ref[...] = dx

        @pl.when(pl.program_id(0) == 0)
        def _():
            dg_ref[...] = jnp.zeros_like(dg_ref)
        dg_ref[...] += jnp.sum(dhn * xh, axis=0, keepdims=True)

    row = pl.BlockSpec((tm, D), lambda i: (i, 0))
    vec = pl.BlockSpec((1, D), lambda i: (0, 0))
    in_specs = [pl.BlockSpec((tm, N), lambda i: (i, 0)), pl.BlockSpec((P, Ko, C), lambda i: (0, 0, 0)), row, vec]
    args = [a, w3, h, g]
    if has_res:
        in_specs.append(row)
        args.append(dres)
    blk = _nbytes((tm, N), a.dtype) + _nbytes((P, Ko, C), BF16) + 5 * _nbytes((tm, D), F32)
    res, got = _pcall(body, name=name, grid=(M // tm,), in_specs=in_specs, out_specs=[row, vec],
                      out_shape=[_sds((M, D), F32), _sds((1, D), F32)],
                      params=_cparams(1, blk), args=args, plan=plan)
    return res if plan is None else (res, got)


def _ffn_up_bwd(name, dgu, wgu3, h, g, dres, *, tm):
    M, D = h.shape
    P, _, C = wgu3.shape
    F = dgu.shape[2]
    per = F // C

    def body(a_ref, w_ref, h_ref, g_ref, r_ref, dx_ref, dg_ref):
        dhn = None
        for p in range(P):
            part = _dot_nt(a_ref[p // per, :, (p % per) * C:(p % per + 1) * C], w_ref[p])
            dhn = part if dhn is None else dhn + part
        dx, xh = _norm_bwd(dhn, h_ref[...], g_ref[...])
        dx_ref[...] = dx + r_ref[...]

        @pl.when(pl.program_id(0) == 0)
        def _():
            dg_ref[...] = jnp.zeros_like(dg_ref)
        dg_ref[...] += jnp.sum(dhn * xh, axis=0, keepdims=True)

    row = pl.BlockSpec((tm, D), lambda i: (i, 0))
    vec = pl.BlockSpec((1, D), lambda i: (0, 0))
    blk = _nbytes((2, tm, F), BF16) + _nbytes((P, D, C), BF16) + 5 * _nbytes((tm, D), F32)
    return pl.pallas_call(
        body, name=name, grid=(M // tm,),
        in_specs=[pl.BlockSpec((2, tm, F), lambda i: (0, i, 0)),
                  pl.BlockSpec((P, D, C), lambda i: (0, 0, 0), pipeline_mode=pl.Buffered(1)), row, vec, row],
        out_specs=[row, vec],
        out_shape=[_sds((M, D), F32), _sds((1, D), F32)],
        compiler_params=_cparams(1, blk))(*_pin([dgu, wgu3, h, g, dres]))


def _ffn_down_bwd(name, dy, wd3, gu, *, tm, tn):
    M, D = dy.shape
    _, F, _ = wd3.shape

    def body(dy_ref, w_ref, gu_ref, o_ref, ab_ref):
        @pl.when(pl.program_id(1) == 0)
        def _():
            ab_ref[...] = _bf(dy_ref[...])
        da = _dot_nt(ab_ref[...], w_ref[...])
        gate = gu_ref[0].astype(F32)
        up = gu_ref[1].astype(F32)
        sig = 1.0 / (1.0 + jnp.exp(-gate))
        silu = gate * sig
        o_ref[0] = _bf(da * up * (sig + silu * (1.0 - sig)))
        o_ref[1] = _bf(da * silu)

    blk = (_nbytes((tm, D), F32) + _nbytes((tn, D), BF16) + 4 * _nbytes((tm, tn), BF16)
           + 4 * _nbytes((tm, tn), F32))
    return pl.pallas_call(
        body, name=name, grid=(M // tm, F // tn),
        in_specs=[pl.BlockSpec((tm, D), lambda i, j: (i, 0)),
                  pl.BlockSpec((None, tn, D), lambda i, j: (0, j, 0)),
                  pl.BlockSpec((2, tm, tn), lambda i, j: (0, i, j))],
        out_specs=pl.BlockSpec((2, tm, tn), lambda i, j: (0, i, j)),
        out_shape=_sds((2, M, F), BF16),
        scratch_shapes=[pltpu.VMEM((tm, D), BF16)],
        compiler_params=_cparams(2, blk, _nbytes((tm, D), BF16)))(*_pin([dy, wd3, gu]))


def _mm_tn(name, a, a_spec, a_blk, b, b_spec, b_blk, out3, o_spec, o_blk, grid, plan=None):
    def body(a_ref, b_ref, o_ref):
        @pl.when(pl.program_id(2) == 0)
        def _():
            o_ref[...] = jnp.zeros_like(o_ref)
        o_ref[...] += _dot_tn(_bf(a_ref[...]), _bf(b_ref[...]))

    blk = _nbytes(a_blk, a.dtype) + _nbytes(b_blk, b.dtype) + 2 * _nbytes(o_blk, F32)
    res, got = _pcall(body, name=name, grid=grid, in_specs=[a_spec, b_spec], out_specs=[o_spec],
                      out_shape=[_sds(out3, F32)], params=_cparams(3, blk), args=[a, b], plan=plan)
    return res[0] if plan is None else (res[0], got)


def _swap_halves(v, lane, masked):
    up = pltpu.roll(v, HEAD - ROT_HALF, 1)
    down = pltpu.roll(v, ROT_HALF, 1)
    rest = jnp.where(lane < ROT_DIM, down, 0.0) if masked else down
    return jnp.where(lane < ROT_HALF, up, rest)


def _qk_prep(name, proj, ctab, stab, qg, kg, *, tm, plan=None):
    S = proj.shape[0]
    width = Q_W + KV_W

    def body(p_ref, c_ref, s_ref, qg_ref, kg_ref, q_ref, k_ref):
        lane = lax.broadcasted_iota(jnp.int32, (tm, HEAD), 1)
        cv = c_ref[...]
        sv = s_ref[...]

        def prep(t, g, scale):
            n = t * _rstd(t) * g
            return (n * cv + _swap_halves(n, lane, False) * sv) * scale

        for j in range(Q_W // HEAD):
            q_ref[:, j * HEAD:(j + 1) * HEAD] = prep(p_ref[:, j * HEAD:(j + 1) * HEAD], qg_ref[...], SCALE)
        for j in range(KV_W // HEAD):
            k_ref[:, j * HEAD:(j + 1) * HEAD] = prep(p_ref[:, Q_W + j * HEAD:Q_W + (j + 1) * HEAD], kg_ref[...], 1.0)

    blk = 2 * _nbytes((tm, width), F32) + 2 * _nbytes((tm, HEAD), F32)
    res, got = _pcall(
        body, name=name, grid=(S // tm,),
        in_specs=[pl.BlockSpec((tm, width), lambda i: (i, 0)),
                  pl.BlockSpec((tm, HEAD), lambda i: (i, 0)),
                  pl.BlockSpec((tm, HEAD), lambda i: (i, 0)),
                  pl.BlockSpec((1, HEAD), lambda i: (0, 0)),
                  pl.BlockSpec((1, HEAD), lambda i: (0, 0))],
        out_specs=[pl.BlockSpec((tm, Q_W), lambda i: (i, 0)), pl.BlockSpec((tm, KV_W), lambda i: (i, 0))],
        out_shape=[_sds((S, Q_W), F32), _sds((S, KV_W), F32)],
        params=_cparams(1, blk), args=[proj, ctab, stab, qg, kg], plan=plan)
    return res if plan is None else (res, got)


ATTN_STEP = 2048


def _row_idx(start, dil):
    return pl.ds(start, BLOCK) if dil == 1 else pl.ds(start, BLOCK, stride=dil)


def _rows(ref, start, dil):
    return ref[_row_idx(start, dil), :]


def _set_rows(ref, start, dil, val):
    ref[_row_idx(start, dil), :] = val


def _add_rows(ref, start, dil, val):
    idx = _row_idx(start, dil)
    ref[idx, :] = ref[idx, :] + val


def _band_mask(first):
    qi = lax.broadcasted_iota(jnp.int32, (BLOCK, 2 * BLOCK), 0)
    kj = lax.broadcasted_iota(jnp.int32, (BLOCK, 2 * BLOCK), 1)
    band = (kj >= qi) & (kj <= qi + BLOCK)
    if first is None:
        return band
    return band & ((kj >= BLOCK) | jnp.logical_not(first))


def _attn_geometry(S, grp):
    dil = DILATIONS[grp]
    bt = BLOCK * dil
    assert S % ATTN_STEP == 0 and ATTN_STEP % bt == 0
    return dil, bt, ATTN_STEP // bt, S // ATTN_STEP


def _attn_fwd(name, qn, kn, proj, grp, plan=None):
    S = qn.shape[0]
    dil, bt, nsub, nsb = _attn_geometry(S, grp)
    vcol = (Q_W + KV_W) // HEAD

    def body(q_ref, kp_ref, kc_ref, vp_ref, vc_ref, o_ref, l_ref):
        valid_first = _band_mask(pl.program_id(1) == 0)
        valid_inner = _band_mask(None)
        for b in range(nsub):
            for r in range(dil):
                at = b * bt + r
                if b == 0:
                    kprev, vprev, valid = _rows(kp_ref, r, dil), _rows(vp_ref, r, dil), valid_first
                else:
                    kprev, vprev, valid = _rows(kc_ref, at - bt, dil), _rows(vc_ref, at - bt, dil), valid_inner
                q = _bf(_rows(q_ref, at, dil))
                k2 = _bf(jnp.concatenate([kprev, _rows(kc_ref, at, dil)], axis=0))
                v2 = _bf(jnp.concatenate([vprev, _rows(vc_ref, at, dil)], axis=0))
                s = jnp.where(valid, _dot_nt(q, k2), NEG_INF)
                m = jnp.max(s, axis=-1, keepdims=True)
                p = jnp.exp(s - m)
                l = jnp.sum(p, axis=-1, keepdims=True)
                acc = _dot(_bf(p), v2)
                _set_rows(o_ref, at, dil, acc / l)
                _set_rows(l_ref, at, dil, jnp.broadcast_to(m + jnp.log(l), (BLOCK, HEAD)))

    prev = lambda n: jnp.maximum(n * nsub - 1, 0)
    big = lambda col: pl.BlockSpec((ATTN_STEP, HEAD), lambda h, n: (n, col(h)))
    tail = lambda col: pl.BlockSpec((bt, HEAD), lambda h, n: (prev(n), col(h)))
    blk = 5 * _nbytes((ATTN_STEP, HEAD), F32) + 2 * _nbytes((bt, HEAD), F32)
    res, got = _pcall(
        body, name=name, grid=(N_HEADS, nsb),
        in_specs=[big(lambda h: grp * N_HEADS + h), tail(lambda h: h), big(lambda h: h),
                  tail(lambda h: vcol + h), big(lambda h: vcol + h)],
        out_specs=[big(lambda h: h), big(lambda h: h)],
        out_shape=[_sds((S, KV_W), F32), _sds((S, KV_W), F32)],
        params=_cparams(2, blk), args=[qn, kn, kn, proj, proj], plan=plan)
    return res if plan is None else (res, got)


def _attn_bwd(name, qn, kn, proj, dattn, lse, delta, grp, plan=None):
    S = qn.shape[0]
    dil, bt, nsub, nsb = _attn_geometry(S, grp)
    vcol = (Q_W + KV_W) // HEAD

    def body(q_ref, kp_ref, kc_ref, vp_ref, vc_ref, do_ref, l_ref, d_ref, dq_ref, dk_ref, dv_ref, ck_ref, cv_ref):
        n = pl.program_id(1)
        par = n % 2

        @pl.when(n < nsb)
        def _():
            valid_first = _band_mask(n == 0)
            valid_inner = _band_mask(None)
            ck, cv = ck_ref.at[par], cv_ref.at[par]
            pk, pv = ck_ref.at[1 - par], cv_ref.at[1 - par]
            for b in range(nsub):
                for r in range(dil):
                    at = b * bt + r
                    if b == 0:
                        kprev, vprev, valid = _rows(kp_ref, r, dil), _rows(vp_ref, r, dil), valid_first
                    else:
                        kprev, vprev, valid = _rows(kc_ref, at - bt, dil), _rows(vc_ref, at - bt, dil), valid_inner
                    q = _bf(_rows(q_ref, at, dil))
                    k2 = _bf(jnp.concatenate([kprev, _rows(kc_ref, at, dil)], axis=0))
                    v2 = _bf(jnp.concatenate([vprev, _rows(vc_ref, at, dil)], axis=0))
                    do = _bf(_rows(do_ref, at, dil))
                    lse_r = _rows(l_ref, at, dil)[:, :1]
                    del_r = _rows(d_ref, at, dil)[:, :1]
                    s = jnp.where(valid, _dot_nt(q, k2), NEG_INF)
                    p = jnp.exp(s - lse_r)
                    ds = _bf(p * (_dot_nt(do, v2) - del_r))
                    _set_rows(dq_ref, at, dil, _dot(ds, k2))
                    dk2 = _dot_tn(ds, q)
                    dv2 = _dot_tn(_bf(p), do)
                    _set_rows(ck, at, dil, dk2[BLOCK:])
                    _set_rows(cv, at, dil, dv2[BLOCK:])
                    if b > 0:
                        _add_rows(ck, at - bt, dil, dk2[:BLOCK])
                        _add_rows(cv, at - bt, dil, dv2[:BLOCK])
                    else:
                        @pl.when(n > 0)
                        def _():
                            _add_rows(pk, (nsub - 1) * bt + r, dil, dk2[:BLOCK])
                            _add_rows(pv, (nsub - 1) * bt + r, dil, dv2[:BLOCK])

        @pl.when(n > 0)
        def _():
            dk_ref[...] = ck_ref[1 - par]
            dv_ref[...] = cv_ref[1 - par]

    cur = lambda n: jnp.minimum(n, nsb - 1)
    prev = lambda n: jnp.maximum(n - 1, 0)
    tail_at = lambda n: jnp.maximum(cur(n) * nsub - 1, 0)
    big = lambda col: pl.BlockSpec((ATTN_STEP, HEAD), lambda h, n: (cur(n), col(h)))
    tail = lambda col: pl.BlockSpec((bt, HEAD), lambda h, n: (tail_at(n), col(h)))
    late = pl.BlockSpec((ATTN_STEP, HEAD), lambda h, n: (prev(n), h))
    blk = 9 * _nbytes((ATTN_STEP, HEAD), F32) + 2 * _nbytes((bt, HEAD), F32)
    res, got = _pcall(
        body, name=name, grid=(N_HEADS, nsb + 1),
        in_specs=[big(lambda h: grp * N_HEADS + h), tail(lambda h: h), big(lambda h: h),
                  tail(lambda h: vcol + h), big(lambda h: vcol + h),
                  big(lambda h: h), big(lambda h: h), big(lambda h: h)],
        out_specs=[big(lambda h: h), late, late],
        out_shape=[_sds((S, KV_W), F32)] * 3,
        scratch_shapes=[pltpu.VMEM((2, ATTN_STEP, HEAD), F32), pltpu.VMEM((2, ATTN_STEP, HEAD), F32)],
        params=_cparams(2, blk, 4 * _nbytes((ATTN_STEP, HEAD), F32)),
        args=[qn, kn, kn, proj, proj, dattn, lse, delta], plan=plan)
    return res if plan is None else (res, got)


def _window_sum(v, n_doublings, back):
    rows = v.shape[0]
    step = 1
    for _ in range(n_doublings):
        v = v + pltpu.roll(v, step if back else rows - step, 0)
        step *= 2
    return v


def _mix_post(name, outs, lses, proj, pool_w, pool_scale, *, tm):
    S = proj.shape[0]
    ucol = (IN_W - POOL_W) // POOL_W
    hpt = tm // POOL_HALO

    def body(o0, o1, o2, l0, l1, l2, u_ref, uh_ref, pw_ref, ps_ref, mix_ref, lse_ref, dp_ref):
        i = pl.program_id(0)
        for h in range(N_HEADS):
            sl = slice(h * HEAD, (h + 1) * HEAD)
            ls = [l0[:, sl], l1[:, sl], l2[:, sl]]
            m = jnp.maximum(jnp.maximum(ls[0], ls[1]), ls[2])
            ws = [jnp.exp(v - m) for v in ls]
            den = ws[0] + ws[1] + ws[2]
            num = ws[0] * o0[:, sl] + ws[1] * o1[:, sl] + ws[2] * o2[:, sl]
            mix_ref[:, sl] = _bf(num / den)
            lse_ref[:, sl] = m + jnp.log(den)
        halo = jnp.where(i == 0, 0.0, uh_ref[...])
        t = lax.broadcasted_iota(jnp.int32, (tm + POOL_HALO, HEAD), 0) + (i * tm - POOL_HALO)
        for g, w in enumerate(POOL_WINDOWS):
            sl = slice(g * HEAD, (g + 1) * HEAD)
            ub = jnp.concatenate([halo[:, sl], u_ref[:, sl]], axis=0)
            cnt = jnp.minimum(t + 1, w).astype(F32)
            d = (_window_sum(ub, g + 1, True) / cnt - ub)[POOL_HALO:]
            db = _bf(d)
            dp_ref[:, sl] = db
            mix_ref[:, KV_W + g * HEAD:KV_W + (g + 1) * HEAD] = _bf(_dot(db, pw_ref[g]) * ps_ref[:, sl])

    tile = pl.BlockSpec((tm, KV_W), lambda i: (i, 0))
    blk = 9 * _nbytes((tm, KV_W), F32) + _nbytes((tm, 2 * KV_W), BF16)
    return pl.pallas_call(
        body, name=name, grid=(S // tm,),
        in_specs=[tile] * 6 + [
            pl.BlockSpec((tm, POOL_W), lambda i: (i, ucol)),
            pl.BlockSpec((POOL_HALO, POOL_W), lambda i: (jnp.maximum(i * hpt - 1, 0), ucol)),
            pl.BlockSpec((len(POOL_WINDOWS), HEAD, HEAD), lambda i: (0, 0, 0)),
            pl.BlockSpec((1, POOL_W), lambda i: (0, 0))],
        out_specs=[pl.BlockSpec((tm, 2 * KV_W), lambda i: (i, 0)), tile, tile],
        out_shape=[_sds((S, 2 * KV_W), BF16), _sds((S, KV_W), F32),
                   _sds((S, POOL_W), BF16)],
        compiler_params=_cparams(1, blk))(*_pin([*outs, *lses, proj, proj, pool_w, pool_scale]))


def _mix_bwd(name, dmix, mix, dpool, pool_w, pool_scale, *, tm, plan=None):
    S = dmix.shape[0]
    hpt = tm // POOL_HALO
    last_halo = S // POOL_HALO - 1
    n_tiles = S // tm

    def body(dm_ref, dh_ref, at_ref, dp_ref, pw_ref, ps_ref, da_ref, dl_ref, du_ref, gw_ref, gs_ref):
        i = pl.program_id(0)

        @pl.when(i == 0)
        def _():
            gw_ref[...] = jnp.zeros_like(gw_ref)
            gs_ref[...] = jnp.zeros_like(gs_ref)

        for h in range(N_HEADS):
            sl = slice(h * HEAD, (h + 1) * HEAD)
            da = dm_ref[:, sl]
            da_ref[:, sl] = da
            dl_ref[:, sl] = jnp.broadcast_to(
                jnp.sum(da * at_ref[:, sl].astype(F32), axis=-1, keepdims=True), (tm, HEAD))
        halo = jnp.where(i == n_tiles - 1, 0.0, dh_ref[...])
        t = lax.broadcasted_iota(jnp.int32, (tm + POOL_HALO, HEAD), 0) + i * tm
        for g, w in enumerate(POOL_WINDOWS):
            sl = slice(g * HEAD, (g + 1) * HEAD)
            dy = jnp.concatenate([dm_ref[:, KV_W + g * HEAD:KV_W + (g + 1) * HEAD], halo[:, sl]], axis=0)
            dys = _bf(dy * ps_ref[:, sl])
            dd = _dot_nt(dys, pw_ref[g])
            cnt = jnp.minimum(t + 1, w).astype(F32)
            du_ref[:, sl] = (_window_sum(dd / cnt, g + 1, False) - dd)[:tm]
            db = dp_ref[:, sl]
            gw_ref[g] += _dot_tn(db, dys[:tm])
            gs_ref[:, sl] += jnp.sum(dy[:tm] * _dot(db, pw_ref[g]), axis=0, keepdims=True)

    tile = pl.BlockSpec((tm, KV_W), lambda i: (i, 0))
    blk = _nbytes((tm, 2 * KV_W), F32) + 6 * _nbytes((tm, KV_W), F32)
    res, got = _pcall(
        body, name=name, grid=(n_tiles,),
        in_specs=[pl.BlockSpec((tm, 2 * KV_W), lambda i: (i, 0)),
                  pl.BlockSpec((POOL_HALO, POOL_W), lambda i: (jnp.minimum((i + 1) * hpt, last_halo), 1)),
                  tile, tile,
                  pl.BlockSpec((len(POOL_WINDOWS), HEAD, HEAD), lambda i: (0, 0, 0)),
                  pl.BlockSpec((1, POOL_W), lambda i: (0, 0))],
        out_specs=[tile, tile, tile,
                   pl.BlockSpec((len(POOL_WINDOWS), HEAD, HEAD), lambda i: (0, 0, 0)),
                   pl.BlockSpec((1, POOL_W), lambda i: (0, 0))],
        out_shape=[_sds((S, KV_W), F32)] * 3 + [
            _sds((len(POOL_WINDOWS), HEAD, HEAD), F32), _sds((1, POOL_W), F32)],
        params=_cparams(1, blk), args=[dmix, dmix, mix, dpool, pool_w, pool_scale], plan=plan)
    return res if plan is None else (res, got)


def _qkv_bwd(name, dqs, dks, dvs, du, proj, ctab, stab, qg, kg, *, tm, plan=None):
    S = proj.shape[0]
    width = Q_W + KV_W

    def body(dq0, dq1, dq2, dk0, dk1, dk2, dv0, dv1, dv2, du_ref, p_ref, c_ref, s_ref, qg_ref, kg_ref,
             dp_ref, gq_ref, gk_ref):
        @pl.when(pl.program_id(0) == 0)
        def _():
            gq_ref[...] = jnp.zeros_like(gq_ref)
            gk_ref[...] = jnp.zeros_like(gk_ref)

        lane = lax.broadcasted_iota(jnp.int32, (tm, HEAD), 1)
        cv = c_ref[...]
        sv = s_ref[...]

        def back(dy, t, g, scale):
            dy = dy * scale
            dn = dy * cv + _swap_halves(dy * sv, lane, True)
            dt, xh = _norm_bwd(dn, t, g)
            return dt, jnp.sum(dn * xh, axis=0, keepdims=True)

        dqr = (dq0, dq1, dq2)
        gq = jnp.zeros((1, HEAD), F32)
        for j in range(Q_W // HEAD):
            grp, h = divmod(j, N_HEADS)
            dt, gj = back(dqr[grp][:, h * HEAD:(h + 1) * HEAD], p_ref[:, j * HEAD:(j + 1) * HEAD], qg_ref[...], SCALE)
            dp_ref[:, j * HEAD:(j + 1) * HEAD] = _bf(dt)
            gq = gq + gj
        gq_ref[...] += gq
        gk = jnp.zeros((1, HEAD), F32)
        for h in range(N_HEADS):
            sl = slice(h * HEAD, (h + 1) * HEAD)
            dt, gj = back(dk0[:, sl] + dk1[:, sl] + dk2[:, sl], p_ref[:, Q_W + h * HEAD:Q_W + (h + 1) * HEAD],
                          kg_ref[...], 1.0)
            dp_ref[:, Q_W + h * HEAD:Q_W + (h + 1) * HEAD] = _bf(dt)
            gk = gk + gj
        gk_ref[...] += gk
        dp_ref[:, width:width + KV_W] = _bf(dv0[...] + dv1[...] + dv2[...])
        dp_ref[:, width + KV_W:] = _bf(du_ref[...])

    tile = pl.BlockSpec((tm, KV_W), lambda i: (i, 0))
    vec = pl.BlockSpec((1, HEAD), lambda i: (0, 0))
    rot = pl.BlockSpec((tm, HEAD), lambda i: (i, 0))
    blk = 10 * _nbytes((tm, KV_W), F32) + _nbytes((tm, width), F32) + _nbytes((tm, IN_W), BF16)
    res, got = _pcall(
        body, name=name, grid=(S // tm,),
        in_specs=[tile] * 10 + [pl.BlockSpec((tm, width), lambda i: (i, 0)), rot, rot, vec, vec],
        out_specs=[pl.BlockSpec((tm, IN_W), lambda i: (i, 0)), vec, vec],
        out_shape=[_sds((S, IN_W), BF16), _sds((1, HEAD), F32),
                   _sds((1, HEAD), F32)],
        params=_cparams(1, blk), args=[*dqs, *dks, *dvs, du, proj, ctab, stab, qg, kg], plan=plan)
    return res if plan is None else (res, got)


def _cross_heads(q_ref, kv_ref, qg, kg, h):
    sl = slice(h * HEAD, (h + 1) * HEAD)
    qr = q_ref[:, sl]
    kr = kv_ref[:, sl]
    qh = qr * _rstd(qr) * qg * SCALE
    kh = kr * _rstd(kr) * kg
    vh = kv_ref[:, X_W + h * HEAD:X_W + (h + 1) * HEAD]
    return qr, _bf(qh), _bf(kh), _bf(vh)


def _cross_fwd(name, qraw, kv, qg, kg, *, tm):
    S = qraw.shape[0]
    M = kv.shape[0]

    def body(q_ref, kv_ref, qg_ref, kg_ref, o_ref):
        for h in range(N_HEADS):
            _, qh, kh, vh = _cross_heads(q_ref, kv_ref, qg_ref[...], kg_ref[...], h)
            s = _dot_nt(qh, kh)
            p = jnp.exp(s - jnp.max(s, axis=-1, keepdims=True))
            l = jnp.sum(p, axis=-1, keepdims=True)
            o_ref[:, h * HEAD:(h + 1) * HEAD] = _bf(_dot(_bf(p), vh) / l)

    vec = pl.BlockSpec((1, HEAD), lambda i: (0, 0))
    blk = 2 * _nbytes((tm, X_W), F32) + _nbytes((M, 2 * X_W), F32) + 4 * _nbytes((tm, M), F32)
    return pl.pallas_call(
        body, name=name, grid=(S // tm,),
        in_specs=[pl.BlockSpec((tm, X_W), lambda i: (i, 0)), pl.BlockSpec((M, 2 * X_W), lambda i: (0, 0)), vec, vec],
        out_specs=pl.BlockSpec((tm, X_W), lambda i: (i, 0)),
        out_shape=_sds((S, X_W), BF16),
        compiler_params=_cparams(1, blk))(*_pin([qraw, kv, qg, kg]))


def _cross_bwd(name, do, qraw, kv, qg, kg, *, tm, plan=None):
    S = qraw.shape[0]
    M = kv.shape[0]

    def body(do_ref, q_ref, kv_ref, qg_ref, kg_ref, dq_ref, dk_ref, dv_ref, gq_ref):
        @pl.when(pl.program_id(0) == 0)
        def _():
            dk_ref[...] = jnp.zeros_like(dk_ref)
            dv_ref[...] = jnp.zeros_like(dv_ref)
            gq_ref[...] = jnp.zeros_like(gq_ref)

        gq = jnp.zeros((1, HEAD), F32)
        for h in range(N_HEADS):
            sl = slice(h * HEAD, (h + 1) * HEAD)
            qr, qh, kh, vh = _cross_heads(q_ref, kv_ref, qg_ref[...], kg_ref[...], h)
            doh = _bf(do_ref[:, sl])
            s = _dot_nt(qh, kh)
            p = jnp.exp(s - jnp.max(s, axis=-1, keepdims=True))
            p = p / jnp.sum(p, axis=-1, keepdims=True)
            pb = _bf(p)
            dp = _dot_nt(doh, vh)
            ds = _bf(p * (dp - jnp.sum(dp * p, axis=-1, keepdims=True)))
            dv_ref[:, sl] += _dot_tn(pb, doh)
            dk_ref[:, sl] += _dot_tn(ds, qh)
            dn = _dot(ds, kh) * SCALE
            dt, xh = _norm_bwd(dn, qr, qg_ref[...])
            dq_ref[:, sl] = _bf(dt)
            gq = gq + jnp.sum(dn * xh, axis=0, keepdims=True)
        gq_ref[...] += gq

    vec = pl.BlockSpec((1, HEAD), lambda i: (0, 0))
    acc = pl.BlockSpec((M, X_W), lambda i: (0, 0))
    blk = 3 * _nbytes((tm, X_W), F32) + 3 * _nbytes((M, 2 * X_W), F32) + 6 * _nbytes((tm, M), F32)
    res, got = _pcall(
        body, name=name, grid=(S // tm,),
        in_specs=[pl.BlockSpec((tm, X_W), lambda i: (i, 0)), pl.BlockSpec((tm, X_W), lambda i: (i, 0)),
                  pl.BlockSpec((M, 2 * X_W), lambda i: (0, 0)), vec, vec],
        out_specs=[pl.BlockSpec((tm, X_W), lambda i: (i, 0)), acc, acc, vec],
        out_shape=[_sds((S, X_W), BF16), _sds((M, X_W), F32),
                   _sds((M, X_W), F32), _sds((1, HEAD), F32)],
        params=_cparams(1, blk), args=[do, qraw, kv, qg, kg], plan=plan)
    return res if plan is None else (res, got)


def _cross_kv_bwd(name, dkn, dv, kv, kg):
    M = kv.shape[0]

    def body(dk_ref, dv_ref, kv_ref, kg_ref, o_ref, g_ref):
        gk = jnp.zeros((1, HEAD), F32)
        for h in range(N_HEADS):
            sl = slice(h * HEAD, (h + 1) * HEAD)
            dn = dk_ref[:, sl]
            dt, xh = _norm_bwd(dn, kv_ref[:, sl], kg_ref[...])
            o_ref[:, sl] = _bf(dt)
            gk = gk + jnp.sum(dn * xh, axis=0, keepdims=True)
        o_ref[:, X_W:] = _bf(dv_ref[...])
        g_ref[...] = gk

    full = lambda shape: pl.BlockSpec(shape, lambda i: (0,) * len(shape))
    return pl.pallas_call(
        body, name=name, grid=(1,),
        in_specs=[full((M, X_W)), full((M, X_W)), full((M, 2 * X_W)), full((1, HEAD))],
        out_specs=[full((M, 2 * X_W)), full((1, HEAD))],
        out_shape=[_sds((M, 2 * X_W), BF16), _sds((1, HEAD), F32)],
        compiler_params=_cparams(1, 6 * _nbytes((M, 2 * X_W), F32)))(dkn, dv, kv, kg)


def _rope_tables(positions):
    inv_freq = ROPE_THETA ** (-jnp.arange(0, ROT_DIM, 2, dtype=F32) / ROT_DIM)
    ang = positions.astype(F32)[:, None] * inv_freq
    cos, sin = jnp.cos(ang), jnp.sin(ang)
    S = positions.shape[0]
    ctab = jnp.concatenate([cos, cos, jnp.ones((S, HEAD - ROT_DIM), F32)], axis=-1)
    stab = jnp.concatenate([-sin, sin, jnp.zeros((S, HEAD - ROT_DIM), F32)], axis=-1)
    return ctab, stab


GATHER_BEHIND_IN_PROJ = ("w_out", "w_cq", "w_ckv", "w_co")
FFN_WEIGHTS = ("w_gate_up", "w_down")


def _hosted(fn, *args, plan=None, **kw):
    if plan is None:
        return fn(*args, **kw), []
    return fn(*args, plan=plan, **kw)


def _local_step(x, mem, positions, target, wb, sm, *, tm=512, place=None):
    S, D = x.shape
    M = mem.shape[0]
    dist = place is not None
    wb = dict(wb)
    ctab, stab = _rope_tables(positions)
    pool_w_b = _bf(sm["pool_w"])

    def gather(names):
        return _GatherPlan([wb[k] for k in names]) if dist else None

    if dist:
        wb["w_in"], = _run_plan("gather_w_in", gather(["w_in"]))
    w_in = wb["w_in"]
    (proj, xn1), got = _hosted(_mm_nn, "in_proj", x, w_in, tm=tm, norm_g=sm["mix_norm_g"],
                               plan=gather(GATHER_BEHIND_IN_PROJ))
    wb.update(zip(GATHER_BEHIND_IN_PROJ, got))
    (qn, kn), got = _hosted(_qk_prep, "qk_prep", proj, ctab, stab, sm["q_norm_g"], sm["k_norm_g"], tm=tm,
                            plan=gather(["w_gate_up"]))
    wb.update(zip(["w_gate_up"], got))
    outs, lses = [], []
    for grp in range(len(DILATIONS)):
        last = grp == len(DILATIONS) - 1
        (o, l), got = _hosted(_attn_fwd, f"attn_fwd{grp}", qn, kn, proj, grp, plan=gather(["w_down"]) if last else None)
        wb.update(zip(["w_down"], got))
        outs.append(o)
        lses.append(l)
    w_out = wb["w_out"].reshape(1, 2 * KV_W, D)
    w_cq = wb["w_cq"].reshape(1, D, X_W)
    w_ckv = wb["w_ckv"].reshape(1, D, 2 * X_W)
    w_co = wb["w_co"]
    w_gu = wb["w_gate_up"]
    w_down = wb["w_down"].reshape(1, D_FF, D)
    cin = w_in.shape[2]
    cco = w_co.shape[2]
    cgu = w_gu.shape[2]
    mix, lse, dpool = _mix_post("mix_post", outs, lses, proj, pool_w_b, sm["pool_scale"], tm=tm)
    h1 = _mm_nn("out_proj", mix, w_out, tm=tm, residual=x)
    cq_raw, hn2 = _mm_nn("cq_proj", h1, w_cq, tm=tm, norm_g=sm["cross_norm_g"])
    kv, mem_n = _mm_nn("ckv_proj", mem, w_ckv, tm=M, norm_g=sm["mem_norm_g"])
    xo = _cross_fwd("cross_fwd", cq_raw, kv, sm["cq_norm_g"], sm["ck_norm_g"], tm=tm)
    h2 = _mm_nn("co_proj", xo, w_co, tm=tm, residual=h1)
    act, gu, hn3 = _ffn_up("ffn_up", h2, w_gu, sm["ffn_norm_g"], tm=tm)
    dy, lsum = _ffn_down_loss("ffn_down_loss", act, w_down, h2, target, tm=tm)
    loss = 0.5 * jnp.sum(lsum) / D

    nS = S // tm
    dgu = _ffn_down_bwd("ffn_down_bwd", dy, w_down, gu, tm=tm, tn=cgu)
    g_down = _mm_tn("g_w_down", act, pl.BlockSpec((tm, cgu), lambda r, c, s: (s, r)), (tm, cgu),
                    dy, pl.BlockSpec((tm, D), lambda r, c, s: (s, 0)), (tm, D),
                    (1, D_FF, D), pl.BlockSpec((None, cgu, D), lambda r, c, s: (0, r, 0)), (cgu, D),
                    (D_FF // cgu, 1, nS))
    dh2, g_ffn_norm = _ffn_up_bwd("ffn_up_bwd", dgu, w_gu, h2, sm["ffn_norm_g"], dy, tm=tm)
    g_gu = _mm_tn("g_w_gate_up", hn3, pl.BlockSpec((tm, D), lambda r, c, s: (s, 0)), (tm, D),
                  dgu, pl.BlockSpec((None, tm, cgu), lambda r, c, s: (c // 2, s, c % 2)), (tm, cgu),
                  (4, D, cgu), pl.BlockSpec((None, D, cgu), lambda r, c, s: (c, 0, 0)), (D, cgu),
                  (1, 4, nS))

    dxo = _mm_nt("co_proj_bwd", dh2, w_co, tm=tm, out_dtype=BF16)
    g_co = _mm_tn("g_w_co", xo, pl.BlockSpec((tm, X_W), lambda r, c, s: (s, 0)), (tm, X_W),
                  dh2, pl.BlockSpec((tm, cco), lambda r, c, s: (s, c)), (tm, cco),
                  (N_CHIPS, X_W, cco), pl.BlockSpec((None, X_W, cco), lambda r, c, s: (c, 0, 0)), (X_W, cco),
                  (1, N_CHIPS, nS))
    full = {"w_gate_up": g_gu, "w_down": g_down.reshape(N_CHIPS, D_FF // N_CHIPS, D)}
    sums = {}

    def swap(names):
        return _SwapPlan([full[k] for k in names]) if dist else None

    def add_halves(names, from_sibling):
        for k, t in zip(names, from_sibling):
            sums[k] = _add_core_halves(f"add_halves_{k}", full[k], t, place[0])

    def exchange(names):
        return _ExchangePlan([sums[k][1] for k in names]) if dist else None

    def sum_chips(names, from_chips):
        return [_sum_chips(f"sum_chips_{k}", sums[k][0], t, place[1]) for k, t in zip(names, from_chips)]

    (dcq, dkn, dvm, g_cq_norm), got = _hosted(_cross_bwd, "cross_bwd", dxo, cq_raw, kv, sm["cq_norm_g"],
                                              sm["ck_norm_g"], tm=tm, plan=swap(FFN_WEIGHTS))
    add_halves(FFN_WEIGHTS, got)
    dkv, g_ck_norm = _cross_kv_bwd("cross_kv_bwd", dkn, dvm, kv, sm["ck_norm_g"])
    dh1, g_cross_norm = _mm_nt_normbwd("cq_proj_bwd", dcq, w_cq, h1, sm["cross_norm_g"], dh2, tm=tm)
    g_cq = _mm_tn("g_w_cq", hn2, pl.BlockSpec((tm, D), lambda r, c, s: (s, 0)), (tm, D),
                  dcq, pl.BlockSpec((tm, X_W), lambda r, c, s: (s, 0)), (tm, X_W),
                  (1, D, X_W), pl.BlockSpec((None, D, X_W), lambda r, c, s: (0, 0, 0)), (D, X_W), (1, 1, nS))
    _, g_mem_norm = _mm_nt_normbwd("ckv_proj_bwd", dkv, w_ckv, mem, sm["mem_norm_g"], None, tm=M)
    g_ckv = _mm_tn("g_w_ckv", mem_n, pl.BlockSpec((M, D), lambda r, c, s: (0, 0)), (M, D),
                   dkv, pl.BlockSpec((M, 2 * X_W), lambda r, c, s: (0, 0)), (M, 2 * X_W),
                   (1, D, 2 * X_W), pl.BlockSpec((None, D, 2 * X_W), lambda r, c, s: (0, 0, 0)), (D, 2 * X_W),
                   (1, 1, 1))

    dmix = _mm_nt("out_proj_bwd", dh1, w_out, tm=tm, out_dtype=F32)
    g_out = _mm_tn("g_w_out", mix, pl.BlockSpec((tm, 2 * KV_W), lambda r, c, s: (s, 0)), (tm, 2 * KV_W),
                   dh1, pl.BlockSpec((tm, D), lambda r, c, s: (s, 0)), (tm, D),
                   (1, 2 * KV_W, D), pl.BlockSpec((None, 2 * KV_W, D), lambda r, c, s: (0, 0, 0)), (2 * KV_W, D),
                   (1, 1, nS))
    full.update({
        "w_out": g_out.reshape(N_CHIPS, 2 * KV_W // N_CHIPS, D),
        "w_cq": g_cq.reshape(N_CHIPS, D // N_CHIPS, X_W),
        "w_ckv": g_ckv.reshape(N_CHIPS, D // N_CHIPS, 2 * X_W),
        "w_co": g_co,
    })
    mixer = GATHER_BEHIND_IN_PROJ
    (dattn, delta, du, g_pool_w, g_pool_scale), got = _hosted(
        _mix_bwd, "mix_bwd", dmix, mix, dpool, pool_w_b, sm["pool_scale"], tm=tm, plan=swap(mixer))
    add_halves(mixer, got)
    behind_attn = (None, mixer, FFN_WEIGHTS)
    halves = {}
    dqs, dks, dvs = [], [], []
    for grp in range(len(DILATIONS)):
        names = behind_attn[grp]
        (dq, dk, dv), got = _hosted(_attn_bwd, f"attn_bwd{grp}", qn, kn, proj, dattn, lse, delta, grp,
                                    plan=exchange(names) if names else None)
        if dist and names:
            halves.update(zip(names, sum_chips(names, got)))
        dqs.append(dq)
        dks.append(dk)
        dvs.append(dv)
    joined = mixer + FFN_WEIGHTS
    (dproj, g_q_norm, g_k_norm), got = _hosted(
        _qkv_bwd, "qkv_bwd", dqs, dks, dvs, du, proj, ctab, stab, sm["q_norm_g"], sm["k_norm_g"], tm=tm,
        plan=_JoinPlan([halves[k] for k in joined]) if dist else None)
    shards = dict(zip(joined, got))
    dx, g_mix_norm = _mm_nt_normbwd("in_proj_bwd", dproj, w_in, x, sm["mix_norm_g"], dh1, tm=tm)
    small = {
        "mix_norm_g": g_mix_norm, "q_norm_g": g_q_norm, "k_norm_g": g_k_norm, "pool_w": g_pool_w,
        "pool_scale": g_pool_scale, "cross_norm_g": g_cross_norm, "mem_norm_g": g_mem_norm,
        "cq_norm_g": g_cq_norm, "ck_norm_g": g_ck_norm, "ffn_norm_g": g_ffn_norm,
    }
    full["w_in"], got = _hosted(
        _mm_tn, "g_w_in", xn1, pl.BlockSpec((tm, D), lambda r, c, s: (s, 0)), (tm, D),
        dproj, pl.BlockSpec((tm, cin), lambda r, c, s: (s, c)), (tm, cin),
        (N_CHIPS, D, cin), pl.BlockSpec((None, D, cin), lambda r, c, s: (c, 0, 0)), (D, cin), (1, N_CHIPS, nS),
        plan=_AllPushPlan(_pack_small(small)) if dist else None)
    if dist:
        small = _unpack_small(_sum_slots("sum_small", got[0]), sm)
        add_halves(["w_in"], _run_plan("swap_w_in", swap(["w_in"])))
        half, = sum_chips(["w_in"], _run_plan("exchange_w_in", exchange(["w_in"])))
        shards["w_in"], = _run_plan("join_w_in", _JoinPlan([half]))
    big = shards if dist else full
    return loss, dx, big, small


BIG = ("w_in", "w_out", "w_cq", "w_ckv", "w_co", "w_gate_up", "w_down")
SMALL = ("mix_norm_g", "q_norm_g", "k_norm_g", "pool_w", "pool_scale", "cross_norm_g", "mem_norm_g",
         "cq_norm_g", "ck_norm_g", "ffn_norm_g")
WEIGHTS = ("mix_norm_g", "w_in", "q_norm_g", "k_norm_g", "pool_w", "pool_scale", "w_out", "cross_norm_g",
           "mem_norm_g", "w_cq", "w_ckv", "cq_norm_g", "ck_norm_g", "w_co", "ffn_norm_g", "w_gate_up", "w_down")


def _cast_piece(name, w, k_arr):
    R, C = w.shape
    hr = R // 2

    def body(k_ref, w_ref, o_ref):
        o_ref[...] = _bf(w_ref[...])

    return pl.pallas_call(
        body, name=name,
        grid_spec=pltpu.PrefetchScalarGridSpec(
            num_scalar_prefetch=1, grid=(2,),
            in_specs=[pl.BlockSpec((hr, C), lambda i, k: (i, 0))],
            out_specs=pl.BlockSpec((None, hr, C), lambda i, k: (k[0], i, 0))),
        out_shape=_sds((N_CHIPS, R, C), BF16),
        compiler_params=_cparams(1, 2 * _nbytes((hr, C), F32)))(k_arr, *_pin([w]))


def _add_core_halves(name, g, t, c_arr):
    P, R, C = g.shape
    hr = R // 2

    def body(c_ref, g_ref, t_ref, o_ref, ob_ref):
        tot = g_ref[...] + t_ref[...]
        o_ref[...] = tot
        ob_ref[...] = _bf(tot)

    piece = pl.BlockSpec((None, hr, C), lambda p, c: (p, 0, 0))
    return pl.pallas_call(
        body, name=name,
        grid_spec=pltpu.PrefetchScalarGridSpec(
            num_scalar_prefetch=1, grid=(P,),
            in_specs=[pl.BlockSpec((None, hr, C), lambda p, c: (p, c[0], 0)), piece],
            out_specs=[piece, piece]),
        out_shape=[_sds((P, hr, C), F32), _sds((P, hr, C), BF16)],
        compiler_params=_cparams(1, 4 * _nbytes((hr, C), F32)))(c_arr, *_pin([g, t]))


def _sum_chips(name, own, got, kc_arr):
    P, hr, C = own.shape

    def body(kc_ref, o_ref, g_ref, r_ref):
        r_ref[...] = ((o_ref[...] + g_ref[0].astype(F32)) + g_ref[1].astype(F32)) + g_ref[2].astype(F32)

    return pl.pallas_call(
        body, name=name,
        grid_spec=pltpu.PrefetchScalarGridSpec(
            num_scalar_prefetch=1, grid=(1,),
            in_specs=[pl.BlockSpec((None, hr, C), lambda i, kc: (kc[0], 0, 0)),
                      pl.BlockSpec((N_CHIPS - 1, hr, C), lambda i, kc: (0, 0, 0))],
            out_specs=pl.BlockSpec((hr, C), lambda i, kc: (kc[1], 0))),
        out_shape=_sds((2 * hr, C), F32),
        compiler_params=_cparams(1, 5 * _nbytes((hr, C), F32)))(kc_arr, *_pin([own, got]))


N_DEV = 8


class _AllPushPlan(_Plan):
    def __init__(self, v):
        self.ins = [v]
        self.out_shapes = [jax.ShapeDtypeStruct((N_DEV,) + v.shape, v.dtype)]
        self.sem_shapes = [pltpu.SemaphoreType.DMA((N_DEV - 1,)), pltpu.SemaphoreType.DMA((N_DEV - 1,)),
                           pltpu.SemaphoreType.DMA]

    def copies(self, ins, outs, sems):
        send, recv, own = sems
        x, y, c, _ = _place()
        slot = outs[0].at[4 * x + 2 * y + c]
        cps = [pltpu.make_async_copy(ins[0], slot, own)]
        flips = [(dx, dy, dc) for dx in (0, 1) for dy in (0, 1) for dc in (0, 1)][1:]
        for q, (dx, dy, dc) in enumerate(flips):
            to = (x + dx - 2 * x * dx, y + dy - 2 * y * dy, c + dc - 2 * c * dc)
            cps.append(pltpu.make_async_remote_copy(src_ref=ins[0], dst_ref=slot, send_sem=send.at[q],
                                                    recv_sem=recv.at[q], device_id=to, device_id_type=MESH))
        return cps


def _sum_slots(name, slots):
    n, R, C = slots.shape

    def body(s_ref, o_ref):
        acc = s_ref[0]
        for d in range(1, n):
            acc = acc + s_ref[d]
        o_ref[...] = acc

    return pl.pallas_call(
        body, name=name, grid=(1,),
        in_specs=[pl.BlockSpec((n, R, C), lambda i: (0, 0, 0))], out_specs=pl.BlockSpec((R, C), lambda i: (0, 0)),
        out_shape=_sds((R, C), F32),
        compiler_params=_cparams(1, _nbytes((n + 1, R, C), F32)))(*_pin([slots]))


def _adamw(name, w, g, m, v, *, tr):
    R, C = w.shape

    def body(w_ref, g_ref, m_ref, v_ref, d_ref, nm_ref, nv_ref):
        gv = g_ref[...]
        nm = ADAM_B1 * m_ref[...] + (1.0 - ADAM_B1) * gv
        nv = ADAM_B2 * v_ref[...] + (1.0 - ADAM_B2) * (gv * gv)
        m_hat = nm / (1.0 - ADAM_B1 ** ADAM_STEP)
        v_hat = nv / (1.0 - ADAM_B2 ** ADAM_STEP)
        d_ref[...] = -ADAM_LR * (m_hat / (jnp.sqrt(v_hat) + ADAM_EPS) + ADAM_WD * w_ref[...])
        nm_ref[...] = nm
        nv_ref[...] = nv

    tile = pl.BlockSpec((tr, C), lambda i: (i, 0))
    return pl.pallas_call(
        body, name=name, grid=(R // tr,), in_specs=[tile] * 4, out_specs=[tile] * 3,
        out_shape=[_sds((R, C), F32)] * 3,
        compiler_params=_cparams(1, 7 * _nbytes((tr, C), F32)))(*_pin([w, g, m, v]))


def _pack_small(d):
    parts = []
    for name in SMALL:
        a = d[name].reshape(-1, HEAD)
        pad = (-a.shape[0]) % 8
        parts.append(jnp.pad(a, ((0, pad), (0, 0))))
    return jnp.concatenate(parts, axis=0)


def _unpack_small(packed, like):
    out = {}
    row = 0
    for name in SMALL:
        shape = like[name].shape
        rows = like[name].size // HEAD
        out[name] = packed[row:row + rows].reshape(shape)
        row += rows + (-rows) % 8
    return out


def kernel(x, mem, positions, mix_norm_g, w_in, q_norm_g, k_norm_g, pool_w, pool_scale, w_out, cross_norm_g, mem_norm_g, w_cq, w_ckv, cq_norm_g, ck_norm_g, w_co, ffn_norm_g, w_gate_up, w_down, loss_target, m_mix_norm_g, m_w_in, m_q_norm_g, m_k_norm_g, m_pool_w, m_pool_scale, m_w_out, m_cross_norm_g, m_mem_norm_g, m_w_cq, m_w_ckv, m_cq_norm_g, m_ck_norm_g, m_w_co, m_ffn_norm_g, m_w_gate_up, m_w_down, v_mix_norm_g, v_w_in, v_q_norm_g, v_k_norm_g, v_pool_w, v_pool_scale, v_w_out, v_cross_norm_g, v_mem_norm_g, v_w_cq, v_w_ckv, v_cq_norm_g, v_ck_norm_g, v_w_co, v_ffn_norm_g, v_w_gate_up, v_w_down):
    w = dict(mix_norm_g=mix_norm_g, w_in=w_in, q_norm_g=q_norm_g, k_norm_g=k_norm_g, pool_w=pool_w,
             pool_scale=pool_scale, w_out=w_out, cross_norm_g=cross_norm_g, mem_norm_g=mem_norm_g, w_cq=w_cq,
             w_ckv=w_ckv, cq_norm_g=cq_norm_g, ck_norm_g=ck_norm_g, w_co=w_co, ffn_norm_g=ffn_norm_g,
             w_gate_up=w_gate_up, w_down=w_down)
    m = dict(mix_norm_g=m_mix_norm_g, w_in=m_w_in, q_norm_g=m_q_norm_g, k_norm_g=m_k_norm_g, pool_w=m_pool_w,
             pool_scale=m_pool_scale, w_out=m_w_out, cross_norm_g=m_cross_norm_g, mem_norm_g=m_mem_norm_g,
             w_cq=m_w_cq, w_ckv=m_w_ckv, cq_norm_g=m_cq_norm_g, ck_norm_g=m_ck_norm_g, w_co=m_w_co,
             ffn_norm_g=m_ffn_norm_g, w_gate_up=m_w_gate_up, w_down=m_w_down)
    v = dict(mix_norm_g=v_mix_norm_g, w_in=v_w_in, q_norm_g=v_q_norm_g, k_norm_g=v_k_norm_g, pool_w=v_pool_w,
             pool_scale=v_pool_scale, w_out=v_w_out, cross_norm_g=v_cross_norm_g, mem_norm_g=v_mem_norm_g,
             w_cq=v_w_cq, w_ckv=v_w_ckv, cq_norm_g=v_cq_norm_g, ck_norm_g=v_ck_norm_g, w_co=v_w_co,
             ffn_norm_g=v_ffn_norm_g, w_gate_up=v_w_gate_up, w_down=v_w_down)

    c_arr = lax.axis_index("c").astype(jnp.int32).reshape(1)
    k_arr = (2 * lax.axis_index("x") + lax.axis_index("y")).astype(jnp.int32).reshape(1)
    kc_arr = jnp.concatenate([k_arr, c_arr])
    wb = {k: _cast_piece(f"cast_{k}", w[k][0], k_arr) for k in BIG}
    sm = {k: (w[k][0] if k == "pool_w" else w[k]) for k in SMALL}
    loss_part, dx, gshard, gsm = _local_step(x[0], mem[0], positions[0], loss_target[0], wb, sm,
                                             place=(c_arr, kc_arr))
    loss = lax.psum(loss_part, ("x", "y", "c"))

    grads, deltas, new_m, new_v = {}, {}, {}, {}
    for k in BIG:
        shard = w[k][0]
        tr = shard.shape[0] // 4
        d, nm, nv = _adamw(f"adamw_{k}", shard, gshard[k], m[k][0], v[k][0], tr=tr)
        grads[k], deltas[k], new_m[k], new_v[k] = gshard[k][None], d[None], nm[None], nv[None]
    smw = {k: (w[k][0] if k == "pool_w" else w[k]) for k in SMALL}
    smm = {k: (m[k][0] if k == "pool_w" else m[k]) for k in SMALL}
    smv = {k: (v[k][0] if k == "pool_w" else v[k]) for k in SMALL}
    pw, pg, pm, pv = _pack_small(smw), _pack_small(gsm), _pack_small(smm), _pack_small(smv)
    d, nm, nv = _adamw("adamw_small", pw, pg, pm, pv, tr=pw.shape[0])
    for dst, packed in ((deltas, d), (new_m, nm), (new_v, nv)):
        un = _unpack_small(packed, sm)
        for k in SMALL:
            dst[k] = un[k].reshape(w[k].shape)
    for k in SMALL:
        grads[k] = gsm[k].reshape(w[k].shape)

    return (loss, dx[None], *[grads[k] for k in WEIGHTS], *[deltas[k] for k in WEIGHTS],
            *[new_m[k] for k in WEIGHTS], *[new_v[k] for k in WEIGHTS])
```

```python
import functools

import jax
import jax.numpy as jnp
from jax import lax
from jax.experimental import pallas as pl
from jax.experimental.pallas import tpu as pltpu

F32 = jnp.float32
BF16 = jnp.bfloat16
MESH = pl.DeviceIdType.MESH
ANY = pl.BlockSpec(memory_space=pl.ANY)

D_MODEL = 1024
HEAD = 128
N_HEADS = 4
DILATIONS = (1, 4, 16)
BLOCK = 128
Q_W = 1536
KV_W = 512
POOL_W = 512
POOL_WINDOWS = (2, 4, 8, 16)
POOL_HALO = 16
IN_W = 3072
ROT_DIM = 32
ROT_HALF = 16
ROPE_THETA = 500000.0
X_W = 512
D_FF = 2816
EPS = 1e-6
NEG_INF = -1e30
SCALE = HEAD ** -0.5
N_CHIPS = 4

ADAM_LR = 0.001
ADAM_B1 = 0.9
ADAM_B2 = 0.999
ADAM_EPS = 1e-08
ADAM_WD = 0.01
ADAM_STEP = 10

VMEM_BYTES_V7X = 64 * 2 ** 20
VMEM_LIMIT_MAX = 56 * 2 ** 20
VMEM_LIMIT_MIN = 24 * 2 ** 20


def _nbytes(shape, dtype):
    n = 1
    for s in shape:
        n *= s
    return n * jnp.dtype(dtype).itemsize


def _cparams(n_axes, block_bytes, scratch_bytes=0):
    est = 2 * (2 * block_bytes + scratch_bytes)
    lim = int(min(VMEM_LIMIT_MAX, max(VMEM_LIMIT_MIN, est)))
    return pltpu.CompilerParams(dimension_semantics=("arbitrary",) * n_axes, vmem_limit_bytes=lim)


def _bf(v):
    return v.astype(BF16)


def _dot(a, b):
    return jnp.dot(a, b, preferred_element_type=F32)


def _dot_nt(a, b):
    return lax.dot_general(a, b, (((1,), (1,)), ((), ())), preferred_element_type=F32)


def _dot_tn(a, b):
    return lax.dot_general(a, b, (((0,), (0,)), ((), ())), preferred_element_type=F32)


def _rstd(v):
    return lax.rsqrt(jnp.mean(v * v, axis=-1, keepdims=True) + EPS)


def _norm_bwd(dy, xv, g):
    r = _rstd(xv)
    xh = xv * r
    dxh = dy * g
    dx = r * (dxh - xh * jnp.mean(dxh * xh, axis=-1, keepdims=True))
    return dx, xh


def _place():
    x, y, c = lax.axis_index("x"), lax.axis_index("y"), lax.axis_index("c")
    other_chips = [(1 - x, y), (x, 1 - y), (1 - x, 1 - y)]
    return x, y, c, other_chips


class _Plan:
    ins = ()
    out_shapes = ()
    aliases = {}
    sem_shapes = ()

    def copies(self, ins, outs, sems):
        raise NotImplementedError

    def begin(self, ins, outs, sems):
        for cp in self.copies(ins, outs, sems):
            cp.start()

    def finish(self, ins, outs, sems):
        for cp in self.copies(ins, outs, sems):
            cp.wait()


class _GatherPlan(_Plan):
    def __init__(self, bufs):
        n = len(bufs)
        self.ins = list(bufs)
        self.out_shapes = [_sds(b.shape, b.dtype) for b in bufs]
        self.aliases = {i: i for i in range(n)}
        self.sem_shapes = [pltpu.SemaphoreType.DMA((n, 6)), pltpu.SemaphoreType.DMA((n, 6))]

    def _parts(self, outs, sems):
        send, recv = sems
        x, y, c, chips = _place()

        def half(i, piece, which):
            hr = outs[i].shape[1] // 2
            return outs[i].at[piece, pl.ds(which * hr, hr), :]

        def copy(i, k, ref, to):
            return pltpu.make_async_remote_copy(src_ref=ref, dst_ref=ref, send_sem=send.at[i, k], recv_sem=recv.at[i, k],
                                                device_id=to, device_id_type=MESH)

        return x, y, c, chips, half, copy

    def begin(self, ins, outs, sems):
        x, y, c, chips, half, copy = self._parts(outs, sems)
        for i in range(len(outs)):
            for j, (cx, cy) in enumerate(chips):
                copy(i, j, half(i, 2 * x + y, c), (cx, cy, c)).start()

    def finish(self, ins, outs, sems):
        x, y, c, chips, half, copy = self._parts(outs, sems)
        sib = (x, y, 1 - c)
        n = len(outs)
        for i in range(n):
            for j, (cx, cy) in enumerate(chips):
                piece = half(i, 2 * cx + cy, c)
                copy(i, j, piece, (cx, cy, c)).wait_recv()
                copy(i, 3 + j, piece, sib).start()
        for i in range(n):
            for j, (cx, cy) in enumerate(chips):
                copy(i, 3 + j, half(i, 2 * cx + cy, 1 - c), sib).wait_recv()
        for i in range(n):
            for j, (cx, cy) in enumerate(chips):
                copy(i, j, half(i, 2 * x + y, c), (cx, cy, c)).wait_send()
                copy(i, 3 + j, half(i, 2 * cx + cy, c), sib).wait_send()


class _SwapPlan(_Plan):
    def __init__(self, grads):
        n = len(grads)
        self.ins = list(grads)
        self.out_shapes = [_sds((g.shape[0], g.shape[1] // 2, g.shape[2]), g.dtype) for g in grads]
        self.sem_shapes = [pltpu.SemaphoreType.DMA((n,)), pltpu.SemaphoreType.DMA((n,))]

    def copies(self, ins, outs, sems):
        send, recv = sems
        x, y, c, _ = _place()
        cps = []
        for i in range(len(ins)):
            hr = ins[i].shape[1] // 2
            cps.append(pltpu.make_async_remote_copy(
                src_ref=ins[i].at[:, pl.ds((1 - c) * hr, hr), :], dst_ref=outs[i], send_sem=send.at[i],
                recv_sem=recv.at[i], device_id=(x, y, 1 - c), device_id_type=MESH))
        return cps


class _ExchangePlan(_Plan):
    def __init__(self, sums):
        n = len(sums)
        self.ins = list(sums)
        self.out_shapes = [_sds((N_CHIPS - 1,) + s.shape[1:], s.dtype) for s in sums]
        self.sem_shapes = [pltpu.SemaphoreType.DMA((n, 3)), pltpu.SemaphoreType.DMA((n, 3))]

    def copies(self, ins, outs, sems):
        send, recv = sems
        x, y, c, chips = _place()
        cps = []
        for i in range(len(ins)):
            for j, (cx, cy) in enumerate(chips):
                cps.append(pltpu.make_async_remote_copy(
                    src_ref=ins[i].at[2 * cx + cy], dst_ref=outs[i].at[j], send_sem=send.at[i, j],
                    recv_sem=recv.at[i, j], device_id=(cx, cy, c), device_id_type=MESH))
        return cps


class _JoinPlan(_Plan):
    def __init__(self, shards):
        n = len(shards)
        self.ins = list(shards)
        self.out_shapes = [_sds(s.shape, s.dtype) for s in shards]
        self.aliases = {i: i for i in range(n)}
        self.sem_shapes = [pltpu.SemaphoreType.DMA((n,)), pltpu.SemaphoreType.DMA((n,))]

    def copies(self, ins, outs, sems):
        send, recv = sems
        x, y, c, _ = _place()
        cps = []
        for i in range(len(outs)):
            hr = outs[i].shape[0] // 2
            mine = outs[i].at[pl.ds(c * hr, hr), :]
            cps.append(pltpu.make_async_remote_copy(src_ref=mine, dst_ref=mine, send_sem=send.at[i], recv_sem=recv.at[i],
                                                    device_id=(x, y, 1 - c), device_id_type=MESH))
        return cps


def _run_plan(name, plan):
    n_in, n_out = len(plan.ins), len(plan.out_shapes)

    def body(*refs):
        ins, outs, sems = refs[:n_in], refs[n_in:n_in + n_out], refs[n_in + n_out:]
        plan.begin(ins, outs, sems)
        plan.finish(ins, outs, sems)

    return pl.pallas_call(
        body, name=name, in_specs=[ANY] * n_in, out_specs=[ANY] * n_out, out_shape=list(plan.out_shapes),
        input_output_aliases=dict(plan.aliases), scratch_shapes=list(plan.sem_shapes))(*plan.ins)


HBM_PIN_BYTES = 1 << 20


def _sds(shape, dtype):
    if _nbytes(shape, dtype) >= HBM_PIN_BYTES:
        return pltpu.HBM(shape, dtype)
    return jax.ShapeDtypeStruct(shape, dtype)


def _pin(args):
    return [pltpu.with_memory_space_constraint(a, pltpu.HBM) if _nbytes(a.shape, a.dtype) >= HBM_PIN_BYTES else a
            for a in args]


def _pcall(body, *, name, grid, in_specs, out_specs, out_shape, params, args, scratch_shapes=(), plan=None):
    in_specs, out_specs, out_shape, scratch = list(in_specs), list(out_specs), list(out_shape), list(scratch_shapes)
    args = _pin(args)
    if plan is None:
        res = pl.pallas_call(body, name=name, grid=grid, in_specs=in_specs, out_specs=out_specs, out_shape=out_shape,
                             scratch_shapes=scratch, compiler_params=params)(*args)
        return list(res), []
    ni, no, ns = len(in_specs), len(out_specs), len(scratch)
    pi, po = len(plan.ins), len(plan.out_shapes)

    def wrapped(*refs):
        ins, pins = refs[:ni], refs[ni:ni + pi]
        outs, pouts = refs[ni + pi:ni + pi + no], refs[ni + pi + no:ni + pi + no + po]
        scr, psems = refs[ni + pi + no + po:ni + pi + no + po + ns], refs[ni + pi + no + po + ns:]
        ids = [pl.program_id(a) for a in range(len(grid))]
        first = functools.reduce(jnp.logical_and, [i == 0 for i in ids])
        last = functools.reduce(jnp.logical_and, [i == g - 1 for i, g in zip(ids, grid)])

        @pl.when(first)
        def _():
            plan.begin(pins, pouts, psems)

        body(*ins, *outs, *scr)

        @pl.when(last)
        def _():
            plan.finish(pins, pouts, psems)

    res = pl.pallas_call(
        wrapped, name=name, grid=grid, in_specs=in_specs + [ANY] * pi, out_specs=out_specs + [ANY] * po,
        out_shape=out_shape + list(plan.out_shapes), scratch_shapes=scratch + list(plan.sem_shapes),
        input_output_aliases={ni + a: no + b for a, b in plan.aliases.items()},
        compiler_params=params)(*args, *plan.ins)
    return list(res[:no]), list(res[no:])


def _mm_nn(name, a, w3, *, tm, norm_g=None, residual=None, out_dtype=F32, plan=None):
    M, K = a.shape
    P, Kw, C = w3.shape
    assert Kw == K and M % tm == 0
    N = P * C
    has_norm = norm_g is not None
    has_res = residual is not None

    def body(*refs):
        refs = list(refs)
        a_ref = refs.pop(0)
        w_ref = refs.pop(0)
        g_ref = refs.pop(0) if has_norm else None
        r_ref = refs.pop(0) if has_res else None
        o_ref = refs.pop(0)
        xn_ref = refs.pop(0) if has_norm else None
        if has_norm:
            av = a_ref[...].astype(F32)
            ab = _bf(av * _rstd(av) * g_ref[...])
            xn_ref[...] = ab
        else:
            ab = _bf(a_ref[...])
        for p in range(P):
            acc = _dot(ab, w_ref[p])
            if has_res:
                acc = acc + r_ref[:, p * C:(p + 1) * C]
            o_ref[:, p * C:(p + 1) * C] = acc.astype(o_ref.dtype)

    in_specs = [pl.BlockSpec((tm, K), lambda i: (i, 0)), pl.BlockSpec((P, K, C), lambda i: (0, 0, 0))]
    args = [a, w3]
    if has_norm:
        in_specs.append(pl.BlockSpec((1, K), lambda i: (0, 0)))
        args.append(norm_g)
    if has_res:
        in_specs.append(pl.BlockSpec((tm, N), lambda i: (i, 0)))
        args.append(residual)
    out_shape = [_sds((M, N), out_dtype)]
    out_specs = [pl.BlockSpec((tm, N), lambda i: (i, 0))]
    if has_norm:
        out_shape.append(_sds((M, K), BF16))
        out_specs.append(pl.BlockSpec((tm, K), lambda i: (i, 0)))
    blk = (_nbytes((tm, K), a.dtype) + _nbytes((P, K, C), BF16) + 2 * _nbytes((tm, N), F32)
           + _nbytes((tm, K), BF16))
    res, got = _pcall(body, name=name, grid=(M // tm,), in_specs=in_specs, out_specs=out_specs, out_shape=out_shape,
                      params=_cparams(1, blk), args=args, plan=plan)
    res = res if has_norm else res[0]
    return res if plan is None else (res, got)


MXU_COLS_V7X = 256


def _col_chunks(n):
    return [(c, min(c + MXU_COLS_V7X, n)) for c in range(0, n, MXU_COLS_V7X)]


def _ffn_up(name, h, wgu3, g, *, tm):
    M, K = h.shape
    P, _, C = wgu3.shape
    half = P // 2

    def body(h_ref, wg_ref, wu_ref, g_ref, act_ref, gu_ref, xn_ref, ab_ref):
        @pl.when(pl.program_id(1) == 0)
        def _():
            hv = h_ref[...]
            xn = _bf(hv * _rstd(hv) * g_ref[...])
            ab_ref[...] = xn
            xn_ref[...] = xn
        ab = ab_ref[...]
        for c0, c1 in _col_chunks(C):
            gate = _dot(ab, wg_ref[:, c0:c1])
            up = _dot(ab, wu_ref[:, c0:c1])
            act_ref[:, c0:c1] = _bf(gate * (1.0 / (1.0 + jnp.exp(-gate))) * up)
            gu_ref[0, :, c0:c1] = _bf(gate)
            gu_ref[1, :, c0:c1] = _bf(up)

    blk = (_nbytes((tm, K), F32) + 2 * _nbytes((K, C), BF16) + 3 * _nbytes((tm, C), BF16)
           + 2 * _nbytes((tm, C), F32) + _nbytes((tm, K), BF16))
    return pl.pallas_call(
        body, name=name, grid=(M // tm, half),
        in_specs=[pl.BlockSpec((tm, K), lambda i, j: (i, 0)),
                  pl.BlockSpec((None, K, C), lambda i, j: (j, 0, 0)),
                  pl.BlockSpec((None, K, C), lambda i, j: (j + half, 0, 0)),
                  pl.BlockSpec((1, K), lambda i, j: (0, 0))],
        out_specs=[pl.BlockSpec((tm, C), lambda i, j: (i, j)),
                   pl.BlockSpec((2, tm, C), lambda i, j: (0, i, j)),
                   pl.BlockSpec((tm, K), lambda i, j: (i, 0))],
        out_shape=[_sds((M, half * C), BF16),
                   _sds((2, M, half * C), BF16),
                   _sds((M, K), BF16)],
        scratch_shapes=[pltpu.VMEM((tm, K), BF16)],
        compiler_params=_cparams(2, blk, _nbytes((tm, K), BF16)))(*_pin([h, wgu3, wgu3, g]))


def _ffn_down_loss(name, act, wd3, h, target, *, tm):
    M, K = act.shape
    _, _, N = wd3.shape

    def body(a_ref, w_ref, h_ref, t_ref, dy_ref, ls_ref):
        err = _dot(a_ref[...], w_ref[...]) + h_ref[...] - t_ref[...]
        dy_ref[...] = err * (1.0 / N)

        @pl.when(pl.program_id(0) == 0)
        def _():
            ls_ref[...] = jnp.zeros_like(ls_ref)
        ls_ref[...] += jnp.sum(err * err, axis=0, keepdims=True)

    blk = _nbytes((tm, K), BF16) + _nbytes((K, N), BF16) + 4 * _nbytes((tm, N), F32)
    return pl.pallas_call(
        body, name=name, grid=(M // tm,),
        in_specs=[pl.BlockSpec((tm, K), lambda i: (i, 0)),
                  pl.BlockSpec((None, K, N), lambda i: (0, 0, 0)),
                  pl.BlockSpec((tm, N), lambda i: (i, 0)),
                  pl.BlockSpec((tm, N), lambda i: (i, 0))],
        out_specs=[pl.BlockSpec((tm, N), lambda i: (i, 0)),
                   pl.BlockSpec((1, N), lambda i: (0, 0))],
        out_shape=[_sds((M, N), F32), _sds((1, N), F32)],
        compiler_params=_cparams(1, blk))(*_pin([act, wd3, h, target]))


def _nt_pieces(a_ref, w_ref):
    P, _, C = w_ref.shape
    acc = _dot_nt(_bf(a_ref[:, 0:C]), w_ref[0])
    for p in range(1, P):
        acc = acc + _dot_nt(_bf(a_ref[:, p * C:(p + 1) * C]), w_ref[p])
    return acc


def _mm_nt(name, a, w3, *, tm, out_dtype):
    M, N = a.shape
    P, Ko, C = w3.shape
    assert N == P * C and M % tm == 0

    def body(a_ref, w_ref, o_ref):
        o_ref[...] = _nt_pieces(a_ref, w_ref).astype(o_ref.dtype)

    blk = _nbytes((tm, N), a.dtype) + _nbytes((P, Ko, C), BF16) + 2 * _nbytes((tm, Ko), F32)
    return pl.pallas_call(
        body, name=name, grid=(M // tm,),
        in_specs=[pl.BlockSpec((tm, N), lambda i: (i, 0)), pl.BlockSpec((P, Ko, C), lambda i: (0, 0, 0))],
        out_specs=pl.BlockSpec((tm, Ko), lambda i: (i, 0)),
        out_shape=_sds((M, Ko), out_dtype),
        compiler_params=_cparams(1, blk))(*_pin([a, w3]))


def _mm_nt_normbwd(name, a, w3, h, g, dres, *, tm, plan=None):
    M, D = h.shape
    P, Ko, C = w3.shape
    N = a.shape[1]
    assert N == P * C and Ko == D and M % tm == 0
    has_res = dres is not None

    def body(*refs):
        a_ref, w_ref, h_ref, g_ref = refs[:4]
        r_ref = refs[4] if has_res else None
        dx_ref, dg_ref = refs[-2:]
        dhn = _nt_pieces(a_ref, w_ref)
        dx, xh = _norm_bwd(dhn, h_ref[...], g_ref[...])
        if has_res:
            dx = dx + r_ref[...]
        dx_ref[...] = dx

        @pl.when(pl.program_id(0) == 0)
        def _():
            dg_ref[...] = jnp.zeros_like(dg_ref)
        dg_ref[...] += jnp.sum(dhn * xh, axis=0, keepdims=True)

    row = pl.BlockSpec((tm, D), lambda i: (i, 0))
    vec = pl.BlockSpec((1, D), lambda i: (0, 0))
    in_specs = [pl.BlockSpec((tm, N), lambda i: (i, 0)), pl.BlockSpec((P, Ko, C), lambda i: (0, 0, 0)), row, vec]
    args = [a, w3, h, g]
    if has_res:
        in_specs.append(row)
        args.append(dres)
    blk = _nbytes((tm, N), a.dtype) + _nbytes((P, Ko, C), BF16) + 5 * _nbytes((tm, D), F32)
    res, got = _pcall(body, name=name, grid=(M // tm,), in_specs=in_specs, out_specs=[row, vec],
                      out_shape=[_sds((M, D), F32), _sds((1, D), F32)],
                      params=_cparams(1, blk), args=args, plan=plan)
    return res if plan is None else (res, got)


def _ffn_up_bwd(name, dgu, wgu3, h, g, dres, *, tm):
    M, D = h.shape
    P, _, C = wgu3.shape
    F = dgu.shape[2]
    per = F // C

    def body(a_ref, w_ref, h_ref, g_ref, r_ref, dx_ref, dg_ref):
        dhn = None
        for p in range(P):
            part = _dot_nt(a_ref[p // per, :, (p % per) * C:(p % per + 1) * C], w_ref[p])
            dhn = part if dhn is None else dhn + part
        dx, xh = _norm_bwd(dhn, h_ref[...], g_ref[...])
        dx_ref[...] = dx + r_ref[...]

        @pl.when(pl.program_id(0) == 0)
        def _():
            dg_ref[...] = jnp.zeros_like(dg_ref)
        dg_ref[...] += jnp.sum(dhn * xh, axis=0, keepdims=True)

    row = pl.BlockSpec((tm, D), lambda i: (i, 0))
    vec = pl.BlockSpec((1, D), lambda i: (0, 0))
    blk = _nbytes((2, tm, F), BF16) + _nbytes((P, D, C), BF16) + 5 * _nbytes((tm, D), F32)
    return pl.pallas_call(
        body, name=name, grid=(M // tm,),
        in_specs=[pl.BlockSpec((2, tm, F), lambda i: (0, i, 0)),
                  pl.BlockSpec((P, D, C), lambda i: (0, 0, 0), pipeline_mode=pl.Buffered(1)), row, vec, row],
        out_specs=[row, vec],
        out_shape=[_sds((M, D), F32), _sds((1, D), F32)],
        compiler_params=_cparams(1, blk))(*_pin([dgu, wgu3, h, g, dres]))


def _ffn_down_bwd(name, dy, wd3, gu, *, tm, tn):
    M, D = dy.shape
    _, F, _ = wd3.shape

    def body(dy_ref, w_ref, gu_ref, o_ref, ab_ref):
        @pl.when(pl.program_id(1) == 0)
        def _():
            ab_ref[...] = _bf(dy_ref[...])
        ab = ab_ref[...]
        for c0, c1 in _col_chunks(tn):
            da = _dot_nt(ab, w_ref[c0:c1, :])
            gate = gu_ref[0, :, c0:c1].astype(F32)
            up = gu_ref[1, :, c0:c1].astype(F32)
            sig = 1.0 / (1.0 + jnp.exp(-gate))
            silu = gate * sig
            o_ref[0, :, c0:c1] = _bf(da * up * (sig + silu * (1.0 - sig)))
            o_ref[1, :, c0:c1] = _bf(da * silu)

    blk = (_nbytes((tm, D), F32) + _nbytes((tn, D), BF16) + 4 * _nbytes((tm, tn), BF16)
           + 4 * _nbytes((tm, tn), F32))
    return pl.pallas_call(
        body, name=name, grid=(M // tm, F // tn),
        in_specs=[pl.BlockSpec((tm, D), lambda i, j: (i, 0)),
                  pl.BlockSpec((None, tn, D), lambda i, j: (0, j, 0)),
                  pl.BlockSpec((2, tm, tn), lambda i, j: (0, i, j))],
        out_specs=pl.BlockSpec((2, tm, tn), lambda i, j: (0, i, j)),
        out_shape=_sds((2, M, F), BF16),
        scratch_shapes=[pltpu.VMEM((tm, D), BF16)],
        compiler_params=_cparams(2, blk, _nbytes((tm, D), BF16)))(*_pin([dy, wd3, gu]))


def _mm_tn_wide(name, a, b, pieces, *, ts, plan=None):
    S, K = a.shape
    N = b.shape[1]
    C = N // pieces

    def body(a_ref, b_ref, o_ref):
        @pl.when(pl.program_id(0) == 0)
        def _():
            o_ref[...] = jnp.zeros_like(o_ref)
        acc = _dot_tn(_bf(a_ref[...]), _bf(b_ref[...]))
        for p in range(pieces):
            o_ref[p] += acc[:, p * C:(p + 1) * C]

    blk = _nbytes((ts, K), a.dtype) + _nbytes((ts, N), b.dtype) + 2 * _nbytes((K, N), F32)
    res, got = _pcall(body, name=name, grid=(S // ts,),
                      in_specs=[pl.BlockSpec((ts, K), lambda s: (s, 0)), pl.BlockSpec((ts, N), lambda s: (s, 0))],
                      out_specs=[pl.BlockSpec((pieces, K, C), lambda s: (0, 0, 0))],
                      out_shape=[_sds((pieces, K, C), F32)], params=_cparams(1, blk), args=[a, b], plan=plan)
    return res[0] if plan is None else (res[0], got)


def _mm_tn(name, a, a_spec, a_blk, b, b_spec, b_blk, out3, o_spec, o_blk, grid, plan=None):
    def body(a_ref, b_ref, o_ref):
        @pl.when(pl.program_id(2) == 0)
        def _():
            o_ref[...] = jnp.zeros_like(o_ref)
        o_ref[...] += _dot_tn(_bf(a_ref[...]), _bf(b_ref[...]))

    blk = _nbytes(a_blk, a.dtype) + _nbytes(b_blk, b.dtype) + 2 * _nbytes(o_blk, F32)
    res, got = _pcall(body, name=name, grid=grid, in_specs=[a_spec, b_spec], out_specs=[o_spec],
                      out_shape=[_sds(out3, F32)], params=_cparams(3, blk), args=[a, b], plan=plan)
    return res[0] if plan is None else (res[0], got)


def _swap_halves(v, lane, masked):
    up = pltpu.roll(v, HEAD - ROT_HALF, 1)
    down = pltpu.roll(v, ROT_HALF, 1)
    rest = jnp.where(lane < ROT_DIM, down, 0.0) if masked else down
    return jnp.where(lane < ROT_HALF, up, rest)


def _qk_prep(name, proj, ctab, stab, qg, kg, *, tm, plan=None):
    S = proj.shape[0]
    width = Q_W + KV_W

    def body(p_ref, c_ref, s_ref, qg_ref, kg_ref, q_ref, k_ref):
        lane = lax.broadcasted_iota(jnp.int32, (tm, HEAD), 1)
        cv = c_ref[...]
        sv = s_ref[...]

        def prep(t, g, scale):
            n = t * _rstd(t) * g
            return (n * cv + _swap_halves(n, lane, False) * sv) * scale

        for j in range(Q_W // HEAD):
            q_ref[:, j * HEAD:(j + 1) * HEAD] = prep(p_ref[:, j * HEAD:(j + 1) * HEAD], qg_ref[...], SCALE)
        for j in range(KV_W // HEAD):
            k_ref[:, j * HEAD:(j + 1) * HEAD] = prep(p_ref[:, Q_W + j * HEAD:Q_W + (j + 1) * HEAD], kg_ref[...], 1.0)

    blk = 2 * _nbytes((tm, width), F32) + 2 * _nbytes((tm, HEAD), F32)
    res, got = _pcall(
        body, name=name, grid=(S // tm,),
        in_specs=[pl.BlockSpec((tm, width), lambda i: (i, 0)),
                  pl.BlockSpec((tm, HEAD), lambda i: (i, 0)),
                  pl.BlockSpec((tm, HEAD), lambda i: (i, 0)),
                  pl.BlockSpec((1, HEAD), lambda i: (0, 0)),
                  pl.BlockSpec((1, HEAD), lambda i: (0, 0))],
        out_specs=[pl.BlockSpec((tm, Q_W), lambda i: (i, 0)), pl.BlockSpec((tm, KV_W), lambda i: (i, 0))],
        out_shape=[_sds((S, Q_W), F32), _sds((S, KV_W), F32)],
        params=_cparams(1, blk), args=[proj, ctab, stab, qg, kg], plan=plan)
    return res if plan is None else (res, got)


ATTN_STEP = 2048


def _row_idx(start, dil):
    return pl.ds(start, BLOCK) if dil == 1 else pl.ds(start, BLOCK, stride=dil)


def _rows(ref, start, dil):
    return ref[_row_idx(start, dil), :]


def _set_rows(ref, start, dil, val):
    ref[_row_idx(start, dil), :] = val


def _add_rows(ref, start, dil, val):
    idx = _row_idx(start, dil)
    ref[idx, :] = ref[idx, :] + val


def _band_mask(first):
    qi = lax.broadcasted_iota(jnp.int32, (BLOCK, 2 * BLOCK), 0)
    kj = lax.broadcasted_iota(jnp.int32, (BLOCK, 2 * BLOCK), 1)
    band = (kj >= qi) & (kj <= qi + BLOCK)
    if first is None:
        return band
    return band & ((kj >= BLOCK) | jnp.logical_not(first))


def _attn_geometry(S, grp):
    dil = DILATIONS[grp]
    bt = BLOCK * dil
    assert S % ATTN_STEP == 0 and ATTN_STEP % bt == 0
    return dil, bt, ATTN_STEP // bt, S // ATTN_STEP


def _attn_fwd(name, qn, kn, proj, grp, plan=None):
    S = qn.shape[0]
    dil, bt, nsub, nsb = _attn_geometry(S, grp)
    vcol = (Q_W + KV_W) // HEAD

    def body(q_ref, kp_ref, kc_ref, vp_ref, vc_ref, o_ref, l_ref):
        valid_first = _band_mask(pl.program_id(1) == 0)
        valid_inner = _band_mask(None)
        for r in range(dil):
            kprev, vprev = _bf(_rows(kp_ref, r, dil)), _bf(_rows(vp_ref, r, dil))
            for b in range(nsub):
                at = b * bt + r
                kcur, vcur = _bf(_rows(kc_ref, at, dil)), _bf(_rows(vc_ref, at, dil))
                q = _bf(_rows(q_ref, at, dil))
                k2 = jnp.concatenate([kprev, kcur], axis=0)
                v2 = jnp.concatenate([vprev, vcur], axis=0)
                s = jnp.where(valid_first if b == 0 else valid_inner, _dot_nt(q, k2), NEG_INF)
                m = jnp.max(s, axis=-1, keepdims=True)
                p = jnp.exp(s - m)
                l = jnp.sum(p, axis=-1, keepdims=True)
                acc = _dot(_bf(p), v2)
                _set_rows(o_ref, at, dil, acc / l)
                _set_rows(l_ref, at, dil, jnp.broadcast_to(m + jnp.log(l), (BLOCK, HEAD)))
                kprev, vprev = kcur, vcur

    prev = lambda n: jnp.maximum(n * nsub - 1, 0)
    big = lambda col: pl.BlockSpec((ATTN_STEP, HEAD), lambda h, n: (n, col(h)))
    tail = lambda col: pl.BlockSpec((bt, HEAD), lambda h, n: (prev(n), col(h)))
    blk = 5 * _nbytes((ATTN_STEP, HEAD), F32) + 2 * _nbytes((bt, HEAD), F32)
    res, got = _pcall(
        body, name=name, grid=(N_HEADS, nsb),
        in_specs=[big(lambda h: grp * N_HEADS + h), tail(lambda h: h), big(lambda h: h),
                  tail(lambda h: vcol + h), big(lambda h: vcol + h)],
        out_specs=[big(lambda h: h), big(lambda h: h)],
        out_shape=[_sds((S, KV_W), F32), _sds((S, KV_W), F32)],
        params=_cparams(2, blk), args=[qn, kn, kn, proj, proj], plan=plan)
    return res if plan is None else (res, got)


def _attn_bwd(name, qn, kn, proj, dattn, lse, delta, grp, plan=None):
    S = qn.shape[0]
    dil, bt, nsub, nsb = _attn_geometry(S, grp)
    vcol = (Q_W + KV_W) // HEAD

    def body(q_ref, kp_ref, kc_ref, vp_ref, vc_ref, do_ref, l_ref, d_ref, dq_ref, dk_ref, dv_ref, ck_ref, cv_ref):
        n = pl.program_id(1)
        par = n % 2

        @pl.when(n < nsb)
        def _():
            valid_first = _band_mask(n == 0)
            valid_inner = _band_mask(None)
            ck, cv = ck_ref.at[par], cv_ref.at[par]
            pk, pv = ck_ref.at[1 - par], cv_ref.at[1 - par]
            for r in range(dil):
                kprev, vprev = _bf(_rows(kp_ref, r, dil)), _bf(_rows(vp_ref, r, dil))
                own_k = own_v = None
                for b in range(nsub):
                    at = b * bt + r
                    kcur, vcur = _bf(_rows(kc_ref, at, dil)), _bf(_rows(vc_ref, at, dil))
                    q = _bf(_rows(q_ref, at, dil))
                    k2 = jnp.concatenate([kprev, kcur], axis=0)
                    v2 = jnp.concatenate([vprev, vcur], axis=0)
                    do = _bf(_rows(do_ref, at, dil))
                    lse_r = _rows(l_ref, at, dil)[:, :1]
                    del_r = _rows(d_ref, at, dil)[:, :1]
                    s = jnp.where(valid_first if b == 0 else valid_inner, _dot_nt(q, k2), NEG_INF)
                    p = jnp.exp(s - lse_r)
                    ds = _bf(p * (_dot_nt(do, v2) - del_r))
                    _set_rows(dq_ref, at, dil, _dot(ds, k2))
                    dk2 = _dot_tn(ds, q)
                    dv2 = _dot_tn(_bf(p), do)
                    if b > 0:
                        _set_rows(ck, at - bt, dil, own_k + dk2[:BLOCK])
                        _set_rows(cv, at - bt, dil, own_v + dv2[:BLOCK])
                    else:
                        @pl.when(n > 0)
                        def _():
                            _add_rows(pk, (nsub - 1) * bt + r, dil, dk2[:BLOCK])
                            _add_rows(pv, (nsub - 1) * bt + r, dil, dv2[:BLOCK])
                    own_k, own_v = dk2[BLOCK:], dv2[BLOCK:]
                    kprev, vprev = kcur, vcur
                _set_rows(ck, (nsub - 1) * bt + r, dil, own_k)
                _set_rows(cv, (nsub - 1) * bt + r, dil, own_v)

        @pl.when(n > 0)
        def _():
            dk_ref[...] = ck_ref[1 - par]
            dv_ref[...] = cv_ref[1 - par]

    cur = lambda n: jnp.minimum(n, nsb - 1)
    prev = lambda n: jnp.maximum(n - 1, 0)
    tail_at = lambda n: jnp.maximum(cur(n) * nsub - 1, 0)
    big = lambda col: pl.BlockSpec((ATTN_STEP, HEAD), lambda h, n: (cur(n), col(h)))
    tail = lambda col: pl.BlockSpec((bt, HEAD), lambda h, n: (tail_at(n), col(h)))
    late = pl.BlockSpec((ATTN_STEP, HEAD), lambda h, n: (prev(n), h))
    blk = 9 * _nbytes((ATTN_STEP, HEAD), F32) + 2 * _nbytes((bt, HEAD), F32)
    res, got = _pcall(
        body, name=name, grid=(N_HEADS, nsb + 1),
        in_specs=[big(lambda h: grp * N_HEADS + h), tail(lambda h: h), big(lambda h: h),
                  tail(lambda h: vcol + h), big(lambda h: vcol + h),
                  big(lambda h: h), big(lambda h: h), big(lambda h: h)],
        out_specs=[big(lambda h: h), late, late],
        out_shape=[_sds((S, KV_W), F32)] * 3,
        scratch_shapes=[pltpu.VMEM((2, ATTN_STEP, HEAD), F32), pltpu.VMEM((2, ATTN_STEP, HEAD), F32)],
        params=_cparams(2, blk, 4 * _nbytes((ATTN_STEP, HEAD), F32)),
        args=[qn, kn, kn, proj, proj, dattn, lse, delta], plan=plan)
    return res if plan is None else (res, got)


def _window_sum(v, n_doublings, back):
    rows = v.shape[0]
    step = 1
    for _ in range(n_doublings):
        v = v + pltpu.roll(v, step if back else rows - step, 0)
        step *= 2
    return v


def _mix_post(name, outs, lses, proj, pool_w, pool_scale, *, tm):
    S = proj.shape[0]
    ucol = (IN_W - POOL_W) // POOL_W
    hpt = tm // POOL_HALO

    def body(o0, o1, o2, l0, l1, l2, u_ref, uh_ref, pw_ref, ps_ref, mix_ref, lse_ref, dp_ref):
        i = pl.program_id(0)
        for h in range(N_HEADS):
            sl = slice(h * HEAD, (h + 1) * HEAD)
            ls = [l0[:, sl], l1[:, sl], l2[:, sl]]
            m = jnp.maximum(jnp.maximum(ls[0], ls[1]), ls[2])
            ws = [jnp.exp(v - m) for v in ls]
            den = ws[0] + ws[1] + ws[2]
            num = ws[0] * o0[:, sl] + ws[1] * o1[:, sl] + ws[2] * o2[:, sl]
            mix_ref[:, sl] = _bf(num / den)
            lse_ref[:, sl] = m + jnp.log(den)
        halo = jnp.where(i == 0, 0.0, uh_ref[...])
        t = lax.broadcasted_iota(jnp.int32, (tm + POOL_HALO, HEAD), 0) + (i * tm - POOL_HALO)
        for g, w in enumerate(POOL_WINDOWS):
            sl = slice(g * HEAD, (g + 1) * HEAD)
            ub = jnp.concatenate([halo[:, sl], u_ref[:, sl]], axis=0)
            cnt = jnp.minimum(t + 1, w).astype(F32)
            d = (_window_sum(ub, g + 1, True) / cnt - ub)[POOL_HALO:]
            db = _bf(d)
            dp_ref[:, sl] = db
            mix_ref[:, KV_W + g * HEAD:KV_W + (g + 1) * HEAD] = _bf(_dot(db, pw_ref[g]) * ps_ref[:, sl])

    tile = pl.BlockSpec((tm, KV_W), lambda i: (i, 0))
    blk = 9 * _nbytes((tm, KV_W), F32) + _nbytes((tm, 2 * KV_W), BF16)
    return pl.pallas_call(
        body, name=name, grid=(S // tm,),
        in_specs=[tile] * 6 + [
            pl.BlockSpec((tm, POOL_W), lambda i: (i, ucol)),
            pl.BlockSpec((POOL_HALO, POOL_W), lambda i: (jnp.maximum(i * hpt - 1, 0), ucol)),
            pl.BlockSpec((len(POOL_WINDOWS), HEAD, HEAD), lambda i: (0, 0, 0)),
            pl.BlockSpec((1, POOL_W), lambda i: (0, 0))],
        out_specs=[pl.BlockSpec((tm, 2 * KV_W), lambda i: (i, 0)), tile, tile],
        out_shape=[_sds((S, 2 * KV_W), BF16), _sds((S, KV_W), F32),
                   _sds((S, POOL_W), BF16)],
        compiler_params=_cparams(1, blk))(*_pin([*outs, *lses, proj, proj, pool_w, pool_scale]))


def _mix_bwd(name, dmix, mix, dpool, pool_w, pool_scale, *, tm, plan=None):
    S = dmix.shape[0]
    hpt = tm // POOL_HALO
    last_halo = S // POOL_HALO - 1
    n_tiles = S // tm

    def body(dm_ref, dh_ref, at_ref, dp_ref, pw_ref, ps_ref, da_ref, dl_ref, du_ref, gw_ref, gs_ref):
        i = pl.program_id(0)

        @pl.when(i == 0)
        def _():
            gw_ref[...] = jnp.zeros_like(gw_ref)
            gs_ref[...] = jnp.zeros_like(gs_ref)

        for h in range(N_HEADS):
            sl = slice(h * HEAD, (h + 1) * HEAD)
            da = dm_ref[:, sl]
            da_ref[:, sl] = da
            dl_ref[:, sl] = jnp.broadcast_to(
                jnp.sum(da * at_ref[:, sl].astype(F32), axis=-1, keepdims=True), (tm, HEAD))
        halo = jnp.where(i == n_tiles - 1, 0.0, dh_ref[...])
        t = lax.broadcasted_iota(jnp.int32, (tm + POOL_HALO, HEAD), 0) + i * tm
        for g, w in enumerate(POOL_WINDOWS):
            sl = slice(g * HEAD, (g + 1) * HEAD)
            dy = jnp.concatenate([dm_ref[:, KV_W + g * HEAD:KV_W + (g + 1) * HEAD], halo[:, sl]], axis=0)
            dys = _bf(dy * ps_ref[:, sl])
            dd = _dot_nt(dys, pw_ref[g])
            cnt = jnp.minimum(t + 1, w).astype(F32)
            du_ref[:, sl] = (_window_sum(dd / cnt, g + 1, False) - dd)[:tm]
            db = dp_ref[:, sl]
            gw_ref[g] += _dot_tn(db, dys[:tm])
            gs_ref[:, sl] += jnp.sum(dy[:tm] * _dot(db, pw_ref[g]), axis=0, keepdims=True)

    tile = pl.BlockSpec((tm, KV_W), lambda i: (i, 0))
    blk = _nbytes((tm, 2 * KV_W), F32) + 6 * _nbytes((tm, KV_W), F32)
    res, got = _pcall(
        body, name=name, grid=(n_tiles,),
        in_specs=[pl.BlockSpec((tm, 2 * KV_W), lambda i: (i, 0)),
                  pl.BlockSpec((POOL_HALO, POOL_W), lambda i: (jnp.minimum((i + 1) * hpt, last_halo), 1)),
                  tile, tile,
                  pl.BlockSpec((len(POOL_WINDOWS), HEAD, HEAD), lambda i: (0, 0, 0)),
                  pl.BlockSpec((1, POOL_W), lambda i: (0, 0))],
        out_specs=[tile, tile, tile,
                   pl.BlockSpec((len(POOL_WINDOWS), HEAD, HEAD), lambda i: (0, 0, 0)),
                   pl.BlockSpec((1, POOL_W), lambda i: (0, 0))],
        out_shape=[_sds((S, KV_W), F32)] * 3 + [
            _sds((len(POOL_WINDOWS), HEAD, HEAD), F32), _sds((1, POOL_W), F32)],
        params=_cparams(1, blk), args=[dmix, dmix, mix, dpool, pool_w, pool_scale], plan=plan)
    return res if plan is None else (res, got)


def _qkv_bwd(name, dqs, dks, dvs, du, proj, ctab, stab, qg, kg, *, tm, plan=None):
    S = proj.shape[0]
    width = Q_W + KV_W

    def body(dq0, dq1, dq2, dk0, dk1, dk2, dv0, dv1, dv2, du_ref, p_ref, c_ref, s_ref, qg_ref, kg_ref,
             dp_ref, gq_ref, gk_ref):
        @pl.when(pl.program_id(0) == 0)
        def _():
            gq_ref[...] = jnp.zeros_like(gq_ref)
            gk_ref[...] = jnp.zeros_like(gk_ref)

        lane = lax.broadcasted_iota(jnp.int32, (tm, HEAD), 1)
        cv = c_ref[...]
        sv = s_ref[...]

        def back(dy, t, g, scale):
            dy = dy * scale
            dn = dy * cv + _swap_halves(dy * sv, lane, True)
            dt, xh = _norm_bwd(dn, t, g)
            return dt, jnp.sum(dn * xh, axis=0, keepdims=True)

        dqr = (dq0, dq1, dq2)
        gq = jnp.zeros((1, HEAD), F32)
        for j in range(Q_W // HEAD):
            grp, h = divmod(j, N_HEADS)
            dt, gj = back(dqr[grp][:, h * HEAD:(h + 1) * HEAD], p_ref[:, j * HEAD:(j + 1) * HEAD], qg_ref[...], SCALE)
            dp_ref[:, j * HEAD:(j + 1) * HEAD] = _bf(dt)
            gq = gq + gj
        gq_ref[...] += gq
        gk = jnp.zeros((1, HEAD), F32)
        for h in range(N_HEADS):
            sl = slice(h * HEAD, (h + 1) * HEAD)
            dt, gj = back(dk0[:, sl] + dk1[:, sl] + dk2[:, sl], p_ref[:, Q_W + h * HEAD:Q_W + (h + 1) * HEAD],
                          kg_ref[...], 1.0)
            dp_ref[:, Q_W + h * HEAD:Q_W + (h + 1) * HEAD] = _bf(dt)
            gk = gk + gj
        gk_ref[...] += gk
        dp_ref[:, width:width + KV_W] = _bf(dv0[...] + dv1[...] + dv2[...])
        dp_ref[:, width + KV_W:] = _bf(du_ref[...])

    tile = pl.BlockSpec((tm, KV_W), lambda i: (i, 0))
    vec = pl.BlockSpec((1, HEAD), lambda i: (0, 0))
    rot = pl.BlockSpec((tm, HEAD), lambda i: (i, 0))
    blk = 10 * _nbytes((tm, KV_W), F32) + _nbytes((tm, width), F32) + _nbytes((tm, IN_W), BF16)
    res, got = _pcall(
        body, name=name, grid=(S // tm,),
        in_specs=[tile] * 10 + [pl.BlockSpec((tm, width), lambda i: (i, 0)), rot, rot, vec, vec],
        out_specs=[pl.BlockSpec((tm, IN_W), lambda i: (i, 0)), vec, vec],
        out_shape=[_sds((S, IN_W), BF16), _sds((1, HEAD), F32),
                   _sds((1, HEAD), F32)],
        params=_cparams(1, blk), args=[*dqs, *dks, *dvs, du, proj, ctab, stab, qg, kg], plan=plan)
    return res if plan is None else (res, got)


def _cross_heads(q_ref, kv_ref, qg, kg, h):
    sl = slice(h * HEAD, (h + 1) * HEAD)
    qr = q_ref[:, sl]
    kr = kv_ref[:, sl]
    qh = qr * _rstd(qr) * qg * SCALE
    kh = kr * _rstd(kr) * kg
    vh = kv_ref[:, X_W + h * HEAD:X_W + (h + 1) * HEAD]
    return qr, _bf(qh), _bf(kh), _bf(vh)


def _cross_fwd(name, qraw, kv, qg, kg, *, tm):
    S = qraw.shape[0]
    M = kv.shape[0]

    def body(q_ref, kv_ref, qg_ref, kg_ref, o_ref):
        for h in range(N_HEADS):
            _, qh, kh, vh = _cross_heads(q_ref, kv_ref, qg_ref[...], kg_ref[...], h)
            s = _dot_nt(qh, kh)
            p = jnp.exp(s - jnp.max(s, axis=-1, keepdims=True))
            l = jnp.sum(p, axis=-1, keepdims=True)
            o_ref[:, h * HEAD:(h + 1) * HEAD] = _bf(_dot(_bf(p), vh) / l)

    vec = pl.BlockSpec((1, HEAD), lambda i: (0, 0))
    blk = 2 * _nbytes((tm, X_W), F32) + _nbytes((M, 2 * X_W), F32) + 4 * _nbytes((tm, M), F32)
    return pl.pallas_call(
        body, name=name, grid=(S // tm,),
        in_specs=[pl.BlockSpec((tm, X_W), lambda i: (i, 0)), pl.BlockSpec((M, 2 * X_W), lambda i: (0, 0)), vec, vec],
        out_specs=pl.BlockSpec((tm, X_W), lambda i: (i, 0)),
        out_shape=_sds((S, X_W), BF16),
        compiler_params=_cparams(1, blk))(*_pin([qraw, kv, qg, kg]))


def _cross_bwd(name, do, qraw, kv, qg, kg, *, tm, plan=None):
    S = qraw.shape[0]
    M = kv.shape[0]

    def body(do_ref, q_ref, kv_ref, qg_ref, kg_ref, dq_ref, dk_ref, dv_ref, gq_ref):
        @pl.when(pl.program_id(0) == 0)
        def _():
            dk_ref[...] = jnp.zeros_like(dk_ref)
            dv_ref[...] = jnp.zeros_like(dv_ref)
            gq_ref[...] = jnp.zeros_like(gq_ref)

        gq = jnp.zeros((1, HEAD), F32)
        for h in range(N_HEADS):
            sl = slice(h * HEAD, (h + 1) * HEAD)
            qr, qh, kh, vh = _cross_heads(q_ref, kv_ref, qg_ref[...], kg_ref[...], h)
            doh = _bf(do_ref[:, sl])
            s = _dot_nt(qh, kh)
            p = jnp.exp(s - jnp.max(s, axis=-1, keepdims=True))
            p = p / jnp.sum(p, axis=-1, keepdims=True)
            pb = _bf(p)
            dp = _dot_nt(doh, vh)
            ds = _bf(p * (dp - jnp.sum(dp * p, axis=-1, keepdims=True)))
            dv_ref[:, sl] += _dot_tn(pb, doh)
            dk_ref[:, sl] += _dot_tn(ds, qh)
            dn = _dot(ds, kh) * SCALE
            dt, xh = _norm_bwd(dn, qr, qg_ref[...])
            dq_ref[:, sl] = _bf(dt)
            gq = gq + jnp.sum(dn * xh, axis=0, keepdims=True)
        gq_ref[...] += gq

    vec = pl.BlockSpec((1, HEAD), lambda i: (0, 0))
    acc = pl.BlockSpec((M, X_W), lambda i: (0, 0))
    blk = 3 * _nbytes((tm, X_W), F32) + 3 * _nbytes((M, 2 * X_W), F32) + 6 * _nbytes((tm, M), F32)
    res, got = _pcall(
        body, name=name, grid=(S // tm,),
        in_specs=[pl.BlockSpec((tm, X_W), lambda i: (i, 0)), pl.BlockSpec((tm, X_W), lambda i: (i, 0)),
                  pl.BlockSpec((M, 2 * X_W), lambda i: (0, 0)), vec, vec],
        out_specs=[pl.BlockSpec((tm, X_W), lambda i: (i, 0)), acc, acc, vec],
        out_shape=[_sds((S, X_W), BF16), _sds((M, X_W), F32),
                   _sds((M, X_W), F32), _sds((1, HEAD), F32)],
        params=_cparams(1, blk), args=[do, qraw, kv, qg, kg], plan=plan)
    return res if plan is None else (res, got)


def _cross_kv_bwd(name, dkn, dv, kv, kg):
    M = kv.shape[0]

    def body(dk_ref, dv_ref, kv_ref, kg_ref, o_ref, g_ref):
        gk = jnp.zeros((1, HEAD), F32)
        for h in range(N_HEADS):
            sl = slice(h * HEAD, (h + 1) * HEAD)
            dn = dk_ref[:, sl]
            dt, xh = _norm_bwd(dn, kv_ref[:, sl], kg_ref[...])
            o_ref[:, sl] = _bf(dt)
            gk = gk + jnp.sum(dn * xh, axis=0, keepdims=True)
        o_ref[:, X_W:] = _bf(dv_ref[...])
        g_ref[...] = gk

    full = lambda shape: pl.BlockSpec(shape, lambda i: (0,) * len(shape))
    return pl.pallas_call(
        body, name=name, grid=(1,),
        in_specs=[full((M, X_W)), full((M, X_W)), full((M, 2 * X_W)), full((1, HEAD))],
        out_specs=[full((M, 2 * X_W)), full((1, HEAD))],
        out_shape=[_sds((M, 2 * X_W), BF16), _sds((1, HEAD), F32)],
        compiler_params=_cparams(1, 6 * _nbytes((M, 2 * X_W), F32)))(dkn, dv, kv, kg)


def _rope_tables(positions):
    inv_freq = ROPE_THETA ** (-jnp.arange(0, ROT_DIM, 2, dtype=F32) / ROT_DIM)
    ang = positions.astype(F32)[:, None] * inv_freq
    cos, sin = jnp.cos(ang), jnp.sin(ang)
    S = positions.shape[0]
    ctab = jnp.concatenate([cos, cos, jnp.ones((S, HEAD - ROT_DIM), F32)], axis=-1)
    stab = jnp.concatenate([-sin, sin, jnp.zeros((S, HEAD - ROT_DIM), F32)], axis=-1)
    return ctab, stab


GATHER_BEHIND_IN_PROJ = ("w_out", "w_cq", "w_ckv", "w_co")
FFN_WEIGHTS = ("w_gate_up", "w_down")


def _hosted(fn, *args, plan=None, **kw):
    if plan is None:
        return fn(*args, **kw), []
    return fn(*args, plan=plan, **kw)


def _local_step(x, mem, positions, target, wb, sm, *, tm=512, place=None):
    S, D = x.shape
    M = mem.shape[0]
    dist = place is not None
    wb = dict(wb)
    ctab, stab = _rope_tables(positions)
    pool_w_b = _bf(sm["pool_w"])

    def gather(names):
        return _GatherPlan([wb[k] for k in names]) if dist else None

    if dist:
        wb["w_in"], = _run_plan("gather_w_in", gather(["w_in"]))
    w_in = wb["w_in"]
    (proj, xn1), got = _hosted(_mm_nn, "in_proj", x, w_in, tm=tm, norm_g=sm["mix_norm_g"],
                               plan=gather(GATHER_BEHIND_IN_PROJ))
    wb.update(zip(GATHER_BEHIND_IN_PROJ, got))
    (qn, kn), got = _hosted(_qk_prep, "qk_prep", proj, ctab, stab, sm["q_norm_g"], sm["k_norm_g"], tm=tm,
                            plan=gather(["w_gate_up"]))
    wb.update(zip(["w_gate_up"], got))
    outs, lses = [], []
    for grp in range(len(DILATIONS)):
        last = grp == len(DILATIONS) - 1
        (o, l), got = _hosted(_attn_fwd, f"attn_fwd{grp}", qn, kn, proj, grp, plan=gather(["w_down"]) if last else None)
        wb.update(zip(["w_down"], got))
        outs.append(o)
        lses.append(l)
    w_out = wb["w_out"].reshape(1, 2 * KV_W, D)
    w_cq = wb["w_cq"].reshape(1, D, X_W)
    w_ckv = wb["w_ckv"].reshape(1, D, 2 * X_W)
    w_co = wb["w_co"]
    w_gu = wb["w_gate_up"]
    w_down = wb["w_down"].reshape(1, D_FF, D)
    cin = w_in.shape[2]
    cco = w_co.shape[2]
    cgu = w_gu.shape[2]
    mix, lse, dpool = _mix_post("mix_post", outs, lses, proj, pool_w_b, sm["pool_scale"], tm=tm)
    h1 = _mm_nn("out_proj", mix, w_out, tm=tm, residual=x)
    cq_raw, hn2 = _mm_nn("cq_proj", h1, w_cq, tm=tm, norm_g=sm["cross_norm_g"])
    kv, mem_n = _mm_nn("ckv_proj", mem, w_ckv, tm=M, norm_g=sm["mem_norm_g"])
    xo = _cross_fwd("cross_fwd", cq_raw, kv, sm["cq_norm_g"], sm["ck_norm_g"], tm=tm)
    h2 = _mm_nn("co_proj", xo, w_co, tm=tm, residual=h1)
    act, gu, hn3 = _ffn_up("ffn_up", h2, w_gu, sm["ffn_norm_g"], tm=tm)
    dy, lsum = _ffn_down_loss("ffn_down_loss", act, w_down, h2, target, tm=tm)
    loss = 0.5 * jnp.sum(lsum) / D

    nS = S // tm
    dgu = _ffn_down_bwd("ffn_down_bwd", dy, w_down, gu, tm=tm, tn=cgu)
    g_down = _mm_tn("g_w_down", act, pl.BlockSpec((tm, cgu), lambda r, c, s: (s, r)), (tm, cgu),
                    dy, pl.BlockSpec((tm, D), lambda r, c, s: (s, 0)), (tm, D),
                    (1, D_FF, D), pl.BlockSpec((None, cgu, D), lambda r, c, s: (0, r, 0)), (cgu, D),
                    (D_FF // cgu, 1, nS))
    dh2, g_ffn_norm = _ffn_up_bwd("ffn_up_bwd", dgu, w_gu, h2, sm["ffn_norm_g"], dy, tm=tm)
    g_gu = _mm_tn("g_w_gate_up", hn3, pl.BlockSpec((tm, D), lambda r, c, s: (s, 0)), (tm, D),
                  dgu, pl.BlockSpec((None, tm, cgu), lambda r, c, s: (c // 2, s, c % 2)), (tm, cgu),
                  (4, D, cgu), pl.BlockSpec((None, D, cgu), lambda r, c, s: (c, 0, 0)), (D, cgu),
                  (1, 4, nS))

    dxo = _mm_nt("co_proj_bwd", dh2, w_co, tm=tm, out_dtype=BF16)
    g_co = _mm_tn_wide("g_w_co", xo, dh2, N_CHIPS, ts=tm)
    full = {"w_gate_up": g_gu, "w_down": g_down.reshape(N_CHIPS, D_FF // N_CHIPS, D)}
    sums = {}

    def swap(names):
        return _SwapPlan([full[k] for k in names]) if dist else None

    def add_halves(names, from_sibling):
        for k, t in zip(names, from_sibling):
            sums[k] = _add_core_halves(f"add_halves_{k}", full[k], t, place[0])

    def exchange(names):
        return _ExchangePlan([sums[k][1] for k in names]) if dist else None

    def sum_chips(names, from_chips):
        return [_sum_chips(f"sum_chips_{k}", sums[k][0], t, place[1]) for k, t in zip(names, from_chips)]

    (dcq, dkn, dvm, g_cq_norm), got = _hosted(_cross_bwd, "cross_bwd", dxo, cq_raw, kv, sm["cq_norm_g"],
                                              sm["ck_norm_g"], tm=tm, plan=swap(FFN_WEIGHTS))
    add_halves(FFN_WEIGHTS, got)
    dkv, g_ck_norm = _cross_kv_bwd("cross_kv_bwd", dkn, dvm, kv, sm["ck_norm_g"])
    dh1, g_cross_norm = _mm_nt_normbwd("cq_proj_bwd", dcq, w_cq, h1, sm["cross_norm_g"], dh2, tm=tm)
    g_cq = _mm_tn("g_w_cq", hn2, pl.BlockSpec((tm, D), lambda r, c, s: (s, 0)), (tm, D),
                  dcq, pl.BlockSpec((tm, X_W), lambda r, c, s: (s, 0)), (tm, X_W),
                  (1, D, X_W), pl.BlockSpec((None, D, X_W), lambda r, c, s: (0, 0, 0)), (D, X_W), (1, 1, nS))
    _, g_mem_norm = _mm_nt_normbwd("ckv_proj_bwd", dkv, w_ckv, mem, sm["mem_norm_g"], None, tm=M)
    g_ckv = _mm_tn("g_w_ckv", mem_n, pl.BlockSpec((M, D), lambda r, c, s: (0, 0)), (M, D),
                   dkv, pl.BlockSpec((M, 2 * X_W), lambda r, c, s: (0, 0)), (M, 2 * X_W),
                   (1, D, 2 * X_W), pl.BlockSpec((None, D, 2 * X_W), lambda r, c, s: (0, 0, 0)), (D, 2 * X_W),
                   (1, 1, 1))

    dmix = _mm_nt("out_proj_bwd", dh1, w_out, tm=tm, out_dtype=F32)
    g_out = _mm_tn("g_w_out", mix, pl.BlockSpec((tm, 2 * KV_W), lambda r, c, s: (s, 0)), (tm, 2 * KV_W),
                   dh1, pl.BlockSpec((tm, D), lambda r, c, s: (s, 0)), (tm, D),
                   (1, 2 * KV_W, D), pl.BlockSpec((None, 2 * KV_W, D), lambda r, c, s: (0, 0, 0)), (2 * KV_W, D),
                   (1, 1, nS))
    full.update({
        "w_out": g_out.reshape(N_CHIPS, 2 * KV_W // N_CHIPS, D),
        "w_cq": g_cq.reshape(N_CHIPS, D // N_CHIPS, X_W),
        "w_ckv": g_ckv.reshape(N_CHIPS, D // N_CHIPS, 2 * X_W),
        "w_co": g_co,
    })
    mixer = GATHER_BEHIND_IN_PROJ
    (dattn, delta, du, g_pool_w, g_pool_scale), got = _hosted(
        _mix_bwd, "mix_bwd", dmix, mix, dpool, pool_w_b, sm["pool_scale"], tm=tm, plan=swap(mixer))
    add_halves(mixer, got)
    behind_attn = (None, mixer, FFN_WEIGHTS)
    halves = {}
    dqs, dks, dvs = [], [], []
    for grp in range(len(DILATIONS)):
        names = behind_attn[grp]
        (dq, dk, dv), got = _hosted(_attn_bwd, f"attn_bwd{grp}", qn, kn, proj, dattn, lse, delta, grp,
                                    plan=exchange(names) if names else None)
        if dist and names:
            halves.update(zip(names, sum_chips(names, got)))
        dqs.append(dq)
        dks.append(dk)
        dvs.append(dv)
    joined = mixer + FFN_WEIGHTS
    (dproj, g_q_norm, g_k_norm), got = _hosted(
        _qkv_bwd, "qkv_bwd", dqs, dks, dvs, du, proj, ctab, stab, sm["q_norm_g"], sm["k_norm_g"], tm=tm,
        plan=_JoinPlan([halves[k] for k in joined]) if dist else None)
    shards = dict(zip(joined, got))
    dx, g_mix_norm = _mm_nt_normbwd("in_proj_bwd", dproj, w_in, x, sm["mix_norm_g"], dh1, tm=tm)
    small = {
        "mix_norm_g": g_mix_norm, "q_norm_g": g_q_norm, "k_norm_g": g_k_norm, "pool_w": g_pool_w,
        "pool_scale": g_pool_scale, "cross_norm_g": g_cross_norm, "mem_norm_g": g_mem_norm,
        "cq_norm_g": g_cq_norm, "ck_norm_g": g_ck_norm, "ffn_norm_g": g_ffn_norm,
    }
    full["w_in"], got = _hosted(_mm_tn_wide, "g_w_in", xn1, dproj, N_CHIPS, ts=tm,
                                plan=_AllPushPlan(_pack_small(small)) if dist else None)
    if dist:
        small = _unpack_small(_sum_slots("sum_small", got[0]), sm)
        add_halves(["w_in"], _run_plan("swap_w_in", swap(["w_in"])))
        half, = sum_chips(["w_in"], _run_plan("exchange_w_in", exchange(["w_in"])))
        shards["w_in"], = _run_plan("join_w_in", _JoinPlan([half]))
    big = shards if dist else full
    return loss, dx, big, small


BIG = ("w_in", "w_out", "w_cq", "w_ckv", "w_co", "w_gate_up", "w_down")
SMALL = ("mix_norm_g", "q_norm_g", "k_norm_g", "pool_w", "pool_scale", "cross_norm_g", "mem_norm_g",
         "cq_norm_g", "ck_norm_g", "ffn_norm_g")
WEIGHTS = ("mix_norm_g", "w_in", "q_norm_g", "k_norm_g", "pool_w", "pool_scale", "w_out", "cross_norm_g",
           "mem_norm_g", "w_cq", "w_ckv", "cq_norm_g", "ck_norm_g", "w_co", "ffn_norm_g", "w_gate_up", "w_down")


def _cast_piece(name, w, k_arr):
    R, C = w.shape
    hr = R // 2

    def body(k_ref, w_ref, o_ref):
        o_ref[...] = _bf(w_ref[...])

    return pl.pallas_call(
        body, name=name,
        grid_spec=pltpu.PrefetchScalarGridSpec(
            num_scalar_prefetch=1, grid=(2,),
            in_specs=[pl.BlockSpec((hr, C), lambda i, k: (i, 0))],
            out_specs=pl.BlockSpec((None, hr, C), lambda i, k: (k[0], i, 0))),
        out_shape=_sds((N_CHIPS, R, C), BF16),
        compiler_params=_cparams(1, 2 * _nbytes((hr, C), F32)))(k_arr, *_pin([w]))


def _add_core_halves(name, g, t, c_arr):
    P, R, C = g.shape
    hr = R // 2

    def body(c_ref, g_ref, t_ref, o_ref, ob_ref):
        tot = g_ref[...] + t_ref[...]
        o_ref[...] = tot
        ob_ref[...] = _bf(tot)

    piece = pl.BlockSpec((None, hr, C), lambda p, c: (p, 0, 0))
    return pl.pallas_call(
        body, name=name,
        grid_spec=pltpu.PrefetchScalarGridSpec(
            num_scalar_prefetch=1, grid=(P,),
            in_specs=[pl.BlockSpec((None, hr, C), lambda p, c: (p, c[0], 0)), piece],
            out_specs=[piece, piece]),
        out_shape=[_sds((P, hr, C), F32), _sds((P, hr, C), BF16)],
        compiler_params=_cparams(1, 4 * _nbytes((hr, C), F32)))(c_arr, *_pin([g, t]))


def _sum_chips(name, own, got, kc_arr):
    P, hr, C = own.shape

    def body(kc_ref, o_ref, g_ref, r_ref):
        r_ref[...] = ((o_ref[...] + g_ref[0].astype(F32)) + g_ref[1].astype(F32)) + g_ref[2].astype(F32)

    return pl.pallas_call(
        body, name=name,
        grid_spec=pltpu.PrefetchScalarGridSpec(
            num_scalar_prefetch=1, grid=(1,),
            in_specs=[pl.BlockSpec((None, hr, C), lambda i, kc: (kc[0], 0, 0)),
                      pl.BlockSpec((N_CHIPS - 1, hr, C), lambda i, kc: (0, 0, 0))],
            out_specs=pl.BlockSpec((hr, C), lambda i, kc: (kc[1], 0))),
        out_shape=_sds((2 * hr, C), F32),
        compiler_params=_cparams(1, 5 * _nbytes((hr, C), F32)))(kc_arr, *_pin([own, got]))


N_DEV = 8


class _AllPushPlan(_Plan):
    def __init__(self, v):
        self.ins = [v]
        self.out_shapes = [jax.ShapeDtypeStruct((N_DEV,) + v.shape, v.dtype)]
        self.sem_shapes = [pltpu.SemaphoreType.DMA((N_DEV - 1,)), pltpu.SemaphoreType.DMA((N_DEV - 1,)),
                           pltpu.SemaphoreType.DMA]

    def copies(self, ins, outs, sems):
        send, recv, own = sems
        x, y, c, _ = _place()
        slot = outs[0].at[4 * x + 2 * y + c]
        cps = [pltpu.make_async_copy(ins[0], slot, own)]
        flips = [(dx, dy, dc) for dx in (0, 1) for dy in (0, 1) for dc in (0, 1)][1:]
        for q, (dx, dy, dc) in enumerate(flips):
            to = (x + dx - 2 * x * dx, y + dy - 2 * y * dy, c + dc - 2 * c * dc)
            cps.append(pltpu.make_async_remote_copy(src_ref=ins[0], dst_ref=slot, send_sem=send.at[q],
                                                    recv_sem=recv.at[q], device_id=to, device_id_type=MESH))
        return cps


def _sum_slots(name, slots):
    n, R, C = slots.shape

    def body(s_ref, o_ref):
        acc = s_ref[0]
        for d in range(1, n):
            acc = acc + s_ref[d]
        o_ref[...] = acc

    return pl.pallas_call(
        body, name=name, grid=(1,),
        in_specs=[pl.BlockSpec((n, R, C), lambda i: (0, 0, 0))], out_specs=pl.BlockSpec((R, C), lambda i: (0, 0)),
        out_shape=_sds((R, C), F32),
        compiler_params=_cparams(1, _nbytes((n + 1, R, C), F32)))(*_pin([slots]))


def _adamw(name, w, g, m, v, *, tr):
    R, C = w.shape

    def body(w_ref, g_ref, m_ref, v_ref, d_ref, nm_ref, nv_ref):
        gv = g_ref[...]
        nm = ADAM_B1 * m_ref[...] + (1.0 - ADAM_B1) * gv
        nv = ADAM_B2 * v_ref[...] + (1.0 - ADAM_B2) * (gv * gv)
        m_hat = nm / (1.0 - ADAM_B1 ** ADAM_STEP)
        v_hat = nv / (1.0 - ADAM_B2 ** ADAM_STEP)
        d_ref[...] = -ADAM_LR * (m_hat / (jnp.sqrt(v_hat) + ADAM_EPS) + ADAM_WD * w_ref[...])
        nm_ref[...] = nm
        nv_ref[...] = nv

    tile = pl.BlockSpec((tr, C), lambda i: (i, 0))
    return pl.pallas_call(
        body, name=name, grid=(R // tr,), in_specs=[tile] * 4, out_specs=[tile] * 3,
        out_shape=[_sds((R, C), F32)] * 3,
        compiler_params=_cparams(1, 7 * _nbytes((tr, C), F32)))(*_pin([w, g, m, v]))


def _pack_small(d):
    parts = []
    for name in SMALL:
        a = d[name].reshape(-1, HEAD)
        pad = (-a.shape[0]) % 8
        parts.append(jnp.pad(a, ((0, pad), (0, 0))))
    return jnp.concatenate(parts, axis=0)


def _unpack_small(packed, like):
    out = {}
    row = 0
    for name in SMALL:
        shape = like[name].shape
        rows = like[name].size // HEAD
        out[name] = packed[row:row + rows].reshape(shape)
        row += rows + (-rows) % 8
    return out


def kernel(x, mem, positions, mix_norm_g, w_in, q_norm_g, k_norm_g, pool_w, pool_scale, w_out, cross_norm_g, mem_norm_g, w_cq, w_ckv, cq_norm_g, ck_norm_g, w_co, ffn_norm_g, w_gate_up, w_down, loss_target, m_mix_norm_g, m_w_in, m_q_norm_g, m_k_norm_g, m_pool_w, m_pool_scale, m_w_out, m_cross_norm_g, m_mem_norm_g, m_w_cq, m_w_ckv, m_cq_norm_g, m_ck_norm_g, m_w_co, m_ffn_norm_g, m_w_gate_up, m_w_down, v_mix_norm_g, v_w_in, v_q_norm_g, v_k_norm_g, v_pool_w, v_pool_scale, v_w_out, v_cross_norm_g, v_mem_norm_g, v_w_cq, v_w_ckv, v_cq_norm_g, v_ck_norm_g, v_w_co, v_ffn_norm_g, v_w_gate_up, v_w_down):
    w = dict(mix_norm_g=mix_norm_g, w_in=w_in, q_norm_g=q_norm_g, k_norm_g=k_norm_g, pool_w=pool_w,
             pool_scale=pool_scale, w_out=w_out, cross_norm_g=cross_norm_g, mem_norm_g=mem_norm_g, w_cq=w_cq,
             w_ckv=w_ckv, cq_norm_g=cq_norm_g, ck_norm_g=ck_norm_g, w_co=w_co, ffn_norm_g=ffn_norm_g,
             w_gate_up=w_gate_up, w_down=w_down)
    m = dict(mix_norm_g=m_mix_norm_g, w_in=m_w_in, q_norm_g=m_q_norm_g, k_norm_g=m_k_norm_g, pool_w=m_pool_w,
             pool_scale=m_pool_scale, w_out=m_w_out, cross_norm_g=m_cross_norm_g, mem_norm_g=m_mem_norm_g,
             w_cq=m_w_cq, w_ckv=m_w_ckv, cq_norm_g=m_cq_norm_g, ck_norm_g=m_ck_norm_g, w_co=m_w_co,
             ffn_norm_g=m_ffn_norm_g, w_gate_up=m_w_gate_up, w_down=m_w_down)
    v = dict(mix_norm_g=v_mix_norm_g, w_in=v_w_in, q_norm_g=v_q_norm_g, k_norm_g=v_k_norm_g, pool_w=v_pool_w,
             pool_scale=v_pool_scale, w_out=v_w_out, cross_norm_g=v_cross_norm_g, mem_norm_g=v_mem_norm_g,
             w_cq=v_w_cq, w_ckv=v_w_ckv, cq_norm_g=v_cq_norm_g, ck_norm_g=v_ck_norm_g, w_co=v_w_co,
             ffn_norm_g=v_ffn_norm_g, w_gate_up=v_w_gate_up, w_down=v_w_down)

    c_arr = lax.axis_index("c").astype(jnp.int32).reshape(1)
    k_arr = (2 * lax.axis_index("x") + lax.axis_index("y")).astype(jnp.int32).reshape(1)
    kc_arr = jnp.concatenate([k_arr, c_arr])
    wb = {k: _cast_piece(f"cast_{k}", w[k][0], k_arr) for k in BIG}
    sm = {k: (w[k][0] if k == "pool_w" else w[k]) for k in SMALL}
    loss_part, dx, gshard, gsm = _local_step(x[0], mem[0], positions[0], loss_target[0], wb, sm,
                                             place=(c_arr, kc_arr))
    loss = lax.psum(loss_part, ("x", "y", "c"))

    grads, deltas, new_m, new_v = {}, {}, {}, {}
    for k in BIG:
        shard = w[k][0]
        tr = shard.shape[0] // 4
        d, nm, nv = _adamw(f"adamw_{k}", shard, gshard[k], m[k][0], v[k][0], tr=tr)
        grads[k], deltas[k], new_m[k], new_v[k] = gshard[k][None], d[None], nm[None], nv[None]
    smw = {k: (w[k][0] if k == "pool_w" else w[k]) for k in SMALL}
    smm = {k: (m[k][0] if k == "pool_w" else m[k]) for k in SMALL}
    smv = {k: (v[k][0] if k == "pool_w" else v[k]) for k in SMALL}
    pw, pg, pm, pv = _pack_small(smw), _pack_small(gsm), _pack_small(smm), _pack_small(smv)
    d, nm, nv = _adamw("adamw_small", pw, pg, pm, pv, tr=pw.shape[0])
    for dst, packed in ((deltas, d), (new_m, nm), (new_v, nv)):
        un = _unpack_small(packed, sm)
        for k in SMALL:
            dst[k] = un[k].reshape(w[k].shape)
    for k in SMALL:
        grads[k] = gsm[k].reshape(w[k].shape)

    return (loss, dx[None], *[grads[k] for k in WEIGHTS], *[deltas[k] for k in WEIGHTS],
            *[new_m[k] for k in WEIGHTS], *[new_v[k] for k in WEIGHTS])
```

```python
import functools

import jax
import jax.numpy as jnp
from jax import lax
from jax.experimental import pallas as pl
from jax.experimental.pallas import tpu as pltpu

F32 = jnp.float32
BF16 = jnp.bfloat16
MESH = pl.DeviceIdType.MESH
ANY = pl.BlockSpec(memory_space=pl.ANY)

D_MODEL = 1024
HEAD = 128
N_HEADS = 4
DILATIONS = (1, 4, 16)
BLOCK = 128
Q_W = 1536
KV_W = 512
POOL_W = 512
POOL_WINDOWS = (2, 4, 8, 16)
POOL_HALO = 16
IN_W = 3072
ROT_DIM = 32
ROT_HALF = 16
ROPE_THETA = 500000.0
X_W = 512
D_FF = 2816
EPS = 1e-6
NEG_INF = -1e30
SCALE = HEAD ** -0.5
N_CHIPS = 4

ADAM_LR = 0.001
ADAM_B1 = 0.9
ADAM_B2 = 0.999
ADAM_EPS = 1e-08
ADAM_WD = 0.01
ADAM_STEP = 10

VMEM_BYTES_V7X = 64 * 2 ** 20
VMEM_LIMIT_MAX = 56 * 2 ** 20
VMEM_LIMIT_MIN = 24 * 2 ** 20


def _nbytes(shape, dtype):
    n = 1
    for s in shape:
        n *= s
    return n * jnp.dtype(dtype).itemsize


def _cparams(n_axes, block_bytes, scratch_bytes=0):
    est = 2 * (2 * block_bytes + scratch_bytes)
    lim = int(min(VMEM_LIMIT_MAX, max(VMEM_LIMIT_MIN, est)))
    return pltpu.CompilerParams(dimension_semantics=("arbitrary",) * n_axes, vmem_limit_bytes=lim)


def _bf(v):
    return v.astype(BF16)


def _dot(a, b):
    return jnp.dot(a, b, preferred_element_type=F32)


def _dot_nt(a, b):
    return lax.dot_general(a, b, (((1,), (1,)), ((), ())), preferred_element_type=F32)


def _dot_tn(a, b):
    return lax.dot_general(a, b, (((0,), (0,)), ((), ())), preferred_element_type=F32)


def _rstd(v):
    return lax.rsqrt(jnp.mean(v * v, axis=-1, keepdims=True) + EPS)


def _norm_bwd(dy, xv, g):
    r = _rstd(xv)
    xh = xv * r
    dxh = dy * g
    dx = r * (dxh - xh * jnp.mean(dxh * xh, axis=-1, keepdims=True))
    return dx, xh


def _place():
    x, y, c = lax.axis_index("x"), lax.axis_index("y"), lax.axis_index("c")
    other_chips = [(1 - x, y), (x, 1 - y), (1 - x, 1 - y)]
    return x, y, c, other_chips


class _Plan:
    ins = ()
    out_shapes = ()
    aliases = {}
    sem_shapes = ()

    def copies(self, ins, outs, sems):
        raise NotImplementedError

    def begin(self, ins, outs, sems):
        for cp in self.copies(ins, outs, sems):
            cp.start()

    def finish(self, ins, outs, sems):
        for cp in self.copies(ins, outs, sems):
            cp.wait()


class _GatherPlan(_Plan):
    def __init__(self, bufs):
        n = len(bufs)
        self.ins = list(bufs)
        self.out_shapes = [_sds(b.shape, b.dtype) for b in bufs]
        self.aliases = {i: i for i in range(n)}
        self.sem_shapes = [pltpu.SemaphoreType.DMA((n, 6)), pltpu.SemaphoreType.DMA((n, 6))]

    def _parts(self, outs, sems):
        send, recv = sems
        x, y, c, chips = _place()

        def half(i, piece, which):
            hr = outs[i].shape[1] // 2
            return outs[i].at[piece, pl.ds(which * hr, hr), :]

        def copy(i, k, ref, to):
            return pltpu.make_async_remote_copy(src_ref=ref, dst_ref=ref, send_sem=send.at[i, k], recv_sem=recv.at[i, k],
                                                device_id=to, device_id_type=MESH)

        return x, y, c, chips, half, copy

    def begin(self, ins, outs, sems):
        x, y, c, chips, half, copy = self._parts(outs, sems)
        for i in range(len(outs)):
            for j, (cx, cy) in enumerate(chips):
                copy(i, j, half(i, 2 * x + y, c), (cx, cy, c)).start()

    def finish(self, ins, outs, sems):
        x, y, c, chips, half, copy = self._parts(outs, sems)
        sib = (x, y, 1 - c)
        n = len(outs)
        for i in range(n):
            for j, (cx, cy) in enumerate(chips):
                piece = half(i, 2 * cx + cy, c)
                copy(i, j, piece, (cx, cy, c)).wait_recv()
                copy(i, 3 + j, piece, sib).start()
        for i in range(n):
            for j, (cx, cy) in enumerate(chips):
                copy(i, 3 + j, half(i, 2 * cx + cy, 1 - c), sib).wait_recv()
        for i in range(n):
            for j, (cx, cy) in enumerate(chips):
                copy(i, j, half(i, 2 * x + y, c), (cx, cy, c)).wait_send()
                copy(i, 3 + j, half(i, 2 * cx + cy, c), sib).wait_send()


class _SwapPlan(_Plan):
    def __init__(self, grads):
        n = len(grads)
        self.ins = list(grads)
        self.out_shapes = [_sds((g.shape[0], g.shape[1] // 2, g.shape[2]), g.dtype) for g in grads]
        self.sem_shapes = [pltpu.SemaphoreType.DMA((n,)), pltpu.SemaphoreType.DMA((n,))]

    def copies(self, ins, outs, sems):
        send, recv = sems
        x, y, c, _ = _place()
        cps = []
        for i in range(len(ins)):
            hr = ins[i].shape[1] // 2
            cps.append(pltpu.make_async_remote_copy(
                src_ref=ins[i].at[:, pl.ds((1 - c) * hr, hr), :], dst_ref=outs[i], send_sem=send.at[i],
                recv_sem=recv.at[i], device_id=(x, y, 1 - c), device_id_type=MESH))
        return cps


class _ExchangePlan(_Plan):
    def __init__(self, sums):
        n = len(sums)
        self.ins = list(sums)
        self.out_shapes = [_sds((N_CHIPS - 1,) + s.shape[1:], s.dtype) for s in sums]
        self.sem_shapes = [pltpu.SemaphoreType.DMA((n, 3)), pltpu.SemaphoreType.DMA((n, 3))]

    def copies(self, ins, outs, sems):
        send, recv = sems
        x, y, c, chips = _place()
        cps = []
        for i in range(len(ins)):
            for j, (cx, cy) in enumerate(chips):
                cps.append(pltpu.make_async_remote_copy(
                    src_ref=ins[i].at[2 * cx + cy], dst_ref=outs[i].at[j], send_sem=send.at[i, j],
                    recv_sem=recv.at[i, j], device_id=(cx, cy, c), device_id_type=MESH))
        return cps


class _JoinPlan(_Plan):
    def __init__(self, shards):
        n = len(shards)
        self.ins = list(shards)
        self.out_shapes = [_sds(s.shape, s.dtype) for s in shards]
        self.aliases = {i: i for i in range(n)}
        self.sem_shapes = [pltpu.SemaphoreType.DMA((n,)), pltpu.SemaphoreType.DMA((n,))]

    def copies(self, ins, outs, sems):
        send, recv = sems
        x, y, c, _ = _place()
        cps = []
        for i in range(len(outs)):
            hr = outs[i].shape[0] // 2
            mine = outs[i].at[pl.ds(c * hr, hr), :]
            cps.append(pltpu.make_async_remote_copy(src_ref=mine, dst_ref=mine, send_sem=send.at[i], recv_sem=recv.at[i],
                                                    device_id=(x, y, 1 - c), device_id_type=MESH))
        return cps


def _run_plan(name, plan):
    n_in, n_out = len(plan.ins), len(plan.out_shapes)

    def body(*refs):
        ins, outs, sems = refs[:n_in], refs[n_in:n_in + n_out], refs[n_in + n_out:]
        plan.begin(ins, outs, sems)
        plan.finish(ins, outs, sems)

    return pl.pallas_call(
        body, name=name, in_specs=[ANY] * n_in, out_specs=[ANY] * n_out, out_shape=list(plan.out_shapes),
        input_output_aliases=dict(plan.aliases), scratch_shapes=list(plan.sem_shapes))(*plan.ins)


HBM_PIN_BYTES = 1 << 20


def _sds(shape, dtype):
    if _nbytes(shape, dtype) >= HBM_PIN_BYTES:
        return pltpu.HBM(shape, dtype)
    return jax.ShapeDtypeStruct(shape, dtype)


def _pin(args):
    return [pltpu.with_memory_space_constraint(a, pltpu.HBM) if _nbytes(a.shape, a.dtype) >= HBM_PIN_BYTES else a
            for a in args]


def _pcall(body, *, name, grid, in_specs, out_specs, out_shape, params, args, scratch_shapes=(), plan=None):
    in_specs, out_specs, out_shape, scratch = list(in_specs), list(out_specs), list(out_shape), list(scratch_shapes)
    args = _pin(args)
    if plan is None:
        res = pl.pallas_call(body, name=name, grid=grid, in_specs=in_specs, out_specs=out_specs, out_shape=out_shape,
                             scratch_shapes=scratch, compiler_params=params)(*args)
        return list(res), []
    ni, no, ns = len(in_specs), len(out_specs), len(scratch)
    pi, po = len(plan.ins), len(plan.out_shapes)

    def wrapped(*refs):
        ins, pins = refs[:ni], refs[ni:ni + pi]
        outs, pouts = refs[ni + pi:ni + pi + no], refs[ni + pi + no:ni + pi + no + po]
        scr, psems = refs[ni + pi + no + po:ni + pi + no + po + ns], refs[ni + pi + no + po + ns:]
        ids = [pl.program_id(a) for a in range(len(grid))]
        first = functools.reduce(jnp.logical_and, [i == 0 for i in ids])
        last = functools.reduce(jnp.logical_and, [i == g - 1 for i, g in zip(ids, grid)])

        @pl.when(first)
        def _():
            plan.begin(pins, pouts, psems)

        body(*ins, *outs, *scr)

        @pl.when(last)
        def _():
            plan.finish(pins, pouts, psems)

    res = pl.pallas_call(
        wrapped, name=name, grid=grid, in_specs=in_specs + [ANY] * pi, out_specs=out_specs + [ANY] * po,
        out_shape=out_shape + list(plan.out_shapes), scratch_shapes=scratch + list(plan.sem_shapes),
        input_output_aliases={ni + a: no + b for a, b in plan.aliases.items()},
        compiler_params=params)(*args, *plan.ins)
    return list(res[:no]), list(res[no:])


def _mm_nn(name, a, w3, *, tm, norm_g=None, residual=None, out_dtype=F32, plan=None):
    M, K = a.shape
    P, Kw, C = w3.shape
    assert Kw == K and M % tm == 0
    N = P * C
    has_norm = norm_g is not None
    has_res = residual is not None

    def body(*refs):
        refs = list(refs)
        a_ref = refs.pop(0)
        w_ref = refs.pop(0)
        g_ref = refs.pop(0) if has_norm else None
        r_ref = refs.pop(0) if has_res else None
        o_ref = refs.pop(0)
        xn_ref = refs.pop(0) if has_norm else None
        if has_norm:
            av = a_ref[...].astype(F32)
            ab = _bf(av * _rstd(av) * g_ref[...])
            xn_ref[...] = ab
        else:
            ab = _bf(a_ref[...])
        for p in range(P):
            acc = _dot(ab, w_ref[p])
            if has_res:
                acc = acc + r_ref[:, p * C:(p + 1) * C]
            o_ref[:, p * C:(p + 1) * C] = acc.astype(o_ref.dtype)

    in_specs = [pl.BlockSpec((tm, K), lambda i: (i, 0)), pl.BlockSpec((P, K, C), lambda i: (0, 0, 0))]
    args = [a, w3]
    if has_norm:
        in_specs.append(pl.BlockSpec((1, K), lambda i: (0, 0)))
        args.append(norm_g)
    if has_res:
        in_specs.append(pl.BlockSpec((tm, N), lambda i: (i, 0)))
        args.append(residual)
    out_shape = [_sds((M, N), out_dtype)]
    out_specs = [pl.BlockSpec((tm, N), lambda i: (i, 0))]
    if has_norm:
        out_shape.append(_sds((M, K), BF16))
        out_specs.append(pl.BlockSpec((tm, K), lambda i: (i, 0)))
    blk = (_nbytes((tm, K), a.dtype) + _nbytes((P, K, C), BF16) + 2 * _nbytes((tm, N), F32)
           + _nbytes((tm, K), BF16))
    res, got = _pcall(body, name=name, grid=(M // tm,), in_specs=in_specs, out_specs=out_specs, out_shape=out_shape,
                      params=_cparams(1, blk), args=args, plan=plan)
    res = res if has_norm else res[0]
    return res if plan is None else (res, got)


MXU_COLS_V7X = 256


def _col_chunks(n):
    return [(c, min(c + MXU_COLS_V7X, n)) for c in range(0, n, MXU_COLS_V7X)]


def _ffn_up(name, h, wgu3, g, *, tm):
    M, K = h.shape
    P, _, C = wgu3.shape
    half = P // 2

    def body(h_ref, wg_ref, wu_ref, g_ref, act_ref, fac_ref, xn_ref, ab_ref):
        @pl.when(pl.program_id(1) == 0)
        def _():
            hv = h_ref[...]
            xn = _bf(hv * _rstd(hv) * g_ref[...])
            ab_ref[...] = xn
            xn_ref[...] = xn
        ab = ab_ref[...]
        for c0, c1 in _col_chunks(C):
            gate = _dot(ab, wg_ref[:, c0:c1])
            up = _dot(ab, wu_ref[:, c0:c1])
            sig = 1.0 / (1.0 + jnp.exp(-gate))
            silu = gate * sig
            act_ref[:, c0:c1] = _bf(silu * up)
            fac_ref[0, :, c0:c1] = _bf(up * (sig + silu * (1.0 - sig)))
            fac_ref[1, :, c0:c1] = _bf(silu)

    blk = (_nbytes((tm, K), F32) + 2 * _nbytes((K, C), BF16) + 3 * _nbytes((tm, C), BF16)
           + 2 * _nbytes((tm, C), F32) + _nbytes((tm, K), BF16))
    return pl.pallas_call(
        body, name=name, grid=(M // tm, half),
        in_specs=[pl.BlockSpec((tm, K), lambda i, j: (i, 0)),
                  pl.BlockSpec((None, K, C), lambda i, j: (j, 0, 0)),
                  pl.BlockSpec((None, K, C), lambda i, j: (j + half, 0, 0)),
                  pl.BlockSpec((1, K), lambda i, j: (0, 0))],
        out_specs=[pl.BlockSpec((tm, C), lambda i, j: (i, j)),
                   pl.BlockSpec((2, tm, C), lambda i, j: (0, i, j)),
                   pl.BlockSpec((tm, K), lambda i, j: (i, 0))],
        out_shape=[_sds((M, half * C), BF16),
                   _sds((2, M, half * C), BF16),
                   _sds((M, K), BF16)],
        scratch_shapes=[pltpu.VMEM((tm, K), BF16)],
        compiler_params=_cparams(2, blk, _nbytes((tm, K), BF16)))(*_pin([h, wgu3, wgu3, g]))


def _ffn_down_loss(name, act, wd3, h, target, *, tm):
    M, K = act.shape
    _, _, N = wd3.shape

    def body(a_ref, w_ref, h_ref, t_ref, dy_ref, ls_ref):
        err = _dot(a_ref[...], w_ref[...]) + h_ref[...] - t_ref[...]
        dy_ref[...] = err * (1.0 / N)

        @pl.when(pl.program_id(0) == 0)
        def _():
            ls_ref[...] = jnp.zeros_like(ls_ref)
        ls_ref[...] += jnp.sum(err * err, axis=0, keepdims=True)

    blk = _nbytes((tm, K), BF16) + _nbytes((K, N), BF16) + 4 * _nbytes((tm, N), F32)
    return pl.pallas_call(
        body, name=name, grid=(M // tm,),
        in_specs=[pl.BlockSpec((tm, K), lambda i: (i, 0)),
                  pl.BlockSpec((None, K, N), lambda i: (0, 0, 0)),
                  pl.BlockSpec((tm, N), lambda i: (i, 0)),
                  pl.BlockSpec((tm, N), lambda i: (i, 0))],
        out_specs=[pl.BlockSpec((tm, N), lambda i: (i, 0)),
                   pl.BlockSpec((1, N), lambda i: (0, 0))],
        out_shape=[_sds((M, N), F32), _sds((1, N), F32)],
        compiler_params=_cparams(1, blk))(*_pin([act, wd3, h, target]))


def _nt_pieces(a_ref, w_ref):
    P, _, C = w_ref.shape
    acc = _dot_nt(_bf(a_ref[:, 0:C]), w_ref[0])
    for p in range(1, P):
        acc = acc + _dot_nt(_bf(a_ref[:, p * C:(p + 1) * C]), w_ref[p])
    return acc


def _mm_nt(name, a, w3, *, tm, out_dtype):
    M, N = a.shape
    P, Ko, C = w3.shape
    assert N == P * C and M % tm == 0

    def body(a_ref, w_ref, o_ref):
        o_ref[...] = _nt_pieces(a_ref, w_ref).astype(o_ref.dtype)

    blk = _nbytes((tm, N), a.dtype) + _nbytes((P, Ko, C), BF16) + 2 * _nbytes((tm, Ko), F32)
    return pl.pallas_call(
        body, name=name, grid=(M // tm,),
        in_specs=[pl.BlockSpec((tm, N), lambda i: (i, 0)), pl.BlockSpec((P, Ko, C), lambda i: (0, 0, 0))],
        out_specs=pl.BlockSpec((tm, Ko), lambda i: (i, 0)),
        out_shape=_sds((M, Ko), out_dtype),
        compiler_params=_cparams(1, blk))(*_pin([a, w3]))


def _mm_nt_normbwd(name, a, w3, h, g, dres, *, tm, plan=None):
    M, D = h.shape
    P, Ko, C = w3.shape
    N = a.shape[1]
    assert N == P * C and Ko == D and M % tm == 0
    has_res = dres is not None

    def body(*refs):
        a_ref, w_ref, h_ref, g_ref = refs[:4]
        r_ref = refs[4] if has_res else None
        dx_ref, dg_ref = refs[-2:]
        dhn = _nt_pieces(a_ref, w_ref)
        dx, xh = _norm_bwd(dhn, h_ref[...], g_ref[...])
        if has_res:
            dx = dx + r_ref[...]
        dx_ref[...] = dx

        @pl.when(pl.program_id(0) == 0)
        def _():
            dg_ref[...] = jnp.zeros_like(dg_ref)
        dg_ref[...] += jnp.sum(dhn * xh, axis=0, keepdims=True)

    row = pl.BlockSpec((tm, D), lambda i: (i, 0))
    vec = pl.BlockSpec((1, D), lambda i: (0, 0))
    in_specs = [pl.BlockSpec((tm, N), lambda i: (i, 0)), pl.BlockSpec((P, Ko, C), lambda i: (0, 0, 0)), row, vec]
    args = [a, w3, h, g]
    if has_res:
        in_specs.append(row)
        args.append(dres)
    blk = _nbytes((tm, N), a.dtype) + _nbytes((P, Ko, C), BF16) + 5 * _nbytes((tm, D), F32)
    res, got = _pcall(body, name=name, grid=(M // tm,), in_specs=in_specs, out_specs=[row, vec],
                      out_shape=[_sds((M, D), F32), _sds((1, D), F32)],
                      params=_cparams(1, blk), args=args, plan=plan)
    return res if plan is None else (res, got)


def _ffn_up_bwd(name, dgu, wgu3, h, g, dres, *, tm):
    M, D = h.shape
    P, _, C = wgu3.shape
    F = dgu.shape[2]
    per = F // C

    def body(a_ref, w_ref, h_ref, g_ref, r_ref, dx_ref, dg_ref):
        dhn = None
        for p in range(P):
            part = _dot_nt(a_ref[p // per, :, (p % per) * C:(p % per + 1) * C], w_ref[p])
            dhn = part if dhn is None else dhn + part
        dx, xh = _norm_bwd(dhn, h_ref[...], g_ref[...])
        dx_ref[...] = dx + r_ref[...]

        @pl.when(pl.program_id(0) == 0)
        def _():
            dg_ref[...] = jnp.zeros_like(dg_ref)
        dg_ref[...] += jnp.sum(dhn * xh, axis=0, keepdims=True)

    row = pl.BlockSpec((tm, D), lambda i: (i, 0))
    vec = pl.BlockSpec((1, D), lambda i: (0, 0))
    blk = _nbytes((2, tm, F), BF16) + _nbytes((P, D, C), BF16) + 5 * _nbytes((tm, D), F32)
    return pl.pallas_call(
        body, name=name, grid=(M // tm,),
        in_specs=[pl.BlockSpec((2, tm, F), lambda i: (0, i, 0)),
                  pl.BlockSpec((P, D, C), lambda i: (0, 0, 0), pipeline_mode=pl.Buffered(1)), row, vec, row],
        out_specs=[row, vec],
        out_shape=[_sds((M, D), F32), _sds((1, D), F32)],
        compiler_params=_cparams(1, blk))(*_pin([dgu, wgu3, h, g, dres]))


def _ffn_down_bwd(name, dy, wd3, gu, *, tm, tn):
    M, D = dy.shape
    _, F, _ = wd3.shape

    def body(dy_ref, w_ref, gu_ref, o_ref, ab_ref):
        @pl.when(pl.program_id(1) == 0)
        def _():
            ab_ref[...] = _bf(dy_ref[...])
        ab = ab_ref[...]
        for c0, c1 in _col_chunks(tn):
            da = _dot_nt(ab, w_ref[c0:c1, :])
            o_ref[0, :, c0:c1] = _bf(da * gu_ref[0, :, c0:c1].astype(F32))
            o_ref[1, :, c0:c1] = _bf(da * gu_ref[1, :, c0:c1].astype(F32))

    blk = (_nbytes((tm, D), F32) + _nbytes((tn, D), BF16) + 4 * _nbytes((tm, tn), BF16)
           + 4 * _nbytes((tm, tn), F32))
    return pl.pallas_call(
        body, name=name, grid=(M // tm, F // tn),
        in_specs=[pl.BlockSpec((tm, D), lambda i, j: (i, 0)),
                  pl.BlockSpec((None, tn, D), lambda i, j: (0, j, 0)),
                  pl.BlockSpec((2, tm, tn), lambda i, j: (0, i, j))],
        out_specs=pl.BlockSpec((2, tm, tn), lambda i, j: (0, i, j)),
        out_shape=_sds((2, M, F), BF16),
        scratch_shapes=[pltpu.VMEM((tm, D), BF16)],
        compiler_params=_cparams(2, blk, _nbytes((tm, D), BF16)))(*_pin([dy, wd3, gu]))


def _mm_tn_wide(name, a, b, pieces, *, ts, plan=None):
    S, K = a.shape
    N = b.shape[1]
    C = N // pieces

    def body(a_ref, b_ref, o_ref):
        @pl.when(pl.program_id(0) == 0)
        def _():
            o_ref[...] = jnp.zeros_like(o_ref)
        acc = _dot_tn(_bf(a_ref[...]), _bf(b_ref[...]))
        for p in range(pieces):
            o_ref[p] += acc[:, p * C:(p + 1) * C]

    blk = _nbytes((ts, K), a.dtype) + _nbytes((ts, N), b.dtype) + 2 * _nbytes((K, N), F32)
    res, got = _pcall(body, name=name, grid=(S // ts,),
                      in_specs=[pl.BlockSpec((ts, K), lambda s: (s, 0)), pl.BlockSpec((ts, N), lambda s: (s, 0))],
                      out_specs=[pl.BlockSpec((pieces, K, C), lambda s: (0, 0, 0))],
                      out_shape=[_sds((pieces, K, C), F32)], params=_cparams(1, blk), args=[a, b], plan=plan)
    return res[0] if plan is None else (res[0], got)


def _mm_tn(name, a, a_spec, a_blk, b, b_spec, b_blk, out3, o_spec, o_blk, grid, plan=None):
    def body(a_ref, b_ref, o_ref):
        @pl.when(pl.program_id(2) == 0)
        def _():
            o_ref[...] = jnp.zeros_like(o_ref)
        o_ref[...] += _dot_tn(_bf(a_ref[...]), _bf(b_ref[...]))

    blk = _nbytes(a_blk, a.dtype) + _nbytes(b_blk, b.dtype) + 2 * _nbytes(o_blk, F32)
    res, got = _pcall(body, name=name, grid=grid, in_specs=[a_spec, b_spec], out_specs=[o_spec],
                      out_shape=[_sds(out3, F32)], params=_cparams(3, blk), args=[a, b], plan=plan)
    return res[0] if plan is None else (res[0], got)


def _swap_halves(v, lane, masked):
    up = pltpu.roll(v, HEAD - ROT_HALF, 1)
    down = pltpu.roll(v, ROT_HALF, 1)
    rest = jnp.where(lane < ROT_DIM, down, 0.0) if masked else down
    return jnp.where(lane < ROT_HALF, up, rest)


def _qk_prep(name, proj, ctab, stab, qg, kg, *, tm, plan=None):
    S = proj.shape[0]
    width = Q_W + KV_W

    def body(p_ref, c_ref, s_ref, qg_ref, kg_ref, q_ref, k_ref):
        lane = lax.broadcasted_iota(jnp.int32, (tm, HEAD), 1)
        cv = c_ref[...]
        sv = s_ref[...]

        def prep(t, g, scale):
            n = t * _rstd(t) * g
            return (n * cv + _swap_halves(n, lane, False) * sv) * scale

        for j in range(Q_W // HEAD):
            q_ref[:, j * HEAD:(j + 1) * HEAD] = prep(p_ref[:, j * HEAD:(j + 1) * HEAD], qg_ref[...], SCALE)
        for j in range(KV_W // HEAD):
            k_ref[:, j * HEAD:(j + 1) * HEAD] = prep(p_ref[:, Q_W + j * HEAD:Q_W + (j + 1) * HEAD], kg_ref[...], 1.0)

    blk = 2 * _nbytes((tm, width), F32) + 2 * _nbytes((tm, HEAD), F32)
    res, got = _pcall(
        body, name=name, grid=(S // tm,),
        in_specs=[pl.BlockSpec((tm, width), lambda i: (i, 0)),
                  pl.BlockSpec((tm, HEAD), lambda i: (i, 0)),
                  pl.BlockSpec((tm, HEAD), lambda i: (i, 0)),
                  pl.BlockSpec((1, HEAD), lambda i: (0, 0)),
                  pl.BlockSpec((1, HEAD), lambda i: (0, 0))],
        out_specs=[pl.BlockSpec((tm, Q_W), lambda i: (i, 0)), pl.BlockSpec((tm, KV_W), lambda i: (i, 0))],
        out_shape=[_sds((S, Q_W), F32), _sds((S, KV_W), F32)],
        params=_cparams(1, blk), args=[proj, ctab, stab, qg, kg], plan=plan)
    return res if plan is None else (res, got)


ATTN_STEP = 2048


def _row_idx(start, dil):
    return pl.ds(start, BLOCK) if dil == 1 else pl.ds(start, BLOCK, stride=dil)


def _rows(ref, start, dil):
    return ref[_row_idx(start, dil), :]


def _set_rows(ref, start, dil, val):
    ref[_row_idx(start, dil), :] = val


def _band_mask(first):
    qi = lax.broadcasted_iota(jnp.int32, (BLOCK, 2 * BLOCK), 0)
    kj = lax.broadcasted_iota(jnp.int32, (BLOCK, 2 * BLOCK), 1)
    band = (kj >= qi) & (kj <= qi + BLOCK)
    if first is None:
        return band
    return band & ((kj >= BLOCK) | jnp.logical_not(first))


def _attn_geometry(S, grp):
    dil = DILATIONS[grp]
    bt = BLOCK * dil
    assert S % ATTN_STEP == 0 and ATTN_STEP % bt == 0
    return dil, bt, ATTN_STEP // bt, S // ATTN_STEP


def _attn_fwd(name, qn, kn, proj, grp, plan=None):
    S = qn.shape[0]
    dil, bt, nsub, nsb = _attn_geometry(S, grp)
    vcol = (Q_W + KV_W) // HEAD

    def body(q_ref, kp_ref, kc_ref, vp_ref, vc_ref, o_ref, l_ref):
        valid_first = _band_mask(pl.program_id(1) == 0)
        valid_inner = _band_mask(None)
        for r in range(dil):
            kprev, vprev = _bf(_rows(kp_ref, r, dil)), _bf(_rows(vp_ref, r, dil))
            for b in range(nsub):
                at = b * bt + r
                kcur, vcur = _bf(_rows(kc_ref, at, dil)), _bf(_rows(vc_ref, at, dil))
                q = _bf(_rows(q_ref, at, dil))
                k2 = jnp.concatenate([kprev, kcur], axis=0)
                v2 = jnp.concatenate([vprev, vcur], axis=0)
                s = jnp.where(valid_first if b == 0 else valid_inner, _dot_nt(q, k2), NEG_INF)
                m = jnp.max(s, axis=-1, keepdims=True)
                p = jnp.exp(s - m)
                l = jnp.sum(p, axis=-1, keepdims=True)
                acc = _dot(_bf(p), v2)
                _set_rows(o_ref, at, dil, acc / l)
                _set_rows(l_ref, at, dil, jnp.broadcast_to(m + jnp.log(l), (BLOCK, HEAD)))
                kprev, vprev = kcur, vcur

    prev = lambda n: jnp.maximum(n * nsub - 1, 0)
    big = lambda col: pl.BlockSpec((ATTN_STEP, HEAD), lambda h, n: (n, col(h)))
    tail = lambda col: pl.BlockSpec((bt, HEAD), lambda h, n: (prev(n), col(h)))
    blk = 5 * _nbytes((ATTN_STEP, HEAD), F32) + 2 * _nbytes((bt, HEAD), F32)
    res, got = _pcall(
        body, name=name, grid=(N_HEADS, nsb),
        in_specs=[big(lambda h: grp * N_HEADS + h), tail(lambda h: h), big(lambda h: h),
                  tail(lambda h: vcol + h), big(lambda h: vcol + h)],
        out_specs=[big(lambda h: h), big(lambda h: h)],
        out_shape=[_sds((S, KV_W), F32), _sds((S, KV_W), F32)],
        params=_cparams(2, blk), args=[qn, kn, kn, proj, proj], plan=plan)
    return res if plan is None else (res, got)


def _attn_bwd(name, qn, kn, proj, dattn, lse, delta, grp, plan=None):
    S = qn.shape[0]
    dil, bt, nsub, nsb = _attn_geometry(S, grp)
    vcol = (Q_W + KV_W) // HEAD

    def slot(b, r):
        return pl.ds((b * dil + r) * BLOCK, BLOCK)

    def body(q_ref, kp_ref, kc_ref, vp_ref, vc_ref, do_ref, l_ref, d_ref, dq_ref, dk_ref, dv_ref, ck_ref, cv_ref):
        n = pl.program_id(1)
        par = n % 2

        @pl.when(n < nsb)
        def _():
            valid_first = _band_mask(n == 0)
            valid_inner = _band_mask(None)
            ck, cv = ck_ref.at[par], cv_ref.at[par]
            pk, pv = ck_ref.at[1 - par], cv_ref.at[1 - par]
            for r in range(dil):
                kprev, vprev = _bf(_rows(kp_ref, r, dil)), _bf(_rows(vp_ref, r, dil))
                own_k = own_v = None
                for b in range(nsub):
                    at = b * bt + r
                    kcur, vcur = _bf(_rows(kc_ref, at, dil)), _bf(_rows(vc_ref, at, dil))
                    q = _bf(_rows(q_ref, at, dil))
                    k2 = jnp.concatenate([kprev, kcur], axis=0)
                    v2 = jnp.concatenate([vprev, vcur], axis=0)
                    do = _bf(_rows(do_ref, at, dil))
                    lse_r = _rows(l_ref, at, dil)[:, :1]
                    del_r = _rows(d_ref, at, dil)[:, :1]
                    s = jnp.where(valid_first if b == 0 else valid_inner, _dot_nt(q, k2), NEG_INF)
                    p = jnp.exp(s - lse_r)
                    ds = _bf(p * (_dot_nt(do, v2) - del_r))
                    _set_rows(dq_ref, at, dil, _dot(ds, k2))
                    dk2 = _dot_tn(ds, q)
                    dv2 = _dot_tn(_bf(p), do)
                    if b > 0:
                        ck[slot(b - 1, r), :] = own_k + dk2[:BLOCK]
                        cv[slot(b - 1, r), :] = own_v + dv2[:BLOCK]
                    else:
                        @pl.when(n > 0)
                        def _():
                            pk[slot(nsub - 1, r), :] += dk2[:BLOCK]
                            pv[slot(nsub - 1, r), :] += dv2[:BLOCK]
                    own_k, own_v = dk2[BLOCK:], dv2[BLOCK:]
                    kprev, vprev = kcur, vcur
                ck[slot(nsub - 1, r), :] = own_k
                cv[slot(nsub - 1, r), :] = own_v

        @pl.when(n > 0)
        def _():
            for r in range(dil):
                for b in range(nsub):
                    _set_rows(dk_ref, b * bt + r, dil, ck_ref[1 - par, slot(b, r), :])
                    _set_rows(dv_ref, b * bt + r, dil, cv_ref[1 - par, slot(b, r), :])

    cur = lambda n: jnp.minimum(n, nsb - 1)
    prev = lambda n: jnp.maximum(n - 1, 0)
    tail_at = lambda n: jnp.maximum(cur(n) * nsub - 1, 0)
    big = lambda col: pl.BlockSpec((ATTN_STEP, HEAD), lambda h, n: (cur(n), col(h)))
    tail = lambda col: pl.BlockSpec((bt, HEAD), lambda h, n: (tail_at(n), col(h)))
    late = pl.BlockSpec((ATTN_STEP, HEAD), lambda h, n: (prev(n), h))
    blk = 9 * _nbytes((ATTN_STEP, HEAD), F32) + 2 * _nbytes((bt, HEAD), F32)
    res, got = _pcall(
        body, name=name, grid=(N_HEADS, nsb + 1),
        in_specs=[big(lambda h: grp * N_HEADS + h), tail(lambda h: h), big(lambda h: h),
                  tail(lambda h: vcol + h), big(lambda h: vcol + h),
                  big(lambda h: h), big(lambda h: h), big(lambda h: h)],
        out_specs=[big(lambda h: h), late, late],
        out_shape=[_sds((S, KV_W), F32)] * 3,
        scratch_shapes=[pltpu.VMEM((2, ATTN_STEP, HEAD), F32), pltpu.VMEM((2, ATTN_STEP, HEAD), F32)],
        params=_cparams(2, blk, 4 * _nbytes((ATTN_STEP, HEAD), F32)),
        args=[qn, kn, kn, proj, proj, dattn, lse, delta], plan=plan)
    return res if plan is None else (res, got)


def _window_sum(v, n_doublings, back):
    rows = v.shape[0]
    step = 1
    for _ in range(n_doublings):
        v = v + pltpu.roll(v, step if back else rows - step, 0)
        step *= 2
    return v


def _mix_post(name, outs, lses, proj, pool_w, pool_scale, *, tm):
    S = proj.shape[0]
    ucol = (IN_W - POOL_W) // POOL_W
    hpt = tm // POOL_HALO

    def body(o0, o1, o2, l0, l1, l2, u_ref, uh_ref, pw_ref, ps_ref, mix_ref, lse_ref, dp_ref):
        i = pl.program_id(0)
        for h in range(N_HEADS):
            sl = slice(h * HEAD, (h + 1) * HEAD)
            ls = [l0[:, sl], l1[:, sl], l2[:, sl]]
            m = jnp.maximum(jnp.maximum(ls[0], ls[1]), ls[2])
            ws = [jnp.exp(v - m) for v in ls]
            den = ws[0] + ws[1] + ws[2]
            num = ws[0] * o0[:, sl] + ws[1] * o1[:, sl] + ws[2] * o2[:, sl]
            mix_ref[:, sl] = _bf(num / den)
            lse_ref[:, sl] = m + jnp.log(den)
        halo = jnp.where(i == 0, 0.0, uh_ref[...])
        t = lax.broadcasted_iota(jnp.int32, (tm + POOL_HALO, HEAD), 0) + (i * tm - POOL_HALO)
        for g, w in enumerate(POOL_WINDOWS):
            sl = slice(g * HEAD, (g + 1) * HEAD)
            ub = jnp.concatenate([halo[:, sl], u_ref[:, sl]], axis=0)
            cnt = jnp.minimum(t + 1, w).astype(F32)
            d = (_window_sum(ub, g + 1, True) / cnt - ub)[POOL_HALO:]
            db = _bf(d)
            dp_ref[:, sl] = db
            mix_ref[:, KV_W + g * HEAD:KV_W + (g + 1) * HEAD] = _bf(_dot(db, pw_ref[g]) * ps_ref[:, sl])

    tile = pl.BlockSpec((tm, KV_W), lambda i: (i, 0))
    blk = 9 * _nbytes((tm, KV_W), F32) + _nbytes((tm, 2 * KV_W), BF16)
    return pl.pallas_call(
        body, name=name, grid=(S // tm,),
        in_specs=[tile] * 6 + [
            pl.BlockSpec((tm, POOL_W), lambda i: (i, ucol)),
            pl.BlockSpec((POOL_HALO, POOL_W), lambda i: (jnp.maximum(i * hpt - 1, 0), ucol)),
            pl.BlockSpec((len(POOL_WINDOWS), HEAD, HEAD), lambda i: (0, 0, 0)),
            pl.BlockSpec((1, POOL_W), lambda i: (0, 0))],
        out_specs=[pl.BlockSpec((tm, 2 * KV_W), lambda i: (i, 0)), tile, tile],
        out_shape=[_sds((S, 2 * KV_W), BF16), _sds((S, KV_W), F32),
                   _sds((S, POOL_W), BF16)],
        compiler_params=_cparams(1, blk))(*_pin([*outs, *lses, proj, proj, pool_w, pool_scale]))


def _mix_bwd(name, dmix, mix, dpool, pool_w, pool_scale, *, tm, plan=None):
    S = dmix.shape[0]
    hpt = tm // POOL_HALO
    last_halo = S // POOL_HALO - 1
    n_tiles = S // tm

    def body(dm_ref, dh_ref, at_ref, dp_ref, pw_ref, ps_ref, da_ref, dl_ref, du_ref, gw_ref, gs_ref):
        i = pl.program_id(0)

        @pl.when(i == 0)
        def _():
            gw_ref[...] = jnp.zeros_like(gw_ref)
            gs_ref[...] = jnp.zeros_like(gs_ref)

        for h in range(N_HEADS):
            sl = slice(h * HEAD, (h + 1) * HEAD)
            da = dm_ref[:, sl]
            da_ref[:, sl] = da
            dl_ref[:, sl] = jnp.broadcast_to(
                jnp.sum(da * at_ref[:, sl].astype(F32), axis=-1, keepdims=True), (tm, HEAD))
        halo = jnp.where(i == n_tiles - 1, 0.0, dh_ref[...])
        t = lax.broadcasted_iota(jnp.int32, (tm + POOL_HALO, HEAD), 0) + i * tm
        for g, w in enumerate(POOL_WINDOWS):
            sl = slice(g * HEAD, (g + 1) * HEAD)
            dy = jnp.concatenate([dm_ref[:, KV_W + g * HEAD:KV_W + (g + 1) * HEAD], halo[:, sl]], axis=0)
            dys = _bf(dy * ps_ref[:, sl])
            dd = _dot_nt(dys, pw_ref[g])
            cnt = jnp.minimum(t + 1, w).astype(F32)
            du_ref[:, sl] = (_window_sum(dd / cnt, g + 1, False) - dd)[:tm]
            db = dp_ref[:, sl]
            gw_ref[g] += _dot_tn(db, dys[:tm])
            gs_ref[:, sl] += jnp.sum(dy[:tm] * _dot(db, pw_ref[g]), axis=0, keepdims=True)

    tile = pl.BlockSpec((tm, KV_W), lambda i: (i, 0))
    blk = _nbytes((tm, 2 * KV_W), F32) + 6 * _nbytes((tm, KV_W), F32)
    res, got = _pcall(
        body, name=name, grid=(n_tiles,),
        in_specs=[pl.BlockSpec((tm, 2 * KV_W), lambda i: (i, 0)),
                  pl.BlockSpec((POOL_HALO, POOL_W), lambda i: (jnp.minimum((i + 1) * hpt, last_halo), 1)),
                  tile, tile,
                  pl.BlockSpec((len(POOL_WINDOWS), HEAD, HEAD), lambda i: (0, 0, 0)),
                  pl.BlockSpec((1, POOL_W), lambda i: (0, 0))],
        out_specs=[tile, tile, tile,
                   pl.BlockSpec((len(POOL_WINDOWS), HEAD, HEAD), lambda i: (0, 0, 0)),
                   pl.BlockSpec((1, POOL_W), lambda i: (0, 0))],
        out_shape=[_sds((S, KV_W), F32)] * 3 + [
            _sds((len(POOL_WINDOWS), HEAD, HEAD), F32), _sds((1, POOL_W), F32)],
        params=_cparams(1, blk), args=[dmix, dmix, mix, dpool, pool_w, pool_scale], plan=plan)
    return res if plan is None else (res, got)


def _qkv_bwd(name, dqs, dks, dvs, du, proj, ctab, stab, qg, kg, *, tm, plan=None):
    S = proj.shape[0]
    width = Q_W + KV_W

    def body(dq0, dq1, dq2, dk0, dk1, dk2, dv0, dv1, dv2, du_ref, p_ref, c_ref, s_ref, qg_ref, kg_ref,
             dp_ref, gq_ref, gk_ref):
        @pl.when(pl.program_id(0) == 0)
        def _():
            gq_ref[...] = jnp.zeros_like(gq_ref)
            gk_ref[...] = jnp.zeros_like(gk_ref)

        lane = lax.broadcasted_iota(jnp.int32, (tm, HEAD), 1)
        cv = c_ref[...]
        sv = s_ref[...]

        def back(dy, t, g, scale):
            dy = dy * scale
            dn = dy * cv + _swap_halves(dy * sv, lane, True)
            dt, xh = _norm_bwd(dn, t, g)
            return dt, jnp.sum(dn * xh, axis=0, keepdims=True)

        dqr = (dq0, dq1, dq2)
        gq = jnp.zeros((1, HEAD), F32)
        for j in range(Q_W // HEAD):
            grp, h = divmod(j, N_HEADS)
            dt, gj = back(dqr[grp][:, h * HEAD:(h + 1) * HEAD], p_ref[:, j * HEAD:(j + 1) * HEAD], qg_ref[...], SCALE)
            dp_ref[:, j * HEAD:(j + 1) * HEAD] = _bf(dt)
            gq = gq + gj
        gq_ref[...] += gq
        gk = jnp.zeros((1, HEAD), F32)
        for h in range(N_HEADS):
            sl = slice(h * HEAD, (h + 1) * HEAD)
            dt, gj = back(dk0[:, sl] + dk1[:, sl] + dk2[:, sl], p_ref[:, Q_W + h * HEAD:Q_W + (h + 1) * HEAD],
                          kg_ref[...], 1.0)
            dp_ref[:, Q_W + h * HEAD:Q_W + (h + 1) * HEAD] = _bf(dt)
            gk = gk + gj
        gk_ref[...] += gk
        dp_ref[:, width:width + KV_W] = _bf(dv0[...] + dv1[...] + dv2[...])
        dp_ref[:, width + KV_W:] = _bf(du_ref[...])

    tile = pl.BlockSpec((tm, KV_W), lambda i: (i, 0))
    vec = pl.BlockSpec((1, HEAD), lambda i: (0, 0))
    rot = pl.BlockSpec((tm, HEAD), lambda i: (i, 0))
    blk = 10 * _nbytes((tm, KV_W), F32) + _nbytes((tm, width), F32) + _nbytes((tm, IN_W), BF16)
    res, got = _pcall(
        body, name=name, grid=(S // tm,),
        in_specs=[tile] * 10 + [pl.BlockSpec((tm, width), lambda i: (i, 0)), rot, rot, vec, vec],
        out_specs=[pl.BlockSpec((tm, IN_W), lambda i: (i, 0)), vec, vec],
        out_shape=[_sds((S, IN_W), BF16), _sds((1, HEAD), F32),
                   _sds((1, HEAD), F32)],
        params=_cparams(1, blk), args=[*dqs, *dks, *dvs, du, proj, ctab, stab, qg, kg], plan=plan)
    return res if plan is None else (res, got)


def _cross_heads(q_ref, kv_ref, qg, kg, h):
    sl = slice(h * HEAD, (h + 1) * HEAD)
    qr = q_ref[:, sl]
    kr = kv_ref[:, sl]
    qh = qr * _rstd(qr) * qg * SCALE
    kh = kr * _rstd(kr) * kg
    vh = kv_ref[:, X_W + h * HEAD:X_W + (h + 1) * HEAD]
    return qr, _bf(qh), _bf(kh), _bf(vh)


def _cross_fwd(name, qraw, kv, qg, kg, *, tm):
    S = qraw.shape[0]
    M = kv.shape[0]

    def body(q_ref, kv_ref, qg_ref, kg_ref, o_ref):
        for h in range(N_HEADS):
            _, qh, kh, vh = _cross_heads(q_ref, kv_ref, qg_ref[...], kg_ref[...], h)
            s = _dot_nt(qh, kh)
            p = jnp.exp(s - jnp.max(s, axis=-1, keepdims=True))
            l = jnp.sum(p, axis=-1, keepdims=True)
            o_ref[:, h * HEAD:(h + 1) * HEAD] = _bf(_dot(_bf(p), vh) / l)

    vec = pl.BlockSpec((1, HEAD), lambda i: (0, 0))
    blk = 2 * _nbytes((tm, X_W), F32) + _nbytes((M, 2 * X_W), F32) + 4 * _nbytes((tm, M), F32)
    return pl.pallas_call(
        body, name=name, grid=(S // tm,),
        in_specs=[pl.BlockSpec((tm, X_W), lambda i: (i, 0)), pl.BlockSpec((M, 2 * X_W), lambda i: (0, 0)), vec, vec],
        out_specs=pl.BlockSpec((tm, X_W), lambda i: (i, 0)),
        out_shape=_sds((S, X_W), BF16),
        compiler_params=_cparams(1, blk))(*_pin([qraw, kv, qg, kg]))


def _cross_bwd(name, do, qraw, kv, qg, kg, *, tm, plan=None):
    S = qraw.shape[0]
    M = kv.shape[0]

    def body(do_ref, q_ref, kv_ref, qg_ref, kg_ref, dq_ref, dk_ref, dv_ref, gq_ref):
        @pl.when(pl.program_id(0) == 0)
        def _():
            dk_ref[...] = jnp.zeros_like(dk_ref)
            dv_ref[...] = jnp.zeros_like(dv_ref)
            gq_ref[...] = jnp.zeros_like(gq_ref)

        gq = jnp.zeros((1, HEAD), F32)
        for h in range(N_HEADS):
            sl = slice(h * HEAD, (h + 1) * HEAD)
            qr, qh, kh, vh = _cross_heads(q_ref, kv_ref, qg_ref[...], kg_ref[...], h)
            doh = _bf(do_ref[:, sl])
            s = _dot_nt(qh, kh)
            p = jnp.exp(s - jnp.max(s, axis=-1, keepdims=True))
            p = p / jnp.sum(p, axis=-1, keepdims=True)
            pb = _bf(p)
            dp = _dot_nt(doh, vh)
            ds = _bf(p * (dp - jnp.sum(dp * p, axis=-1, keepdims=True)))
            dv_ref[:, sl] += _dot_tn(pb, doh)
            dk_ref[:, sl] += _dot_tn(ds, qh)
            dn = _dot(ds, kh) * SCALE
            dt, xh = _norm_bwd(dn, qr, qg_ref[...])
            dq_ref[:, sl] = _bf(dt)
            gq = gq + jnp.sum(dn * xh, axis=0, keepdims=True)
        gq_ref[...] += gq

    vec = pl.BlockSpec((1, HEAD), lambda i: (0, 0))
    acc = pl.BlockSpec((M, X_W), lambda i: (0, 0))
    blk = 3 * _nbytes((tm, X_W), F32) + 3 * _nbytes((M, 2 * X_W), F32) + 6 * _nbytes((tm, M), F32)
    res, got = _pcall(
        body, name=name, grid=(S // tm,),
        in_specs=[pl.BlockSpec((tm, X_W), lambda i: (i, 0)), pl.BlockSpec((tm, X_W), lambda i: (i, 0)),
                  pl.BlockSpec((M, 2 * X_W), lambda i: (0, 0)), vec, vec],
        out_specs=[pl.BlockSpec((tm, X_W), lambda i: (i, 0)), acc, acc, vec],
        out_shape=[_sds((S, X_W), BF16), _sds((M, X_W), F32),
                   _sds((M, X_W), F32), _sds((1, HEAD), F32)],
        params=_cparams(1, blk), args=[do, qraw, kv, qg, kg], plan=plan)
    return res if plan is None else (res, got)


def _cross_kv_bwd(name, dkn, dv, kv, kg):
    M = kv.shape[0]

    def body(dk_ref, dv_ref, kv_ref, kg_ref, o_ref, g_ref):
        gk = jnp.zeros((1, HEAD), F32)
        for h in range(N_HEADS):
            sl = slice(h * HEAD, (h + 1) * HEAD)
            dn = dk_ref[:, sl]
            dt, xh = _norm_bwd(dn, kv_ref[:, sl], kg_ref[...])
            o_ref[:, sl] = _bf(dt)
            gk = gk + jnp.sum(dn * xh, axis=0, keepdims=True)
        o_ref[:, X_W:] = _bf(dv_ref[...])
        g_ref[...] = gk

    full = lambda shape: pl.BlockSpec(shape, lambda i: (0,) * len(shape))
    return pl.pallas_call(
        body, name=name, grid=(1,),
        in_specs=[full((M, X_W)), full((M, X_W)), full((M, 2 * X_W)), full((1, HEAD))],
        out_specs=[full((M, 2 * X_W)), full((1, HEAD))],
        out_shape=[_sds((M, 2 * X_W), BF16), _sds((1, HEAD), F32)],
        compiler_params=_cparams(1, 6 * _nbytes((M, 2 * X_W), F32)))(dkn, dv, kv, kg)


def _rope_tables(positions):
    inv_freq = ROPE_THETA ** (-jnp.arange(0, ROT_DIM, 2, dtype=F32) / ROT_DIM)
    ang = positions.astype(F32)[:, None] * inv_freq
    cos, sin = jnp.cos(ang), jnp.sin(ang)
    S = positions.shape[0]
    ctab = jnp.concatenate([cos, cos, jnp.ones((S, HEAD - ROT_DIM), F32)], axis=-1)
    stab = jnp.concatenate([-sin, sin, jnp.zeros((S, HEAD - ROT_DIM), F32)], axis=-1)
    return ctab, stab


GATHER_BEHIND_IN_PROJ = ("w_out", "w_cq", "w_ckv", "w_co")
FFN_WEIGHTS = ("w_gate_up", "w_down")


def _hosted(fn, *args, plan=None, **kw):
    if plan is None:
        return fn(*args, **kw), []
    return fn(*args, plan=plan, **kw)


def _local_step(x, mem, positions, target, wb, sm, *, tm=512, place=None):
    S, D = x.shape
    M = mem.shape[0]
    dist = place is not None
    wb = dict(wb)
    ctab, stab = _rope_tables(positions)
    pool_w_b = _bf(sm["pool_w"])

    def gather(names):
        return _GatherPlan([wb[k] for k in names]) if dist else None

    if dist:
        wb["w_in"], = _run_plan("gather_w_in", gather(["w_in"]))
    w_in = wb["w_in"]
    (proj, xn1), got = _hosted(_mm_nn, "in_proj", x, w_in, tm=tm, norm_g=sm["mix_norm_g"],
                               plan=gather(GATHER_BEHIND_IN_PROJ))
    wb.update(zip(GATHER_BEHIND_IN_PROJ, got))
    (qn, kn), got = _hosted(_qk_prep, "qk_prep", proj, ctab, stab, sm["q_norm_g"], sm["k_norm_g"], tm=tm,
                            plan=gather(["w_gate_up"]))
    wb.update(zip(["w_gate_up"], got))
    outs, lses = [], []
    for grp in range(len(DILATIONS)):
        last = grp == len(DILATIONS) - 1
        (o, l), got = _hosted(_attn_fwd, f"attn_fwd{grp}", qn, kn, proj, grp, plan=gather(["w_down"]) if last else None)
        wb.update(zip(["w_down"], got))
        outs.append(o)
        lses.append(l)
    w_out = wb["w_out"].reshape(1, 2 * KV_W, D)
    w_cq = wb["w_cq"].reshape(1, D, X_W)
    w_ckv = wb["w_ckv"].reshape(1, D, 2 * X_W)
    w_co = wb["w_co"]
    w_gu = wb["w_gate_up"]
    w_down = wb["w_down"].reshape(1, D_FF, D)
    cin = w_in.shape[2]
    cco = w_co.shape[2]
    cgu = w_gu.shape[2]
    mix, lse, dpool = _mix_post("mix_post", outs, lses, proj, pool_w_b, sm["pool_scale"], tm=tm)
    h1 = _mm_nn("out_proj", mix, w_out, tm=tm, residual=x)
    cq_raw, hn2 = _mm_nn("cq_proj", h1, w_cq, tm=tm, norm_g=sm["cross_norm_g"])
    kv, mem_n = _mm_nn("ckv_proj", mem, w_ckv, tm=M, norm_g=sm["mem_norm_g"])
    xo = _cross_fwd("cross_fwd", cq_raw, kv, sm["cq_norm_g"], sm["ck_norm_g"], tm=tm)
    h2 = _mm_nn("co_proj", xo, w_co, tm=tm, residual=h1)
    act, gu, hn3 = _ffn_up("ffn_up", h2, w_gu, sm["ffn_norm_g"], tm=tm)
    dy, lsum = _ffn_down_loss("ffn_down_loss", act, w_down, h2, target, tm=tm)
    loss = 0.5 * jnp.sum(lsum) / D

    nS = S // tm
    dgu = _ffn_down_bwd("ffn_down_bwd", dy, w_down, gu, tm=tm, tn=cgu)
    g_down = _mm_tn("g_w_down", act, pl.BlockSpec((tm, cgu), lambda r, c, s: (s, r)), (tm, cgu),
                    dy, pl.BlockSpec((tm, D), lambda r, c, s: (s, 0)), (tm, D),
                    (1, D_FF, D), pl.BlockSpec((None, cgu, D), lambda r, c, s: (0, r, 0)), (cgu, D),
                    (D_FF // cgu, 1, nS))
    dh2, g_ffn_norm = _ffn_up_bwd("ffn_up_bwd", dgu, w_gu, h2, sm["ffn_norm_g"], dy, tm=tm)
    g_gu = _mm_tn("g_w_gate_up", hn3, pl.BlockSpec((tm, D), lambda r, c, s: (s, 0)), (tm, D),
                  dgu, pl.BlockSpec((None, tm, cgu), lambda r, c, s: (c // 2, s, c % 2)), (tm, cgu),
                  (4, D, cgu), pl.BlockSpec((None, D, cgu), lambda r, c, s: (c, 0, 0)), (D, cgu),
                  (1, 4, nS))

    dxo = _mm_nt("co_proj_bwd", dh2, w_co, tm=tm, out_dtype=BF16)
    g_co = _mm_tn_wide("g_w_co", xo, dh2, N_CHIPS, ts=tm)
    full = {"w_gate_up": g_gu, "w_down": g_down.reshape(N_CHIPS, D_FF // N_CHIPS, D)}
    sums = {}

    def swap(names):
        return _SwapPlan([full[k] for k in names]) if dist else None

    def add_halves(names, from_sibling):
        for k, t in zip(names, from_sibling):
            sums[k] = _add_core_halves(f"add_halves_{k}", full[k], t, place[0])

    def exchange(names):
        return _ExchangePlan([sums[k][1] for k in names]) if dist else None

    def sum_chips(names, from_chips):
        return [_sum_chips(f"sum_chips_{k}", sums[k][0], t, place[1]) for k, t in zip(names, from_chips)]

    (dcq, dkn, dvm, g_cq_norm), got = _hosted(_cross_bwd, "cross_bwd", dxo, cq_raw, kv, sm["cq_norm_g"],
                                              sm["ck_norm_g"], tm=tm, plan=swap(FFN_WEIGHTS))
    add_halves(FFN_WEIGHTS, got)
    dkv, g_ck_norm = _cross_kv_bwd("cross_kv_bwd", dkn, dvm, kv, sm["ck_norm_g"])
    dh1, g_cross_norm = _mm_nt_normbwd("cq_proj_bwd", dcq, w_cq, h1, sm["cross_norm_g"], dh2, tm=tm)
    g_cq = _mm_tn("g_w_cq", hn2, pl.BlockSpec((tm, D), lambda r, c, s: (s, 0)), (tm, D),
                  dcq, pl.BlockSpec((tm, X_W), lambda r, c, s: (s, 0)), (tm, X_W),
                  (1, D, X_W), pl.BlockSpec((None, D, X_W), lambda r, c, s: (0, 0, 0)), (D, X_W), (1, 1, nS))
    _, g_mem_norm = _mm_nt_normbwd("ckv_proj_bwd", dkv, w_ckv, mem, sm["mem_norm_g"], None, tm=M)
    g_ckv = _mm_tn("g_w_ckv", mem_n, pl.BlockSpec((M, D), lambda r, c, s: (0, 0)), (M, D),
                   dkv, pl.BlockSpec((M, 2 * X_W), lambda r, c, s: (0, 0)), (M, 2 * X_W),
                   (1, D, 2 * X_W), pl.BlockSpec((None, D, 2 * X_W), lambda r, c, s: (0, 0, 0)), (D, 2 * X_W),
                   (1, 1, 1))

    dmix = _mm_nt("out_proj_bwd", dh1, w_out, tm=tm, out_dtype=F32)
    g_out = _mm_tn("g_w_out", mix, pl.BlockSpec((tm, 2 * KV_W), lambda r, c, s: (s, 0)), (tm, 2 * KV_W),
                   dh1, pl.BlockSpec((tm, D), lambda r, c, s: (s, 0)), (tm, D),
                   (1, 2 * KV_W, D), pl.BlockSpec((None, 2 * KV_W, D), lambda r, c, s: (0, 0, 0)), (2 * KV_W, D),
                   (1, 1, nS))
    full.update({
        "w_out": g_out.reshape(N_CHIPS, 2 * KV_W // N_CHIPS, D),
        "w_cq": g_cq.reshape(N_CHIPS, D // N_CHIPS, X_W),
        "w_ckv": g_ckv.reshape(N_CHIPS, D // N_CHIPS, 2 * X_W),
        "w_co": g_co,
    })
    mixer = GATHER_BEHIND_IN_PROJ
    (dattn, delta, du, g_pool_w, g_pool_scale), got = _hosted(
        _mix_bwd, "mix_bwd", dmix, mix, dpool, pool_w_b, sm["pool_scale"], tm=tm, plan=swap(mixer))
    add_halves(mixer, got)
    behind_attn = (None, mixer, FFN_WEIGHTS)
    halves = {}
    dqs, dks, dvs = [], [], []
    for grp in range(len(DILATIONS)):
        names = behind_attn[grp]
        (dq, dk, dv), got = _hosted(_attn_bwd, f"attn_bwd{grp}", qn, kn, proj, dattn, lse, delta, grp,
                                    plan=exchange(names) if names else None)
        if dist and names:
            halves.update(zip(names, sum_chips(names, got)))
        dqs.append(dq)
        dks.append(dk)
        dvs.append(dv)
    joined = mixer + FFN_WEIGHTS
    (dproj, g_q_norm, g_k_norm), got = _hosted(
        _qkv_bwd, "qkv_bwd", dqs, dks, dvs, du, proj, ctab, stab, sm["q_norm_g"], sm["k_norm_g"], tm=tm,
        plan=_JoinPlan([halves[k] for k in joined]) if dist else None)
    shards = dict(zip(joined, got))
    dx, g_mix_norm = _mm_nt_normbwd("in_proj_bwd", dproj, w_in, x, sm["mix_norm_g"], dh1, tm=tm)
    small = {
        "mix_norm_g": g_mix_norm, "q_norm_g": g_q_norm, "k_norm_g": g_k_norm, "pool_w": g_pool_w,
        "pool_scale": g_pool_scale, "cross_norm_g": g_cross_norm, "mem_norm_g": g_mem_norm,
        "cq_norm_g": g_cq_norm, "ck_norm_g": g_ck_norm, "ffn_norm_g": g_ffn_norm,
    }
    full["w_in"], got = _hosted(_mm_tn_wide, "g_w_in", xn1, dproj, N_CHIPS, ts=tm,
                                plan=_AllPushPlan(_pack_small(small)) if dist else None)
    if dist:
        small = _unpack_small(_sum_slots("sum_small", got[0]), sm)
        add_halves(["w_in"], _run_plan("swap_w_in", swap(["w_in"])))
        half, = sum_chips(["w_in"], _run_plan("exchange_w_in", exchange(["w_in"])))
        shards["w_in"], = _run_plan("join_w_in", _JoinPlan([half]))
    big = shards if dist else full
    return loss, dx, big, small


BIG = ("w_in", "w_out", "w_cq", "w_ckv", "w_co", "w_gate_up", "w_down")
SMALL = ("mix_norm_g", "q_norm_g", "k_norm_g", "pool_w", "pool_scale", "cross_norm_g", "mem_norm_g",
         "cq_norm_g", "ck_norm_g", "ffn_norm_g")
WEIGHTS = ("mix_norm_g", "w_in", "q_norm_g", "k_norm_g", "pool_w", "pool_scale", "w_out", "cross_norm_g",
           "mem_norm_g", "w_cq", "w_ckv", "cq_norm_g", "ck_norm_g", "w_co", "ffn_norm_g", "w_gate_up", "w_down")


def _cast_piece(name, w, k_arr):
    R, C = w.shape
    hr = R // 2

    def body(k_ref, w_ref, o_ref):
        o_ref[...] = _bf(w_ref[...])

    return pl.pallas_call(
        body, name=name,
        grid_spec=pltpu.PrefetchScalarGridSpec(
            num_scalar_prefetch=1, grid=(2,),
            in_specs=[pl.BlockSpec((hr, C), lambda i, k: (i, 0))],
            out_specs=pl.BlockSpec((None, hr, C), lambda i, k: (k[0], i, 0))),
        out_shape=_sds((N_CHIPS, R, C), BF16),
        compiler_params=_cparams(1, 2 * _nbytes((hr, C), F32)))(k_arr, *_pin([w]))


def _add_core_halves(name, g, t, c_arr):
    P, R, C = g.shape
    hr = R // 2

    def body(c_ref, g_ref, t_ref, o_ref, ob_ref):
        tot = g_ref[...] + t_ref[...]
        o_ref[...] = tot
        ob_ref[...] = _bf(tot)

    piece = pl.BlockSpec((None, hr, C), lambda p, c: (p, 0, 0))
    return pl.pallas_call(
        body, name=name,
        grid_spec=pltpu.PrefetchScalarGridSpec(
            num_scalar_prefetch=1, grid=(P,),
            in_specs=[pl.BlockSpec((None, hr, C), lambda p, c: (p, c[0], 0)), piece],
            out_specs=[piece, piece]),
        out_shape=[_sds((P, hr, C), F32), _sds((P, hr, C), BF16)],
        compiler_params=_cparams(1, 4 * _nbytes((hr, C), F32)))(c_arr, *_pin([g, t]))


def _sum_chips(name, own, got, kc_arr):
    P, hr, C = own.shape

    def body(kc_ref, o_ref, g_ref, r_ref):
        r_ref[...] = ((o_ref[...] + g_ref[0].astype(F32)) + g_ref[1].astype(F32)) + g_ref[2].astype(F32)

    return pl.pallas_call(
        body, name=name,
        grid_spec=pltpu.PrefetchScalarGridSpec(
            num_scalar_prefetch=1, grid=(1,),
            in_specs=[pl.BlockSpec((None, hr, C), lambda i, kc: (kc[0], 0, 0)),
                      pl.BlockSpec((N_CHIPS - 1, hr, C), lambda i, kc: (0, 0, 0))],
            out_specs=pl.BlockSpec((hr, C), lambda i, kc: (kc[1], 0))),
        out_shape=_sds((2 * hr, C), F32),
        compiler_params=_cparams(1, 5 * _nbytes((hr, C), F32)))(kc_arr, *_pin([own, got]))


N_DEV = 8


class _AllPushPlan(_Plan):
    def __init__(self, v):
        self.ins = [v]
        self.out_shapes = [jax.ShapeDtypeStruct((N_DEV,) + v.shape, v.dtype)]
        self.sem_shapes = [pltpu.SemaphoreType.DMA((N_DEV - 1,)), pltpu.SemaphoreType.DMA((N_DEV - 1,)),
                           pltpu.SemaphoreType.DMA]

    def copies(self, ins, outs, sems):
        send, recv, own = sems
        x, y, c, _ = _place()
        slot = outs[0].at[4 * x + 2 * y + c]
        cps = [pltpu.make_async_copy(ins[0], slot, own)]
        flips = [(dx, dy, dc) for dx in (0, 1) for dy in (0, 1) for dc in (0, 1)][1:]
        for q, (dx, dy, dc) in enumerate(flips):
            to = (x + dx - 2 * x * dx, y + dy - 2 * y * dy, c + dc - 2 * c * dc)
            cps.append(pltpu.make_async_remote_copy(src_ref=ins[0], dst_ref=slot, send_sem=send.at[q],
                                                    recv_sem=recv.at[q], device_id=to, device_id_type=MESH))
        return cps


def _sum_slots(name, slots):
    n, R, C = slots.shape

    def body(s_ref, o_ref):
        acc = s_ref[0]
        for d in range(1, n):
            acc = acc + s_ref[d]
        o_ref[...] = acc

    return pl.pallas_call(
        body, name=name, grid=(1,),
        in_specs=[pl.BlockSpec((n, R, C), lambda i: (0, 0, 0))], out_specs=pl.BlockSpec((R, C), lambda i: (0, 0)),
        out_shape=_sds((R, C), F32),
        compiler_params=_cparams(1, _nbytes((n + 1, R, C), F32)))(*_pin([slots]))


def _adamw(name, w, g, m, v, *, tr):
    R, C = w.shape

    def body(w_ref, g_ref, m_ref, v_ref, d_ref, nm_ref, nv_ref):
        gv = g_ref[...]
        nm = ADAM_B1 * m_ref[...] + (1.0 - ADAM_B1) * gv
        nv = ADAM_B2 * v_ref[...] + (1.0 - ADAM_B2) * (gv * gv)
        m_hat = nm / (1.0 - ADAM_B1 ** ADAM_STEP)
        v_hat = nv / (1.0 - ADAM_B2 ** ADAM_STEP)
        d_ref[...] = -ADAM_LR * (m_hat / (jnp.sqrt(v_hat) + ADAM_EPS) + ADAM_WD * w_ref[...])
        nm_ref[...] = nm
        nv_ref[...] = nv

    tile = pl.BlockSpec((tr, C), lambda i: (i, 0))
    return pl.pallas_call(
        body, name=name, grid=(R // tr,), in_specs=[tile] * 4, out_specs=[tile] * 3,
        out_shape=[_sds((R, C), F32)] * 3,
        compiler_params=_cparams(1, 7 * _nbytes((tr, C), F32)))(*_pin([w, g, m, v]))


def _pack_small(d):
    parts = []
    for name in SMALL:
        a = d[name].reshape(-1, HEAD)
        pad = (-a.shape[0]) % 8
        parts.append(jnp.pad(a, ((0, pad), (0, 0))))
    return jnp.concatenate(parts, axis=0)


def _unpack_small(packed, like):
    out = {}
    row = 0
    for name in SMALL:
        shape = like[name].shape
        rows = like[name].size // HEAD
        out[name] = packed[row:row + rows].reshape(shape)
        row += rows + (-rows) % 8
    return out


def kernel(x, mem, positions, mix_norm_g, w_in, q_norm_g, k_norm_g, pool_w, pool_scale, w_out, cross_norm_g, mem_norm_g, w_cq, w_ckv, cq_norm_g, ck_norm_g, w_co, ffn_norm_g, w_gate_up, w_down, loss_target, m_mix_norm_g, m_w_in, m_q_norm_g, m_k_norm_g, m_pool_w, m_pool_scale, m_w_out, m_cross_norm_g, m_mem_norm_g, m_w_cq, m_w_ckv, m_cq_norm_g, m_ck_norm_g, m_w_co, m_ffn_norm_g, m_w_gate_up, m_w_down, v_mix_norm_g, v_w_in, v_q_norm_g, v_k_norm_g, v_pool_w, v_pool_scale, v_w_out, v_cross_norm_g, v_mem_norm_g, v_w_cq, v_w_ckv, v_cq_norm_g, v_ck_norm_g, v_w_co, v_ffn_norm_g, v_w_gate_up, v_w_down):
    w = dict(mix_norm_g=mix_norm_g, w_in=w_in, q_norm_g=q_norm_g, k_norm_g=k_norm_g, pool_w=pool_w,
             pool_scale=pool_scale, w_out=w_out, cross_norm_g=cross_norm_g, mem_norm_g=mem_norm_g, w_cq=w_cq,
             w_ckv=w_ckv, cq_norm_g=cq_norm_g, ck_norm_g=ck_norm_g, w_co=w_co, ffn_norm_g=ffn_norm_g,
             w_gate_up=w_gate_up, w_down=w_down)
    m = dict(mix_norm_g=m_mix_norm_g, w_in=m_w_in, q_norm_g=m_q_norm_g, k_norm_g=m_k_norm_g, pool_w=m_pool_w,
             pool_scale=m_pool_scale, w_out=m_w_out, cross_norm_g=m_cross_norm_g, mem_norm_g=m_mem_norm_g,
             w_cq=m_w_cq, w_ckv=m_w_ckv, cq_norm_g=m_cq_norm_g, ck_norm_g=m_ck_norm_g, w_co=m_w_co,
             ffn_norm_g=m_ffn_norm_g, w_gate_up=m_w_gate_up, w_down=m_w_down)
    v = dict(mix_norm_g=v_mix_norm_g, w_in=v_w_in, q_norm_g=v_q_norm_g, k_norm_g=v_k_norm_g, pool_w=v_pool_w,
             pool_scale=v_pool_scale, w_out=v_w_out, cross_norm_g=v_cross_norm_g, mem_norm_g=v_mem_norm_g,
             w_cq=v_w_cq, w_ckv=v_w_ckv, cq_norm_g=v_cq_norm_g, ck_norm_g=v_ck_norm_g, w_co=v_w_co,
             ffn_norm_g=v_ffn_norm_g, w_gate_up=v_w_gate_up, w_down=v_w_down)

    c_arr = lax.axis_index("c").astype(jnp.int32).reshape(1)
    k_arr = (2 * lax.axis_index("x") + lax.axis_index("y")).astype(jnp.int32).reshape(1)
    kc_arr = jnp.concatenate([k_arr, c_arr])
    wb = {k: _cast_piece(f"cast_{k}", w[k][0], k_arr) for k in BIG}
    sm = {k: (w[k][0] if k == "pool_w" else w[k]) for k in SMALL}
    loss_part, dx, gshard, gsm = _local_step(x[0], mem[0], positions[0], loss_target[0], wb, sm,
                                             place=(c_arr, kc_arr))
    loss = lax.psum(loss_part, ("x", "y", "c"))

    grads, deltas, new_m, new_v = {}, {}, {}, {}
    for k in BIG:
        shard = w[k][0]
        tr = shard.shape[0] // 4
        d, nm, nv = _adamw(f"adamw_{k}", shard, gshard[k], m[k][0], v[k][0], tr=tr)
        grads[k], deltas[k], new_m[k], new_v[k] = gshard[k][None], d[None], nm[None], nv[None]
    smw = {k: (w[k][0] if k == "pool_w" else w[k]) for k in SMALL}
    smm = {k: (m[k][0] if k == "pool_w" else m[k]) for k in SMALL}
    smv = {k: (v[k][0] if k == "pool_w" else v[k]) for k in SMALL}
    pw, pg, pm, pv = _pack_small(smw), _pack_small(gsm), _pack_small(smm), _pack_small(smv)
    d, nm, nv = _adamw("adamw_small", pw, pg, pm, pv, tr=pw.shape[0])
    for dst, packed in ((deltas, d), (new_m, nm), (new_v, nv)):
        un = _unpack_small(packed, sm)
        for k in SMALL:
            dst[k] = un[k].reshape(w[k].shape)
    for k in SMALL:
        grads[k] = gsm[k].reshape(w[k].shape)

    return (loss, dx[None], *[grads[k] for k in WEIGHTS], *[deltas[k] for k in WEIGHTS],
            *[new_m[k] for k in WEIGHTS], *[new_v[k] for k in WEIGHTS])
```

```python
import functools

import jax
import jax.numpy as jnp
from jax import lax
from jax.experimental import pallas as pl
from jax.experimental.pallas import tpu as pltpu

F32 = jnp.float32
BF16 = jnp.bfloat16
MESH = pl.DeviceIdType.MESH
ANY = pl.BlockSpec(memory_space=pl.ANY)

D_MODEL = 1024
HEAD = 128
N_HEADS = 4
DILATIONS = (1, 4, 16)
BLOCK = 128
Q_W = 1536
KV_W = 512
POOL_W = 512
POOL_WINDOWS = (2, 4, 8, 16)
POOL_HALO = 16
IN_W = 3072
ROT_DIM = 32
ROT_HALF = 16
ROPE_THETA = 500000.0
X_W = 512
D_FF = 2816
EPS = 1e-6
NEG_INF = -1e30
SCALE = HEAD ** -0.5
N_CHIPS = 4

ADAM_LR = 0.001
ADAM_B1 = 0.9
ADAM_B2 = 0.999
ADAM_EPS = 1e-08
ADAM_WD = 0.01
ADAM_STEP = 10

VMEM_BYTES_V7X = 64 * 2 ** 20
VMEM_LIMIT_MAX = 56 * 2 ** 20
VMEM_LIMIT_MIN = 24 * 2 ** 20


def _nbytes(shape, dtype):
    n = 1
    for s in shape:
        n *= s
    return n * jnp.dtype(dtype).itemsize


def _cparams(n_axes, block_bytes, scratch_bytes=0):
    est = 2 * (2 * block_bytes + scratch_bytes)
    lim = int(min(VMEM_LIMIT_MAX, max(VMEM_LIMIT_MIN, est)))
    return pltpu.CompilerParams(dimension_semantics=("arbitrary",) * n_axes, vmem_limit_bytes=lim)


def _bf(v):
    return v.astype(BF16)


def _dot(a, b):
    return jnp.dot(a, b, preferred_element_type=F32)


def _dot_nt(a, b):
    return lax.dot_general(a, b, (((1,), (1,)), ((), ())), preferred_element_type=F32)


def _dot_tn(a, b):
    return lax.dot_general(a, b, (((0,), (0,)), ((), ())), preferred_element_type=F32)


def _rstd(v):
    return lax.rsqrt(jnp.mean(v * v, axis=-1, keepdims=True) + EPS)


def _norm_bwd(dy, xv, g):
    r = _rstd(xv)
    xh = xv * r
    dxh = dy * g
    dx = r * (dxh - xh * jnp.mean(dxh * xh, axis=-1, keepdims=True))
    return dx, xh


def _place():
    x, y, c = lax.axis_index("x"), lax.axis_index("y"), lax.axis_index("c")
    other_chips = [(1 - x, y), (x, 1 - y), (1 - x, 1 - y)]
    return x, y, c, other_chips


class _Plan:
    ins = ()
    out_shapes = ()
    aliases = {}
    sem_shapes = ()

    def copies(self, ins, outs, sems):
        raise NotImplementedError

    def begin(self, ins, outs, sems):
        for cp in self.copies(ins, outs, sems):
            cp.start()

    def finish(self, ins, outs, sems):
        for cp in self.copies(ins, outs, sems):
            cp.wait()


class _GatherPlan(_Plan):
    def __init__(self, bufs):
        n = len(bufs)
        self.ins = list(bufs)
        self.out_shapes = [_sds(b.shape, b.dtype) for b in bufs]
        self.aliases = {i: i for i in range(n)}
        self.sem_shapes = [pltpu.SemaphoreType.DMA((n, 6)), pltpu.SemaphoreType.DMA((n, 6))]

    def _parts(self, outs, sems):
        send, recv = sems
        x, y, c, chips = _place()

        def half(i, piece, which):
            hr = outs[i].shape[1] // 2
            return outs[i].at[piece, pl.ds(which * hr, hr), :]

        def copy(i, k, ref, to):
            return pltpu.make_async_remote_copy(src_ref=ref, dst_ref=ref, send_sem=send.at[i, k], recv_sem=recv.at[i, k],
                                                device_id=to, device_id_type=MESH)

        return x, y, c, chips, half, copy

    def begin(self, ins, outs, sems):
        x, y, c, chips, half, copy = self._parts(outs, sems)
        for i in range(len(outs)):
            for j, (cx, cy) in enumerate(chips):
                copy(i, j, half(i, 2 * x + y, c), (cx, cy, c)).start()

    def finish(self, ins, outs, sems):
        x, y, c, chips, half, copy = self._parts(outs, sems)
        sib = (x, y, 1 - c)
        n = len(outs)
        for i in range(n):
            for j, (cx, cy) in enumerate(chips):
                piece = half(i, 2 * cx + cy, c)
                copy(i, j, piece, (cx, cy, c)).wait_recv()
                copy(i, 3 + j, piece, sib).start()
        for i in range(n):
            for j, (cx, cy) in enumerate(chips):
                copy(i, 3 + j, half(i, 2 * cx + cy, 1 - c), sib).wait_recv()
        for i in range(n):
            for j, (cx, cy) in enumerate(chips):
                copy(i, j, half(i, 2 * x + y, c), (cx, cy, c)).wait_send()
                copy(i, 3 + j, half(i, 2 * cx + cy, c), sib).wait_send()


class _SwapPlan(_Plan):
    def __init__(self, grads):
        n = len(grads)
        self.ins = list(grads)
        self.out_shapes = [_sds((g.shape[0], g.shape[1] // 2, g.shape[2]), g.dtype) for g in grads]
        self.sem_shapes = [pltpu.SemaphoreType.DMA((n,)), pltpu.SemaphoreType.DMA((n,))]

    def copies(self, ins, outs, sems):
        send, recv = sems
        x, y, c, _ = _place()
        cps = []
        for i in range(len(ins)):
            hr = ins[i].shape[1] // 2
            cps.append(pltpu.make_async_remote_copy(
                src_ref=ins[i].at[:, pl.ds((1 - c) * hr, hr), :], dst_ref=outs[i], send_sem=send.at[i],
                recv_sem=recv.at[i], device_id=(x, y, 1 - c), device_id_type=MESH))
        return cps


class _ExchangePlan(_Plan):
    def __init__(self, sums):
        n = len(sums)
        self.ins = list(sums)
        self.out_shapes = [_sds((N_CHIPS - 1,) + s.shape[1:], s.dtype) for s in sums]
        self.sem_shapes = [pltpu.SemaphoreType.DMA((n, 3)), pltpu.SemaphoreType.DMA((n, 3))]

    def copies(self, ins, outs, sems):
        send, recv = sems
        x, y, c, chips = _place()
        cps = []
        for i in range(len(ins)):
            for j, (cx, cy) in enumerate(chips):
                cps.append(pltpu.make_async_remote_copy(
                    src_ref=ins[i].at[2 * cx + cy], dst_ref=outs[i].at[j], send_sem=send.at[i, j],
                    recv_sem=recv.at[i, j], device_id=(cx, cy, c), device_id_type=MESH))
        return cps


class _JoinPlan(_Plan):
    def __init__(self, shards):
        n = len(shards)
        self.ins = list(shards)
        self.out_shapes = [_sds(s.shape, s.dtype) for s in shards]
        self.aliases = {i: i for i in range(n)}
        self.sem_shapes = [pltpu.SemaphoreType.DMA((n,)), pltpu.SemaphoreType.DMA((n,))]

    def copies(self, ins, outs, sems):
        send, recv = sems
        x, y, c, _ = _place()
        cps = []
        for i in range(len(outs)):
            hr = outs[i].shape[0] // 2
            mine = outs[i].at[pl.ds(c * hr, hr), :]
            cps.append(pltpu.make_async_remote_copy(src_ref=mine, dst_ref=mine, send_sem=send.at[i], recv_sem=recv.at[i],
                                                    device_id=(x, y, 1 - c), device_id_type=MESH))
        return cps


def _run_plan(name, plan):
    n_in, n_out = len(plan.ins), len(plan.out_shapes)

    def body(*refs):
        ins, outs, sems = refs[:n_in], refs[n_in:n_in + n_out], refs[n_in + n_out:]
        plan.begin(ins, outs, sems)
        plan.finish(ins, outs, sems)

    return pl.pallas_call(
        body, name=name, in_specs=[ANY] * n_in, out_specs=[ANY] * n_out, out_shape=list(plan.out_shapes),
        input_output_aliases=dict(plan.aliases), scratch_shapes=list(plan.sem_shapes))(*plan.ins)


HBM_PIN_BYTES = 1 << 20


def _sds(shape, dtype):
    if _nbytes(shape, dtype) >= HBM_PIN_BYTES:
        return pltpu.HBM(shape, dtype)
    return jax.ShapeDtypeStruct(shape, dtype)


def _pin(args):
    return [pltpu.with_memory_space_constraint(a, pltpu.HBM) if _nbytes(a.shape, a.dtype) >= HBM_PIN_BYTES else a
            for a in args]


def _pcall(body, *, name, grid, in_specs, out_specs, out_shape, params, args, scratch_shapes=(), plan=None):
    in_specs, out_specs, out_shape, scratch = list(in_specs), list(out_specs), list(out_shape), list(scratch_shapes)
    args = _pin(args)
    if plan is None:
        res = pl.pallas_call(body, name=name, grid=grid, in_specs=in_specs, out_specs=out_specs, out_shape=out_shape,
                             scratch_shapes=scratch, compiler_params=params)(*args)
        return list(res), []
    ni, no, ns = len(in_specs), len(out_specs), len(scratch)
    pi, po = len(plan.ins), len(plan.out_shapes)

    def wrapped(*refs):
        ins, pins = refs[:ni], refs[ni:ni + pi]
        outs, pouts = refs[ni + pi:ni + pi + no], refs[ni + pi + no:ni + pi + no + po]
        scr, psems = refs[ni + pi + no + po:ni + pi + no + po + ns], refs[ni + pi + no + po + ns:]
        ids = [pl.program_id(a) for a in range(len(grid))]
        first = functools.reduce(jnp.logical_and, [i == 0 for i in ids])
        last = functools.reduce(jnp.logical_and, [i == g - 1 for i, g in zip(ids, grid)])

        @pl.when(first)
        def _():
            plan.begin(pins, pouts, psems)

        body(*ins, *outs, *scr)

        @pl.when(last)
        def _():
            plan.finish(pins, pouts, psems)

    res = pl.pallas_call(
        wrapped, name=name, grid=grid, in_specs=in_specs + [ANY] * pi, out_specs=out_specs + [ANY] * po,
        out_shape=out_shape + list(plan.out_shapes), scratch_shapes=scratch + list(plan.sem_shapes),
        input_output_aliases={ni + a: no + b for a, b in plan.aliases.items()},
        compiler_params=params)(*args, *plan.ins)
    return list(res[:no]), list(res[no:])


def _mm_nn(name, a, w3, *, tm, norm_g=None, residual=None, out_dtype=F32, plan=None):
    M, K = a.shape
    P, Kw, C = w3.shape
    assert Kw == K and M % tm == 0
    N = P * C
    has_norm = norm_g is not None
    has_res = residual is not None

    def body(*refs):
        refs = list(refs)
        a_ref = refs.pop(0)
        w_ref = refs.pop(0)
        g_ref = refs.pop(0) if has_norm else None
        r_ref = refs.pop(0) if has_res else None
        o_ref = refs.pop(0)
        xn_ref = refs.pop(0) if has_norm else None
        if has_norm:
            av = a_ref[...].astype(F32)
            ab = _bf(av * _rstd(av) * g_ref[...])
            xn_ref[...] = ab
        else:
            ab = _bf(a_ref[...])
        for p in range(P):
            acc = _dot(ab, w_ref[p])
            if has_res:
                acc = acc + r_ref[:, p * C:(p + 1) * C]
            o_ref[:, p * C:(p + 1) * C] = acc.astype(o_ref.dtype)

    in_specs = [pl.BlockSpec((tm, K), lambda i: (i, 0)), pl.BlockSpec((P, K, C), lambda i: (0, 0, 0))]
    args = [a, w3]
    if has_norm:
        in_specs.append(pl.BlockSpec((1, K), lambda i: (0, 0)))
        args.append(norm_g)
    if has_res:
        in_specs.append(pl.BlockSpec((tm, N), lambda i: (i, 0)))
        args.append(residual)
    out_shape = [_sds((M, N), out_dtype)]
    out_specs = [pl.BlockSpec((tm, N), lambda i: (i, 0))]
    if has_norm:
        out_shape.append(_sds((M, K), BF16))
        out_specs.append(pl.BlockSpec((tm, K), lambda i: (i, 0)))
    blk = (_nbytes((tm, K), a.dtype) + _nbytes((P, K, C), BF16) + 2 * _nbytes((tm, N), F32)
           + _nbytes((tm, K), BF16))
    res, got = _pcall(body, name=name, grid=(M // tm,), in_specs=in_specs, out_specs=out_specs, out_shape=out_shape,
                      params=_cparams(1, blk), args=args, plan=plan)
    res = res if has_norm else res[0]
    return res if plan is None else (res, got)


MXU_COLS_V7X = 256


def _col_chunks(n):
    return [(c, min(c + MXU_COLS_V7X, n)) for c in range(0, n, MXU_COLS_V7X)]


def _ffn_up(name, h, wgu3, g, *, tm):
    M, K = h.shape
    P, _, C = wgu3.shape
    half = P // 2

    def body(h_ref, wg_ref, wu_ref, g_ref, act_ref, fac_ref, xn_ref, ab_ref):
        @pl.when(pl.program_id(1) == 0)
        def _():
            hv = h_ref[...]
            xn = _bf(hv * _rstd(hv) * g_ref[...])
            ab_ref[...] = xn
            xn_ref[...] = xn
        ab = ab_ref[...]
        for c0, c1 in _col_chunks(C):
            gate = _dot(ab, wg_ref[:, c0:c1])
            up = _dot(ab, wu_ref[:, c0:c1])
            sig = 1.0 / (1.0 + jnp.exp(-gate))
            silu = gate * sig
            act_ref[:, c0:c1] = _bf(silu * up)
            fac_ref[0, :, c0:c1] = _bf(up * (sig + silu * (1.0 - sig)))
            fac_ref[1, :, c0:c1] = _bf(silu)

    blk = (_nbytes((tm, K), F32) + 2 * _nbytes((K, C), BF16) + 3 * _nbytes((tm, C), BF16)
           + 2 * _nbytes((tm, C), F32) + _nbytes((tm, K), BF16))
    return pl.pallas_call(
        body, name=name, grid=(M // tm, half),
        in_specs=[pl.BlockSpec((tm, K), lambda i, j: (i, 0)),
                  pl.BlockSpec((None, K, C), lambda i, j: (j, 0, 0)),
                  pl.BlockSpec((None, K, C), lambda i, j: (j + half, 0, 0)),
                  pl.BlockSpec((1, K), lambda i, j: (0, 0))],
        out_specs=[pl.BlockSpec((tm, C), lambda i, j: (i, j)),
                   pl.BlockSpec((2, tm, C), lambda i, j: (0, i, j)),
                   pl.BlockSpec((tm, K), lambda i, j: (i, 0))],
        out_shape=[_sds((M, half * C), BF16),
                   _sds((2, M, half * C), BF16),
                   _sds((M, K), BF16)],
        scratch_shapes=[pltpu.VMEM((tm, K), BF16)],
        compiler_params=_cparams(2, blk, _nbytes((tm, K), BF16)))(*_pin([h, wgu3, wgu3, g]))


def _ffn_down_loss(name, act, wd3, h, target, *, tm):
    M, K = act.shape
    _, _, N = wd3.shape

    def body(a_ref, w_ref, h_ref, t_ref, dy_ref, ls_ref):
        err = _dot(a_ref[...], w_ref[...]) + h_ref[...] - t_ref[...]
        dy_ref[...] = err * (1.0 / N)

        @pl.when(pl.program_id(0) == 0)
        def _():
            ls_ref[...] = jnp.zeros_like(ls_ref)
        ls_ref[...] += jnp.sum(err * err, axis=0, keepdims=True)

    blk = _nbytes((tm, K), BF16) + _nbytes((K, N), BF16) + 4 * _nbytes((tm, N), F32)
    return pl.pallas_call(
        body, name=name, grid=(M // tm,),
        in_specs=[pl.BlockSpec((tm, K), lambda i: (i, 0)),
                  pl.BlockSpec((None, K, N), lambda i: (0, 0, 0)),
                  pl.BlockSpec((tm, N), lambda i: (i, 0)),
                  pl.BlockSpec((tm, N), lambda i: (i, 0))],
        out_specs=[pl.BlockSpec((tm, N), lambda i: (i, 0)),
                   pl.BlockSpec((1, N), lambda i: (0, 0))],
        out_shape=[_sds((M, N), F32), _sds((1, N), F32)],
        compiler_params=_cparams(1, blk))(*_pin([act, wd3, h, target]))


def _nt_pieces(a_ref, w_ref):
    P, _, C = w_ref.shape
    acc = _dot_nt(_bf(a_ref[:, 0:C]), w_ref[0])
    for p in range(1, P):
        acc = acc + _dot_nt(_bf(a_ref[:, p * C:(p + 1) * C]), w_ref[p])
    return acc


def _mm_nt(name, a, w3, *, tm, out_dtype):
    M, N = a.shape
    P, Ko, C = w3.shape
    assert N == P * C and M % tm == 0

    def body(a_ref, w_ref, o_ref):
        o_ref[...] = _nt_pieces(a_ref, w_ref).astype(o_ref.dtype)

    blk = _nbytes((tm, N), a.dtype) + _nbytes((P, Ko, C), BF16) + 2 * _nbytes((tm, Ko), F32)
    return pl.pallas_call(
        body, name=name, grid=(M // tm,),
        in_specs=[pl.BlockSpec((tm, N), lambda i: (i, 0)), pl.BlockSpec((P, Ko, C), lambda i: (0, 0, 0))],
        out_specs=pl.BlockSpec((tm, Ko), lambda i: (i, 0)),
        out_shape=_sds((M, Ko), out_dtype),
        compiler_params=_cparams(1, blk))(*_pin([a, w3]))


def _mm_nt_normbwd(name, a, w3, h, g, dres, *, tm, plan=None):
    M, D = h.shape
    P, Ko, C = w3.shape
    N = a.shape[1]
    assert N == P * C and Ko == D and M % tm == 0
    has_res = dres is not None

    def body(*refs):
        a_ref, w_ref, h_ref, g_ref = refs[:4]
        r_ref = refs[4] if has_res else None
        dx_ref, dg_ref = refs[-2:]
        dhn = _nt_pieces(a_ref, w_ref)
        dx, xh = _norm_bwd(dhn, h_ref[...], g_ref[...])
        if has_res:
            dx = dx + r_ref[...]
        dx_ref[...] = dx

        @pl.when(pl.program_id(0) == 0)
        def _():
            dg_ref[...] = jnp.zeros_like(dg_ref)
        dg_ref[...] += jnp.sum(dhn * xh, axis=0, keepdims=True)

    row = pl.BlockSpec((tm, D), lambda i: (i, 0))
    vec = pl.BlockSpec((1, D), lambda i: (0, 0))
    in_specs = [pl.BlockSpec((tm, N), lambda i: (i, 0)), pl.BlockSpec((P, Ko, C), lambda i: (0, 0, 0)), row, vec]
    args = [a, w3, h, g]
    if has_res:
        in_specs.append(row)
        args.append(dres)
    blk = _nbytes((tm, N), a.dtype) + _nbytes((P, Ko, C), BF16) + 5 * _nbytes((tm, D), F32)
    res, got = _pcall(body, name=name, grid=(M // tm,), in_specs=in_specs, out_specs=[row, vec],
                      out_shape=[_sds((M, D), F32), _sds((1, D), F32)],
                      params=_cparams(1, blk), args=args, plan=plan)
    return res if plan is None else (res, got)


def _ffn_bwd(name, dy, wd3, fac, wgu3, h, g, *, tm):
    M, D = h.shape
    P, _, C = wgu3.shape
    F = fac.shape[2]
    per = F // C

    def body(dy_ref, wd_ref, f_ref, w_ref, h_ref, g_ref, dgu_ref, dx_ref, dg_ref):
        dyv = dy_ref[...]
        ab = _bf(dyv)
        dhn = None
        for pc in range(per):
            cols = slice(pc * C, (pc + 1) * C)
            da = _dot_nt(ab, wd_ref[cols, :])
            dgate = _bf(da * f_ref[0, :, cols].astype(F32))
            dup = _bf(da * f_ref[1, :, cols].astype(F32))
            dgu_ref[0, :, cols] = dgate
            dgu_ref[1, :, cols] = dup
            part = _dot_nt(dgate, w_ref[pc]) + _dot_nt(dup, w_ref[per + pc])
            dhn = part if dhn is None else dhn + part
        dx, xh = _norm_bwd(dhn, h_ref[...], g_ref[...])
        dx_ref[...] = dx + dyv

        @pl.when(pl.program_id(0) == 0)
        def _():
            dg_ref[...] = jnp.zeros_like(dg_ref)
        dg_ref[...] += jnp.sum(dhn * xh, axis=0, keepdims=True)

    row = pl.BlockSpec((tm, D), lambda i: (i, 0))
    vec = pl.BlockSpec((1, D), lambda i: (0, 0))
    planes = pl.BlockSpec((2, tm, F), lambda i: (0, i, 0))
    blk = (2 * _nbytes((2, tm, F), BF16) + (_nbytes((F, D), BF16) + _nbytes((P, D, C), BF16)) // 2
           + 4 * _nbytes((tm, D), F32))
    return pl.pallas_call(
        body, name=name, grid=(M // tm,),
        in_specs=[row, pl.BlockSpec((None, F, D), lambda i: (0, 0, 0), pipeline_mode=pl.Buffered(1)), planes,
                  pl.BlockSpec((P, D, C), lambda i: (0, 0, 0), pipeline_mode=pl.Buffered(1)), row, vec],
        out_specs=[planes, row, vec],
        out_shape=[_sds((2, M, F), BF16), _sds((M, D), F32), _sds((1, D), F32)],
        compiler_params=_cparams(1, blk))(*_pin([dy, wd3, fac, wgu3, h, g]))


def _mm_tn_wide(name, a, b, pieces, *, ts, plan=None):
    S, K = a.shape
    N = b.shape[1]
    C = N // pieces

    def body(a_ref, b_ref, o_ref):
        @pl.when(pl.program_id(0) == 0)
        def _():
            o_ref[...] = jnp.zeros_like(o_ref)
        acc = _dot_tn(_bf(a_ref[...]), _bf(b_ref[...]))
        for p in range(pieces):
            o_ref[p] += acc[:, p * C:(p + 1) * C]

    blk = _nbytes((ts, K), a.dtype) + _nbytes((ts, N), b.dtype) + 2 * _nbytes((K, N), F32)
    res, got = _pcall(body, name=name, grid=(S // ts,),
                      in_specs=[pl.BlockSpec((ts, K), lambda s: (s, 0)), pl.BlockSpec((ts, N), lambda s: (s, 0))],
                      out_specs=[pl.BlockSpec((pieces, K, C), lambda s: (0, 0, 0))],
                      out_shape=[_sds((pieces, K, C), F32)], params=_cparams(1, blk), args=[a, b], plan=plan)
    return res[0] if plan is None else (res[0], got)


def _mm_tn(name, a, a_spec, a_blk, b, b_spec, b_blk, out3, o_spec, o_blk, grid, plan=None):
    def body(a_ref, b_ref, o_ref):
        @pl.when(pl.program_id(2) == 0)
        def _():
            o_ref[...] = jnp.zeros_like(o_ref)
        o_ref[...] += _dot_tn(_bf(a_ref[...]), _bf(b_ref[...]))

    blk = _nbytes(a_blk, a.dtype) + _nbytes(b_blk, b.dtype) + 2 * _nbytes(o_blk, F32)
    res, got = _pcall(body, name=name, grid=grid, in_specs=[a_spec, b_spec], out_specs=[o_spec],
                      out_shape=[_sds(out3, F32)], params=_cparams(3, blk), args=[a, b], plan=plan)
    return res[0] if plan is None else (res[0], got)


def _swap_halves(v, lane, masked):
    up = pltpu.roll(v, HEAD - ROT_HALF, 1)
    down = pltpu.roll(v, ROT_HALF, 1)
    rest = jnp.where(lane < ROT_DIM, down, 0.0) if masked else down
    return jnp.where(lane < ROT_HALF, up, rest)


def _qk_prep(name, proj, ctab, stab, qg, kg, *, tm, plan=None):
    S = proj.shape[0]
    width = Q_W + KV_W

    def body(p_ref, c_ref, s_ref, qg_ref, kg_ref, q_ref, k_ref):
        lane = lax.broadcasted_iota(jnp.int32, (tm, HEAD), 1)
        cv = c_ref[...]
        sv = s_ref[...]

        def prep(t, g, scale):
            n = t * _rstd(t) * g
            return (n * cv + _swap_halves(n, lane, False) * sv) * scale

        for j in range(Q_W // HEAD):
            q_ref[:, j * HEAD:(j + 1) * HEAD] = prep(p_ref[:, j * HEAD:(j + 1) * HEAD], qg_ref[...], SCALE)
        for j in range(KV_W // HEAD):
            k_ref[:, j * HEAD:(j + 1) * HEAD] = prep(p_ref[:, Q_W + j * HEAD:Q_W + (j + 1) * HEAD], kg_ref[...], 1.0)

    blk = 2 * _nbytes((tm, width), F32) + 2 * _nbytes((tm, HEAD), F32)
    res, got = _pcall(
        body, name=name, grid=(S // tm,),
        in_specs=[pl.BlockSpec((tm, width), lambda i: (i, 0)),
                  pl.BlockSpec((tm, HEAD), lambda i: (i, 0)),
                  pl.BlockSpec((tm, HEAD), lambda i: (i, 0)),
                  pl.BlockSpec((1, HEAD), lambda i: (0, 0)),
                  pl.BlockSpec((1, HEAD), lambda i: (0, 0))],
        out_specs=[pl.BlockSpec((tm, Q_W), lambda i: (i, 0)), pl.BlockSpec((tm, KV_W), lambda i: (i, 0))],
        out_shape=[_sds((S, Q_W), F32), _sds((S, KV_W), F32)],
        params=_cparams(1, blk), args=[proj, ctab, stab, qg, kg], plan=plan)
    return res if plan is None else (res, got)


ATTN_STEP = 2048


def _row_idx(start, dil):
    return pl.ds(start, BLOCK) if dil == 1 else pl.ds(start, BLOCK, stride=dil)


def _rows(ref, start, dil):
    return ref[_row_idx(start, dil), :]


def _set_rows(ref, start, dil, val):
    ref[_row_idx(start, dil), :] = val


def _band_mask(first):
    qi = lax.broadcasted_iota(jnp.int32, (BLOCK, 2 * BLOCK), 0)
    kj = lax.broadcasted_iota(jnp.int32, (BLOCK, 2 * BLOCK), 1)
    band = (kj >= qi) & (kj <= qi + BLOCK)
    if first is None:
        return band
    return band & ((kj >= BLOCK) | jnp.logical_not(first))


def _attn_geometry(S, grp):
    dil = DILATIONS[grp]
    bt = BLOCK * dil
    assert S % ATTN_STEP == 0 and ATTN_STEP % bt == 0
    return dil, bt, ATTN_STEP // bt, S // ATTN_STEP


MERGE_ROWS = 256


def _attn_fwd(name, qn, kn, proj, plan=None):
    S = qn.shape[0]
    groups = range(len(DILATIONS))
    geo = [_attn_geometry(S, g) for g in groups]
    nsb = geo[0][3]
    vcol = (Q_W + KV_W) // HEAD

    def body(q0, q1, q2, kc_ref, vc_ref, kp0, kp1, kp2, vp0, vp1, vp2, mix_ref, lse_ref, o_s, l_s):
        q_refs, kp_refs, vp_refs = (q0, q1, q2), (kp0, kp1, kp2), (vp0, vp1, vp2)
        valid_first = _band_mask(pl.program_id(1) == 0)
        valid_inner = _band_mask(None)
        for g in groups:
            dil, bt, nsub, _ = geo[g]
            o_g, l_g = o_s.at[g], l_s.at[g]
            for r in range(dil):
                kprev, vprev = _bf(_rows(kp_refs[g], r, dil)), _bf(_rows(vp_refs[g], r, dil))
                for b in range(nsub):
                    at = b * bt + r
                    kcur, vcur = _bf(_rows(kc_ref, at, dil)), _bf(_rows(vc_ref, at, dil))
                    q = _bf(_rows(q_refs[g], at, dil))
                    k2 = jnp.concatenate([kprev, kcur], axis=0)
                    v2 = jnp.concatenate([vprev, vcur], axis=0)
                    s = jnp.where(valid_first if b == 0 else valid_inner, _dot_nt(q, k2), NEG_INF)
                    m = jnp.max(s, axis=-1, keepdims=True)
                    p = jnp.exp(s - m)
                    l = jnp.sum(p, axis=-1, keepdims=True)
                    acc = _dot(_bf(p), v2)
                    _set_rows(o_g, at, dil, acc / l)
                    _set_rows(l_g, at, dil, jnp.broadcast_to(m + jnp.log(l), (BLOCK, HEAD)))
                    kprev, vprev = kcur, vcur
        for c in range(ATTN_STEP // MERGE_ROWS):
            rows = pl.ds(c * MERGE_ROWS, MERGE_ROWS)
            ls = [l_s[g, rows, :] for g in groups]
            m = jnp.maximum(jnp.maximum(ls[0], ls[1]), ls[2])
            ws = [jnp.exp(v - m) for v in ls]
            den = ws[0] + ws[1] + ws[2]
            num = ws[0] * o_s[0, rows, :] + ws[1] * o_s[1, rows, :] + ws[2] * o_s[2, rows, :]
            mix_ref[rows, :] = _bf(num / den)
            lse_ref[rows, :] = m + jnp.log(den)

    def big(col):
        return pl.BlockSpec((ATTN_STEP, HEAD), lambda h, n: (n, col + h))

    def tail(g, col):
        _, bt, nsub, _ = geo[g]
        return pl.BlockSpec((bt, HEAD), lambda h, n: (jnp.maximum(n * nsub - 1, 0), col + h))

    blk = 7 * _nbytes((ATTN_STEP, HEAD), F32) + 2 * sum(_nbytes((geo[g][1], HEAD), F32) for g in groups)
    scratch = 2 * len(DILATIONS) * _nbytes((ATTN_STEP, HEAD), F32)
    res, got = _pcall(
        body, name=name, grid=(N_HEADS, nsb),
        in_specs=[big(g * N_HEADS) for g in groups] + [big(0), big(vcol)]
                 + [tail(g, 0) for g in groups] + [tail(g, vcol) for g in groups],
        out_specs=[big(0), big(0)],
        out_shape=[_sds((S, 2 * KV_W), BF16), _sds((S, KV_W), F32)],
        scratch_shapes=[pltpu.VMEM((len(DILATIONS), ATTN_STEP, HEAD), F32)] * 2,
        params=_cparams(2, blk, scratch), args=[qn, qn, qn, kn, proj, kn, kn, kn, proj, proj, proj], plan=plan)
    return res if plan is None else (res, got)


def _attn_bwd(name, qn, kn, proj, dattn, lse, delta, grp, plan=None):
    S = qn.shape[0]
    dil, bt, nsub, nsb = _attn_geometry(S, grp)
    vcol = (Q_W + KV_W) // HEAD

    def slot(b, r):
        return pl.ds((b * dil + r) * BLOCK, BLOCK)

    def body(q_ref, kp_ref, kc_ref, vp_ref, vc_ref, do_ref, l_ref, d_ref, dq_ref, dk_ref, dv_ref, ck_ref, cv_ref):
        n = pl.program_id(1)
        par = n % 2

        @pl.when(n < nsb)
        def _():
            valid_first = _band_mask(n == 0)
            valid_inner = _band_mask(None)
            ck, cv = ck_ref.at[par], cv_ref.at[par]
            pk, pv = ck_ref.at[1 - par], cv_ref.at[1 - par]
            for r in range(dil):
                kprev, vprev = _bf(_rows(kp_ref, r, dil)), _bf(_rows(vp_ref, r, dil))
                own_k = own_v = None
                for b in range(nsub):
                    at = b * bt + r
                    kcur, vcur = _bf(_rows(kc_ref, at, dil)), _bf(_rows(vc_ref, at, dil))
                    q = _bf(_rows(q_ref, at, dil))
                    k2 = jnp.concatenate([kprev, kcur], axis=0)
                    v2 = jnp.concatenate([vprev, vcur], axis=0)
                    do = _bf(_rows(do_ref, at, dil))
                    lse_r = _rows(l_ref, at, dil)[:, :1]
                    del_r = _rows(d_ref, at, dil)[:, :1]
                    s = jnp.where(valid_first if b == 0 else valid_inner, _dot_nt(q, k2), NEG_INF)
                    p = jnp.exp(s - lse_r)
                    ds = _bf(p * (_dot_nt(do, v2) - del_r))
                    _set_rows(dq_ref, at, dil, _dot(ds, k2))
                    dk2 = _dot_tn(ds, q)
                    dv2 = _dot_tn(_bf(p), do)
                    if b > 0:
                        ck[slot(b - 1, r), :] = own_k + dk2[:BLOCK]
                        cv[slot(b - 1, r), :] = own_v + dv2[:BLOCK]
                    else:
                        @pl.when(n > 0)
                        def _():
                            pk[slot(nsub - 1, r), :] += dk2[:BLOCK]
                            pv[slot(nsub - 1, r), :] += dv2[:BLOCK]
                    own_k, own_v = dk2[BLOCK:], dv2[BLOCK:]
                    kprev, vprev = kcur, vcur
                ck[slot(nsub - 1, r), :] = own_k
                cv[slot(nsub - 1, r), :] = own_v

        @pl.when(n > 0)
        def _():
            for r in range(dil):
                for b in range(nsub):
                    _set_rows(dk_ref, b * bt + r, dil, ck_ref[1 - par, slot(b, r), :])
                    _set_rows(dv_ref, b * bt + r, dil, cv_ref[1 - par, slot(b, r), :])

    cur = lambda n: jnp.minimum(n, nsb - 1)
    prev = lambda n: jnp.maximum(n - 1, 0)
    tail_at = lambda n: jnp.maximum(cur(n) * nsub - 1, 0)
    big = lambda col: pl.BlockSpec((ATTN_STEP, HEAD), lambda h, n: (cur(n), col(h)))
    tail = lambda col: pl.BlockSpec((bt, HEAD), lambda h, n: (tail_at(n), col(h)))
    late = pl.BlockSpec((ATTN_STEP, HEAD), lambda h, n: (prev(n), h))
    blk = 9 * _nbytes((ATTN_STEP, HEAD), F32) + 2 * _nbytes((bt, HEAD), F32)
    res, got = _pcall(
        body, name=name, grid=(N_HEADS, nsb + 1),
        in_specs=[big(lambda h: grp * N_HEADS + h), tail(lambda h: h), big(lambda h: h),
                  tail(lambda h: vcol + h), big(lambda h: vcol + h),
                  big(lambda h: h), big(lambda h: h), big(lambda h: h)],
        out_specs=[big(lambda h: h), late, late],
        out_shape=[_sds((S, KV_W), F32)] * 3,
        scratch_shapes=[pltpu.VMEM((2, ATTN_STEP, HEAD), F32), pltpu.VMEM((2, ATTN_STEP, HEAD), F32)],
        params=_cparams(2, blk, 4 * _nbytes((ATTN_STEP, HEAD), F32)),
        args=[qn, kn, kn, proj, proj, dattn, lse, delta], plan=plan)
    return res if plan is None else (res, got)


def _window_sum(v, n_doublings, back):
    rows = v.shape[0]
    step = 1
    for _ in range(n_doublings):
        v = v + pltpu.roll(v, step if back else rows - step, 0)
        step *= 2
    return v


def _pool_fwd(name, mix, proj, pool_w, pool_scale, *, tm):
    S = proj.shape[0]
    ucol = (IN_W - POOL_W) // POOL_W
    hpt = tm // POOL_HALO

    def body(mix_in, u_ref, uh_ref, pw_ref, ps_ref, pooled_ref, dp_ref):
        i = pl.program_id(0)
        halo = jnp.where(i == 0, 0.0, uh_ref[...])
        t = lax.broadcasted_iota(jnp.int32, (tm + POOL_HALO, HEAD), 0) + (i * tm - POOL_HALO)
        for g, w in enumerate(POOL_WINDOWS):
            sl = slice(g * HEAD, (g + 1) * HEAD)
            ub = jnp.concatenate([halo[:, sl], u_ref[:, sl]], axis=0)
            cnt = jnp.minimum(t + 1, w).astype(F32)
            d = (_window_sum(ub, g + 1, True) / cnt - ub)[POOL_HALO:]
            db = _bf(d)
            dp_ref[:, sl] = db
            pooled_ref[:, sl] = _bf(_dot(db, pw_ref[g]) * ps_ref[:, sl])

    blk = 3 * _nbytes((tm, POOL_W), F32) + 2 * _nbytes((tm, POOL_W), BF16)
    return pl.pallas_call(
        body, name=name, grid=(S // tm,),
        in_specs=[ANY,
                  pl.BlockSpec((tm, POOL_W), lambda i: (i, ucol)),
                  pl.BlockSpec((POOL_HALO, POOL_W), lambda i: (jnp.maximum(i * hpt - 1, 0), ucol)),
                  pl.BlockSpec((len(POOL_WINDOWS), HEAD, HEAD), lambda i: (0, 0, 0)),
                  pl.BlockSpec((1, POOL_W), lambda i: (0, 0))],
        out_specs=[pl.BlockSpec((tm, POOL_W), lambda i: (i, 1)), pl.BlockSpec((tm, POOL_W), lambda i: (i, 0))],
        out_shape=[_sds((S, 2 * KV_W), BF16), _sds((S, POOL_W), BF16)],
        input_output_aliases={0: 0},
        compiler_params=_cparams(1, blk))(*_pin([mix, proj, proj, pool_w, pool_scale]))


def _mix_bwd(name, dmix, mix, dpool, pool_w, pool_scale, *, tm, plan=None):
    S = dmix.shape[0]
    hpt = tm // POOL_HALO
    last_halo = S // POOL_HALO - 1
    n_tiles = S // tm

    def body(dm_ref, dh_ref, at_ref, dp_ref, pw_ref, ps_ref, da_ref, dl_ref, du_ref, gw_ref, gs_ref):
        i = pl.program_id(0)

        @pl.when(i == 0)
        def _():
            gw_ref[...] = jnp.zeros_like(gw_ref)
            gs_ref[...] = jnp.zeros_like(gs_ref)

        for h in range(N_HEADS):
            sl = slice(h * HEAD, (h + 1) * HEAD)
            da = dm_ref[:, sl]
            da_ref[:, sl] = da
            dl_ref[:, sl] = jnp.broadcast_to(
                jnp.sum(da * at_ref[:, sl].astype(F32), axis=-1, keepdims=True), (tm, HEAD))
        halo = jnp.where(i == n_tiles - 1, 0.0, dh_ref[...])
        t = lax.broadcasted_iota(jnp.int32, (tm + POOL_HALO, HEAD), 0) + i * tm
        for g, w in enumerate(POOL_WINDOWS):
            sl = slice(g * HEAD, (g + 1) * HEAD)
            dy = jnp.concatenate([dm_ref[:, KV_W + g * HEAD:KV_W + (g + 1) * HEAD], halo[:, sl]], axis=0)
            dys = _bf(dy * ps_ref[:, sl])
            dd = _dot_nt(dys, pw_ref[g])
            cnt = jnp.minimum(t + 1, w).astype(F32)
            du_ref[:, sl] = (_window_sum(dd / cnt, g + 1, False) - dd)[:tm]
            db = dp_ref[:, sl]
            gw_ref[g] += _dot_tn(db, dys[:tm])
            gs_ref[:, sl] += jnp.sum(dy[:tm] * _dot(db, pw_ref[g]), axis=0, keepdims=True)

    tile = pl.BlockSpec((tm, KV_W), lambda i: (i, 0))
    blk = _nbytes((tm, 2 * KV_W), F32) + 6 * _nbytes((tm, KV_W), F32)
    res, got = _pcall(
        body, name=name, grid=(n_tiles,),
        in_specs=[pl.BlockSpec((tm, 2 * KV_W), lambda i: (i, 0)),
                  pl.BlockSpec((POOL_HALO, POOL_W), lambda i: (jnp.minimum((i + 1) * hpt, last_halo), 1)),
                  tile, tile,
                  pl.BlockSpec((len(POOL_WINDOWS), HEAD, HEAD), lambda i: (0, 0, 0)),
                  pl.BlockSpec((1, POOL_W), lambda i: (0, 0))],
        out_specs=[tile, tile, tile,
                   pl.BlockSpec((len(POOL_WINDOWS), HEAD, HEAD), lambda i: (0, 0, 0)),
                   pl.BlockSpec((1, POOL_W), lambda i: (0, 0))],
        out_shape=[_sds((S, KV_W), F32)] * 3 + [
            _sds((len(POOL_WINDOWS), HEAD, HEAD), F32), _sds((1, POOL_W), F32)],
        params=_cparams(1, blk), args=[dmix, dmix, mix, dpool, pool_w, pool_scale], plan=plan)
    return res if plan is None else (res, got)


def _qkv_bwd(name, dqs, dks, dvs, du, proj, ctab, stab, qg, kg, *, tm, plan=None):
    S = proj.shape[0]
    width = Q_W + KV_W

    def body(dq0, dq1, dq2, dk0, dk1, dk2, dv0, dv1, dv2, du_ref, p_ref, c_ref, s_ref, qg_ref, kg_ref,
             dp_ref, gq_ref, gk_ref):
        @pl.when(pl.program_id(0) == 0)
        def _():
            gq_ref[...] = jnp.zeros_like(gq_ref)
            gk_ref[...] = jnp.zeros_like(gk_ref)

        lane = lax.broadcasted_iota(jnp.int32, (tm, HEAD), 1)
        cv = c_ref[...]
        sv = s_ref[...]

        def back(dy, t, g, scale):
            dy = dy * scale
            dn = dy * cv + _swap_halves(dy * sv, lane, True)
            dt, xh = _norm_bwd(dn, t, g)
            return dt, jnp.sum(dn * xh, axis=0, keepdims=True)

        dqr = (dq0, dq1, dq2)
        gq = jnp.zeros((1, HEAD), F32)
        for j in range(Q_W // HEAD):
            grp, h = divmod(j, N_HEADS)
            dt, gj = back(dqr[grp][:, h * HEAD:(h + 1) * HEAD], p_ref[:, j * HEAD:(j + 1) * HEAD], qg_ref[...], SCALE)
            dp_ref[:, j * HEAD:(j + 1) * HEAD] = _bf(dt)
            gq = gq + gj
        gq_ref[...] += gq
        gk = jnp.zeros((1, HEAD), F32)
        for h in range(N_HEADS):
            sl = slice(h * HEAD, (h + 1) * HEAD)
            dt, gj = back(dk0[:, sl] + dk1[:, sl] + dk2[:, sl], p_ref[:, Q_W + h * HEAD:Q_W + (h + 1) * HEAD],
                          kg_ref[...], 1.0)
            dp_ref[:, Q_W + h * HEAD:Q_W + (h + 1) * HEAD] = _bf(dt)
            gk = gk + gj
        gk_ref[...] += gk
        dp_ref[:, width:width + KV_W] = _bf(dv0[...] + dv1[...] + dv2[...])
        dp_ref[:, width + KV_W:] = _bf(du_ref[...])

    tile = pl.BlockSpec((tm, KV_W), lambda i: (i, 0))
    vec = pl.BlockSpec((1, HEAD), lambda i: (0, 0))
    rot = pl.BlockSpec((tm, HEAD), lambda i: (i, 0))
    blk = 10 * _nbytes((tm, KV_W), F32) + _nbytes((tm, width), F32) + _nbytes((tm, IN_W), BF16)
    res, got = _pcall(
        body, name=name, grid=(S // tm,),
        in_specs=[tile] * 10 + [pl.BlockSpec((tm, width), lambda i: (i, 0)), rot, rot, vec, vec],
        out_specs=[pl.BlockSpec((tm, IN_W), lambda i: (i, 0)), vec, vec],
        out_shape=[_sds((S, IN_W), BF16), _sds((1, HEAD), F32),
                   _sds((1, HEAD), F32)],
        params=_cparams(1, blk), args=[*dqs, *dks, *dvs, du, proj, ctab, stab, qg, kg], plan=plan)
    return res if plan is None else (res, got)


def _cross_heads(q_ref, kv_ref, qg, kg, h):
    sl = slice(h * HEAD, (h + 1) * HEAD)
    qr = q_ref[:, sl]
    kr = kv_ref[:, sl]
    qh = qr * _rstd(qr) * qg * SCALE
    kh = kr * _rstd(kr) * kg
    vh = kv_ref[:, X_W + h * HEAD:X_W + (h + 1) * HEAD]
    return qr, _bf(qh), _bf(kh), _bf(vh)


def _cross_fwd(name, qraw, kv, qg, kg, *, tm):
    S = qraw.shape[0]
    M = kv.shape[0]

    def body(q_ref, kv_ref, qg_ref, kg_ref, o_ref):
        for h in range(N_HEADS):
            _, qh, kh, vh = _cross_heads(q_ref, kv_ref, qg_ref[...], kg_ref[...], h)
            s = _dot_nt(qh, kh)
            p = jnp.exp(s - jnp.max(s, axis=-1, keepdims=True))
            l = jnp.sum(p, axis=-1, keepdims=True)
            o_ref[:, h * HEAD:(h + 1) * HEAD] = _bf(_dot(_bf(p), vh) / l)

    vec = pl.BlockSpec((1, HEAD), lambda i: (0, 0))
    blk = 2 * _nbytes((tm, X_W), F32) + _nbytes((M, 2 * X_W), F32) + 4 * _nbytes((tm, M), F32)
    return pl.pallas_call(
        body, name=name, grid=(S // tm,),
        in_specs=[pl.BlockSpec((tm, X_W), lambda i: (i, 0)), pl.BlockSpec((M, 2 * X_W), lambda i: (0, 0)), vec, vec],
        out_specs=pl.BlockSpec((tm, X_W), lambda i: (i, 0)),
        out_shape=_sds((S, X_W), BF16),
        compiler_params=_cparams(1, blk))(*_pin([qraw, kv, qg, kg]))


def _cross_bwd(name, do, qraw, kv, qg, kg, *, tm, plan=None):
    S = qraw.shape[0]
    M = kv.shape[0]

    def body(do_ref, q_ref, kv_ref, qg_ref, kg_ref, dq_ref, dk_ref, dv_ref, gq_ref):
        @pl.when(pl.program_id(0) == 0)
        def _():
            dk_ref[...] = jnp.zeros_like(dk_ref)
            dv_ref[...] = jnp.zeros_like(dv_ref)
            gq_ref[...] = jnp.zeros_like(gq_ref)

        gq = jnp.zeros((1, HEAD), F32)
        for h in range(N_HEADS):
            sl = slice(h * HEAD, (h + 1) * HEAD)
            qr, qh, kh, vh = _cross_heads(q_ref, kv_ref, qg_ref[...], kg_ref[...], h)
            doh = _bf(do_ref[:, sl])
            s = _dot_nt(qh, kh)
            p = jnp.exp(s - jnp.max(s, axis=-1, keepdims=True))
            p = p / jnp.sum(p, axis=-1, keepdims=True)
            pb = _bf(p)
            dp = _dot_nt(doh, vh)
            ds = _bf(p * (dp - jnp.sum(dp * p, axis=-1, keepdims=True)))
            dv_ref[:, sl] += _dot_tn(pb, doh)
            dk_ref[:, sl] += _dot_tn(ds, qh)
            dn = _dot(ds, kh) * SCALE
            dt, xh = _norm_bwd(dn, qr, qg_ref[...])
            dq_ref[:, sl] = _bf(dt)
            gq = gq + jnp.sum(dn * xh, axis=0, keepdims=True)
        gq_ref[...] += gq

    vec = pl.BlockSpec((1, HEAD), lambda i: (0, 0))
    acc = pl.BlockSpec((M, X_W), lambda i: (0, 0))
    blk = 3 * _nbytes((tm, X_W), F32) + 3 * _nbytes((M, 2 * X_W), F32) + 6 * _nbytes((tm, M), F32)
    res, got = _pcall(
        body, name=name, grid=(S // tm,),
        in_specs=[pl.BlockSpec((tm, X_W), lambda i: (i, 0)), pl.BlockSpec((tm, X_W), lambda i: (i, 0)),
                  pl.BlockSpec((M, 2 * X_W), lambda i: (0, 0)), vec, vec],
        out_specs=[pl.BlockSpec((tm, X_W), lambda i: (i, 0)), acc, acc, vec],
        out_shape=[_sds((S, X_W), BF16), _sds((M, X_W), F32),
                   _sds((M, X_W), F32), _sds((1, HEAD), F32)],
        params=_cparams(1, blk), args=[do, qraw, kv, qg, kg], plan=plan)
    return res if plan is None else (res, got)


def _cross_kv_bwd(name, dkn, dv, kv, kg):
    M = kv.shape[0]

    def body(dk_ref, dv_ref, kv_ref, kg_ref, o_ref, g_ref):
        gk = jnp.zeros((1, HEAD), F32)
        for h in range(N_HEADS):
            sl = slice(h * HEAD, (h + 1) * HEAD)
            dn = dk_ref[:, sl]
            dt, xh = _norm_bwd(dn, kv_ref[:, sl], kg_ref[...])
            o_ref[:, sl] = _bf(dt)
            gk = gk + jnp.sum(dn * xh, axis=0, keepdims=True)
        o_ref[:, X_W:] = _bf(dv_ref[...])
        g_ref[...] = gk

    full = lambda shape: pl.BlockSpec(shape, lambda i: (0,) * len(shape))
    return pl.pallas_call(
        body, name=name, grid=(1,),
        in_specs=[full((M, X_W)), full((M, X_W)), full((M, 2 * X_W)), full((1, HEAD))],
        out_specs=[full((M, 2 * X_W)), full((1, HEAD))],
        out_shape=[_sds((M, 2 * X_W), BF16), _sds((1, HEAD), F32)],
        compiler_params=_cparams(1, 6 * _nbytes((M, 2 * X_W), F32)))(dkn, dv, kv, kg)


def _rope_tables(positions):
    inv_freq = ROPE_THETA ** (-jnp.arange(0, ROT_DIM, 2, dtype=F32) / ROT_DIM)
    ang = positions.astype(F32)[:, None] * inv_freq
    cos, sin = jnp.cos(ang), jnp.sin(ang)
    S = positions.shape[0]
    ctab = jnp.concatenate([cos, cos, jnp.ones((S, HEAD - ROT_DIM), F32)], axis=-1)
    stab = jnp.concatenate([-sin, sin, jnp.zeros((S, HEAD - ROT_DIM), F32)], axis=-1)
    return ctab, stab


GATHER_BEHIND_IN_PROJ = ("w_out", "w_cq", "w_ckv", "w_co")
FFN_WEIGHTS = ("w_gate_up", "w_down")


def _hosted(fn, *args, plan=None, **kw):
    if plan is None:
        return fn(*args, **kw), []
    return fn(*args, plan=plan, **kw)


def _local_step(x, mem, positions, target, wb, sm, *, tm=512, place=None):
    S, D = x.shape
    M = mem.shape[0]
    dist = place is not None
    wb = dict(wb)
    ctab, stab = _rope_tables(positions)
    pool_w_b = _bf(sm["pool_w"])

    def gather(names):
        return _GatherPlan([wb[k] for k in names]) if dist else None

    if dist:
        wb["w_in"], = _run_plan("gather_w_in", gather(["w_in"]))
    w_in = wb["w_in"]
    (proj, xn1), got = _hosted(_mm_nn, "in_proj", x, w_in, tm=tm, norm_g=sm["mix_norm_g"],
                               plan=gather(GATHER_BEHIND_IN_PROJ))
    wb.update(zip(GATHER_BEHIND_IN_PROJ, got))
    (qn, kn), got = _hosted(_qk_prep, "qk_prep", proj, ctab, stab, sm["q_norm_g"], sm["k_norm_g"], tm=tm,
                            plan=gather(["w_gate_up"]))
    wb.update(zip(["w_gate_up"], got))
    (mix, lse), got = _hosted(_attn_fwd, "attn_fwd", qn, kn, proj, plan=gather(["w_down"]))
    wb.update(zip(["w_down"], got))
    w_out = wb["w_out"].reshape(1, 2 * KV_W, D)
    w_cq = wb["w_cq"].reshape(1, D, X_W)
    w_ckv = wb["w_ckv"].reshape(1, D, 2 * X_W)
    w_co = wb["w_co"]
    w_gu = wb["w_gate_up"]
    w_down = wb["w_down"].reshape(1, D_FF, D)
    cin = w_in.shape[2]
    cco = w_co.shape[2]
    cgu = w_gu.shape[2]
    mix, dpool = _pool_fwd("pool_fwd", mix, proj, pool_w_b, sm["pool_scale"], tm=tm)
    h1 = _mm_nn("out_proj", mix, w_out, tm=tm, residual=x)
    cq_raw, hn2 = _mm_nn("cq_proj", h1, w_cq, tm=tm, norm_g=sm["cross_norm_g"])
    kv, mem_n = _mm_nn("ckv_proj", mem, w_ckv, tm=M, norm_g=sm["mem_norm_g"])
    xo = _cross_fwd("cross_fwd", cq_raw, kv, sm["cq_norm_g"], sm["ck_norm_g"], tm=tm)
    h2 = _mm_nn("co_proj", xo, w_co, tm=tm, residual=h1)
    act, gu, hn3 = _ffn_up("ffn_up", h2, w_gu, sm["ffn_norm_g"], tm=tm)
    dy, lsum = _ffn_down_loss("ffn_down_loss", act, w_down, h2, target, tm=tm)
    loss = 0.5 * jnp.sum(lsum) / D

    nS = S // tm
    g_down = _mm_tn("g_w_down", act, pl.BlockSpec((tm, cgu), lambda r, c, s: (s, r)), (tm, cgu),
                    dy, pl.BlockSpec((tm, D), lambda r, c, s: (s, 0)), (tm, D),
                    (1, D_FF, D), pl.BlockSpec((None, cgu, D), lambda r, c, s: (0, r, 0)), (cgu, D),
                    (D_FF // cgu, 1, nS))
    dgu, dh2, g_ffn_norm = _ffn_bwd("ffn_bwd", dy, w_down, gu, w_gu, h2, sm["ffn_norm_g"], tm=tm // 2)
    g_gu = _mm_tn("g_w_gate_up", hn3, pl.BlockSpec((tm, D), lambda r, c, s: (s, 0)), (tm, D),
                  dgu, pl.BlockSpec((None, tm, cgu), lambda r, c, s: (c // 2, s, c % 2)), (tm, cgu),
                  (4, D, cgu), pl.BlockSpec((None, D, cgu), lambda r, c, s: (c, 0, 0)), (D, cgu),
                  (1, 4, nS))

    dxo = _mm_nt("co_proj_bwd", dh2, w_co, tm=tm, out_dtype=BF16)
    g_co = _mm_tn_wide("g_w_co", xo, dh2, N_CHIPS, ts=tm)
    full = {"w_gate_up": g_gu, "w_down": g_down.reshape(N_CHIPS, D_FF // N_CHIPS, D)}
    sums = {}

    def swap(names):
        return _SwapPlan([full[k] for k in names]) if dist else None

    def add_halves(names, from_sibling):
        for k, t in zip(names, from_sibling):
            sums[k] = _add_core_halves(f"add_halves_{k}", full[k], t, place[0])

    def exchange(names):
        return _ExchangePlan([sums[k][1] for k in names]) if dist else None

    def sum_chips(names, from_chips):
        return [_sum_chips(f"sum_chips_{k}", sums[k][0], t, place[1]) for k, t in zip(names, from_chips)]

    (dcq, dkn, dvm, g_cq_norm), got = _hosted(_cross_bwd, "cross_bwd", dxo, cq_raw, kv, sm["cq_norm_g"],
                                              sm["ck_norm_g"], tm=tm, plan=swap(FFN_WEIGHTS))
    add_halves(FFN_WEIGHTS, got)
    dkv, g_ck_norm = _cross_kv_bwd("cross_kv_bwd", dkn, dvm, kv, sm["ck_norm_g"])
    dh1, g_cross_norm = _mm_nt_normbwd("cq_proj_bwd", dcq, w_cq, h1, sm["cross_norm_g"], dh2, tm=tm)
    g_cq = _mm_tn("g_w_cq", hn2, pl.BlockSpec((tm, D), lambda r, c, s: (s, 0)), (tm, D),
                  dcq, pl.BlockSpec((tm, X_W), lambda r, c, s: (s, 0)), (tm, X_W),
                  (1, D, X_W), pl.BlockSpec((None, D, X_W), lambda r, c, s: (0, 0, 0)), (D, X_W), (1, 1, nS))
    _, g_mem_norm = _mm_nt_normbwd("ckv_proj_bwd", dkv, w_ckv, mem, sm["mem_norm_g"], None, tm=M)
    g_ckv = _mm_tn("g_w_ckv", mem_n, pl.BlockSpec((M, D), lambda r, c, s: (0, 0)), (M, D),
                   dkv, pl.BlockSpec((M, 2 * X_W), lambda r, c, s: (0, 0)), (M, 2 * X_W),
                   (1, D, 2 * X_W), pl.BlockSpec((None, D, 2 * X_W), lambda r, c, s: (0, 0, 0)), (D, 2 * X_W),
                   (1, 1, 1))

    dmix = _mm_nt("out_proj_bwd", dh1, w_out, tm=tm, out_dtype=F32)
    g_out = _mm_tn("g_w_out", mix, pl.BlockSpec((tm, 2 * KV_W), lambda r, c, s: (s, 0)), (tm, 2 * KV_W),
                   dh1, pl.BlockSpec((tm, D), lambda r, c, s: (s, 0)), (tm, D),
                   (1, 2 * KV_W, D), pl.BlockSpec((None, 2 * KV_W, D), lambda r, c, s: (0, 0, 0)), (2 * KV_W, D),
                   (1, 1, nS))
    full.update({
        "w_out": g_out.reshape(N_CHIPS, 2 * KV_W // N_CHIPS, D),
        "w_cq": g_cq.reshape(N_CHIPS, D // N_CHIPS, X_W),
        "w_ckv": g_ckv.reshape(N_CHIPS, D // N_CHIPS, 2 * X_W),
        "w_co": g_co,
    })
    mixer = GATHER_BEHIND_IN_PROJ
    (dattn, delta, du, g_pool_w, g_pool_scale), got = _hosted(
        _mix_bwd, "mix_bwd", dmix, mix, dpool, pool_w_b, sm["pool_scale"], tm=tm, plan=swap(mixer))
    add_halves(mixer, got)
    behind_attn = (None, mixer, FFN_WEIGHTS)
    halves = {}
    dqs, dks, dvs = [], [], []
    for grp in range(len(DILATIONS)):
        names = behind_attn[grp]
        (dq, dk, dv), got = _hosted(_attn_bwd, f"attn_bwd{grp}", qn, kn, proj, dattn, lse, delta, grp,
                                    plan=exchange(names) if names else None)
        if dist and names:
            halves.update(zip(names, sum_chips(names, got)))
        dqs.append(dq)
        dks.append(dk)
        dvs.append(dv)
    joined = mixer + FFN_WEIGHTS
    (dproj, g_q_norm, g_k_norm), got = _hosted(
        _qkv_bwd, "qkv_bwd", dqs, dks, dvs, du, proj, ctab, stab, sm["q_norm_g"], sm["k_norm_g"], tm=tm,
        plan=_JoinPlan([halves[k] for k in joined]) if dist else None)
    shards = dict(zip(joined, got))
    dx, g_mix_norm = _mm_nt_normbwd("in_proj_bwd", dproj, w_in, x, sm["mix_norm_g"], dh1, tm=tm)
    small = {
        "mix_norm_g": g_mix_norm, "q_norm_g": g_q_norm, "k_norm_g": g_k_norm, "pool_w": g_pool_w,
        "pool_scale": g_pool_scale, "cross_norm_g": g_cross_norm, "mem_norm_g": g_mem_norm,
        "cq_norm_g": g_cq_norm, "ck_norm_g": g_ck_norm, "ffn_norm_g": g_ffn_norm,
    }
    full["w_in"], got = _hosted(_mm_tn_wide, "g_w_in", xn1, dproj, N_CHIPS, ts=tm,
                                plan=_AllPushPlan(_pack_small(small)) if dist else None)
    if dist:
        small = _unpack_small(_sum_slots("sum_small", got[0]), sm)
        add_halves(["w_in"], _run_plan("swap_w_in", swap(["w_in"])))
        half, = sum_chips(["w_in"], _run_plan("exchange_w_in", exchange(["w_in"])))
        shards["w_in"], = _run_plan("join_w_in", _JoinPlan([half]))
    big = shards if dist else full
    return loss, dx, big, small


BIG = ("w_in", "w_out", "w_cq", "w_ckv", "w_co", "w_gate_up", "w_down")
SMALL = ("mix_norm_g", "q_norm_g", "k_norm_g", "pool_w", "pool_scale", "cross_norm_g", "mem_norm_g",
         "cq_norm_g", "ck_norm_g", "ffn_norm_g")
WEIGHTS = ("mix_norm_g", "w_in", "q_norm_g", "k_norm_g", "pool_w", "pool_scale", "w_out", "cross_norm_g",
           "mem_norm_g", "w_cq", "w_ckv", "cq_norm_g", "ck_norm_g", "w_co", "ffn_norm_g", "w_gate_up", "w_down")


def _cast_piece(name, w, k_arr):
    R, C = w.shape
    hr = R // 2

    def body(k_ref, w_ref, o_ref):
        o_ref[...] = _bf(w_ref[...])

    return pl.pallas_call(
        body, name=name,
        grid_spec=pltpu.PrefetchScalarGridSpec(
            num_scalar_prefetch=1, grid=(2,),
            in_specs=[pl.BlockSpec((hr, C), lambda i, k: (i, 0))],
            out_specs=pl.BlockSpec((None, hr, C), lambda i, k: (k[0], i, 0))),
        out_shape=_sds((N_CHIPS, R, C), BF16),
        compiler_params=_cparams(1, 2 * _nbytes((hr, C), F32)))(k_arr, *_pin([w]))


def _add_core_halves(name, g, t, c_arr):
    P, R, C = g.shape
    hr = R // 2

    def body(c_ref, g_ref, t_ref, o_ref, ob_ref):
        tot = g_ref[...] + t_ref[...]
        o_ref[...] = tot
        ob_ref[...] = _bf(tot)

    piece = pl.BlockSpec((None, hr, C), lambda p, c: (p, 0, 0))
    return pl.pallas_call(
        body, name=name,
        grid_spec=pltpu.PrefetchScalarGridSpec(
            num_scalar_prefetch=1, grid=(P,),
            in_specs=[pl.BlockSpec((None, hr, C), lambda p, c: (p, c[0], 0)), piece],
            out_specs=[piece, piece]),
        out_shape=[_sds((P, hr, C), F32), _sds((P, hr, C), BF16)],
        compiler_params=_cparams(1, 4 * _nbytes((hr, C), F32)))(c_arr, *_pin([g, t]))


def _sum_chips(name, own, got, kc_arr):
    P, hr, C = own.shape

    def body(kc_ref, o_ref, g_ref, r_ref):
        r_ref[...] = ((o_ref[...] + g_ref[0].astype(F32)) + g_ref[1].astype(F32)) + g_ref[2].astype(F32)

    return pl.pallas_call(
        body, name=name,
        grid_spec=pltpu.PrefetchScalarGridSpec(
            num_scalar_prefetch=1, grid=(1,),
            in_specs=[pl.BlockSpec((None, hr, C), lambda i, kc: (kc[0], 0, 0)),
                      pl.BlockSpec((N_CHIPS - 1, hr, C), lambda i, kc: (0, 0, 0))],
            out_specs=pl.BlockSpec((hr, C), lambda i, kc: (kc[1], 0))),
        out_shape=_sds((2 * hr, C), F32),
        compiler_params=_cparams(1, 5 * _nbytes((hr, C), F32)))(kc_arr, *_pin([own, got]))


N_DEV = 8


class _AllPushPlan(_Plan):
    def __init__(self, v):
        self.ins = [v]
        self.out_shapes = [jax.ShapeDtypeStruct((N_DEV,) + v.shape, v.dtype)]
        self.sem_shapes = [pltpu.SemaphoreType.DMA((N_DEV - 1,)), pltpu.SemaphoreType.DMA((N_DEV - 1,)),
                           pltpu.SemaphoreType.DMA]

    def copies(self, ins, outs, sems):
        send, recv, own = sems
        x, y, c, _ = _place()
        slot = outs[0].at[4 * x + 2 * y + c]
        cps = [pltpu.make_async_copy(ins[0], slot, own)]
        flips = [(dx, dy, dc) for dx in (0, 1) for dy in (0, 1) for dc in (0, 1)][1:]
        for q, (dx, dy, dc) in enumerate(flips):
            to = (x + dx - 2 * x * dx, y + dy - 2 * y * dy, c + dc - 2 * c * dc)
            cps.append(pltpu.make_async_remote_copy(src_ref=ins[0], dst_ref=slot, send_sem=send.at[q],
                                                    recv_sem=recv.at[q], device_id=to, device_id_type=MESH))
        return cps


def _sum_slots(name, slots):
    n, R, C = slots.shape

    def body(s_ref, o_ref):
        acc = s_ref[0]
        for d in range(1, n):
            acc = acc + s_ref[d]
        o_ref[...] = acc

    return pl.pallas_call(
        body, name=name, grid=(1,),
        in_specs=[pl.BlockSpec((n, R, C), lambda i: (0, 0, 0))], out_specs=pl.BlockSpec((R, C), lambda i: (0, 0)),
        out_shape=_sds((R, C), F32),
        compiler_params=_cparams(1, _nbytes((n + 1, R, C), F32)))(*_pin([slots]))


def _adamw(name, w, g, m, v, *, tr):
    R, C = w.shape

    def body(w_ref, g_ref, m_ref, v_ref, d_ref, nm_ref, nv_ref):
        gv = g_ref[...]
        nm = ADAM_B1 * m_ref[...] + (1.0 - ADAM_B1) * gv
        nv = ADAM_B2 * v_ref[...] + (1.0 - ADAM_B2) * (gv * gv)
        m_hat = nm / (1.0 - ADAM_B1 ** ADAM_STEP)
        v_hat = nv / (1.0 - ADAM_B2 ** ADAM_STEP)
        d_ref[...] = -ADAM_LR * (m_hat / (jnp.sqrt(v_hat) + ADAM_EPS) + ADAM_WD * w_ref[...])
        nm_ref[...] = nm
        nv_ref[...] = nv

    tile = pl.BlockSpec((tr, C), lambda i: (i, 0))
    return pl.pallas_call(
        body, name=name, grid=(R // tr,), in_specs=[tile] * 4, out_specs=[tile] * 3,
        out_shape=[_sds((R, C), F32)] * 3,
        compiler_params=_cparams(1, 7 * _nbytes((tr, C), F32)))(*_pin([w, g, m, v]))


def _pack_small(d):
    parts = []
    for name in SMALL:
        a = d[name].reshape(-1, HEAD)
        pad = (-a.shape[0]) % 8
        parts.append(jnp.pad(a, ((0, pad), (0, 0))))
    return jnp.concatenate(parts, axis=0)


def _unpack_small(packed, like):
    out = {}
    row = 0
    for name in SMALL:
        shape = like[name].shape
        rows = like[name].size // HEAD
        out[name] = packed[row:row + rows].reshape(shape)
        row += rows + (-rows) % 8
    return out


def kernel(x, mem, positions, mix_norm_g, w_in, q_norm_g, k_norm_g, pool_w, pool_scale, w_out, cross_norm_g, mem_norm_g, w_cq, w_ckv, cq_norm_g, ck_norm_g, w_co, ffn_norm_g, w_gate_up, w_down, loss_target, m_mix_norm_g, m_w_in, m_q_norm_g, m_k_norm_g, m_pool_w, m_pool_scale, m_w_out, m_cross_norm_g, m_mem_norm_g, m_w_cq, m_w_ckv, m_cq_norm_g, m_ck_norm_g, m_w_co, m_ffn_norm_g, m_w_gate_up, m_w_down, v_mix_norm_g, v_w_in, v_q_norm_g, v_k_norm_g, v_pool_w, v_pool_scale, v_w_out, v_cross_norm_g, v_mem_norm_g, v_w_cq, v_w_ckv, v_cq_norm_g, v_ck_norm_g, v_w_co, v_ffn_norm_g, v_w_gate_up, v_w_down):
    w = dict(mix_norm_g=mix_norm_g, w_in=w_in, q_norm_g=q_norm_g, k_norm_g=k_norm_g, pool_w=pool_w,
             pool_scale=pool_scale, w_out=w_out, cross_norm_g=cross_norm_g, mem_norm_g=mem_norm_g, w_cq=w_cq,
             w_ckv=w_ckv, cq_norm_g=cq_norm_g, ck_norm_g=ck_norm_g, w_co=w_co, ffn_norm_g=ffn_norm_g,
             w_gate_up=w_gate_up, w_down=w_down)
    m = dict(mix_norm_g=m_mix_norm_g, w_in=m_w_in, q_norm_g=m_q_norm_g, k_norm_g=m_k_norm_g, pool_w=m_pool_w,
             pool_scale=m_pool_scale, w_out=m_w_out, cross_norm_g=m_cross_norm_g, mem_norm_g=m_mem_norm_g,
             w_cq=m_w_cq, w_ckv=m_w_ckv, cq_norm_g=m_cq_norm_g, ck_norm_g=m_ck_norm_g, w_co=m_w_co,
             ffn_norm_g=m_ffn_norm_g, w_gate_up=m_w_gate_up, w_down=m_w_down)
    v = dict(mix_norm_g=v_mix_norm_g, w_in=v_w_in, q_norm_g=v_q_norm_g, k_norm_g=v_k_norm_g, pool_w=v_pool_w,
             pool_scale=v_pool_scale, w_out=v_w_out, cross_norm_g=v_cross_norm_g, mem_norm_g=v_mem_norm_g,
             w_cq=v_w_cq, w_ckv=v_w_ckv, cq_norm_g=v_cq_norm_g, ck_norm_g=v_ck_norm_g, w_co=v_w_co,
             ffn_norm_g=v_ffn_norm_g, w_gate_up=v_w_gate_up, w_down=v_w_down)

    c_arr = lax.axis_index("c").astype(jnp.int32).reshape(1)
    k_arr = (2 * lax.axis_index("x") + lax.axis_index("y")).astype(jnp.int32).reshape(1)
    kc_arr = jnp.concatenate([k_arr, c_arr])
    wb = {k: _cast_piece(f"cast_{k}", w[k][0], k_arr) for k in BIG}
    sm = {k: (w[k][0] if k == "pool_w" else w[k]) for k in SMALL}
    loss_part, dx, gshard, gsm = _local_step(x[0], mem[0], positions[0], loss_target[0], wb, sm,
                                             place=(c_arr, kc_arr))
    loss = lax.psum(loss_part, ("x", "y", "c"))

    grads, deltas, new_m, new_v = {}, {}, {}, {}
    for k in BIG:
        shard = w[k][0]
        tr = shard.shape[0] // 4
        d, nm, nv = _adamw(f"adamw_{k}", shard, gshard[k], m[k][0], v[k][0], tr=tr)
        grads[k], deltas[k], new_m[k], new_v[k] = gshard[k][None], d[None], nm[None], nv[None]
    smw = {k: (w[k][0] if k == "pool_w" else w[k]) for k in SMALL}
    smm = {k: (m[k][0] if k == "pool_w" else m[k]) for k in SMALL}
    smv = {k: (v[k][0] if k == "pool_w" else v[k]) for k in SMALL}
    pw, pg, pm, pv = _pack_small(smw), _pack_small(gsm), _pack_small(smm), _pack_small(smv)
    d, nm, nv = _adamw("adamw_small", pw, pg, pm, pv, tr=pw.shape[0])
    for dst, packed in ((deltas, d), (new_m, nm), (new_v, nv)):
        un = _unpack_small(packed, sm)
        for k in SMALL:
            dst[k] = un[k].reshape(w[k].shape)
    for k in SMALL:
        grads[k] = gsm[k].reshape(w[k].shape)

    return (loss, dx[None], *[grads[k] for k in WEIGHTS], *[deltas[k] for k in WEIGHTS],
            *[new_m[k] for k in WEIGHTS], *[new_v[k] for k in WEIGHTS])
```

```python
import functools

import jax
import jax.numpy as jnp
from jax import lax
from jax.experimental import pallas as pl
from jax.experimental.pallas import tpu as pltpu

F32 = jnp.float32
BF16 = jnp.bfloat16
MESH = pl.DeviceIdType.MESH
ANY = pl.BlockSpec(memory_space=pl.ANY)

D_MODEL = 1024
HEAD = 128
N_HEADS = 4
DILATIONS = (1, 4, 16)
BLOCK = 128
Q_W = 1536
KV_W = 512
POOL_W = 512
POOL_WINDOWS = (2, 4, 8, 16)
POOL_HALO = 16
IN_W = 3072
ROT_DIM = 32
ROT_HALF = 16
ROPE_THETA = 500000.0
X_W = 512
D_FF = 2816
EPS = 1e-6
NEG_INF = -1e30
SCALE = HEAD ** -0.5
N_CHIPS = 4

ADAM_LR = 0.001
ADAM_B1 = 0.9
ADAM_B2 = 0.999
ADAM_EPS = 1e-08
ADAM_WD = 0.01
ADAM_STEP = 10

VMEM_BYTES_V7X = 64 * 2 ** 20
VMEM_LIMIT_MAX = 56 * 2 ** 20
VMEM_LIMIT_MIN = 24 * 2 ** 20


def _nbytes(shape, dtype):
    n = 1
    for s in shape:
        n *= s
    return n * jnp.dtype(dtype).itemsize


def _cparams(n_axes, block_bytes, scratch_bytes=0):
    est = 2 * (2 * block_bytes + scratch_bytes)
    lim = int(min(VMEM_LIMIT_MAX, max(VMEM_LIMIT_MIN, est)))
    return pltpu.CompilerParams(dimension_semantics=("arbitrary",) * n_axes, vmem_limit_bytes=lim)


def _bf(v):
    return v.astype(BF16)


def _dot(a, b):
    return jnp.dot(a, b, preferred_element_type=F32)


def _dot_nt(a, b):
    return lax.dot_general(a, b, (((1,), (1,)), ((), ())), preferred_element_type=F32)


def _dot_tn(a, b):
    return lax.dot_general(a, b, (((0,), (0,)), ((), ())), preferred_element_type=F32)


def _rstd(v):
    return lax.rsqrt(jnp.mean(v * v, axis=-1, keepdims=True) + EPS)


def _norm_bwd(dy, xv, g):
    r = _rstd(xv)
    xh = xv * r
    dxh = dy * g
    dx = r * (dxh - xh * jnp.mean(dxh * xh, axis=-1, keepdims=True))
    return dx, xh


def _place():
    x, y, c = lax.axis_index("x"), lax.axis_index("y"), lax.axis_index("c")
    other_chips = [(1 - x, y), (x, 1 - y), (1 - x, 1 - y)]
    return x, y, c, other_chips


class _Plan:
    ins = ()
    out_shapes = ()
    aliases = {}
    sem_shapes = ()

    def copies(self, ins, outs, sems):
        raise NotImplementedError

    def begin(self, ins, outs, sems):
        for cp in self.copies(ins, outs, sems):
            cp.start()

    def finish(self, ins, outs, sems):
        for cp in self.copies(ins, outs, sems):
            cp.wait()


class _GatherPlan(_Plan):
    def __init__(self, bufs):
        n = len(bufs)
        self.ins = list(bufs)
        self.out_shapes = [_sds(b.shape, b.dtype) for b in bufs]
        self.aliases = {i: i for i in range(n)}
        self.sem_shapes = [pltpu.SemaphoreType.DMA((n, 6)), pltpu.SemaphoreType.DMA((n, 6))]

    def _parts(self, outs, sems):
        send, recv = sems
        x, y, c, chips = _place()

        def half(i, piece, which):
            hr = outs[i].shape[1] // 2
            return outs[i].at[piece, pl.ds(which * hr, hr), :]

        def copy(i, k, ref, to):
            return pltpu.make_async_remote_copy(src_ref=ref, dst_ref=ref, send_sem=send.at[i, k], recv_sem=recv.at[i, k],
                                                device_id=to, device_id_type=MESH)

        return x, y, c, chips, half, copy

    def begin(self, ins, outs, sems):
        x, y, c, chips, half, copy = self._parts(outs, sems)
        for i in range(len(outs)):
            for j, (cx, cy) in enumerate(chips):
                copy(i, j, half(i, 2 * x + y, c), (cx, cy, c)).start()

    def finish(self, ins, outs, sems):
        x, y, c, chips, half, copy = self._parts(outs, sems)
        sib = (x, y, 1 - c)
        n = len(outs)
        for i in range(n):
            for j, (cx, cy) in enumerate(chips):
                piece = half(i, 2 * cx + cy, c)
                copy(i, j, piece, (cx, cy, c)).wait_recv()
                copy(i, 3 + j, piece, sib).start()
        for i in range(n):
            for j, (cx, cy) in enumerate(chips):
                copy(i, 3 + j, half(i, 2 * cx + cy, 1 - c), sib).wait_recv()
        for i in range(n):
            for j, (cx, cy) in enumerate(chips):
                copy(i, j, half(i, 2 * x + y, c), (cx, cy, c)).wait_send()
                copy(i, 3 + j, half(i, 2 * cx + cy, c), sib).wait_send()


class _SwapPlan(_Plan):
    def __init__(self, grads):
        n = len(grads)
        self.ins = list(grads)
        self.out_shapes = [_sds((g.shape[0], g.shape[1] // 2, g.shape[2]), g.dtype) for g in grads]
        self.sem_shapes = [pltpu.SemaphoreType.DMA((n,)), pltpu.SemaphoreType.DMA((n,))]

    def copies(self, ins, outs, sems):
        send, recv = sems
        x, y, c, _ = _place()
        cps = []
        for i in range(len(ins)):
            hr = ins[i].shape[1] // 2
            cps.append(pltpu.make_async_remote_copy(
                src_ref=ins[i].at[:, pl.ds((1 - c) * hr, hr), :], dst_ref=outs[i], send_sem=send.at[i],
                recv_sem=recv.at[i], device_id=(x, y, 1 - c), device_id_type=MESH))
        return cps


class _ExchangePlan(_Plan):
    def __init__(self, sums):
        n = len(sums)
        self.ins = list(sums)
        self.out_shapes = [_sds((N_CHIPS - 1,) + s.shape[1:], s.dtype) for s in sums]
        self.sem_shapes = [pltpu.SemaphoreType.DMA((n, 3)), pltpu.SemaphoreType.DMA((n, 3))]

    def copies(self, ins, outs, sems):
        send, recv = sems
        x, y, c, chips = _place()
        cps = []
        for i in range(len(ins)):
            for j, (cx, cy) in enumerate(chips):
                cps.append(pltpu.make_async_remote_copy(
                    src_ref=ins[i].at[2 * cx + cy], dst_ref=outs[i].at[j], send_sem=send.at[i, j],
                    recv_sem=recv.at[i, j], device_id=(cx, cy, c), device_id_type=MESH))
        return cps


class _JoinPlan(_Plan):
    def __init__(self, shards):
        n = len(shards)
        self.ins = list(shards)
        self.out_shapes = [_sds(s.shape, s.dtype) for s in shards]
        self.aliases = {i: i for i in range(n)}
        self.sem_shapes = [pltpu.SemaphoreType.DMA((n,)), pltpu.SemaphoreType.DMA((n,))]

    def copies(self, ins, outs, sems):
        send, recv = sems
        x, y, c, _ = _place()
        cps = []
        for i in range(len(outs)):
            hr = outs[i].shape[0] // 2
            mine = outs[i].at[pl.ds(c * hr, hr), :]
            cps.append(pltpu.make_async_remote_copy(src_ref=mine, dst_ref=mine, send_sem=send.at[i], recv_sem=recv.at[i],
                                                    device_id=(x, y, 1 - c), device_id_type=MESH))
        return cps


def _run_plan(name, plan):
    n_in, n_out = len(plan.ins), len(plan.out_shapes)

    def body(*refs):
        ins, outs, sems = refs[:n_in], refs[n_in:n_in + n_out], refs[n_in + n_out:]
        plan.begin(ins, outs, sems)
        plan.finish(ins, outs, sems)

    return pl.pallas_call(
        body, name=name, in_specs=[ANY] * n_in, out_specs=[ANY] * n_out, out_shape=list(plan.out_shapes),
        input_output_aliases=dict(plan.aliases), scratch_shapes=list(plan.sem_shapes))(*plan.ins)


HBM_PIN_BYTES = 1 << 20


def _sds(shape, dtype):
    if _nbytes(shape, dtype) >= HBM_PIN_BYTES:
        return pltpu.HBM(shape, dtype)
    return jax.ShapeDtypeStruct(shape, dtype)


def _pin(args):
    return [pltpu.with_memory_space_constraint(a, pltpu.HBM) if _nbytes(a.shape, a.dtype) >= HBM_PIN_BYTES else a
            for a in args]


def _pcall(body, *, name, grid, in_specs, out_specs, out_shape, params, args, scratch_shapes=(), plan=None):
    in_specs, out_specs, out_shape, scratch = list(in_specs), list(out_specs), list(out_shape), list(scratch_shapes)
    args = _pin(args)
    if plan is None:
        res = pl.pallas_call(body, name=name, grid=grid, in_specs=in_specs, out_specs=out_specs, out_shape=out_shape,
                             scratch_shapes=scratch, compiler_params=params)(*args)
        return list(res), []
    ni, no, ns = len(in_specs), len(out_specs), len(scratch)
    pi, po = len(plan.ins), len(plan.out_shapes)

    def wrapped(*refs):
        ins, pins = refs[:ni], refs[ni:ni + pi]
        outs, pouts = refs[ni + pi:ni + pi + no], refs[ni + pi + no:ni + pi + no + po]
        scr, psems = refs[ni + pi + no + po:ni + pi + no + po + ns], refs[ni + pi + no + po + ns:]
        ids = [pl.program_id(a) for a in range(len(grid))]
        first = functools.reduce(jnp.logical_and, [i == 0 for i in ids])
        last = functools.reduce(jnp.logical_and, [i == g - 1 for i, g in zip(ids, grid)])

        @pl.when(first)
        def _():
            plan.begin(pins, pouts, psems)

        body(*ins, *outs, *scr)

        @pl.when(last)
        def _():
            plan.finish(pins, pouts, psems)

    res = pl.pallas_call(
        wrapped, name=name, grid=grid, in_specs=in_specs + [ANY] * pi, out_specs=out_specs + [ANY] * po,
        out_shape=out_shape + list(plan.out_shapes), scratch_shapes=scratch + list(plan.sem_shapes),
        input_output_aliases={ni + a: no + b for a, b in plan.aliases.items()},
        compiler_params=params)(*args, *plan.ins)
    return list(res[:no]), list(res[no:])


def _mm_nn(name, a, w3, *, tm, norm_g=None, residual=None, out_dtype=F32, plan=None):
    M, K = a.shape
    P, Kw, C = w3.shape
    assert Kw == K and M % tm == 0
    N = P * C
    has_norm = norm_g is not None
    has_res = residual is not None

    def body(*refs):
        refs = list(refs)
        a_ref = refs.pop(0)
        w_ref = refs.pop(0)
        g_ref = refs.pop(0) if has_norm else None
        r_ref = refs.pop(0) if has_res else None
        o_ref = refs.pop(0)
        xn_ref = refs.pop(0) if has_norm else None
        if has_norm:
            av = a_ref[...].astype(F32)
            ab = _bf(av * _rstd(av) * g_ref[...])
            xn_ref[...] = ab
        else:
            ab = _bf(a_ref[...])
        for p in range(P):
            acc = _dot(ab, w_ref[p])
            if has_res:
                acc = acc + r_ref[:, p * C:(p + 1) * C]
            o_ref[:, p * C:(p + 1) * C] = acc.astype(o_ref.dtype)

    in_specs = [pl.BlockSpec((tm, K), lambda i: (i, 0)), pl.BlockSpec((P, K, C), lambda i: (0, 0, 0))]
    args = [a, w3]
    if has_norm:
        in_specs.append(pl.BlockSpec((1, K), lambda i: (0, 0)))
        args.append(norm_g)
    if has_res:
        in_specs.append(pl.BlockSpec((tm, N), lambda i: (i, 0)))
        args.append(residual)
    out_shape = [_sds((M, N), out_dtype)]
    out_specs = [pl.BlockSpec((tm, N), lambda i: (i, 0))]
    if has_norm:
        out_shape.append(_sds((M, K), BF16))
        out_specs.append(pl.BlockSpec((tm, K), lambda i: (i, 0)))
    blk = (_nbytes((tm, K), a.dtype) + _nbytes((P, K, C), BF16) + 2 * _nbytes((tm, N), F32)
           + _nbytes((tm, K), BF16))
    res, got = _pcall(body, name=name, grid=(M // tm,), in_specs=in_specs, out_specs=out_specs, out_shape=out_shape,
                      params=_cparams(1, blk), args=args, plan=plan)
    res = res if has_norm else res[0]
    return res if plan is None else (res, got)


MXU_COLS_V7X = 256


def _col_chunks(n):
    return [(c, min(c + MXU_COLS_V7X, n)) for c in range(0, n, MXU_COLS_V7X)]


def _ffn_up(name, h, wgu3, g, *, tm):
    M, K = h.shape
    P, _, C = wgu3.shape
    half = P // 2

    def body(h_ref, wg_ref, wu_ref, g_ref, act_ref, fac_ref, xn_ref, ab_ref):
        @pl.when(pl.program_id(1) == 0)
        def _():
            hv = h_ref[...]
            xn = _bf(hv * _rstd(hv) * g_ref[...])
            ab_ref[...] = xn
            xn_ref[...] = xn
        ab = ab_ref[...]
        for c0, c1 in _col_chunks(C):
            gate = _dot(ab, wg_ref[:, c0:c1])
            up = _dot(ab, wu_ref[:, c0:c1])
            sig = 1.0 / (1.0 + jnp.exp(-gate))
            silu = gate * sig
            act_ref[:, c0:c1] = _bf(silu * up)
            fac_ref[0, :, c0:c1] = _bf(up * (sig + silu * (1.0 - sig)))
            fac_ref[1, :, c0:c1] = _bf(silu)

    blk = (_nbytes((tm, K), F32) + 2 * _nbytes((K, C), BF16) + 3 * _nbytes((tm, C), BF16)
           + 2 * _nbytes((tm, C), F32) + _nbytes((tm, K), BF16))
    return pl.pallas_call(
        body, name=name, grid=(M // tm, half),
        in_specs=[pl.BlockSpec((tm, K), lambda i, j: (i, 0)),
                  pl.BlockSpec((None, K, C), lambda i, j: (j, 0, 0)),
                  pl.BlockSpec((None, K, C), lambda i, j: (j + half, 0, 0)),
                  pl.BlockSpec((1, K), lambda i, j: (0, 0))],
        out_specs=[pl.BlockSpec((tm, C), lambda i, j: (i, j)),
                   pl.BlockSpec((2, tm, C), lambda i, j: (0, i, j)),
                   pl.BlockSpec((tm, K), lambda i, j: (i, 0))],
        out_shape=[_sds((M, half * C), BF16),
                   _sds((2, M, half * C), BF16),
                   _sds((M, K), BF16)],
        scratch_shapes=[pltpu.VMEM((tm, K), BF16)],
        compiler_params=_cparams(2, blk, _nbytes((tm, K), BF16)))(*_pin([h, wgu3, wgu3, g]))


def _ffn_down_loss(name, act, wd3, h, target, *, tm):
    M, K = act.shape
    _, _, N = wd3.shape

    def body(a_ref, w_ref, h_ref, t_ref, dy_ref, ls_ref):
        err = _dot(a_ref[...], w_ref[...]) + h_ref[...] - t_ref[...]
        dy_ref[...] = err * (1.0 / N)

        @pl.when(pl.program_id(0) == 0)
        def _():
            ls_ref[...] = jnp.zeros_like(ls_ref)
        ls_ref[...] += jnp.sum(err * err, axis=0, keepdims=True)

    blk = _nbytes((tm, K), BF16) + _nbytes((K, N), BF16) + 4 * _nbytes((tm, N), F32)
    return pl.pallas_call(
        body, name=name, grid=(M // tm,),
        in_specs=[pl.BlockSpec((tm, K), lambda i: (i, 0)),
                  pl.BlockSpec((None, K, N), lambda i: (0, 0, 0)),
                  pl.BlockSpec((tm, N), lambda i: (i, 0)),
                  pl.BlockSpec((tm, N), lambda i: (i, 0))],
        out_specs=[pl.BlockSpec((tm, N), lambda i: (i, 0)),
                   pl.BlockSpec((1, N), lambda i: (0, 0))],
        out_shape=[_sds((M, N), F32), _sds((1, N), F32)],
        compiler_params=_cparams(1, blk))(*_pin([act, wd3, h, target]))


def _nt_pieces(a_ref, w_ref):
    P, _, C = w_ref.shape
    acc = _dot_nt(_bf(a_ref[:, 0:C]), w_ref[0])
    for p in range(1, P):
        acc = acc + _dot_nt(_bf(a_ref[:, p * C:(p + 1) * C]), w_ref[p])
    return acc


def _mm_nt(name, a, w3, *, tm, out_dtype):
    M, N = a.shape
    P, Ko, C = w3.shape
    assert N == P * C and M % tm == 0

    def body(a_ref, w_ref, o_ref):
        o_ref[...] = _nt_pieces(a_ref, w_ref).astype(o_ref.dtype)

    blk = _nbytes((tm, N), a.dtype) + _nbytes((P, Ko, C), BF16) + 2 * _nbytes((tm, Ko), F32)
    return pl.pallas_call(
        body, name=name, grid=(M // tm,),
        in_specs=[pl.BlockSpec((tm, N), lambda i: (i, 0)), pl.BlockSpec((P, Ko, C), lambda i: (0, 0, 0))],
        out_specs=pl.BlockSpec((tm, Ko), lambda i: (i, 0)),
        out_shape=_sds((M, Ko), out_dtype),
        compiler_params=_cparams(1, blk))(*_pin([a, w3]))


def _mm_nt_normbwd(name, a, w3, h, g, dres, *, tm, plan=None):
    M, D = h.shape
    P, Ko, C = w3.shape
    N = a.shape[1]
    assert N == P * C and Ko == D and M % tm == 0
    has_res = dres is not None

    def body(*refs):
        a_ref, w_ref, h_ref, g_ref = refs[:4]
        r_ref = refs[4] if has_res else None
        dx_ref, dg_ref = refs[-2:]
        dhn = _nt_pieces(a_ref, w_ref)
        dx, xh = _norm_bwd(dhn, h_ref[...], g_ref[...])
        if has_res:
            dx = dx + r_ref[...]
        dx_ref[...] = dx

        @pl.when(pl.program_id(0) == 0)
        def _():
            dg_ref[...] = jnp.zeros_like(dg_ref)
        dg_ref[...] += jnp.sum(dhn * xh, axis=0, keepdims=True)

    row = pl.BlockSpec((tm, D), lambda i: (i, 0))
    vec = pl.BlockSpec((1, D), lambda i: (0, 0))
    in_specs = [pl.BlockSpec((tm, N), lambda i: (i, 0)), pl.BlockSpec((P, Ko, C), lambda i: (0, 0, 0)), row, vec]
    args = [a, w3, h, g]
    if has_res:
        in_specs.append(row)
        args.append(dres)
    blk = _nbytes((tm, N), a.dtype) + _nbytes((P, Ko, C), BF16) + 5 * _nbytes((tm, D), F32)
    res, got = _pcall(body, name=name, grid=(M // tm,), in_specs=in_specs, out_specs=[row, vec],
                      out_shape=[_sds((M, D), F32), _sds((1, D), F32)],
                      params=_cparams(1, blk), args=args, plan=plan)
    return res if plan is None else (res, got)


def _ffn_bwd(name, dy, wd3, fac, wgu3, h, g, *, tm):
    M, D = h.shape
    P, _, C = wgu3.shape
    F = fac.shape[2]
    per = F // C

    def body(dy_ref, wd_ref, f_ref, w_ref, h_ref, g_ref, dgu_ref, dx_ref, dg_ref):
        dyv = dy_ref[...]
        ab = _bf(dyv)
        dhn = None
        for pc in range(per):
            cols = slice(pc * C, (pc + 1) * C)
            da = _dot_nt(ab, wd_ref[cols, :])
            dgate = _bf(da * f_ref[0, :, cols].astype(F32))
            dup = _bf(da * f_ref[1, :, cols].astype(F32))
            dgu_ref[0, :, cols] = dgate
            dgu_ref[1, :, cols] = dup
            part = _dot_nt(dgate, w_ref[pc]) + _dot_nt(dup, w_ref[per + pc])
            dhn = part if dhn is None else dhn + part
        dx, xh = _norm_bwd(dhn, h_ref[...], g_ref[...])
        dx_ref[...] = dx + dyv

        @pl.when(pl.program_id(0) == 0)
        def _():
            dg_ref[...] = jnp.zeros_like(dg_ref)
        dg_ref[...] += jnp.sum(dhn * xh, axis=0, keepdims=True)

    row = pl.BlockSpec((tm, D), lambda i: (i, 0))
    vec = pl.BlockSpec((1, D), lambda i: (0, 0))
    planes = pl.BlockSpec((2, tm, F), lambda i: (0, i, 0))
    blk = (2 * _nbytes((2, tm, F), BF16) + (_nbytes((F, D), BF16) + _nbytes((P, D, C), BF16)) // 2
           + 4 * _nbytes((tm, D), F32))
    return pl.pallas_call(
        body, name=name, grid=(M // tm,),
        in_specs=[row, pl.BlockSpec((None, F, D), lambda i: (0, 0, 0), pipeline_mode=pl.Buffered(1)), planes,
                  pl.BlockSpec((P, D, C), lambda i: (0, 0, 0), pipeline_mode=pl.Buffered(1)), row, vec],
        out_specs=[planes, row, vec],
        out_shape=[_sds((2, M, F), BF16), _sds((M, D), F32), _sds((1, D), F32)],
        compiler_params=_cparams(1, blk))(*_pin([dy, wd3, fac, wgu3, h, g]))


def _mm_tn_wide(name, a, b, pieces, *, ts, plan=None):
    S, K = a.shape
    N = b.shape[1]
    C = N // pieces

    def body(a_ref, b_ref, o_ref):
        @pl.when(pl.program_id(0) == 0)
        def _():
            o_ref[...] = jnp.zeros_like(o_ref)
        acc = _dot_tn(_bf(a_ref[...]), _bf(b_ref[...]))
        for p in range(pieces):
            o_ref[p] += acc[:, p * C:(p + 1) * C]

    blk = _nbytes((ts, K), a.dtype) + _nbytes((ts, N), b.dtype) + 2 * _nbytes((K, N), F32)
    res, got = _pcall(body, name=name, grid=(S // ts,),
                      in_specs=[pl.BlockSpec((ts, K), lambda s: (s, 0)), pl.BlockSpec((ts, N), lambda s: (s, 0))],
                      out_specs=[pl.BlockSpec((pieces, K, C), lambda s: (0, 0, 0))],
                      out_shape=[_sds((pieces, K, C), F32)], params=_cparams(1, blk), args=[a, b], plan=plan)
    return res[0] if plan is None else (res[0], got)


def _mm_tn(name, a, a_spec, a_blk, b, b_spec, b_blk, out3, o_spec, o_blk, grid, plan=None):
    def body(a_ref, b_ref, o_ref):
        @pl.when(pl.program_id(2) == 0)
        def _():
            o_ref[...] = jnp.zeros_like(o_ref)
        o_ref[...] += _dot_tn(_bf(a_ref[...]), _bf(b_ref[...]))

    blk = _nbytes(a_blk, a.dtype) + _nbytes(b_blk, b.dtype) + 2 * _nbytes(o_blk, F32)
    res, got = _pcall(body, name=name, grid=grid, in_specs=[a_spec, b_spec], out_specs=[o_spec],
                      out_shape=[_sds(out3, F32)], params=_cparams(3, blk), args=[a, b], plan=plan)
    return res[0] if plan is None else (res[0], got)


def _swap_halves(v, lane, masked):
    up = pltpu.roll(v, HEAD - ROT_HALF, 1)
    down = pltpu.roll(v, ROT_HALF, 1)
    rest = jnp.where(lane < ROT_DIM, down, 0.0) if masked else down
    return jnp.where(lane < ROT_HALF, up, rest)


def _in_proj(name, x, w3, g, ctab, stab, qg, kg, *, tm, plan=None):
    S, K = x.shape
    P, _, C = w3.shape

    def body(x_ref, w_ref, g_ref, c_ref, s_ref, qg_ref, kg_ref, p_ref, xn_ref, q_ref, k_ref):
        xv = x_ref[...]
        ab = _bf(xv * _rstd(xv) * g_ref[...])
        xn_ref[...] = ab
        for p in range(P):
            p_ref[:, p * C:(p + 1) * C] = _dot(ab, w_ref[p])
        lane = lax.broadcasted_iota(jnp.int32, (tm, HEAD), 1)
        cv = c_ref[...]
        sv = s_ref[...]

        def prep(t, gain, scale):
            n = t * _rstd(t) * gain
            return (n * cv + _swap_halves(n, lane, False) * sv) * scale

        for j in range(Q_W // HEAD):
            q_ref[:, j * HEAD:(j + 1) * HEAD] = prep(p_ref[:, j * HEAD:(j + 1) * HEAD], qg_ref[...], SCALE)
        for j in range(KV_W // HEAD):
            k_ref[:, j * HEAD:(j + 1) * HEAD] = prep(p_ref[:, Q_W + j * HEAD:Q_W + (j + 1) * HEAD], kg_ref[...], 1.0)

    row = lambda width: pl.BlockSpec((tm, width), lambda i: (i, 0))
    vec = lambda width: pl.BlockSpec((1, width), lambda i: (0, 0))
    blk = (_nbytes((tm, K), F32) + _nbytes((P, K, C), BF16) + _nbytes((tm, P * C + Q_W + KV_W + 2 * HEAD), F32)
           + _nbytes((tm, K), BF16))
    res, got = _pcall(
        body, name=name, grid=(S // tm,),
        in_specs=[row(K), pl.BlockSpec((P, K, C), lambda i: (0, 0, 0)), vec(K), row(HEAD), row(HEAD), vec(HEAD), vec(HEAD)],
        out_specs=[row(P * C), row(K), row(Q_W), row(KV_W)],
        out_shape=[_sds((S, P * C), F32), _sds((S, K), BF16), _sds((S, Q_W), F32), _sds((S, KV_W), F32)],
        params=_cparams(1, blk), args=[x, w3, g, ctab, stab, qg, kg], plan=plan)
    return res if plan is None else (res, got)


ATTN_STEP = 2048


def _row_idx(start, dil):
    return pl.ds(start, BLOCK) if dil == 1 else pl.ds(start, BLOCK, stride=dil)


def _rows(ref, start, dil):
    return ref[_row_idx(start, dil), :]


def _set_rows(ref, start, dil, val):
    ref[_row_idx(start, dil), :] = val


def _band_mask(first):
    qi = lax.broadcasted_iota(jnp.int32, (BLOCK, 2 * BLOCK), 0)
    kj = lax.broadcasted_iota(jnp.int32, (BLOCK, 2 * BLOCK), 1)
    band = (kj >= qi) & (kj <= qi + BLOCK)
    if first is None:
        return band
    return band & ((kj >= BLOCK) | jnp.logical_not(first))


def _attn_geometry(S, grp):
    dil = DILATIONS[grp]
    bt = BLOCK * dil
    assert S % ATTN_STEP == 0 and ATTN_STEP % bt == 0
    return dil, bt, ATTN_STEP // bt, S // ATTN_STEP


MERGE_ROWS = 256


def _attn_fwd(name, qn, kn, proj, plan=None):
    S = qn.shape[0]
    groups = range(len(DILATIONS))
    geo = [_attn_geometry(S, g) for g in groups]
    nsb = geo[0][3]
    vcol = (Q_W + KV_W) // HEAD

    def body(q0, q1, q2, kc_ref, vc_ref, kp0, kp1, kp2, vp0, vp1, vp2, mix_ref, lse_ref, o_s, l_s):
        q_refs, kp_refs, vp_refs = (q0, q1, q2), (kp0, kp1, kp2), (vp0, vp1, vp2)
        valid_first = _band_mask(pl.program_id(1) == 0)
        valid_inner = _band_mask(None)
        for g in groups:
            dil, bt, nsub, _ = geo[g]
            o_g, l_g = o_s.at[g], l_s.at[g]
            for r in range(dil):
                kprev, vprev = _bf(_rows(kp_refs[g], r, dil)), _bf(_rows(vp_refs[g], r, dil))
                for b in range(nsub):
                    at = b * bt + r
                    kcur, vcur = _bf(_rows(kc_ref, at, dil)), _bf(_rows(vc_ref, at, dil))
                    q = _bf(_rows(q_refs[g], at, dil))
                    k2 = jnp.concatenate([kprev, kcur], axis=0)
                    v2 = jnp.concatenate([vprev, vcur], axis=0)
                    s = jnp.where(valid_first if b == 0 else valid_inner, _dot_nt(q, k2), NEG_INF)
                    m = jnp.max(s, axis=-1, keepdims=True)
                    p = jnp.exp(s - m)
                    l = jnp.sum(p, axis=-1, keepdims=True)
                    acc = _dot(_bf(p), v2)
                    _set_rows(o_g, at, dil, acc / l)
                    _set_rows(l_g, at, dil, jnp.broadcast_to(m + jnp.log(l), (BLOCK, HEAD)))
                    kprev, vprev = kcur, vcur
        for c in range(ATTN_STEP // MERGE_ROWS):
            rows = pl.ds(c * MERGE_ROWS, MERGE_ROWS)
            ls = [l_s[g, rows, :] for g in groups]
            m = jnp.maximum(jnp.maximum(ls[0], ls[1]), ls[2])
            ws = [jnp.exp(v - m) for v in ls]
            den = ws[0] + ws[1] + ws[2]
            num = ws[0] * o_s[0, rows, :] + ws[1] * o_s[1, rows, :] + ws[2] * o_s[2, rows, :]
            mix_ref[rows, :] = _bf(num / den)
            lse_ref[rows, :] = m + jnp.log(den)

    def big(col):
        return pl.BlockSpec((ATTN_STEP, HEAD), lambda h, n: (n, col + h))

    def tail(g, col):
        _, bt, nsub, _ = geo[g]
        return pl.BlockSpec((bt, HEAD), lambda h, n: (jnp.maximum(n * nsub - 1, 0), col + h))

    blk = 7 * _nbytes((ATTN_STEP, HEAD), F32) + 2 * sum(_nbytes((geo[g][1], HEAD), F32) for g in groups)
    scratch = 2 * len(DILATIONS) * _nbytes((ATTN_STEP, HEAD), F32)
    res, got = _pcall(
        body, name=name, grid=(N_HEADS, nsb),
        in_specs=[big(g * N_HEADS) for g in groups] + [big(0), big(vcol)]
                 + [tail(g, 0) for g in groups] + [tail(g, vcol) for g in groups],
        out_specs=[big(0), big(0)],
        out_shape=[_sds((S, 2 * KV_W), BF16), _sds((S, KV_W), F32)],
        scratch_shapes=[pltpu.VMEM((len(DILATIONS), ATTN_STEP, HEAD), F32)] * 2,
        params=_cparams(2, blk, scratch), args=[qn, qn, qn, kn, proj, kn, kn, kn, proj, proj, proj], plan=plan)
    return res if plan is None else (res, got)


def _attn_bwd(name, qn, kn, proj, dattn, lse, delta, grp, plan=None):
    S = qn.shape[0]
    dil, bt, nsub, nsb = _attn_geometry(S, grp)
    vcol = (Q_W + KV_W) // HEAD

    def slot(b, r):
        return pl.ds((b * dil + r) * BLOCK, BLOCK)

    def body(q_ref, kp_ref, kc_ref, vp_ref, vc_ref, do_ref, l_ref, d_ref, dq_ref, dk_ref, dv_ref, ck_ref, cv_ref):
        n = pl.program_id(1)
        par = n % 2

        @pl.when(n < nsb)
        def _():
            valid_first = _band_mask(n == 0)
            valid_inner = _band_mask(None)
            ck, cv = ck_ref.at[par], cv_ref.at[par]
            pk, pv = ck_ref.at[1 - par], cv_ref.at[1 - par]
            for r in range(dil):
                kprev, vprev = _bf(_rows(kp_ref, r, dil)), _bf(_rows(vp_ref, r, dil))
                own_k = own_v = None
                for b in range(nsub):
                    at = b * bt + r
                    kcur, vcur = _bf(_rows(kc_ref, at, dil)), _bf(_rows(vc_ref, at, dil))
                    q = _bf(_rows(q_ref, at, dil))
                    k2 = jnp.concatenate([kprev, kcur], axis=0)
                    v2 = jnp.concatenate([vprev, vcur], axis=0)
                    do = _bf(_rows(do_ref, at, dil))
                    lse_r = _rows(l_ref, at, dil)[:, :1]
                    del_r = _rows(d_ref, at, dil)[:, :1]
                    s = jnp.where(valid_first if b == 0 else valid_inner, _dot_nt(q, k2), NEG_INF)
                    p = jnp.exp(s - lse_r)
                    ds = _bf(p * (_dot_nt(do, v2) - del_r))
                    _set_rows(dq_ref, at, dil, _dot(ds, k2))
                    dk2 = _dot_tn(ds, q)
                    dv2 = _dot_tn(_bf(p), do)
                    if b > 0:
                        ck[slot(b - 1, r), :] = own_k + dk2[:BLOCK]
                        cv[slot(b - 1, r), :] = own_v + dv2[:BLOCK]
                    else:
                        @pl.when(n > 0)
                        def _():
                            pk[slot(nsub - 1, r), :] += dk2[:BLOCK]
                            pv[slot(nsub - 1, r), :] += dv2[:BLOCK]
                    own_k, own_v = dk2[BLOCK:], dv2[BLOCK:]
                    kprev, vprev = kcur, vcur
                ck[slot(nsub - 1, r), :] = own_k
                cv[slot(nsub - 1, r), :] = own_v

        @pl.when(n > 0)
        def _():
            for r in range(dil):
                for b in range(nsub):
                    _set_rows(dk_ref, b * bt + r, dil, ck_ref[1 - par, slot(b, r), :])
                    _set_rows(dv_ref, b * bt + r, dil, cv_ref[1 - par, slot(b, r), :])

    cur = lambda n: jnp.minimum(n, nsb - 1)
    prev = lambda n: jnp.maximum(n - 1, 0)
    tail_at = lambda n: jnp.maximum(cur(n) * nsub - 1, 0)
    big = lambda col: pl.BlockSpec((ATTN_STEP, HEAD), lambda h, n: (cur(n), col(h)))
    tail = lambda col: pl.BlockSpec((bt, HEAD), lambda h, n: (tail_at(n), col(h)))
    late = pl.BlockSpec((ATTN_STEP, HEAD), lambda h, n: (prev(n), h))
    blk = 9 * _nbytes((ATTN_STEP, HEAD), F32) + 2 * _nbytes((bt, HEAD), F32)
    res, got = _pcall(
        body, name=name, grid=(N_HEADS, nsb + 1),
        in_specs=[big(lambda h: grp * N_HEADS + h), tail(lambda h: h), big(lambda h: h),
                  tail(lambda h: vcol + h), big(lambda h: vcol + h),
                  big(lambda h: h), big(lambda h: h), big(lambda h: h)],
        out_specs=[big(lambda h: h), late, late],
        out_shape=[_sds((S, KV_W), F32)] * 3,
        scratch_shapes=[pltpu.VMEM((2, ATTN_STEP, HEAD), F32), pltpu.VMEM((2, ATTN_STEP, HEAD), F32)],
        params=_cparams(2, blk, 4 * _nbytes((ATTN_STEP, HEAD), F32)),
        args=[qn, kn, kn, proj, proj, dattn, lse, delta], plan=plan)
    return res if plan is None else (res, got)


def _window_sum(v, n_doublings, back):
    rows = v.shape[0]
    step = 1
    for _ in range(n_doublings):
        v = v + pltpu.roll(v, step if back else rows - step, 0)
        step *= 2
    return v


def _pool_fwd(name, mix, proj, pool_w, pool_scale, *, tm):
    S = proj.shape[0]
    ucol = (IN_W - POOL_W) // POOL_W
    hpt = tm // POOL_HALO

    def body(mix_in, u_ref, uh_ref, pw_ref, ps_ref, pooled_ref, dp_ref):
        i = pl.program_id(0)
        halo = jnp.where(i == 0, 0.0, uh_ref[...])
        t = lax.broadcasted_iota(jnp.int32, (tm + POOL_HALO, HEAD), 0) + (i * tm - POOL_HALO)
        for g, w in enumerate(POOL_WINDOWS):
            sl = slice(g * HEAD, (g + 1) * HEAD)
            ub = jnp.concatenate([halo[:, sl], u_ref[:, sl]], axis=0)
            cnt = jnp.minimum(t + 1, w).astype(F32)
            d = (_window_sum(ub, g + 1, True) / cnt - ub)[POOL_HALO:]
            db = _bf(d)
            dp_ref[:, sl] = db
            pooled_ref[:, sl] = _bf(_dot(db, pw_ref[g]) * ps_ref[:, sl])

    blk = 3 * _nbytes((tm, POOL_W), F32) + 2 * _nbytes((tm, POOL_W), BF16)
    return pl.pallas_call(
        body, name=name, grid=(S // tm,),
        in_specs=[ANY,
                  pl.BlockSpec((tm, POOL_W), lambda i: (i, ucol)),
                  pl.BlockSpec((POOL_HALO, POOL_W), lambda i: (jnp.maximum(i * hpt - 1, 0), ucol)),
                  pl.BlockSpec((len(POOL_WINDOWS), HEAD, HEAD), lambda i: (0, 0, 0)),
                  pl.BlockSpec((1, POOL_W), lambda i: (0, 0))],
        out_specs=[pl.BlockSpec((tm, POOL_W), lambda i: (i, 1)), pl.BlockSpec((tm, POOL_W), lambda i: (i, 0))],
        out_shape=[_sds((S, 2 * KV_W), BF16), _sds((S, POOL_W), BF16)],
        input_output_aliases={0: 0},
        compiler_params=_cparams(1, blk))(*_pin([mix, proj, proj, pool_w, pool_scale]))


def _mix_bwd(name, dmix, mix, dpool, pool_w, pool_scale, *, tm, plan=None):
    S = dmix.shape[0]
    hpt = tm // POOL_HALO
    last_halo = S // POOL_HALO - 1
    n_tiles = S // tm

    def body(dm_ref, dh_ref, at_ref, dp_ref, pw_ref, ps_ref, da_ref, dl_ref, du_ref, gw_ref, gs_ref):
        i = pl.program_id(0)

        @pl.when(i == 0)
        def _():
            gw_ref[...] = jnp.zeros_like(gw_ref)
            gs_ref[...] = jnp.zeros_like(gs_ref)

        for h in range(N_HEADS):
            sl = slice(h * HEAD, (h + 1) * HEAD)
            da = dm_ref[:, sl]
            da_ref[:, sl] = da
            dl_ref[:, sl] = jnp.broadcast_to(
                jnp.sum(da * at_ref[:, sl].astype(F32), axis=-1, keepdims=True), (tm, HEAD))
        halo = jnp.where(i == n_tiles - 1, 0.0, dh_ref[...])
        t = lax.broadcasted_iota(jnp.int32, (tm + POOL_HALO, HEAD), 0) + i * tm
        for g, w in enumerate(POOL_WINDOWS):
            sl = slice(g * HEAD, (g + 1) * HEAD)
            dy = jnp.concatenate([dm_ref[:, KV_W + g * HEAD:KV_W + (g + 1) * HEAD], halo[:, sl]], axis=0)
            dys = _bf(dy * ps_ref[:, sl])
            dd = _dot_nt(dys, pw_ref[g])
            cnt = jnp.minimum(t + 1, w).astype(F32)
            du_ref[:, sl] = (_window_sum(dd / cnt, g + 1, False) - dd)[:tm]
            db = dp_ref[:, sl]
            gw_ref[g] += _dot_tn(db, dys[:tm])
            gs_ref[:, sl] += jnp.sum(dy[:tm] * _dot(db, pw_ref[g]), axis=0, keepdims=True)

    tile = pl.BlockSpec((tm, KV_W), lambda i: (i, 0))
    blk = _nbytes((tm, 2 * KV_W), F32) + 6 * _nbytes((tm, KV_W), F32)
    res, got = _pcall(
        body, name=name, grid=(n_tiles,),
        in_specs=[pl.BlockSpec((tm, 2 * KV_W), lambda i: (i, 0)),
                  pl.BlockSpec((POOL_HALO, POOL_W), lambda i: (jnp.minimum((i + 1) * hpt, last_halo), 1)),
                  tile, tile,
                  pl.BlockSpec((len(POOL_WINDOWS), HEAD, HEAD), lambda i: (0, 0, 0)),
                  pl.BlockSpec((1, POOL_W), lambda i: (0, 0))],
        out_specs=[tile, tile, tile,
                   pl.BlockSpec((len(POOL_WINDOWS), HEAD, HEAD), lambda i: (0, 0, 0)),
                   pl.BlockSpec((1, POOL_W), lambda i: (0, 0))],
        out_shape=[_sds((S, KV_W), F32)] * 3 + [
            _sds((len(POOL_WINDOWS), HEAD, HEAD), F32), _sds((1, POOL_W), F32)],
        params=_cparams(1, blk), args=[dmix, dmix, mix, dpool, pool_w, pool_scale], plan=plan)
    return res if plan is None else (res, got)


def _qkv_bwd(name, dqs, dks, dvs, du, proj, ctab, stab, qg, kg, *, tm, plan=None):
    S = proj.shape[0]
    width = Q_W + KV_W

    def body(dq0, dq1, dq2, dk0, dk1, dk2, dv0, dv1, dv2, du_ref, p_ref, c_ref, s_ref, qg_ref, kg_ref,
             dp_ref, gq_ref, gk_ref):
        @pl.when(pl.program_id(0) == 0)
        def _():
            gq_ref[...] = jnp.zeros_like(gq_ref)
            gk_ref[...] = jnp.zeros_like(gk_ref)

        lane = lax.broadcasted_iota(jnp.int32, (tm, HEAD), 1)
        cv = c_ref[...]
        sv = s_ref[...]

        def back(dy, t, g, scale):
            dy = dy * scale
            dn = dy * cv + _swap_halves(dy * sv, lane, True)
            dt, xh = _norm_bwd(dn, t, g)
            return dt, jnp.sum(dn * xh, axis=0, keepdims=True)

        dqr = (dq0, dq1, dq2)
        gq = jnp.zeros((1, HEAD), F32)
        for j in range(Q_W // HEAD):
            grp, h = divmod(j, N_HEADS)
            dt, gj = back(dqr[grp][:, h * HEAD:(h + 1) * HEAD], p_ref[:, j * HEAD:(j + 1) * HEAD], qg_ref[...], SCALE)
            dp_ref[:, j * HEAD:(j + 1) * HEAD] = _bf(dt)
            gq = gq + gj
        gq_ref[...] += gq
        gk = jnp.zeros((1, HEAD), F32)
        for h in range(N_HEADS):
            sl = slice(h * HEAD, (h + 1) * HEAD)
            dt, gj = back(dk0[:, sl] + dk1[:, sl] + dk2[:, sl], p_ref[:, Q_W + h * HEAD:Q_W + (h + 1) * HEAD],
                          kg_ref[...], 1.0)
            dp_ref[:, Q_W + h * HEAD:Q_W + (h + 1) * HEAD] = _bf(dt)
            gk = gk + gj
        gk_ref[...] += gk
        dp_ref[:, width:width + KV_W] = _bf(dv0[...] + dv1[...] + dv2[...])
        dp_ref[:, width + KV_W:] = _bf(du_ref[...])

    tile = pl.BlockSpec((tm, KV_W), lambda i: (i, 0))
    vec = pl.BlockSpec((1, HEAD), lambda i: (0, 0))
    rot = pl.BlockSpec((tm, HEAD), lambda i: (i, 0))
    blk = 10 * _nbytes((tm, KV_W), F32) + _nbytes((tm, width), F32) + _nbytes((tm, IN_W), BF16)
    res, got = _pcall(
        body, name=name, grid=(S // tm,),
        in_specs=[tile] * 10 + [pl.BlockSpec((tm, width), lambda i: (i, 0)), rot, rot, vec, vec],
        out_specs=[pl.BlockSpec((tm, IN_W), lambda i: (i, 0)), vec, vec],
        out_shape=[_sds((S, IN_W), BF16), _sds((1, HEAD), F32),
                   _sds((1, HEAD), F32)],
        params=_cparams(1, blk), args=[*dqs, *dks, *dvs, du, proj, ctab, stab, qg, kg], plan=plan)
    return res if plan is None else (res, got)


def _cross_heads(q_ref, kv_ref, qg, kg, h):
    sl = slice(h * HEAD, (h + 1) * HEAD)
    qr = q_ref[:, sl]
    kr = kv_ref[:, sl]
    qh = qr * _rstd(qr) * qg * SCALE
    kh = kr * _rstd(kr) * kg
    vh = kv_ref[:, X_W + h * HEAD:X_W + (h + 1) * HEAD]
    return qr, _bf(qh), _bf(kh), _bf(vh)


def _cross_fwd(name, qraw, kv, qg, kg, *, tm):
    S = qraw.shape[0]
    M = kv.shape[0]

    def body(q_ref, kv_ref, qg_ref, kg_ref, o_ref):
        for h in range(N_HEADS):
            _, qh, kh, vh = _cross_heads(q_ref, kv_ref, qg_ref[...], kg_ref[...], h)
            s = _dot_nt(qh, kh)
            p = jnp.exp(s - jnp.max(s, axis=-1, keepdims=True))
            l = jnp.sum(p, axis=-1, keepdims=True)
            o_ref[:, h * HEAD:(h + 1) * HEAD] = _bf(_dot(_bf(p), vh) / l)

    vec = pl.BlockSpec((1, HEAD), lambda i: (0, 0))
    blk = 2 * _nbytes((tm, X_W), F32) + _nbytes((M, 2 * X_W), F32) + 4 * _nbytes((tm, M), F32)
    return pl.pallas_call(
        body, name=name, grid=(S // tm,),
        in_specs=[pl.BlockSpec((tm, X_W), lambda i: (i, 0)), pl.BlockSpec((M, 2 * X_W), lambda i: (0, 0)), vec, vec],
        out_specs=pl.BlockSpec((tm, X_W), lambda i: (i, 0)),
        out_shape=_sds((S, X_W), BF16),
        compiler_params=_cparams(1, blk))(*_pin([qraw, kv, qg, kg]))


def _cross_bwd(name, do, qraw, kv, qg, kg, *, tm, plan=None):
    S = qraw.shape[0]
    M = kv.shape[0]

    def body(do_ref, q_ref, kv_ref, qg_ref, kg_ref, dq_ref, dk_ref, dv_ref, gq_ref):
        @pl.when(pl.program_id(0) == 0)
        def _():
            dk_ref[...] = jnp.zeros_like(dk_ref)
            dv_ref[...] = jnp.zeros_like(dv_ref)
            gq_ref[...] = jnp.zeros_like(gq_ref)

        gq = jnp.zeros((1, HEAD), F32)
        for h in range(N_HEADS):
            sl = slice(h * HEAD, (h + 1) * HEAD)
            qr, qh, kh, vh = _cross_heads(q_ref, kv_ref, qg_ref[...], kg_ref[...], h)
            doh = _bf(do_ref[:, sl])
            s = _dot_nt(qh, kh)
            p = jnp.exp(s - jnp.max(s, axis=-1, keepdims=True))
            p = p / jnp.sum(p, axis=-1, keepdims=True)
            pb = _bf(p)
            dp = _dot_nt(doh, vh)
            ds = _bf(p * (dp - jnp.sum(dp * p, axis=-1, keepdims=True)))
            dv_ref[:, sl] += _dot_tn(pb, doh)
            dk_ref[:, sl] += _dot_tn(ds, qh)
            dn = _dot(ds, kh) * SCALE
            dt, xh = _norm_bwd(dn, qr, qg_ref[...])
            dq_ref[:, sl] = _bf(dt)
            gq = gq + jnp.sum(dn * xh, axis=0, keepdims=True)
        gq_ref[...] += gq

    vec = pl.BlockSpec((1, HEAD), lambda i: (0, 0))
    acc = pl.BlockSpec((M, X_W), lambda i: (0, 0))
    blk = 3 * _nbytes((tm, X_W), F32) + 3 * _nbytes((M, 2 * X_W), F32) + 6 * _nbytes((tm, M), F32)
    res, got = _pcall(
        body, name=name, grid=(S // tm,),
        in_specs=[pl.BlockSpec((tm, X_W), lambda i: (i, 0)), pl.BlockSpec((tm, X_W), lambda i: (i, 0)),
                  pl.BlockSpec((M, 2 * X_W), lambda i: (0, 0)), vec, vec],
        out_specs=[pl.BlockSpec((tm, X_W), lambda i: (i, 0)), acc, acc, vec],
        out_shape=[_sds((S, X_W), BF16), _sds((M, X_W), F32),
                   _sds((M, X_W), F32), _sds((1, HEAD), F32)],
        params=_cparams(1, blk), args=[do, qraw, kv, qg, kg], plan=plan)
    return res if plan is None else (res, got)


def _cross_kv_bwd(name, dkn, dv, kv, kg):
    M = kv.shape[0]

    def body(dk_ref, dv_ref, kv_ref, kg_ref, o_ref, g_ref):
        gk = jnp.zeros((1, HEAD), F32)
        for h in range(N_HEADS):
            sl = slice(h * HEAD, (h + 1) * HEAD)
            dn = dk_ref[:, sl]
            dt, xh = _norm_bwd(dn, kv_ref[:, sl], kg_ref[...])
            o_ref[:, sl] = _bf(dt)
            gk = gk + jnp.sum(dn * xh, axis=0, keepdims=True)
        o_ref[:, X_W:] = _bf(dv_ref[...])
        g_ref[...] = gk

    full = lambda shape: pl.BlockSpec(shape, lambda i: (0,) * len(shape))
    return pl.pallas_call(
        body, name=name, grid=(1,),
        in_specs=[full((M, X_W)), full((M, X_W)), full((M, 2 * X_W)), full((1, HEAD))],
        out_specs=[full((M, 2 * X_W)), full((1, HEAD))],
        out_shape=[_sds((M, 2 * X_W), BF16), _sds((1, HEAD), F32)],
        compiler_params=_cparams(1, 6 * _nbytes((M, 2 * X_W), F32)))(dkn, dv, kv, kg)


def _rope_tables(positions):
    inv_freq = ROPE_THETA ** (-jnp.arange(0, ROT_DIM, 2, dtype=F32) / ROT_DIM)
    ang = positions.astype(F32)[:, None] * inv_freq
    cos, sin = jnp.cos(ang), jnp.sin(ang)
    S = positions.shape[0]
    ctab = jnp.concatenate([cos, cos, jnp.ones((S, HEAD - ROT_DIM), F32)], axis=-1)
    stab = jnp.concatenate([-sin, sin, jnp.zeros((S, HEAD - ROT_DIM), F32)], axis=-1)
    return ctab, stab


GATHER_BEHIND_IN_PROJ = ("w_out", "w_cq", "w_ckv", "w_co")
FFN_WEIGHTS = ("w_gate_up", "w_down")


def _hosted(fn, *args, plan=None, **kw):
    if plan is None:
        return fn(*args, **kw), []
    return fn(*args, plan=plan, **kw)


def _local_step(x, mem, positions, target, wb, sm, *, tm=512, place=None):
    S, D = x.shape
    M = mem.shape[0]
    dist = place is not None
    wb = dict(wb)
    ctab, stab = _rope_tables(positions)
    pool_w_b = _bf(sm["pool_w"])

    def gather(names):
        return _GatherPlan([wb[k] for k in names]) if dist else None

    if dist:
        wb["w_in"], = _run_plan("gather_w_in", gather(["w_in"]))
    w_in = wb["w_in"]
    behind_in_proj = GATHER_BEHIND_IN_PROJ + ("w_gate_up",)
    (proj, xn1, qn, kn), got = _hosted(_in_proj, "in_proj", x, w_in, sm["mix_norm_g"], ctab, stab,
                                       sm["q_norm_g"], sm["k_norm_g"], tm=tm, plan=gather(behind_in_proj))
    wb.update(zip(behind_in_proj, got))
    (mix, lse), got = _hosted(_attn_fwd, "attn_fwd", qn, kn, proj, plan=gather(["w_down"]))
    wb.update(zip(["w_down"], got))
    w_out = wb["w_out"].reshape(1, 2 * KV_W, D)
    w_cq = wb["w_cq"].reshape(1, D, X_W)
    w_ckv = wb["w_ckv"].reshape(1, D, 2 * X_W)
    w_co = wb["w_co"]
    w_gu = wb["w_gate_up"]
    w_down = wb["w_down"].reshape(1, D_FF, D)
    cin = w_in.shape[2]
    cco = w_co.shape[2]
    cgu = w_gu.shape[2]
    mix, dpool = _pool_fwd("pool_fwd", mix, proj, pool_w_b, sm["pool_scale"], tm=tm)
    h1 = _mm_nn("out_proj", mix, w_out, tm=tm, residual=x)
    cq_raw, hn2 = _mm_nn("cq_proj", h1, w_cq, tm=tm, norm_g=sm["cross_norm_g"])
    kv, mem_n = _mm_nn("ckv_proj", mem, w_ckv, tm=M, norm_g=sm["mem_norm_g"])
    xo = _cross_fwd("cross_fwd", cq_raw, kv, sm["cq_norm_g"], sm["ck_norm_g"], tm=tm)
    h2 = _mm_nn("co_proj", xo, w_co, tm=tm, residual=h1)
    act, gu, hn3 = _ffn_up("ffn_up", h2, w_gu, sm["ffn_norm_g"], tm=tm)
    dy, lsum = _ffn_down_loss("ffn_down_loss", act, w_down, h2, target, tm=tm)
    loss = 0.5 * jnp.sum(lsum) / D

    ts = 2 * tm
    nS = S // ts
    g_down = _mm_tn("g_w_down", act, pl.BlockSpec((ts, cgu), lambda r, c, s: (s, r)), (ts, cgu),
                    dy, pl.BlockSpec((ts, D), lambda r, c, s: (s, 0)), (ts, D),
                    (1, D_FF, D), pl.BlockSpec((None, cgu, D), lambda r, c, s: (0, r, 0)), (cgu, D),
                    (D_FF // cgu, 1, nS))
    dgu, dh2, g_ffn_norm = _ffn_bwd("ffn_bwd", dy, w_down, gu, w_gu, h2, sm["ffn_norm_g"], tm=tm // 2)
    g_gu = _mm_tn("g_w_gate_up", hn3, pl.BlockSpec((ts, D), lambda r, c, s: (s, 0)), (ts, D),
                  dgu, pl.BlockSpec((None, ts, cgu), lambda r, c, s: (c // 2, s, c % 2)), (ts, cgu),
                  (4, D, cgu), pl.BlockSpec((None, D, cgu), lambda r, c, s: (c, 0, 0)), (D, cgu),
                  (1, 4, nS))

    dxo = _mm_nt("co_proj_bwd", dh2, w_co, tm=tm, out_dtype=BF16)
    g_co = _mm_tn_wide("g_w_co", xo, dh2, N_CHIPS, ts=ts)
    full = {"w_gate_up": g_gu, "w_down": g_down.reshape(N_CHIPS, D_FF // N_CHIPS, D)}
    sums = {}

    def swap(names):
        return _SwapPlan([full[k] for k in names]) if dist else None

    def add_halves(names, from_sibling):
        for k, t in zip(names, from_sibling):
            sums[k] = _add_core_halves(f"add_halves_{k}", full[k], t, place[0])

    def exchange(names):
        return _ExchangePlan([sums[k][1] for k in names]) if dist else None

    def sum_chips(names, from_chips):
        return [_sum_chips(f"sum_chips_{k}", sums[k][0], t, place[1]) for k, t in zip(names, from_chips)]

    (dcq, dkn, dvm, g_cq_norm), got = _hosted(_cross_bwd, "cross_bwd", dxo, cq_raw, kv, sm["cq_norm_g"],
                                              sm["ck_norm_g"], tm=tm, plan=swap(FFN_WEIGHTS))
    add_halves(FFN_WEIGHTS, got)
    dkv, g_ck_norm = _cross_kv_bwd("cross_kv_bwd", dkn, dvm, kv, sm["ck_norm_g"])
    dh1, g_cross_norm = _mm_nt_normbwd("cq_proj_bwd", dcq, w_cq, h1, sm["cross_norm_g"], dh2, tm=tm)
    g_cq = _mm_tn("g_w_cq", hn2, pl.BlockSpec((ts, D), lambda r, c, s: (s, 0)), (ts, D),
                  dcq, pl.BlockSpec((ts, X_W), lambda r, c, s: (s, 0)), (ts, X_W),
                  (1, D, X_W), pl.BlockSpec((None, D, X_W), lambda r, c, s: (0, 0, 0)), (D, X_W), (1, 1, nS))
    _, g_mem_norm = _mm_nt_normbwd("ckv_proj_bwd", dkv, w_ckv, mem, sm["mem_norm_g"], None, tm=M)
    g_ckv = _mm_tn("g_w_ckv", mem_n, pl.BlockSpec((M, D), lambda r, c, s: (0, 0)), (M, D),
                   dkv, pl.BlockSpec((M, 2 * X_W), lambda r, c, s: (0, 0)), (M, 2 * X_W),
                   (1, D, 2 * X_W), pl.BlockSpec((None, D, 2 * X_W), lambda r, c, s: (0, 0, 0)), (D, 2 * X_W),
                   (1, 1, 1))

    dmix = _mm_nt("out_proj_bwd", dh1, w_out, tm=tm, out_dtype=F32)
    g_out = _mm_tn("g_w_out", mix, pl.BlockSpec((ts, 2 * KV_W), lambda r, c, s: (s, 0)), (ts, 2 * KV_W),
                   dh1, pl.BlockSpec((ts, D), lambda r, c, s: (s, 0)), (ts, D),
                   (1, 2 * KV_W, D), pl.BlockSpec((None, 2 * KV_W, D), lambda r, c, s: (0, 0, 0)), (2 * KV_W, D),
                   (1, 1, nS))
    full.update({
        "w_out": g_out.reshape(N_CHIPS, 2 * KV_W // N_CHIPS, D),
        "w_cq": g_cq.reshape(N_CHIPS, D // N_CHIPS, X_W),
        "w_ckv": g_ckv.reshape(N_CHIPS, D // N_CHIPS, 2 * X_W),
        "w_co": g_co,
    })
    mixer = GATHER_BEHIND_IN_PROJ
    (dattn, delta, du, g_pool_w, g_pool_scale), got = _hosted(
        _mix_bwd, "mix_bwd", dmix, mix, dpool, pool_w_b, sm["pool_scale"], tm=tm, plan=swap(mixer))
    add_halves(mixer, got)
    behind_attn = (None, mixer, FFN_WEIGHTS)
    halves = {}
    dqs, dks, dvs = [], [], []
    for grp in range(len(DILATIONS)):
        names = behind_attn[grp]
        (dq, dk, dv), got = _hosted(_attn_bwd, f"attn_bwd{grp}", qn, kn, proj, dattn, lse, delta, grp,
                                    plan=exchange(names) if names else None)
        if dist and names:
            halves.update(zip(names, sum_chips(names, got)))
        dqs.append(dq)
        dks.append(dk)
        dvs.append(dv)
    joined = mixer + FFN_WEIGHTS
    (dproj, g_q_norm, g_k_norm), got = _hosted(
        _qkv_bwd, "qkv_bwd", dqs, dks, dvs, du, proj, ctab, stab, sm["q_norm_g"], sm["k_norm_g"], tm=tm,
        plan=_JoinPlan([halves[k] for k in joined]) if dist else None)
    shards = dict(zip(joined, got))
    dx, g_mix_norm = _mm_nt_normbwd("in_proj_bwd", dproj, w_in, x, sm["mix_norm_g"], dh1, tm=tm)
    small = {
        "mix_norm_g": g_mix_norm, "q_norm_g": g_q_norm, "k_norm_g": g_k_norm, "pool_w": g_pool_w,
        "pool_scale": g_pool_scale, "cross_norm_g": g_cross_norm, "mem_norm_g": g_mem_norm,
        "cq_norm_g": g_cq_norm, "ck_norm_g": g_ck_norm, "ffn_norm_g": g_ffn_norm,
    }
    full["w_in"], got = _hosted(_mm_tn_wide, "g_w_in", xn1, dproj, N_CHIPS, ts=tm,
                                plan=_AllPushPlan(_pack_small(small)) if dist else None)
    if dist:
        small = _unpack_small(_sum_slots("sum_small", got[0]), sm)
        add_halves(["w_in"], _run_plan("swap_w_in", swap(["w_in"])))
        half, = sum_chips(["w_in"], _run_plan("exchange_w_in", exchange(["w_in"])))
        shards["w_in"], = _run_plan("join_w_in", _JoinPlan([half]))
    big = shards if dist else full
    return loss, dx, big, small


BIG = ("w_in", "w_out", "w_cq", "w_ckv", "w_co", "w_gate_up", "w_down")
SMALL = ("mix_norm_g", "q_norm_g", "k_norm_g", "pool_w", "pool_scale", "cross_norm_g", "mem_norm_g",
         "cq_norm_g", "ck_norm_g", "ffn_norm_g")
WEIGHTS = ("mix_norm_g", "w_in", "q_norm_g", "k_norm_g", "pool_w", "pool_scale", "w_out", "cross_norm_g",
           "mem_norm_g", "w_cq", "w_ckv", "cq_norm_g", "ck_norm_g", "w_co", "ffn_norm_g", "w_gate_up", "w_down")


def _cast_piece(name, w, k_arr):
    R, C = w.shape
    hr = R // 2

    def body(k_ref, w_ref, o_ref):
        o_ref[...] = _bf(w_ref[...])

    return pl.pallas_call(
        body, name=name,
        grid_spec=pltpu.PrefetchScalarGridSpec(
            num_scalar_prefetch=1, grid=(2,),
            in_specs=[pl.BlockSpec((hr, C), lambda i, k: (i, 0))],
            out_specs=pl.BlockSpec((None, hr, C), lambda i, k: (k[0], i, 0))),
        out_shape=_sds((N_CHIPS, R, C), BF16),
        compiler_params=_cparams(1, 2 * _nbytes((hr, C), F32)))(k_arr, *_pin([w]))


def _add_core_halves(name, g, t, c_arr):
    P, R, C = g.shape
    hr = R // 2

    def body(c_ref, g_ref, t_ref, o_ref, ob_ref):
        tot = g_ref[...] + t_ref[...]
        o_ref[...] = tot
        ob_ref[...] = _bf(tot)

    piece = pl.BlockSpec((None, hr, C), lambda p, c: (p, 0, 0))
    return pl.pallas_call(
        body, name=name,
        grid_spec=pltpu.PrefetchScalarGridSpec(
            num_scalar_prefetch=1, grid=(P,),
            in_specs=[pl.BlockSpec((None, hr, C), lambda p, c: (p, c[0], 0)), piece],
            out_specs=[piece, piece]),
        out_shape=[_sds((P, hr, C), F32), _sds((P, hr, C), BF16)],
        compiler_params=_cparams(1, 4 * _nbytes((hr, C), F32)))(c_arr, *_pin([g, t]))


def _sum_chips(name, own, got, kc_arr):
    P, hr, C = own.shape

    def body(kc_ref, o_ref, g_ref, r_ref):
        r_ref[...] = ((o_ref[...] + g_ref[0].astype(F32)) + g_ref[1].astype(F32)) + g_ref[2].astype(F32)

    return pl.pallas_call(
        body, name=name,
        grid_spec=pltpu.PrefetchScalarGridSpec(
            num_scalar_prefetch=1, grid=(1,),
            in_specs=[pl.BlockSpec((None, hr, C), lambda i, kc: (kc[0], 0, 0)),
                      pl.BlockSpec((N_CHIPS - 1, hr, C), lambda i, kc: (0, 0, 0))],
            out_specs=pl.BlockSpec((hr, C), lambda i, kc: (kc[1], 0))),
        out_shape=_sds((2 * hr, C), F32),
        compiler_params=_cparams(1, 5 * _nbytes((hr, C), F32)))(kc_arr, *_pin([own, got]))


N_DEV = 8


class _AllPushPlan(_Plan):
    def __init__(self, v):
        self.ins = [v]
        self.out_shapes = [jax.ShapeDtypeStruct((N_DEV,) + v.shape, v.dtype)]
        self.sem_shapes = [pltpu.SemaphoreType.DMA((N_DEV - 1,)), pltpu.SemaphoreType.DMA((N_DEV - 1,)),
                           pltpu.SemaphoreType.DMA]

    def copies(self, ins, outs, sems):
        send, recv, own = sems
        x, y, c, _ = _place()
        slot = outs[0].at[4 * x + 2 * y + c]
        cps = [pltpu.make_async_copy(ins[0], slot, own)]
        flips = [(dx, dy, dc) for dx in (0, 1) for dy in (0, 1) for dc in (0, 1)][1:]
        for q, (dx, dy, dc) in enumerate(flips):
            to = (x + dx - 2 * x * dx, y + dy - 2 * y * dy, c + dc - 2 * c * dc)
            cps.append(pltpu.make_async_remote_copy(src_ref=ins[0], dst_ref=slot, send_sem=send.at[q],
                                                    recv_sem=recv.at[q], device_id=to, device_id_type=MESH))
        return cps


def _sum_slots(name, slots):
    n, R, C = slots.shape

    def body(s_ref, o_ref):
        acc = s_ref[0]
        for d in range(1, n):
            acc = acc + s_ref[d]
        o_ref[...] = acc

    return pl.pallas_call(
        body, name=name, grid=(1,),
        in_specs=[pl.BlockSpec((n, R, C), lambda i: (0, 0, 0))], out_specs=pl.BlockSpec((R, C), lambda i: (0, 0)),
        out_shape=_sds((R, C), F32),
        compiler_params=_cparams(1, _nbytes((n + 1, R, C), F32)))(*_pin([slots]))


def _adamw(name, w, g, m, v, *, tr):
    R, C = w.shape

    def body(w_ref, g_ref, m_ref, v_ref, d_ref, nm_ref, nv_ref):
        gv = g_ref[...]
        nm = ADAM_B1 * m_ref[...] + (1.0 - ADAM_B1) * gv
        nv = ADAM_B2 * v_ref[...] + (1.0 - ADAM_B2) * (gv * gv)
        m_hat = nm / (1.0 - ADAM_B1 ** ADAM_STEP)
        v_hat = nv / (1.0 - ADAM_B2 ** ADAM_STEP)
        d_ref[...] = -ADAM_LR * (m_hat / (jnp.sqrt(v_hat) + ADAM_EPS) + ADAM_WD * w_ref[...])
        nm_ref[...] = nm
        nv_ref[...] = nv

    tile = pl.BlockSpec((tr, C), lambda i: (i, 0))
    return pl.pallas_call(
        body, name=name, grid=(R // tr,), in_specs=[tile] * 4, out_specs=[tile] * 3,
        out_shape=[_sds((R, C), F32)] * 3,
        compiler_params=_cparams(1, 7 * _nbytes((tr, C), F32)))(*_pin([w, g, m, v]))


def _pack_small(d):
    parts = []
    for name in SMALL:
        a = d[name].reshape(-1, HEAD)
        pad = (-a.shape[0]) % 8
        parts.append(jnp.pad(a, ((0, pad), (0, 0))))
    return jnp.concatenate(parts, axis=0)


def _unpack_small(packed, like):
    out = {}
    row = 0
    for name in SMALL:
        shape = like[name].shape
        rows = like[name].size // HEAD
        out[name] = packed[row:row + rows].reshape(shape)
        row += rows + (-rows) % 8
    return out


def kernel(x, mem, positions, mix_norm_g, w_in, q_norm_g, k_norm_g, pool_w, pool_scale, w_out, cross_norm_g, mem_norm_g, w_cq, w_ckv, cq_norm_g, ck_norm_g, w_co, ffn_norm_g, w_gate_up, w_down, loss_target, m_mix_norm_g, m_w_in, m_q_norm_g, m_k_norm_g, m_pool_w, m_pool_scale, m_w_out, m_cross_norm_g, m_mem_norm_g, m_w_cq, m_w_ckv, m_cq_norm_g, m_ck_norm_g, m_w_co, m_ffn_norm_g, m_w_gate_up, m_w_down, v_mix_norm_g, v_w_in, v_q_norm_g, v_k_norm_g, v_pool_w, v_pool_scale, v_w_out, v_cross_norm_g, v_mem_norm_g, v_w_cq, v_w_ckv, v_cq_norm_g, v_ck_norm_g, v_w_co, v_ffn_norm_g, v_w_gate_up, v_w_down):
    w = dict(mix_norm_g=mix_norm_g, w_in=w_in, q_norm_g=q_norm_g, k_norm_g=k_norm_g, pool_w=pool_w,
             pool_scale=pool_scale, w_out=w_out, cross_norm_g=cross_norm_g, mem_norm_g=mem_norm_g, w_cq=w_cq,
             w_ckv=w_ckv, cq_norm_g=cq_norm_g, ck_norm_g=ck_norm_g, w_co=w_co, ffn_norm_g=ffn_norm_g,
             w_gate_up=w_gate_up, w_down=w_down)
    m = dict(mix_norm_g=m_mix_norm_g, w_in=m_w_in, q_norm_g=m_q_norm_g, k_norm_g=m_k_norm_g, pool_w=m_pool_w,
             pool_scale=m_pool_scale, w_out=m_w_out, cross_norm_g=m_cross_norm_g, mem_norm_g=m_mem_norm_g,
             w_cq=m_w_cq, w_ckv=m_w_ckv, cq_norm_g=m_cq_norm_g, ck_norm_g=m_ck_norm_g, w_co=m_w_co,
             ffn_norm_g=m_ffn_norm_g, w_gate_up=m_w_gate_up, w_down=m_w_down)
    v = dict(mix_norm_g=v_mix_norm_g, w_in=v_w_in, q_norm_g=v_q_norm_g, k_norm_g=v_k_norm_g, pool_w=v_pool_w,
             pool_scale=v_pool_scale, w_out=v_w_out, cross_norm_g=v_cross_norm_g, mem_norm_g=v_mem_norm_g,
             w_cq=v_w_cq, w_ckv=v_w_ckv, cq_norm_g=v_cq_norm_g, ck_norm_g=v_ck_norm_g, w_co=v_w_co,
             ffn_norm_g=v_ffn_norm_g, w_gate_up=v_w_gate_up, w_down=v_w_down)

    c_arr = lax.axis_index("c").astype(jnp.int32).reshape(1)
    k_arr = (2 * lax.axis_index("x") + lax.axis_index("y")).astype(jnp.int32).reshape(1)
    kc_arr = jnp.concatenate([k_arr, c_arr])
    wb = {k: _cast_piece(f"cast_{k}", w[k][0], k_arr) for k in BIG}
    sm = {k: (w[k][0] if k == "pool_w" else w[k]) for k in SMALL}
    loss_part, dx, gshard, gsm = _local_step(x[0], mem[0], positions[0], loss_target[0], wb, sm,
                                             place=(c_arr, kc_arr))
    loss = lax.psum(loss_part, ("x", "y", "c"))

    grads, deltas, new_m, new_v = {}, {}, {}, {}
    for k in BIG:
        shard = w[k][0]
        tr = shard.shape[0] // 4
        d, nm, nv = _adamw(f"adamw_{k}", shard, gshard[k], m[k][0], v[k][0], tr=tr)
        grads[k], deltas[k], new_m[k], new_v[k] = gshard[k][None], d[None], nm[None], nv[None]
    smw = {k: (w[k][0] if k == "pool_w" else w[k]) for k in SMALL}
    smm = {k: (m[k][0] if k == "pool_w" else m[k]) for k in SMALL}
    smv = {k: (v[k][0] if k == "pool_w" else v[k]) for k in SMALL}
    pw, pg, pm, pv = _pack_small(smw), _pack_small(gsm), _pack_small(smm), _pack_small(smv)
    d, nm, nv = _adamw("adamw_small", pw, pg, pm, pv, tr=pw.shape[0])
    for dst, packed in ((deltas, d), (new_m, nm), (new_v, nv)):
        un = _unpack_small(packed, sm)
        for k in SMALL:
            dst[k] = un[k].reshape(w[k].shape)
    for k in SMALL:
        grads[k] = gsm[k].reshape(w[k].shape)

    return (loss, dx[None], *[grads[k] for k in WEIGHTS], *[deltas[k] for k in WEIGHTS],
            *[new_m[k] for k in WEIGHTS], *[new_v[k] for k in WEIGHTS])
```

```python
import functools

import jax
import jax.numpy as jnp
from jax import lax
from jax.experimental import pallas as pl
from jax.experimental.pallas import tpu as pltpu

F32 = jnp.float32
BF16 = jnp.bfloat16
MESH = pl.DeviceIdType.MESH
ANY = pl.BlockSpec(memory_space=pl.ANY)

D_MODEL = 1024
HEAD = 128
N_HEADS = 4
DILATIONS = (1, 4, 16)
BLOCK = 128
Q_W = 1536
KV_W = 512
POOL_W = 512
POOL_WINDOWS = (2, 4, 8, 16)
POOL_HALO = 16
IN_W = 3072
ROT_DIM = 32
ROT_HALF = 16
ROPE_THETA = 500000.0
X_W = 512
D_FF = 2816
EPS = 1e-6
NEG_INF = -1e30
SCALE = HEAD ** -0.5
N_CHIPS = 4

ADAM_LR = 0.001
ADAM_B1 = 0.9
ADAM_B2 = 0.999
ADAM_EPS = 1e-08
ADAM_WD = 0.01
ADAM_STEP = 10

VMEM_BYTES_V7X = 64 * 2 ** 20
VMEM_LIMIT_MAX = 56 * 2 ** 20
VMEM_LIMIT_MIN = 24 * 2 ** 20


def _nbytes(shape, dtype):
    n = 1
    for s in shape:
        n *= s
    return n * jnp.dtype(dtype).itemsize


def _cparams(n_axes, block_bytes, scratch_bytes=0):
    est = 2 * (2 * block_bytes + scratch_bytes)
    lim = int(min(VMEM_LIMIT_MAX, max(VMEM_LIMIT_MIN, est)))
    return pltpu.CompilerParams(dimension_semantics=("arbitrary",) * n_axes, vmem_limit_bytes=lim)


def _bf(v):
    return v.astype(BF16)


def _dot(a, b):
    return jnp.dot(a, b, preferred_element_type=F32)


def _dot_nt(a, b):
    return lax.dot_general(a, b, (((1,), (1,)), ((), ())), preferred_element_type=F32)


def _dot_tn(a, b):
    return lax.dot_general(a, b, (((0,), (0,)), ((), ())), preferred_element_type=F32)


def _rstd(v):
    return lax.rsqrt(jnp.mean(v * v, axis=-1, keepdims=True) + EPS)


def _norm_bwd(dy, xv, g):
    r = _rstd(xv)
    xh = xv * r
    dxh = dy * g
    dx = r * (dxh - xh * jnp.mean(dxh * xh, axis=-1, keepdims=True))
    return dx, xh


def _place():
    x, y, c = lax.axis_index("x"), lax.axis_index("y"), lax.axis_index("c")
    other_chips = [(1 - x, y), (x, 1 - y), (1 - x, 1 - y)]
    return x, y, c, other_chips


class _Plan:
    ins = ()
    out_shapes = ()
    aliases = {}
    sem_shapes = ()

    def copies(self, ins, outs, sems):
        raise NotImplementedError

    def begin(self, ins, outs, sems):
        for cp in self.copies(ins, outs, sems):
            cp.start()

    def finish(self, ins, outs, sems):
        for cp in self.copies(ins, outs, sems):
            cp.wait()


class _GatherPlan(_Plan):
    def __init__(self, bufs):
        n = len(bufs)
        self.ins = list(bufs)
        self.out_shapes = [_sds(b.shape, b.dtype) for b in bufs]
        self.aliases = {i: i for i in range(n)}
        self.sem_shapes = [pltpu.SemaphoreType.DMA((n, 6)), pltpu.SemaphoreType.DMA((n, 6))]

    def _parts(self, outs, sems):
        send, recv = sems
        x, y, c, chips = _place()

        def half(i, piece, which):
            hr = outs[i].shape[1] // 2
            return outs[i].at[piece, pl.ds(which * hr, hr), :]

        def copy(i, k, ref, to):
            return pltpu.make_async_remote_copy(src_ref=ref, dst_ref=ref, send_sem=send.at[i, k], recv_sem=recv.at[i, k],
                                                device_id=to, device_id_type=MESH)

        return x, y, c, chips, half, copy

    def begin(self, ins, outs, sems):
        x, y, c, chips, half, copy = self._parts(outs, sems)
        for i in range(len(outs)):
            for j, (cx, cy) in enumerate(chips):
                copy(i, j, half(i, 2 * x + y, c), (cx, cy, c)).start()

    def finish(self, ins, outs, sems):
        x, y, c, chips, half, copy = self._parts(outs, sems)
        sib = (x, y, 1 - c)
        n = len(outs)
        for i in range(n):
            for j, (cx, cy) in enumerate(chips):
                piece = half(i, 2 * cx + cy, c)
                copy(i, j, piece, (cx, cy, c)).wait_recv()
                copy(i, 3 + j, piece, sib).start()
        for i in range(n):
            for j, (cx, cy) in enumerate(chips):
                copy(i, 3 + j, half(i, 2 * cx + cy, 1 - c), sib).wait_recv()
        for i in range(n):
            for j, (cx, cy) in enumerate(chips):
                copy(i, j, half(i, 2 * x + y, c), (cx, cy, c)).wait_send()
                copy(i, 3 + j, half(i, 2 * cx + cy, c), sib).wait_send()


class _SwapPlan(_Plan):
    def __init__(self, grads):
        n = len(grads)
        self.ins = list(grads)
        self.out_shapes = [_sds((g.shape[0], g.shape[1] // 2, g.shape[2]), g.dtype) for g in grads]
        self.sem_shapes = [pltpu.SemaphoreType.DMA((n,)), pltpu.SemaphoreType.DMA((n,))]

    def copies(self, ins, outs, sems):
        send, recv = sems
        x, y, c, _ = _place()
        cps = []
        for i in range(len(ins)):
            hr = ins[i].shape[1] // 2
            cps.append(pltpu.make_async_remote_copy(
                src_ref=ins[i].at[:, pl.ds((1 - c) * hr, hr), :], dst_ref=outs[i], send_sem=send.at[i],
                recv_sem=recv.at[i], device_id=(x, y, 1 - c), device_id_type=MESH))
        return cps


class _ExchangePlan(_Plan):
    def __init__(self, sums):
        n = len(sums)
        self.ins = list(sums)
        self.out_shapes = [_sds((N_CHIPS - 1,) + s.shape[1:], s.dtype) for s in sums]
        self.sem_shapes = [pltpu.SemaphoreType.DMA((n, 3)), pltpu.SemaphoreType.DMA((n, 3))]

    def copies(self, ins, outs, sems):
        send, recv = sems
        x, y, c, chips = _place()
        cps = []
        for i in range(len(ins)):
            for j, (cx, cy) in enumerate(chips):
                cps.append(pltpu.make_async_remote_copy(
                    src_ref=ins[i].at[2 * cx + cy], dst_ref=outs[i].at[j], send_sem=send.at[i, j],
                    recv_sem=recv.at[i, j], device_id=(cx, cy, c), device_id_type=MESH))
        return cps


class _JoinPlan(_Plan):
    def __init__(self, shards):
        n = len(shards)
        self.ins = list(shards)
        self.out_shapes = [_sds(s.shape, s.dtype) for s in shards]
        self.aliases = {i: i for i in range(n)}
        self.sem_shapes = [pltpu.SemaphoreType.DMA((n,)), pltpu.SemaphoreType.DMA((n,))]

    def copies(self, ins, outs, sems):
        send, recv = sems
        x, y, c, _ = _place()
        cps = []
        for i in range(len(outs)):
            hr = outs[i].shape[0] // 2
            mine = outs[i].at[pl.ds(c * hr, hr), :]
            cps.append(pltpu.make_async_remote_copy(src_ref=mine, dst_ref=mine, send_sem=send.at[i], recv_sem=recv.at[i],
                                                    device_id=(x, y, 1 - c), device_id_type=MESH))
        return cps


class _PlanList(_Plan):
    def __init__(self, plans):
        self.plans = list(plans)
        self.ins, self.out_shapes, self.sem_shapes, self.aliases = [], [], [], {}
        for p in self.plans:
            self.aliases.update({len(self.ins) + a: len(self.out_shapes) + b for a, b in p.aliases.items()})
            self.ins += list(p.ins)
            self.out_shapes += list(p.out_shapes)
            self.sem_shapes += list(p.sem_shapes)

    def _each(self, ins, outs, sems):
        i = o = s = 0
        for p in self.plans:
            ni, no, ns = len(p.ins), len(p.out_shapes), len(p.sem_shapes)
            yield p, ins[i:i + ni], outs[o:o + no], sems[s:s + ns]
            i, o, s = i + ni, o + no, s + ns

    def begin(self, ins, outs, sems):
        for p, pi, po, ps in self._each(ins, outs, sems):
            p.begin(pi, po, ps)

    def finish(self, ins, outs, sems):
        for p, pi, po, ps in self._each(ins, outs, sems):
            p.finish(pi, po, ps)


def _run_plan(name, plan):
    n_in, n_out = len(plan.ins), len(plan.out_shapes)

    def body(*refs):
        ins, outs, sems = refs[:n_in], refs[n_in:n_in + n_out], refs[n_in + n_out:]
        plan.begin(ins, outs, sems)
        plan.finish(ins, outs, sems)

    return pl.pallas_call(
        body, name=name, in_specs=[ANY] * n_in, out_specs=[ANY] * n_out, out_shape=list(plan.out_shapes),
        input_output_aliases=dict(plan.aliases), scratch_shapes=list(plan.sem_shapes))(*plan.ins)


HBM_PIN_BYTES = 1 << 20


def _sds(shape, dtype):
    if _nbytes(shape, dtype) >= HBM_PIN_BYTES:
        return pltpu.HBM(shape, dtype)
    return jax.ShapeDtypeStruct(shape, dtype)


def _pin(args):
    return [pltpu.with_memory_space_constraint(a, pltpu.HBM) if _nbytes(a.shape, a.dtype) >= HBM_PIN_BYTES else a
            for a in args]


def _pcall(body, *, name, grid, in_specs, out_specs, out_shape, params, args, scratch_shapes=(), plan=None):
    in_specs, out_specs, out_shape, scratch = list(in_specs), list(out_specs), list(out_shape), list(scratch_shapes)
    args = _pin(args)
    if plan is None:
        res = pl.pallas_call(body, name=name, grid=grid, in_specs=in_specs, out_specs=out_specs, out_shape=out_shape,
                             scratch_shapes=scratch, compiler_params=params)(*args)
        return list(res), []
    ni, no, ns = len(in_specs), len(out_specs), len(scratch)
    pi, po = len(plan.ins), len(plan.out_shapes)

    def wrapped(*refs):
        ins, pins = refs[:ni], refs[ni:ni + pi]
        outs, pouts = refs[ni + pi:ni + pi + no], refs[ni + pi + no:ni + pi + no + po]
        scr, psems = refs[ni + pi + no + po:ni + pi + no + po + ns], refs[ni + pi + no + po + ns:]
        ids = [pl.program_id(a) for a in range(len(grid))]
        first = functools.reduce(jnp.logical_and, [i == 0 for i in ids])
        last = functools.reduce(jnp.logical_and, [i == g - 1 for i, g in zip(ids, grid)])

        @pl.when(first)
        def _():
            plan.begin(pins, pouts, psems)

        body(*ins, *outs, *scr)

        @pl.when(last)
        def _():
            plan.finish(pins, pouts, psems)

    res = pl.pallas_call(
        wrapped, name=name, grid=grid, in_specs=in_specs + [ANY] * pi, out_specs=out_specs + [ANY] * po,
        out_shape=out_shape + list(plan.out_shapes), scratch_shapes=scratch + list(plan.sem_shapes),
        input_output_aliases={ni + a: no + b for a, b in plan.aliases.items()},
        compiler_params=params)(*args, *plan.ins)
    return list(res[:no]), list(res[no:])


def _mm_nn(name, a, w3, *, tm, norm_g=None, residual=None, out_dtype=F32, plan=None):
    M, K = a.shape
    P, Kw, C = w3.shape
    assert Kw == K and M % tm == 0
    N = P * C
    has_norm = norm_g is not None
    has_res = residual is not None

    def body(*refs):
        refs = list(refs)
        a_ref = refs.pop(0)
        w_ref = refs.pop(0)
        g_ref = refs.pop(0) if has_norm else None
        r_ref = refs.pop(0) if has_res else None
        o_ref = refs.pop(0)
        xn_ref = refs.pop(0) if has_norm else None
        if has_norm:
            av = a_ref[...].astype(F32)
            ab = _bf(av * _rstd(av) * g_ref[...])
            xn_ref[...] = ab
        else:
            ab = _bf(a_ref[...])
        for p in range(P):
            acc = _dot(ab, w_ref[p])
            if has_res:
                acc = acc + r_ref[:, p * C:(p + 1) * C]
            o_ref[:, p * C:(p + 1) * C] = acc.astype(o_ref.dtype)

    in_specs = [pl.BlockSpec((tm, K), lambda i: (i, 0)), pl.BlockSpec((P, K, C), lambda i: (0, 0, 0))]
    args = [a, w3]
    if has_norm:
        in_specs.append(pl.BlockSpec((1, K), lambda i: (0, 0)))
        args.append(norm_g)
    if has_res:
        in_specs.append(pl.BlockSpec((tm, N), lambda i: (i, 0)))
        args.append(residual)
    out_shape = [_sds((M, N), out_dtype)]
    out_specs = [pl.BlockSpec((tm, N), lambda i: (i, 0))]
    if has_norm:
        out_shape.append(_sds((M, K), BF16))
        out_specs.append(pl.BlockSpec((tm, K), lambda i: (i, 0)))
    blk = (_nbytes((tm, K), a.dtype) + _nbytes((P, K, C), BF16) + 2 * _nbytes((tm, N), F32)
           + _nbytes((tm, K), BF16))
    res, got = _pcall(body, name=name, grid=(M // tm,), in_specs=in_specs, out_specs=out_specs, out_shape=out_shape,
                      params=_cparams(1, blk), args=args, plan=plan)
    res = res if has_norm else res[0]
    return res if plan is None else (res, got)


MXU_COLS_V7X = 256


def _col_chunks(n):
    return [(c, min(c + MXU_COLS_V7X, n)) for c in range(0, n, MXU_COLS_V7X)]


def _ffn_up(name, h, wgu3, g, *, tm):
    M, K = h.shape
    P, _, C = wgu3.shape
    half = P // 2

    def body(h_ref, wg_ref, wu_ref, g_ref, act_ref, fac_ref, xn_ref, ab_ref):
        @pl.when(pl.program_id(1) == 0)
        def _():
            hv = h_ref[...]
            xn = _bf(hv * _rstd(hv) * g_ref[...])
            ab_ref[...] = xn
            xn_ref[...] = xn
        ab = ab_ref[...]
        for c0, c1 in _col_chunks(C):
            gate = _dot(ab, wg_ref[:, c0:c1])
            up = _dot(ab, wu_ref[:, c0:c1])
            sig = 1.0 / (1.0 + jnp.exp(-gate))
            silu = gate * sig
            act_ref[:, c0:c1] = _bf(silu * up)
            fac_ref[0, :, c0:c1] = _bf(up * (sig + silu * (1.0 - sig)))
            fac_ref[1, :, c0:c1] = _bf(silu)

    blk = (_nbytes((tm, K), F32) + 2 * _nbytes((K, C), BF16) + 3 * _nbytes((tm, C), BF16)
           + 2 * _nbytes((tm, C), F32) + _nbytes((tm, K), BF16))
    return pl.pallas_call(
        body, name=name, grid=(M // tm, half),
        in_specs=[pl.BlockSpec((tm, K), lambda i, j: (i, 0)),
                  pl.BlockSpec((None, K, C), lambda i, j: (j, 0, 0)),
                  pl.BlockSpec((None, K, C), lambda i, j: (j + half, 0, 0)),
                  pl.BlockSpec((1, K), lambda i, j: (0, 0))],
        out_specs=[pl.BlockSpec((tm, C), lambda i, j: (i, j)),
                   pl.BlockSpec((2, tm, C), lambda i, j: (0, i, j)),
                   pl.BlockSpec((tm, K), lambda i, j: (i, 0))],
        out_shape=[_sds((M, half * C), BF16),
                   _sds((2, M, half * C), BF16),
                   _sds((M, K), BF16)],
        scratch_shapes=[pltpu.VMEM((tm, K), BF16)],
        compiler_params=_cparams(2, blk, _nbytes((tm, K), BF16)))(*_pin([h, wgu3, wgu3, g]))


def _ffn_down_loss(name, act, wd3, h, target, *, tm):
    M, K = act.shape
    _, _, N = wd3.shape

    def body(a_ref, w_ref, h_ref, t_ref, dy_ref, ls_ref):
        err = _dot(a_ref[...], w_ref[...]) + h_ref[...] - t_ref[...]
        dy_ref[...] = err * (1.0 / N)

        @pl.when(pl.program_id(0) == 0)
        def _():
            ls_ref[...] = jnp.zeros_like(ls_ref)
        ls_ref[...] += jnp.sum(err * err, axis=0, keepdims=True)

    blk = _nbytes((tm, K), BF16) + _nbytes((K, N), BF16) + 4 * _nbytes((tm, N), F32)
    return pl.pallas_call(
        body, name=name, grid=(M // tm,),
        in_specs=[pl.BlockSpec((tm, K), lambda i: (i, 0)),
                  pl.BlockSpec((None, K, N), lambda i: (0, 0, 0)),
                  pl.BlockSpec((tm, N), lambda i: (i, 0)),
                  pl.BlockSpec((tm, N), lambda i: (i, 0))],
        out_specs=[pl.BlockSpec((tm, N), lambda i: (i, 0)),
                   pl.BlockSpec((1, N), lambda i: (0, 0))],
        out_shape=[_sds((M, N), F32), _sds((1, N), F32)],
        compiler_params=_cparams(1, blk))(*_pin([act, wd3, h, target]))


def _nt_pieces(a_ref, w_ref):
    P, _, C = w_ref.shape
    acc = _dot_nt(_bf(a_ref[:, 0:C]), w_ref[0])
    for p in range(1, P):
        acc = acc + _dot_nt(_bf(a_ref[:, p * C:(p + 1) * C]), w_ref[p])
    return acc


def _mm_nt(name, a, w3, *, tm, out_dtype):
    M, N = a.shape
    P, Ko, C = w3.shape
    assert N == P * C and M % tm == 0

    def body(a_ref, w_ref, o_ref):
        o_ref[...] = _nt_pieces(a_ref, w_ref).astype(o_ref.dtype)

    blk = _nbytes((tm, N), a.dtype) + _nbytes((P, Ko, C), BF16) + 2 * _nbytes((tm, Ko), F32)
    return pl.pallas_call(
        body, name=name, grid=(M // tm,),
        in_specs=[pl.BlockSpec((tm, N), lambda i: (i, 0)), pl.BlockSpec((P, Ko, C), lambda i: (0, 0, 0))],
        out_specs=pl.BlockSpec((tm, Ko), lambda i: (i, 0)),
        out_shape=_sds((M, Ko), out_dtype),
        compiler_params=_cparams(1, blk))(*_pin([a, w3]))


def _mm_nt_normbwd(name, a, w3, h, g, dres, *, tm, plan=None):
    M, D = h.shape
    P, Ko, C = w3.shape
    N = a.shape[1]
    assert N == P * C and Ko == D and M % tm == 0
    has_res = dres is not None

    def body(*refs):
        a_ref, w_ref, h_ref, g_ref = refs[:4]
        r_ref = refs[4] if has_res else None
        dx_ref, dg_ref = refs[-2:]
        dhn = _nt_pieces(a_ref, w_ref)
        dx, xh = _norm_bwd(dhn, h_ref[...], g_ref[...])
        if has_res:
            dx = dx + r_ref[...]
        dx_ref[...] = dx

        @pl.when(pl.program_id(0) == 0)
        def _():
            dg_ref[...] = jnp.zeros_like(dg_ref)
        dg_ref[...] += jnp.sum(dhn * xh, axis=0, keepdims=True)

    row = pl.BlockSpec((tm, D), lambda i: (i, 0))
    vec = pl.BlockSpec((1, D), lambda i: (0, 0))
    in_specs = [pl.BlockSpec((tm, N), lambda i: (i, 0)), pl.BlockSpec((P, Ko, C), lambda i: (0, 0, 0)), row, vec]
    args = [a, w3, h, g]
    if has_res:
        in_specs.append(row)
        args.append(dres)
    blk = _nbytes((tm, N), a.dtype) + _nbytes((P, Ko, C), BF16) + 5 * _nbytes((tm, D), F32)
    res, got = _pcall(body, name=name, grid=(M // tm,), in_specs=in_specs, out_specs=[row, vec],
                      out_shape=[_sds((M, D), F32), _sds((1, D), F32)],
                      params=_cparams(1, blk), args=args, plan=plan)
    return res if plan is None else (res, got)


def _ffn_bwd(name, dy, wd3, fac, wgu3, h, g, *, tm):
    M, D = h.shape
    P, _, C = wgu3.shape
    F = fac.shape[2]
    per = F // C

    def body(dy_ref, wd_ref, f_ref, w_ref, h_ref, g_ref, dgu_ref, dx_ref, dg_ref):
        dyv = dy_ref[...]
        ab = _bf(dyv)
        dhn = None
        for pc in range(per):
            cols = slice(pc * C, (pc + 1) * C)
            da = _dot_nt(ab, wd_ref[cols, :])
            dgate = _bf(da * f_ref[0, :, cols].astype(F32))
            dup = _bf(da * f_ref[1, :, cols].astype(F32))
            dgu_ref[0, :, cols] = dgate
            dgu_ref[1, :, cols] = dup
            part = _dot_nt(dgate, w_ref[pc]) + _dot_nt(dup, w_ref[per + pc])
            dhn = part if dhn is None else dhn + part
        dx, xh = _norm_bwd(dhn, h_ref[...], g_ref[...])
        dx_ref[...] = dx + dyv

        @pl.when(pl.program_id(0) == 0)
        def _():
            dg_ref[...] = jnp.zeros_like(dg_ref)
        dg_ref[...] += jnp.sum(dhn * xh, axis=0, keepdims=True)

    row = pl.BlockSpec((tm, D), lambda i: (i, 0))
    vec = pl.BlockSpec((1, D), lambda i: (0, 0))
    planes = pl.BlockSpec((2, tm, F), lambda i: (0, i, 0))
    blk = (2 * _nbytes((2, tm, F), BF16) + (_nbytes((F, D), BF16) + _nbytes((P, D, C), BF16)) // 2
           + 4 * _nbytes((tm, D), F32))
    return pl.pallas_call(
        body, name=name, grid=(M // tm,),
        in_specs=[row, pl.BlockSpec((None, F, D), lambda i: (0, 0, 0), pipeline_mode=pl.Buffered(1)), planes,
                  pl.BlockSpec((P, D, C), lambda i: (0, 0, 0), pipeline_mode=pl.Buffered(1)), row, vec],
        out_specs=[planes, row, vec],
        out_shape=[_sds((2, M, F), BF16), _sds((M, D), F32), _sds((1, D), F32)],
        compiler_params=_cparams(1, blk))(*_pin([dy, wd3, fac, wgu3, h, g]))


def _mm_tn_wide(name, a, b, pieces, *, ts, plan=None):
    S, K = a.shape
    N = b.shape[1]
    C = N // pieces

    def body(a_ref, b_ref, o_ref):
        @pl.when(pl.program_id(0) == 0)
        def _():
            o_ref[...] = jnp.zeros_like(o_ref)
        acc = _dot_tn(_bf(a_ref[...]), _bf(b_ref[...]))
        for p in range(pieces):
            o_ref[p] += acc[:, p * C:(p + 1) * C]

    blk = _nbytes((ts, K), a.dtype) + _nbytes((ts, N), b.dtype) + 2 * _nbytes((K, N), F32)
    res, got = _pcall(body, name=name, grid=(S // ts,),
                      in_specs=[pl.BlockSpec((ts, K), lambda s: (s, 0)), pl.BlockSpec((ts, N), lambda s: (s, 0))],
                      out_specs=[pl.BlockSpec((pieces, K, C), lambda s: (0, 0, 0))],
                      out_shape=[_sds((pieces, K, C), F32)], params=_cparams(1, blk), args=[a, b], plan=plan)
    return res[0] if plan is None else (res[0], got)


def _mm_tn(name, a, a_spec, a_blk, b, b_spec, b_blk, out3, o_spec, o_blk, grid, plan=None):
    def body(a_ref, b_ref, o_ref):
        @pl.when(pl.program_id(2) == 0)
        def _():
            o_ref[...] = jnp.zeros_like(o_ref)
        o_ref[...] += _dot_tn(_bf(a_ref[...]), _bf(b_ref[...]))

    blk = _nbytes(a_blk, a.dtype) + _nbytes(b_blk, b.dtype) + 2 * _nbytes(o_blk, F32)
    res, got = _pcall(body, name=name, grid=grid, in_specs=[a_spec, b_spec], out_specs=[o_spec],
                      out_shape=[_sds(out3, F32)], params=_cparams(3, blk), args=[a, b], plan=plan)
    return res[0] if plan is None else (res[0], got)


def _swap_halves(v, lane, masked):
    up = pltpu.roll(v, HEAD - ROT_HALF, 1)
    down = pltpu.roll(v, ROT_HALF, 1)
    rest = jnp.where(lane < ROT_DIM, down, 0.0) if masked else down
    return jnp.where(lane < ROT_HALF, up, rest)


def _in_proj(name, x, w3, g, ctab, stab, qg, kg, *, tm, plan=None):
    S, K = x.shape
    P, _, C = w3.shape

    def body(x_ref, w_ref, g_ref, c_ref, s_ref, qg_ref, kg_ref, p_ref, xn_ref, q_ref, k_ref):
        xv = x_ref[...]
        ab = _bf(xv * _rstd(xv) * g_ref[...])
        xn_ref[...] = ab
        for p in range(P):
            p_ref[:, p * C:(p + 1) * C] = _dot(ab, w_ref[p])
        lane = lax.broadcasted_iota(jnp.int32, (tm, HEAD), 1)
        cv = c_ref[...]
        sv = s_ref[...]

        def prep(t, gain, scale):
            n = t * _rstd(t) * gain
            return (n * cv + _swap_halves(n, lane, False) * sv) * scale

        for j in range(Q_W // HEAD):
            q_ref[:, j * HEAD:(j + 1) * HEAD] = prep(p_ref[:, j * HEAD:(j + 1) * HEAD], qg_ref[...], SCALE)
        for j in range(KV_W // HEAD):
            k_ref[:, j * HEAD:(j + 1) * HEAD] = prep(p_ref[:, Q_W + j * HEAD:Q_W + (j + 1) * HEAD], kg_ref[...], 1.0)

    row = lambda width: pl.BlockSpec((tm, width), lambda i: (i, 0))
    vec = lambda width: pl.BlockSpec((1, width), lambda i: (0, 0))
    blk = (_nbytes((tm, K), F32) + _nbytes((P, K, C), BF16) + _nbytes((tm, P * C + Q_W + KV_W + 2 * HEAD), F32)
           + _nbytes((tm, K), BF16))
    res, got = _pcall(
        body, name=name, grid=(S // tm,),
        in_specs=[row(K), pl.BlockSpec((P, K, C), lambda i: (0, 0, 0)), vec(K), row(HEAD), row(HEAD), vec(HEAD), vec(HEAD)],
        out_specs=[row(P * C), row(K), row(Q_W), row(KV_W)],
        out_shape=[_sds((S, P * C), F32), _sds((S, K), BF16), _sds((S, Q_W), F32), _sds((S, KV_W), F32)],
        params=_cparams(1, blk), args=[x, w3, g, ctab, stab, qg, kg], plan=plan)
    return res if plan is None else (res, got)


ATTN_STEP = 2048


def _row_idx(start, dil):
    return pl.ds(start, BLOCK) if dil == 1 else pl.ds(start, BLOCK, stride=dil)


def _rows(ref, start, dil):
    return ref[_row_idx(start, dil), :]


def _set_rows(ref, start, dil, val):
    ref[_row_idx(start, dil), :] = val


def _band_mask(first):
    qi = lax.broadcasted_iota(jnp.int32, (BLOCK, 2 * BLOCK), 0)
    kj = lax.broadcasted_iota(jnp.int32, (BLOCK, 2 * BLOCK), 1)
    band = (kj >= qi) & (kj <= qi + BLOCK)
    if first is None:
        return band
    return band & ((kj >= BLOCK) | jnp.logical_not(first))


def _attn_geometry(S, grp):
    dil = DILATIONS[grp]
    bt = BLOCK * dil
    assert S % ATTN_STEP == 0 and ATTN_STEP % bt == 0
    return dil, bt, ATTN_STEP // bt, S // ATTN_STEP


MERGE_ROWS = 256


def _attn_fwd(name, qn, kn, proj, plan=None):
    S = qn.shape[0]
    groups = range(len(DILATIONS))
    geo = [_attn_geometry(S, g) for g in groups]
    nsb = geo[0][3]
    vcol = (Q_W + KV_W) // HEAD

    def body(q0, q1, q2, kc_ref, vc_ref, kp0, kp1, kp2, vp0, vp1, vp2, mix_ref, lse_ref, o_s, l_s):
        q_refs, kp_refs, vp_refs = (q0, q1, q2), (kp0, kp1, kp2), (vp0, vp1, vp2)
        valid_first = _band_mask(pl.program_id(1) == 0)
        valid_inner = _band_mask(None)
        for g in groups:
            dil, bt, nsub, _ = geo[g]
            o_g, l_g = o_s.at[g], l_s.at[g]
            for r in range(dil):
                kprev, vprev = _bf(_rows(kp_refs[g], r, dil)), _bf(_rows(vp_refs[g], r, dil))
                for b in range(nsub):
                    at = b * bt + r
                    kcur, vcur = _bf(_rows(kc_ref, at, dil)), _bf(_rows(vc_ref, at, dil))
                    q = _bf(_rows(q_refs[g], at, dil))
                    k2 = jnp.concatenate([kprev, kcur], axis=0)
                    v2 = jnp.concatenate([vprev, vcur], axis=0)
                    s = jnp.where(valid_first if b == 0 else valid_inner, _dot_nt(q, k2), NEG_INF)
                    m = jnp.max(s, axis=-1, keepdims=True)
                    p = jnp.exp(s - m)
                    l = jnp.sum(p, axis=-1, keepdims=True)
                    acc = _dot(_bf(p), v2)
                    _set_rows(o_g, at, dil, acc / l)
                    _set_rows(l_g, at, dil, jnp.broadcast_to(m + jnp.log(l), (BLOCK, HEAD)))
                    kprev, vprev = kcur, vcur
        for c in range(ATTN_STEP // MERGE_ROWS):
            rows = pl.ds(c * MERGE_ROWS, MERGE_ROWS)
            ls = [l_s[g, rows, :] for g in groups]
            m = jnp.maximum(jnp.maximum(ls[0], ls[1]), ls[2])
            ws = [jnp.exp(v - m) for v in ls]
            den = ws[0] + ws[1] + ws[2]
            num = ws[0] * o_s[0, rows, :] + ws[1] * o_s[1, rows, :] + ws[2] * o_s[2, rows, :]
            mix_ref[rows, :] = _bf(num / den)
            lse_ref[rows, :] = m + jnp.log(den)

    def big(col):
        return pl.BlockSpec((ATTN_STEP, HEAD), lambda h, n: (n, col + h))

    def tail(g, col):
        _, bt, nsub, _ = geo[g]
        return pl.BlockSpec((bt, HEAD), lambda h, n: (jnp.maximum(n * nsub - 1, 0), col + h))

    blk = 7 * _nbytes((ATTN_STEP, HEAD), F32) + 2 * sum(_nbytes((geo[g][1], HEAD), F32) for g in groups)
    scratch = 2 * len(DILATIONS) * _nbytes((ATTN_STEP, HEAD), F32)
    res, got = _pcall(
        body, name=name, grid=(N_HEADS, nsb),
        in_specs=[big(g * N_HEADS) for g in groups] + [big(0), big(vcol)]
                 + [tail(g, 0) for g in groups] + [tail(g, vcol) for g in groups],
        out_specs=[big(0), big(0)],
        out_shape=[_sds((S, 2 * KV_W), BF16), _sds((S, KV_W), F32)],
        scratch_shapes=[pltpu.VMEM((len(DILATIONS), ATTN_STEP, HEAD), F32)] * 2,
        params=_cparams(2, blk, scratch), args=[qn, qn, qn, kn, proj, kn, kn, kn, proj, proj, proj], plan=plan)
    return res if plan is None else (res, got)


def _attn_bwd(name, qn, kn, proj, dattn, lse, delta, grp, plan=None):
    S = qn.shape[0]
    dil, bt, nsub, nsb = _attn_geometry(S, grp)
    vcol = (Q_W + KV_W) // HEAD

    def slot(b, r):
        return pl.ds((b * dil + r) * BLOCK, BLOCK)

    def body(q_ref, kp_ref, kc_ref, vp_ref, vc_ref, do_ref, l_ref, d_ref, dq_ref, dk_ref, dv_ref, ck_ref, cv_ref):
        n = pl.program_id(1)
        par = n % 2

        @pl.when(n < nsb)
        def _():
            valid_first = _band_mask(n == 0)
            valid_inner = _band_mask(None)
            ck, cv = ck_ref.at[par], cv_ref.at[par]
            pk, pv = ck_ref.at[1 - par], cv_ref.at[1 - par]
            for r in range(dil):
                kprev, vprev = _bf(_rows(kp_ref, r, dil)), _bf(_rows(vp_ref, r, dil))
                own_k = own_v = None
                for b in range(nsub):
                    at = b * bt + r
                    kcur, vcur = _bf(_rows(kc_ref, at, dil)), _bf(_rows(vc_ref, at, dil))
                    q = _bf(_rows(q_ref, at, dil))
                    k2 = jnp.concatenate([kprev, kcur], axis=0)
                    v2 = jnp.concatenate([vprev, vcur], axis=0)
                    do = _bf(_rows(do_ref, at, dil))
                    lse_r = _rows(l_ref, at, dil)[:, :1]
                    del_r = _rows(d_ref, at, dil)[:, :1]
                    s = jnp.where(valid_first if b == 0 else valid_inner, _dot_nt(q, k2), NEG_INF)
                    p = jnp.exp(s - lse_r)
                    ds = _bf(p * (_dot_nt(do, v2) - del_r))
                    _set_rows(dq_ref, at, dil, _dot(ds, k2))
                    dk2 = _dot_tn(ds, q)
                    dv2 = _dot_tn(_bf(p), do)
                    if b > 0:
                        ck[slot(b - 1, r), :] = own_k + dk2[:BLOCK]
                        cv[slot(b - 1, r), :] = own_v + dv2[:BLOCK]
                    else:
                        @pl.when(n > 0)
                        def _():
                            pk[slot(nsub - 1, r), :] += dk2[:BLOCK]
                            pv[slot(nsub - 1, r), :] += dv2[:BLOCK]
                    own_k, own_v = dk2[BLOCK:], dv2[BLOCK:]
                    kprev, vprev = kcur, vcur
                ck[slot(nsub - 1, r), :] = own_k
                cv[slot(nsub - 1, r), :] = own_v

        @pl.when(n > 0)
        def _():
            for r in range(dil):
                for b in range(nsub):
                    _set_rows(dk_ref, b * bt + r, dil, ck_ref[1 - par, slot(b, r), :])
                    _set_rows(dv_ref, b * bt + r, dil, cv_ref[1 - par, slot(b, r), :])

    cur = lambda n: jnp.minimum(n, nsb - 1)
    prev = lambda n: jnp.maximum(n - 1, 0)
    tail_at = lambda n: jnp.maximum(cur(n) * nsub - 1, 0)
    big = lambda col: pl.BlockSpec((ATTN_STEP, HEAD), lambda h, n: (cur(n), col(h)))
    tail = lambda col: pl.BlockSpec((bt, HEAD), lambda h, n: (tail_at(n), col(h)))
    late = pl.BlockSpec((ATTN_STEP, HEAD), lambda h, n: (prev(n), h))
    blk = 9 * _nbytes((ATTN_STEP, HEAD), F32) + 2 * _nbytes((bt, HEAD), F32)
    res, got = _pcall(
        body, name=name, grid=(N_HEADS, nsb + 1),
        in_specs=[big(lambda h: grp * N_HEADS + h), tail(lambda h: h), big(lambda h: h),
                  tail(lambda h: vcol + h), big(lambda h: vcol + h),
                  big(lambda h: h), big(lambda h: h), big(lambda h: h)],
        out_specs=[big(lambda h: h), late, late],
        out_shape=[_sds((S, KV_W), F32)] * 3,
        scratch_shapes=[pltpu.VMEM((2, ATTN_STEP, HEAD), F32), pltpu.VMEM((2, ATTN_STEP, HEAD), F32)],
        params=_cparams(2, blk, 4 * _nbytes((ATTN_STEP, HEAD), F32)),
        args=[qn, kn, kn, proj, proj, dattn, lse, delta], plan=plan)
    return res if plan is None else (res, got)


def _window_sum(v, n_doublings, back):
    rows = v.shape[0]
    step = 1
    for _ in range(n_doublings):
        v = v + pltpu.roll(v, step if back else rows - step, 0)
        step *= 2
    return v


def _pool_fwd(name, mix, proj, pool_w, pool_scale, *, tm):
    S = proj.shape[0]
    ucol = (IN_W - POOL_W) // POOL_W
    hpt = tm // POOL_HALO

    def body(mix_in, u_ref, uh_ref, pw_ref, ps_ref, pooled_ref, dp_ref):
        i = pl.program_id(0)
        halo = jnp.where(i == 0, 0.0, uh_ref[...])
        t = lax.broadcasted_iota(jnp.int32, (tm + POOL_HALO, HEAD), 0) + (i * tm - POOL_HALO)
        for g, w in enumerate(POOL_WINDOWS):
            sl = slice(g * HEAD, (g + 1) * HEAD)
            ub = jnp.concatenate([halo[:, sl], u_ref[:, sl]], axis=0)
            cnt = jnp.minimum(t + 1, w).astype(F32)
            d = (_window_sum(ub, g + 1, True) / cnt - ub)[POOL_HALO:]
            db = _bf(d)
            dp_ref[:, sl] = db
            pooled_ref[:, sl] = _bf(_dot(db, pw_ref[g]) * ps_ref[:, sl])

    blk = 3 * _nbytes((tm, POOL_W), F32) + 2 * _nbytes((tm, POOL_W), BF16)
    return pl.pallas_call(
        body, name=name, grid=(S // tm,),
        in_specs=[ANY,
                  pl.BlockSpec((tm, POOL_W), lambda i: (i, ucol)),
                  pl.BlockSpec((POOL_HALO, POOL_W), lambda i: (jnp.maximum(i * hpt - 1, 0), ucol)),
                  pl.BlockSpec((len(POOL_WINDOWS), HEAD, HEAD), lambda i: (0, 0, 0)),
                  pl.BlockSpec((1, POOL_W), lambda i: (0, 0))],
        out_specs=[pl.BlockSpec((tm, POOL_W), lambda i: (i, 1)), pl.BlockSpec((tm, POOL_W), lambda i: (i, 0))],
        out_shape=[_sds((S, 2 * KV_W), BF16), _sds((S, POOL_W), BF16)],
        input_output_aliases={0: 0},
        compiler_params=_cparams(1, blk))(*_pin([mix, proj, proj, pool_w, pool_scale]))


def _mix_bwd(name, dh, w_out3, mix, dpool, pool_w, pool_scale, *, tm, plan=None):
    S, D = dh.shape
    hpt = tm // POOL_HALO
    last_halo = S // POOL_HALO - 1
    n_tiles = S // tm

    def body(dh_ref, dhn_ref, w_ref, at_ref, dp_ref, pw_ref, ps_ref, da_ref, dl_ref, du_ref, gw_ref, gs_ref):
        i = pl.program_id(0)

        @pl.when(i == 0)
        def _():
            gw_ref[...] = jnp.zeros_like(gw_ref)
            gs_ref[...] = jnp.zeros_like(gs_ref)

        dm = _dot_nt(_bf(dh_ref[...]), w_ref[...])
        ahead = _dot_nt(_bf(dhn_ref[...]), w_ref[KV_W:, :])
        for h in range(N_HEADS):
            sl = slice(h * HEAD, (h + 1) * HEAD)
            da = dm[:, sl]
            da_ref[:, sl] = da
            dl_ref[:, sl] = jnp.broadcast_to(
                jnp.sum(da * at_ref[:, sl].astype(F32), axis=-1, keepdims=True), (tm, HEAD))
        halo = jnp.where(i == n_tiles - 1, 0.0, ahead)
        t = lax.broadcasted_iota(jnp.int32, (tm + POOL_HALO, HEAD), 0) + i * tm
        for g, w in enumerate(POOL_WINDOWS):
            sl = slice(g * HEAD, (g + 1) * HEAD)
            dy = jnp.concatenate([dm[:, KV_W + g * HEAD:KV_W + (g + 1) * HEAD], halo[:, sl]], axis=0)
            dys = _bf(dy * ps_ref[:, sl])
            dd = _dot_nt(dys, pw_ref[g])
            cnt = jnp.minimum(t + 1, w).astype(F32)
            du_ref[:, sl] = (_window_sum(dd / cnt, g + 1, False) - dd)[:tm]
            db = dp_ref[:, sl]
            gw_ref[g] += _dot_tn(db, dys[:tm])
            gs_ref[:, sl] += jnp.sum(dy[:tm] * _dot(db, pw_ref[g]), axis=0, keepdims=True)

    tile = pl.BlockSpec((tm, KV_W), lambda i: (i, 0))
    blk = 2 * _nbytes((tm, D), F32) + _nbytes((2 * KV_W, D), BF16) + 6 * _nbytes((tm, KV_W), F32)
    res, got = _pcall(
        body, name=name, grid=(n_tiles,),
        in_specs=[pl.BlockSpec((tm, D), lambda i: (i, 0)),
                  pl.BlockSpec((POOL_HALO, D), lambda i: (jnp.minimum((i + 1) * hpt, last_halo), 0)),
                  pl.BlockSpec((None, 2 * KV_W, D), lambda i: (0, 0, 0)),
                  tile, tile,
                  pl.BlockSpec((len(POOL_WINDOWS), HEAD, HEAD), lambda i: (0, 0, 0)),
                  pl.BlockSpec((1, POOL_W), lambda i: (0, 0))],
        out_specs=[tile, tile, tile,
                   pl.BlockSpec((len(POOL_WINDOWS), HEAD, HEAD), lambda i: (0, 0, 0)),
                   pl.BlockSpec((1, POOL_W), lambda i: (0, 0))],
        out_shape=[_sds((S, KV_W), F32)] * 3 + [
            _sds((len(POOL_WINDOWS), HEAD, HEAD), F32), _sds((1, POOL_W), F32)],
        params=_cparams(1, blk), args=[dh, dh, w_out3, mix, dpool, pool_w, pool_scale], plan=plan)
    return res if plan is None else (res, got)


def _in_proj_bwd(name, dqs, dks, dvs, du, proj, ctab, stab, qg, kg, w3, x, g, dres, *, tm):
    S, D = x.shape
    width = Q_W + KV_W

    def body(dq0, dq1, dq2, dk0, dk1, dk2, dv0, dv1, dv2, du_ref, p_ref, c_ref, s_ref, qg_ref, kg_ref,
             w_ref, x_ref, g_ref, r_ref, dp_ref, dx_ref, gq_ref, gk_ref, gn_ref):
        @pl.when(pl.program_id(0) == 0)
        def _():
            gq_ref[...] = jnp.zeros_like(gq_ref)
            gk_ref[...] = jnp.zeros_like(gk_ref)
            gn_ref[...] = jnp.zeros_like(gn_ref)

        lane = lax.broadcasted_iota(jnp.int32, (tm, HEAD), 1)
        cv = c_ref[...]
        sv = s_ref[...]

        def back(dy, t, g, scale):
            dy = dy * scale
            dn = dy * cv + _swap_halves(dy * sv, lane, True)
            dt, xh = _norm_bwd(dn, t, g)
            return dt, jnp.sum(dn * xh, axis=0, keepdims=True)

        dqr = (dq0, dq1, dq2)
        gq = jnp.zeros((1, HEAD), F32)
        for j in range(Q_W // HEAD):
            grp, h = divmod(j, N_HEADS)
            dt, gj = back(dqr[grp][:, h * HEAD:(h + 1) * HEAD], p_ref[:, j * HEAD:(j + 1) * HEAD], qg_ref[...], SCALE)
            dp_ref[:, j * HEAD:(j + 1) * HEAD] = _bf(dt)
            gq = gq + gj
        gq_ref[...] += gq
        gk = jnp.zeros((1, HEAD), F32)
        for h in range(N_HEADS):
            sl = slice(h * HEAD, (h + 1) * HEAD)
            dt, gj = back(dk0[:, sl] + dk1[:, sl] + dk2[:, sl], p_ref[:, Q_W + h * HEAD:Q_W + (h + 1) * HEAD],
                          kg_ref[...], 1.0)
            dp_ref[:, Q_W + h * HEAD:Q_W + (h + 1) * HEAD] = _bf(dt)
            gk = gk + gj
        gk_ref[...] += gk
        dp_ref[:, width:width + KV_W] = _bf(dv0[...] + dv1[...] + dv2[...])
        dp_ref[:, width + KV_W:] = _bf(du_ref[...])
        dhn = _nt_pieces(dp_ref, w_ref)
        dx, xh = _norm_bwd(dhn, x_ref[...], g_ref[...])
        dx_ref[...] = dx + r_ref[...]
        gn_ref[...] += jnp.sum(dhn * xh, axis=0, keepdims=True)

    tile = pl.BlockSpec((tm, KV_W), lambda i: (i, 0))
    vec = pl.BlockSpec((1, HEAD), lambda i: (0, 0))
    rot = pl.BlockSpec((tm, HEAD), lambda i: (i, 0))
    row = pl.BlockSpec((tm, D), lambda i: (i, 0))
    wide = pl.BlockSpec((1, D), lambda i: (0, 0))
    blk = (10 * _nbytes((tm, KV_W), F32) + _nbytes((tm, width), F32) + _nbytes((tm, IN_W), BF16)
           + _nbytes(w3.shape, BF16) // 2 + 3 * _nbytes((tm, D), F32))
    return pl.pallas_call(
        body, name=name, grid=(S // tm,),
        in_specs=[tile] * 10 + [pl.BlockSpec((tm, width), lambda i: (i, 0)), rot, rot, vec, vec,
                                pl.BlockSpec(w3.shape, lambda i: (0, 0, 0), pipeline_mode=pl.Buffered(1)),
                                row, wide, row],
        out_specs=[pl.BlockSpec((tm, IN_W), lambda i: (i, 0)), row, vec, vec, wide],
        out_shape=[_sds((S, IN_W), BF16), _sds((S, D), F32), _sds((1, HEAD), F32), _sds((1, HEAD), F32),
                   _sds((1, D), F32)],
        compiler_params=_cparams(1, blk))(
            *_pin([*dqs, *dks, *dvs, du, proj, ctab, stab, qg, kg, w3, x, g, dres]))


def _cross_heads(q_ref, kv_ref, qg, kg, h):
    sl = slice(h * HEAD, (h + 1) * HEAD)
    qr = q_ref[:, sl]
    kr = kv_ref[:, sl]
    qh = qr * _rstd(qr) * qg * SCALE
    kh = kr * _rstd(kr) * kg
    vh = kv_ref[:, X_W + h * HEAD:X_W + (h + 1) * HEAD]
    return qr, _bf(qh), _bf(kh), _bf(vh)


def _cross_fwd(name, qraw, kv, qg, kg, *, tm):
    S = qraw.shape[0]
    M = kv.shape[0]

    def body(q_ref, kv_ref, qg_ref, kg_ref, o_ref):
        for h in range(N_HEADS):
            _, qh, kh, vh = _cross_heads(q_ref, kv_ref, qg_ref[...], kg_ref[...], h)
            s = _dot_nt(qh, kh)
            p = jnp.exp(s - jnp.max(s, axis=-1, keepdims=True))
            l = jnp.sum(p, axis=-1, keepdims=True)
            o_ref[:, h * HEAD:(h + 1) * HEAD] = _bf(_dot(_bf(p), vh) / l)

    vec = pl.BlockSpec((1, HEAD), lambda i: (0, 0))
    blk = 2 * _nbytes((tm, X_W), F32) + _nbytes((M, 2 * X_W), F32) + 4 * _nbytes((tm, M), F32)
    return pl.pallas_call(
        body, name=name, grid=(S // tm,),
        in_specs=[pl.BlockSpec((tm, X_W), lambda i: (i, 0)), pl.BlockSpec((M, 2 * X_W), lambda i: (0, 0)), vec, vec],
        out_specs=pl.BlockSpec((tm, X_W), lambda i: (i, 0)),
        out_shape=_sds((S, X_W), BF16),
        compiler_params=_cparams(1, blk))(*_pin([qraw, kv, qg, kg]))


def _cross_bwd(name, do, qraw, kv, qg, kg, *, tm, plan=None):
    S = qraw.shape[0]
    M = kv.shape[0]

    def body(do_ref, q_ref, kv_ref, qg_ref, kg_ref, dq_ref, dk_ref, dv_ref, gq_ref):
        @pl.when(pl.program_id(0) == 0)
        def _():
            dk_ref[...] = jnp.zeros_like(dk_ref)
            dv_ref[...] = jnp.zeros_like(dv_ref)
            gq_ref[...] = jnp.zeros_like(gq_ref)

        gq = jnp.zeros((1, HEAD), F32)
        for h in range(N_HEADS):
            sl = slice(h * HEAD, (h + 1) * HEAD)
            qr, qh, kh, vh = _cross_heads(q_ref, kv_ref, qg_ref[...], kg_ref[...], h)
            doh = _bf(do_ref[:, sl])
            s = _dot_nt(qh, kh)
            p = jnp.exp(s - jnp.max(s, axis=-1, keepdims=True))
            p = p / jnp.sum(p, axis=-1, keepdims=True)
            pb = _bf(p)
            dp = _dot_nt(doh, vh)
            ds = _bf(p * (dp - jnp.sum(dp * p, axis=-1, keepdims=True)))
            dv_ref[:, sl] += _dot_tn(pb, doh)
            dk_ref[:, sl] += _dot_tn(ds, qh)
            dn = _dot(ds, kh) * SCALE
            dt, xh = _norm_bwd(dn, qr, qg_ref[...])
            dq_ref[:, sl] = _bf(dt)
            gq = gq + jnp.sum(dn * xh, axis=0, keepdims=True)
        gq_ref[...] += gq

    vec = pl.BlockSpec((1, HEAD), lambda i: (0, 0))
    acc = pl.BlockSpec((M, X_W), lambda i: (0, 0))
    blk = 3 * _nbytes((tm, X_W), F32) + 3 * _nbytes((M, 2 * X_W), F32) + 6 * _nbytes((tm, M), F32)
    res, got = _pcall(
        body, name=name, grid=(S // tm,),
        in_specs=[pl.BlockSpec((tm, X_W), lambda i: (i, 0)), pl.BlockSpec((tm, X_W), lambda i: (i, 0)),
                  pl.BlockSpec((M, 2 * X_W), lambda i: (0, 0)), vec, vec],
        out_specs=[pl.BlockSpec((tm, X_W), lambda i: (i, 0)), acc, acc, vec],
        out_shape=[_sds((S, X_W), BF16), _sds((M, X_W), F32),
                   _sds((M, X_W), F32), _sds((1, HEAD), F32)],
        params=_cparams(1, blk), args=[do, qraw, kv, qg, kg], plan=plan)
    return res if plan is None else (res, got)


def _cross_kv_bwd(name, dkn, dv, kv, kg):
    M = kv.shape[0]

    def body(dk_ref, dv_ref, kv_ref, kg_ref, o_ref, g_ref):
        gk = jnp.zeros((1, HEAD), F32)
        for h in range(N_HEADS):
            sl = slice(h * HEAD, (h + 1) * HEAD)
            dn = dk_ref[:, sl]
            dt, xh = _norm_bwd(dn, kv_ref[:, sl], kg_ref[...])
            o_ref[:, sl] = _bf(dt)
            gk = gk + jnp.sum(dn * xh, axis=0, keepdims=True)
        o_ref[:, X_W:] = _bf(dv_ref[...])
        g_ref[...] = gk

    full = lambda shape: pl.BlockSpec(shape, lambda i: (0,) * len(shape))
    return pl.pallas_call(
        body, name=name, grid=(1,),
        in_specs=[full((M, X_W)), full((M, X_W)), full((M, 2 * X_W)), full((1, HEAD))],
        out_specs=[full((M, 2 * X_W)), full((1, HEAD))],
        out_shape=[_sds((M, 2 * X_W), BF16), _sds((1, HEAD), F32)],
        compiler_params=_cparams(1, 6 * _nbytes((M, 2 * X_W), F32)))(dkn, dv, kv, kg)


def _rope_tables(positions):
    inv_freq = ROPE_THETA ** (-jnp.arange(0, ROT_DIM, 2, dtype=F32) / ROT_DIM)
    ang = positions.astype(F32)[:, None] * inv_freq
    cos, sin = jnp.cos(ang), jnp.sin(ang)
    S = positions.shape[0]
    ctab = jnp.concatenate([cos, cos, jnp.ones((S, HEAD - ROT_DIM), F32)], axis=-1)
    stab = jnp.concatenate([-sin, sin, jnp.zeros((S, HEAD - ROT_DIM), F32)], axis=-1)
    return ctab, stab


GATHER_BEHIND_IN_PROJ = ("w_out", "w_cq", "w_ckv", "w_co")
FFN_WEIGHTS = ("w_gate_up", "w_down")


def _hosted(fn, *args, plan=None, **kw):
    if plan is None:
        return fn(*args, **kw), []
    return fn(*args, plan=plan, **kw)


def _local_step(x, mem, positions, target, wb, sm, *, tm=512, place=None):
    S, D = x.shape
    M = mem.shape[0]
    dist = place is not None
    wb = dict(wb)
    ctab, stab = _rope_tables(positions)
    pool_w_b = _bf(sm["pool_w"])

    def gather(names):
        return _GatherPlan([wb[k] for k in names]) if dist else None

    if dist:
        wb["w_in"], = _run_plan("gather_w_in", gather(["w_in"]))
    w_in = wb["w_in"]
    behind_in_proj = GATHER_BEHIND_IN_PROJ + ("w_gate_up",)
    (proj, xn1, qn, kn), got = _hosted(_in_proj, "in_proj", x, w_in, sm["mix_norm_g"], ctab, stab,
                                       sm["q_norm_g"], sm["k_norm_g"], tm=tm, plan=gather(behind_in_proj))
    wb.update(zip(behind_in_proj, got))
    (mix, lse), got = _hosted(_attn_fwd, "attn_fwd", qn, kn, proj, plan=gather(["w_down"]))
    wb.update(zip(["w_down"], got))
    w_out = wb["w_out"].reshape(1, 2 * KV_W, D)
    w_cq = wb["w_cq"].reshape(1, D, X_W)
    w_ckv = wb["w_ckv"].reshape(1, D, 2 * X_W)
    w_co = wb["w_co"]
    w_gu = wb["w_gate_up"]
    w_down = wb["w_down"].reshape(1, D_FF, D)
    cin = w_in.shape[2]
    cco = w_co.shape[2]
    cgu = w_gu.shape[2]
    mix, dpool = _pool_fwd("pool_fwd", mix, proj, pool_w_b, sm["pool_scale"], tm=tm)
    h1 = _mm_nn("out_proj", mix, w_out, tm=tm, residual=x)
    cq_raw, hn2 = _mm_nn("cq_proj", h1, w_cq, tm=tm, norm_g=sm["cross_norm_g"])
    kv, mem_n = _mm_nn("ckv_proj", mem, w_ckv, tm=M, norm_g=sm["mem_norm_g"])
    xo = _cross_fwd("cross_fwd", cq_raw, kv, sm["cq_norm_g"], sm["ck_norm_g"], tm=tm)
    h2 = _mm_nn("co_proj", xo, w_co, tm=tm, residual=h1)
    act, gu, hn3 = _ffn_up("ffn_up", h2, w_gu, sm["ffn_norm_g"], tm=tm)
    dy, lsum = _ffn_down_loss("ffn_down_loss", act, w_down, h2, target, tm=tm)
    loss = 0.5 * jnp.sum(lsum) / D

    ts = 2 * tm
    nS = S // ts
    g_down = _mm_tn("g_w_down", act, pl.BlockSpec((ts, cgu), lambda r, c, s: (s, r)), (ts, cgu),
                    dy, pl.BlockSpec((ts, D), lambda r, c, s: (s, 0)), (ts, D),
                    (1, D_FF, D), pl.BlockSpec((None, cgu, D), lambda r, c, s: (0, r, 0)), (cgu, D),
                    (D_FF // cgu, 1, nS))
    dgu, dh2, g_ffn_norm = _ffn_bwd("ffn_bwd", dy, w_down, gu, w_gu, h2, sm["ffn_norm_g"], tm=tm // 2)
    g_gu = _mm_tn("g_w_gate_up", hn3, pl.BlockSpec((ts, D), lambda r, c, s: (s, 0)), (ts, D),
                  dgu, pl.BlockSpec((None, ts, cgu), lambda r, c, s: (c // 2, s, c % 2)), (ts, cgu),
                  (4, D, cgu), pl.BlockSpec((None, D, cgu), lambda r, c, s: (c, 0, 0)), (D, cgu),
                  (1, 4, nS))

    dxo = _mm_nt("co_proj_bwd", dh2, w_co, tm=tm, out_dtype=BF16)
    g_co = _mm_tn_wide("g_w_co", xo, dh2, N_CHIPS, ts=ts)
    full = {"w_gate_up": g_gu, "w_down": g_down.reshape(N_CHIPS, D_FF // N_CHIPS, D)}
    sums = {}

    def swap(names):
        return _SwapPlan([full[k] for k in names]) if dist else None

    def add_halves(names, from_sibling):
        for k, t in zip(names, from_sibling):
            sums[k] = _add_core_halves(f"add_halves_{k}", full[k], t, place[0])

    def exchange(names):
        return _ExchangePlan([sums[k][1] for k in names]) if dist else None

    def sum_chips(names, from_chips):
        return [_sum_chips(f"sum_chips_{k}", sums[k][0], t, place[1]) for k, t in zip(names, from_chips)]

    (dcq, dkn, dvm, g_cq_norm), got = _hosted(_cross_bwd, "cross_bwd", dxo, cq_raw, kv, sm["cq_norm_g"],
                                              sm["ck_norm_g"], tm=tm, plan=swap(FFN_WEIGHTS))
    add_halves(FFN_WEIGHTS, got)
    dkv, g_ck_norm = _cross_kv_bwd("cross_kv_bwd", dkn, dvm, kv, sm["ck_norm_g"])
    dh1, g_cross_norm = _mm_nt_normbwd("cq_proj_bwd", dcq, w_cq, h1, sm["cross_norm_g"], dh2, tm=tm)
    g_cq = _mm_tn("g_w_cq", hn2, pl.BlockSpec((ts, D), lambda r, c, s: (s, 0)), (ts, D),
                  dcq, pl.BlockSpec((ts, X_W), lambda r, c, s: (s, 0)), (ts, X_W),
                  (1, D, X_W), pl.BlockSpec((None, D, X_W), lambda r, c, s: (0, 0, 0)), (D, X_W), (1, 1, nS))
    _, g_mem_norm = _mm_nt_normbwd("ckv_proj_bwd", dkv, w_ckv, mem, sm["mem_norm_g"], None, tm=M)
    g_ckv = _mm_tn("g_w_ckv", mem_n, pl.BlockSpec((M, D), lambda r, c, s: (0, 0)), (M, D),
                   dkv, pl.BlockSpec((M, 2 * X_W), lambda r, c, s: (0, 0)), (M, 2 * X_W),
                   (1, D, 2 * X_W), pl.BlockSpec((None, D, 2 * X_W), lambda r, c, s: (0, 0, 0)), (D, 2 * X_W),
                   (1, 1, 1))

    g_out = _mm_tn("g_w_out", mix, pl.BlockSpec((ts, 2 * KV_W), lambda r, c, s: (s, 0)), (ts, 2 * KV_W),
                   dh1, pl.BlockSpec((ts, D), lambda r, c, s: (s, 0)), (ts, D),
                   (1, 2 * KV_W, D), pl.BlockSpec((None, 2 * KV_W, D), lambda r, c, s: (0, 0, 0)), (2 * KV_W, D),
                   (1, 1, nS))
    full.update({
        "w_out": g_out.reshape(N_CHIPS, 2 * KV_W // N_CHIPS, D),
        "w_cq": g_cq.reshape(N_CHIPS, D // N_CHIPS, X_W),
        "w_ckv": g_ckv.reshape(N_CHIPS, D // N_CHIPS, 2 * X_W),
        "w_co": g_co,
    })
    mixer = GATHER_BEHIND_IN_PROJ
    (dattn, delta, du, g_pool_w, g_pool_scale), got = _hosted(
        _mix_bwd, "mix_bwd", dh1, w_out, mix, dpool, pool_w_b, sm["pool_scale"], tm=tm, plan=swap(mixer))
    add_halves(mixer, got)
    behind_attn = (None, mixer, FFN_WEIGHTS)
    halves = {}
    dqs, dks, dvs = [], [], []
    for grp in range(len(DILATIONS)):
        names = behind_attn[grp]
        (dq, dk, dv), got = _hosted(_attn_bwd, f"attn_bwd{grp}", qn, kn, proj, dattn, lse, delta, grp,
                                    plan=exchange(names) if names else None)
        if dist and names:
            halves.update(zip(names, sum_chips(names, got)))
        dqs.append(dq)
        dks.append(dk)
        dvs.append(dv)
    joined = mixer + FFN_WEIGHTS
    dproj, dx, g_q_norm, g_k_norm, g_mix_norm = _in_proj_bwd(
        "in_proj_bwd", dqs, dks, dvs, du, proj, ctab, stab, sm["q_norm_g"], sm["k_norm_g"], w_in, x,
        sm["mix_norm_g"], dh1, tm=tm // 2)
    small = {
        "mix_norm_g": g_mix_norm, "q_norm_g": g_q_norm, "k_norm_g": g_k_norm, "pool_w": g_pool_w,
        "pool_scale": g_pool_scale, "cross_norm_g": g_cross_norm, "mem_norm_g": g_mem_norm,
        "cq_norm_g": g_cq_norm, "ck_norm_g": g_ck_norm, "ffn_norm_g": g_ffn_norm,
    }
    behind_g_w_in = _PlanList([_JoinPlan([halves[k] for k in joined]), _AllPushPlan(_pack_small(small))]) if dist else None
    full["w_in"], got = _hosted(_mm_tn_wide, "g_w_in", xn1, dproj, N_CHIPS, ts=tm, plan=behind_g_w_in)
    shards = dict(zip(joined, got))
    if dist:
        small = _unpack_small(_sum_slots("sum_small", got[len(joined)]), sm)
        add_halves(["w_in"], _run_plan("swap_w_in", swap(["w_in"])))
        half, = sum_chips(["w_in"], _run_plan("exchange_w_in", exchange(["w_in"])))
        shards["w_in"], = _run_plan("join_w_in", _JoinPlan([half]))
    big = shards if dist else full
    return loss, dx, big, small


BIG = ("w_in", "w_out", "w_cq", "w_ckv", "w_co", "w_gate_up", "w_down")
SMALL = ("mix_norm_g", "q_norm_g", "k_norm_g", "pool_w", "pool_scale", "cross_norm_g", "mem_norm_g",
         "cq_norm_g", "ck_norm_g", "ffn_norm_g")
WEIGHTS = ("mix_norm_g", "w_in", "q_norm_g", "k_norm_g", "pool_w", "pool_scale", "w_out", "cross_norm_g",
           "mem_norm_g", "w_cq", "w_ckv", "cq_norm_g", "ck_norm_g", "w_co", "ffn_norm_g", "w_gate_up", "w_down")


def _cast_piece(name, w, k_arr):
    R, C = w.shape
    hr = R // 2

    def body(k_ref, w_ref, o_ref):
        o_ref[...] = _bf(w_ref[...])

    return pl.pallas_call(
        body, name=name,
        grid_spec=pltpu.PrefetchScalarGridSpec(
            num_scalar_prefetch=1, grid=(2,),
            in_specs=[pl.BlockSpec((hr, C), lambda i, k: (i, 0))],
            out_specs=pl.BlockSpec((None, hr, C), lambda i, k: (k[0], i, 0))),
        out_shape=_sds((N_CHIPS, R, C), BF16),
        compiler_params=_cparams(1, 2 * _nbytes((hr, C), F32)))(k_arr, *_pin([w]))


def _add_core_halves(name, g, t, c_arr):
    P, R, C = g.shape
    hr = R // 2

    def body(c_ref, g_ref, t_ref, o_ref, ob_ref):
        tot = g_ref[...] + t_ref[...]
        o_ref[...] = tot
        ob_ref[...] = _bf(tot)

    piece = pl.BlockSpec((None, hr, C), lambda p, c: (p, 0, 0))
    return pl.pallas_call(
        body, name=name,
        grid_spec=pltpu.PrefetchScalarGridSpec(
            num_scalar_prefetch=1, grid=(P,),
            in_specs=[pl.BlockSpec((None, hr, C), lambda p, c: (p, c[0], 0)), piece],
            out_specs=[piece, piece]),
        out_shape=[_sds((P, hr, C), F32), _sds((P, hr, C), BF16)],
        compiler_params=_cparams(1, 4 * _nbytes((hr, C), F32)))(c_arr, *_pin([g, t]))


def _sum_chips(name, own, got, kc_arr):
    P, hr, C = own.shape

    def body(kc_ref, o_ref, g_ref, r_ref):
        r_ref[...] = ((o_ref[...] + g_ref[0].astype(F32)) + g_ref[1].astype(F32)) + g_ref[2].astype(F32)

    return pl.pallas_call(
        body, name=name,
        grid_spec=pltpu.PrefetchScalarGridSpec(
            num_scalar_prefetch=1, grid=(1,),
            in_specs=[pl.BlockSpec((None, hr, C), lambda i, kc: (kc[0], 0, 0)),
                      pl.BlockSpec((N_CHIPS - 1, hr, C), lambda i, kc: (0, 0, 0))],
            out_specs=pl.BlockSpec((hr, C), lambda i, kc: (kc[1], 0))),
        out_shape=_sds((2 * hr, C), F32),
        compiler_params=_cparams(1, 5 * _nbytes((hr, C), F32)))(kc_arr, *_pin([own, got]))


N_DEV = 8


class _AllPushPlan(_Plan):
    def __init__(self, v):
        self.ins = [v]
        self.out_shapes = [jax.ShapeDtypeStruct((N_DEV,) + v.shape, v.dtype)]
        self.sem_shapes = [pltpu.SemaphoreType.DMA((N_DEV - 1,)), pltpu.SemaphoreType.DMA((N_DEV - 1,)),
                           pltpu.SemaphoreType.DMA]

    def copies(self, ins, outs, sems):
        send, recv, own = sems
        x, y, c, _ = _place()
        slot = outs[0].at[4 * x + 2 * y + c]
        cps = [pltpu.make_async_copy(ins[0], slot, own)]
        flips = [(dx, dy, dc) for dx in (0, 1) for dy in (0, 1) for dc in (0, 1)][1:]
        for q, (dx, dy, dc) in enumerate(flips):
            to = (x + dx - 2 * x * dx, y + dy - 2 * y * dy, c + dc - 2 * c * dc)
            cps.append(pltpu.make_async_remote_copy(src_ref=ins[0], dst_ref=slot, send_sem=send.at[q],
                                                    recv_sem=recv.at[q], device_id=to, device_id_type=MESH))
        return cps


def _sum_slots(name, slots):
    n, R, C = slots.shape

    def body(s_ref, o_ref):
        acc = s_ref[0]
        for d in range(1, n):
            acc = acc + s_ref[d]
        o_ref[...] = acc

    return pl.pallas_call(
        body, name=name, grid=(1,),
        in_specs=[pl.BlockSpec((n, R, C), lambda i: (0, 0, 0))], out_specs=pl.BlockSpec((R, C), lambda i: (0, 0)),
        out_shape=_sds((R, C), F32),
        compiler_params=_cparams(1, _nbytes((n + 1, R, C), F32)))(*_pin([slots]))


def _adamw(name, w, g, m, v, *, tr):
    R, C = w.shape

    def body(w_ref, g_ref, m_ref, v_ref, d_ref, nm_ref, nv_ref):
        gv = g_ref[...]
        nm = ADAM_B1 * m_ref[...] + (1.0 - ADAM_B1) * gv
        nv = ADAM_B2 * v_ref[...] + (1.0 - ADAM_B2) * (gv * gv)
        m_hat = nm / (1.0 - ADAM_B1 ** ADAM_STEP)
        v_hat = nv / (1.0 - ADAM_B2 ** ADAM_STEP)
        d_ref[...] = -ADAM_LR * (m_hat / (jnp.sqrt(v_hat) + ADAM_EPS) + ADAM_WD * w_ref[...])
        nm_ref[...] = nm
        nv_ref[...] = nv

    tile = pl.BlockSpec((tr, C), lambda i: (i, 0))
    return pl.pallas_call(
        body, name=name, grid=(R // tr,), in_specs=[tile] * 4, out_specs=[tile] * 3,
        out_shape=[_sds((R, C), F32)] * 3,
        compiler_params=_cparams(1, 7 * _nbytes((tr, C), F32)))(*_pin([w, g, m, v]))


def _pack_small(d):
    parts = []
    for name in SMALL:
        a = d[name].reshape(-1, HEAD)
        pad = (-a.shape[0]) % 8
        parts.append(jnp.pad(a, ((0, pad), (0, 0))))
    return jnp.concatenate(parts, axis=0)


def _unpack_small(packed, like):
    out = {}
    row = 0
    for name in SMALL:
        shape = like[name].shape
        rows = like[name].size // HEAD
        out[name] = packed[row:row + rows].reshape(shape)
        row += rows + (-rows) % 8
    return out


def kernel(x, mem, positions, mix_norm_g, w_in, q_norm_g, k_norm_g, pool_w, pool_scale, w_out, cross_norm_g, mem_norm_g, w_cq, w_ckv, cq_norm_g, ck_norm_g, w_co, ffn_norm_g, w_gate_up, w_down, loss_target, m_mix_norm_g, m_w_in, m_q_norm_g, m_k_norm_g, m_pool_w, m_pool_scale, m_w_out, m_cross_norm_g, m_mem_norm_g, m_w_cq, m_w_ckv, m_cq_norm_g, m_ck_norm_g, m_w_co, m_ffn_norm_g, m_w_gate_up, m_w_down, v_mix_norm_g, v_w_in, v_q_norm_g, v_k_norm_g, v_pool_w, v_pool_scale, v_w_out, v_cross_norm_g, v_mem_norm_g, v_w_cq, v_w_ckv, v_cq_norm_g, v_ck_norm_g, v_w_co, v_ffn_norm_g, v_w_gate_up, v_w_down):
    w = dict(mix_norm_g=mix_norm_g, w_in=w_in, q_norm_g=q_norm_g, k_norm_g=k_norm_g, pool_w=pool_w,
             pool_scale=pool_scale, w_out=w_out, cross_norm_g=cross_norm_g, mem_norm_g=mem_norm_g, w_cq=w_cq,
             w_ckv=w_ckv, cq_norm_g=cq_norm_g, ck_norm_g=ck_norm_g, w_co=w_co, ffn_norm_g=ffn_norm_g,
             w_gate_up=w_gate_up, w_down=w_down)
    m = dict(mix_norm_g=m_mix_norm_g, w_in=m_w_in, q_norm_g=m_q_norm_g, k_norm_g=m_k_norm_g, pool_w=m_pool_w,
             pool_scale=m_pool_scale, w_out=m_w_out, cross_norm_g=m_cross_norm_g, mem_norm_g=m_mem_norm_g,
             w_cq=m_w_cq, w_ckv=m_w_ckv, cq_norm_g=m_cq_norm_g, ck_norm_g=m_ck_norm_g, w_co=m_w_co,
             ffn_norm_g=m_ffn_norm_g, w_gate_up=m_w_gate_up, w_down=m_w_down)
    v = dict(mix_norm_g=v_mix_norm_g, w_in=v_w_in, q_norm_g=v_q_norm_g, k_norm_g=v_k_norm_g, pool_w=v_pool_w,
             pool_scale=v_pool_scale, w_out=v_w_out, cross_norm_g=v_cross_norm_g, mem_norm_g=v_mem_norm_g,
             w_cq=v_w_cq, w_ckv=v_w_ckv, cq_norm_g=v_cq_norm_g, ck_norm_g=v_ck_norm_g, w_co=v_w_co,
             ffn_norm_g=v_ffn_norm_g, w_gate_up=v_w_gate_up, w_down=v_w_down)

    c_arr = lax.axis_index("c").astype(jnp.int32).reshape(1)
    k_arr = (2 * lax.axis_index("x") + lax.axis_index("y")).astype(jnp.int32).reshape(1)
    kc_arr = jnp.concatenate([k_arr, c_arr])
    wb = {k: _cast_piece(f"cast_{k}", w[k][0], k_arr) for k in BIG}
    sm = {k: (w[k][0] if k == "pool_w" else w[k]) for k in SMALL}
    loss_part, dx, gshard, gsm = _local_step(x[0], mem[0], positions[0], loss_target[0], wb, sm,
                                             place=(c_arr, kc_arr))
    loss = lax.psum(loss_part, ("x", "y", "c"))

    grads, deltas, new_m, new_v = {}, {}, {}, {}
    for k in BIG:
        shard = w[k][0]
        tr = shard.shape[0] // 4
        d, nm, nv = _adamw(f"adamw_{k}", shard, gshard[k], m[k][0], v[k][0], tr=tr)
        grads[k], deltas[k], new_m[k], new_v[k] = gshard[k][None], d[None], nm[None], nv[None]
    smw = {k: (w[k][0] if k == "pool_w" else w[k]) for k in SMALL}
    smm = {k: (m[k][0] if k == "pool_w" else m[k]) for k in SMALL}
    smv = {k: (v[k][0] if k == "pool_w" else v[k]) for k in SMALL}
    pw, pg, pm, pv = _pack_small(smw), _pack_small(gsm), _pack_small(smm), _pack_small(smv)
    d, nm, nv = _adamw("adamw_small", pw, pg, pm, pv, tr=pw.shape[0])
    for dst, packed in ((deltas, d), (new_m, nm), (new_v, nv)):
        un = _unpack_small(packed, sm)
        for k in SMALL:
            dst[k] = un[k].reshape(w[k].shape)
    for k in SMALL:
        grads[k] = gsm[k].reshape(w[k].shape)

    return (loss, dx[None], *[grads[k] for k in WEIGHTS], *[deltas[k] for k in WEIGHTS],
            *[new_m[k] for k in WEIGHTS], *[new_v[k] for k in WEIGHTS])
```

```python
import functools

import jax
import jax.numpy as jnp
from jax import lax
from jax.experimental import pallas as pl
from jax.experimental.pallas import tpu as pltpu

F32 = jnp.float32
BF16 = jnp.bfloat16
MESH = pl.DeviceIdType.MESH
ANY = pl.BlockSpec(memory_space=pl.ANY)

D_MODEL = 1024
HEAD = 128
N_HEADS = 4
DILATIONS = (1, 4, 16)
BLOCK = 128
Q_W = 1536
KV_W = 512
POOL_W = 512
POOL_WINDOWS = (2, 4, 8, 16)
POOL_HALO = 16
IN_W = 3072
ROT_DIM = 32
ROT_HALF = 16
ROPE_THETA = 500000.0
X_W = 512
D_FF = 2816
EPS = 1e-6
NEG_INF = -1e30
SCALE = HEAD ** -0.5
N_CHIPS = 4

ADAM_LR = 0.001
ADAM_B1 = 0.9
ADAM_B2 = 0.999
ADAM_EPS = 1e-08
ADAM_WD = 0.01
ADAM_STEP = 10

VMEM_BYTES_V7X = 64 * 2 ** 20
VMEM_LIMIT_MAX = 56 * 2 ** 20
VMEM_LIMIT_MIN = 24 * 2 ** 20


def _nbytes(shape, dtype):
    n = 1
    for s in shape:
        n *= s
    return n * jnp.dtype(dtype).itemsize


def _cparams(n_axes, block_bytes, scratch_bytes=0):
    est = 2 * (2 * block_bytes + scratch_bytes)
    lim = int(min(VMEM_LIMIT_MAX, max(VMEM_LIMIT_MIN, est)))
    return pltpu.CompilerParams(dimension_semantics=("arbitrary",) * n_axes, vmem_limit_bytes=lim)


def _bf(v):
    return v.astype(BF16)


def _dot(a, b):
    return jnp.dot(a, b, preferred_element_type=F32)


def _dot_nt(a, b):
    return lax.dot_general(a, b, (((1,), (1,)), ((), ())), preferred_element_type=F32)


def _dot_tn(a, b):
    return lax.dot_general(a, b, (((0,), (0,)), ((), ())), preferred_element_type=F32)


def _rstd(v):
    return lax.rsqrt(jnp.mean(v * v, axis=-1, keepdims=True) + EPS)


def _norm_bwd(dy, xv, g):
    r = _rstd(xv)
    xh = xv * r
    dxh = dy * g
    dx = r * (dxh - xh * jnp.mean(dxh * xh, axis=-1, keepdims=True))
    return dx, xh


def _place():
    x, y, c = lax.axis_index("x"), lax.axis_index("y"), lax.axis_index("c")
    other_chips = [(1 - x, y), (x, 1 - y), (1 - x, 1 - y)]
    return x, y, c, other_chips


class _Plan:
    ins = ()
    out_shapes = ()
    aliases = {}
    sem_shapes = ()

    def copies(self, ins, outs, sems):
        raise NotImplementedError

    def begin(self, ins, outs, sems):
        for cp in self.copies(ins, outs, sems):
            cp.start()

    def finish(self, ins, outs, sems):
        for cp in self.copies(ins, outs, sems):
            cp.wait()


class _GatherPlan(_Plan):
    def __init__(self, bufs):
        n = len(bufs)
        self.ins = list(bufs)
        self.out_shapes = [_sds(b.shape, b.dtype) for b in bufs]
        self.aliases = {i: i for i in range(n)}
        self.sem_shapes = [pltpu.SemaphoreType.DMA((n, 6)), pltpu.SemaphoreType.DMA((n, 6))]

    def _parts(self, outs, sems):
        send, recv = sems
        x, y, c, chips = _place()

        def half(i, piece, which):
            hr = outs[i].shape[1] // 2
            return outs[i].at[piece, pl.ds(which * hr, hr), :]

        def copy(i, k, ref, to):
            return pltpu.make_async_remote_copy(src_ref=ref, dst_ref=ref, send_sem=send.at[i, k], recv_sem=recv.at[i, k],
                                                device_id=to, device_id_type=MESH)

        return x, y, c, chips, half, copy

    def begin(self, ins, outs, sems):
        x, y, c, chips, half, copy = self._parts(outs, sems)
        for i in range(len(outs)):
            for j, (cx, cy) in enumerate(chips):
                copy(i, j, half(i, 2 * x + y, c), (cx, cy, c)).start()

    def finish(self, ins, outs, sems):
        x, y, c, chips, half, copy = self._parts(outs, sems)
        sib = (x, y, 1 - c)
        n = len(outs)
        for i in range(n):
            for j, (cx, cy) in enumerate(chips):
                piece = half(i, 2 * cx + cy, c)
                copy(i, j, piece, (cx, cy, c)).wait_recv()
                copy(i, 3 + j, piece, sib).start()
        for i in range(n):
            for j, (cx, cy) in enumerate(chips):
                copy(i, 3 + j, half(i, 2 * cx + cy, 1 - c), sib).wait_recv()
        for i in range(n):
            for j, (cx, cy) in enumerate(chips):
                copy(i, j, half(i, 2 * x + y, c), (cx, cy, c)).wait_send()
                copy(i, 3 + j, half(i, 2 * cx + cy, c), sib).wait_send()


class _SwapPlan(_Plan):
    def __init__(self, grads):
        n = len(grads)
        self.ins = list(grads)
        self.out_shapes = [_sds((g.shape[0], g.shape[1] // 2, g.shape[2]), g.dtype) for g in grads]
        self.sem_shapes = [pltpu.SemaphoreType.DMA((n,)), pltpu.SemaphoreType.DMA((n,))]

    def copies(self, ins, outs, sems):
        send, recv = sems
        x, y, c, _ = _place()
        cps = []
        for i in range(len(ins)):
            hr = ins[i].shape[1] // 2
            cps.append(pltpu.make_async_remote_copy(
                src_ref=ins[i].at[:, pl.ds((1 - c) * hr, hr), :], dst_ref=outs[i], send_sem=send.at[i],
                recv_sem=recv.at[i], device_id=(x, y, 1 - c), device_id_type=MESH))
        return cps


class _ExchangePlan(_Plan):
    def __init__(self, sums):
        n = len(sums)
        self.ins = list(sums)
        self.out_shapes = [_sds((N_CHIPS - 1,) + s.shape[1:], s.dtype) for s in sums]
        self.sem_shapes = [pltpu.SemaphoreType.DMA((n, 3)), pltpu.SemaphoreType.DMA((n, 3))]

    def copies(self, ins, outs, sems):
        send, recv = sems
        x, y, c, chips = _place()
        cps = []
        for i in range(len(ins)):
            for j, (cx, cy) in enumerate(chips):
                cps.append(pltpu.make_async_remote_copy(
                    src_ref=ins[i].at[2 * cx + cy], dst_ref=outs[i].at[j], send_sem=send.at[i, j],
                    recv_sem=recv.at[i, j], device_id=(cx, cy, c), device_id_type=MESH))
        return cps


class _JoinPlan(_Plan):
    def __init__(self, shards):
        n = len(shards)
        self.ins = list(shards)
        self.out_shapes = [_sds(s.shape, s.dtype) for s in shards]
        self.aliases = {i: i for i in range(n)}
        self.sem_shapes = [pltpu.SemaphoreType.DMA((n,)), pltpu.SemaphoreType.DMA((n,))]

    def copies(self, ins, outs, sems):
        send, recv = sems
        x, y, c, _ = _place()
        cps = []
        for i in range(len(outs)):
            hr = outs[i].shape[0] // 2
            mine = outs[i].at[pl.ds(c * hr, hr), :]
            cps.append(pltpu.make_async_remote_copy(src_ref=mine, dst_ref=mine, send_sem=send.at[i], recv_sem=recv.at[i],
                                                    device_id=(x, y, 1 - c), device_id_type=MESH))
        return cps


def _run_plan(name, plan):
    n_in, n_out = len(plan.ins), len(plan.out_shapes)

    def body(*refs):
        ins, outs, sems = refs[:n_in], refs[n_in:n_in + n_out], refs[n_in + n_out:]
        plan.begin(ins, outs, sems)
        plan.finish(ins, outs, sems)

    return pl.pallas_call(
        body, name=name, in_specs=[ANY] * n_in, out_specs=[ANY] * n_out, out_shape=list(plan.out_shapes),
        input_output_aliases=dict(plan.aliases), scratch_shapes=list(plan.sem_shapes))(*plan.ins)


HBM_PIN_BYTES = 1 << 20


def _sds(shape, dtype):
    if _nbytes(shape, dtype) >= HBM_PIN_BYTES:
        return pltpu.HBM(shape, dtype)
    return jax.ShapeDtypeStruct(shape, dtype)


def _pin(args):
    return [pltpu.with_memory_space_constraint(a, pltpu.HBM) if _nbytes(a.shape, a.dtype) >= HBM_PIN_BYTES else a
            for a in args]


def _pcall(body, *, name, grid, in_specs, out_specs, out_shape, params, args, scratch_shapes=(), plan=None):
    in_specs, out_specs, out_shape, scratch = list(in_specs), list(out_specs), list(out_shape), list(scratch_shapes)
    args = _pin(args)
    if plan is None:
        res = pl.pallas_call(body, name=name, grid=grid, in_specs=in_specs, out_specs=out_specs, out_shape=out_shape,
                             scratch_shapes=scratch, compiler_params=params)(*args)
        return list(res), []
    ni, no, ns = len(in_specs), len(out_specs), len(scratch)
    pi, po = len(plan.ins), len(plan.out_shapes)

    def wrapped(*refs):
        ins, pins = refs[:ni], refs[ni:ni + pi]
        outs, pouts = refs[ni + pi:ni + pi + no], refs[ni + pi + no:ni + pi + no + po]
        scr, psems = refs[ni + pi + no + po:ni + pi + no + po + ns], refs[ni + pi + no + po + ns:]
        ids = [pl.program_id(a) for a in range(len(grid))]
        first = functools.reduce(jnp.logical_and, [i == 0 for i in ids])
        last = functools.reduce(jnp.logical_and, [i == g - 1 for i, g in zip(ids, grid)])

        @pl.when(first)
        def _():
            plan.begin(pins, pouts, psems)

        body(*ins, *outs, *scr)

        @pl.when(last)
        def _():
            plan.finish(pins, pouts, psems)

    res = pl.pallas_call(
        wrapped, name=name, grid=grid, in_specs=in_specs + [ANY] * pi, out_specs=out_specs + [ANY] * po,
        out_shape=out_shape + list(plan.out_shapes), scratch_shapes=scratch + list(plan.sem_shapes),
        input_output_aliases={ni + a: no + b for a, b in plan.aliases.items()},
        compiler_params=params)(*args, *plan.ins)
    return list(res[:no]), list(res[no:])


def _mm_nn(name, a, w3, *, tm, norm_g=None, residual=None, out_dtype=F32, plan=None):
    M, K = a.shape
    P, Kw, C = w3.shape
    assert Kw == K and M % tm == 0
    N = P * C
    has_norm = norm_g is not None
    has_res = residual is not None

    def body(*refs):
        refs = list(refs)
        a_ref = refs.pop(0)
        w_ref = refs.pop(0)
        g_ref = refs.pop(0) if has_norm else None
        r_ref = refs.pop(0) if has_res else None
        o_ref = refs.pop(0)
        xn_ref = refs.pop(0) if has_norm else None
        if has_norm:
            av = a_ref[...].astype(F32)
            ab = _bf(av * _rstd(av) * g_ref[...])
            xn_ref[...] = ab
        else:
            ab = _bf(a_ref[...])
        for p in range(P):
            acc = _dot(ab, w_ref[p])
            if has_res:
                acc = acc + r_ref[:, p * C:(p + 1) * C]
            o_ref[:, p * C:(p + 1) * C] = acc.astype(o_ref.dtype)

    in_specs = [pl.BlockSpec((tm, K), lambda i: (i, 0)), pl.BlockSpec((P, K, C), lambda i: (0, 0, 0))]
    args = [a, w3]
    if has_norm:
        in_specs.append(pl.BlockSpec((1, K), lambda i: (0, 0)))
        args.append(norm_g)
    if has_res:
        in_specs.append(pl.BlockSpec((tm, N), lambda i: (i, 0)))
        args.append(residual)
    out_shape = [_sds((M, N), out_dtype)]
    out_specs = [pl.BlockSpec((tm, N), lambda i: (i, 0))]
    if has_norm:
        out_shape.append(_sds((M, K), BF16))
        out_specs.append(pl.BlockSpec((tm, K), lambda i: (i, 0)))
    blk = (_nbytes((tm, K), a.dtype) + _nbytes((P, K, C), BF16) + 2 * _nbytes((tm, N), F32)
           + _nbytes((tm, K), BF16))
    res, got = _pcall(body, name=name, grid=(M // tm,), in_specs=in_specs, out_specs=out_specs, out_shape=out_shape,
                      params=_cparams(1, blk), args=args, plan=plan)
    res = res if has_norm else res[0]
    return res if plan is None else (res, got)


MXU_COLS_V7X = 256


def _col_chunks(n):
    return [(c, min(c + MXU_COLS_V7X, n)) for c in range(0, n, MXU_COLS_V7X)]


def _ffn_up(name, h, wgu3, g, *, tm):
    M, K = h.shape
    P, _, C = wgu3.shape
    half = P // 2

    def body(h_ref, wg_ref, wu_ref, g_ref, act_ref, fac_ref, xn_ref, ab_ref):
        @pl.when(pl.program_id(1) == 0)
        def _():
            hv = h_ref[...]
            xn = _bf(hv * _rstd(hv) * g_ref[...])
            ab_ref[...] = xn
            xn_ref[...] = xn
        ab = ab_ref[...]
        for c0, c1 in _col_chunks(C):
            gate = _dot(ab, wg_ref[:, c0:c1])
            up = _dot(ab, wu_ref[:, c0:c1])
            sig = 1.0 / (1.0 + jnp.exp(-gate))
            silu = gate * sig
            act_ref[:, c0:c1] = _bf(silu * up)
            fac_ref[0, :, c0:c1] = _bf(up * (sig + silu * (1.0 - sig)))
            fac_ref[1, :, c0:c1] = _bf(silu)

    blk = (_nbytes((tm, K), F32) + 2 * _nbytes((K, C), BF16) + 3 * _nbytes((tm, C), BF16)
           + 2 * _nbytes((tm, C), F32) + _nbytes((tm, K), BF16))
    return pl.pallas_call(
        body, name=name, grid=(M // tm, half),
        in_specs=[pl.BlockSpec((tm, K), lambda i, j: (i, 0)),
                  pl.BlockSpec((None, K, C), lambda i, j: (j, 0, 0)),
                  pl.BlockSpec((None, K, C), lambda i, j: (j + half, 0, 0)),
                  pl.BlockSpec((1, K), lambda i, j: (0, 0))],
        out_specs=[pl.BlockSpec((tm, C), lambda i, j: (i, j)),
                   pl.BlockSpec((2, tm, C), lambda i, j: (0, i, j)),
                   pl.BlockSpec((tm, K), lambda i, j: (i, 0))],
        out_shape=[_sds((M, half * C), BF16),
                   _sds((2, M, half * C), BF16),
                   _sds((M, K), BF16)],
        scratch_shapes=[pltpu.VMEM((tm, K), BF16)],
        compiler_params=_cparams(2, blk, _nbytes((tm, K), BF16)))(*_pin([h, wgu3, wgu3, g]))


def _ffn_down_loss(name, act, wd3, h, target, *, tm):
    M, K = act.shape
    _, _, N = wd3.shape

    def body(a_ref, w_ref, h_ref, t_ref, dy_ref, ls_ref):
        err = _dot(a_ref[...], w_ref[...]) + h_ref[...] - t_ref[...]
        dy_ref[...] = err * (1.0 / N)

        @pl.when(pl.program_id(0) == 0)
        def _():
            ls_ref[...] = jnp.zeros_like(ls_ref)
        ls_ref[...] += jnp.sum(err * err, axis=0, keepdims=True)

    blk = _nbytes((tm, K), BF16) + _nbytes((K, N), BF16) + 4 * _nbytes((tm, N), F32)
    return pl.pallas_call(
        body, name=name, grid=(M // tm,),
        in_specs=[pl.BlockSpec((tm, K), lambda i: (i, 0)),
                  pl.BlockSpec((None, K, N), lambda i: (0, 0, 0)),
                  pl.BlockSpec((tm, N), lambda i: (i, 0)),
                  pl.BlockSpec((tm, N), lambda i: (i, 0))],
        out_specs=[pl.BlockSpec((tm, N), lambda i: (i, 0)),
                   pl.BlockSpec((1, N), lambda i: (0, 0))],
        out_shape=[_sds((M, N), F32), _sds((1, N), F32)],
        compiler_params=_cparams(1, blk))(*_pin([act, wd3, h, target]))


def _nt_pieces(a_ref, w_ref):
    P, _, C = w_ref.shape
    acc = _dot_nt(_bf(a_ref[:, 0:C]), w_ref[0])
    for p in range(1, P):
        acc = acc + _dot_nt(_bf(a_ref[:, p * C:(p + 1) * C]), w_ref[p])
    return acc


def _mm_nt_normbwd(name, a, w3, h, g, dres, *, tm, plan=None):
    M, D = h.shape
    P, Ko, C = w3.shape
    N = a.shape[1]
    assert N == P * C and Ko == D and M % tm == 0
    has_res = dres is not None

    def body(*refs):
        a_ref, w_ref, h_ref, g_ref = refs[:4]
        r_ref = refs[4] if has_res else None
        dx_ref, dg_ref = refs[-2:]
        dhn = _nt_pieces(a_ref, w_ref)
        dx, xh = _norm_bwd(dhn, h_ref[...], g_ref[...])
        if has_res:
            dx = dx + r_ref[...]
        dx_ref[...] = dx

        @pl.when(pl.program_id(0) == 0)
        def _():
            dg_ref[...] = jnp.zeros_like(dg_ref)
        dg_ref[...] += jnp.sum(dhn * xh, axis=0, keepdims=True)

    row = pl.BlockSpec((tm, D), lambda i: (i, 0))
    vec = pl.BlockSpec((1, D), lambda i: (0, 0))
    in_specs = [pl.BlockSpec((tm, N), lambda i: (i, 0)), pl.BlockSpec((P, Ko, C), lambda i: (0, 0, 0)), row, vec]
    args = [a, w3, h, g]
    if has_res:
        in_specs.append(row)
        args.append(dres)
    blk = _nbytes((tm, N), a.dtype) + _nbytes((P, Ko, C), BF16) + 5 * _nbytes((tm, D), F32)
    res, got = _pcall(body, name=name, grid=(M // tm,), in_specs=in_specs, out_specs=[row, vec],
                      out_shape=[_sds((M, D), F32), _sds((1, D), F32)],
                      params=_cparams(1, blk), args=args, plan=plan)
    return res if plan is None else (res, got)


def _ffn_bwd(name, dy, wd3, fac, wgu3, h, g, *, tm):
    M, D = h.shape
    P, _, C = wgu3.shape
    F = fac.shape[2]
    per = F // C

    def body(dy_ref, wd_ref, f_ref, w_ref, h_ref, g_ref, dgu_ref, dx_ref, dg_ref):
        dyv = dy_ref[...]
        ab = _bf(dyv)
        dhn = None
        for pc in range(per):
            cols = slice(pc * C, (pc + 1) * C)
            da = _dot_nt(ab, wd_ref[cols, :])
            dgate = _bf(da * f_ref[0, :, cols].astype(F32))
            dup = _bf(da * f_ref[1, :, cols].astype(F32))
            dgu_ref[0, :, cols] = dgate
            dgu_ref[1, :, cols] = dup
            part = _dot_nt(dgate, w_ref[pc]) + _dot_nt(dup, w_ref[per + pc])
            dhn = part if dhn is None else dhn + part
        dx, xh = _norm_bwd(dhn, h_ref[...], g_ref[...])
        dx_ref[...] = dx + dyv

        @pl.when(pl.program_id(0) == 0)
        def _():
            dg_ref[...] = jnp.zeros_like(dg_ref)
        dg_ref[...] += jnp.sum(dhn * xh, axis=0, keepdims=True)

    row = pl.BlockSpec((tm, D), lambda i: (i, 0))
    vec = pl.BlockSpec((1, D), lambda i: (0, 0))
    planes = pl.BlockSpec((2, tm, F), lambda i: (0, i, 0))
    blk = (2 * _nbytes((2, tm, F), BF16) + (_nbytes((F, D), BF16) + _nbytes((P, D, C), BF16)) // 2
           + 4 * _nbytes((tm, D), F32))
    return pl.pallas_call(
        body, name=name, grid=(M // tm,),
        in_specs=[row, pl.BlockSpec((None, F, D), lambda i: (0, 0, 0), pipeline_mode=pl.Buffered(1)), planes,
                  pl.BlockSpec((P, D, C), lambda i: (0, 0, 0), pipeline_mode=pl.Buffered(1)), row, vec],
        out_specs=[planes, row, vec],
        out_shape=[_sds((2, M, F), BF16), _sds((M, D), F32), _sds((1, D), F32)],
        compiler_params=_cparams(1, blk))(*_pin([dy, wd3, fac, wgu3, h, g]))


def _mm_tn_wide(name, a, b, pieces, *, ts, plan=None):
    S, K = a.shape
    N = b.shape[1]
    C = N // pieces

    def body(a_ref, b_ref, o_ref):
        @pl.when(pl.program_id(0) == 0)
        def _():
            o_ref[...] = jnp.zeros_like(o_ref)
        acc = _dot_tn(_bf(a_ref[...]), _bf(b_ref[...]))
        for p in range(pieces):
            o_ref[p] += acc[:, p * C:(p + 1) * C]

    blk = _nbytes((ts, K), a.dtype) + _nbytes((ts, N), b.dtype) + 2 * _nbytes((K, N), F32)
    res, got = _pcall(body, name=name, grid=(S // ts,),
                      in_specs=[pl.BlockSpec((ts, K), lambda s: (s, 0)), pl.BlockSpec((ts, N), lambda s: (s, 0))],
                      out_specs=[pl.BlockSpec((pieces, K, C), lambda s: (0, 0, 0))],
                      out_shape=[_sds((pieces, K, C), F32)], params=_cparams(1, blk), args=[a, b], plan=plan)
    return res[0] if plan is None else (res[0], got)


def _mm_tn(name, a, a_spec, a_blk, b, b_spec, b_blk, out3, o_spec, o_blk, grid, plan=None):
    def body(a_ref, b_ref, o_ref):
        @pl.when(pl.program_id(2) == 0)
        def _():
            o_ref[...] = jnp.zeros_like(o_ref)
        o_ref[...] += _dot_tn(_bf(a_ref[...]), _bf(b_ref[...]))

    blk = _nbytes(a_blk, a.dtype) + _nbytes(b_blk, b.dtype) + 2 * _nbytes(o_blk, F32)
    res, got = _pcall(body, name=name, grid=grid, in_specs=[a_spec, b_spec], out_specs=[o_spec],
                      out_shape=[_sds(out3, F32)], params=_cparams(3, blk), args=[a, b], plan=plan)
    return res[0] if plan is None else (res[0], got)


def _swap_halves(v, lane, masked):
    up = pltpu.roll(v, HEAD - ROT_HALF, 1)
    down = pltpu.roll(v, ROT_HALF, 1)
    rest = jnp.where(lane < ROT_DIM, down, 0.0) if masked else down
    return jnp.where(lane < ROT_HALF, up, rest)


def _in_proj(name, x, w3, g, ctab, stab, qg, kg, *, tm, plan=None):
    S, K = x.shape
    P, _, C = w3.shape

    def body(x_ref, w_ref, g_ref, c_ref, s_ref, qg_ref, kg_ref, p_ref, xn_ref, q_ref, k_ref):
        xv = x_ref[...]
        ab = _bf(xv * _rstd(xv) * g_ref[...])
        xn_ref[...] = ab
        for p in range(P):
            p_ref[:, p * C:(p + 1) * C] = _dot(ab, w_ref[p])
        lane = lax.broadcasted_iota(jnp.int32, (tm, HEAD), 1)
        cv = c_ref[...]
        sv = s_ref[...]

        def prep(t, gain, scale):
            n = t * _rstd(t) * gain
            return (n * cv + _swap_halves(n, lane, False) * sv) * scale

        for j in range(Q_W // HEAD):
            q_ref[:, j * HEAD:(j + 1) * HEAD] = prep(p_ref[:, j * HEAD:(j + 1) * HEAD], qg_ref[...], SCALE)
        for j in range(KV_W // HEAD):
            k_ref[:, j * HEAD:(j + 1) * HEAD] = prep(p_ref[:, Q_W + j * HEAD:Q_W + (j + 1) * HEAD], kg_ref[...], 1.0)

    row = lambda width: pl.BlockSpec((tm, width), lambda i: (i, 0))
    vec = lambda width: pl.BlockSpec((1, width), lambda i: (0, 0))
    blk = (_nbytes((tm, K), F32) + _nbytes((P, K, C), BF16) + _nbytes((tm, P * C + Q_W + KV_W + 2 * HEAD), F32)
           + _nbytes((tm, K), BF16))
    res, got = _pcall(
        body, name=name, grid=(S // tm,),
        in_specs=[row(K), pl.BlockSpec((P, K, C), lambda i: (0, 0, 0)), vec(K), row(HEAD), row(HEAD), vec(HEAD), vec(HEAD)],
        out_specs=[row(P * C), row(K), row(Q_W), row(KV_W)],
        out_shape=[_sds((S, P * C), F32), _sds((S, K), BF16), _sds((S, Q_W), F32), _sds((S, KV_W), F32)],
        params=_cparams(1, blk), args=[x, w3, g, ctab, stab, qg, kg], plan=plan)
    return res if plan is None else (res, got)


ATTN_STEP = 2048


def _row_idx(start, dil):
    return pl.ds(start, BLOCK) if dil == 1 else pl.ds(start, BLOCK, stride=dil)


def _rows(ref, start, dil):
    return ref[_row_idx(start, dil), :]


def _set_rows(ref, start, dil, val):
    ref[_row_idx(start, dil), :] = val


def _band_mask(first):
    qi = lax.broadcasted_iota(jnp.int32, (BLOCK, 2 * BLOCK), 0)
    kj = lax.broadcasted_iota(jnp.int32, (BLOCK, 2 * BLOCK), 1)
    band = (kj >= qi) & (kj <= qi + BLOCK)
    if first is None:
        return band
    return band & ((kj >= BLOCK) | jnp.logical_not(first))


def _attn_geometry(S, grp):
    dil = DILATIONS[grp]
    bt = BLOCK * dil
    assert S % ATTN_STEP == 0 and ATTN_STEP % bt == 0
    return dil, bt, ATTN_STEP // bt, S // ATTN_STEP


MERGE_ROWS = 256


def _attn_fwd(name, qn, kn, proj, plan=None):
    S = qn.shape[0]
    groups = range(len(DILATIONS))
    geo = [_attn_geometry(S, g) for g in groups]
    nsb = geo[0][3]
    vcol = (Q_W + KV_W) // HEAD

    def body(q0, q1, q2, kc_ref, vc_ref, kp0, kp1, kp2, vp0, vp1, vp2, mix_ref, lse_ref, o_s, l_s):
        q_refs, kp_refs, vp_refs = (q0, q1, q2), (kp0, kp1, kp2), (vp0, vp1, vp2)
        valid_first = _band_mask(pl.program_id(1) == 0)
        valid_inner = _band_mask(None)
        for g in groups:
            dil, bt, nsub, _ = geo[g]
            o_g, l_g = o_s.at[g], l_s.at[g]
            for r in range(dil):
                kprev, vprev = _bf(_rows(kp_refs[g], r, dil)), _bf(_rows(vp_refs[g], r, dil))
                for b in range(nsub):
                    at = b * bt + r
                    kcur, vcur = _bf(_rows(kc_ref, at, dil)), _bf(_rows(vc_ref, at, dil))
                    q = _bf(_rows(q_refs[g], at, dil))
                    k2 = jnp.concatenate([kprev, kcur], axis=0)
                    v2 = jnp.concatenate([vprev, vcur], axis=0)
                    s = jnp.where(valid_first if b == 0 else valid_inner, _dot_nt(q, k2), NEG_INF)
                    m = jnp.max(s, axis=-1, keepdims=True)
                    p = jnp.exp(s - m)
                    l = jnp.sum(p, axis=-1, keepdims=True)
                    acc = _dot(_bf(p), v2)
                    _set_rows(o_g, at, dil, acc / l)
                    _set_rows(l_g, at, dil, jnp.broadcast_to(m + jnp.log(l), (BLOCK, HEAD)))
                    kprev, vprev = kcur, vcur
        for c in range(ATTN_STEP // MERGE_ROWS):
            rows = pl.ds(c * MERGE_ROWS, MERGE_ROWS)
            ls = [l_s[g, rows, :] for g in groups]
            m = jnp.maximum(jnp.maximum(ls[0], ls[1]), ls[2])
            ws = [jnp.exp(v - m) for v in ls]
            den = ws[0] + ws[1] + ws[2]
            num = ws[0] * o_s[0, rows, :] + ws[1] * o_s[1, rows, :] + ws[2] * o_s[2, rows, :]
            mix_ref[rows, :] = _bf(num / den)
            lse_ref[rows, :] = m + jnp.log(den)

    def big(col):
        return pl.BlockSpec((ATTN_STEP, HEAD), lambda h, n: (n, col + h))

    def tail(g, col):
        _, bt, nsub, _ = geo[g]
        return pl.BlockSpec((bt, HEAD), lambda h, n: (jnp.maximum(n * nsub - 1, 0), col + h))

    blk = 7 * _nbytes((ATTN_STEP, HEAD), F32) + 2 * sum(_nbytes((geo[g][1], HEAD), F32) for g in groups)
    scratch = 2 * len(DILATIONS) * _nbytes((ATTN_STEP, HEAD), F32)
    res, got = _pcall(
        body, name=name, grid=(N_HEADS, nsb),
        in_specs=[big(g * N_HEADS) for g in groups] + [big(0), big(vcol)]
                 + [tail(g, 0) for g in groups] + [tail(g, vcol) for g in groups],
        out_specs=[big(0), big(0)],
        out_shape=[_sds((S, 2 * KV_W), BF16), _sds((S, KV_W), F32)],
        scratch_shapes=[pltpu.VMEM((len(DILATIONS), ATTN_STEP, HEAD), F32)] * 2,
        params=_cparams(2, blk, scratch), args=[qn, qn, qn, kn, proj, kn, kn, kn, proj, proj, proj], plan=plan)
    return res if plan is None else (res, got)


def _attn_bwd(name, qn, kn, proj, dattn, lse, delta, grp, plan=None):
    S = qn.shape[0]
    dil, bt, nsub, nsb = _attn_geometry(S, grp)
    vcol = (Q_W + KV_W) // HEAD

    def slot(b, r):
        return pl.ds((b * dil + r) * BLOCK, BLOCK)

    def body(q_ref, kp_ref, kc_ref, vp_ref, vc_ref, do_ref, l_ref, d_ref, dq_ref, dk_ref, dv_ref, ck_ref, cv_ref):
        n = pl.program_id(1)
        par = n % 2

        @pl.when(n < nsb)
        def _():
            valid_first = _band_mask(n == 0)
            valid_inner = _band_mask(None)
            ck, cv = ck_ref.at[par], cv_ref.at[par]
            pk, pv = ck_ref.at[1 - par], cv_ref.at[1 - par]
            for r in range(dil):
                kprev, vprev = _bf(_rows(kp_ref, r, dil)), _bf(_rows(vp_ref, r, dil))
                own_k = own_v = None
                for b in range(nsub):
                    at = b * bt + r
                    kcur, vcur = _bf(_rows(kc_ref, at, dil)), _bf(_rows(vc_ref, at, dil))
                    q = _bf(_rows(q_ref, at, dil))
                    k2 = jnp.concatenate([kprev, kcur], axis=0)
                    v2 = jnp.concatenate([vprev, vcur], axis=0)
                    do = _bf(_rows(do_ref, at, dil))
                    lse_r = _rows(l_ref, at, dil)[:, :1]
                    del_r = _rows(d_ref, at, dil)[:, :1]
                    s = jnp.where(valid_first if b == 0 else valid_inner, _dot_nt(q, k2), NEG_INF)
                    p = jnp.exp(s - lse_r)
                    ds = _bf(p * (_dot_nt(do, v2) - del_r))
                    _set_rows(dq_ref, at, dil, _dot(ds, k2))
                    dk2 = _dot_tn(ds, q)
                    dv2 = _dot_tn(_bf(p), do)
                    if b > 0:
                        ck[slot(b - 1, r), :] = own_k + dk2[:BLOCK]
                        cv[slot(b - 1, r), :] = own_v + dv2[:BLOCK]
                    else:
                        @pl.when(n > 0)
                        def _():
                            pk[slot(nsub - 1, r), :] += dk2[:BLOCK]
                            pv[slot(nsub - 1, r), :] += dv2[:BLOCK]
                    own_k, own_v = dk2[BLOCK:], dv2[BLOCK:]
                    kprev, vprev = kcur, vcur
                ck[slot(nsub - 1, r), :] = own_k
                cv[slot(nsub - 1, r), :] = own_v

        @pl.when(n > 0)
        def _():
            for r in range(dil):
                for b in range(nsub):
                    _set_rows(dk_ref, b * bt + r, dil, ck_ref[1 - par, slot(b, r), :])
                    _set_rows(dv_ref, b * bt + r, dil, cv_ref[1 - par, slot(b, r), :])

    cur = lambda n: jnp.minimum(n, nsb - 1)
    prev = lambda n: jnp.maximum(n - 1, 0)
    tail_at = lambda n: jnp.maximum(cur(n) * nsub - 1, 0)
    big = lambda col: pl.BlockSpec((ATTN_STEP, HEAD), lambda h, n: (cur(n), col(h)))
    tail = lambda col: pl.BlockSpec((bt, HEAD), lambda h, n: (tail_at(n), col(h)))
    late = pl.BlockSpec((ATTN_STEP, HEAD), lambda h, n: (prev(n), h))
    blk = 9 * _nbytes((ATTN_STEP, HEAD), F32) + 2 * _nbytes((bt, HEAD), F32)
    res, got = _pcall(
        body, name=name, grid=(N_HEADS, nsb + 1),
        in_specs=[big(lambda h: grp * N_HEADS + h), tail(lambda h: h), big(lambda h: h),
                  tail(lambda h: vcol + h), big(lambda h: vcol + h),
                  big(lambda h: h), big(lambda h: h), big(lambda h: h)],
        out_specs=[big(lambda h: h), late, late],
        out_shape=[_sds((S, KV_W), F32)] * 3,
        scratch_shapes=[pltpu.VMEM((2, ATTN_STEP, HEAD), F32), pltpu.VMEM((2, ATTN_STEP, HEAD), F32)],
        params=_cparams(2, blk, 4 * _nbytes((ATTN_STEP, HEAD), F32)),
        args=[qn, kn, kn, proj, proj, dattn, lse, delta], plan=plan)
    return res if plan is None else (res, got)


def _window_sum(v, n_doublings, back):
    rows = v.shape[0]
    step = 1
    for _ in range(n_doublings):
        v = v + pltpu.roll(v, step if back else rows - step, 0)
        step *= 2
    return v


def _pool_fwd(name, mix, proj, pool_w, pool_scale, *, tm):
    S = proj.shape[0]
    ucol = (IN_W - POOL_W) // POOL_W
    hpt = tm // POOL_HALO

    def body(mix_in, u_ref, uh_ref, pw_ref, ps_ref, pooled_ref, dp_ref):
        i = pl.program_id(0)
        halo = jnp.where(i == 0, 0.0, uh_ref[...])
        t = lax.broadcasted_iota(jnp.int32, (tm + POOL_HALO, HEAD), 0) + (i * tm - POOL_HALO)
        for g, w in enumerate(POOL_WINDOWS):
            sl = slice(g * HEAD, (g + 1) * HEAD)
            ub = jnp.concatenate([halo[:, sl], u_ref[:, sl]], axis=0)
            cnt = jnp.minimum(t + 1, w).astype(F32)
            d = (_window_sum(ub, g + 1, True) / cnt - ub)[POOL_HALO:]
            db = _bf(d)
            dp_ref[:, sl] = db
            pooled_ref[:, sl] = _bf(_dot(db, pw_ref[g]) * ps_ref[:, sl])

    blk = 3 * _nbytes((tm, POOL_W), F32) + 2 * _nbytes((tm, POOL_W), BF16)
    return pl.pallas_call(
        body, name=name, grid=(S // tm,),
        in_specs=[ANY,
                  pl.BlockSpec((tm, POOL_W), lambda i: (i, ucol)),
                  pl.BlockSpec((POOL_HALO, POOL_W), lambda i: (jnp.maximum(i * hpt - 1, 0), ucol)),
                  pl.BlockSpec((len(POOL_WINDOWS), HEAD, HEAD), lambda i: (0, 0, 0)),
                  pl.BlockSpec((1, POOL_W), lambda i: (0, 0))],
        out_specs=[pl.BlockSpec((tm, POOL_W), lambda i: (i, 1)), pl.BlockSpec((tm, POOL_W), lambda i: (i, 0))],
        out_shape=[_sds((S, 2 * KV_W), BF16), _sds((S, POOL_W), BF16)],
        input_output_aliases={0: 0},
        compiler_params=_cparams(1, blk))(*_pin([mix, proj, proj, pool_w, pool_scale]))


def _mix_bwd(name, dh, w_out3, mix, dpool, pool_w, pool_scale, *, tm, plan=None):
    S, D = dh.shape
    hpt = tm // POOL_HALO
    last_halo = S // POOL_HALO - 1
    n_tiles = S // tm

    def body(dh_ref, dhn_ref, w_ref, at_ref, dp_ref, pw_ref, ps_ref, da_ref, dl_ref, du_ref, gw_ref, gs_ref):
        i = pl.program_id(0)

        @pl.when(i == 0)
        def _():
            gw_ref[...] = jnp.zeros_like(gw_ref)
            gs_ref[...] = jnp.zeros_like(gs_ref)

        dm = _dot_nt(_bf(dh_ref[...]), w_ref[...])
        ahead = _dot_nt(_bf(dhn_ref[...]), w_ref[KV_W:, :])
        for h in range(N_HEADS):
            sl = slice(h * HEAD, (h + 1) * HEAD)
            da = dm[:, sl]
            da_ref[:, sl] = da
            dl_ref[:, sl] = jnp.broadcast_to(
                jnp.sum(da * at_ref[:, sl].astype(F32), axis=-1, keepdims=True), (tm, HEAD))
        halo = jnp.where(i == n_tiles - 1, 0.0, ahead)
        t = lax.broadcasted_iota(jnp.int32, (tm + POOL_HALO, HEAD), 0) + i * tm
        for g, w in enumerate(POOL_WINDOWS):
            sl = slice(g * HEAD, (g + 1) * HEAD)
            dy = jnp.concatenate([dm[:, KV_W + g * HEAD:KV_W + (g + 1) * HEAD], halo[:, sl]], axis=0)
            dys = _bf(dy * ps_ref[:, sl])
            dd = _dot_nt(dys, pw_ref[g])
            cnt = jnp.minimum(t + 1, w).astype(F32)
            du_ref[:, sl] = (_window_sum(dd / cnt, g + 1, False) - dd)[:tm]
            db = dp_ref[:, sl]
            gw_ref[g] += _dot_tn(db, dys[:tm])
            gs_ref[:, sl] += jnp.sum(dy[:tm] * _dot(db, pw_ref[g]), axis=0, keepdims=True)

    tile = pl.BlockSpec((tm, KV_W), lambda i: (i, 0))
    blk = 2 * _nbytes((tm, D), F32) + _nbytes((2 * KV_W, D), BF16) + 6 * _nbytes((tm, KV_W), F32)
    res, got = _pcall(
        body, name=name, grid=(n_tiles,),
        in_specs=[pl.BlockSpec((tm, D), lambda i: (i, 0)),
                  pl.BlockSpec((POOL_HALO, D), lambda i: (jnp.minimum((i + 1) * hpt, last_halo), 0)),
                  pl.BlockSpec((None, 2 * KV_W, D), lambda i: (0, 0, 0)),
                  tile, tile,
                  pl.BlockSpec((len(POOL_WINDOWS), HEAD, HEAD), lambda i: (0, 0, 0)),
                  pl.BlockSpec((1, POOL_W), lambda i: (0, 0))],
        out_specs=[tile, tile, tile,
                   pl.BlockSpec((len(POOL_WINDOWS), HEAD, HEAD), lambda i: (0, 0, 0)),
                   pl.BlockSpec((1, POOL_W), lambda i: (0, 0))],
        out_shape=[_sds((S, KV_W), F32)] * 3 + [
            _sds((len(POOL_WINDOWS), HEAD, HEAD), F32), _sds((1, POOL_W), F32)],
        params=_cparams(1, blk), args=[dh, dh, w_out3, mix, dpool, pool_w, pool_scale], plan=plan)
    return res if plan is None else (res, got)


def _qkv_bwd(name, dqs, dks, dvs, du, proj, ctab, stab, qg, kg, *, tm, plan=None):
    S = proj.shape[0]
    width = Q_W + KV_W

    def body(dq0, dq1, dq2, dk0, dk1, dk2, dv0, dv1, dv2, du_ref, p_ref, c_ref, s_ref, qg_ref, kg_ref,
             dp_ref, gq_ref, gk_ref):
        @pl.when(pl.program_id(0) == 0)
        def _():
            gq_ref[...] = jnp.zeros_like(gq_ref)
            gk_ref[...] = jnp.zeros_like(gk_ref)

        lane = lax.broadcasted_iota(jnp.int32, (tm, HEAD), 1)
        cv = c_ref[...]
        sv = s_ref[...]

        def back(dy, t, g, scale):
            dy = dy * scale
            dn = dy * cv + _swap_halves(dy * sv, lane, True)
            dt, xh = _norm_bwd(dn, t, g)
            return dt, jnp.sum(dn * xh, axis=0, keepdims=True)

        dqr = (dq0, dq1, dq2)
        gq = jnp.zeros((1, HEAD), F32)
        for j in range(Q_W // HEAD):
            grp, h = divmod(j, N_HEADS)
            dt, gj = back(dqr[grp][:, h * HEAD:(h + 1) * HEAD], p_ref[:, j * HEAD:(j + 1) * HEAD], qg_ref[...], SCALE)
            dp_ref[:, j * HEAD:(j + 1) * HEAD] = _bf(dt)
            gq = gq + gj
        gq_ref[...] += gq
        gk = jnp.zeros((1, HEAD), F32)
        for h in range(N_HEADS):
            sl = slice(h * HEAD, (h + 1) * HEAD)
            dt, gj = back(dk0[:, sl] + dk1[:, sl] + dk2[:, sl], p_ref[:, Q_W + h * HEAD:Q_W + (h + 1) * HEAD],
                          kg_ref[...], 1.0)
            dp_ref[:, Q_W + h * HEAD:Q_W + (h + 1) * HEAD] = _bf(dt)
            gk = gk + gj
        gk_ref[...] += gk
        dp_ref[:, width:width + KV_W] = _bf(dv0[...] + dv1[...] + dv2[...])
        dp_ref[:, width + KV_W:] = _bf(du_ref[...])

    tile = pl.BlockSpec((tm, KV_W), lambda i: (i, 0))
    vec = pl.BlockSpec((1, HEAD), lambda i: (0, 0))
    rot = pl.BlockSpec((tm, HEAD), lambda i: (i, 0))
    blk = 10 * _nbytes((tm, KV_W), F32) + _nbytes((tm, width), F32) + _nbytes((tm, IN_W), BF16)
    res, got = _pcall(
        body, name=name, grid=(S // tm,),
        in_specs=[tile] * 10 + [pl.BlockSpec((tm, width), lambda i: (i, 0)), rot, rot, vec, vec],
        out_specs=[pl.BlockSpec((tm, IN_W), lambda i: (i, 0)), vec, vec],
        out_shape=[_sds((S, IN_W), BF16), _sds((1, HEAD), F32),
                   _sds((1, HEAD), F32)],
        params=_cparams(1, blk), args=[*dqs, *dks, *dvs, du, proj, ctab, stab, qg, kg], plan=plan)
    return res if plan is None else (res, got)


def _cross_heads(q_ref, kv_ref, qg, kg, h):
    sl = slice(h * HEAD, (h + 1) * HEAD)
    qr = q_ref[:, sl]
    kr = kv_ref[:, sl]
    qh = qr * _rstd(qr) * qg * SCALE
    kh = kr * _rstd(kr) * kg
    vh = kv_ref[:, X_W + h * HEAD:X_W + (h + 1) * HEAD]
    return qr, _bf(qh), _bf(kh), _bf(vh)


def _cross_fwd(name, qraw, kv, qg, kg, *, tm):
    S = qraw.shape[0]
    M = kv.shape[0]

    def body(q_ref, kv_ref, qg_ref, kg_ref, o_ref):
        for h in range(N_HEADS):
            _, qh, kh, vh = _cross_heads(q_ref, kv_ref, qg_ref[...], kg_ref[...], h)
            s = _dot_nt(qh, kh)
            p = jnp.exp(s - jnp.max(s, axis=-1, keepdims=True))
            l = jnp.sum(p, axis=-1, keepdims=True)
            o_ref[:, h * HEAD:(h + 1) * HEAD] = _bf(_dot(_bf(p), vh) / l)

    vec = pl.BlockSpec((1, HEAD), lambda i: (0, 0))
    blk = 2 * _nbytes((tm, X_W), F32) + _nbytes((M, 2 * X_W), F32) + 4 * _nbytes((tm, M), F32)
    return pl.pallas_call(
        body, name=name, grid=(S // tm,),
        in_specs=[pl.BlockSpec((tm, X_W), lambda i: (i, 0)), pl.BlockSpec((M, 2 * X_W), lambda i: (0, 0)), vec, vec],
        out_specs=pl.BlockSpec((tm, X_W), lambda i: (i, 0)),
        out_shape=_sds((S, X_W), BF16),
        compiler_params=_cparams(1, blk))(*_pin([qraw, kv, qg, kg]))


def _cross_bwd(name, dh, w_co3, qraw, kv, qg, kg, w_cq3, h_in, g_in, *, tm, plan=None):
    S, D = dh.shape
    M = kv.shape[0]

    def body(dh_ref, wco_ref, q_ref, kv_ref, qg_ref, kg_ref, wcq_ref, h_ref, g_ref,
             dq_ref, dk_ref, dv_ref, gq_ref, dx_ref, gn_ref):
        @pl.when(pl.program_id(0) == 0)
        def _():
            dk_ref[...] = jnp.zeros_like(dk_ref)
            dv_ref[...] = jnp.zeros_like(dv_ref)
            gq_ref[...] = jnp.zeros_like(gq_ref)
            gn_ref[...] = jnp.zeros_like(gn_ref)

        dhv = dh_ref[...]
        do_all = _nt_pieces(dh_ref, wco_ref)
        gq = jnp.zeros((1, HEAD), F32)
        for h in range(N_HEADS):
            sl = slice(h * HEAD, (h + 1) * HEAD)
            qr, qh, kh, vh = _cross_heads(q_ref, kv_ref, qg_ref[...], kg_ref[...], h)
            doh = _bf(do_all[:, sl])
            s = _dot_nt(qh, kh)
            p = jnp.exp(s - jnp.max(s, axis=-1, keepdims=True))
            p = p / jnp.sum(p, axis=-1, keepdims=True)
            pb = _bf(p)
            dp = _dot_nt(doh, vh)
            ds = _bf(p * (dp - jnp.sum(dp * p, axis=-1, keepdims=True)))
            dv_ref[:, sl] += _dot_tn(pb, doh)
            dk_ref[:, sl] += _dot_tn(ds, qh)
            dn = _dot(ds, kh) * SCALE
            dt, xh = _norm_bwd(dn, qr, qg_ref[...])
            dq_ref[:, sl] = _bf(dt)
            gq = gq + jnp.sum(dn * xh, axis=0, keepdims=True)
        gq_ref[...] += gq
        dhn = _dot_nt(dq_ref[...], wcq_ref[...])
        dx, xh = _norm_bwd(dhn, h_ref[...], g_ref[...])
        dx_ref[...] = dx + dhv
        gn_ref[...] += jnp.sum(dhn * xh, axis=0, keepdims=True)

    vec = pl.BlockSpec((1, HEAD), lambda i: (0, 0))
    acc = pl.BlockSpec((M, X_W), lambda i: (0, 0))
    row = pl.BlockSpec((tm, D), lambda i: (i, 0))
    wide = pl.BlockSpec((1, D), lambda i: (0, 0))
    qtile = pl.BlockSpec((tm, X_W), lambda i: (i, 0))
    blk = (3 * _nbytes((tm, X_W), F32) + 3 * _nbytes((M, 2 * X_W), F32) + 6 * _nbytes((tm, M), F32)
           + 4 * _nbytes((tm, D), F32) + _nbytes(w_co3.shape, BF16) + _nbytes((D, X_W), BF16))
    res, got = _pcall(
        body, name=name, grid=(S // tm,),
        in_specs=[row, pl.BlockSpec(w_co3.shape, lambda i: (0, 0, 0)), qtile,
                  pl.BlockSpec((M, 2 * X_W), lambda i: (0, 0)), vec, vec,
                  pl.BlockSpec((None, D, X_W), lambda i: (0, 0, 0)), row, wide],
        out_specs=[qtile, acc, acc, vec, row, wide],
        out_shape=[_sds((S, X_W), BF16), _sds((M, X_W), F32), _sds((M, X_W), F32), _sds((1, HEAD), F32),
                   _sds((S, D), F32), _sds((1, D), F32)],
        params=_cparams(1, blk), args=[dh, w_co3, qraw, kv, qg, kg, w_cq3, h_in, g_in], plan=plan)
    return res if plan is None else (res, got)


def _cross_kv_bwd(name, dkn, dv, kv, kg):
    M = kv.shape[0]

    def body(dk_ref, dv_ref, kv_ref, kg_ref, o_ref, g_ref):
        gk = jnp.zeros((1, HEAD), F32)
        for h in range(N_HEADS):
            sl = slice(h * HEAD, (h + 1) * HEAD)
            dn = dk_ref[:, sl]
            dt, xh = _norm_bwd(dn, kv_ref[:, sl], kg_ref[...])
            o_ref[:, sl] = _bf(dt)
            gk = gk + jnp.sum(dn * xh, axis=0, keepdims=True)
        o_ref[:, X_W:] = _bf(dv_ref[...])
        g_ref[...] = gk

    full = lambda shape: pl.BlockSpec(shape, lambda i: (0,) * len(shape))
    return pl.pallas_call(
        body, name=name, grid=(1,),
        in_specs=[full((M, X_W)), full((M, X_W)), full((M, 2 * X_W)), full((1, HEAD))],
        out_specs=[full((M, 2 * X_W)), full((1, HEAD))],
        out_shape=[_sds((M, 2 * X_W), BF16), _sds((1, HEAD), F32)],
        compiler_params=_cparams(1, 6 * _nbytes((M, 2 * X_W), F32)))(dkn, dv, kv, kg)


def _rope_tables(positions):
    inv_freq = ROPE_THETA ** (-jnp.arange(0, ROT_DIM, 2, dtype=F32) / ROT_DIM)
    ang = positions.astype(F32)[:, None] * inv_freq
    cos, sin = jnp.cos(ang), jnp.sin(ang)
    S = positions.shape[0]
    ctab = jnp.concatenate([cos, cos, jnp.ones((S, HEAD - ROT_DIM), F32)], axis=-1)
    stab = jnp.concatenate([-sin, sin, jnp.zeros((S, HEAD - ROT_DIM), F32)], axis=-1)
    return ctab, stab


GATHER_BEHIND_IN_PROJ = ("w_out", "w_cq", "w_ckv", "w_co")
FFN_WEIGHTS = ("w_gate_up", "w_down")


def _hosted(fn, *args, plan=None, **kw):
    if plan is None:
        return fn(*args, **kw), []
    return fn(*args, plan=plan, **kw)


def _local_step(x, mem, positions, target, wb, sm, *, tm=512, place=None):
    S, D = x.shape
    M = mem.shape[0]
    dist = place is not None
    wb = dict(wb)
    ctab, stab = _rope_tables(positions)
    pool_w_b = _bf(sm["pool_w"])

    def gather(names):
        return _GatherPlan([wb[k] for k in names]) if dist else None

    if dist:
        wb["w_in"], = _run_plan("gather_w_in", gather(["w_in"]))
    w_in = wb["w_in"]
    behind_in_proj = GATHER_BEHIND_IN_PROJ + ("w_gate_up",)
    (proj, xn1, qn, kn), got = _hosted(_in_proj, "in_proj", x, w_in, sm["mix_norm_g"], ctab, stab,
                                       sm["q_norm_g"], sm["k_norm_g"], tm=tm, plan=gather(behind_in_proj))
    wb.update(zip(behind_in_proj, got))
    (mix, lse), got = _hosted(_attn_fwd, "attn_fwd", qn, kn, proj, plan=gather(["w_down"]))
    wb.update(zip(["w_down"], got))
    w_out = wb["w_out"].reshape(1, 2 * KV_W, D)
    w_cq = wb["w_cq"].reshape(1, D, X_W)
    w_ckv = wb["w_ckv"].reshape(1, D, 2 * X_W)
    w_co = wb["w_co"]
    w_gu = wb["w_gate_up"]
    w_down = wb["w_down"].reshape(1, D_FF, D)
    cin = w_in.shape[2]
    cco = w_co.shape[2]
    cgu = w_gu.shape[2]
    mix, dpool = _pool_fwd("pool_fwd", mix, proj, pool_w_b, sm["pool_scale"], tm=tm)
    h1 = _mm_nn("out_proj", mix, w_out, tm=tm, residual=x)
    cq_raw, hn2 = _mm_nn("cq_proj", h1, w_cq, tm=tm, norm_g=sm["cross_norm_g"])
    kv, mem_n = _mm_nn("ckv_proj", mem, w_ckv, tm=M, norm_g=sm["mem_norm_g"])
    xo = _cross_fwd("cross_fwd", cq_raw, kv, sm["cq_norm_g"], sm["ck_norm_g"], tm=tm)
    h2 = _mm_nn("co_proj", xo, w_co, tm=tm, residual=h1)
    act, gu, hn3 = _ffn_up("ffn_up", h2, w_gu, sm["ffn_norm_g"], tm=tm)
    dy, lsum = _ffn_down_loss("ffn_down_loss", act, w_down, h2, target, tm=tm)
    loss = 0.5 * jnp.sum(lsum) / D

    ts = 2 * tm
    nS = S // ts
    g_down = _mm_tn("g_w_down", act, pl.BlockSpec((ts, cgu), lambda r, c, s: (s, r)), (ts, cgu),
                    dy, pl.BlockSpec((ts, D), lambda r, c, s: (s, 0)), (ts, D),
                    (1, D_FF, D), pl.BlockSpec((None, cgu, D), lambda r, c, s: (0, r, 0)), (cgu, D),
                    (D_FF // cgu, 1, nS))
    dgu, dh2, g_ffn_norm = _ffn_bwd("ffn_bwd", dy, w_down, gu, w_gu, h2, sm["ffn_norm_g"], tm=tm // 2)
    g_gu = _mm_tn("g_w_gate_up", hn3, pl.BlockSpec((ts, D), lambda r, c, s: (s, 0)), (ts, D),
                  dgu, pl.BlockSpec((None, ts, cgu), lambda r, c, s: (c // 2, s, c % 2)), (ts, cgu),
                  (4, D, cgu), pl.BlockSpec((None, D, cgu), lambda r, c, s: (c, 0, 0)), (D, cgu),
                  (1, 4, nS))

    g_co = _mm_tn_wide("g_w_co", xo, dh2, N_CHIPS, ts=ts)
    full = {"w_gate_up": g_gu, "w_down": g_down.reshape(N_CHIPS, D_FF // N_CHIPS, D)}
    sums = {}

    def swap(names):
        return _SwapPlan([full[k] for k in names]) if dist else None

    def add_halves(names, from_sibling):
        for k, t in zip(names, from_sibling):
            sums[k] = _add_core_halves(f"add_halves_{k}", full[k], t, place[0])

    def exchange(names):
        return _ExchangePlan([sums[k][1] for k in names]) if dist else None

    def sum_chips(names, from_chips):
        return [_sum_chips(f"sum_chips_{k}", sums[k][0], t, place[1]) for k, t in zip(names, from_chips)]

    (dcq, dkn, dvm, g_cq_norm, dh1, g_cross_norm), got = _hosted(
        _cross_bwd, "cross_bwd", dh2, w_co, cq_raw, kv, sm["cq_norm_g"], sm["ck_norm_g"], w_cq, h1,
        sm["cross_norm_g"], tm=tm, plan=swap(FFN_WEIGHTS))
    add_halves(FFN_WEIGHTS, got)
    dkv, g_ck_norm = _cross_kv_bwd("cross_kv_bwd", dkn, dvm, kv, sm["ck_norm_g"])
    g_cq = _mm_tn("g_w_cq", hn2, pl.BlockSpec((ts, D), lambda r, c, s: (s, 0)), (ts, D),
                  dcq, pl.BlockSpec((ts, X_W), lambda r, c, s: (s, 0)), (ts, X_W),
                  (1, D, X_W), pl.BlockSpec((None, D, X_W), lambda r, c, s: (0, 0, 0)), (D, X_W), (1, 1, nS))
    _, g_mem_norm = _mm_nt_normbwd("ckv_proj_bwd", dkv, w_ckv, mem, sm["mem_norm_g"], None, tm=M)
    g_ckv = _mm_tn("g_w_ckv", mem_n, pl.BlockSpec((M, D), lambda r, c, s: (0, 0)), (M, D),
                   dkv, pl.BlockSpec((M, 2 * X_W), lambda r, c, s: (0, 0)), (M, 2 * X_W),
                   (1, D, 2 * X_W), pl.BlockSpec((None, D, 2 * X_W), lambda r, c, s: (0, 0, 0)), (D, 2 * X_W),
                   (1, 1, 1))

    g_out = _mm_tn("g_w_out", mix, pl.BlockSpec((ts, 2 * KV_W), lambda r, c, s: (s, 0)), (ts, 2 * KV_W),
                   dh1, pl.BlockSpec((ts, D), lambda r, c, s: (s, 0)), (ts, D),
                   (1, 2 * KV_W, D), pl.BlockSpec((None, 2 * KV_W, D), lambda r, c, s: (0, 0, 0)), (2 * KV_W, D),
                   (1, 1, nS))
    full.update({
        "w_out": g_out.reshape(N_CHIPS, 2 * KV_W // N_CHIPS, D),
        "w_cq": g_cq.reshape(N_CHIPS, D // N_CHIPS, X_W),
        "w_ckv": g_ckv.reshape(N_CHIPS, D // N_CHIPS, 2 * X_W),
        "w_co": g_co,
    })
    mixer = GATHER_BEHIND_IN_PROJ
    (dattn, delta, du, g_pool_w, g_pool_scale), got = _hosted(
        _mix_bwd, "mix_bwd", dh1, w_out, mix, dpool, pool_w_b, sm["pool_scale"], tm=tm, plan=swap(mixer))
    add_halves(mixer, got)
    behind_attn = (None, mixer, FFN_WEIGHTS)
    halves = {}
    dqs, dks, dvs = [], [], []
    for grp in range(len(DILATIONS)):
        names = behind_attn[grp]
        (dq, dk, dv), got = _hosted(_attn_bwd, f"attn_bwd{grp}", qn, kn, proj, dattn, lse, delta, grp,
                                    plan=exchange(names) if names else None)
        if dist and names:
            halves.update(zip(names, sum_chips(names, got)))
        dqs.append(dq)
        dks.append(dk)
        dvs.append(dv)
    joined = mixer + FFN_WEIGHTS
    (dproj, g_q_norm, g_k_norm), got = _hosted(
        _qkv_bwd, "qkv_bwd", dqs, dks, dvs, du, proj, ctab, stab, sm["q_norm_g"], sm["k_norm_g"], tm=tm,
        plan=_JoinPlan([halves[k] for k in joined]) if dist else None)
    shards = dict(zip(joined, got))
    dx, g_mix_norm = _mm_nt_normbwd("in_proj_bwd", dproj, w_in, x, sm["mix_norm_g"], dh1, tm=tm)
    small = {
        "mix_norm_g": g_mix_norm, "q_norm_g": g_q_norm, "k_norm_g": g_k_norm, "pool_w": g_pool_w,
        "pool_scale": g_pool_scale, "cross_norm_g": g_cross_norm, "mem_norm_g": g_mem_norm,
        "cq_norm_g": g_cq_norm, "ck_norm_g": g_ck_norm, "ffn_norm_g": g_ffn_norm,
    }
    full["w_in"], got = _hosted(_mm_tn_wide, "g_w_in", xn1, dproj, N_CHIPS, ts=tm,
                                plan=_AllPushPlan(_pack_small(small)) if dist else None)
    if dist:
        small = _unpack_small(_sum_slots("sum_small", got[0]), sm)
        add_halves(["w_in"], _run_plan("swap_w_in", swap(["w_in"])))
        half, = sum_chips(["w_in"], _run_plan("exchange_w_in", exchange(["w_in"])))
        shards["w_in"], = _run_plan("join_w_in", _JoinPlan([half]))
    big = shards if dist else full
    return loss, dx, big, small


BIG = ("w_in", "w_out", "w_cq", "w_ckv", "w_co", "w_gate_up", "w_down")
SMALL = ("mix_norm_g", "q_norm_g", "k_norm_g", "pool_w", "pool_scale", "cross_norm_g", "mem_norm_g",
         "cq_norm_g", "ck_norm_g", "ffn_norm_g")
WEIGHTS = ("mix_norm_g", "w_in", "q_norm_g", "k_norm_g", "pool_w", "pool_scale", "w_out", "cross_norm_g",
           "mem_norm_g", "w_cq", "w_ckv", "cq_norm_g", "ck_norm_g", "w_co", "ffn_norm_g", "w_gate_up", "w_down")


def _cast_piece(name, w, k_arr):
    R, C = w.shape
    hr = R // 2

    def body(k_ref, w_ref, o_ref):
        o_ref[...] = _bf(w_ref[...])

    return pl.pallas_call(
        body, name=name,
        grid_spec=pltpu.PrefetchScalarGridSpec(
            num_scalar_prefetch=1, grid=(2,),
            in_specs=[pl.BlockSpec((hr, C), lambda i, k: (i, 0))],
            out_specs=pl.BlockSpec((None, hr, C), lambda i, k: (k[0], i, 0))),
        out_shape=_sds((N_CHIPS, R, C), BF16),
        compiler_params=_cparams(1, 2 * _nbytes((hr, C), F32)))(k_arr, *_pin([w]))


def _add_core_halves(name, g, t, c_arr):
    P, R, C = g.shape
    hr = R // 2

    def body(c_ref, g_ref, t_ref, o_ref, ob_ref):
        tot = g_ref[...] + t_ref[...]
        o_ref[...] = tot
        ob_ref[...] = _bf(tot)

    piece = pl.BlockSpec((None, hr, C), lambda p, c: (p, 0, 0))
    return pl.pallas_call(
        body, name=name,
        grid_spec=pltpu.PrefetchScalarGridSpec(
            num_scalar_prefetch=1, grid=(P,),
            in_specs=[pl.BlockSpec((None, hr, C), lambda p, c: (p, c[0], 0)), piece],
            out_specs=[piece, piece]),
        out_shape=[_sds((P, hr, C), F32), _sds((P, hr, C), BF16)],
        compiler_params=_cparams(1, 4 * _nbytes((hr, C), F32)))(c_arr, *_pin([g, t]))


def _sum_chips(name, own, got, kc_arr):
    P, hr, C = own.shape

    def body(kc_ref, o_ref, g_ref, r_ref):
        r_ref[...] = ((o_ref[...] + g_ref[0].astype(F32)) + g_ref[1].astype(F32)) + g_ref[2].astype(F32)

    return pl.pallas_call(
        body, name=name,
        grid_spec=pltpu.PrefetchScalarGridSpec(
            num_scalar_prefetch=1, grid=(1,),
            in_specs=[pl.BlockSpec((None, hr, C), lambda i, kc: (kc[0], 0, 0)),
                      pl.BlockSpec((N_CHIPS - 1, hr, C), lambda i, kc: (0, 0, 0))],
            out_specs=pl.BlockSpec((hr, C), lambda i, kc: (kc[1], 0))),
        out_shape=_sds((2 * hr, C), F32),
        compiler_params=_cparams(1, 5 * _nbytes((hr, C), F32)))(kc_arr, *_pin([own, got]))


N_DEV = 8


class _AllPushPlan(_Plan):
    def __init__(self, v):
        self.ins = [v]
        self.out_shapes = [jax.ShapeDtypeStruct((N_DEV,) + v.shape, v.dtype)]
        self.sem_shapes = [pltpu.SemaphoreType.DMA((N_DEV - 1,)), pltpu.SemaphoreType.DMA((N_DEV - 1,)),
                           pltpu.SemaphoreType.DMA]

    def copies(self, ins, outs, sems):
        send, recv, own = sems
        x, y, c, _ = _place()
        slot = outs[0].at[4 * x + 2 * y + c]
        cps = [pltpu.make_async_copy(ins[0], slot, own)]
        flips = [(dx, dy, dc) for dx in (0, 1) for dy in (0, 1) for dc in (0, 1)][1:]
        for q, (dx, dy, dc) in enumerate(flips):
            to = (x + dx - 2 * x * dx, y + dy - 2 * y * dy, c + dc - 2 * c * dc)
            cps.append(pltpu.make_async_remote_copy(src_ref=ins[0], dst_ref=slot, send_sem=send.at[q],
                                                    recv_sem=recv.at[q], device_id=to, device_id_type=MESH))
        return cps


def _sum_slots(name, slots):
    n, R, C = slots.shape

    def body(s_ref, o_ref):
        acc = s_ref[0]
        for d in range(1, n):
            acc = acc + s_ref[d]
        o_ref[...] = acc

    return pl.pallas_call(
        body, name=name, grid=(1,),
        in_specs=[pl.BlockSpec((n, R, C), lambda i: (0, 0, 0))], out_specs=pl.BlockSpec((R, C), lambda i: (0, 0)),
        out_shape=_sds((R, C), F32),
        compiler_params=_cparams(1, _nbytes((n + 1, R, C), F32)))(*_pin([slots]))


def _adamw(name, w, g, m, v, *, tr):
    R, C = w.shape

    def body(w_ref, g_ref, m_ref, v_ref, d_ref, nm_ref, nv_ref):
        gv = g_ref[...]
        nm = ADAM_B1 * m_ref[...] + (1.0 - ADAM_B1) * gv
        nv = ADAM_B2 * v_ref[...] + (1.0 - ADAM_B2) * (gv * gv)
        m_hat = nm / (1.0 - ADAM_B1 ** ADAM_STEP)
        v_hat = nv / (1.0 - ADAM_B2 ** ADAM_STEP)
        d_ref[...] = -ADAM_LR * (m_hat / (jnp.sqrt(v_hat) + ADAM_EPS) + ADAM_WD * w_ref[...])
        nm_ref[...] = nm
        nv_ref[...] = nv

    tile = pl.BlockSpec((tr, C), lambda i: (i, 0))
    return pl.pallas_call(
        body, name=name, grid=(R // tr,), in_specs=[tile] * 4, out_specs=[tile] * 3,
        out_shape=[_sds((R, C), F32)] * 3,
        compiler_params=_cparams(1, 7 * _nbytes((tr, C), F32)))(*_pin([w, g, m, v]))


def _pack_small(d):
    parts = []
    for name in SMALL:
        a = d[name].reshape(-1, HEAD)
        pad = (-a.shape[0]) % 8
        parts.append(jnp.pad(a, ((0, pad), (0, 0))))
    return jnp.concatenate(parts, axis=0)


def _unpack_small(packed, like):
    out = {}
    row = 0
    for name in SMALL:
        shape = like[name].shape
        rows = like[name].size // HEAD
        out[name] = packed[row:row + rows].reshape(shape)
        row += rows + (-rows) % 8
    return out


def kernel(x, mem, positions, mix_norm_g, w_in, q_norm_g, k_norm_g, pool_w, pool_scale, w_out, cross_norm_g, mem_norm_g, w_cq, w_ckv, cq_norm_g, ck_norm_g, w_co, ffn_norm_g, w_gate_up, w_down, loss_target, m_mix_norm_g, m_w_in, m_q_norm_g, m_k_norm_g, m_pool_w, m_pool_scale, m_w_out, m_cross_norm_g, m_mem_norm_g, m_w_cq, m_w_ckv, m_cq_norm_g, m_ck_norm_g, m_w_co, m_ffn_norm_g, m_w_gate_up, m_w_down, v_mix_norm_g, v_w_in, v_q_norm_g, v_k_norm_g, v_pool_w, v_pool_scale, v_w_out, v_cross_norm_g, v_mem_norm_g, v_w_cq, v_w_ckv, v_cq_norm_g, v_ck_norm_g, v_w_co, v_ffn_norm_g, v_w_gate_up, v_w_down):
    w = dict(mix_norm_g=mix_norm_g, w_in=w_in, q_norm_g=q_norm_g, k_norm_g=k_norm_g, pool_w=pool_w,
             pool_scale=pool_scale, w_out=w_out, cross_norm_g=cross_norm_g, mem_norm_g=mem_norm_g, w_cq=w_cq,
             w_ckv=w_ckv, cq_norm_g=cq_norm_g, ck_norm_g=ck_norm_g, w_co=w_co, ffn_norm_g=ffn_norm_g,
             w_gate_up=w_gate_up, w_down=w_down)
    m = dict(mix_norm_g=m_mix_norm_g, w_in=m_w_in, q_norm_g=m_q_norm_g, k_norm_g=m_k_norm_g, pool_w=m_pool_w,
             pool_scale=m_pool_scale, w_out=m_w_out, cross_norm_g=m_cross_norm_g, mem_norm_g=m_mem_norm_g,
             w_cq=m_w_cq, w_ckv=m_w_ckv, cq_norm_g=m_cq_norm_g, ck_norm_g=m_ck_norm_g, w_co=m_w_co,
             ffn_norm_g=m_ffn_norm_g, w_gate_up=m_w_gate_up, w_down=m_w_down)
    v = dict(mix_norm_g=v_mix_norm_g, w_in=v_w_in, q_norm_g=v_q_norm_g, k_norm_g=v_k_norm_g, pool_w=v_pool_w,
             pool_scale=v_pool_scale, w_out=v_w_out, cross_norm_g=v_cross_norm_g, mem_norm_g=v_mem_norm_g,
             w_cq=v_w_cq, w_ckv=v_w_ckv, cq_norm_g=v_cq_norm_g, ck_norm_g=v_ck_norm_g, w_co=v_w_co,
             ffn_norm_g=v_ffn_norm_g, w_gate_up=v_w_gate_up, w_down=v_w_down)

    c_arr = lax.axis_index("c").astype(jnp.int32).reshape(1)
    k_arr = (2 * lax.axis_index("x") + lax.axis_index("y")).astype(jnp.int32).reshape(1)
    kc_arr = jnp.concatenate([k_arr, c_arr])
    wb = {k: _cast_piece(f"cast_{k}", w[k][0], k_arr) for k in BIG}
    sm = {k: (w[k][0] if k == "pool_w" else w[k]) for k in SMALL}
    loss_part, dx, gshard, gsm = _local_step(x[0], mem[0], positions[0], loss_target[0], wb, sm,
                                             place=(c_arr, kc_arr))
    loss = lax.psum(loss_part, ("x", "y", "c"))

    grads, deltas, new_m, new_v = {}, {}, {}, {}
    for k in BIG:
        shard = w[k][0]
        tr = shard.shape[0] // 4
        d, nm, nv = _adamw(f"adamw_{k}", shard, gshard[k], m[k][0], v[k][0], tr=tr)
        grads[k], deltas[k], new_m[k], new_v[k] = gshard[k][None], d[None], nm[None], nv[None]
    smw = {k: (w[k][0] if k == "pool_w" else w[k]) for k in SMALL}
    smm = {k: (m[k][0] if k == "pool_w" else m[k]) for k in SMALL}
    smv = {k: (v[k][0] if k == "pool_w" else v[k]) for k in SMALL}
    pw, pg, pm, pv = _pack_small(smw), _pack_small(gsm), _pack_small(smm), _pack_small(smv)
    d, nm, nv = _adamw("adamw_small", pw, pg, pm, pv, tr=pw.shape[0])
    for dst, packed in ((deltas, d), (new_m, nm), (new_v, nv)):
        un = _unpack_small(packed, sm)
        for k in SMALL:
            dst[k] = un[k].reshape(w[k].shape)
    for k in SMALL:
        grads[k] = gsm[k].reshape(w[k].shape)

    return (loss, dx[None], *[grads[k] for k in WEIGHTS], *[deltas[k] for k in WEIGHTS],
            *[new_m[k] for k in WEIGHTS], *[new_v[k] for k in WEIGHTS])
```

```python
import functools

import jax
import jax.numpy as jnp
from jax import lax
from jax.experimental import pallas as pl
from jax.experimental.pallas import tpu as pltpu

F32 = jnp.float32
BF16 = jnp.bfloat16
MESH = pl.DeviceIdType.MESH
ANY = pl.BlockSpec(memory_space=pl.ANY)

D_MODEL = 1024
HEAD = 128
N_HEADS = 4
DILATIONS = (1, 4, 16)
BLOCK = 128
Q_W = 1536
KV_W = 512
POOL_W = 512
POOL_WINDOWS = (2, 4, 8, 16)
POOL_HALO = 16
IN_W = 3072
ROT_DIM = 32
ROT_HALF = 16
ROPE_THETA = 500000.0
X_W = 512
D_FF = 2816
EPS = 1e-6
NEG_INF = -1e30
SCALE = HEAD ** -0.5
N_CHIPS = 4

ADAM_LR = 0.001
ADAM_B1 = 0.9
ADAM_B2 = 0.999
ADAM_EPS = 1e-08
ADAM_WD = 0.01
ADAM_STEP = 10

VMEM_BYTES_V7X = 64 * 2 ** 20
VMEM_LIMIT_MAX = 56 * 2 ** 20
VMEM_LIMIT_MIN = 24 * 2 ** 20


def _nbytes(shape, dtype):
    n = 1
    for s in shape:
        n *= s
    return n * jnp.dtype(dtype).itemsize


def _cparams(n_axes, block_bytes, scratch_bytes=0):
    est = 2 * (2 * block_bytes + scratch_bytes)
    lim = int(min(VMEM_LIMIT_MAX, max(VMEM_LIMIT_MIN, est)))
    return pltpu.CompilerParams(dimension_semantics=("arbitrary",) * n_axes, vmem_limit_bytes=lim)


def _bf(v):
    return v.astype(BF16)


def _dot(a, b):
    return jnp.dot(a, b, preferred_element_type=F32)


def _dot_nt(a, b):
    return lax.dot_general(a, b, (((1,), (1,)), ((), ())), preferred_element_type=F32)


def _dot_tn(a, b):
    return lax.dot_general(a, b, (((0,), (0,)), ((), ())), preferred_element_type=F32)


def _rstd(v):
    return lax.rsqrt(jnp.mean(v * v, axis=-1, keepdims=True) + EPS)


def _norm_bwd(dy, xv, g):
    r = _rstd(xv)
    xh = xv * r
    dxh = dy * g
    dx = r * (dxh - xh * jnp.mean(dxh * xh, axis=-1, keepdims=True))
    return dx, xh


def _place():
    x, y, c = lax.axis_index("x"), lax.axis_index("y"), lax.axis_index("c")
    other_chips = [(1 - x, y), (x, 1 - y), (1 - x, 1 - y)]
    return x, y, c, other_chips


class _Plan:
    ins = ()
    out_shapes = ()
    aliases = {}
    sem_shapes = ()

    def copies(self, ins, outs, sems):
        raise NotImplementedError

    def begin(self, ins, outs, sems):
        for cp in self.copies(ins, outs, sems):
            cp.start()

    def finish(self, ins, outs, sems):
        for cp in self.copies(ins, outs, sems):
            cp.wait()


class _GatherPlan(_Plan):
    def __init__(self, bufs):
        n = len(bufs)
        self.ins = list(bufs)
        self.out_shapes = [_sds(b.shape, b.dtype) for b in bufs]
        self.aliases = {i: i for i in range(n)}
        self.sem_shapes = [pltpu.SemaphoreType.DMA((n, 6)), pltpu.SemaphoreType.DMA((n, 6))]

    def _parts(self, outs, sems):
        send, recv = sems
        x, y, c, chips = _place()

        def half(i, piece, which):
            hr = outs[i].shape[1] // 2
            return outs[i].at[piece, pl.ds(which * hr, hr), :]

        def copy(i, k, ref, to):
            return pltpu.make_async_remote_copy(src_ref=ref, dst_ref=ref, send_sem=send.at[i, k], recv_sem=recv.at[i, k],
                                                device_id=to, device_id_type=MESH)

        return x, y, c, chips, half, copy

    def begin(self, ins, outs, sems):
        x, y, c, chips, half, copy = self._parts(outs, sems)
        for i in range(len(outs)):
            for j, (cx, cy) in enumerate(chips):
                copy(i, j, half(i, 2 * x + y, c), (cx, cy, c)).start()

    def finish(self, ins, outs, sems):
        x, y, c, chips, half, copy = self._parts(outs, sems)
        sib = (x, y, 1 - c)
        n = len(outs)
        for i in range(n):
            for j, (cx, cy) in enumerate(chips):
                piece = half(i, 2 * cx + cy, c)
                copy(i, j, piece, (cx, cy, c)).wait_recv()
                copy(i, 3 + j, piece, sib).start()
        for i in range(n):
            for j, (cx, cy) in enumerate(chips):
                copy(i, 3 + j, half(i, 2 * cx + cy, 1 - c), sib).wait_recv()
        for i in range(n):
            for j, (cx, cy) in enumerate(chips):
                copy(i, j, half(i, 2 * x + y, c), (cx, cy, c)).wait_send()
                copy(i, 3 + j, half(i, 2 * cx + cy, c), sib).wait_send()


class _SwapPlan(_Plan):
    def __init__(self, grads):
        n = len(grads)
        self.ins = list(grads)
        self.out_shapes = [_sds((g.shape[0], g.shape[1] // 2, g.shape[2]), g.dtype) for g in grads]
        self.sem_shapes = [pltpu.SemaphoreType.DMA((n,)), pltpu.SemaphoreType.DMA((n,))]

    def copies(self, ins, outs, sems):
        send, recv = sems
        x, y, c, _ = _place()
        cps = []
        for i in range(len(ins)):
            hr = ins[i].shape[1] // 2
            cps.append(pltpu.make_async_remote_copy(
                src_ref=ins[i].at[:, pl.ds((1 - c) * hr, hr), :], dst_ref=outs[i], send_sem=send.at[i],
                recv_sem=recv.at[i], device_id=(x, y, 1 - c), device_id_type=MESH))
        return cps


class _ExchangePlan(_Plan):
    def __init__(self, sums):
        n = len(sums)
        self.ins = list(sums)
        self.out_shapes = [_sds((N_CHIPS - 1,) + s.shape[1:], s.dtype) for s in sums]
        self.sem_shapes = [pltpu.SemaphoreType.DMA((n, 3)), pltpu.SemaphoreType.DMA((n, 3))]

    def copies(self, ins, outs, sems):
        send, recv = sems
        x, y, c, chips = _place()
        cps = []
        for i in range(len(ins)):
            for j, (cx, cy) in enumerate(chips):
                cps.append(pltpu.make_async_remote_copy(
                    src_ref=ins[i].at[2 * cx + cy], dst_ref=outs[i].at[j], send_sem=send.at[i, j],
                    recv_sem=recv.at[i, j], device_id=(cx, cy, c), device_id_type=MESH))
        return cps


class _JoinPlan(_Plan):
    def __init__(self, shards):
        n = len(shards)
        self.ins = list(shards)
        self.out_shapes = [_sds(s.shape, s.dtype) for s in shards]
        self.aliases = {i: i for i in range(n)}
        self.sem_shapes = [pltpu.SemaphoreType.DMA((n,)), pltpu.SemaphoreType.DMA((n,))]

    def copies(self, ins, outs, sems):
        send, recv = sems
        x, y, c, _ = _place()
        cps = []
        for i in range(len(outs)):
            hr = outs[i].shape[0] // 2
            mine = outs[i].at[pl.ds(c * hr, hr), :]
            cps.append(pltpu.make_async_remote_copy(src_ref=mine, dst_ref=mine, send_sem=send.at[i], recv_sem=recv.at[i],
                                                    device_id=(x, y, 1 - c), device_id_type=MESH))
        return cps


def _run_plan(name, plan):
    n_in, n_out = len(plan.ins), len(plan.out_shapes)

    def body(*refs):
        ins, outs, sems = refs[:n_in], refs[n_in:n_in + n_out], refs[n_in + n_out:]
        plan.begin(ins, outs, sems)
        plan.finish(ins, outs, sems)

    return pl.pallas_call(
        body, name=name, in_specs=[ANY] * n_in, out_specs=[ANY] * n_out, out_shape=list(plan.out_shapes),
        input_output_aliases=dict(plan.aliases), scratch_shapes=list(plan.sem_shapes))(*plan.ins)


HBM_PIN_BYTES = 1 << 20


def _sds(shape, dtype):
    if _nbytes(shape, dtype) >= HBM_PIN_BYTES:
        return pltpu.HBM(shape, dtype)
    return jax.ShapeDtypeStruct(shape, dtype)


def _pin(args):
    return [pltpu.with_memory_space_constraint(a, pltpu.HBM) if _nbytes(a.shape, a.dtype) >= HBM_PIN_BYTES else a
            for a in args]


def _pcall(body, *, name, grid, in_specs, out_specs, out_shape, params, args, scratch_shapes=(), plan=None):
    in_specs, out_specs, out_shape, scratch = list(in_specs), list(out_specs), list(out_shape), list(scratch_shapes)
    args = _pin(args)
    if plan is None:
        res = pl.pallas_call(body, name=name, grid=grid, in_specs=in_specs, out_specs=out_specs, out_shape=out_shape,
                             scratch_shapes=scratch, compiler_params=params)(*args)
        return list(res), []
    ni, no, ns = len(in_specs), len(out_specs), len(scratch)
    pi, po = len(plan.ins), len(plan.out_shapes)

    def wrapped(*refs):
        ins, pins = refs[:ni], refs[ni:ni + pi]
        outs, pouts = refs[ni + pi:ni + pi + no], refs[ni + pi + no:ni + pi + no + po]
        scr, psems = refs[ni + pi + no + po:ni + pi + no + po + ns], refs[ni + pi + no + po + ns:]
        ids = [pl.program_id(a) for a in range(len(grid))]
        first = functools.reduce(jnp.logical_and, [i == 0 for i in ids])
        last = functools.reduce(jnp.logical_and, [i == g - 1 for i, g in zip(ids, grid)])

        @pl.when(first)
        def _():
            plan.begin(pins, pouts, psems)

        body(*ins, *outs, *scr)

        @pl.when(last)
        def _():
            plan.finish(pins, pouts, psems)

    res = pl.pallas_call(
        wrapped, name=name, grid=grid, in_specs=in_specs + [ANY] * pi, out_specs=out_specs + [ANY] * po,
        out_shape=out_shape + list(plan.out_shapes), scratch_shapes=scratch + list(plan.sem_shapes),
        input_output_aliases={ni + a: no + b for a, b in plan.aliases.items()},
        compiler_params=params)(*args, *plan.ins)
    return list(res[:no]), list(res[no:])


def _mm_nn(name, a, w3, *, tm, norm_g=None, residual=None, out_dtype=F32, plan=None):
    M, K = a.shape
    P, Kw, C = w3.shape
    assert Kw == K and M % tm == 0
    N = P * C
    has_norm = norm_g is not None
    has_res = residual is not None

    def body(*refs):
        refs = list(refs)
        a_ref = refs.pop(0)
        w_ref = refs.pop(0)
        g_ref = refs.pop(0) if has_norm else None
        r_ref = refs.pop(0) if has_res else None
        o_ref = refs.pop(0)
        xn_ref = refs.pop(0) if has_norm else None
        if has_norm:
            av = a_ref[...].astype(F32)
            ab = _bf(av * _rstd(av) * g_ref[...])
            xn_ref[...] = ab
        else:
            ab = _bf(a_ref[...])
        for p in range(P):
            acc = _dot(ab, w_ref[p])
            if has_res:
                acc = acc + r_ref[:, p * C:(p + 1) * C]
            o_ref[:, p * C:(p + 1) * C] = acc.astype(o_ref.dtype)

    in_specs = [pl.BlockSpec((tm, K), lambda i: (i, 0)), pl.BlockSpec((P, K, C), lambda i: (0, 0, 0))]
    args = [a, w3]
    if has_norm:
        in_specs.append(pl.BlockSpec((1, K), lambda i: (0, 0)))
        args.append(norm_g)
    if has_res:
        in_specs.append(pl.BlockSpec((tm, N), lambda i: (i, 0)))
        args.append(residual)
    out_shape = [_sds((M, N), out_dtype)]
    out_specs = [pl.BlockSpec((tm, N), lambda i: (i, 0))]
    if has_norm:
        out_shape.append(_sds((M, K), BF16))
        out_specs.append(pl.BlockSpec((tm, K), lambda i: (i, 0)))
    blk = (_nbytes((tm, K), a.dtype) + _nbytes((P, K, C), BF16) + 2 * _nbytes((tm, N), F32)
           + _nbytes((tm, K), BF16))
    res, got = _pcall(body, name=name, grid=(M // tm,), in_specs=in_specs, out_specs=out_specs, out_shape=out_shape,
                      params=_cparams(1, blk), args=args, plan=plan)
    res = res if has_norm else res[0]
    return res if plan is None else (res, got)


MXU_COLS_V7X = 256


def _col_chunks(n):
    return [(c, min(c + MXU_COLS_V7X, n)) for c in range(0, n, MXU_COLS_V7X)]


def _ffn_up(name, h, wgu3, g, *, tm):
    M, K = h.shape
    P, _, C = wgu3.shape
    half = P // 2

    def body(h_ref, w_ref, g_ref, act_ref, fac_ref, xn_ref):
        hv = h_ref[...]
        ab = _bf(hv * _rstd(hv) * g_ref[...])
        xn_ref[...] = ab
        for j in range(half):
            for c0, c1 in _col_chunks(C):
                gate = _dot(ab, w_ref[j, :, c0:c1])
                up = _dot(ab, w_ref[half + j, :, c0:c1])
                sig = 1.0 / (1.0 + jnp.exp(-gate))
                silu = gate * sig
                cols = slice(j * C + c0, j * C + c1)
                act_ref[:, cols] = _bf(silu * up)
                fac_ref[0, :, cols] = _bf(up * (sig + silu * (1.0 - sig)))
                fac_ref[1, :, cols] = _bf(silu)

    F = half * C
    blk = (_nbytes((tm, K), F32) + _nbytes((P, K, C), BF16) // 2 + 3 * _nbytes((tm, F), BF16)
           + _nbytes((tm, K), BF16))
    return pl.pallas_call(
        body, name=name, grid=(M // tm,),
        in_specs=[pl.BlockSpec((tm, K), lambda i: (i, 0)),
                  pl.BlockSpec((P, K, C), lambda i: (0, 0, 0), pipeline_mode=pl.Buffered(1)),
                  pl.BlockSpec((1, K), lambda i: (0, 0))],
        out_specs=[pl.BlockSpec((tm, F), lambda i: (i, 0)),
                   pl.BlockSpec((2, tm, F), lambda i: (0, i, 0)),
                   pl.BlockSpec((tm, K), lambda i: (i, 0))],
        out_shape=[_sds((M, F), BF16), _sds((2, M, F), BF16), _sds((M, K), BF16)],
        compiler_params=_cparams(1, blk))(*_pin([h, wgu3, g]))


def _ffn_down_loss(name, act, wd3, h, target, *, tm):
    M, K = act.shape
    _, _, N = wd3.shape

    def body(a_ref, w_ref, h_ref, t_ref, dy_ref, ls_ref):
        err = _dot(a_ref[...], w_ref[...]) + h_ref[...] - t_ref[...]
        dy_ref[...] = err * (1.0 / N)

        @pl.when(pl.program_id(0) == 0)
        def _():
            ls_ref[...] = jnp.zeros_like(ls_ref)
        ls_ref[...] += jnp.sum(err * err, axis=0, keepdims=True)

    blk = _nbytes((tm, K), BF16) + _nbytes((K, N), BF16) + 4 * _nbytes((tm, N), F32)
    return pl.pallas_call(
        body, name=name, grid=(M // tm,),
        in_specs=[pl.BlockSpec((tm, K), lambda i: (i, 0)),
                  pl.BlockSpec((None, K, N), lambda i: (0, 0, 0)),
                  pl.BlockSpec((tm, N), lambda i: (i, 0)),
                  pl.BlockSpec((tm, N), lambda i: (i, 0))],
        out_specs=[pl.BlockSpec((tm, N), lambda i: (i, 0)),
                   pl.BlockSpec((1, N), lambda i: (0, 0))],
        out_shape=[_sds((M, N), F32), _sds((1, N), F32)],
        compiler_params=_cparams(1, blk))(*_pin([act, wd3, h, target]))


def _nt_pieces(a_ref, w_ref):
    P, _, C = w_ref.shape
    acc = _dot_nt(_bf(a_ref[:, 0:C]), w_ref[0])
    for p in range(1, P):
        acc = acc + _dot_nt(_bf(a_ref[:, p * C:(p + 1) * C]), w_ref[p])
    return acc


def _mm_nt_normbwd(name, a, w3, h, g, dres, *, tm, plan=None):
    M, D = h.shape
    P, Ko, C = w3.shape
    N = a.shape[1]
    assert N == P * C and Ko == D and M % tm == 0
    has_res = dres is not None

    def body(*refs):
        a_ref, w_ref, h_ref, g_ref = refs[:4]
        r_ref = refs[4] if has_res else None
        dx_ref, dg_ref = refs[-2:]
        dhn = _nt_pieces(a_ref, w_ref)
        dx, xh = _norm_bwd(dhn, h_ref[...], g_ref[...])
        if has_res:
            dx = dx + r_ref[...]
        dx_ref[...] = dx

        @pl.when(pl.program_id(0) == 0)
        def _():
            dg_ref[...] = jnp.zeros_like(dg_ref)
        dg_ref[...] += jnp.sum(dhn * xh, axis=0, keepdims=True)

    row = pl.BlockSpec((tm, D), lambda i: (i, 0))
    vec = pl.BlockSpec((1, D), lambda i: (0, 0))
    in_specs = [pl.BlockSpec((tm, N), lambda i: (i, 0)), pl.BlockSpec((P, Ko, C), lambda i: (0, 0, 0)), row, vec]
    args = [a, w3, h, g]
    if has_res:
        in_specs.append(row)
        args.append(dres)
    blk = _nbytes((tm, N), a.dtype) + _nbytes((P, Ko, C), BF16) + 5 * _nbytes((tm, D), F32)
    res, got = _pcall(body, name=name, grid=(M // tm,), in_specs=in_specs, out_specs=[row, vec],
                      out_shape=[_sds((M, D), F32), _sds((1, D), F32)],
                      params=_cparams(1, blk), args=args, plan=plan)
    return res if plan is None else (res, got)


def _ffn_bwd(name, dy, wd3, fac, wgu3, h, g, *, tm):
    M, D = h.shape
    P, _, C = wgu3.shape
    F = fac.shape[2]
    per = F // C

    def body(dy_ref, wd_ref, f_ref, w_ref, h_ref, g_ref, dgu_ref, dx_ref, dg_ref):
        dyv = dy_ref[...]
        ab = _bf(dyv)
        dhn = None
        for pc in range(per):
            cols = slice(pc * C, (pc + 1) * C)
            da = _dot_nt(ab, wd_ref[cols, :])
            dgate = _bf(da * f_ref[0, :, cols].astype(F32))
            dup = _bf(da * f_ref[1, :, cols].astype(F32))
            dgu_ref[0, :, cols] = dgate
            dgu_ref[1, :, cols] = dup
            part = _dot_nt(dgate, w_ref[pc]) + _dot_nt(dup, w_ref[per + pc])
            dhn = part if dhn is None else dhn + part
        dx, xh = _norm_bwd(dhn, h_ref[...], g_ref[...])
        dx_ref[...] = dx + dyv

        @pl.when(pl.program_id(0) == 0)
        def _():
            dg_ref[...] = jnp.zeros_like(dg_ref)
        dg_ref[...] += jnp.sum(dhn * xh, axis=0, keepdims=True)

    row = pl.BlockSpec((tm, D), lambda i: (i, 0))
    vec = pl.BlockSpec((1, D), lambda i: (0, 0))
    planes = pl.BlockSpec((2, tm, F), lambda i: (0, i, 0))
    blk = (2 * _nbytes((2, tm, F), BF16) + (_nbytes((F, D), BF16) + _nbytes((P, D, C), BF16)) // 2
           + 4 * _nbytes((tm, D), F32))
    return pl.pallas_call(
        body, name=name, grid=(M // tm,),
        in_specs=[row, pl.BlockSpec((None, F, D), lambda i: (0, 0, 0), pipeline_mode=pl.Buffered(1)), planes,
                  pl.BlockSpec((P, D, C), lambda i: (0, 0, 0), pipeline_mode=pl.Buffered(1)), row, vec],
        out_specs=[planes, row, vec],
        out_shape=[_sds((2, M, F), BF16), _sds((M, D), F32), _sds((1, D), F32)],
        compiler_params=_cparams(1, blk))(*_pin([dy, wd3, fac, wgu3, h, g]))


def _mm_tn_wide(name, a, b, pieces, *, ts, plan=None):
    S, K = a.shape
    N = b.shape[1]
    C = N // pieces

    def body(a_ref, b_ref, o_ref):
        @pl.when(pl.program_id(0) == 0)
        def _():
            o_ref[...] = jnp.zeros_like(o_ref)
        acc = _dot_tn(_bf(a_ref[...]), _bf(b_ref[...]))
        for p in range(pieces):
            o_ref[p] += acc[:, p * C:(p + 1) * C]

    blk = _nbytes((ts, K), a.dtype) + _nbytes((ts, N), b.dtype) + 2 * _nbytes((K, N), F32)
    res, got = _pcall(body, name=name, grid=(S // ts,),
                      in_specs=[pl.BlockSpec((ts, K), lambda s: (s, 0)), pl.BlockSpec((ts, N), lambda s: (s, 0))],
                      out_specs=[pl.BlockSpec((pieces, K, C), lambda s: (0, 0, 0))],
                      out_shape=[_sds((pieces, K, C), F32)], params=_cparams(1, blk), args=[a, b], plan=plan)
    return res[0] if plan is None else (res[0], got)


def _mm_tn(name, a, a_spec, a_blk, b, b_spec, b_blk, out3, o_spec, o_blk, grid, plan=None):
    def body(a_ref, b_ref, o_ref):
        @pl.when(pl.program_id(2) == 0)
        def _():
            o_ref[...] = jnp.zeros_like(o_ref)
        o_ref[...] += _dot_tn(_bf(a_ref[...]), _bf(b_ref[...]))

    blk = _nbytes(a_blk, a.dtype) + _nbytes(b_blk, b.dtype) + 2 * _nbytes(o_blk, F32)
    res, got = _pcall(body, name=name, grid=grid, in_specs=[a_spec, b_spec], out_specs=[o_spec],
                      out_shape=[_sds(out3, F32)], params=_cparams(3, blk), args=[a, b], plan=plan)
    return res[0] if plan is None else (res[0], got)


def _swap_halves(v, lane, masked):
    up = pltpu.roll(v, HEAD - ROT_HALF, 1)
    down = pltpu.roll(v, ROT_HALF, 1)
    rest = jnp.where(lane < ROT_DIM, down, 0.0) if masked else down
    return jnp.where(lane < ROT_HALF, up, rest)


def _in_proj(name, x, w3, g, ctab, stab, qg, kg, *, tm, plan=None):
    S, K = x.shape
    P, _, C = w3.shape

    def body(x_ref, w_ref, g_ref, c_ref, s_ref, qg_ref, kg_ref, p_ref, xn_ref, q_ref, k_ref):
        xv = x_ref[...]
        ab = _bf(xv * _rstd(xv) * g_ref[...])
        xn_ref[...] = ab
        for p in range(P):
            p_ref[:, p * C:(p + 1) * C] = _dot(ab, w_ref[p])
        lane = lax.broadcasted_iota(jnp.int32, (tm, HEAD), 1)
        cv = c_ref[...]
        sv = s_ref[...]

        def prep(t, gain, scale):
            n = t * _rstd(t) * gain
            return (n * cv + _swap_halves(n, lane, False) * sv) * scale

        for j in range(Q_W // HEAD):
            q_ref[:, j * HEAD:(j + 1) * HEAD] = prep(p_ref[:, j * HEAD:(j + 1) * HEAD], qg_ref[...], SCALE)
        for j in range(KV_W // HEAD):
            k_ref[:, j * HEAD:(j + 1) * HEAD] = prep(p_ref[:, Q_W + j * HEAD:Q_W + (j + 1) * HEAD], kg_ref[...], 1.0)

    row = lambda width: pl.BlockSpec((tm, width), lambda i: (i, 0))
    vec = lambda width: pl.BlockSpec((1, width), lambda i: (0, 0))
    blk = (_nbytes((tm, K), F32) + _nbytes((P, K, C), BF16) + _nbytes((tm, P * C + Q_W + KV_W + 2 * HEAD), F32)
           + _nbytes((tm, K), BF16))
    res, got = _pcall(
        body, name=name, grid=(S // tm,),
        in_specs=[row(K), pl.BlockSpec((P, K, C), lambda i: (0, 0, 0)), vec(K), row(HEAD), row(HEAD), vec(HEAD), vec(HEAD)],
        out_specs=[row(P * C), row(K), row(Q_W), row(KV_W)],
        out_shape=[_sds((S, P * C), F32), _sds((S, K), BF16), _sds((S, Q_W), F32), _sds((S, KV_W), F32)],
        params=_cparams(1, blk), args=[x, w3, g, ctab, stab, qg, kg], plan=plan)
    return res if plan is None else (res, got)


ATTN_STEP = 2048


def _row_idx(start, dil):
    return pl.ds(start, BLOCK) if dil == 1 else pl.ds(start, BLOCK, stride=dil)


def _rows(ref, start, dil):
    return ref[_row_idx(start, dil), :]


def _set_rows(ref, start, dil, val):
    ref[_row_idx(start, dil), :] = val


def _band_mask(first):
    qi = lax.broadcasted_iota(jnp.int32, (BLOCK, 2 * BLOCK), 0)
    kj = lax.broadcasted_iota(jnp.int32, (BLOCK, 2 * BLOCK), 1)
    band = (kj >= qi) & (kj <= qi + BLOCK)
    if first is None:
        return band
    return band & ((kj >= BLOCK) | jnp.logical_not(first))


def _attn_geometry(S, grp):
    dil = DILATIONS[grp]
    bt = BLOCK * dil
    assert S % ATTN_STEP == 0 and ATTN_STEP % bt == 0
    return dil, bt, ATTN_STEP // bt, S // ATTN_STEP


MERGE_ROWS = 256


def _attn_fwd(name, qn, kn, proj, plan=None):
    S = qn.shape[0]
    groups = range(len(DILATIONS))
    geo = [_attn_geometry(S, g) for g in groups]
    nsb = geo[0][3]
    vcol = (Q_W + KV_W) // HEAD

    def body(q0, q1, q2, kc_ref, vc_ref, kp0, kp1, kp2, vp0, vp1, vp2, mix_ref, lse_ref, o_s, l_s):
        q_refs, kp_refs, vp_refs = (q0, q1, q2), (kp0, kp1, kp2), (vp0, vp1, vp2)
        valid_first = _band_mask(pl.program_id(1) == 0)
        valid_inner = _band_mask(None)
        for g in groups:
            dil, bt, nsub, _ = geo[g]
            o_g, l_g = o_s.at[g], l_s.at[g]
            for r in range(dil):
                kprev, vprev = _bf(_rows(kp_refs[g], r, dil)), _bf(_rows(vp_refs[g], r, dil))
                for b in range(nsub):
                    at = b * bt + r
                    kcur, vcur = _bf(_rows(kc_ref, at, dil)), _bf(_rows(vc_ref, at, dil))
                    q = _bf(_rows(q_refs[g], at, dil))
                    k2 = jnp.concatenate([kprev, kcur], axis=0)
                    v2 = jnp.concatenate([vprev, vcur], axis=0)
                    s = jnp.where(valid_first if b == 0 else valid_inner, _dot_nt(q, k2), NEG_INF)
                    m = jnp.max(s, axis=-1, keepdims=True)
                    p = jnp.exp(s - m)
                    l = jnp.sum(p, axis=-1, keepdims=True)
                    acc = _dot(_bf(p), v2)
                    _set_rows(o_g, at, dil, acc / l)
                    _set_rows(l_g, at, dil, jnp.broadcast_to(m + jnp.log(l), (BLOCK, HEAD)))
                    kprev, vprev = kcur, vcur
        for c in range(ATTN_STEP // MERGE_ROWS):
            rows = pl.ds(c * MERGE_ROWS, MERGE_ROWS)
            ls = [l_s[g, rows, :] for g in groups]
            m = jnp.maximum(jnp.maximum(ls[0], ls[1]), ls[2])
            ws = [jnp.exp(v - m) for v in ls]
            den = ws[0] + ws[1] + ws[2]
            num = ws[0] * o_s[0, rows, :] + ws[1] * o_s[1, rows, :] + ws[2] * o_s[2, rows, :]
            mix_ref[rows, :] = _bf(num / den)
            lse_ref[rows, :] = m + jnp.log(den)

    def big(col):
        return pl.BlockSpec((ATTN_STEP, HEAD), lambda h, n: (n, col + h))

    def tail(g, col):
        _, bt, nsub, _ = geo[g]
        return pl.BlockSpec((bt, HEAD), lambda h, n: (jnp.maximum(n * nsub - 1, 0), col + h))

    blk = 7 * _nbytes((ATTN_STEP, HEAD), F32) + 2 * sum(_nbytes((geo[g][1], HEAD), F32) for g in groups)
    scratch = 2 * len(DILATIONS) * _nbytes((ATTN_STEP, HEAD), F32)
    res, got = _pcall(
        body, name=name, grid=(N_HEADS, nsb),
        in_specs=[big(g * N_HEADS) for g in groups] + [big(0), big(vcol)]
                 + [tail(g, 0) for g in groups] + [tail(g, vcol) for g in groups],
        out_specs=[big(0), big(0)],
        out_shape=[_sds((S, 2 * KV_W), BF16), _sds((S, KV_W), F32)],
        scratch_shapes=[pltpu.VMEM((len(DILATIONS), ATTN_STEP, HEAD), F32)] * 2,
        params=_cparams(2, blk, scratch), args=[qn, qn, qn, kn, proj, kn, kn, kn, proj, proj, proj], plan=plan)
    return res if plan is None else (res, got)


def _attn_bwd(name, qn, kn, proj, dattn, lse, delta, grp, plan=None):
    S = qn.shape[0]
    dil, bt, nsub, nsb = _attn_geometry(S, grp)
    vcol = (Q_W + KV_W) // HEAD

    def slot(b, r):
        return pl.ds((b * dil + r) * BLOCK, BLOCK)

    def body(q_ref, kp_ref, kc_ref, vp_ref, vc_ref, do_ref, l_ref, d_ref, dq_ref, dk_ref, dv_ref, ck_ref, cv_ref):
        n = pl.program_id(1)
        par = n % 2

        @pl.when(n < nsb)
        def _():
            valid_first = _band_mask(n == 0)
            valid_inner = _band_mask(None)
            ck, cv = ck_ref.at[par], cv_ref.at[par]
            pk, pv = ck_ref.at[1 - par], cv_ref.at[1 - par]
            for r in range(dil):
                kprev, vprev = _bf(_rows(kp_ref, r, dil)), _bf(_rows(vp_ref, r, dil))
                own_k = own_v = None
                for b in range(nsub):
                    at = b * bt + r
                    kcur, vcur = _bf(_rows(kc_ref, at, dil)), _bf(_rows(vc_ref, at, dil))
                    q = _bf(_rows(q_ref, at, dil))
                    k2 = jnp.concatenate([kprev, kcur], axis=0)
                    v2 = jnp.concatenate([vprev, vcur], axis=0)
                    do = _bf(_rows(do_ref, at, dil))
                    lse_r = _rows(l_ref, at, dil)[:, :1]
                    del_r = _rows(d_ref, at, dil)[:, :1]
                    s = jnp.where(valid_first if b == 0 else valid_inner, _dot_nt(q, k2), NEG_INF)
                    p = jnp.exp(s - lse_r)
                    ds = _bf(p * (_dot_nt(do, v2) - del_r))
                    _set_rows(dq_ref, at, dil, _dot(ds, k2))
                    dk2 = _dot_tn(ds, q)
                    dv2 = _dot_tn(_bf(p), do)
                    if b > 0:
                        ck[slot(b - 1, r), :] = own_k + dk2[:BLOCK]
                        cv[slot(b - 1, r), :] = own_v + dv2[:BLOCK]
                    else:
                        @pl.when(n > 0)
                        def _():
                            pk[slot(nsub - 1, r), :] += dk2[:BLOCK]
                            pv[slot(nsub - 1, r), :] += dv2[:BLOCK]
                    own_k, own_v = dk2[BLOCK:], dv2[BLOCK:]
                    kprev, vprev = kcur, vcur
                ck[slot(nsub - 1, r), :] = own_k
                cv[slot(nsub - 1, r), :] = own_v

        @pl.when(n > 0)
        def _():
            for r in range(dil):
                for b in range(nsub):
                    _set_rows(dk_ref, b * bt + r, dil, ck_ref[1 - par, slot(b, r), :])
                    _set_rows(dv_ref, b * bt + r, dil, cv_ref[1 - par, slot(b, r), :])

    cur = lambda n: jnp.minimum(n, nsb - 1)
    prev = lambda n: jnp.maximum(n - 1, 0)
    tail_at = lambda n: jnp.maximum(cur(n) * nsub - 1, 0)
    big = lambda col: pl.BlockSpec((ATTN_STEP, HEAD), lambda h, n: (cur(n), col(h)))
    tail = lambda col: pl.BlockSpec((bt, HEAD), lambda h, n: (tail_at(n), col(h)))
    late = pl.BlockSpec((ATTN_STEP, HEAD), lambda h, n: (prev(n), h))
    blk = 9 * _nbytes((ATTN_STEP, HEAD), F32) + 2 * _nbytes((bt, HEAD), F32)
    res, got = _pcall(
        body, name=name, grid=(N_HEADS, nsb + 1),
        in_specs=[big(lambda h: grp * N_HEADS + h), tail(lambda h: h), big(lambda h: h),
                  tail(lambda h: vcol + h), big(lambda h: vcol + h),
                  big(lambda h: h), big(lambda h: h), big(lambda h: h)],
        out_specs=[big(lambda h: h), late, late],
        out_shape=[_sds((S, KV_W), F32)] * 3,
        scratch_shapes=[pltpu.VMEM((2, ATTN_STEP, HEAD), F32), pltpu.VMEM((2, ATTN_STEP, HEAD), F32)],
        params=_cparams(2, blk, 4 * _nbytes((ATTN_STEP, HEAD), F32)),
        args=[qn, kn, kn, proj, proj, dattn, lse, delta], plan=plan)
    return res if plan is None else (res, got)


def _window_sum(v, n_doublings, back):
    rows = v.shape[0]
    step = 1
    for _ in range(n_doublings):
        v = v + pltpu.roll(v, step if back else rows - step, 0)
        step *= 2
    return v


def _pool_out_proj(name, mix, proj, pool_w, pool_scale, w_out3, x, *, tm):
    S, D = x.shape
    ucol = (IN_W - POOL_W) // POOL_W
    hpt = tm // POOL_HALO

    def body(at_ref, u_ref, uh_ref, pw_ref, ps_ref, w_ref, x_ref, pooled_ref, dp_ref, h_ref):
        i = pl.program_id(0)
        halo = jnp.where(i == 0, 0.0, uh_ref[...])
        t = lax.broadcasted_iota(jnp.int32, (tm + POOL_HALO, HEAD), 0) + (i * tm - POOL_HALO)
        for g, w in enumerate(POOL_WINDOWS):
            sl = slice(g * HEAD, (g + 1) * HEAD)
            ub = jnp.concatenate([halo[:, sl], u_ref[:, sl]], axis=0)
            cnt = jnp.minimum(t + 1, w).astype(F32)
            d = (_window_sum(ub, g + 1, True) / cnt - ub)[POOL_HALO:]
            db = _bf(d)
            dp_ref[:, sl] = db
            pooled_ref[:, sl] = _bf(_dot(db, pw_ref[g]) * ps_ref[:, sl])
        h_ref[...] = x_ref[...] + _dot(at_ref[...], w_ref[:KV_W, :]) + _dot(pooled_ref[...], w_ref[KV_W:, :])

    row = pl.BlockSpec((tm, D), lambda i: (i, 0))
    blk = (3 * _nbytes((tm, POOL_W), F32) + 3 * _nbytes((tm, POOL_W), BF16) + _nbytes((2 * KV_W, D), BF16)
           + 3 * _nbytes((tm, D), F32))
    return pl.pallas_call(
        body, name=name, grid=(S // tm,),
        in_specs=[pl.BlockSpec((tm, KV_W), lambda i: (i, 0)),
                  pl.BlockSpec((tm, POOL_W), lambda i: (i, ucol)),
                  pl.BlockSpec((POOL_HALO, POOL_W), lambda i: (jnp.maximum(i * hpt - 1, 0), ucol)),
                  pl.BlockSpec((len(POOL_WINDOWS), HEAD, HEAD), lambda i: (0, 0, 0)),
                  pl.BlockSpec((1, POOL_W), lambda i: (0, 0)),
                  pl.BlockSpec((None, 2 * KV_W, D), lambda i: (0, 0, 0)), row],
        out_specs=[pl.BlockSpec((tm, POOL_W), lambda i: (i, 1)), pl.BlockSpec((tm, POOL_W), lambda i: (i, 0)), row],
        out_shape=[_sds((S, 2 * KV_W), BF16), _sds((S, POOL_W), BF16), _sds((S, D), F32)],
        input_output_aliases={0: 0},
        compiler_params=_cparams(1, blk))(*_pin([mix, proj, proj, pool_w, pool_scale, w_out3, x]))


def _mix_bwd(name, dh, w_out3, mix, dpool, pool_w, pool_scale, *, tm, plan=None):
    S, D = dh.shape
    hpt = tm // POOL_HALO
    last_halo = S // POOL_HALO - 1
    n_tiles = S // tm

    def body(dh_ref, dhn_ref, w_ref, at_ref, dp_ref, pw_ref, ps_ref, da_ref, dl_ref, du_ref, gw_ref, gs_ref):
        i = pl.program_id(0)

        @pl.when(i == 0)
        def _():
            gw_ref[...] = jnp.zeros_like(gw_ref)
            gs_ref[...] = jnp.zeros_like(gs_ref)

        dm = _dot_nt(_bf(dh_ref[...]), w_ref[...])
        ahead = _dot_nt(_bf(dhn_ref[...]), w_ref[KV_W:, :])
        for h in range(N_HEADS):
            sl = slice(h * HEAD, (h + 1) * HEAD)
            da = dm[:, sl]
            da_ref[:, sl] = da
            dl_ref[:, sl] = jnp.broadcast_to(
                jnp.sum(da * at_ref[:, sl].astype(F32), axis=-1, keepdims=True), (tm, HEAD))
        halo = jnp.where(i == n_tiles - 1, 0.0, ahead)
        t = lax.broadcasted_iota(jnp.int32, (tm + POOL_HALO, HEAD), 0) + i * tm
        for g, w in enumerate(POOL_WINDOWS):
            sl = slice(g * HEAD, (g + 1) * HEAD)
            dy = jnp.concatenate([dm[:, KV_W + g * HEAD:KV_W + (g + 1) * HEAD], halo[:, sl]], axis=0)
            dys = _bf(dy * ps_ref[:, sl])
            dd = _dot_nt(dys, pw_ref[g])
            cnt = jnp.minimum(t + 1, w).astype(F32)
            du_ref[:, sl] = (_window_sum(dd / cnt, g + 1, False) - dd)[:tm]
            db = dp_ref[:, sl]
            gw_ref[g] += _dot_tn(db, dys[:tm])
            gs_ref[:, sl] += jnp.sum(dy[:tm] * _dot(db, pw_ref[g]), axis=0, keepdims=True)

    tile = pl.BlockSpec((tm, KV_W), lambda i: (i, 0))
    blk = 2 * _nbytes((tm, D), F32) + _nbytes((2 * KV_W, D), BF16) + 6 * _nbytes((tm, KV_W), F32)
    res, got = _pcall(
        body, name=name, grid=(n_tiles,),
        in_specs=[pl.BlockSpec((tm, D), lambda i: (i, 0)),
                  pl.BlockSpec((POOL_HALO, D), lambda i: (jnp.minimum((i + 1) * hpt, last_halo), 0)),
                  pl.BlockSpec((None, 2 * KV_W, D), lambda i: (0, 0, 0)),
                  tile, tile,
                  pl.BlockSpec((len(POOL_WINDOWS), HEAD, HEAD), lambda i: (0, 0, 0)),
                  pl.BlockSpec((1, POOL_W), lambda i: (0, 0))],
        out_specs=[tile, tile, tile,
                   pl.BlockSpec((len(POOL_WINDOWS), HEAD, HEAD), lambda i: (0, 0, 0)),
                   pl.BlockSpec((1, POOL_W), lambda i: (0, 0))],
        out_shape=[_sds((S, KV_W), F32)] * 3 + [
            _sds((len(POOL_WINDOWS), HEAD, HEAD), F32), _sds((1, POOL_W), F32)],
        params=_cparams(1, blk), args=[dh, dh, w_out3, mix, dpool, pool_w, pool_scale], plan=plan)
    return res if plan is None else (res, got)


def _qkv_bwd(name, dqs, dks, dvs, du, proj, ctab, stab, qg, kg, *, tm, plan=None):
    S = proj.shape[0]
    width = Q_W + KV_W

    def body(dq0, dq1, dq2, dk0, dk1, dk2, dv0, dv1, dv2, du_ref, p_ref, c_ref, s_ref, qg_ref, kg_ref,
             dp_ref, gq_ref, gk_ref):
        @pl.when(pl.program_id(0) == 0)
        def _():
            gq_ref[...] = jnp.zeros_like(gq_ref)
            gk_ref[...] = jnp.zeros_like(gk_ref)

        lane = lax.broadcasted_iota(jnp.int32, (tm, HEAD), 1)
        cv = c_ref[...]
        sv = s_ref[...]

        def back(dy, t, g, scale):
            dy = dy * scale
            dn = dy * cv + _swap_halves(dy * sv, lane, True)
            dt, xh = _norm_bwd(dn, t, g)
            return dt, jnp.sum(dn * xh, axis=0, keepdims=True)

        dqr = (dq0, dq1, dq2)
        gq = jnp.zeros((1, HEAD), F32)
        for j in range(Q_W // HEAD):
            grp, h = divmod(j, N_HEADS)
            dt, gj = back(dqr[grp][:, h * HEAD:(h + 1) * HEAD], p_ref[:, j * HEAD:(j + 1) * HEAD], qg_ref[...], SCALE)
            dp_ref[:, j * HEAD:(j + 1) * HEAD] = _bf(dt)
            gq = gq + gj
        gq_ref[...] += gq
        gk = jnp.zeros((1, HEAD), F32)
        for h in range(N_HEADS):
            sl = slice(h * HEAD, (h + 1) * HEAD)
            dt, gj = back(dk0[:, sl] + dk1[:, sl] + dk2[:, sl], p_ref[:, Q_W + h * HEAD:Q_W + (h + 1) * HEAD],
                          kg_ref[...], 1.0)
            dp_ref[:, Q_W + h * HEAD:Q_W + (h + 1) * HEAD] = _bf(dt)
            gk = gk + gj
        gk_ref[...] += gk
        dp_ref[:, width:width + KV_W] = _bf(dv0[...] + dv1[...] + dv2[...])
        dp_ref[:, width + KV_W:] = _bf(du_ref[...])

    tile = pl.BlockSpec((tm, KV_W), lambda i: (i, 0))
    vec = pl.BlockSpec((1, HEAD), lambda i: (0, 0))
    rot = pl.BlockSpec((tm, HEAD), lambda i: (i, 0))
    blk = 10 * _nbytes((tm, KV_W), F32) + _nbytes((tm, width), F32) + _nbytes((tm, IN_W), BF16)
    res, got = _pcall(
        body, name=name, grid=(S // tm,),
        in_specs=[tile] * 10 + [pl.BlockSpec((tm, width), lambda i: (i, 0)), rot, rot, vec, vec],
        out_specs=[pl.BlockSpec((tm, IN_W), lambda i: (i, 0)), vec, vec],
        out_shape=[_sds((S, IN_W), BF16), _sds((1, HEAD), F32),
                   _sds((1, HEAD), F32)],
        params=_cparams(1, blk), args=[*dqs, *dks, *dvs, du, proj, ctab, stab, qg, kg], plan=plan)
    return res if plan is None else (res, got)


def _cross_heads(q_ref, kv_ref, qg, kg, h):
    sl = slice(h * HEAD, (h + 1) * HEAD)
    qr = q_ref[:, sl]
    kr = kv_ref[:, sl]
    qh = qr * _rstd(qr) * qg * SCALE
    kh = kr * _rstd(kr) * kg
    vh = kv_ref[:, X_W + h * HEAD:X_W + (h + 1) * HEAD]
    return qr, _bf(qh), _bf(kh), _bf(vh)


def _cross_fwd(name, h_in, g_in, w_cq3, kv, qg, kg, w_co3, *, tm):
    S, D = h_in.shape
    M = kv.shape[0]
    P, _, C = w_co3.shape

    def body(h_ref, g_ref, wcq_ref, kv_ref, qg_ref, kg_ref, wco_ref, q_ref, hn_ref, o_ref, ho_ref):
        hv = h_ref[...]
        ab = _bf(hv * _rstd(hv) * g_ref[...])
        hn_ref[...] = ab
        q_ref[...] = _dot(ab, wcq_ref[...])
        for h in range(N_HEADS):
            _, qh, kh, vh = _cross_heads(q_ref, kv_ref, qg_ref[...], kg_ref[...], h)
            s = _dot_nt(qh, kh)
            p = jnp.exp(s - jnp.max(s, axis=-1, keepdims=True))
            l = jnp.sum(p, axis=-1, keepdims=True)
            o_ref[:, h * HEAD:(h + 1) * HEAD] = _bf(_dot(_bf(p), vh) / l)
        ob = o_ref[...]
        for p in range(P):
            ho_ref[:, p * C:(p + 1) * C] = hv[:, p * C:(p + 1) * C] + _dot(ob, wco_ref[p])

    vec = pl.BlockSpec((1, HEAD), lambda i: (0, 0))
    row = pl.BlockSpec((tm, D), lambda i: (i, 0))
    qtile = pl.BlockSpec((tm, X_W), lambda i: (i, 0))
    blk = (3 * _nbytes((tm, D), F32) + 2 * _nbytes((tm, X_W), F32) + _nbytes((M, 2 * X_W), F32)
           + 4 * _nbytes((tm, M), F32) + _nbytes((D, X_W), BF16) + _nbytes(w_co3.shape, BF16))
    return pl.pallas_call(
        body, name=name, grid=(S // tm,),
        in_specs=[row, pl.BlockSpec((1, D), lambda i: (0, 0)), pl.BlockSpec((None, D, X_W), lambda i: (0, 0, 0)),
                  pl.BlockSpec((M, 2 * X_W), lambda i: (0, 0)), vec, vec,
                  pl.BlockSpec(w_co3.shape, lambda i: (0, 0, 0))],
        out_specs=[qtile, row, qtile, row],
        out_shape=[_sds((S, X_W), F32), _sds((S, D), BF16), _sds((S, X_W), BF16), _sds((S, D), F32)],
        compiler_params=_cparams(1, blk))(*_pin([h_in, g_in, w_cq3, kv, qg, kg, w_co3]))


def _cross_bwd(name, dh, w_co3, qraw, kv, qg, kg, w_cq3, h_in, g_in, *, tm, plan=None):
    S, D = dh.shape
    M = kv.shape[0]

    def body(dh_ref, wco_ref, q_ref, kv_ref, qg_ref, kg_ref, wcq_ref, h_ref, g_ref,
             dq_ref, dk_ref, dv_ref, gq_ref, dx_ref, gn_ref):
        @pl.when(pl.program_id(0) == 0)
        def _():
            dk_ref[...] = jnp.zeros_like(dk_ref)
            dv_ref[...] = jnp.zeros_like(dv_ref)
            gq_ref[...] = jnp.zeros_like(gq_ref)
            gn_ref[...] = jnp.zeros_like(gn_ref)

        dhv = dh_ref[...]
        do_all = _nt_pieces(dh_ref, wco_ref)
        gq = jnp.zeros((1, HEAD), F32)
        for h in range(N_HEADS):
            sl = slice(h * HEAD, (h + 1) * HEAD)
            qr, qh, kh, vh = _cross_heads(q_ref, kv_ref, qg_ref[...], kg_ref[...], h)
            doh = _bf(do_all[:, sl])
            s = _dot_nt(qh, kh)
            p = jnp.exp(s - jnp.max(s, axis=-1, keepdims=True))
            p = p / jnp.sum(p, axis=-1, keepdims=True)
            pb = _bf(p)
            dp = _dot_nt(doh, vh)
            ds = _bf(p * (dp - jnp.sum(dp * p, axis=-1, keepdims=True)))
            dv_ref[:, sl] += _dot_tn(pb, doh)
            dk_ref[:, sl] += _dot_tn(ds, qh)
            dn = _dot(ds, kh) * SCALE
            dt, xh = _norm_bwd(dn, qr, qg_ref[...])
            dq_ref[:, sl] = _bf(dt)
            gq = gq + jnp.sum(dn * xh, axis=0, keepdims=True)
        gq_ref[...] += gq
        dhn = _dot_nt(dq_ref[...], wcq_ref[...])
        dx, xh = _norm_bwd(dhn, h_ref[...], g_ref[...])
        dx_ref[...] = dx + dhv
        gn_ref[...] += jnp.sum(dhn * xh, axis=0, keepdims=True)

    vec = pl.BlockSpec((1, HEAD), lambda i: (0, 0))
    acc = pl.BlockSpec((M, X_W), lambda i: (0, 0))
    row = pl.BlockSpec((tm, D), lambda i: (i, 0))
    wide = pl.BlockSpec((1, D), lambda i: (0, 0))
    qtile = pl.BlockSpec((tm, X_W), lambda i: (i, 0))
    blk = (3 * _nbytes((tm, X_W), F32) + 3 * _nbytes((M, 2 * X_W), F32) + 6 * _nbytes((tm, M), F32)
           + 4 * _nbytes((tm, D), F32) + _nbytes(w_co3.shape, BF16) + _nbytes((D, X_W), BF16))
    res, got = _pcall(
        body, name=name, grid=(S // tm,),
        in_specs=[row, pl.BlockSpec(w_co3.shape, lambda i: (0, 0, 0)), qtile,
                  pl.BlockSpec((M, 2 * X_W), lambda i: (0, 0)), vec, vec,
                  pl.BlockSpec((None, D, X_W), lambda i: (0, 0, 0)), row, wide],
        out_specs=[qtile, acc, acc, vec, row, wide],
        out_shape=[_sds((S, X_W), BF16), _sds((M, X_W), F32), _sds((M, X_W), F32), _sds((1, HEAD), F32),
                   _sds((S, D), F32), _sds((1, D), F32)],
        params=_cparams(1, blk), args=[dh, w_co3, qraw, kv, qg, kg, w_cq3, h_in, g_in], plan=plan)
    return res if plan is None else (res, got)


def _cross_kv_bwd(name, dkn, dv, kv, kg):
    M = kv.shape[0]

    def body(dk_ref, dv_ref, kv_ref, kg_ref, o_ref, g_ref):
        gk = jnp.zeros((1, HEAD), F32)
        for h in range(N_HEADS):
            sl = slice(h * HEAD, (h + 1) * HEAD)
            dn = dk_ref[:, sl]
            dt, xh = _norm_bwd(dn, kv_ref[:, sl], kg_ref[...])
            o_ref[:, sl] = _bf(dt)
            gk = gk + jnp.sum(dn * xh, axis=0, keepdims=True)
        o_ref[:, X_W:] = _bf(dv_ref[...])
        g_ref[...] = gk

    full = lambda shape: pl.BlockSpec(shape, lambda i: (0,) * len(shape))
    return pl.pallas_call(
        body, name=name, grid=(1,),
        in_specs=[full((M, X_W)), full((M, X_W)), full((M, 2 * X_W)), full((1, HEAD))],
        out_specs=[full((M, 2 * X_W)), full((1, HEAD))],
        out_shape=[_sds((M, 2 * X_W), BF16), _sds((1, HEAD), F32)],
        compiler_params=_cparams(1, 6 * _nbytes((M, 2 * X_W), F32)))(dkn, dv, kv, kg)


def _rope_tables(positions):
    inv_freq = ROPE_THETA ** (-jnp.arange(0, ROT_DIM, 2, dtype=F32) / ROT_DIM)
    ang = positions.astype(F32)[:, None] * inv_freq
    cos, sin = jnp.cos(ang), jnp.sin(ang)
    S = positions.shape[0]
    ctab = jnp.concatenate([cos, cos, jnp.ones((S, HEAD - ROT_DIM), F32)], axis=-1)
    stab = jnp.concatenate([-sin, sin, jnp.zeros((S, HEAD - ROT_DIM), F32)], axis=-1)
    return ctab, stab


GATHER_BEHIND_IN_PROJ = ("w_out", "w_cq", "w_ckv", "w_co")
FFN_WEIGHTS = ("w_gate_up", "w_down")


def _hosted(fn, *args, plan=None, **kw):
    if plan is None:
        return fn(*args, **kw), []
    return fn(*args, plan=plan, **kw)


def _local_step(x, mem, positions, target, wb, sm, *, tm=512, place=None):
    S, D = x.shape
    M = mem.shape[0]
    dist = place is not None
    wb = dict(wb)
    ctab, stab = _rope_tables(positions)
    pool_w_b = _bf(sm["pool_w"])

    def gather(names):
        return _GatherPlan([wb[k] for k in names]) if dist else None

    if dist:
        wb["w_in"], = _run_plan("gather_w_in", gather(["w_in"]))
    w_in = wb["w_in"]
    behind_in_proj = GATHER_BEHIND_IN_PROJ + ("w_gate_up",)
    (proj, xn1, qn, kn), got = _hosted(_in_proj, "in_proj", x, w_in, sm["mix_norm_g"], ctab, stab,
                                       sm["q_norm_g"], sm["k_norm_g"], tm=tm, plan=gather(behind_in_proj))
    wb.update(zip(behind_in_proj, got))
    (mix, lse), got = _hosted(_attn_fwd, "attn_fwd", qn, kn, proj, plan=gather(["w_down"]))
    wb.update(zip(["w_down"], got))
    w_out = wb["w_out"].reshape(1, 2 * KV_W, D)
    w_cq = wb["w_cq"].reshape(1, D, X_W)
    w_ckv = wb["w_ckv"].reshape(1, D, 2 * X_W)
    w_co = wb["w_co"]
    w_gu = wb["w_gate_up"]
    w_down = wb["w_down"].reshape(1, D_FF, D)
    cin = w_in.shape[2]
    cco = w_co.shape[2]
    cgu = w_gu.shape[2]
    mix, dpool, h1 = _pool_out_proj("pool_out_proj", mix, proj, pool_w_b, sm["pool_scale"], w_out, x, tm=tm)
    kv, mem_n = _mm_nn("ckv_proj", mem, w_ckv, tm=M, norm_g=sm["mem_norm_g"])
    cq_raw, hn2, xo, h2 = _cross_fwd("cross_fwd", h1, sm["cross_norm_g"], w_cq, kv, sm["cq_norm_g"],
                                     sm["ck_norm_g"], w_co, tm=tm)
    act, gu, hn3 = _ffn_up("ffn_up", h2, w_gu, sm["ffn_norm_g"], tm=tm)
    dy, lsum = _ffn_down_loss("ffn_down_loss", act, w_down, h2, target, tm=tm)
    loss = 0.5 * jnp.sum(lsum) / D

    ts = 2 * tm
    nS = S // ts
    g_down = _mm_tn("g_w_down", act, pl.BlockSpec((ts, cgu), lambda r, c, s: (s, r)), (ts, cgu),
                    dy, pl.BlockSpec((ts, D), lambda r, c, s: (s, 0)), (ts, D),
                    (1, D_FF, D), pl.BlockSpec((None, cgu, D), lambda r, c, s: (0, r, 0)), (cgu, D),
                    (D_FF // cgu, 1, nS))
    dgu, dh2, g_ffn_norm = _ffn_bwd("ffn_bwd", dy, w_down, gu, w_gu, h2, sm["ffn_norm_g"], tm=tm // 2)
    g_gu = _mm_tn("g_w_gate_up", hn3, pl.BlockSpec((ts, D), lambda r, c, s: (s, 0)), (ts, D),
                  dgu, pl.BlockSpec((None, ts, cgu), lambda r, c, s: (c // 2, s, c % 2)), (ts, cgu),
                  (4, D, cgu), pl.BlockSpec((None, D, cgu), lambda r, c, s: (c, 0, 0)), (D, cgu),
                  (1, 4, nS))

    g_co = _mm_tn_wide("g_w_co", xo, dh2, N_CHIPS, ts=ts)
    full = {"w_gate_up": g_gu, "w_down": g_down.reshape(N_CHIPS, D_FF // N_CHIPS, D)}
    sums = {}

    def swap(names):
        return _SwapPlan([full[k] for k in names]) if dist else None

    def add_halves(names, from_sibling):
        for k, t in zip(names, from_sibling):
            sums[k] = _add_core_halves(f"add_halves_{k}", full[k], t, place[0])

    def exchange(names):
        return _ExchangePlan([sums[k][1] for k in names]) if dist else None

    def sum_chips(names, from_chips):
        return [_sum_chips(f"sum_chips_{k}", sums[k][0], t, place[1]) for k, t in zip(names, from_chips)]

    (dcq, dkn, dvm, g_cq_norm, dh1, g_cross_norm), got = _hosted(
        _cross_bwd, "cross_bwd", dh2, w_co, cq_raw, kv, sm["cq_norm_g"], sm["ck_norm_g"], w_cq, h1,
        sm["cross_norm_g"], tm=tm, plan=swap(FFN_WEIGHTS))
    add_halves(FFN_WEIGHTS, got)
    dkv, g_ck_norm = _cross_kv_bwd("cross_kv_bwd", dkn, dvm, kv, sm["ck_norm_g"])
    g_cq = _mm_tn("g_w_cq", hn2, pl.BlockSpec((ts, D), lambda r, c, s: (s, 0)), (ts, D),
                  dcq, pl.BlockSpec((ts, X_W), lambda r, c, s: (s, 0)), (ts, X_W),
                  (1, D, X_W), pl.BlockSpec((None, D, X_W), lambda r, c, s: (0, 0, 0)), (D, X_W), (1, 1, nS))
    _, g_mem_norm = _mm_nt_normbwd("ckv_proj_bwd", dkv, w_ckv, mem, sm["mem_norm_g"], None, tm=M)
    g_ckv = _mm_tn("g_w_ckv", mem_n, pl.BlockSpec((M, D), lambda r, c, s: (0, 0)), (M, D),
                   dkv, pl.BlockSpec((M, 2 * X_W), lambda r, c, s: (0, 0)), (M, 2 * X_W),
                   (1, D, 2 * X_W), pl.BlockSpec((None, D, 2 * X_W), lambda r, c, s: (0, 0, 0)), (D, 2 * X_W),
                   (1, 1, 1))

    g_out = _mm_tn("g_w_out", mix, pl.BlockSpec((ts, 2 * KV_W), lambda r, c, s: (s, 0)), (ts, 2 * KV_W),
                   dh1, pl.BlockSpec((ts, D), lambda r, c, s: (s, 0)), (ts, D),
                   (1, 2 * KV_W, D), pl.BlockSpec((None, 2 * KV_W, D), lambda r, c, s: (0, 0, 0)), (2 * KV_W, D),
                   (1, 1, nS))
    full.update({
        "w_out": g_out.reshape(N_CHIPS, 2 * KV_W // N_CHIPS, D),
        "w_cq": g_cq.reshape(N_CHIPS, D // N_CHIPS, X_W),
        "w_ckv": g_ckv.reshape(N_CHIPS, D // N_CHIPS, 2 * X_W),
        "w_co": g_co,
    })
    mixer = GATHER_BEHIND_IN_PROJ
    (dattn, delta, du, g_pool_w, g_pool_scale), got = _hosted(
        _mix_bwd, "mix_bwd", dh1, w_out, mix, dpool, pool_w_b, sm["pool_scale"], tm=tm, plan=swap(mixer))
    add_halves(mixer, got)
    behind_attn = (None, mixer, FFN_WEIGHTS)
    halves = {}
    dqs, dks, dvs = [], [], []
    for grp in range(len(DILATIONS)):
        names = behind_attn[grp]
        (dq, dk, dv), got = _hosted(_attn_bwd, f"attn_bwd{grp}", qn, kn, proj, dattn, lse, delta, grp,
                                    plan=exchange(names) if names else None)
        if dist and names:
            halves.update(zip(names, sum_chips(names, got)))
        dqs.append(dq)
        dks.append(dk)
        dvs.append(dv)
    joined = mixer + FFN_WEIGHTS
    (dproj, g_q_norm, g_k_norm), got = _hosted(
        _qkv_bwd, "qkv_bwd", dqs, dks, dvs, du, proj, ctab, stab, sm["q_norm_g"], sm["k_norm_g"], tm=tm,
        plan=_JoinPlan([halves[k] for k in joined]) if dist else None)
    shards = dict(zip(joined, got))
    dx, g_mix_norm = _mm_nt_normbwd("in_proj_bwd", dproj, w_in, x, sm["mix_norm_g"], dh1, tm=tm)
    small = {
        "mix_norm_g": g_mix_norm, "q_norm_g": g_q_norm, "k_norm_g": g_k_norm, "pool_w": g_pool_w,
        "pool_scale": g_pool_scale, "cross_norm_g": g_cross_norm, "mem_norm_g": g_mem_norm,
        "cq_norm_g": g_cq_norm, "ck_norm_g": g_ck_norm, "ffn_norm_g": g_ffn_norm,
    }
    full["w_in"], got = _hosted(_mm_tn_wide, "g_w_in", xn1, dproj, N_CHIPS, ts=tm,
                                plan=_AllPushPlan(_pack_small(small)) if dist else None)
    if dist:
        small = _unpack_small(_sum_slots("sum_small", got[0]), sm)
        add_halves(["w_in"], _run_plan("swap_w_in", swap(["w_in"])))
        half, = sum_chips(["w_in"], _run_plan("exchange_w_in", exchange(["w_in"])))
        shards["w_in"], = _run_plan("join_w_in", _JoinPlan([half]))
    big = shards if dist else full
    return loss, dx, big, small


BIG = ("w_in", "w_out", "w_cq", "w_ckv", "w_co", "w_gate_up", "w_down")
SMALL = ("mix_norm_g", "q_norm_g", "k_norm_g", "pool_w", "pool_scale", "cross_norm_g", "mem_norm_g",
         "cq_norm_g", "ck_norm_g", "ffn_norm_g")
WEIGHTS = ("mix_norm_g", "w_in", "q_norm_g", "k_norm_g", "pool_w", "pool_scale", "w_out", "cross_norm_g",
           "mem_norm_g", "w_cq", "w_ckv", "cq_norm_g", "ck_norm_g", "w_co", "ffn_norm_g", "w_gate_up", "w_down")


def _cast_piece(name, w, k_arr):
    R, C = w.shape
    hr = R // 2

    def body(k_ref, w_ref, o_ref):
        o_ref[...] = _bf(w_ref[...])

    return pl.pallas_call(
        body, name=name,
        grid_spec=pltpu.PrefetchScalarGridSpec(
            num_scalar_prefetch=1, grid=(2,),
            in_specs=[pl.BlockSpec((hr, C), lambda i, k: (i, 0))],
            out_specs=pl.BlockSpec((None, hr, C), lambda i, k: (k[0], i, 0))),
        out_shape=_sds((N_CHIPS, R, C), BF16),
        compiler_params=_cparams(1, 2 * _nbytes((hr, C), F32)))(k_arr, *_pin([w]))


def _add_core_halves(name, g, t, c_arr):
    P, R, C = g.shape
    hr = R // 2

    def body(c_ref, g_ref, t_ref, o_ref, ob_ref):
        tot = g_ref[...] + t_ref[...]
        o_ref[...] = tot
        ob_ref[...] = _bf(tot)

    piece = pl.BlockSpec((None, hr, C), lambda p, c: (p, 0, 0))
    return pl.pallas_call(
        body, name=name,
        grid_spec=pltpu.PrefetchScalarGridSpec(
            num_scalar_prefetch=1, grid=(P,),
            in_specs=[pl.BlockSpec((None, hr, C), lambda p, c: (p, c[0], 0)), piece],
            out_specs=[piece, piece]),
        out_shape=[_sds((P, hr, C), F32), _sds((P, hr, C), BF16)],
        compiler_params=_cparams(1, 4 * _nbytes((hr, C), F32)))(c_arr, *_pin([g, t]))


def _sum_chips(name, own, got, kc_arr):
    P, hr, C = own.shape

    def body(kc_ref, o_ref, g_ref, r_ref):
        r_ref[...] = ((o_ref[...] + g_ref[0].astype(F32)) + g_ref[1].astype(F32)) + g_ref[2].astype(F32)

    return pl.pallas_call(
        body, name=name,
        grid_spec=pltpu.PrefetchScalarGridSpec(
            num_scalar_prefetch=1, grid=(1,),
            in_specs=[pl.BlockSpec((None, hr, C), lambda i, kc: (kc[0], 0, 0)),
                      pl.BlockSpec((N_CHIPS - 1, hr, C), lambda i, kc: (0, 0, 0))],
            out_specs=pl.BlockSpec((hr, C), lambda i, kc: (kc[1], 0))),
        out_shape=_sds((2 * hr, C), F32),
        compiler_params=_cparams(1, 5 * _nbytes((hr, C), F32)))(kc_arr, *_pin([own, got]))


N_DEV = 8


class _AllPushPlan(_Plan):
    def __init__(self, v):
        self.ins = [v]
        self.out_shapes = [jax.ShapeDtypeStruct((N_DEV,) + v.shape, v.dtype)]
        self.sem_shapes = [pltpu.SemaphoreType.DMA((N_DEV - 1,)), pltpu.SemaphoreType.DMA((N_DEV - 1,)),
                           pltpu.SemaphoreType.DMA]

    def copies(self, ins, outs, sems):
        send, recv, own = sems
        x, y, c, _ = _place()
        slot = outs[0].at[4 * x + 2 * y + c]
        cps = [pltpu.make_async_copy(ins[0], slot, own)]
        flips = [(dx, dy, dc) for dx in (0, 1) for dy in (0, 1) for dc in (0, 1)][1:]
        for q, (dx, dy, dc) in enumerate(flips):
            to = (x + dx - 2 * x * dx, y + dy - 2 * y * dy, c + dc - 2 * c * dc)
            cps.append(pltpu.make_async_remote_copy(src_ref=ins[0], dst_ref=slot, send_sem=send.at[q],
                                                    recv_sem=recv.at[q], device_id=to, device_id_type=MESH))
        return cps


def _sum_slots(name, slots):
    n, R, C = slots.shape

    def body(s_ref, o_ref):
        acc = s_ref[0]
        for d in range(1, n):
            acc = acc + s_ref[d]
        o_ref[...] = acc

    return pl.pallas_call(
        body, name=name, grid=(1,),
        in_specs=[pl.BlockSpec((n, R, C), lambda i: (0, 0, 0))], out_specs=pl.BlockSpec((R, C), lambda i: (0, 0)),
        out_shape=_sds((R, C), F32),
        compiler_params=_cparams(1, _nbytes((n + 1, R, C), F32)))(*_pin([slots]))


def _adamw(name, w, g, m, v, *, tr):
    R, C = w.shape

    def body(w_ref, g_ref, m_ref, v_ref, d_ref, nm_ref, nv_ref):
        gv = g_ref[...]
        nm = ADAM_B1 * m_ref[...] + (1.0 - ADAM_B1) * gv
        nv = ADAM_B2 * v_ref[...] + (1.0 - ADAM_B2) * (gv * gv)
        m_hat = nm / (1.0 - ADAM_B1 ** ADAM_STEP)
        v_hat = nv / (1.0 - ADAM_B2 ** ADAM_STEP)
        d_ref[...] = -ADAM_LR * (m_hat / (jnp.sqrt(v_hat) + ADAM_EPS) + ADAM_WD * w_ref[...])
        nm_ref[...] = nm
        nv_ref[...] = nv

    tile = pl.BlockSpec((tr, C), lambda i: (i, 0))
    return pl.pallas_call(
        body, name=name, grid=(R // tr,), in_specs=[tile] * 4, out_specs=[tile] * 3,
        out_shape=[_sds((R, C), F32)] * 3,
        compiler_params=_cparams(1, 7 * _nbytes((tr, C), F32)))(*_pin([w, g, m, v]))


def _pack_small(d):
    parts = []
    for name in SMALL:
        a = d[name].reshape(-1, HEAD)
        pad = (-a.shape[0]) % 8
        parts.append(jnp.pad(a, ((0, pad), (0, 0))))
    return jnp.concatenate(parts, axis=0)


def _unpack_small(packed, like):
    out = {}
    row = 0
    for name in SMALL:
        shape = like[name].shape
        rows = like[name].size // HEAD
        out[name] = packed[row:row + rows].reshape(shape)
        row += rows + (-rows) % 8
    return out


def kernel(x, mem, positions, mix_norm_g, w_in, q_norm_g, k_norm_g, pool_w, pool_scale, w_out, cross_norm_g, mem_norm_g, w_cq, w_ckv, cq_norm_g, ck_norm_g, w_co, ffn_norm_g, w_gate_up, w_down, loss_target, m_mix_norm_g, m_w_in, m_q_norm_g, m_k_norm_g, m_pool_w, m_pool_scale, m_w_out, m_cross_norm_g, m_mem_norm_g, m_w_cq, m_w_ckv, m_cq_norm_g, m_ck_norm_g, m_w_co, m_ffn_norm_g, m_w_gate_up, m_w_down, v_mix_norm_g, v_w_in, v_q_norm_g, v_k_norm_g, v_pool_w, v_pool_scale, v_w_out, v_cross_norm_g, v_mem_norm_g, v_w_cq, v_w_ckv, v_cq_norm_g, v_ck_norm_g, v_w_co, v_ffn_norm_g, v_w_gate_up, v_w_down):
    w = dict(mix_norm_g=mix_norm_g, w_in=w_in, q_norm_g=q_norm_g, k_norm_g=k_norm_g, pool_w=pool_w,
             pool_scale=pool_scale, w_out=w_out, cross_norm_g=cross_norm_g, mem_norm_g=mem_norm_g, w_cq=w_cq,
             w_ckv=w_ckv, cq_norm_g=cq_norm_g, ck_norm_g=ck_norm_g, w_co=w_co, ffn_norm_g=ffn_norm_g,
             w_gate_up=w_gate_up, w_down=w_down)
    m = dict(mix_norm_g=m_mix_norm_g, w_in=m_w_in, q_norm_g=m_q_norm_g, k_norm_g=m_k_norm_g, pool_w=m_pool_w,
             pool_scale=m_pool_scale, w_out=m_w_out, cross_norm_g=m_cross_norm_g, mem_norm_g=m_mem_norm_g,
             w_cq=m_w_cq, w_ckv=m_w_ckv, cq_norm_g=m_cq_norm_g, ck_norm_g=m_ck_norm_g, w_co=m_w_co,
             ffn_norm_g=m_ffn_norm_g, w_gate_up=m_w_gate_up, w_down=m_w_down)
    v = dict(mix_norm_g=v_mix_norm_g, w_in=v_w_in, q_norm_g=v_q_norm_g, k_norm_g=v_k_norm_g, pool_w=v_pool_w,
             pool_scale=v_pool_scale, w_out=v_w_out, cross_norm_g=v_cross_norm_g, mem_norm_g=v_mem_norm_g,
             w_cq=v_w_cq, w_ckv=v_w_ckv, cq_norm_g=v_cq_norm_g, ck_norm_g=v_ck_norm_g, w_co=v_w_co,
             ffn_norm_g=v_ffn_norm_g, w_gate_up=v_w_gate_up, w_down=v_w_down)

    c_arr = lax.axis_index("c").astype(jnp.int32).reshape(1)
    k_arr = (2 * lax.axis_index("x") + lax.axis_index("y")).astype(jnp.int32).reshape(1)
    kc_arr = jnp.concatenate([k_arr, c_arr])
    wb = {k: _cast_piece(f"cast_{k}", w[k][0], k_arr) for k in BIG}
    sm = {k: (w[k][0] if k == "pool_w" else w[k]) for k in SMALL}
    loss_part, dx, gshard, gsm = _local_step(x[0], mem[0], positions[0], loss_target[0], wb, sm,
                                             place=(c_arr, kc_arr))
    loss = lax.psum(loss_part, ("x", "y", "c"))

    grads, deltas, new_m, new_v = {}, {}, {}, {}
    for k in BIG:
        shard = w[k][0]
        tr = shard.shape[0] // 4
        d, nm, nv = _adamw(f"adamw_{k}", shard, gshard[k], m[k][0], v[k][0], tr=tr)
        grads[k], deltas[k], new_m[k], new_v[k] = gshard[k][None], d[None], nm[None], nv[None]
    smw = {k: (w[k][0] if k == "pool_w" else w[k]) for k in SMALL}
    smm = {k: (m[k][0] if k == "pool_w" else m[k]) for k in SMALL}
    smv = {k: (v[k][0] if k == "pool_w" else v[k]) for k in SMALL}
    pw, pg, pm, pv = _pack_small(smw), _pack_small(gsm), _pack_small(smm), _pack_small(smv)
    d, nm, nv = _adamw("adamw_small", pw, pg, pm, pv, tr=pw.shape[0])
    for dst, packed in ((deltas, d), (new_m, nm), (new_v, nv)):
        un = _unpack_small(packed, sm)
        for k in SMALL:
            dst[k] = un[k].reshape(w[k].shape)
    for k in SMALL:
        grads[k] = gsm[k].reshape(w[k].shape)

    return (loss, dx[None], *[grads[k] for k in WEIGHTS], *[deltas[k] for k in WEIGHTS],
            *[new_m[k] for k in WEIGHTS], *[new_v[k] for k in WEIGHTS])
```

```python
import functools

import jax
import jax.numpy as jnp
from jax import lax
from jax.experimental import pallas as pl
from jax.experimental.pallas import tpu as pltpu

F32 = jnp.float32
BF16 = jnp.bfloat16
MESH = pl.DeviceIdType.MESH
ANY = pl.BlockSpec(memory_space=pl.ANY)

D_MODEL = 1024
HEAD = 128
N_HEADS = 4
DILATIONS = (1, 4, 16)
BLOCK = 128
Q_W = 1536
KV_W = 512
POOL_W = 512
POOL_WINDOWS = (2, 4, 8, 16)
POOL_HALO = 16
IN_W = 3072
ROT_DIM = 32
ROT_HALF = 16
ROPE_THETA = 500000.0
X_W = 512
D_FF = 2816
EPS = 1e-6
NEG_INF = -1e30
SCALE = HEAD ** -0.5
N_CHIPS = 4

ADAM_LR = 0.001
ADAM_B1 = 0.9
ADAM_B2 = 0.999
ADAM_EPS = 1e-08
ADAM_WD = 0.01
ADAM_STEP = 10

VMEM_BYTES_V7X = 64 * 2 ** 20
VMEM_LIMIT_MAX = 56 * 2 ** 20
VMEM_LIMIT_MIN = 24 * 2 ** 20


def _nbytes(shape, dtype):
    n = 1
    for s in shape:
        n *= s
    return n * jnp.dtype(dtype).itemsize


def _cparams(n_axes, block_bytes, scratch_bytes=0):
    est = 2 * (2 * block_bytes + scratch_bytes)
    lim = int(min(VMEM_LIMIT_MAX, max(VMEM_LIMIT_MIN, est)))
    return pltpu.CompilerParams(dimension_semantics=("arbitrary",) * n_axes, vmem_limit_bytes=lim)


def _bf(v):
    return v.astype(BF16)


def _dot(a, b):
    return jnp.dot(a, b, preferred_element_type=F32)


def _dot_nt(a, b):
    return lax.dot_general(a, b, (((1,), (1,)), ((), ())), preferred_element_type=F32)


def _dot_tn(a, b):
    return lax.dot_general(a, b, (((0,), (0,)), ((), ())), preferred_element_type=F32)


def _rstd(v):
    return lax.rsqrt(jnp.mean(v * v, axis=-1, keepdims=True) + EPS)


def _norm_bwd(dy, xv, g):
    r = _rstd(xv)
    xh = xv * r
    dxh = dy * g
    dx = r * (dxh - xh * jnp.mean(dxh * xh, axis=-1, keepdims=True))
    return dx, xh


def _place():
    x, y, c = lax.axis_index("x"), lax.axis_index("y"), lax.axis_index("c")
    other_chips = [(1 - x, y), (x, 1 - y), (1 - x, 1 - y)]
    return x, y, c, other_chips


class _Plan:
    ins = ()
    out_shapes = ()
    aliases = {}
    sem_shapes = ()

    def copies(self, ins, outs, sems):
        raise NotImplementedError

    def begin(self, ins, outs, sems):
        for cp in self.copies(ins, outs, sems):
            cp.start()

    def finish(self, ins, outs, sems):
        for cp in self.copies(ins, outs, sems):
            cp.wait()


class _GatherPlan(_Plan):
    def __init__(self, bufs):
        n = len(bufs)
        self.ins = list(bufs)
        self.out_shapes = [_sds(b.shape, b.dtype) for b in bufs]
        self.aliases = {i: i for i in range(n)}
        self.sem_shapes = [pltpu.SemaphoreType.DMA((n, 6)), pltpu.SemaphoreType.DMA((n, 6))]

    def _parts(self, outs, sems):
        send, recv = sems
        x, y, c, chips = _place()

        def half(i, piece, which):
            hr = outs[i].shape[1] // 2
            return outs[i].at[piece, pl.ds(which * hr, hr), :]

        def copy(i, k, ref, to):
            return pltpu.make_async_remote_copy(src_ref=ref, dst_ref=ref, send_sem=send.at[i, k], recv_sem=recv.at[i, k],
                                                device_id=to, device_id_type=MESH)

        return x, y, c, chips, half, copy

    def begin(self, ins, outs, sems):
        x, y, c, chips, half, copy = self._parts(outs, sems)
        for i in range(len(outs)):
            for j, (cx, cy) in enumerate(chips):
                copy(i, j, half(i, 2 * x + y, c), (cx, cy, c)).start()

    def finish(self, ins, outs, sems):
        x, y, c, chips, half, copy = self._parts(outs, sems)
        sib = (x, y, 1 - c)
        n = len(outs)
        for i in range(n):
            for j, (cx, cy) in enumerate(chips):
                piece = half(i, 2 * cx + cy, c)
                copy(i, j, piece, (cx, cy, c)).wait_recv()
                copy(i, 3 + j, piece, sib).start()
        for i in range(n):
            for j, (cx, cy) in enumerate(chips):
                copy(i, 3 + j, half(i, 2 * cx + cy, 1 - c), sib).wait_recv()
        for i in range(n):
            for j, (cx, cy) in enumerate(chips):
                copy(i, j, half(i, 2 * x + y, c), (cx, cy, c)).wait_send()
                copy(i, 3 + j, half(i, 2 * cx + cy, c), sib).wait_send()


class _SwapPlan(_Plan):
    def __init__(self, grads):
        n = len(grads)
        self.ins = list(grads)
        self.out_shapes = [_sds((g.shape[0], g.shape[1] // 2, g.shape[2]), g.dtype) for g in grads]
        self.sem_shapes = [pltpu.SemaphoreType.DMA((n,)), pltpu.SemaphoreType.DMA((n,))]

    def copies(self, ins, outs, sems):
        send, recv = sems
        x, y, c, _ = _place()
        cps = []
        for i in range(len(ins)):
            hr = ins[i].shape[1] // 2
            cps.append(pltpu.make_async_remote_copy(
                src_ref=ins[i].at[:, pl.ds((1 - c) * hr, hr), :], dst_ref=outs[i], send_sem=send.at[i],
                recv_sem=recv.at[i], device_id=(x, y, 1 - c), device_id_type=MESH))
        return cps


class _ExchangePlan(_Plan):
    def __init__(self, sums):
        n = len(sums)
        self.ins = list(sums)
        self.out_shapes = [_sds((N_CHIPS - 1,) + s.shape[1:], s.dtype) for s in sums]
        self.sem_shapes = [pltpu.SemaphoreType.DMA((n, 3)), pltpu.SemaphoreType.DMA((n, 3))]

    def copies(self, ins, outs, sems):
        send, recv = sems
        x, y, c, chips = _place()
        cps = []
        for i in range(len(ins)):
            for j, (cx, cy) in enumerate(chips):
                cps.append(pltpu.make_async_remote_copy(
                    src_ref=ins[i].at[2 * cx + cy], dst_ref=outs[i].at[j], send_sem=send.at[i, j],
                    recv_sem=recv.at[i, j], device_id=(cx, cy, c), device_id_type=MESH))
        return cps


class _JoinPlan(_Plan):
    def __init__(self, shards):
        n = len(shards)
        self.ins = list(shards)
        self.out_shapes = [_sds(s.shape, s.dtype) for s in shards]
        self.aliases = {i: i for i in range(n)}
        self.sem_shapes = [pltpu.SemaphoreType.DMA((n,)), pltpu.SemaphoreType.DMA((n,))]

    def copies(self, ins, outs, sems):
        send, recv = sems
        x, y, c, _ = _place()
        cps = []
        for i in range(len(outs)):
            hr = outs[i].shape[0] // 2
            mine = outs[i].at[pl.ds(c * hr, hr), :]
            cps.append(pltpu.make_async_remote_copy(src_ref=mine, dst_ref=mine, send_sem=send.at[i], recv_sem=recv.at[i],
                                                    device_id=(x, y, 1 - c), device_id_type=MESH))
        return cps


def _run_plan(name, plan):
    n_in, n_out = len(plan.ins), len(plan.out_shapes)

    def body(*refs):
        ins, outs, sems = refs[:n_in], refs[n_in:n_in + n_out], refs[n_in + n_out:]
        plan.begin(ins, outs, sems)
        plan.finish(ins, outs, sems)

    return pl.pallas_call(
        body, name=name, in_specs=[ANY] * n_in, out_specs=[ANY] * n_out, out_shape=list(plan.out_shapes),
        input_output_aliases=dict(plan.aliases), scratch_shapes=list(plan.sem_shapes))(*plan.ins)


HBM_PIN_BYTES = 1 << 20


def _sds(shape, dtype):
    if _nbytes(shape, dtype) >= HBM_PIN_BYTES:
        return pltpu.HBM(shape, dtype)
    return jax.ShapeDtypeStruct(shape, dtype)


def _pin(args):
    return [pltpu.with_memory_space_constraint(a, pltpu.HBM) if _nbytes(a.shape, a.dtype) >= HBM_PIN_BYTES else a
            for a in args]


def _pcall(body, *, name, grid, in_specs, out_specs, out_shape, params, args, scratch_shapes=(), plan=None):
    in_specs, out_specs, out_shape, scratch = list(in_specs), list(out_specs), list(out_shape), list(scratch_shapes)
    args = _pin(args)
    if plan is None:
        res = pl.pallas_call(body, name=name, grid=grid, in_specs=in_specs, out_specs=out_specs, out_shape=out_shape,
                             scratch_shapes=scratch, compiler_params=params)(*args)
        return list(res), []
    ni, no, ns = len(in_specs), len(out_specs), len(scratch)
    pi, po = len(plan.ins), len(plan.out_shapes)

    def wrapped(*refs):
        ins, pins = refs[:ni], refs[ni:ni + pi]
        outs, pouts = refs[ni + pi:ni + pi + no], refs[ni + pi + no:ni + pi + no + po]
        scr, psems = refs[ni + pi + no + po:ni + pi + no + po + ns], refs[ni + pi + no + po + ns:]
        ids = [pl.program_id(a) for a in range(len(grid))]
        first = functools.reduce(jnp.logical_and, [i == 0 for i in ids])
        last = functools.reduce(jnp.logical_and, [i == g - 1 for i, g in zip(ids, grid)])

        @pl.when(first)
        def _():
            plan.begin(pins, pouts, psems)

        body(*ins, *outs, *scr)

        @pl.when(last)
        def _():
            plan.finish(pins, pouts, psems)

    res = pl.pallas_call(
        wrapped, name=name, grid=grid, in_specs=in_specs + [ANY] * pi, out_specs=out_specs + [ANY] * po,
        out_shape=out_shape + list(plan.out_shapes), scratch_shapes=scratch + list(plan.sem_shapes),
        input_output_aliases={ni + a: no + b for a, b in plan.aliases.items()},
        compiler_params=params)(*args, *plan.ins)
    return list(res[:no]), list(res[no:])


def _mm_nn(name, a, w3, *, tm, norm_g=None, residual=None, out_dtype=F32, plan=None):
    M, K = a.shape
    P, Kw, C = w3.shape
    assert Kw == K and M % tm == 0
    N = P * C
    has_norm = norm_g is not None
    has_res = residual is not None

    def body(*refs):
        refs = list(refs)
        a_ref = refs.pop(0)
        w_ref = refs.pop(0)
        g_ref = refs.pop(0) if has_norm else None
        r_ref = refs.pop(0) if has_res else None
        o_ref = refs.pop(0)
        xn_ref = refs.pop(0) if has_norm else None
        if has_norm:
            av = a_ref[...].astype(F32)
            ab = _bf(av * _rstd(av) * g_ref[...])
            xn_ref[...] = ab
        else:
            ab = _bf(a_ref[...])
        for p in range(P):
            acc = _dot(ab, w_ref[p])
            if has_res:
                acc = acc + r_ref[:, p * C:(p + 1) * C]
            o_ref[:, p * C:(p + 1) * C] = acc.astype(o_ref.dtype)

    in_specs = [pl.BlockSpec((tm, K), lambda i: (i, 0)), pl.BlockSpec((P, K, C), lambda i: (0, 0, 0))]
    args = [a, w3]
    if has_norm:
        in_specs.append(pl.BlockSpec((1, K), lambda i: (0, 0)))
        args.append(norm_g)
    if has_res:
        in_specs.append(pl.BlockSpec((tm, N), lambda i: (i, 0)))
        args.append(residual)
    out_shape = [_sds((M, N), out_dtype)]
    out_specs = [pl.BlockSpec((tm, N), lambda i: (i, 0))]
    if has_norm:
        out_shape.append(_sds((M, K), BF16))
        out_specs.append(pl.BlockSpec((tm, K), lambda i: (i, 0)))
    blk = (_nbytes((tm, K), a.dtype) + _nbytes((P, K, C), BF16) + 2 * _nbytes((tm, N), F32)
           + _nbytes((tm, K), BF16))
    res, got = _pcall(body, name=name, grid=(M // tm,), in_specs=in_specs, out_specs=out_specs, out_shape=out_shape,
                      params=_cparams(1, blk), args=args, plan=plan)
    res = res if has_norm else res[0]
    return res if plan is None else (res, got)


MXU_COLS_V7X = 256


def _col_chunks(n):
    return [(c, min(c + MXU_COLS_V7X, n)) for c in range(0, n, MXU_COLS_V7X)]


def _ffn_up(name, h, wgu3, g, *, tm):
    M, K = h.shape
    P, _, C = wgu3.shape
    half = P // 2

    def body(h_ref, w_ref, g_ref, act_ref, fac_ref, xn_ref):
        hv = h_ref[...]
        ab = _bf(hv * _rstd(hv) * g_ref[...])
        xn_ref[...] = ab
        for j in range(half):
            for c0, c1 in _col_chunks(C):
                gate = _dot(ab, w_ref[j, :, c0:c1])
                up = _dot(ab, w_ref[half + j, :, c0:c1])
                sig = 1.0 / (1.0 + jnp.exp(-gate))
                silu = gate * sig
                cols = slice(j * C + c0, j * C + c1)
                act_ref[:, cols] = _bf(silu * up)
                fac_ref[0, :, cols] = _bf(up * (sig + silu * (1.0 - sig)))
                fac_ref[1, :, cols] = _bf(silu)

    F = half * C
    blk = (_nbytes((tm, K), F32) + _nbytes((P, K, C), BF16) // 2 + 3 * _nbytes((tm, F), BF16)
           + _nbytes((tm, K), BF16))
    return pl.pallas_call(
        body, name=name, grid=(M // tm,),
        in_specs=[pl.BlockSpec((tm, K), lambda i: (i, 0)),
                  pl.BlockSpec((P, K, C), lambda i: (0, 0, 0), pipeline_mode=pl.Buffered(1)),
                  pl.BlockSpec((1, K), lambda i: (0, 0))],
        out_specs=[pl.BlockSpec((tm, F), lambda i: (i, 0)),
                   pl.BlockSpec((2, tm, F), lambda i: (0, i, 0)),
                   pl.BlockSpec((tm, K), lambda i: (i, 0))],
        out_shape=[_sds((M, F), BF16), _sds((2, M, F), BF16), _sds((M, K), BF16)],
        compiler_params=_cparams(1, blk))(*_pin([h, wgu3, g]))


def _ffn_down_loss(name, act, wd3, h, target, *, tm):
    M, K = act.shape
    _, _, N = wd3.shape

    def body(a_ref, w_ref, h_ref, t_ref, dy_ref, ls_ref):
        err = _dot(a_ref[...], w_ref[...]) + h_ref[...] - t_ref[...]
        dy_ref[...] = err * (1.0 / N)

        @pl.when(pl.program_id(0) == 0)
        def _():
            ls_ref[...] = jnp.zeros_like(ls_ref)
        ls_ref[...] += jnp.sum(err * err, axis=0, keepdims=True)

    blk = _nbytes((tm, K), BF16) + _nbytes((K, N), BF16) + 4 * _nbytes((tm, N), F32)
    return pl.pallas_call(
        body, name=name, grid=(M // tm,),
        in_specs=[pl.BlockSpec((tm, K), lambda i: (i, 0)),
                  pl.BlockSpec((None, K, N), lambda i: (0, 0, 0)),
                  pl.BlockSpec((tm, N), lambda i: (i, 0)),
                  pl.BlockSpec((tm, N), lambda i: (i, 0))],
        out_specs=[pl.BlockSpec((tm, N), lambda i: (i, 0)),
                   pl.BlockSpec((1, N), lambda i: (0, 0))],
        out_shape=[_sds((M, N), F32), _sds((1, N), F32)],
        compiler_params=_cparams(1, blk))(*_pin([act, wd3, h, target]))


def _nt_pieces(a_ref, w_ref):
    P, _, C = w_ref.shape
    acc = _dot_nt(_bf(a_ref[:, 0:C]), w_ref[0])
    for p in range(1, P):
        acc = acc + _dot_nt(_bf(a_ref[:, p * C:(p + 1) * C]), w_ref[p])
    return acc


def _mm_nt_normbwd(name, a, w3, h, g, dres, *, tm, plan=None):
    M, D = h.shape
    P, Ko, C = w3.shape
    N = a.shape[1]
    assert N == P * C and Ko == D and M % tm == 0
    has_res = dres is not None

    def body(*refs):
        a_ref, w_ref, h_ref, g_ref = refs[:4]
        r_ref = refs[4] if has_res else None
        dx_ref, dg_ref = refs[-2:]
        dhn = _nt_pieces(a_ref, w_ref)
        dx, xh = _norm_bwd(dhn, h_ref[...], g_ref[...])
        if has_res:
            dx = dx + r_ref[...]
        dx_ref[...] = dx

        @pl.when(pl.program_id(0) == 0)
        def _():
            dg_ref[...] = jnp.zeros_like(dg_ref)
        dg_ref[...] += jnp.sum(dhn * xh, axis=0, keepdims=True)

    row = pl.BlockSpec((tm, D), lambda i: (i, 0))
    vec = pl.BlockSpec((1, D), lambda i: (0, 0))
    in_specs = [pl.BlockSpec((tm, N), lambda i: (i, 0)), pl.BlockSpec((P, Ko, C), lambda i: (0, 0, 0)), row, vec]
    args = [a, w3, h, g]
    if has_res:
        in_specs.append(row)
        args.append(dres)
    blk = _nbytes((tm, N), a.dtype) + _nbytes((P, Ko, C), BF16) + 5 * _nbytes((tm, D), F32)
    res, got = _pcall(body, name=name, grid=(M // tm,), in_specs=in_specs, out_specs=[row, vec],
                      out_shape=[_sds((M, D), F32), _sds((1, D), F32)],
                      params=_cparams(1, blk), args=args, plan=plan)
    return res if plan is None else (res, got)


def _ffn_bwd(name, dy, wd3, fac, wgu3, h, g, *, tm):
    M, D = h.shape
    P, _, C = wgu3.shape
    F = fac.shape[2]
    per = F // C

    def body(dy_ref, wd_ref, f_ref, w_ref, h_ref, g_ref, dgu_ref, dx_ref, dg_ref):
        dyv = dy_ref[...]
        ab = _bf(dyv)
        dhn = None
        for pc in range(per):
            cols = slice(pc * C, (pc + 1) * C)
            da = _dot_nt(ab, wd_ref[cols, :])
            dgate = _bf(da * f_ref[0, :, cols].astype(F32))
            dup = _bf(da * f_ref[1, :, cols].astype(F32))
            dgu_ref[0, :, cols] = dgate
            dgu_ref[1, :, cols] = dup
            part = _dot_nt(dgate, w_ref[pc]) + _dot_nt(dup, w_ref[per + pc])
            dhn = part if dhn is None else dhn + part
        dx, xh = _norm_bwd(dhn, h_ref[...], g_ref[...])
        dx_ref[...] = dx + dyv

        @pl.when(pl.program_id(0) == 0)
        def _():
            dg_ref[...] = jnp.zeros_like(dg_ref)
        dg_ref[...] += jnp.sum(dhn * xh, axis=0, keepdims=True)

    row = pl.BlockSpec((tm, D), lambda i: (i, 0))
    vec = pl.BlockSpec((1, D), lambda i: (0, 0))
    planes = pl.BlockSpec((2, tm, F), lambda i: (0, i, 0))
    blk = (2 * _nbytes((2, tm, F), BF16) + (_nbytes((F, D), BF16) + _nbytes((P, D, C), BF16)) // 2
           + 4 * _nbytes((tm, D), F32))
    return pl.pallas_call(
        body, name=name, grid=(M // tm,),
        in_specs=[row, pl.BlockSpec((None, F, D), lambda i: (0, 0, 0), pipeline_mode=pl.Buffered(1)), planes,
                  pl.BlockSpec((P, D, C), lambda i: (0, 0, 0), pipeline_mode=pl.Buffered(1)), row, vec],
        out_specs=[planes, row, vec],
        out_shape=[_sds((2, M, F), BF16), _sds((M, D), F32), _sds((1, D), F32)],
        compiler_params=_cparams(1, blk))(*_pin([dy, wd3, fac, wgu3, h, g]))


def _mm_tn_wide(name, a, b, pieces, *, ts, plan=None):
    S, K = a.shape
    N = b.shape[1]
    C = N // pieces

    def body(a_ref, b_ref, o_ref):
        @pl.when(pl.program_id(0) == 0)
        def _():
            o_ref[...] = jnp.zeros_like(o_ref)
        acc = _dot_tn(_bf(a_ref[...]), _bf(b_ref[...]))
        for p in range(pieces):
            o_ref[p] += acc[:, p * C:(p + 1) * C]

    blk = _nbytes((ts, K), a.dtype) + _nbytes((ts, N), b.dtype) + 2 * _nbytes((K, N), F32)
    res, got = _pcall(body, name=name, grid=(S // ts,),
                      in_specs=[pl.BlockSpec((ts, K), lambda s: (s, 0)), pl.BlockSpec((ts, N), lambda s: (s, 0))],
                      out_specs=[pl.BlockSpec((pieces, K, C), lambda s: (0, 0, 0))],
                      out_shape=[_sds((pieces, K, C), F32)], params=_cparams(1, blk), args=[a, b], plan=plan)
    return res[0] if plan is None else (res[0], got)


def _mm_tn(name, a, a_spec, a_blk, b, b_spec, b_blk, out3, o_spec, o_blk, grid, plan=None):
    def body(a_ref, b_ref, o_ref):
        @pl.when(pl.program_id(2) == 0)
        def _():
            o_ref[...] = jnp.zeros_like(o_ref)
        o_ref[...] += _dot_tn(_bf(a_ref[...]), _bf(b_ref[...]))

    blk = _nbytes(a_blk, a.dtype) + _nbytes(b_blk, b.dtype) + 2 * _nbytes(o_blk, F32)
    res, got = _pcall(body, name=name, grid=grid, in_specs=[a_spec, b_spec], out_specs=[o_spec],
                      out_shape=[_sds(out3, F32)], params=_cparams(3, blk), args=[a, b], plan=plan)
    return res[0] if plan is None else (res[0], got)


def _swap_halves(v, lane, masked):
    up = pltpu.roll(v, HEAD - ROT_HALF, 1)
    down = pltpu.roll(v, ROT_HALF, 1)
    rest = jnp.where(lane < ROT_DIM, down, 0.0) if masked else down
    return jnp.where(lane < ROT_HALF, up, rest)


def _in_proj(name, x, w3, g, ctab, stab, qg, kg, *, tm, plan=None):
    S, K = x.shape
    P, _, C = w3.shape

    def body(x_ref, w_ref, g_ref, c_ref, s_ref, qg_ref, kg_ref, p_ref, xn_ref, q_ref, k_ref):
        xv = x_ref[...]
        ab = _bf(xv * _rstd(xv) * g_ref[...])
        xn_ref[...] = ab
        for p in range(P):
            p_ref[:, p * C:(p + 1) * C] = _dot(ab, w_ref[p])
        lane = lax.broadcasted_iota(jnp.int32, (tm, HEAD), 1)
        cv = c_ref[...]
        sv = s_ref[...]

        def prep(t, gain, scale):
            n = t * _rstd(t) * gain
            return (n * cv + _swap_halves(n, lane, False) * sv) * scale

        for j in range(Q_W // HEAD):
            q_ref[:, j * HEAD:(j + 1) * HEAD] = prep(p_ref[:, j * HEAD:(j + 1) * HEAD], qg_ref[...], SCALE)
        for j in range(KV_W // HEAD):
            k_ref[:, j * HEAD:(j + 1) * HEAD] = prep(p_ref[:, Q_W + j * HEAD:Q_W + (j + 1) * HEAD], kg_ref[...], 1.0)

    row = lambda width: pl.BlockSpec((tm, width), lambda i: (i, 0))
    vec = lambda width: pl.BlockSpec((1, width), lambda i: (0, 0))
    blk = (_nbytes((tm, K), F32) + _nbytes((P, K, C), BF16) + _nbytes((tm, P * C + Q_W + KV_W + 2 * HEAD), F32)
           + _nbytes((tm, K), BF16))
    res, got = _pcall(
        body, name=name, grid=(S // tm,),
        in_specs=[row(K), pl.BlockSpec((P, K, C), lambda i: (0, 0, 0)), vec(K), row(HEAD), row(HEAD), vec(HEAD), vec(HEAD)],
        out_specs=[row(P * C), row(K), row(Q_W), row(KV_W)],
        out_shape=[_sds((S, P * C), F32), _sds((S, K), BF16), _sds((S, Q_W), F32), _sds((S, KV_W), F32)],
        params=_cparams(1, blk), args=[x, w3, g, ctab, stab, qg, kg], plan=plan)
    return res if plan is None else (res, got)


ATTN_STEP = 2048


def _row_idx(start, dil):
    return pl.ds(start, BLOCK) if dil == 1 else pl.ds(start, BLOCK, stride=dil)


def _rows(ref, start, dil):
    return ref[_row_idx(start, dil), :]


def _set_rows(ref, start, dil, val):
    ref[_row_idx(start, dil), :] = val


def _band_mask(first):
    qi = lax.broadcasted_iota(jnp.int32, (BLOCK, 2 * BLOCK), 0)
    kj = lax.broadcasted_iota(jnp.int32, (BLOCK, 2 * BLOCK), 1)
    band = (kj >= qi) & (kj <= qi + BLOCK)
    if first is None:
        return band
    return band & ((kj >= BLOCK) | jnp.logical_not(first))


def _attn_geometry(S, grp):
    dil = DILATIONS[grp]
    bt = BLOCK * dil
    assert S % ATTN_STEP == 0 and ATTN_STEP % bt == 0
    return dil, bt, ATTN_STEP // bt, S // ATTN_STEP


MERGE_ROWS = 256


def _attn_fwd(name, qn, kn, proj, plan=None):
    S = qn.shape[0]
    groups = range(len(DILATIONS))
    geo = [_attn_geometry(S, g) for g in groups]
    nsb = geo[0][3]
    vcol = (Q_W + KV_W) // HEAD

    def body(q0, q1, q2, kc_ref, vc_ref, kp0, kp1, kp2, vp0, vp1, vp2, mix_ref, lse_ref, o_s, l_s):
        q_refs, kp_refs, vp_refs = (q0, q1, q2), (kp0, kp1, kp2), (vp0, vp1, vp2)
        valid_first = _band_mask(pl.program_id(1) == 0)
        valid_inner = _band_mask(None)
        for g in groups:
            dil, bt, nsub, _ = geo[g]
            o_g, l_g = o_s.at[g], l_s.at[g]
            for r in range(dil):
                kprev, vprev = _bf(_rows(kp_refs[g], r, dil)), _bf(_rows(vp_refs[g], r, dil))
                for b in range(nsub):
                    at = b * bt + r
                    kcur, vcur = _bf(_rows(kc_ref, at, dil)), _bf(_rows(vc_ref, at, dil))
                    q = _bf(_rows(q_refs[g], at, dil))
                    k2 = jnp.concatenate([kprev, kcur], axis=0)
                    v2 = jnp.concatenate([vprev, vcur], axis=0)
                    s = jnp.where(valid_first if b == 0 else valid_inner, _dot_nt(q, k2), NEG_INF)
                    m = jnp.max(s, axis=-1, keepdims=True)
                    p = jnp.exp(s - m)
                    l = jnp.sum(p, axis=-1, keepdims=True)
                    acc = _dot(_bf(p), v2)
                    _set_rows(o_g, at, dil, acc / l)
                    _set_rows(l_g, at, dil, jnp.broadcast_to(m + jnp.log(l), (BLOCK, HEAD)))
                    kprev, vprev = kcur, vcur
        for c in range(ATTN_STEP // MERGE_ROWS):
            rows = pl.ds(c * MERGE_ROWS, MERGE_ROWS)
            ls = [l_s[g, rows, :] for g in groups]
            m = jnp.maximum(jnp.maximum(ls[0], ls[1]), ls[2])
            ws = [jnp.exp(v - m) for v in ls]
            den = ws[0] + ws[1] + ws[2]
            num = ws[0] * o_s[0, rows, :] + ws[1] * o_s[1, rows, :] + ws[2] * o_s[2, rows, :]
            mix_ref[rows, :] = _bf(num / den)
            lse_ref[rows, :] = m + jnp.log(den)

    def big(col):
        return pl.BlockSpec((ATTN_STEP, HEAD), lambda h, n: (n, col + h))

    def tail(g, col):
        _, bt, nsub, _ = geo[g]
        return pl.BlockSpec((bt, HEAD), lambda h, n: (jnp.maximum(n * nsub - 1, 0), col + h))

    blk = 7 * _nbytes((ATTN_STEP, HEAD), F32) + 2 * sum(_nbytes((geo[g][1], HEAD), F32) for g in groups)
    scratch = 2 * len(DILATIONS) * _nbytes((ATTN_STEP, HEAD), F32)
    res, got = _pcall(
        body, name=name, grid=(N_HEADS, nsb),
        in_specs=[big(g * N_HEADS) for g in groups] + [big(0), big(vcol)]
                 + [tail(g, 0) for g in groups] + [tail(g, vcol) for g in groups],
        out_specs=[big(0), big(0)],
        out_shape=[_sds((S, 2 * KV_W), BF16), _sds((S, KV_W), F32)],
        scratch_shapes=[pltpu.VMEM((len(DILATIONS), ATTN_STEP, HEAD), F32)] * 2,
        params=_cparams(2, blk, scratch), args=[qn, qn, qn, kn, proj, kn, kn, kn, proj, proj, proj], plan=plan)
    return res if plan is None else (res, got)


STAT_LANES = HEAD // 2


def _attn_bwd(name, qn, kn, proj, dattn, stats, grp, plan=None):
    S = qn.shape[0]
    dil, bt, nsub, nsb = _attn_geometry(S, grp)
    vcol = (Q_W + KV_W) // HEAD

    def slot(b, r):
        return pl.ds((b * dil + r) * BLOCK, BLOCK)

    def body(q_ref, kp_ref, kc_ref, vp_ref, vc_ref, do_ref, st_ref, dq_ref, dk_ref, dv_ref, ck_ref, cv_ref):
        n = pl.program_id(1)
        par = n % 2

        @pl.when(n < nsb)
        def _():
            valid_first = _band_mask(n == 0)
            valid_inner = _band_mask(None)
            ck, cv = ck_ref.at[par], cv_ref.at[par]
            pk, pv = ck_ref.at[1 - par], cv_ref.at[1 - par]
            for r in range(dil):
                kprev, vprev = _bf(_rows(kp_ref, r, dil)), _bf(_rows(vp_ref, r, dil))
                own_k = own_v = None
                for b in range(nsub):
                    at = b * bt + r
                    kcur, vcur = _bf(_rows(kc_ref, at, dil)), _bf(_rows(vc_ref, at, dil))
                    q = _bf(_rows(q_ref, at, dil))
                    k2 = jnp.concatenate([kprev, kcur], axis=0)
                    v2 = jnp.concatenate([vprev, vcur], axis=0)
                    do = _bf(_rows(do_ref, at, dil))
                    stat = _rows(st_ref, at, dil)
                    lse_r = stat[:, :1]
                    del_r = stat[:, STAT_LANES:STAT_LANES + 1]
                    s = jnp.where(valid_first if b == 0 else valid_inner, _dot_nt(q, k2), NEG_INF)
                    p = jnp.exp(s - lse_r)
                    ds = _bf(p * (_dot_nt(do, v2) - del_r))
                    _set_rows(dq_ref, at, dil, _dot(ds, k2))
                    dk2 = _dot_tn(ds, q)
                    dv2 = _dot_tn(_bf(p), do)
                    if b > 0:
                        ck[slot(b - 1, r), :] = own_k + dk2[:BLOCK]
                        cv[slot(b - 1, r), :] = own_v + dv2[:BLOCK]
                    else:
                        @pl.when(n > 0)
                        def _():
                            pk[slot(nsub - 1, r), :] += dk2[:BLOCK]
                            pv[slot(nsub - 1, r), :] += dv2[:BLOCK]
                    own_k, own_v = dk2[BLOCK:], dv2[BLOCK:]
                    kprev, vprev = kcur, vcur
                ck[slot(nsub - 1, r), :] = own_k
                cv[slot(nsub - 1, r), :] = own_v

        @pl.when(n > 0)
        def _():
            for r in range(dil):
                for b in range(nsub):
                    _set_rows(dk_ref, b * bt + r, dil, ck_ref[1 - par, slot(b, r), :])
                    _set_rows(dv_ref, b * bt + r, dil, cv_ref[1 - par, slot(b, r), :])

    cur = lambda n: jnp.minimum(n, nsb - 1)
    prev = lambda n: jnp.maximum(n - 1, 0)
    tail_at = lambda n: jnp.maximum(cur(n) * nsub - 1, 0)
    big = lambda col: pl.BlockSpec((ATTN_STEP, HEAD), lambda h, n: (cur(n), col(h)))
    tail = lambda col: pl.BlockSpec((bt, HEAD), lambda h, n: (tail_at(n), col(h)))
    late = pl.BlockSpec((ATTN_STEP, HEAD), lambda h, n: (prev(n), h))
    blk = 8 * _nbytes((ATTN_STEP, HEAD), F32) + 2 * _nbytes((bt, HEAD), F32)
    res, got = _pcall(
        body, name=name, grid=(N_HEADS, nsb + 1),
        in_specs=[big(lambda h: grp * N_HEADS + h), tail(lambda h: h), big(lambda h: h),
                  tail(lambda h: vcol + h), big(lambda h: vcol + h),
                  big(lambda h: h), big(lambda h: h)],
        out_specs=[big(lambda h: h), late, late],
        out_shape=[_sds((S, KV_W), F32)] * 3,
        scratch_shapes=[pltpu.VMEM((2, ATTN_STEP, HEAD), F32), pltpu.VMEM((2, ATTN_STEP, HEAD), F32)],
        params=_cparams(2, blk, 4 * _nbytes((ATTN_STEP, HEAD), F32)),
        args=[qn, kn, kn, proj, proj, dattn, stats], plan=plan)
    return res if plan is None else (res, got)


def _window_sum(v, n_doublings, back):
    rows = v.shape[0]
    step = 1
    for _ in range(n_doublings):
        v = v + pltpu.roll(v, step if back else rows - step, 0)
        step *= 2
    return v


def _pool_out_proj(name, mix, proj, pool_w, pool_scale, w_out3, x, *, tm):
    S, D = x.shape
    ucol = (IN_W - POOL_W) // POOL_W
    hpt = tm // POOL_HALO

    def body(at_ref, u_ref, uh_ref, pw_ref, ps_ref, w_ref, x_ref, pooled_ref, dp_ref, h_ref):
        i = pl.program_id(0)
        halo = jnp.where(i == 0, 0.0, uh_ref[...])
        t = lax.broadcasted_iota(jnp.int32, (tm + POOL_HALO, HEAD), 0) + (i * tm - POOL_HALO)
        for g, w in enumerate(POOL_WINDOWS):
            sl = slice(g * HEAD, (g + 1) * HEAD)
            ub = jnp.concatenate([halo[:, sl], u_ref[:, sl]], axis=0)
            cnt = jnp.minimum(t + 1, w).astype(F32)
            d = (_window_sum(ub, g + 1, True) / cnt - ub)[POOL_HALO:]
            db = _bf(d)
            dp_ref[:, sl] = db
            pooled_ref[:, sl] = _bf(_dot(db, pw_ref[g]) * ps_ref[:, sl])
        h_ref[...] = x_ref[...] + _dot(at_ref[...], w_ref[:KV_W, :]) + _dot(pooled_ref[...], w_ref[KV_W:, :])

    row = pl.BlockSpec((tm, D), lambda i: (i, 0))
    blk = (3 * _nbytes((tm, POOL_W), F32) + 3 * _nbytes((tm, POOL_W), BF16) + _nbytes((2 * KV_W, D), BF16)
           + 3 * _nbytes((tm, D), F32))
    return pl.pallas_call(
        body, name=name, grid=(S // tm,),
        in_specs=[pl.BlockSpec((tm, KV_W), lambda i: (i, 0)),
                  pl.BlockSpec((tm, POOL_W), lambda i: (i, ucol)),
                  pl.BlockSpec((POOL_HALO, POOL_W), lambda i: (jnp.maximum(i * hpt - 1, 0), ucol)),
                  pl.BlockSpec((len(POOL_WINDOWS), HEAD, HEAD), lambda i: (0, 0, 0)),
                  pl.BlockSpec((1, POOL_W), lambda i: (0, 0)),
                  pl.BlockSpec((None, 2 * KV_W, D), lambda i: (0, 0, 0)), row],
        out_specs=[pl.BlockSpec((tm, POOL_W), lambda i: (i, 1)), pl.BlockSpec((tm, POOL_W), lambda i: (i, 0)), row],
        out_shape=[_sds((S, 2 * KV_W), BF16), _sds((S, POOL_W), BF16), _sds((S, D), F32)],
        input_output_aliases={0: 0},
        compiler_params=_cparams(1, blk))(*_pin([mix, proj, proj, pool_w, pool_scale, w_out3, x]))


def _mix_bwd(name, dh, w_out3, mix, lse, dpool, pool_w, pool_scale, *, tm, plan=None):
    S, D = dh.shape
    hpt = tm // POOL_HALO
    last_halo = S // POOL_HALO - 1
    n_tiles = S // tm

    def body(dh_ref, dhn_ref, w_ref, at_ref, l_ref, dp_ref, pw_ref, ps_ref, da_ref, st_ref, du_ref, gw_ref, gs_ref):
        i = pl.program_id(0)

        @pl.when(i == 0)
        def _():
            gw_ref[...] = jnp.zeros_like(gw_ref)
            gs_ref[...] = jnp.zeros_like(gs_ref)

        dm = _dot_nt(_bf(dh_ref[...]), w_ref[...])
        ahead = _dot_nt(_bf(dhn_ref[...]), w_ref[KV_W:, :])
        lane = lax.broadcasted_iota(jnp.int32, (tm, HEAD), 1)
        for h in range(N_HEADS):
            sl = slice(h * HEAD, (h + 1) * HEAD)
            da = dm[:, sl]
            da_ref[:, sl] = da
            delta = jnp.sum(da * at_ref[:, sl].astype(F32), axis=-1, keepdims=True)
            st_ref[:, sl] = jnp.where(lane < STAT_LANES, l_ref[:, sl], delta)
        halo = jnp.where(i == n_tiles - 1, 0.0, ahead)
        t = lax.broadcasted_iota(jnp.int32, (tm + POOL_HALO, HEAD), 0) + i * tm
        for g, w in enumerate(POOL_WINDOWS):
            sl = slice(g * HEAD, (g + 1) * HEAD)
            dy = jnp.concatenate([dm[:, KV_W + g * HEAD:KV_W + (g + 1) * HEAD], halo[:, sl]], axis=0)
            dys = _bf(dy * ps_ref[:, sl])
            dd = _dot_nt(dys, pw_ref[g])
            cnt = jnp.minimum(t + 1, w).astype(F32)
            du_ref[:, sl] = (_window_sum(dd / cnt, g + 1, False) - dd)[:tm]
            db = dp_ref[:, sl]
            gw_ref[g] += _dot_tn(db, dys[:tm])
            gs_ref[:, sl] += jnp.sum(dy[:tm] * _dot(db, pw_ref[g]), axis=0, keepdims=True)

    tile = pl.BlockSpec((tm, KV_W), lambda i: (i, 0))
    blk = 2 * _nbytes((tm, D), F32) + _nbytes((2 * KV_W, D), BF16) + 6 * _nbytes((tm, KV_W), F32)
    res, got = _pcall(
        body, name=name, grid=(n_tiles,),
        in_specs=[pl.BlockSpec((tm, D), lambda i: (i, 0)),
                  pl.BlockSpec((POOL_HALO, D), lambda i: (jnp.minimum((i + 1) * hpt, last_halo), 0)),
                  pl.BlockSpec((None, 2 * KV_W, D), lambda i: (0, 0, 0)),
                  tile, tile, tile,
                  pl.BlockSpec((len(POOL_WINDOWS), HEAD, HEAD), lambda i: (0, 0, 0)),
                  pl.BlockSpec((1, POOL_W), lambda i: (0, 0))],
        out_specs=[tile, tile, tile,
                   pl.BlockSpec((len(POOL_WINDOWS), HEAD, HEAD), lambda i: (0, 0, 0)),
                   pl.BlockSpec((1, POOL_W), lambda i: (0, 0))],
        out_shape=[_sds((S, KV_W), F32)] * 3 + [
            _sds((len(POOL_WINDOWS), HEAD, HEAD), F32), _sds((1, POOL_W), F32)],
        params=_cparams(1, blk), args=[dh, dh, w_out3, mix, lse, dpool, pool_w, pool_scale], plan=plan)
    return res if plan is None else (res, got)


def _qkv_bwd(name, dqs, dks, dvs, du, proj, ctab, stab, qg, kg, *, tm, plan=None):
    S = proj.shape[0]
    width = Q_W + KV_W

    def body(dq0, dq1, dq2, dk0, dk1, dk2, dv0, dv1, dv2, du_ref, p_ref, c_ref, s_ref, qg_ref, kg_ref,
             dp_ref, gq_ref, gk_ref):
        @pl.when(pl.program_id(0) == 0)
        def _():
            gq_ref[...] = jnp.zeros_like(gq_ref)
            gk_ref[...] = jnp.zeros_like(gk_ref)

        lane = lax.broadcasted_iota(jnp.int32, (tm, HEAD), 1)
        cv = c_ref[...]
        sv = s_ref[...]

        def back(dy, t, g, scale):
            dy = dy * scale
            dn = dy * cv + _swap_halves(dy * sv, lane, True)
            dt, xh = _norm_bwd(dn, t, g)
            return dt, jnp.sum(dn * xh, axis=0, keepdims=True)

        dqr = (dq0, dq1, dq2)
        gq = jnp.zeros((1, HEAD), F32)
        for j in range(Q_W // HEAD):
            grp, h = divmod(j, N_HEADS)
            dt, gj = back(dqr[grp][:, h * HEAD:(h + 1) * HEAD], p_ref[:, j * HEAD:(j + 1) * HEAD], qg_ref[...], SCALE)
            dp_ref[:, j * HEAD:(j + 1) * HEAD] = _bf(dt)
            gq = gq + gj
        gq_ref[...] += gq
        gk = jnp.zeros((1, HEAD), F32)
        for h in range(N_HEADS):
            sl = slice(h * HEAD, (h + 1) * HEAD)
            dt, gj = back(dk0[:, sl] + dk1[:, sl] + dk2[:, sl], p_ref[:, Q_W + h * HEAD:Q_W + (h + 1) * HEAD],
                          kg_ref[...], 1.0)
            dp_ref[:, Q_W + h * HEAD:Q_W + (h + 1) * HEAD] = _bf(dt)
            gk = gk + gj
        gk_ref[...] += gk
        dp_ref[:, width:width + KV_W] = _bf(dv0[...] + dv1[...] + dv2[...])
        dp_ref[:, width + KV_W:] = _bf(du_ref[...])

    tile = pl.BlockSpec((tm, KV_W), lambda i: (i, 0))
    vec = pl.BlockSpec((1, HEAD), lambda i: (0, 0))
    rot = pl.BlockSpec((tm, HEAD), lambda i: (i, 0))
    blk = 10 * _nbytes((tm, KV_W), F32) + _nbytes((tm, width), F32) + _nbytes((tm, IN_W), BF16)
    res, got = _pcall(
        body, name=name, grid=(S // tm,),
        in_specs=[tile] * 10 + [pl.BlockSpec((tm, width), lambda i: (i, 0)), rot, rot, vec, vec],
        out_specs=[pl.BlockSpec((tm, IN_W), lambda i: (i, 0)), vec, vec],
        out_shape=[_sds((S, IN_W), BF16), _sds((1, HEAD), F32),
                   _sds((1, HEAD), F32)],
        params=_cparams(1, blk), args=[*dqs, *dks, *dvs, du, proj, ctab, stab, qg, kg], plan=plan)
    return res if plan is None else (res, got)


def _cross_heads(q_ref, kv_ref, qg, kg, h):
    sl = slice(h * HEAD, (h + 1) * HEAD)
    qr = q_ref[:, sl]
    kr = kv_ref[:, sl]
    qh = qr * _rstd(qr) * qg * SCALE
    kh = kr * _rstd(kr) * kg
    vh = kv_ref[:, X_W + h * HEAD:X_W + (h + 1) * HEAD]
    return qr, _bf(qh), _bf(kh), _bf(vh)


def _cross_fwd(name, h_in, g_in, w_cq3, kv, qg, kg, w_co3, *, tm):
    S, D = h_in.shape
    M = kv.shape[0]
    P, _, C = w_co3.shape

    def body(h_ref, g_ref, wcq_ref, kv_ref, qg_ref, kg_ref, wco_ref, q_ref, hn_ref, o_ref, ho_ref):
        hv = h_ref[...]
        ab = _bf(hv * _rstd(hv) * g_ref[...])
        hn_ref[...] = ab
        q_ref[...] = _dot(ab, wcq_ref[...])
        for h in range(N_HEADS):
            _, qh, kh, vh = _cross_heads(q_ref, kv_ref, qg_ref[...], kg_ref[...], h)
            s = _dot_nt(qh, kh)
            p = jnp.exp(s - jnp.max(s, axis=-1, keepdims=True))
            l = jnp.sum(p, axis=-1, keepdims=True)
            o_ref[:, h * HEAD:(h + 1) * HEAD] = _bf(_dot(_bf(p), vh) / l)
        ob = o_ref[...]
        for p in range(P):
            ho_ref[:, p * C:(p + 1) * C] = hv[:, p * C:(p + 1) * C] + _dot(ob, wco_ref[p])

    vec = pl.BlockSpec((1, HEAD), lambda i: (0, 0))
    row = pl.BlockSpec((tm, D), lambda i: (i, 0))
    qtile = pl.BlockSpec((tm, X_W), lambda i: (i, 0))
    blk = (3 * _nbytes((tm, D), F32) + 2 * _nbytes((tm, X_W), F32) + _nbytes((M, 2 * X_W), F32)
           + 4 * _nbytes((tm, M), F32) + _nbytes((D, X_W), BF16) + _nbytes(w_co3.shape, BF16))
    return pl.pallas_call(
        body, name=name, grid=(S // tm,),
        in_specs=[row, pl.BlockSpec((1, D), lambda i: (0, 0)), pl.BlockSpec((None, D, X_W), lambda i: (0, 0, 0)),
                  pl.BlockSpec((M, 2 * X_W), lambda i: (0, 0)), vec, vec,
                  pl.BlockSpec(w_co3.shape, lambda i: (0, 0, 0))],
        out_specs=[qtile, row, qtile, row],
        out_shape=[_sds((S, X_W), F32), _sds((S, D), BF16), _sds((S, X_W), BF16), _sds((S, D), F32)],
        compiler_params=_cparams(1, blk))(*_pin([h_in, g_in, w_cq3, kv, qg, kg, w_co3]))


def _cross_bwd(name, dh, w_co3, qraw, kv, qg, kg, w_cq3, h_in, g_in, *, tm, plan=None):
    S, D = dh.shape
    M = kv.shape[0]

    def body(dh_ref, wco_ref, q_ref, kv_ref, qg_ref, kg_ref, wcq_ref, h_ref, g_ref,
             dq_ref, dk_ref, dv_ref, gq_ref, dx_ref, gn_ref):
        @pl.when(pl.program_id(0) == 0)
        def _():
            dk_ref[...] = jnp.zeros_like(dk_ref)
            dv_ref[...] = jnp.zeros_like(dv_ref)
            gq_ref[...] = jnp.zeros_like(gq_ref)
            gn_ref[...] = jnp.zeros_like(gn_ref)

        dhv = dh_ref[...]
        do_all = _nt_pieces(dh_ref, wco_ref)
        gq = jnp.zeros((1, HEAD), F32)
        for h in range(N_HEADS):
            sl = slice(h * HEAD, (h + 1) * HEAD)
            qr, qh, kh, vh = _cross_heads(q_ref, kv_ref, qg_ref[...], kg_ref[...], h)
            doh = _bf(do_all[:, sl])
            s = _dot_nt(qh, kh)
            p = jnp.exp(s - jnp.max(s, axis=-1, keepdims=True))
            p = p / jnp.sum(p, axis=-1, keepdims=True)
            pb = _bf(p)
            dp = _dot_nt(doh, vh)
            ds = _bf(p * (dp - jnp.sum(dp * p, axis=-1, keepdims=True)))
            dv_ref[:, sl] += _dot_tn(pb, doh)
            dk_ref[:, sl] += _dot_tn(ds, qh)
            dn = _dot(ds, kh) * SCALE
            dt, xh = _norm_bwd(dn, qr, qg_ref[...])
            dq_ref[:, sl] = _bf(dt)
            gq = gq + jnp.sum(dn * xh, axis=0, keepdims=True)
        gq_ref[...] += gq
        dhn = _dot_nt(dq_ref[...], wcq_ref[...])
        dx, xh = _norm_bwd(dhn, h_ref[...], g_ref[...])
        dx_ref[...] = dx + dhv
        gn_ref[...] += jnp.sum(dhn * xh, axis=0, keepdims=True)

    vec = pl.BlockSpec((1, HEAD), lambda i: (0, 0))
    acc = pl.BlockSpec((M, X_W), lambda i: (0, 0))
    row = pl.BlockSpec((tm, D), lambda i: (i, 0))
    wide = pl.BlockSpec((1, D), lambda i: (0, 0))
    qtile = pl.BlockSpec((tm, X_W), lambda i: (i, 0))
    blk = (3 * _nbytes((tm, X_W), F32) + 3 * _nbytes((M, 2 * X_W), F32) + 6 * _nbytes((tm, M), F32)
           + 4 * _nbytes((tm, D), F32) + _nbytes(w_co3.shape, BF16) + _nbytes((D, X_W), BF16))
    res, got = _pcall(
        body, name=name, grid=(S // tm,),
        in_specs=[row, pl.BlockSpec(w_co3.shape, lambda i: (0, 0, 0)), qtile,
                  pl.BlockSpec((M, 2 * X_W), lambda i: (0, 0)), vec, vec,
                  pl.BlockSpec((None, D, X_W), lambda i: (0, 0, 0)), row, wide],
        out_specs=[qtile, acc, acc, vec, row, wide],
        out_shape=[_sds((S, X_W), BF16), _sds((M, X_W), F32), _sds((M, X_W), F32), _sds((1, HEAD), F32),
                   _sds((S, D), F32), _sds((1, D), F32)],
        params=_cparams(1, blk), args=[dh, w_co3, qraw, kv, qg, kg, w_cq3, h_in, g_in], plan=plan)
    return res if plan is None else (res, got)


def _cross_kv_bwd(name, dkn, dv, kv, kg):
    M = kv.shape[0]

    def body(dk_ref, dv_ref, kv_ref, kg_ref, o_ref, g_ref):
        gk = jnp.zeros((1, HEAD), F32)
        for h in range(N_HEADS):
            sl = slice(h * HEAD, (h + 1) * HEAD)
            dn = dk_ref[:, sl]
            dt, xh = _norm_bwd(dn, kv_ref[:, sl], kg_ref[...])
            o_ref[:, sl] = _bf(dt)
            gk = gk + jnp.sum(dn * xh, axis=0, keepdims=True)
        o_ref[:, X_W:] = _bf(dv_ref[...])
        g_ref[...] = gk

    full = lambda shape: pl.BlockSpec(shape, lambda i: (0,) * len(shape))
    return pl.pallas_call(
        body, name=name, grid=(1,),
        in_specs=[full((M, X_W)), full((M, X_W)), full((M, 2 * X_W)), full((1, HEAD))],
        out_specs=[full((M, 2 * X_W)), full((1, HEAD))],
        out_shape=[_sds((M, 2 * X_W), BF16), _sds((1, HEAD), F32)],
        compiler_params=_cparams(1, 6 * _nbytes((M, 2 * X_W), F32)))(dkn, dv, kv, kg)


def _rope_tables(positions):
    inv_freq = ROPE_THETA ** (-jnp.arange(0, ROT_DIM, 2, dtype=F32) / ROT_DIM)
    ang = positions.astype(F32)[:, None] * inv_freq
    cos, sin = jnp.cos(ang), jnp.sin(ang)
    S = positions.shape[0]
    ctab = jnp.concatenate([cos, cos, jnp.ones((S, HEAD - ROT_DIM), F32)], axis=-1)
    stab = jnp.concatenate([-sin, sin, jnp.zeros((S, HEAD - ROT_DIM), F32)], axis=-1)
    return ctab, stab


GATHER_BEHIND_IN_PROJ = ("w_out", "w_cq", "w_ckv", "w_co")
FFN_WEIGHTS = ("w_gate_up", "w_down")


def _hosted(fn, *args, plan=None, **kw):
    if plan is None:
        return fn(*args, **kw), []
    return fn(*args, plan=plan, **kw)


def _local_step(x, mem, positions, target, wb, sm, *, tm=512, place=None):
    S, D = x.shape
    M = mem.shape[0]
    dist = place is not None
    wb = dict(wb)
    ctab, stab = _rope_tables(positions)
    pool_w_b = _bf(sm["pool_w"])

    def gather(names):
        return _GatherPlan([wb[k] for k in names]) if dist else None

    if dist:
        wb["w_in"], = _run_plan("gather_w_in", gather(["w_in"]))
    w_in = wb["w_in"]
    behind_in_proj = GATHER_BEHIND_IN_PROJ + ("w_gate_up",)
    (proj, xn1, qn, kn), got = _hosted(_in_proj, "in_proj", x, w_in, sm["mix_norm_g"], ctab, stab,
                                       sm["q_norm_g"], sm["k_norm_g"], tm=tm, plan=gather(behind_in_proj))
    wb.update(zip(behind_in_proj, got))
    (mix, lse), got = _hosted(_attn_fwd, "attn_fwd", qn, kn, proj, plan=gather(["w_down"]))
    wb.update(zip(["w_down"], got))
    w_out = wb["w_out"].reshape(1, 2 * KV_W, D)
    w_cq = wb["w_cq"].reshape(1, D, X_W)
    w_ckv = wb["w_ckv"].reshape(1, D, 2 * X_W)
    w_co = wb["w_co"]
    w_gu = wb["w_gate_up"]
    w_down = wb["w_down"].reshape(1, D_FF, D)
    cin = w_in.shape[2]
    cco = w_co.shape[2]
    cgu = w_gu.shape[2]
    mix, dpool, h1 = _pool_out_proj("pool_out_proj", mix, proj, pool_w_b, sm["pool_scale"], w_out, x, tm=tm)
    kv, mem_n = _mm_nn("ckv_proj", mem, w_ckv, tm=M, norm_g=sm["mem_norm_g"])
    cq_raw, hn2, xo, h2 = _cross_fwd("cross_fwd", h1, sm["cross_norm_g"], w_cq, kv, sm["cq_norm_g"],
                                     sm["ck_norm_g"], w_co, tm=tm)
    act, gu, hn3 = _ffn_up("ffn_up", h2, w_gu, sm["ffn_norm_g"], tm=tm)
    dy, lsum = _ffn_down_loss("ffn_down_loss", act, w_down, h2, target, tm=tm)
    loss = 0.5 * jnp.sum(lsum) / D

    ts = 2 * tm
    nS = S // ts
    g_down = _mm_tn("g_w_down", act, pl.BlockSpec((ts, cgu), lambda r, c, s: (s, r)), (ts, cgu),
                    dy, pl.BlockSpec((ts, D), lambda r, c, s: (s, 0)), (ts, D),
                    (1, D_FF, D), pl.BlockSpec((None, cgu, D), lambda r, c, s: (0, r, 0)), (cgu, D),
                    (D_FF // cgu, 1, nS))
    dgu, dh2, g_ffn_norm = _ffn_bwd("ffn_bwd", dy, w_down, gu, w_gu, h2, sm["ffn_norm_g"], tm=tm // 2)
    g_gu = _mm_tn("g_w_gate_up", hn3, pl.BlockSpec((ts, D), lambda r, c, s: (s, 0)), (ts, D),
                  dgu, pl.BlockSpec((None, ts, cgu), lambda r, c, s: (c // 2, s, c % 2)), (ts, cgu),
                  (4, D, cgu), pl.BlockSpec((None, D, cgu), lambda r, c, s: (c, 0, 0)), (D, cgu),
                  (1, 4, nS))

    g_co = _mm_tn_wide("g_w_co", xo, dh2, N_CHIPS, ts=ts)
    full = {"w_gate_up": g_gu, "w_down": g_down.reshape(N_CHIPS, D_FF // N_CHIPS, D)}
    sums = {}

    def swap(names):
        return _SwapPlan([full[k] for k in names]) if dist else None

    def add_halves(names, from_sibling):
        for k, t in zip(names, from_sibling):
            sums[k] = _add_core_halves(f"add_halves_{k}", full[k], t, place[1])

    def exchange(names):
        return _ExchangePlan([sums[k][1] for k in names]) if dist else None

    def sum_chips(names, from_chips):
        return [_sum_chips(f"sum_chips_{k}", sums[k][0], t, place[1]) for k, t in zip(names, from_chips)]

    (dcq, dkn, dvm, g_cq_norm, dh1, g_cross_norm), got = _hosted(
        _cross_bwd, "cross_bwd", dh2, w_co, cq_raw, kv, sm["cq_norm_g"], sm["ck_norm_g"], w_cq, h1,
        sm["cross_norm_g"], tm=tm, plan=swap(FFN_WEIGHTS))
    add_halves(FFN_WEIGHTS, got)
    dkv, g_ck_norm = _cross_kv_bwd("cross_kv_bwd", dkn, dvm, kv, sm["ck_norm_g"])
    g_cq = _mm_tn("g_w_cq", hn2, pl.BlockSpec((ts, D), lambda r, c, s: (s, 0)), (ts, D),
                  dcq, pl.BlockSpec((ts, X_W), lambda r, c, s: (s, 0)), (ts, X_W),
                  (1, D, X_W), pl.BlockSpec((None, D, X_W), lambda r, c, s: (0, 0, 0)), (D, X_W), (1, 1, nS))
    _, g_mem_norm = _mm_nt_normbwd("ckv_proj_bwd", dkv, w_ckv, mem, sm["mem_norm_g"], None, tm=M)
    g_ckv = _mm_tn("g_w_ckv", mem_n, pl.BlockSpec((M, D), lambda r, c, s: (0, 0)), (M, D),
                   dkv, pl.BlockSpec((M, 2 * X_W), lambda r, c, s: (0, 0)), (M, 2 * X_W),
                   (1, D, 2 * X_W), pl.BlockSpec((None, D, 2 * X_W), lambda r, c, s: (0, 0, 0)), (D, 2 * X_W),
                   (1, 1, 1))

    g_out = _mm_tn("g_w_out", mix, pl.BlockSpec((ts, 2 * KV_W), lambda r, c, s: (s, 0)), (ts, 2 * KV_W),
                   dh1, pl.BlockSpec((ts, D), lambda r, c, s: (s, 0)), (ts, D),
                   (1, 2 * KV_W, D), pl.BlockSpec((None, 2 * KV_W, D), lambda r, c, s: (0, 0, 0)), (2 * KV_W, D),
                   (1, 1, nS))
    full.update({
        "w_out": g_out.reshape(N_CHIPS, 2 * KV_W // N_CHIPS, D),
        "w_cq": g_cq.reshape(N_CHIPS, D // N_CHIPS, X_W),
        "w_ckv": g_ckv.reshape(N_CHIPS, D // N_CHIPS, 2 * X_W),
        "w_co": g_co,
    })
    mixer = GATHER_BEHIND_IN_PROJ
    (dattn, stats, du, g_pool_w, g_pool_scale), got = _hosted(
        _mix_bwd, "mix_bwd", dh1, w_out, mix, lse, dpool, pool_w_b, sm["pool_scale"], tm=tm, plan=swap(mixer))
    add_halves(mixer, got)
    behind_attn = (None, mixer, FFN_WEIGHTS)
    halves = {}
    dqs, dks, dvs = [], [], []
    for grp in range(len(DILATIONS)):
        names = behind_attn[grp]
        (dq, dk, dv), got = _hosted(_attn_bwd, f"attn_bwd{grp}", qn, kn, proj, dattn, stats, grp,
                                    plan=exchange(names) if names else None)
        if dist and names:
            halves.update(zip(names, sum_chips(names, got)))
        dqs.append(dq)
        dks.append(dk)
        dvs.append(dv)
    joined = mixer + FFN_WEIGHTS
    (dproj, g_q_norm, g_k_norm), got = _hosted(
        _qkv_bwd, "qkv_bwd", dqs, dks, dvs, du, proj, ctab, stab, sm["q_norm_g"], sm["k_norm_g"], tm=tm,
        plan=_JoinPlan([halves[k] for k in joined]) if dist else None)
    shards = dict(zip(joined, got))
    dx, g_mix_norm = _mm_nt_normbwd("in_proj_bwd", dproj, w_in, x, sm["mix_norm_g"], dh1, tm=tm)
    small = {
        "mix_norm_g": g_mix_norm, "q_norm_g": g_q_norm, "k_norm_g": g_k_norm, "pool_w": g_pool_w,
        "pool_scale": g_pool_scale, "cross_norm_g": g_cross_norm, "mem_norm_g": g_mem_norm,
        "cq_norm_g": g_cq_norm, "ck_norm_g": g_ck_norm, "ffn_norm_g": g_ffn_norm,
    }
    full["w_in"], got = _hosted(_mm_tn_wide, "g_w_in", xn1, dproj, N_CHIPS, ts=tm,
                                plan=_AllPushPlan(_pack_small(small)) if dist else None)
    if dist:
        small = _unpack_small(_sum_slots("sum_small", got[0]), sm)
        add_halves(["w_in"], _run_plan("swap_w_in", swap(["w_in"])))
        half, = sum_chips(["w_in"], _run_plan("exchange_w_in", exchange(["w_in"])))
        shards["w_in"], = _run_plan("join_w_in", _JoinPlan([half]))
    big = shards if dist else full
    return loss, dx, big, small


BIG = ("w_in", "w_out", "w_cq", "w_ckv", "w_co", "w_gate_up", "w_down")
SMALL = ("mix_norm_g", "q_norm_g", "k_norm_g", "pool_w", "pool_scale", "cross_norm_g", "mem_norm_g",
         "cq_norm_g", "ck_norm_g", "ffn_norm_g")
WEIGHTS = ("mix_norm_g", "w_in", "q_norm_g", "k_norm_g", "pool_w", "pool_scale", "w_out", "cross_norm_g",
           "mem_norm_g", "w_cq", "w_ckv", "cq_norm_g", "ck_norm_g", "w_co", "ffn_norm_g", "w_gate_up", "w_down")


def _cast_piece(name, w, k_arr):
    R, C = w.shape
    hr = R // 2

    def body(k_ref, w_ref, o_ref):
        o_ref[...] = _bf(w_ref[...])

    return pl.pallas_call(
        body, name=name,
        grid_spec=pltpu.PrefetchScalarGridSpec(
            num_scalar_prefetch=1, grid=(2,),
            in_specs=[pl.BlockSpec((hr, C), lambda i, k: (i, 0))],
            out_specs=pl.BlockSpec((None, hr, C), lambda i, k: (k[0], i, 0))),
        out_shape=_sds((N_CHIPS, R, C), BF16),
        compiler_params=_cparams(1, 2 * _nbytes((hr, C), F32)))(k_arr, *_pin([w]))


def _add_core_halves(name, g, t, kc_arr):
    P, R, C = g.shape
    hr = R // 2

    def body(kc_ref, g_ref, t_ref, o_ref, ob_ref):
        tot = g_ref[...] + t_ref[...]
        ob_ref[...] = _bf(tot)

        @pl.when(pl.program_id(0) == kc_ref[0])
        def _():
            o_ref[...] = tot

    piece = pl.BlockSpec((None, hr, C), lambda p, kc: (p, 0, 0))
    return pl.pallas_call(
        body, name=name,
        grid_spec=pltpu.PrefetchScalarGridSpec(
            num_scalar_prefetch=1, grid=(P,),
            in_specs=[pl.BlockSpec((None, hr, C), lambda p, kc: (p, kc[1], 0)), piece],
            out_specs=[pl.BlockSpec((hr, C), lambda p, kc: (0, 0)), piece]),
        out_shape=[_sds((hr, C), F32), _sds((P, hr, C), BF16)],
        compiler_params=_cparams(1, 4 * _nbytes((hr, C), F32)))(kc_arr, *_pin([g, t]))


def _sum_chips(name, own, got, kc_arr):
    hr, C = own.shape

    def body(kc_ref, o_ref, g_ref, r_ref):
        r_ref[...] = ((o_ref[...] + g_ref[0].astype(F32)) + g_ref[1].astype(F32)) + g_ref[2].astype(F32)

    return pl.pallas_call(
        body, name=name,
        grid_spec=pltpu.PrefetchScalarGridSpec(
            num_scalar_prefetch=1, grid=(1,),
            in_specs=[pl.BlockSpec((hr, C), lambda i, kc: (0, 0)),
                      pl.BlockSpec((N_CHIPS - 1, hr, C), lambda i, kc: (0, 0, 0))],
            out_specs=pl.BlockSpec((hr, C), lambda i, kc: (kc[1], 0))),
        out_shape=_sds((2 * hr, C), F32),
        compiler_params=_cparams(1, 5 * _nbytes((hr, C), F32)))(kc_arr, *_pin([own, got]))


N_DEV = 8


class _AllPushPlan(_Plan):
    def __init__(self, v):
        self.ins = [v]
        self.out_shapes = [jax.ShapeDtypeStruct((N_DEV,) + v.shape, v.dtype)]
        self.sem_shapes = [pltpu.SemaphoreType.DMA((N_DEV - 1,)), pltpu.SemaphoreType.DMA((N_DEV - 1,)),
                           pltpu.SemaphoreType.DMA]

    def copies(self, ins, outs, sems):
        send, recv, own = sems
        x, y, c, _ = _place()
        slot = outs[0].at[4 * x + 2 * y + c]
        cps = [pltpu.make_async_copy(ins[0], slot, own)]
        flips = [(dx, dy, dc) for dx in (0, 1) for dy in (0, 1) for dc in (0, 1)][1:]
        for q, (dx, dy, dc) in enumerate(flips):
            to = (x + dx - 2 * x * dx, y + dy - 2 * y * dy, c + dc - 2 * c * dc)
            cps.append(pltpu.make_async_remote_copy(src_ref=ins[0], dst_ref=slot, send_sem=send.at[q],
                                                    recv_sem=recv.at[q], device_id=to, device_id_type=MESH))
        return cps


def _sum_slots(name, slots):
    n, R, C = slots.shape

    def body(s_ref, o_ref):
        acc = s_ref[0]
        for d in range(1, n):
            acc = acc + s_ref[d]
        o_ref[...] = acc

    return pl.pallas_call(
        body, name=name, grid=(1,),
        in_specs=[pl.BlockSpec((n, R, C), lambda i: (0, 0, 0))], out_specs=pl.BlockSpec((R, C), lambda i: (0, 0)),
        out_shape=_sds((R, C), F32),
        compiler_params=_cparams(1, _nbytes((n + 1, R, C), F32)))(*_pin([slots]))


def _adamw(name, w, g, m, v, *, tr):
    R, C = w.shape

    def body(w_ref, g_ref, m_ref, v_ref, d_ref, nm_ref, nv_ref):
        gv = g_ref[...]
        nm = ADAM_B1 * m_ref[...] + (1.0 - ADAM_B1) * gv
        nv = ADAM_B2 * v_ref[...] + (1.0 - ADAM_B2) * (gv * gv)
        m_hat = nm / (1.0 - ADAM_B1 ** ADAM_STEP)
        v_hat = nv / (1.0 - ADAM_B2 ** ADAM_STEP)
        d_ref[...] = -ADAM_LR * (m_hat / (jnp.sqrt(v_hat) + ADAM_EPS) + ADAM_WD * w_ref[...])
        nm_ref[...] = nm
        nv_ref[...] = nv

    tile = pl.BlockSpec((tr, C), lambda i: (i, 0))
    return pl.pallas_call(
        body, name=name, grid=(R // tr,), in_specs=[tile] * 4, out_specs=[tile] * 3,
        out_shape=[_sds((R, C), F32)] * 3,
        compiler_params=_cparams(1, 7 * _nbytes((tr, C), F32)))(*_pin([w, g, m, v]))


def _pack_small(d):
    parts = []
    for name in SMALL:
        a = d[name].reshape(-1, HEAD)
        pad = (-a.shape[0]) % 8
        parts.append(jnp.pad(a, ((0, pad), (0, 0))))
    return jnp.concatenate(parts, axis=0)


def _unpack_small(packed, like):
    out = {}
    row = 0
    for name in SMALL:
        shape = like[name].shape
        rows = like[name].size // HEAD
        out[name] = packed[row:row + rows].reshape(shape)
        row += rows + (-rows) % 8
    return out


def kernel(x, mem, positions, mix_norm_g, w_in, q_norm_g, k_norm_g, pool_w, pool_scale, w_out, cross_norm_g, mem_norm_g, w_cq, w_ckv, cq_norm_g, ck_norm_g, w_co, ffn_norm_g, w_gate_up, w_down, loss_target, m_mix_norm_g, m_w_in, m_q_norm_g, m_k_norm_g, m_pool_w, m_pool_scale, m_w_out, m_cross_norm_g, m_mem_norm_g, m_w_cq, m_w_ckv, m_cq_norm_g, m_ck_norm_g, m_w_co, m_ffn_norm_g, m_w_gate_up, m_w_down, v_mix_norm_g, v_w_in, v_q_norm_g, v_k_norm_g, v_pool_w, v_pool_scale, v_w_out, v_cross_norm_g, v_mem_norm_g, v_w_cq, v_w_ckv, v_cq_norm_g, v_ck_norm_g, v_w_co, v_ffn_norm_g, v_w_gate_up, v_w_down):
    w = dict(mix_norm_g=mix_norm_g, w_in=w_in, q_norm_g=q_norm_g, k_norm_g=k_norm_g, pool_w=pool_w,
             pool_scale=pool_scale, w_out=w_out, cross_norm_g=cross_norm_g, mem_norm_g=mem_norm_g, w_cq=w_cq,
             w_ckv=w_ckv, cq_norm_g=cq_norm_g, ck_norm_g=ck_norm_g, w_co=w_co, ffn_norm_g=ffn_norm_g,
             w_gate_up=w_gate_up, w_down=w_down)
    m = dict(mix_norm_g=m_mix_norm_g, w_in=m_w_in, q_norm_g=m_q_norm_g, k_norm_g=m_k_norm_g, pool_w=m_pool_w,
             pool_scale=m_pool_scale, w_out=m_w_out, cross_norm_g=m_cross_norm_g, mem_norm_g=m_mem_norm_g,
             w_cq=m_w_cq, w_ckv=m_w_ckv, cq_norm_g=m_cq_norm_g, ck_norm_g=m_ck_norm_g, w_co=m_w_co,
             ffn_norm_g=m_ffn_norm_g, w_gate_up=m_w_gate_up, w_down=m_w_down)
    v = dict(mix_norm_g=v_mix_norm_g, w_in=v_w_in, q_norm_g=v_q_norm_g, k_norm_g=v_k_norm_g, pool_w=v_pool_w,
             pool_scale=v_pool_scale, w_out=v_w_out, cross_norm_g=v_cross_norm_g, mem_norm_g=v_mem_norm_g,
             w_cq=v_w_cq, w_ckv=v_w_ckv, cq_norm_g=v_cq_norm_g, ck_norm_g=v_ck_norm_g, w_co=v_w_co,
             ffn_norm_g=v_ffn_norm_g, w_gate_up=v_w_gate_up, w_down=v_w_down)

    c_arr = lax.axis_index("c").astype(jnp.int32).reshape(1)
    k_arr = (2 * lax.axis_index("x") + lax.axis_index("y")).astype(jnp.int32).reshape(1)
    kc_arr = jnp.concatenate([k_arr, c_arr])
    wb = {k: _cast_piece(f"cast_{k}", w[k][0], k_arr) for k in BIG}
    sm = {k: (w[k][0] if k == "pool_w" else w[k]) for k in SMALL}
    loss_part, dx, gshard, gsm = _local_step(x[0], mem[0], positions[0], loss_target[0], wb, sm,
                                             place=(c_arr, kc_arr))
    loss = lax.psum(loss_part, ("x", "y", "c"))

    grads, deltas, new_m, new_v = {}, {}, {}, {}
    for k in BIG:
        shard = w[k][0]
        tr = shard.shape[0] // 4
        d, nm, nv = _adamw(f"adamw_{k}", shard, gshard[k], m[k][0], v[k][0], tr=tr)
        grads[k], deltas[k], new_m[k], new_v[k] = gshard[k][None], d[None], nm[None], nv[None]
    smw = {k: (w[k][0] if k == "pool_w" else w[k]) for k in SMALL}
    smm = {k: (m[k][0] if k == "pool_w" else m[k]) for k in SMALL}
    smv = {k: (v[k][0] if k == "pool_w" else v[k]) for k in SMALL}
    pw, pg, pm, pv = _pack_small(smw), _pack_small(gsm), _pack_small(smm), _pack_small(smv)
    d, nm, nv = _adamw("adamw_small", pw, pg, pm, pv, tr=pw.shape[0])
    for dst, packed in ((deltas, d), (new_m, nm), (new_v, nv)):
        un = _unpack_small(packed, sm)
        for k in SMALL:
            dst[k] = un[k].reshape(w[k].shape)
    for k in SMALL:
        grads[k] = gsm[k].reshape(w[k].shape)

    return (loss, dx[None], *[grads[k] for k in WEIGHTS], *[deltas[k] for k in WEIGHTS],
            *[new_m[k] for k in WEIGHTS], *[new_v[k] for k in WEIGHTS])
```

```python
import functools

import jax
import jax.numpy as jnp
from jax import lax
from jax.experimental import pallas as pl
from jax.experimental.pallas import tpu as pltpu

F32 = jnp.float32
BF16 = jnp.bfloat16
MESH = pl.DeviceIdType.MESH
ANY = pl.BlockSpec(memory_space=pl.ANY)

D_MODEL = 1024
HEAD = 128
N_HEADS = 4
DILATIONS = (1, 4, 16)
BLOCK = 128
Q_W = 1536
KV_W = 512
POOL_W = 512
POOL_WINDOWS = (2, 4, 8, 16)
POOL_HALO = 16
IN_W = 3072
ROT_DIM = 32
ROT_HALF = 16
ROPE_THETA = 500000.0
X_W = 512
D_FF = 2816
EPS = 1e-6
NEG_INF = -1e30
SCALE = HEAD ** -0.5
N_CHIPS = 4

ADAM_LR = 0.001
ADAM_B1 = 0.9
ADAM_B2 = 0.999
ADAM_EPS = 1e-08
ADAM_WD = 0.01
ADAM_STEP = 10

VMEM_BYTES_V7X = 64 * 2 ** 20
VMEM_LIMIT_MAX = 56 * 2 ** 20
VMEM_LIMIT_MIN = 24 * 2 ** 20


def _nbytes(shape, dtype):
    n = 1
    for s in shape:
        n *= s
    return n * jnp.dtype(dtype).itemsize


def _cparams(n_axes, block_bytes, scratch_bytes=0):
    est = 2 * (2 * block_bytes + scratch_bytes)
    lim = int(min(VMEM_LIMIT_MAX, max(VMEM_LIMIT_MIN, est)))
    return pltpu.CompilerParams(dimension_semantics=("arbitrary",) * n_axes, vmem_limit_bytes=lim)


def _bf(v):
    return v.astype(BF16)


def _dot(a, b):
    return jnp.dot(a, b, preferred_element_type=F32)


def _dot_nt(a, b):
    return lax.dot_general(a, b, (((1,), (1,)), ((), ())), preferred_element_type=F32)


def _dot_tn(a, b):
    return lax.dot_general(a, b, (((0,), (0,)), ((), ())), preferred_element_type=F32)


def _rstd(v):
    return lax.rsqrt(jnp.mean(v * v, axis=-1, keepdims=True) + EPS)


def _norm_bwd(dy, xv, g):
    r = _rstd(xv)
    xh = xv * r
    dxh = dy * g
    dx = r * (dxh - xh * jnp.mean(dxh * xh, axis=-1, keepdims=True))
    return dx, xh


def _place():
    x, y, c = lax.axis_index("x"), lax.axis_index("y"), lax.axis_index("c")
    other_chips = [(1 - x, y), (x, 1 - y), (1 - x, 1 - y)]
    return x, y, c, other_chips


class _Plan:
    ins = ()
    out_shapes = ()
    aliases = {}
    sem_shapes = ()

    def copies(self, ins, outs, sems):
        raise NotImplementedError

    def begin(self, ins, outs, sems):
        for cp in self.copies(ins, outs, sems):
            cp.start()

    def finish(self, ins, outs, sems):
        for cp in self.copies(ins, outs, sems):
            cp.wait()


class _GatherPlan(_Plan):
    def __init__(self, bufs):
        n = len(bufs)
        self.ins = list(bufs)
        self.out_shapes = [_sds(b.shape, b.dtype) for b in bufs]
        self.aliases = {i: i for i in range(n)}
        self.sem_shapes = [pltpu.SemaphoreType.DMA((n, 6)), pltpu.SemaphoreType.DMA((n, 6))]

    def _parts(self, outs, sems):
        send, recv = sems
        x, y, c, chips = _place()

        def half(i, piece, which):
            hr = outs[i].shape[1] // 2
            return outs[i].at[piece, pl.ds(which * hr, hr), :]

        def copy(i, k, ref, to):
            return pltpu.make_async_remote_copy(src_ref=ref, dst_ref=ref, send_sem=send.at[i, k], recv_sem=recv.at[i, k],
                                                device_id=to, device_id_type=MESH)

        return x, y, c, chips, half, copy

    def begin(self, ins, outs, sems):
        x, y, c, chips, half, copy = self._parts(outs, sems)
        for i in range(len(outs)):
            for j, (cx, cy) in enumerate(chips):
                copy(i, j, half(i, 2 * x + y, c), (cx, cy, c)).start()

    def finish(self, ins, outs, sems):
        x, y, c, chips, half, copy = self._parts(outs, sems)
        sib = (x, y, 1 - c)
        n = len(outs)
        for i in range(n):
            for j, (cx, cy) in enumerate(chips):
                piece = half(i, 2 * cx + cy, c)
                copy(i, j, piece, (cx, cy, c)).wait_recv()
                copy(i, 3 + j, piece, sib).start()
        for i in range(n):
            for j, (cx, cy) in enumerate(chips):
                copy(i, 3 + j, half(i, 2 * cx + cy, 1 - c), sib).wait_recv()
        for i in range(n):
            for j, (cx, cy) in enumerate(chips):
                copy(i, j, half(i, 2 * x + y, c), (cx, cy, c)).wait_send()
                copy(i, 3 + j, half(i, 2 * cx + cy, c), sib).wait_send()


class _SwapPlan(_Plan):
    def __init__(self, grads):
        n = len(grads)
        self.ins = list(grads)
        self.out_shapes = [_sds((g.shape[0], g.shape[1] // 2, g.shape[2]), g.dtype) for g in grads]
        self.sem_shapes = [pltpu.SemaphoreType.DMA((n,)), pltpu.SemaphoreType.DMA((n,))]

    def copies(self, ins, outs, sems):
        send, recv = sems
        x, y, c, _ = _place()
        cps = []
        for i in range(len(ins)):
            hr = ins[i].shape[1] // 2
            cps.append(pltpu.make_async_remote_copy(
                src_ref=ins[i].at[:, pl.ds((1 - c) * hr, hr), :], dst_ref=outs[i], send_sem=send.at[i],
                recv_sem=recv.at[i], device_id=(x, y, 1 - c), device_id_type=MESH))
        return cps


class _ExchangePlan(_Plan):
    def __init__(self, sums):
        n = len(sums)
        self.ins = list(sums)
        self.out_shapes = [_sds((N_CHIPS - 1,) + s.shape[1:], s.dtype) for s in sums]
        self.sem_shapes = [pltpu.SemaphoreType.DMA((n, 3)), pltpu.SemaphoreType.DMA((n, 3))]

    def copies(self, ins, outs, sems):
        send, recv = sems
        x, y, c, chips = _place()
        cps = []
        for i in range(len(ins)):
            for j, (cx, cy) in enumerate(chips):
                cps.append(pltpu.make_async_remote_copy(
                    src_ref=ins[i].at[2 * cx + cy], dst_ref=outs[i].at[j], send_sem=send.at[i, j],
                    recv_sem=recv.at[i, j], device_id=(cx, cy, c), device_id_type=MESH))
        return cps


class _JoinPlan(_Plan):
    def __init__(self, shards):
        n = len(shards)
        self.ins = list(shards)
        self.out_shapes = [_sds(s.shape, s.dtype) for s in shards]
        self.aliases = {i: i for i in range(n)}
        self.sem_shapes = [pltpu.SemaphoreType.DMA((n,)), pltpu.SemaphoreType.DMA((n,))]

    def copies(self, ins, outs, sems):
        send, recv = sems
        x, y, c, _ = _place()
        cps = []
        for i in range(len(outs)):
            hr = outs[i].shape[0] // 2
            mine = outs[i].at[pl.ds(c * hr, hr), :]
            cps.append(pltpu.make_async_remote_copy(src_ref=mine, dst_ref=mine, send_sem=send.at[i], recv_sem=recv.at[i],
                                                    device_id=(x, y, 1 - c), device_id_type=MESH))
        return cps


def _run_plan(name, plan):
    n_in, n_out = len(plan.ins), len(plan.out_shapes)

    def body(*refs):
        ins, outs, sems = refs[:n_in], refs[n_in:n_in + n_out], refs[n_in + n_out:]
        plan.begin(ins, outs, sems)
        plan.finish(ins, outs, sems)

    return pl.pallas_call(
        body, name=name, in_specs=[ANY] * n_in, out_specs=[ANY] * n_out, out_shape=list(plan.out_shapes),
        input_output_aliases=dict(plan.aliases), scratch_shapes=list(plan.sem_shapes))(*plan.ins)


HBM_PIN_BYTES = 1 << 20


def _sds(shape, dtype):
    if _nbytes(shape, dtype) >= HBM_PIN_BYTES:
        return pltpu.HBM(shape, dtype)
    return jax.ShapeDtypeStruct(shape, dtype)


def _pin(args):
    return [pltpu.with_memory_space_constraint(a, pltpu.HBM) if _nbytes(a.shape, a.dtype) >= HBM_PIN_BYTES else a
            for a in args]


def _pcall(body, *, name, grid, in_specs, out_specs, out_shape, params, args, scratch_shapes=(), plan=None):
    in_specs, out_specs, out_shape, scratch = list(in_specs), list(out_specs), list(out_shape), list(scratch_shapes)
    args = _pin(args)
    if plan is None:
        res = pl.pallas_call(body, name=name, grid=grid, in_specs=in_specs, out_specs=out_specs, out_shape=out_shape,
                             scratch_shapes=scratch, compiler_params=params)(*args)
        return list(res), []
    ni, no, ns = len(in_specs), len(out_specs), len(scratch)
    pi, po = len(plan.ins), len(plan.out_shapes)

    def wrapped(*refs):
        ins, pins = refs[:ni], refs[ni:ni + pi]
        outs, pouts = refs[ni + pi:ni + pi + no], refs[ni + pi + no:ni + pi + no + po]
        scr, psems = refs[ni + pi + no + po:ni + pi + no + po + ns], refs[ni + pi + no + po + ns:]
        ids = [pl.program_id(a) for a in range(len(grid))]
        first = functools.reduce(jnp.logical_and, [i == 0 for i in ids])
        last = functools.reduce(jnp.logical_and, [i == g - 1 for i, g in zip(ids, grid)])

        @pl.when(first)
        def _():
            plan.begin(pins, pouts, psems)

        body(*ins, *outs, *scr)

        @pl.when(last)
        def _():
            plan.finish(pins, pouts, psems)

    res = pl.pallas_call(
        wrapped, name=name, grid=grid, in_specs=in_specs + [ANY] * pi, out_specs=out_specs + [ANY] * po,
        out_shape=out_shape + list(plan.out_shapes), scratch_shapes=scratch + list(plan.sem_shapes),
        input_output_aliases={ni + a: no + b for a, b in plan.aliases.items()},
        compiler_params=params)(*args, *plan.ins)
    return list(res[:no]), list(res[no:])


def _mm_nn(name, a, w3, *, tm, norm_g=None, residual=None, out_dtype=F32, plan=None):
    M, K = a.shape
    P, Kw, C = w3.shape
    assert Kw == K and M % tm == 0
    N = P * C
    has_norm = norm_g is not None
    has_res = residual is not None

    def body(*refs):
        refs = list(refs)
        a_ref = refs.pop(0)
        w_ref = refs.pop(0)
        g_ref = refs.pop(0) if has_norm else None
        r_ref = refs.pop(0) if has_res else None
        o_ref = refs.pop(0)
        xn_ref = refs.pop(0) if has_norm else None
        if has_norm:
            av = a_ref[...].astype(F32)
            ab = _bf(av * _rstd(av) * g_ref[...])
            xn_ref[...] = ab
        else:
            ab = _bf(a_ref[...])
        for p in range(P):
            acc = _dot(ab, w_ref[p])
            if has_res:
                acc = acc + r_ref[:, p * C:(p + 1) * C]
            o_ref[:, p * C:(p + 1) * C] = acc.astype(o_ref.dtype)

    in_specs = [pl.BlockSpec((tm, K), lambda i: (i, 0)), pl.BlockSpec((P, K, C), lambda i: (0, 0, 0))]
    args = [a, w3]
    if has_norm:
        in_specs.append(pl.BlockSpec((1, K), lambda i: (0, 0)))
        args.append(norm_g)
    if has_res:
        in_specs.append(pl.BlockSpec((tm, N), lambda i: (i, 0)))
        args.append(residual)
    out_shape = [_sds((M, N), out_dtype)]
    out_specs = [pl.BlockSpec((tm, N), lambda i: (i, 0))]
    if has_norm:
        out_shape.append(_sds((M, K), BF16))
        out_specs.append(pl.BlockSpec((tm, K), lambda i: (i, 0)))
    blk = (_nbytes((tm, K), a.dtype) + _nbytes((P, K, C), BF16) + 2 * _nbytes((tm, N), F32)
           + _nbytes((tm, K), BF16))
    res, got = _pcall(body, name=name, grid=(M // tm,), in_specs=in_specs, out_specs=out_specs, out_shape=out_shape,
                      params=_cparams(1, blk), args=args, plan=plan)
    res = res if has_norm else res[0]
    return res if plan is None else (res, got)


MXU_COLS_V7X = 256


def _col_chunks(n):
    return [(c, min(c + MXU_COLS_V7X, n)) for c in range(0, n, MXU_COLS_V7X)]


def _ffn_fwd(name, h, wgu3, g, wd3, target, *, tm):
    M, K = h.shape
    P, _, C = wgu3.shape
    half = P // 2
    F = half * C

    def body(h_ref, w_ref, g_ref, wd_ref, t_ref, act_ref, fac_ref, xn_ref, dy_ref, ls_ref):
        hv = h_ref[...]
        ab = _bf(hv * _rstd(hv) * g_ref[...])
        xn_ref[...] = ab
        for j in range(half):
            for c0, c1 in _col_chunks(C):
                gate = _dot(ab, w_ref[j, :, c0:c1])
                up = _dot(ab, w_ref[half + j, :, c0:c1])
                sig = 1.0 / (1.0 + jnp.exp(-gate))
                silu = gate * sig
                cols = slice(j * C + c0, j * C + c1)
                act_ref[:, cols] = _bf(silu * up)
                fac_ref[0, :, cols] = _bf(up * (sig + silu * (1.0 - sig)))
                fac_ref[1, :, cols] = _bf(silu)
        err = _dot(act_ref[...], wd_ref[...]) + hv - t_ref[...]
        dy_ref[...] = err * (1.0 / K)

        @pl.when(pl.program_id(0) == 0)
        def _():
            ls_ref[...] = jnp.zeros_like(ls_ref)
        ls_ref[...] += jnp.sum(err * err, axis=0, keepdims=True)

    row = pl.BlockSpec((tm, K), lambda i: (i, 0))
    vec = pl.BlockSpec((1, K), lambda i: (0, 0))
    blk = (3 * _nbytes((tm, K), F32) + (_nbytes((P, K, C), BF16) + _nbytes((F, K), BF16)) // 2
           + 3 * _nbytes((tm, F), BF16) + _nbytes((tm, K), BF16))
    return pl.pallas_call(
        body, name=name, grid=(M // tm,),
        in_specs=[row, pl.BlockSpec((P, K, C), lambda i: (0, 0, 0), pipeline_mode=pl.Buffered(1)), vec,
                  pl.BlockSpec((None, F, K), lambda i: (0, 0, 0), pipeline_mode=pl.Buffered(1)), row],
        out_specs=[pl.BlockSpec((tm, F), lambda i: (i, 0)), pl.BlockSpec((2, tm, F), lambda i: (0, i, 0)),
                   row, row, vec],
        out_shape=[_sds((M, F), BF16), _sds((2, M, F), BF16), _sds((M, K), BF16), _sds((M, K), F32),
                   _sds((1, K), F32)],
        compiler_params=_cparams(1, blk))(*_pin([h, wgu3, g, wd3, target]))


def _nt_pieces(a_ref, w_ref):
    P, _, C = w_ref.shape
    acc = _dot_nt(_bf(a_ref[:, 0:C]), w_ref[0])
    for p in range(1, P):
        acc = acc + _dot_nt(_bf(a_ref[:, p * C:(p + 1) * C]), w_ref[p])
    return acc


def _mm_nt_normbwd(name, a, w3, h, g, dres, *, tm, plan=None):
    M, D = h.shape
    P, Ko, C = w3.shape
    N = a.shape[1]
    assert N == P * C and Ko == D and M % tm == 0
    has_res = dres is not None

    def body(*refs):
        a_ref, w_ref, h_ref, g_ref = refs[:4]
        r_ref = refs[4] if has_res else None
        dx_ref, dg_ref = refs[-2:]
        dhn = _nt_pieces(a_ref, w_ref)
        dx, xh = _norm_bwd(dhn, h_ref[...], g_ref[...])
        if has_res:
            dx = dx + r_ref[...]
        dx_ref[...] = dx

        @pl.when(pl.program_id(0) == 0)
        def _():
            dg_ref[...] = jnp.zeros_like(dg_ref)
        dg_ref[...] += jnp.sum(dhn * xh, axis=0, keepdims=True)

    row = pl.BlockSpec((tm, D), lambda i: (i, 0))
    vec = pl.BlockSpec((1, D), lambda i: (0, 0))
    in_specs = [pl.BlockSpec((tm, N), lambda i: (i, 0)), pl.BlockSpec((P, Ko, C), lambda i: (0, 0, 0)), row, vec]
    args = [a, w3, h, g]
    if has_res:
        in_specs.append(row)
        args.append(dres)
    blk = _nbytes((tm, N), a.dtype) + _nbytes((P, Ko, C), BF16) + 5 * _nbytes((tm, D), F32)
    res, got = _pcall(body, name=name, grid=(M // tm,), in_specs=in_specs, out_specs=[row, vec],
                      out_shape=[_sds((M, D), F32), _sds((1, D), F32)],
                      params=_cparams(1, blk), args=args, plan=plan)
    return res if plan is None else (res, got)


def _ffn_bwd(name, dy, wd3, fac, wgu3, h, g, *, tm):
    M, D = h.shape
    P, _, C = wgu3.shape
    F = fac.shape[2]
    per = F // C

    def body(dy_ref, wd_ref, f_ref, w_ref, h_ref, g_ref, dgu_ref, dx_ref, dg_ref):
        dyv = dy_ref[...]
        ab = _bf(dyv)
        dhn = None
        for pc in range(per):
            cols = slice(pc * C, (pc + 1) * C)
            da = _dot_nt(ab, wd_ref[cols, :])
            dgate = _bf(da * f_ref[0, :, cols].astype(F32))
            dup = _bf(da * f_ref[1, :, cols].astype(F32))
            dgu_ref[0, :, cols] = dgate
            dgu_ref[1, :, cols] = dup
            part = _dot_nt(dgate, w_ref[pc]) + _dot_nt(dup, w_ref[per + pc])
            dhn = part if dhn is None else dhn + part
        dx, xh = _norm_bwd(dhn, h_ref[...], g_ref[...])
        dx_ref[...] = dx + dyv

        @pl.when(pl.program_id(0) == 0)
        def _():
            dg_ref[...] = jnp.zeros_like(dg_ref)
        dg_ref[...] += jnp.sum(dhn * xh, axis=0, keepdims=True)

    row = pl.BlockSpec((tm, D), lambda i: (i, 0))
    vec = pl.BlockSpec((1, D), lambda i: (0, 0))
    planes = pl.BlockSpec((2, tm, F), lambda i: (0, i, 0))
    blk = (2 * _nbytes((2, tm, F), BF16) + (_nbytes((F, D), BF16) + _nbytes((P, D, C), BF16)) // 2
           + 4 * _nbytes((tm, D), F32))
    return pl.pallas_call(
        body, name=name, grid=(M // tm,),
        in_specs=[row, pl.BlockSpec((None, F, D), lambda i: (0, 0, 0), pipeline_mode=pl.Buffered(1)), planes,
                  pl.BlockSpec((P, D, C), lambda i: (0, 0, 0), pipeline_mode=pl.Buffered(1)), row, vec],
        out_specs=[planes, row, vec],
        out_shape=[_sds((2, M, F), BF16), _sds((M, D), F32), _sds((1, D), F32)],
        compiler_params=_cparams(1, blk))(*_pin([dy, wd3, fac, wgu3, h, g]))


def _mm_tn_wide(name, a, b, pieces, *, ts, plan=None):
    S, K = a.shape
    N = b.shape[1]
    C = N // pieces

    def body(a_ref, b_ref, o_ref):
        @pl.when(pl.program_id(0) == 0)
        def _():
            o_ref[...] = jnp.zeros_like(o_ref)
        acc = _dot_tn(_bf(a_ref[...]), _bf(b_ref[...]))
        for p in range(pieces):
            o_ref[p] += acc[:, p * C:(p + 1) * C]

    blk = _nbytes((ts, K), a.dtype) + _nbytes((ts, N), b.dtype) + 2 * _nbytes((K, N), F32)
    res, got = _pcall(body, name=name, grid=(S // ts,),
                      in_specs=[pl.BlockSpec((ts, K), lambda s: (s, 0)), pl.BlockSpec((ts, N), lambda s: (s, 0))],
                      out_specs=[pl.BlockSpec((pieces, K, C), lambda s: (0, 0, 0))],
                      out_shape=[_sds((pieces, K, C), F32)], params=_cparams(1, blk), args=[a, b], plan=plan)
    return res[0] if plan is None else (res[0], got)


def _mm_tn(name, a, a_spec, a_blk, b, b_spec, b_blk, out3, o_spec, o_blk, grid, plan=None):
    def body(a_ref, b_ref, o_ref):
        @pl.when(pl.program_id(2) == 0)
        def _():
            o_ref[...] = jnp.zeros_like(o_ref)
        o_ref[...] += _dot_tn(_bf(a_ref[...]), _bf(b_ref[...]))

    blk = _nbytes(a_blk, a.dtype) + _nbytes(b_blk, b.dtype) + 2 * _nbytes(o_blk, F32)
    res, got = _pcall(body, name=name, grid=grid, in_specs=[a_spec, b_spec], out_specs=[o_spec],
                      out_shape=[_sds(out3, F32)], params=_cparams(3, blk), args=[a, b], plan=plan)
    return res[0] if plan is None else (res[0], got)


def _swap_halves(v, lane, masked):
    up = pltpu.roll(v, HEAD - ROT_HALF, 1)
    down = pltpu.roll(v, ROT_HALF, 1)
    rest = jnp.where(lane < ROT_DIM, down, 0.0) if masked else down
    return jnp.where(lane < ROT_HALF, up, rest)


def _in_proj(name, x, w3, g, ctab, stab, qg, kg, *, tm, plan=None):
    S, K = x.shape
    P, _, C = w3.shape

    def body(x_ref, w_ref, g_ref, c_ref, s_ref, qg_ref, kg_ref, p_ref, xn_ref, q_ref, k_ref):
        xv = x_ref[...]
        ab = _bf(xv * _rstd(xv) * g_ref[...])
        xn_ref[...] = ab
        for p in range(P):
            p_ref[:, p * C:(p + 1) * C] = _dot(ab, w_ref[p])
        lane = lax.broadcasted_iota(jnp.int32, (tm, HEAD), 1)
        cv = c_ref[...]
        sv = s_ref[...]

        def prep(t, gain, scale):
            n = t * _rstd(t) * gain
            return (n * cv + _swap_halves(n, lane, False) * sv) * scale

        for j in range(Q_W // HEAD):
            q_ref[:, j * HEAD:(j + 1) * HEAD] = prep(p_ref[:, j * HEAD:(j + 1) * HEAD], qg_ref[...], SCALE)
        for j in range(KV_W // HEAD):
            k_ref[:, j * HEAD:(j + 1) * HEAD] = prep(p_ref[:, Q_W + j * HEAD:Q_W + (j + 1) * HEAD], kg_ref[...], 1.0)

    row = lambda width: pl.BlockSpec((tm, width), lambda i: (i, 0))
    vec = lambda width: pl.BlockSpec((1, width), lambda i: (0, 0))
    blk = (_nbytes((tm, K), F32) + _nbytes((P, K, C), BF16) + _nbytes((tm, P * C + Q_W + KV_W + 2 * HEAD), F32)
           + _nbytes((tm, K), BF16))
    res, got = _pcall(
        body, name=name, grid=(S // tm,),
        in_specs=[row(K), pl.BlockSpec((P, K, C), lambda i: (0, 0, 0)), vec(K), row(HEAD), row(HEAD), vec(HEAD), vec(HEAD)],
        out_specs=[row(P * C), row(K), row(Q_W), row(KV_W)],
        out_shape=[_sds((S, P * C), F32), _sds((S, K), BF16), _sds((S, Q_W), F32), _sds((S, KV_W), F32)],
        params=_cparams(1, blk), args=[x, w3, g, ctab, stab, qg, kg], plan=plan)
    return res if plan is None else (res, got)


ATTN_STEP = 2048


def _row_idx(start, dil):
    return pl.ds(start, BLOCK) if dil == 1 else pl.ds(start, BLOCK, stride=dil)


def _rows(ref, start, dil):
    return ref[_row_idx(start, dil), :]


def _set_rows(ref, start, dil, val):
    ref[_row_idx(start, dil), :] = val


def _band_mask(first):
    qi = lax.broadcasted_iota(jnp.int32, (BLOCK, 2 * BLOCK), 0)
    kj = lax.broadcasted_iota(jnp.int32, (BLOCK, 2 * BLOCK), 1)
    band = (kj >= qi) & (kj <= qi + BLOCK)
    if first is None:
        return band
    return band & ((kj >= BLOCK) | jnp.logical_not(first))


def _attn_geometry(S, grp):
    dil = DILATIONS[grp]
    bt = BLOCK * dil
    assert S % ATTN_STEP == 0 and ATTN_STEP % bt == 0
    return dil, bt, ATTN_STEP // bt, S // ATTN_STEP


MERGE_ROWS = 256


def _attn_fwd(name, qn, kn, proj, plan=None):
    S = qn.shape[0]
    groups = range(len(DILATIONS))
    geo = [_attn_geometry(S, g) for g in groups]
    nsb = geo[0][3]
    vcol = (Q_W + KV_W) // HEAD

    def body(q0, q1, q2, kc_ref, vc_ref, kp0, kp1, kp2, vp0, vp1, vp2, mix_ref, lse_ref, o_s, l_s):
        q_refs, kp_refs, vp_refs = (q0, q1, q2), (kp0, kp1, kp2), (vp0, vp1, vp2)
        valid_first = _band_mask(pl.program_id(1) == 0)
        valid_inner = _band_mask(None)
        for g in groups:
            dil, bt, nsub, _ = geo[g]
            o_g, l_g = o_s.at[g], l_s.at[g]
            for r in range(dil):
                kprev, vprev = _bf(_rows(kp_refs[g], r, dil)), _bf(_rows(vp_refs[g], r, dil))
                for b in range(nsub):
                    at = b * bt + r
                    kcur, vcur = _bf(_rows(kc_ref, at, dil)), _bf(_rows(vc_ref, at, dil))
                    q = _bf(_rows(q_refs[g], at, dil))
                    k2 = jnp.concatenate([kprev, kcur], axis=0)
                    v2 = jnp.concatenate([vprev, vcur], axis=0)
                    s = jnp.where(valid_first if b == 0 else valid_inner, _dot_nt(q, k2), NEG_INF)
                    m = jnp.max(s, axis=-1, keepdims=True)
                    p = jnp.exp(s - m)
                    l = jnp.sum(p, axis=-1, keepdims=True)
                    acc = _dot(_bf(p), v2)
                    _set_rows(o_g, at, dil, acc / l)
                    _set_rows(l_g, at, dil, jnp.broadcast_to(m + jnp.log(l), (BLOCK, HEAD)))
                    kprev, vprev = kcur, vcur
        for c in range(ATTN_STEP // MERGE_ROWS):
            rows = pl.ds(c * MERGE_ROWS, MERGE_ROWS)
            ls = [l_s[g, rows, :] for g in groups]
            m = jnp.maximum(jnp.maximum(ls[0], ls[1]), ls[2])
            ws = [jnp.exp(v - m) for v in ls]
            den = ws[0] + ws[1] + ws[2]
            num = ws[0] * o_s[0, rows, :] + ws[1] * o_s[1, rows, :] + ws[2] * o_s[2, rows, :]
            mix_ref[rows, :] = _bf(num / den)
            lse_ref[rows, :] = m + jnp.log(den)

    def big(col):
        return pl.BlockSpec((ATTN_STEP, HEAD), lambda h, n: (n, col + h))

    def tail(g, col):
        _, bt, nsub, _ = geo[g]
        return pl.BlockSpec((bt, HEAD), lambda h, n: (jnp.maximum(n * nsub - 1, 0), col + h))

    blk = 7 * _nbytes((ATTN_STEP, HEAD), F32) + 2 * sum(_nbytes((geo[g][1], HEAD), F32) for g in groups)
    scratch = 2 * len(DILATIONS) * _nbytes((ATTN_STEP, HEAD), F32)
    res, got = _pcall(
        body, name=name, grid=(N_HEADS, nsb),
        in_specs=[big(g * N_HEADS) for g in groups] + [big(0), big(vcol)]
                 + [tail(g, 0) for g in groups] + [tail(g, vcol) for g in groups],
        out_specs=[big(0), big(0)],
        out_shape=[_sds((S, 2 * KV_W), BF16), _sds((S, KV_W), F32)],
        scratch_shapes=[pltpu.VMEM((len(DILATIONS), ATTN_STEP, HEAD), F32)] * 2,
        params=_cparams(2, blk, scratch), args=[qn, qn, qn, kn, proj, kn, kn, kn, proj, proj, proj], plan=plan)
    return res if plan is None else (res, got)


STAT_LANES = HEAD // 2


def _attn_bwd(name, qn, kn, proj, dattn, stats, grp, plan=None):
    S = qn.shape[0]
    dil, bt, nsub, nsb = _attn_geometry(S, grp)
    vcol = (Q_W + KV_W) // HEAD
    out_dtype = BF16 if dil == 1 else F32

    def slot(b, r):
        return pl.ds((b * dil + r) * BLOCK, BLOCK)

    def body(q_ref, kp_ref, kc_ref, vp_ref, vc_ref, do_ref, st_ref, dq_ref, dk_ref, dv_ref, ck_ref, cv_ref):
        n = pl.program_id(1)
        par = n % 2

        @pl.when(n < nsb)
        def _():
            valid_first = _band_mask(n == 0)
            valid_inner = _band_mask(None)
            ck, cv = ck_ref.at[par], cv_ref.at[par]
            pk, pv = ck_ref.at[1 - par], cv_ref.at[1 - par]
            for r in range(dil):
                kprev, vprev = _bf(_rows(kp_ref, r, dil)), _bf(_rows(vp_ref, r, dil))
                own_k = own_v = None
                for b in range(nsub):
                    at = b * bt + r
                    kcur, vcur = _bf(_rows(kc_ref, at, dil)), _bf(_rows(vc_ref, at, dil))
                    q = _bf(_rows(q_ref, at, dil))
                    k2 = jnp.concatenate([kprev, kcur], axis=0)
                    v2 = jnp.concatenate([vprev, vcur], axis=0)
                    do = _bf(_rows(do_ref, at, dil))
                    stat = _rows(st_ref, at, dil)
                    lse_r = stat[:, :1]
                    del_r = stat[:, STAT_LANES:STAT_LANES + 1]
                    s = jnp.where(valid_first if b == 0 else valid_inner, _dot_nt(q, k2), NEG_INF)
                    p = jnp.exp(s - lse_r)
                    ds = _bf(p * (_dot_nt(do, v2) - del_r))
                    _set_rows(dq_ref, at, dil, _dot(ds, k2).astype(dq_ref.dtype))
                    dk2 = _dot_tn(ds, q)
                    dv2 = _dot_tn(_bf(p), do)
                    if b > 0:
                        ck[slot(b - 1, r), :] = own_k + dk2[:BLOCK]
                        cv[slot(b - 1, r), :] = own_v + dv2[:BLOCK]
                    else:
                        @pl.when(n > 0)
                        def _():
                            pk[slot(nsub - 1, r), :] += dk2[:BLOCK]
                            pv[slot(nsub - 1, r), :] += dv2[:BLOCK]
                    own_k, own_v = dk2[BLOCK:], dv2[BLOCK:]
                    kprev, vprev = kcur, vcur
                ck[slot(nsub - 1, r), :] = own_k
                cv[slot(nsub - 1, r), :] = own_v

        @pl.when(n > 0)
        def _():
            for r in range(dil):
                for b in range(nsub):
                    _set_rows(dk_ref, b * bt + r, dil, ck_ref[1 - par, slot(b, r), :].astype(dk_ref.dtype))
                    _set_rows(dv_ref, b * bt + r, dil, cv_ref[1 - par, slot(b, r), :].astype(dv_ref.dtype))

    cur = lambda n: jnp.minimum(n, nsb - 1)
    prev = lambda n: jnp.maximum(n - 1, 0)
    tail_at = lambda n: jnp.maximum(cur(n) * nsub - 1, 0)
    big = lambda col: pl.BlockSpec((ATTN_STEP, HEAD), lambda h, n: (cur(n), col(h)))
    tail = lambda col: pl.BlockSpec((bt, HEAD), lambda h, n: (tail_at(n), col(h)))
    late = pl.BlockSpec((ATTN_STEP, HEAD), lambda h, n: (prev(n), h))
    blk = 8 * _nbytes((ATTN_STEP, HEAD), F32) + 2 * _nbytes((bt, HEAD), F32)
    res, got = _pcall(
        body, name=name, grid=(N_HEADS, nsb + 1),
        in_specs=[big(lambda h: grp * N_HEADS + h), tail(lambda h: h), big(lambda h: h),
                  tail(lambda h: vcol + h), big(lambda h: vcol + h),
                  big(lambda h: h), big(lambda h: h)],
        out_specs=[big(lambda h: h), late, late],
        out_shape=[_sds((S, KV_W), out_dtype)] * 3,
        scratch_shapes=[pltpu.VMEM((2, ATTN_STEP, HEAD), F32), pltpu.VMEM((2, ATTN_STEP, HEAD), F32)],
        params=_cparams(2, blk, 4 * _nbytes((ATTN_STEP, HEAD), F32)),
        args=[qn, kn, kn, proj, proj, dattn, stats], plan=plan)
    return res if plan is None else (res, got)


def _window_sum(v, n_doublings, back):
    rows = v.shape[0]
    step = 1
    for _ in range(n_doublings):
        v = v + pltpu.roll(v, step if back else rows - step, 0)
        step *= 2
    return v


def _pool_out_proj(name, mix, proj, pool_w, pool_scale, w_out3, x, *, tm):
    S, D = x.shape
    ucol = (IN_W - POOL_W) // POOL_W
    hpt = tm // POOL_HALO

    def body(at_ref, u_ref, uh_ref, pw_ref, ps_ref, w_ref, x_ref, pooled_ref, dp_ref, h_ref):
        i = pl.program_id(0)
        halo = jnp.where(i == 0, 0.0, uh_ref[...])
        t = lax.broadcasted_iota(jnp.int32, (tm + POOL_HALO, HEAD), 0) + (i * tm - POOL_HALO)
        for g, w in enumerate(POOL_WINDOWS):
            sl = slice(g * HEAD, (g + 1) * HEAD)
            ub = jnp.concatenate([halo[:, sl], u_ref[:, sl]], axis=0)
            cnt = jnp.minimum(t + 1, w).astype(F32)
            d = (_window_sum(ub, g + 1, True) / cnt - ub)[POOL_HALO:]
            db = _bf(d)
            dp_ref[:, sl] = db
            pooled_ref[:, sl] = _bf(_dot(db, pw_ref[g]) * ps_ref[:, sl])
        h_ref[...] = x_ref[...] + _dot(at_ref[...], w_ref[:KV_W, :]) + _dot(pooled_ref[...], w_ref[KV_W:, :])

    row = pl.BlockSpec((tm, D), lambda i: (i, 0))
    blk = (3 * _nbytes((tm, POOL_W), F32) + 3 * _nbytes((tm, POOL_W), BF16) + _nbytes((2 * KV_W, D), BF16)
           + 3 * _nbytes((tm, D), F32))
    return pl.pallas_call(
        body, name=name, grid=(S // tm,),
        in_specs=[pl.BlockSpec((tm, KV_W), lambda i: (i, 0)),
                  pl.BlockSpec((tm, POOL_W), lambda i: (i, ucol)),
                  pl.BlockSpec((POOL_HALO, POOL_W), lambda i: (jnp.maximum(i * hpt - 1, 0), ucol)),
                  pl.BlockSpec((len(POOL_WINDOWS), HEAD, HEAD), lambda i: (0, 0, 0)),
                  pl.BlockSpec((1, POOL_W), lambda i: (0, 0)),
                  pl.BlockSpec((None, 2 * KV_W, D), lambda i: (0, 0, 0)), row],
        out_specs=[pl.BlockSpec((tm, POOL_W), lambda i: (i, 1)), pl.BlockSpec((tm, POOL_W), lambda i: (i, 0)), row],
        out_shape=[_sds((S, 2 * KV_W), BF16), _sds((S, POOL_W), BF16), _sds((S, D), F32)],
        input_output_aliases={0: 0},
        compiler_params=_cparams(1, blk))(*_pin([mix, proj, proj, pool_w, pool_scale, w_out3, x]))


def _mix_bwd(name, dh, w_out3, mix, lse, dpool, pool_w, pool_scale, *, tm, plan=None):
    S, D = dh.shape
    hpt = tm // POOL_HALO
    last_halo = S // POOL_HALO - 1
    n_tiles = S // tm

    def body(dh_ref, dhn_ref, w_ref, at_ref, l_ref, dp_ref, pw_ref, ps_ref, da_ref, st_ref, du_ref, gw_ref, gs_ref):
        i = pl.program_id(0)

        @pl.when(i == 0)
        def _():
            gw_ref[...] = jnp.zeros_like(gw_ref)
            gs_ref[...] = jnp.zeros_like(gs_ref)

        dm = _dot_nt(_bf(dh_ref[...]), w_ref[...])
        ahead = _dot_nt(_bf(dhn_ref[...]), w_ref[KV_W:, :])
        lane = lax.broadcasted_iota(jnp.int32, (tm, HEAD), 1)
        for h in range(N_HEADS):
            sl = slice(h * HEAD, (h + 1) * HEAD)
            da = dm[:, sl]
            da_ref[:, sl] = da
            delta = jnp.sum(da * at_ref[:, sl].astype(F32), axis=-1, keepdims=True)
            st_ref[:, sl] = jnp.where(lane < STAT_LANES, l_ref[:, sl], delta)
        halo = jnp.where(i == n_tiles - 1, 0.0, ahead)
        t = lax.broadcasted_iota(jnp.int32, (tm + POOL_HALO, HEAD), 0) + i * tm
        for g, w in enumerate(POOL_WINDOWS):
            sl = slice(g * HEAD, (g + 1) * HEAD)
            dy = jnp.concatenate([dm[:, KV_W + g * HEAD:KV_W + (g + 1) * HEAD], halo[:, sl]], axis=0)
            dys = _bf(dy * ps_ref[:, sl])
            dd = _dot_nt(dys, pw_ref[g])
            cnt = jnp.minimum(t + 1, w).astype(F32)
            du_ref[:, sl] = (_window_sum(dd / cnt, g + 1, False) - dd)[:tm]
            db = dp_ref[:, sl]
            gw_ref[g] += _dot_tn(db, dys[:tm])
            gs_ref[:, sl] += jnp.sum(dy[:tm] * _dot(db, pw_ref[g]), axis=0, keepdims=True)

    tile = pl.BlockSpec((tm, KV_W), lambda i: (i, 0))
    blk = 2 * _nbytes((tm, D), F32) + _nbytes((2 * KV_W, D), BF16) + 6 * _nbytes((tm, KV_W), F32)
    res, got = _pcall(
        body, name=name, grid=(n_tiles,),
        in_specs=[pl.BlockSpec((tm, D), lambda i: (i, 0)),
                  pl.BlockSpec((POOL_HALO, D), lambda i: (jnp.minimum((i + 1) * hpt, last_halo), 0)),
                  pl.BlockSpec((None, 2 * KV_W, D), lambda i: (0, 0, 0)),
                  tile, tile, tile,
                  pl.BlockSpec((len(POOL_WINDOWS), HEAD, HEAD), lambda i: (0, 0, 0)),
                  pl.BlockSpec((1, POOL_W), lambda i: (0, 0))],
        out_specs=[tile, tile, tile,
                   pl.BlockSpec((len(POOL_WINDOWS), HEAD, HEAD), lambda i: (0, 0, 0)),
                   pl.BlockSpec((1, POOL_W), lambda i: (0, 0))],
        out_shape=[_sds((S, KV_W), F32)] * 3 + [
            _sds((len(POOL_WINDOWS), HEAD, HEAD), F32), _sds((1, POOL_W), F32)],
        params=_cparams(1, blk), args=[dh, dh, w_out3, mix, lse, dpool, pool_w, pool_scale], plan=plan)
    return res if plan is None else (res, got)


def _qkv_bwd(name, dqs, dks, dvs, du, proj, ctab, stab, qg, kg, *, tm, plan=None):
    S = proj.shape[0]
    width = Q_W + KV_W

    def body(dq0, dq1, dq2, dk0, dk1, dk2, dv0, dv1, dv2, du_ref, p_ref, c_ref, s_ref, qg_ref, kg_ref,
             dp_ref, gq_ref, gk_ref):
        @pl.when(pl.program_id(0) == 0)
        def _():
            gq_ref[...] = jnp.zeros_like(gq_ref)
            gk_ref[...] = jnp.zeros_like(gk_ref)

        lane = lax.broadcasted_iota(jnp.int32, (tm, HEAD), 1)
        cv = c_ref[...]
        sv = s_ref[...]

        def back(dy, t, g, scale):
            dy = dy * scale
            dn = dy * cv + _swap_halves(dy * sv, lane, True)
            dt, xh = _norm_bwd(dn, t, g)
            return dt, jnp.sum(dn * xh, axis=0, keepdims=True)

        dqr = (dq0, dq1, dq2)
        gq = jnp.zeros((1, HEAD), F32)
        for j in range(Q_W // HEAD):
            grp, h = divmod(j, N_HEADS)
            dt, gj = back(dqr[grp][:, h * HEAD:(h + 1) * HEAD].astype(F32), p_ref[:, j * HEAD:(j + 1) * HEAD],
                          qg_ref[...], SCALE)
            dp_ref[:, j * HEAD:(j + 1) * HEAD] = _bf(dt)
            gq = gq + gj
        gq_ref[...] += gq
        gk = jnp.zeros((1, HEAD), F32)
        for h in range(N_HEADS):
            sl = slice(h * HEAD, (h + 1) * HEAD)
            dt, gj = back(dk0[:, sl] + dk1[:, sl] + dk2[:, sl], p_ref[:, Q_W + h * HEAD:Q_W + (h + 1) * HEAD],
                          kg_ref[...], 1.0)
            dp_ref[:, Q_W + h * HEAD:Q_W + (h + 1) * HEAD] = _bf(dt)
            gk = gk + gj
        gk_ref[...] += gk
        dp_ref[:, width:width + KV_W] = _bf(dv0[...] + dv1[...] + dv2[...])
        dp_ref[:, width + KV_W:] = _bf(du_ref[...])

    tile = pl.BlockSpec((tm, KV_W), lambda i: (i, 0))
    vec = pl.BlockSpec((1, HEAD), lambda i: (0, 0))
    rot = pl.BlockSpec((tm, HEAD), lambda i: (i, 0))
    blk = 10 * _nbytes((tm, KV_W), F32) + _nbytes((tm, width), F32) + _nbytes((tm, IN_W), BF16)
    res, got = _pcall(
        body, name=name, grid=(S // tm,),
        in_specs=[tile] * 10 + [pl.BlockSpec((tm, width), lambda i: (i, 0)), rot, rot, vec, vec],
        out_specs=[pl.BlockSpec((tm, IN_W), lambda i: (i, 0)), vec, vec],
        out_shape=[_sds((S, IN_W), BF16), _sds((1, HEAD), F32),
                   _sds((1, HEAD), F32)],
        params=_cparams(1, blk), args=[*dqs, *dks, *dvs, du, proj, ctab, stab, qg, kg], plan=plan)
    return res if plan is None else (res, got)


def _cross_heads(q_ref, kv_ref, qg, kg, h):
    sl = slice(h * HEAD, (h + 1) * HEAD)
    qr = q_ref[:, sl]
    kr = kv_ref[:, sl]
    qh = qr * _rstd(qr) * qg * SCALE
    kh = kr * _rstd(kr) * kg
    vh = kv_ref[:, X_W + h * HEAD:X_W + (h + 1) * HEAD]
    return qr, _bf(qh), _bf(kh), _bf(vh)


def _cross_fwd(name, h_in, g_in, w_cq3, kv, qg, kg, w_co3, *, tm):
    S, D = h_in.shape
    M = kv.shape[0]
    P, _, C = w_co3.shape

    def body(h_ref, g_ref, wcq_ref, kv_ref, qg_ref, kg_ref, wco_ref, q_ref, hn_ref, o_ref, ho_ref):
        hv = h_ref[...]
        ab = _bf(hv * _rstd(hv) * g_ref[...])
        hn_ref[...] = ab
        q_ref[...] = _dot(ab, wcq_ref[...])
        for h in range(N_HEADS):
            _, qh, kh, vh = _cross_heads(q_ref, kv_ref, qg_ref[...], kg_ref[...], h)
            s = _dot_nt(qh, kh)
            p = jnp.exp(s - jnp.max(s, axis=-1, keepdims=True))
            l = jnp.sum(p, axis=-1, keepdims=True)
            o_ref[:, h * HEAD:(h + 1) * HEAD] = _bf(_dot(_bf(p), vh) / l)
        ob = o_ref[...]
        for p in range(P):
            ho_ref[:, p * C:(p + 1) * C] = hv[:, p * C:(p + 1) * C] + _dot(ob, wco_ref[p])

    vec = pl.BlockSpec((1, HEAD), lambda i: (0, 0))
    row = pl.BlockSpec((tm, D), lambda i: (i, 0))
    qtile = pl.BlockSpec((tm, X_W), lambda i: (i, 0))
    blk = (3 * _nbytes((tm, D), F32) + 2 * _nbytes((tm, X_W), F32) + _nbytes((M, 2 * X_W), F32)
           + 4 * _nbytes((tm, M), F32) + _nbytes((D, X_W), BF16) + _nbytes(w_co3.shape, BF16))
    return pl.pallas_call(
        body, name=name, grid=(S // tm,),
        in_specs=[row, pl.BlockSpec((1, D), lambda i: (0, 0)), pl.BlockSpec((None, D, X_W), lambda i: (0, 0, 0)),
                  pl.BlockSpec((M, 2 * X_W), lambda i: (0, 0)), vec, vec,
                  pl.BlockSpec(w_co3.shape, lambda i: (0, 0, 0))],
        out_specs=[qtile, row, qtile, row],
        out_shape=[_sds((S, X_W), F32), _sds((S, D), BF16), _sds((S, X_W), BF16), _sds((S, D), F32)],
        compiler_params=_cparams(1, blk))(*_pin([h_in, g_in, w_cq3, kv, qg, kg, w_co3]))


def _cross_bwd(name, dh, w_co3, qraw, kv, qg, kg, w_cq3, h_in, g_in, *, tm, plan=None):
    S, D = dh.shape
    M = kv.shape[0]

    def body(dh_ref, wco_ref, q_ref, kv_ref, qg_ref, kg_ref, wcq_ref, h_ref, g_ref,
             dq_ref, dk_ref, dv_ref, gq_ref, dx_ref, gn_ref):
        @pl.when(pl.program_id(0) == 0)
        def _():
            dk_ref[...] = jnp.zeros_like(dk_ref)
            dv_ref[...] = jnp.zeros_like(dv_ref)
            gq_ref[...] = jnp.zeros_like(gq_ref)
            gn_ref[...] = jnp.zeros_like(gn_ref)

        dhv = dh_ref[...]
        do_all = _nt_pieces(dh_ref, wco_ref)
        gq = jnp.zeros((1, HEAD), F32)
        for h in range(N_HEADS):
            sl = slice(h * HEAD, (h + 1) * HEAD)
            qr, qh, kh, vh = _cross_heads(q_ref, kv_ref, qg_ref[...], kg_ref[...], h)
            doh = _bf(do_all[:, sl])
            s = _dot_nt(qh, kh)
            p = jnp.exp(s - jnp.max(s, axis=-1, keepdims=True))
            p = p / jnp.sum(p, axis=-1, keepdims=True)
            pb = _bf(p)
            dp = _dot_nt(doh, vh)
            ds = _bf(p * (dp - jnp.sum(dp * p, axis=-1, keepdims=True)))
            dv_ref[:, sl] += _dot_tn(pb, doh)
            dk_ref[:, sl] += _dot_tn(ds, qh)
            dn = _dot(ds, kh) * SCALE
            dt, xh = _norm_bwd(dn, qr, qg_ref[...])
            dq_ref[:, sl] = _bf(dt)
            gq = gq + jnp.sum(dn * xh, axis=0, keepdims=True)
        gq_ref[...] += gq
        dhn = _dot_nt(dq_ref[...], wcq_ref[...])
        dx, xh = _norm_bwd(dhn, h_ref[...], g_ref[...])
        dx_ref[...] = dx + dhv
        gn_ref[...] += jnp.sum(dhn * xh, axis=0, keepdims=True)

    vec = pl.BlockSpec((1, HEAD), lambda i: (0, 0))
    acc = pl.BlockSpec((M, X_W), lambda i: (0, 0))
    row = pl.BlockSpec((tm, D), lambda i: (i, 0))
    wide = pl.BlockSpec((1, D), lambda i: (0, 0))
    qtile = pl.BlockSpec((tm, X_W), lambda i: (i, 0))
    blk = (3 * _nbytes((tm, X_W), F32) + 3 * _nbytes((M, 2 * X_W), F32) + 6 * _nbytes((tm, M), F32)
           + 4 * _nbytes((tm, D), F32) + _nbytes(w_co3.shape, BF16) + _nbytes((D, X_W), BF16))
    res, got = _pcall(
        body, name=name, grid=(S // tm,),
        in_specs=[row, pl.BlockSpec(w_co3.shape, lambda i: (0, 0, 0)), qtile,
                  pl.BlockSpec((M, 2 * X_W), lambda i: (0, 0)), vec, vec,
                  pl.BlockSpec((None, D, X_W), lambda i: (0, 0, 0)), row, wide],
        out_specs=[qtile, acc, acc, vec, row, wide],
        out_shape=[_sds((S, X_W), BF16), _sds((M, X_W), F32), _sds((M, X_W), F32), _sds((1, HEAD), F32),
                   _sds((S, D), F32), _sds((1, D), F32)],
        params=_cparams(1, blk), args=[dh, w_co3, qraw, kv, qg, kg, w_cq3, h_in, g_in], plan=plan)
    return res if plan is None else (res, got)


def _cross_kv_bwd(name, dkn, dv, kv, kg):
    M = kv.shape[0]

    def body(dk_ref, dv_ref, kv_ref, kg_ref, o_ref, g_ref):
        gk = jnp.zeros((1, HEAD), F32)
        for h in range(N_HEADS):
            sl = slice(h * HEAD, (h + 1) * HEAD)
            dn = dk_ref[:, sl]
            dt, xh = _norm_bwd(dn, kv_ref[:, sl], kg_ref[...])
            o_ref[:, sl] = _bf(dt)
            gk = gk + jnp.sum(dn * xh, axis=0, keepdims=True)
        o_ref[:, X_W:] = _bf(dv_ref[...])
        g_ref[...] = gk

    full = lambda shape: pl.BlockSpec(shape, lambda i: (0,) * len(shape))
    return pl.pallas_call(
        body, name=name, grid=(1,),
        in_specs=[full((M, X_W)), full((M, X_W)), full((M, 2 * X_W)), full((1, HEAD))],
        out_specs=[full((M, 2 * X_W)), full((1, HEAD))],
        out_shape=[_sds((M, 2 * X_W), BF16), _sds((1, HEAD), F32)],
        compiler_params=_cparams(1, 6 * _nbytes((M, 2 * X_W), F32)))(dkn, dv, kv, kg)


def _rope_tables(positions):
    inv_freq = ROPE_THETA ** (-jnp.arange(0, ROT_DIM, 2, dtype=F32) / ROT_DIM)
    ang = positions.astype(F32)[:, None] * inv_freq
    cos, sin = jnp.cos(ang), jnp.sin(ang)
    S = positions.shape[0]
    ctab = jnp.concatenate([cos, cos, jnp.ones((S, HEAD - ROT_DIM), F32)], axis=-1)
    stab = jnp.concatenate([-sin, sin, jnp.zeros((S, HEAD - ROT_DIM), F32)], axis=-1)
    return ctab, stab


GATHER_BEHIND_IN_PROJ = ("w_out", "w_cq", "w_ckv", "w_co")
FFN_WEIGHTS = ("w_gate_up", "w_down")


def _hosted(fn, *args, plan=None, **kw):
    if plan is None:
        return fn(*args, **kw), []
    return fn(*args, plan=plan, **kw)


def _local_step(x, mem, positions, target, wb, sm, *, tm=512, place=None):
    S, D = x.shape
    M = mem.shape[0]
    dist = place is not None
    wb = dict(wb)
    ctab, stab = _rope_tables(positions)
    pool_w_b = _bf(sm["pool_w"])

    def gather(names):
        return _GatherPlan([wb[k] for k in names]) if dist else None

    if dist:
        wb["w_in"], = _run_plan("gather_w_in", gather(["w_in"]))
    w_in = wb["w_in"]
    behind_in_proj = GATHER_BEHIND_IN_PROJ + ("w_gate_up",)
    (proj, xn1, qn, kn), got = _hosted(_in_proj, "in_proj", x, w_in, sm["mix_norm_g"], ctab, stab,
                                       sm["q_norm_g"], sm["k_norm_g"], tm=tm, plan=gather(behind_in_proj))
    wb.update(zip(behind_in_proj, got))
    (mix, lse), got = _hosted(_attn_fwd, "attn_fwd", qn, kn, proj, plan=gather(["w_down"]))
    wb.update(zip(["w_down"], got))
    w_out = wb["w_out"].reshape(1, 2 * KV_W, D)
    w_cq = wb["w_cq"].reshape(1, D, X_W)
    w_ckv = wb["w_ckv"].reshape(1, D, 2 * X_W)
    w_co = wb["w_co"]
    w_gu = wb["w_gate_up"]
    w_down = wb["w_down"].reshape(1, D_FF, D)
    cin = w_in.shape[2]
    cco = w_co.shape[2]
    cgu = w_gu.shape[2]
    mix, dpool, h1 = _pool_out_proj("pool_out_proj", mix, proj, pool_w_b, sm["pool_scale"], w_out, x, tm=tm)
    kv, mem_n = _mm_nn("ckv_proj", mem, w_ckv, tm=M, norm_g=sm["mem_norm_g"])
    cq_raw, hn2, xo, h2 = _cross_fwd("cross_fwd", h1, sm["cross_norm_g"], w_cq, kv, sm["cq_norm_g"],
                                     sm["ck_norm_g"], w_co, tm=tm)
    act, gu, hn3, dy, lsum = _ffn_fwd("ffn_fwd", h2, w_gu, sm["ffn_norm_g"], w_down, target, tm=tm // 2)
    loss = 0.5 * jnp.sum(lsum) / D

    ts = 2 * tm
    nS = S // ts
    g_down = _mm_tn("g_w_down", act, pl.BlockSpec((ts, cgu), lambda r, c, s: (s, r)), (ts, cgu),
                    dy, pl.BlockSpec((ts, D), lambda r, c, s: (s, 0)), (ts, D),
                    (1, D_FF, D), pl.BlockSpec((None, cgu, D), lambda r, c, s: (0, r, 0)), (cgu, D),
                    (D_FF // cgu, 1, nS))
    dgu, dh2, g_ffn_norm = _ffn_bwd("ffn_bwd", dy, w_down, gu, w_gu, h2, sm["ffn_norm_g"], tm=tm // 2)
    g_gu = _mm_tn("g_w_gate_up", hn3, pl.BlockSpec((ts, D), lambda r, c, s: (s, 0)), (ts, D),
                  dgu, pl.BlockSpec((None, ts, cgu), lambda r, c, s: (c // 2, s, c % 2)), (ts, cgu),
                  (4, D, cgu), pl.BlockSpec((None, D, cgu), lambda r, c, s: (c, 0, 0)), (D, cgu),
                  (1, 4, nS))

    g_co = _mm_tn_wide("g_w_co", xo, dh2, N_CHIPS, ts=ts)
    full = {"w_gate_up": g_gu, "w_down": g_down.reshape(N_CHIPS, D_FF // N_CHIPS, D)}
    sums = {}

    def swap(names):
        return _SwapPlan([full[k] for k in names]) if dist else None

    def add_halves(names, from_sibling):
        for k, t in zip(names, from_sibling):
            sums[k] = _add_core_halves(f"add_halves_{k}", full[k], t, place[1])

    def exchange(names):
        return _ExchangePlan([sums[k][1] for k in names]) if dist else None

    def sum_chips(names, from_chips):
        return [_sum_chips(f"sum_chips_{k}", sums[k][0], t, place[1]) for k, t in zip(names, from_chips)]

    (dcq, dkn, dvm, g_cq_norm, dh1, g_cross_norm), got = _hosted(
        _cross_bwd, "cross_bwd", dh2, w_co, cq_raw, kv, sm["cq_norm_g"], sm["ck_norm_g"], w_cq, h1,
        sm["cross_norm_g"], tm=tm, plan=swap(FFN_WEIGHTS))
    add_halves(FFN_WEIGHTS, got)
    dkv, g_ck_norm = _cross_kv_bwd("cross_kv_bwd", dkn, dvm, kv, sm["ck_norm_g"])
    g_cq = _mm_tn("g_w_cq", hn2, pl.BlockSpec((ts, D), lambda r, c, s: (s, 0)), (ts, D),
                  dcq, pl.BlockSpec((ts, X_W), lambda r, c, s: (s, 0)), (ts, X_W),
                  (1, D, X_W), pl.BlockSpec((None, D, X_W), lambda r, c, s: (0, 0, 0)), (D, X_W), (1, 1, nS))
    _, g_mem_norm = _mm_nt_normbwd("ckv_proj_bwd", dkv, w_ckv, mem, sm["mem_norm_g"], None, tm=M)
    g_ckv = _mm_tn("g_w_ckv", mem_n, pl.BlockSpec((M, D), lambda r, c, s: (0, 0)), (M, D),
                   dkv, pl.BlockSpec((M, 2 * X_W), lambda r, c, s: (0, 0)), (M, 2 * X_W),
                   (1, D, 2 * X_W), pl.BlockSpec((None, D, 2 * X_W), lambda r, c, s: (0, 0, 0)), (D, 2 * X_W),
                   (1, 1, 1))

    g_out = _mm_tn("g_w_out", mix, pl.BlockSpec((ts, 2 * KV_W), lambda r, c, s: (s, 0)), (ts, 2 * KV_W),
                   dh1, pl.BlockSpec((ts, D), lambda r, c, s: (s, 0)), (ts, D),
                   (1, 2 * KV_W, D), pl.BlockSpec((None, 2 * KV_W, D), lambda r, c, s: (0, 0, 0)), (2 * KV_W, D),
                   (1, 1, nS))
    full.update({
        "w_out": g_out.reshape(N_CHIPS, 2 * KV_W // N_CHIPS, D),
        "w_cq": g_cq.reshape(N_CHIPS, D // N_CHIPS, X_W),
        "w_ckv": g_ckv.reshape(N_CHIPS, D // N_CHIPS, 2 * X_W),
        "w_co": g_co,
    })
    mixer = GATHER_BEHIND_IN_PROJ
    (dattn, stats, du, g_pool_w, g_pool_scale), got = _hosted(
        _mix_bwd, "mix_bwd", dh1, w_out, mix, lse, dpool, pool_w_b, sm["pool_scale"], tm=tm, plan=swap(mixer))
    add_halves(mixer, got)
    behind_attn = (None, mixer, FFN_WEIGHTS)
    halves = {}
    dqs, dks, dvs = [], [], []
    for grp in range(len(DILATIONS)):
        names = behind_attn[grp]
        (dq, dk, dv), got = _hosted(_attn_bwd, f"attn_bwd{grp}", qn, kn, proj, dattn, stats, grp,
                                    plan=exchange(names) if names else None)
        if dist and names:
            halves.update(zip(names, sum_chips(names, got)))
        dqs.append(dq)
        dks.append(dk)
        dvs.append(dv)
    joined = mixer + FFN_WEIGHTS
    (dproj, g_q_norm, g_k_norm), got = _hosted(
        _qkv_bwd, "qkv_bwd", dqs, dks, dvs, du, proj, ctab, stab, sm["q_norm_g"], sm["k_norm_g"], tm=tm,
        plan=_JoinPlan([halves[k] for k in joined]) if dist else None)
    shards = dict(zip(joined, got))
    dx, g_mix_norm = _mm_nt_normbwd("in_proj_bwd", dproj, w_in, x, sm["mix_norm_g"], dh1, tm=tm)
    small = {
        "mix_norm_g": g_mix_norm, "q_norm_g": g_q_norm, "k_norm_g": g_k_norm, "pool_w": g_pool_w,
        "pool_scale": g_pool_scale, "cross_norm_g": g_cross_norm, "mem_norm_g": g_mem_norm,
        "cq_norm_g": g_cq_norm, "ck_norm_g": g_ck_norm, "ffn_norm_g": g_ffn_norm,
    }
    full["w_in"], got = _hosted(_mm_tn_wide, "g_w_in", xn1, dproj, N_CHIPS, ts=tm,
                                plan=_AllPushPlan(_pack_small(small)) if dist else None)
    if dist:
        small = _unpack_small(_sum_slots("sum_small", got[0]), sm)
        add_halves(["w_in"], _run_plan("swap_w_in", swap(["w_in"])))
        half, = sum_chips(["w_in"], _run_plan("exchange_w_in", exchange(["w_in"])))
        shards["w_in"], = _run_plan("join_w_in", _JoinPlan([half]))
    big = shards if dist else full
    return loss, dx, big, small


BIG = ("w_in", "w_out", "w_cq", "w_ckv", "w_co", "w_gate_up", "w_down")
SMALL = ("mix_norm_g", "q_norm_g", "k_norm_g", "pool_w", "pool_scale", "cross_norm_g", "mem_norm_g",
         "cq_norm_g", "ck_norm_g", "ffn_norm_g")
WEIGHTS = ("mix_norm_g", "w_in", "q_norm_g", "k_norm_g", "pool_w", "pool_scale", "w_out", "cross_norm_g",
           "mem_norm_g", "w_cq", "w_ckv", "cq_norm_g", "ck_norm_g", "w_co", "ffn_norm_g", "w_gate_up", "w_down")


def _cast_piece(name, w, k_arr):
    R, C = w.shape
    hr = R // 2

    def body(k_ref, w_ref, o_ref):
        o_ref[...] = _bf(w_ref[...])

    return pl.pallas_call(
        body, name=name,
        grid_spec=pltpu.PrefetchScalarGridSpec(
            num_scalar_prefetch=1, grid=(2,),
            in_specs=[pl.BlockSpec((hr, C), lambda i, k: (i, 0))],
            out_specs=pl.BlockSpec((None, hr, C), lambda i, k: (k[0], i, 0))),
        out_shape=_sds((N_CHIPS, R, C), BF16),
        compiler_params=_cparams(1, 2 * _nbytes((hr, C), F32)))(k_arr, *_pin([w]))


def _add_core_halves(name, g, t, kc_arr):
    P, R, C = g.shape
    hr = R // 2

    def body(kc_ref, g_ref, t_ref, o_ref, ob_ref):
        tot = g_ref[...] + t_ref[...]
        ob_ref[...] = _bf(tot)

        @pl.when(pl.program_id(0) == kc_ref[0])
        def _():
            o_ref[...] = tot

    piece = pl.BlockSpec((None, hr, C), lambda p, kc: (p, 0, 0))
    return pl.pallas_call(
        body, name=name,
        grid_spec=pltpu.PrefetchScalarGridSpec(
            num_scalar_prefetch=1, grid=(P,),
            in_specs=[pl.BlockSpec((None, hr, C), lambda p, kc: (p, kc[1], 0)), piece],
            out_specs=[pl.BlockSpec((hr, C), lambda p, kc: (0, 0)), piece]),
        out_shape=[_sds((hr, C), F32), _sds((P, hr, C), BF16)],
        compiler_params=_cparams(1, 4 * _nbytes((hr, C), F32)))(kc_arr, *_pin([g, t]))


def _sum_chips(name, own, got, kc_arr):
    hr, C = own.shape

    def body(kc_ref, o_ref, g_ref, r_ref):
        r_ref[...] = ((o_ref[...] + g_ref[0].astype(F32)) + g_ref[1].astype(F32)) + g_ref[2].astype(F32)

    return pl.pallas_call(
        body, name=name,
        grid_spec=pltpu.PrefetchScalarGridSpec(
            num_scalar_prefetch=1, grid=(1,),
            in_specs=[pl.BlockSpec((hr, C), lambda i, kc: (0, 0)),
                      pl.BlockSpec((N_CHIPS - 1, hr, C), lambda i, kc: (0, 0, 0))],
            out_specs=pl.BlockSpec((hr, C), lambda i, kc: (kc[1], 0))),
        out_shape=_sds((2 * hr, C), F32),
        compiler_params=_cparams(1, 5 * _nbytes((hr, C), F32)))(kc_arr, *_pin([own, got]))


N_DEV = 8


class _AllPushPlan(_Plan):
    def __init__(self, v):
        self.ins = [v]
        self.out_shapes = [jax.ShapeDtypeStruct((N_DEV,) + v.shape, v.dtype)]
        self.sem_shapes = [pltpu.SemaphoreType.DMA((N_DEV - 1,)), pltpu.SemaphoreType.DMA((N_DEV - 1,)),
                           pltpu.SemaphoreType.DMA]

    def copies(self, ins, outs, sems):
        send, recv, own = sems
        x, y, c, _ = _place()
        slot = outs[0].at[4 * x + 2 * y + c]
        cps = [pltpu.make_async_copy(ins[0], slot, own)]
        flips = [(dx, dy, dc) for dx in (0, 1) for dy in (0, 1) for dc in (0, 1)][1:]
        for q, (dx, dy, dc) in enumerate(flips):
            to = (x + dx - 2 * x * dx, y + dy - 2 * y * dy, c + dc - 2 * c * dc)
            cps.append(pltpu.make_async_remote_copy(src_ref=ins[0], dst_ref=slot, send_sem=send.at[q],
                                                    recv_sem=recv.at[q], device_id=to, device_id_type=MESH))
        return cps


def _sum_slots(name, slots):
    n, R, C = slots.shape

    def body(s_ref, o_ref):
        acc = s_ref[0]
        for d in range(1, n):
            acc = acc + s_ref[d]
        o_ref[...] = acc

    return pl.pallas_call(
        body, name=name, grid=(1,),
        in_specs=[pl.BlockSpec((n, R, C), lambda i: (0, 0, 0))], out_specs=pl.BlockSpec((R, C), lambda i: (0, 0)),
        out_shape=_sds((R, C), F32),
        compiler_params=_cparams(1, _nbytes((n + 1, R, C), F32)))(*_pin([slots]))


def _adamw(name, w, g, m, v, *, tr):
    R, C = w.shape

    def body(w_ref, g_ref, m_ref, v_ref, d_ref, nm_ref, nv_ref):
        gv = g_ref[...]
        nm = ADAM_B1 * m_ref[...] + (1.0 - ADAM_B1) * gv
        nv = ADAM_B2 * v_ref[...] + (1.0 - ADAM_B2) * (gv * gv)
        m_hat = nm / (1.0 - ADAM_B1 ** ADAM_STEP)
        v_hat = nv / (1.0 - ADAM_B2 ** ADAM_STEP)
        d_ref[...] = -ADAM_LR * (m_hat / (jnp.sqrt(v_hat) + ADAM_EPS) + ADAM_WD * w_ref[...])
        nm_ref[...] = nm
        nv_ref[...] = nv

    tile = pl.BlockSpec((tr, C), lambda i: (i, 0))
    return pl.pallas_call(
        body, name=name, grid=(R // tr,), in_specs=[tile] * 4, out_specs=[tile] * 3,
        out_shape=[_sds((R, C), F32)] * 3,
        compiler_params=_cparams(1, 7 * _nbytes((tr, C), F32)))(*_pin([w, g, m, v]))


def _pack_small(d):
    parts = []
    for name in SMALL:
        a = d[name].reshape(-1, HEAD)
        pad = (-a.shape[0]) % 8
        parts.append(jnp.pad(a, ((0, pad), (0, 0))))
    return jnp.concatenate(parts, axis=0)


def _unpack_small(packed, like):
    out = {}
    row = 0
    for name in SMALL:
        shape = like[name].shape
        rows = like[name].size // HEAD
        out[name] = packed[row:row + rows].reshape(shape)
        row += rows + (-rows) % 8
    return out


def kernel(x, mem, positions, mix_norm_g, w_in, q_norm_g, k_norm_g, pool_w, pool_scale, w_out, cross_norm_g, mem_norm_g, w_cq, w_ckv, cq_norm_g, ck_norm_g, w_co, ffn_norm_g, w_gate_up, w_down, loss_target, m_mix_norm_g, m_w_in, m_q_norm_g, m_k_norm_g, m_pool_w, m_pool_scale, m_w_out, m_cross_norm_g, m_mem_norm_g, m_w_cq, m_w_ckv, m_cq_norm_g, m_ck_norm_g, m_w_co, m_ffn_norm_g, m_w_gate_up, m_w_down, v_mix_norm_g, v_w_in, v_q_norm_g, v_k_norm_g, v_pool_w, v_pool_scale, v_w_out, v_cross_norm_g, v_mem_norm_g, v_w_cq, v_w_ckv, v_cq_norm_g, v_ck_norm_g, v_w_co, v_ffn_norm_g, v_w_gate_up, v_w_down):
    w = dict(mix_norm_g=mix_norm_g, w_in=w_in, q_norm_g=q_norm_g, k_norm_g=k_norm_g, pool_w=pool_w,
             pool_scale=pool_scale, w_out=w_out, cross_norm_g=cross_norm_g, mem_norm_g=mem_norm_g, w_cq=w_cq,
             w_ckv=w_ckv, cq_norm_g=cq_norm_g, ck_norm_g=ck_norm_g, w_co=w_co, ffn_norm_g=ffn_norm_g,
             w_gate_up=w_gate_up, w_down=w_down)
    m = dict(mix_norm_g=m_mix_norm_g, w_in=m_w_in, q_norm_g=m_q_norm_g, k_norm_g=m_k_norm_g, pool_w=m_pool_w,
             pool_scale=m_pool_scale, w_out=m_w_out, cross_norm_g=m_cross_norm_g, mem_norm_g=m_mem_norm_g,
             w_cq=m_w_cq, w_ckv=m_w_ckv, cq_norm_g=m_cq_norm_g, ck_norm_g=m_ck_norm_g, w_co=m_w_co,
             ffn_norm_g=m_ffn_norm_g, w_gate_up=m_w_gate_up, w_down=m_w_down)
    v = dict(mix_norm_g=v_mix_norm_g, w_in=v_w_in, q_norm_g=v_q_norm_g, k_norm_g=v_k_norm_g, pool_w=v_pool_w,
             pool_scale=v_pool_scale, w_out=v_w_out, cross_norm_g=v_cross_norm_g, mem_norm_g=v_mem_norm_g,
             w_cq=v_w_cq, w_ckv=v_w_ckv, cq_norm_g=v_cq_norm_g, ck_norm_g=v_ck_norm_g, w_co=v_w_co,
             ffn_norm_g=v_ffn_norm_g, w_gate_up=v_w_gate_up, w_down=v_w_down)

    c_arr = lax.axis_index("c").astype(jnp.int32).reshape(1)
    k_arr = (2 * lax.axis_index("x") + lax.axis_index("y")).astype(jnp.int32).reshape(1)
    kc_arr = jnp.concatenate([k_arr, c_arr])
    wb = {k: _cast_piece(f"cast_{k}", w[k][0], k_arr) for k in BIG}
    sm = {k: (w[k][0] if k == "pool_w" else w[k]) for k in SMALL}
    loss_part, dx, gshard, gsm = _local_step(x[0], mem[0], positions[0], loss_target[0], wb, sm,
                                             place=(c_arr, kc_arr))
    loss = lax.psum(loss_part, ("x", "y", "c"))

    grads, deltas, new_m, new_v = {}, {}, {}, {}
    for k in BIG:
        shard = w[k][0]
        tr = shard.shape[0] // 4
        d, nm, nv = _adamw(f"adamw_{k}", shard, gshard[k], m[k][0], v[k][0], tr=tr)
        grads[k], deltas[k], new_m[k], new_v[k] = gshard[k][None], d[None], nm[None], nv[None]
    smw = {k: (w[k][0] if k == "pool_w" else w[k]) for k in SMALL}
    smm = {k: (m[k][0] if k == "pool_w" else m[k]) for k in SMALL}
    smv = {k: (v[k][0] if k == "pool_w" else v[k]) for k in SMALL}
    pw, pg, pm, pv = _pack_small(smw), _pack_small(gsm), _pack_small(smm), _pack_small(smv)
    d, nm, nv = _adamw("adamw_small", pw, pg, pm, pv, tr=pw.shape[0])
    for dst, packed in ((deltas, d), (new_m, nm), (new_v, nv)):
        un = _unpack_small(packed, sm)
        for k in SMALL:
            dst[k] = un[k].reshape(w[k].shape)
    for k in SMALL:
        grads[k] = gsm[k].reshape(w[k].shape)

    return (loss, dx[None], *[grads[k] for k in WEIGHTS], *[deltas[k] for k in WEIGHTS],
            *[new_m[k] for k in WEIGHTS], *[new_v[k] for k in WEIGHTS])
```

```python
import functools

import jax
import jax.numpy as jnp
from jax import lax
from jax.experimental import pallas as pl
from jax.experimental.pallas import tpu as pltpu

F32 = jnp.float32
BF16 = jnp.bfloat16
MESH = pl.DeviceIdType.MESH
ANY = pl.BlockSpec(memory_space=pl.ANY)

D_MODEL = 1024
HEAD = 128
N_HEADS = 4
DILATIONS = (1, 4, 16)
BLOCK = 128
Q_W = 1536
KV_W = 512
POOL_W = 512
POOL_WINDOWS = (2, 4, 8, 16)
POOL_HALO = 16
IN_W = 3072
ROT_DIM = 32
ROT_HALF = 16
ROPE_THETA = 500000.0
X_W = 512
D_FF = 2816
EPS = 1e-6
NEG_INF = -1e30
SCALE = HEAD ** -0.5
N_CHIPS = 4

ADAM_LR = 0.001
ADAM_B1 = 0.9
ADAM_B2 = 0.999
ADAM_EPS = 1e-08
ADAM_WD = 0.01
ADAM_STEP = 10

VMEM_BYTES_V7X = 64 * 2 ** 20
VMEM_LIMIT_MAX = 56 * 2 ** 20
VMEM_LIMIT_MIN = 24 * 2 ** 20


def _nbytes(shape, dtype):
    n = 1
    for s in shape:
        n *= s
    return n * jnp.dtype(dtype).itemsize


def _cparams(n_axes, block_bytes, scratch_bytes=0):
    est = 2 * (2 * block_bytes + scratch_bytes)
    lim = int(min(VMEM_LIMIT_MAX, max(VMEM_LIMIT_MIN, est)))
    return pltpu.CompilerParams(dimension_semantics=("arbitrary",) * n_axes, vmem_limit_bytes=lim)


def _bf(v):
    return v.astype(BF16)


def _dot(a, b):
    return jnp.dot(a, b, preferred_element_type=F32)


def _dot_nt(a, b):
    return lax.dot_general(a, b, (((1,), (1,)), ((), ())), preferred_element_type=F32)


def _dot_tn(a, b):
    return lax.dot_general(a, b, (((0,), (0,)), ((), ())), preferred_element_type=F32)


def _rstd(v):
    return lax.rsqrt(jnp.mean(v * v, axis=-1, keepdims=True) + EPS)


def _norm_bwd(dy, xv, g):
    r = _rstd(xv)
    xh = xv * r
    dxh = dy * g
    dx = r * (dxh - xh * jnp.mean(dxh * xh, axis=-1, keepdims=True))
    return dx, xh


def _place():
    x, y, c = lax.axis_index("x"), lax.axis_index("y"), lax.axis_index("c")
    other_chips = [(1 - x, y), (x, 1 - y), (1 - x, 1 - y)]
    return x, y, c, other_chips


class _Plan:
    ins = ()
    out_shapes = ()
    aliases = {}
    sem_shapes = ()

    def copies(self, ins, outs, sems):
        raise NotImplementedError

    def begin(self, ins, outs, sems):
        for cp in self.copies(ins, outs, sems):
            cp.start()

    def finish(self, ins, outs, sems):
        for cp in self.copies(ins, outs, sems):
            cp.wait()


class _GatherPlan(_Plan):
    def __init__(self, bufs):
        n = len(bufs)
        self.ins = list(bufs)
        self.out_shapes = [_sds(b.shape, b.dtype) for b in bufs]
        self.aliases = {i: i for i in range(n)}
        self.sem_shapes = [pltpu.SemaphoreType.DMA((n, 6)), pltpu.SemaphoreType.DMA((n, 6))]

    def _parts(self, outs, sems):
        send, recv = sems
        x, y, c, chips = _place()

        def half(i, piece, which):
            hr = outs[i].shape[1] // 2
            return outs[i].at[piece, pl.ds(which * hr, hr), :]

        def copy(i, k, ref, to):
            return pltpu.make_async_remote_copy(src_ref=ref, dst_ref=ref, send_sem=send.at[i, k], recv_sem=recv.at[i, k],
                                                device_id=to, device_id_type=MESH)

        return x, y, c, chips, half, copy

    def begin(self, ins, outs, sems):
        x, y, c, chips, half, copy = self._parts(outs, sems)
        for i in range(len(outs)):
            for j, (cx, cy) in enumerate(chips):
                copy(i, j, half(i, 2 * x + y, c), (cx, cy, c)).start()

    def finish(self, ins, outs, sems):
        x, y, c, chips, half, copy = self._parts(outs, sems)
        sib = (x, y, 1 - c)
        n = len(outs)
        for i in range(n):
            for j, (cx, cy) in enumerate(chips):
                piece = half(i, 2 * cx + cy, c)
                copy(i, j, piece, (cx, cy, c)).wait_recv()
                copy(i, 3 + j, piece, sib).start()
        for i in range(n):
            for j, (cx, cy) in enumerate(chips):
                copy(i, 3 + j, half(i, 2 * cx + cy, 1 - c), sib).wait_recv()
        for i in range(n):
            for j, (cx, cy) in enumerate(chips):
                copy(i, j, half(i, 2 * x + y, c), (cx, cy, c)).wait_send()
                copy(i, 3 + j, half(i, 2 * cx + cy, c), sib).wait_send()


class _SwapPlan(_Plan):
    def __init__(self, grads):
        n = len(grads)
        self.ins = list(grads)
        self.out_shapes = [_sds((g.shape[0], g.shape[1] // 2, g.shape[2]), g.dtype) for g in grads]
        self.sem_shapes = [pltpu.SemaphoreType.DMA((n,)), pltpu.SemaphoreType.DMA((n,))]

    def copies(self, ins, outs, sems):
        send, recv = sems
        x, y, c, _ = _place()
        cps = []
        for i in range(len(ins)):
            hr = ins[i].shape[1] // 2
            cps.append(pltpu.make_async_remote_copy(
                src_ref=ins[i].at[:, pl.ds((1 - c) * hr, hr), :], dst_ref=outs[i], send_sem=send.at[i],
                recv_sem=recv.at[i], device_id=(x, y, 1 - c), device_id_type=MESH))
        return cps


class _ExchangePlan(_Plan):
    def __init__(self, sums):
        n = len(sums)
        self.ins = list(sums)
        self.out_shapes = [_sds((N_CHIPS - 1,) + s.shape[1:], s.dtype) for s in sums]
        self.sem_shapes = [pltpu.SemaphoreType.DMA((n, 3)), pltpu.SemaphoreType.DMA((n, 3))]

    def copies(self, ins, outs, sems):
        send, recv = sems
        x, y, c, chips = _place()
        cps = []
        for i in range(len(ins)):
            for j, (cx, cy) in enumerate(chips):
                cps.append(pltpu.make_async_remote_copy(
                    src_ref=ins[i].at[2 * cx + cy], dst_ref=outs[i].at[j], send_sem=send.at[i, j],
                    recv_sem=recv.at[i, j], device_id=(cx, cy, c), device_id_type=MESH))
        return cps


class _JoinPlan(_Plan):
    def __init__(self, shards):
        n = len(shards)
        self.ins = list(shards)
        self.out_shapes = [_sds(s.shape, s.dtype) for s in shards]
        self.aliases = {i: i for i in range(n)}
        self.sem_shapes = [pltpu.SemaphoreType.DMA((n,)), pltpu.SemaphoreType.DMA((n,))]

    def copies(self, ins, outs, sems):
        send, recv = sems
        x, y, c, _ = _place()
        cps = []
        for i in range(len(outs)):
            hr = outs[i].shape[0] // 2
            mine = outs[i].at[pl.ds(c * hr, hr), :]
            cps.append(pltpu.make_async_remote_copy(src_ref=mine, dst_ref=mine, send_sem=send.at[i], recv_sem=recv.at[i],
                                                    device_id=(x, y, 1 - c), device_id_type=MESH))
        return cps


def _run_plan(name, plan):
    n_in, n_out = len(plan.ins), len(plan.out_shapes)

    def body(*refs):
        ins, outs, sems = refs[:n_in], refs[n_in:n_in + n_out], refs[n_in + n_out:]
        plan.begin(ins, outs, sems)
        plan.finish(ins, outs, sems)

    return pl.pallas_call(
        body, name=name, in_specs=[ANY] * n_in, out_specs=[ANY] * n_out, out_shape=list(plan.out_shapes),
        input_output_aliases=dict(plan.aliases), scratch_shapes=list(plan.sem_shapes))(*plan.ins)


HBM_PIN_BYTES = 1 << 20


def _sds(shape, dtype):
    if _nbytes(shape, dtype) >= HBM_PIN_BYTES:
        return pltpu.HBM(shape, dtype)
    return jax.ShapeDtypeStruct(shape, dtype)


def _pin(args):
    return [pltpu.with_memory_space_constraint(a, pltpu.HBM) if _nbytes(a.shape, a.dtype) >= HBM_PIN_BYTES else a
            for a in args]


def _pcall(body, *, name, grid, in_specs, out_specs, out_shape, params, args, scratch_shapes=(), plan=None):
    in_specs, out_specs, out_shape, scratch = list(in_specs), list(out_specs), list(out_shape), list(scratch_shapes)
    args = _pin(args)
    if plan is None:
        res = pl.pallas_call(body, name=name, grid=grid, in_specs=in_specs, out_specs=out_specs, out_shape=out_shape,
                             scratch_shapes=scratch, compiler_params=params)(*args)
        return list(res), []
    ni, no, ns = len(in_specs), len(out_specs), len(scratch)
    pi, po = len(plan.ins), len(plan.out_shapes)

    def wrapped(*refs):
        ins, pins = refs[:ni], refs[ni:ni + pi]
        outs, pouts = refs[ni + pi:ni + pi + no], refs[ni + pi + no:ni + pi + no + po]
        scr, psems = refs[ni + pi + no + po:ni + pi + no + po + ns], refs[ni + pi + no + po + ns:]
        ids = [pl.program_id(a) for a in range(len(grid))]
        first = functools.reduce(jnp.logical_and, [i == 0 for i in ids])
        last = functools.reduce(jnp.logical_and, [i == g - 1 for i, g in zip(ids, grid)])

        @pl.when(first)
        def _():
            plan.begin(pins, pouts, psems)

        body(*ins, *outs, *scr)

        @pl.when(last)
        def _():
            plan.finish(pins, pouts, psems)

    res = pl.pallas_call(
        wrapped, name=name, grid=grid, in_specs=in_specs + [ANY] * pi, out_specs=out_specs + [ANY] * po,
        out_shape=out_shape + list(plan.out_shapes), scratch_shapes=scratch + list(plan.sem_shapes),
        input_output_aliases={ni + a: no + b for a, b in plan.aliases.items()},
        compiler_params=params)(*args, *plan.ins)
    return list(res[:no]), list(res[no:])


def _mm_nn(name, a, w3, *, tm, norm_g=None, residual=None, out_dtype=F32, plan=None):
    M, K = a.shape
    P, Kw, C = w3.shape
    assert Kw == K and M % tm == 0
    N = P * C
    has_norm = norm_g is not None
    has_res = residual is not None

    def body(*refs):
        refs = list(refs)
        a_ref = refs.pop(0)
        w_ref = refs.pop(0)
        g_ref = refs.pop(0) if has_norm else None
        r_ref = refs.pop(0) if has_res else None
        o_ref = refs.pop(0)
        xn_ref = refs.pop(0) if has_norm else None
        if has_norm:
            av = a_ref[...].astype(F32)
            ab = _bf(av * _rstd(av) * g_ref[...])
            xn_ref[...] = ab
        else:
            ab = _bf(a_ref[...])
        for p in range(P):
            acc = _dot(ab, w_ref[p])
            if has_res:
                acc = acc + r_ref[:, p * C:(p + 1) * C]
            o_ref[:, p * C:(p + 1) * C] = acc.astype(o_ref.dtype)

    in_specs = [pl.BlockSpec((tm, K), lambda i: (i, 0)), pl.BlockSpec((P, K, C), lambda i: (0, 0, 0))]
    args = [a, w3]
    if has_norm:
        in_specs.append(pl.BlockSpec((1, K), lambda i: (0, 0)))
        args.append(norm_g)
    if has_res:
        in_specs.append(pl.BlockSpec((tm, N), lambda i: (i, 0)))
        args.append(residual)
    out_shape = [_sds((M, N), out_dtype)]
    out_specs = [pl.BlockSpec((tm, N), lambda i: (i, 0))]
    if has_norm:
        out_shape.append(_sds((M, K), BF16))
        out_specs.append(pl.BlockSpec((tm, K), lambda i: (i, 0)))
    blk = (_nbytes((tm, K), a.dtype) + _nbytes((P, K, C), BF16) + 2 * _nbytes((tm, N), F32)
           + _nbytes((tm, K), BF16))
    res, got = _pcall(body, name=name, grid=(M // tm,), in_specs=in_specs, out_specs=out_specs, out_shape=out_shape,
                      params=_cparams(1, blk), args=args, plan=plan)
    res = res if has_norm else res[0]
    return res if plan is None else (res, got)


MXU_COLS_V7X = 256


def _col_chunks(n):
    return [(c, min(c + MXU_COLS_V7X, n)) for c in range(0, n, MXU_COLS_V7X)]


def _ffn_fwd(name, h, wgu3, g, wd3, target, *, tm):
    M, K = h.shape
    P, _, C = wgu3.shape
    half = P // 2
    F = half * C

    def body(h_ref, w_ref, g_ref, wd_ref, t_ref, act_ref, fac_ref, xn_ref, dy_ref, ls_ref):
        hv = h_ref[...]
        ab = _bf(hv * _rstd(hv) * g_ref[...])
        xn_ref[...] = ab
        for j in range(half):
            for c0, c1 in _col_chunks(C):
                gate = _dot(ab, w_ref[j, :, c0:c1])
                up = _dot(ab, w_ref[half + j, :, c0:c1])
                sig = 1.0 / (1.0 + jnp.exp(-gate))
                silu = gate * sig
                cols = slice(j * C + c0, j * C + c1)
                act_ref[:, cols] = _bf(silu * up)
                fac_ref[0, :, cols] = _bf(up * (sig + silu * (1.0 - sig)))
                fac_ref[1, :, cols] = _bf(silu)
        err = _dot(act_ref[...], wd_ref[...]) + hv - t_ref[...]
        dy_ref[...] = err * (1.0 / K)

        @pl.when(pl.program_id(0) == 0)
        def _():
            ls_ref[...] = jnp.zeros_like(ls_ref)
        ls_ref[...] += jnp.sum(err * err, axis=0, keepdims=True)

    row = pl.BlockSpec((tm, K), lambda i: (i, 0))
    vec = pl.BlockSpec((1, K), lambda i: (0, 0))
    blk = (3 * _nbytes((tm, K), F32) + (_nbytes((P, K, C), BF16) + _nbytes((F, K), BF16)) // 2
           + 3 * _nbytes((tm, F), BF16) + _nbytes((tm, K), BF16))
    return pl.pallas_call(
        body, name=name, grid=(M // tm,),
        in_specs=[row, pl.BlockSpec((P, K, C), lambda i: (0, 0, 0), pipeline_mode=pl.Buffered(1)), vec,
                  pl.BlockSpec((None, F, K), lambda i: (0, 0, 0), pipeline_mode=pl.Buffered(1)), row],
        out_specs=[pl.BlockSpec((tm, F), lambda i: (i, 0)), pl.BlockSpec((2, tm, F), lambda i: (0, i, 0)),
                   row, row, vec],
        out_shape=[_sds((M, F), BF16), _sds((2, M, F), BF16), _sds((M, K), BF16), _sds((M, K), F32),
                   _sds((1, K), F32)],
        compiler_params=_cparams(1, blk))(*_pin([h, wgu3, g, wd3, target]))


def _nt_pieces(a_ref, w_ref):
    P, _, C = w_ref.shape
    acc = _dot_nt(_bf(a_ref[:, 0:C]), w_ref[0])
    for p in range(1, P):
        acc = acc + _dot_nt(_bf(a_ref[:, p * C:(p + 1) * C]), w_ref[p])
    return acc


def _mm_nt_normbwd(name, a, w3, h, g, dres, *, tm, plan=None):
    M, D = h.shape
    P, Ko, C = w3.shape
    N = a.shape[1]
    assert N == P * C and Ko == D and M % tm == 0
    has_res = dres is not None

    def body(*refs):
        a_ref, w_ref, h_ref, g_ref = refs[:4]
        r_ref = refs[4] if has_res else None
        dx_ref, dg_ref = refs[-2:]
        dhn = _nt_pieces(a_ref, w_ref)
        dx, xh = _norm_bwd(dhn, h_ref[...], g_ref[...])
        if has_res:
            dx = dx + r_ref[...]
        dx_ref[...] = dx

        @pl.when(pl.program_id(0) == 0)
        def _():
            dg_ref[...] = jnp.zeros_like(dg_ref)
        dg_ref[...] += jnp.sum(dhn * xh, axis=0, keepdims=True)

    row = pl.BlockSpec((tm, D), lambda i: (i, 0))
    vec = pl.BlockSpec((1, D), lambda i: (0, 0))
    in_specs = [pl.BlockSpec((tm, N), lambda i: (i, 0)), pl.BlockSpec((P, Ko, C), lambda i: (0, 0, 0)), row, vec]
    args = [a, w3, h, g]
    if has_res:
        in_specs.append(row)
        args.append(dres)
    blk = _nbytes((tm, N), a.dtype) + _nbytes((P, Ko, C), BF16) + 5 * _nbytes((tm, D), F32)
    res, got = _pcall(body, name=name, grid=(M // tm,), in_specs=in_specs, out_specs=[row, vec],
                      out_shape=[_sds((M, D), F32), _sds((1, D), F32)],
                      params=_cparams(1, blk), args=args, plan=plan)
    return res if plan is None else (res, got)


def _ffn_bwd(name, dy, wd3, fac, wgu3, h, g, *, tm):
    M, D = h.shape
    P, _, C = wgu3.shape
    F = fac.shape[2]
    per = F // C

    def body(dy_ref, wd_ref, f_ref, w_ref, h_ref, g_ref, dgu_ref, dx_ref, dg_ref):
        dyv = dy_ref[...]
        ab = _bf(dyv)
        dhn = None
        for pc in range(per):
            cols = slice(pc * C, (pc + 1) * C)
            da = _dot_nt(ab, wd_ref[cols, :])
            dgate = _bf(da * f_ref[0, :, cols].astype(F32))
            dup = _bf(da * f_ref[1, :, cols].astype(F32))
            dgu_ref[0, :, cols] = dgate
            dgu_ref[1, :, cols] = dup
            part = _dot_nt(dgate, w_ref[pc]) + _dot_nt(dup, w_ref[per + pc])
            dhn = part if dhn is None else dhn + part
        dx, xh = _norm_bwd(dhn, h_ref[...], g_ref[...])
        dx_ref[...] = dx + dyv

        @pl.when(pl.program_id(0) == 0)
        def _():
            dg_ref[...] = jnp.zeros_like(dg_ref)
        dg_ref[...] += jnp.sum(dhn * xh, axis=0, keepdims=True)

    row = pl.BlockSpec((tm, D), lambda i: (i, 0))
    vec = pl.BlockSpec((1, D), lambda i: (0, 0))
    planes = pl.BlockSpec((2, tm, F), lambda i: (0, i, 0))
    blk = (2 * _nbytes((2, tm, F), BF16) + (_nbytes((F, D), BF16) + _nbytes((P, D, C), BF16)) // 2
           + 4 * _nbytes((tm, D), F32))
    return pl.pallas_call(
        body, name=name, grid=(M // tm,),
        in_specs=[row, pl.BlockSpec((None, F, D), lambda i: (0, 0, 0), pipeline_mode=pl.Buffered(1)), planes,
                  pl.BlockSpec((P, D, C), lambda i: (0, 0, 0), pipeline_mode=pl.Buffered(1)), row, vec],
        out_specs=[planes, row, vec],
        out_shape=[_sds((2, M, F), BF16), _sds((M, D), F32), _sds((1, D), F32)],
        compiler_params=_cparams(1, blk))(*_pin([dy, wd3, fac, wgu3, h, g]))


def _mm_tn_wide(name, a, b, pieces, *, ts, plan=None):
    S, K = a.shape
    N = b.shape[1]
    C = N // pieces

    def body(a_ref, b_ref, o_ref):
        @pl.when(pl.program_id(0) == 0)
        def _():
            o_ref[...] = jnp.zeros_like(o_ref)
        acc = _dot_tn(_bf(a_ref[...]), _bf(b_ref[...]))
        for p in range(pieces):
            o_ref[p] += acc[:, p * C:(p + 1) * C]

    blk = _nbytes((ts, K), a.dtype) + _nbytes((ts, N), b.dtype) + 2 * _nbytes((K, N), F32)
    res, got = _pcall(body, name=name, grid=(S // ts,),
                      in_specs=[pl.BlockSpec((ts, K), lambda s: (s, 0)), pl.BlockSpec((ts, N), lambda s: (s, 0))],
                      out_specs=[pl.BlockSpec((pieces, K, C), lambda s: (0, 0, 0))],
                      out_shape=[_sds((pieces, K, C), F32)], params=_cparams(1, blk), args=[a, b], plan=plan)
    return res[0] if plan is None else (res[0], got)


def _mm_tn(name, a, a_spec, a_blk, b, b_spec, b_blk, out3, o_spec, o_blk, grid, plan=None):
    def body(a_ref, b_ref, o_ref):
        @pl.when(pl.program_id(2) == 0)
        def _():
            o_ref[...] = jnp.zeros_like(o_ref)
        o_ref[...] += _dot_tn(_bf(a_ref[...]), _bf(b_ref[...]))

    blk = _nbytes(a_blk, a.dtype) + _nbytes(b_blk, b.dtype) + 2 * _nbytes(o_blk, F32)
    res, got = _pcall(body, name=name, grid=grid, in_specs=[a_spec, b_spec], out_specs=[o_spec],
                      out_shape=[_sds(out3, F32)], params=_cparams(3, blk), args=[a, b], plan=plan)
    return res[0] if plan is None else (res[0], got)


def _swap_halves(v, lane, masked):
    up = pltpu.roll(v, HEAD - ROT_HALF, 1)
    down = pltpu.roll(v, ROT_HALF, 1)
    rest = jnp.where(lane < ROT_DIM, down, 0.0) if masked else down
    return jnp.where(lane < ROT_HALF, up, rest)


def _in_proj(name, x, w3, g, ctab, stab, qg, kg, *, tm, plan=None):
    S, K = x.shape
    P, _, C = w3.shape

    def body(x_ref, w_ref, g_ref, c_ref, s_ref, qg_ref, kg_ref, p_ref, xn_ref, q_ref, k_ref):
        xv = x_ref[...]
        ab = _bf(xv * _rstd(xv) * g_ref[...])
        xn_ref[...] = ab
        for p in range(P):
            p_ref[:, p * C:(p + 1) * C] = _dot(ab, w_ref[p])
        lane = lax.broadcasted_iota(jnp.int32, (tm, HEAD), 1)
        cv = c_ref[...]
        sv = s_ref[...]

        def prep(t, gain, scale):
            n = t * _rstd(t) * gain
            return (n * cv + _swap_halves(n, lane, False) * sv) * scale

        for j in range(Q_W // HEAD):
            q_ref[:, j * HEAD:(j + 1) * HEAD] = prep(p_ref[:, j * HEAD:(j + 1) * HEAD], qg_ref[...], SCALE)
        for j in range(KV_W // HEAD):
            k_ref[:, j * HEAD:(j + 1) * HEAD] = prep(p_ref[:, Q_W + j * HEAD:Q_W + (j + 1) * HEAD], kg_ref[...], 1.0)

    row = lambda width: pl.BlockSpec((tm, width), lambda i: (i, 0))
    vec = lambda width: pl.BlockSpec((1, width), lambda i: (0, 0))
    blk = (_nbytes((tm, K), F32) + _nbytes((P, K, C), BF16) + _nbytes((tm, P * C + Q_W + KV_W + 2 * HEAD), F32)
           + _nbytes((tm, K), BF16))
    res, got = _pcall(
        body, name=name, grid=(S // tm,),
        in_specs=[row(K), pl.BlockSpec((P, K, C), lambda i: (0, 0, 0)), vec(K), row(HEAD), row(HEAD), vec(HEAD), vec(HEAD)],
        out_specs=[row(P * C), row(K), row(Q_W), row(KV_W)],
        out_shape=[_sds((S, P * C), F32), _sds((S, K), BF16), _sds((S, Q_W), F32), _sds((S, KV_W), F32)],
        params=_cparams(1, blk), args=[x, w3, g, ctab, stab, qg, kg], plan=plan)
    return res if plan is None else (res, got)


ATTN_STEP = 2048


def _row_idx(start, dil):
    return pl.ds(start, BLOCK) if dil == 1 else pl.ds(start, BLOCK, stride=dil)


def _rows(ref, start, dil):
    return ref[_row_idx(start, dil), :]


def _set_rows(ref, start, dil, val):
    ref[_row_idx(start, dil), :] = val


def _band_mask(first):
    qi = lax.broadcasted_iota(jnp.int32, (BLOCK, 2 * BLOCK), 0)
    kj = lax.broadcasted_iota(jnp.int32, (BLOCK, 2 * BLOCK), 1)
    band = (kj >= qi) & (kj <= qi + BLOCK)
    if first is None:
        return band
    return band & ((kj >= BLOCK) | jnp.logical_not(first))


def _attn_geometry(S, grp):
    dil = DILATIONS[grp]
    bt = BLOCK * dil
    assert S % ATTN_STEP == 0 and ATTN_STEP % bt == 0
    return dil, bt, ATTN_STEP // bt, S // ATTN_STEP


MERGE_ROWS = 256


def _attn_fwd(name, qn, kn, proj, plan=None):
    S = qn.shape[0]
    groups = range(len(DILATIONS))
    geo = [_attn_geometry(S, g) for g in groups]
    nsb = geo[0][3]
    vcol = (Q_W + KV_W) // HEAD

    def body(q0, q1, q2, kc_ref, vc_ref, kp0, kp1, kp2, vp0, vp1, vp2, mix_ref, lse_ref, o_s, l_s):
        q_refs, kp_refs, vp_refs = (q0, q1, q2), (kp0, kp1, kp2), (vp0, vp1, vp2)
        valid_first = _band_mask(pl.program_id(1) == 0)
        valid_inner = _band_mask(None)
        for g in groups:
            dil, bt, nsub, _ = geo[g]
            o_g, l_g = o_s.at[g], l_s.at[g]
            for r in range(dil):
                kprev, vprev = _bf(_rows(kp_refs[g], r, dil)), _bf(_rows(vp_refs[g], r, dil))
                for b in range(nsub):
                    at = b * bt + r
                    kcur, vcur = _bf(_rows(kc_ref, at, dil)), _bf(_rows(vc_ref, at, dil))
                    q = _bf(_rows(q_refs[g], at, dil))
                    k2 = jnp.concatenate([kprev, kcur], axis=0)
                    v2 = jnp.concatenate([vprev, vcur], axis=0)
                    s = jnp.where(valid_first if b == 0 else valid_inner, _dot_nt(q, k2), NEG_INF)
                    m = jnp.max(s, axis=-1, keepdims=True)
                    p = jnp.exp(s - m)
                    l = jnp.sum(p, axis=-1, keepdims=True)
                    acc = _dot(_bf(p), v2)
                    _set_rows(o_g, at, dil, acc / l)
                    _set_rows(l_g, at, dil, jnp.broadcast_to(m + jnp.log(l), (BLOCK, HEAD)))
                    kprev, vprev = kcur, vcur
        for c in range(ATTN_STEP // MERGE_ROWS):
            rows = pl.ds(c * MERGE_ROWS, MERGE_ROWS)
            ls = [l_s[g, rows, :] for g in groups]
            m = jnp.maximum(jnp.maximum(ls[0], ls[1]), ls[2])
            ws = [jnp.exp(v - m) for v in ls]
            den = ws[0] + ws[1] + ws[2]
            num = ws[0] * o_s[0, rows, :] + ws[1] * o_s[1, rows, :] + ws[2] * o_s[2, rows, :]
            mix_ref[rows, :] = _bf(num / den)
            lse_ref[rows, :] = m + jnp.log(den)

    def big(col):
        return pl.BlockSpec((ATTN_STEP, HEAD), lambda h, n: (n, col + h))

    def tail(g, col):
        _, bt, nsub, _ = geo[g]
        return pl.BlockSpec((bt, HEAD), lambda h, n: (jnp.maximum(n * nsub - 1, 0), col + h))

    blk = 7 * _nbytes((ATTN_STEP, HEAD), F32) + 2 * sum(_nbytes((geo[g][1], HEAD), F32) for g in groups)
    scratch = 2 * len(DILATIONS) * _nbytes((ATTN_STEP, HEAD), F32)
    res, got = _pcall(
        body, name=name, grid=(N_HEADS, nsb),
        in_specs=[big(g * N_HEADS) for g in groups] + [big(0), big(vcol)]
                 + [tail(g, 0) for g in groups] + [tail(g, vcol) for g in groups],
        out_specs=[big(0), big(0)],
        out_shape=[_sds((S, 2 * KV_W), BF16), _sds((S, KV_W), F32)],
        scratch_shapes=[pltpu.VMEM((len(DILATIONS), ATTN_STEP, HEAD), F32)] * 2,
        params=_cparams(2, blk, scratch), args=[qn, qn, qn, kn, proj, kn, kn, kn, proj, proj, proj], plan=plan)
    return res if plan is None else (res, got)


STAT_LANES = HEAD // 2


def _attn_bwd(name, qn, kn, proj, dattn, stats, grp, plan=None):
    S = qn.shape[0]
    dil, bt, nsub, nsb = _attn_geometry(S, grp)
    vcol = (Q_W + KV_W) // HEAD
    out_dtype = BF16 if dil == 1 else F32

    def slot(b, r):
        return pl.ds((b * dil + r) * BLOCK, BLOCK)

    def body(q_ref, kp_ref, kc_ref, vp_ref, vc_ref, do_ref, st_ref, dq_ref, dk_ref, dv_ref, ck_ref, cv_ref):
        n = pl.program_id(1)
        par = n % 2

        @pl.when(n < nsb)
        def _():
            valid_first = _band_mask(n == 0)
            valid_inner = _band_mask(None)
            ck, cv = ck_ref.at[par], cv_ref.at[par]
            pk, pv = ck_ref.at[1 - par], cv_ref.at[1 - par]
            for r in range(dil):
                kprev, vprev = _bf(_rows(kp_ref, r, dil)), _bf(_rows(vp_ref, r, dil))
                own_k = own_v = None
                for b in range(nsub):
                    at = b * bt + r
                    kcur, vcur = _bf(_rows(kc_ref, at, dil)), _bf(_rows(vc_ref, at, dil))
                    q = _bf(_rows(q_ref, at, dil))
                    k2 = jnp.concatenate([kprev, kcur], axis=0)
                    v2 = jnp.concatenate([vprev, vcur], axis=0)
                    do = _bf(_rows(do_ref, at, dil))
                    stat = _rows(st_ref, at, dil)
                    lse_r = stat[:, :1]
                    del_r = stat[:, STAT_LANES:STAT_LANES + 1]
                    s = jnp.where(valid_first if b == 0 else valid_inner, _dot_nt(q, k2), NEG_INF)
                    p = jnp.exp(s - lse_r)
                    ds = _bf(p * (_dot_nt(do, v2) - del_r))
                    _set_rows(dq_ref, at, dil, _dot(ds, k2).astype(dq_ref.dtype))
                    dk2 = _dot_tn(ds, q)
                    dv2 = _dot_tn(_bf(p), do)
                    if b > 0:
                        ck[slot(b - 1, r), :] = own_k + dk2[:BLOCK]
                        cv[slot(b - 1, r), :] = own_v + dv2[:BLOCK]
                    else:
                        @pl.when(n > 0)
                        def _():
                            pk[slot(nsub - 1, r), :] += dk2[:BLOCK]
                            pv[slot(nsub - 1, r), :] += dv2[:BLOCK]
                    own_k, own_v = dk2[BLOCK:], dv2[BLOCK:]
                    kprev, vprev = kcur, vcur
                ck[slot(nsub - 1, r), :] = own_k
                cv[slot(nsub - 1, r), :] = own_v

        @pl.when(n > 0)
        def _():
            for r in range(dil):
                for b in range(nsub):
                    _set_rows(dk_ref, b * bt + r, dil, ck_ref[1 - par, slot(b, r), :].astype(dk_ref.dtype))
                    _set_rows(dv_ref, b * bt + r, dil, cv_ref[1 - par, slot(b, r), :].astype(dv_ref.dtype))

    cur = lambda n: jnp.minimum(n, nsb - 1)
    prev = lambda n: jnp.maximum(n - 1, 0)
    tail_at = lambda n: jnp.maximum(cur(n) * nsub - 1, 0)
    big = lambda col: pl.BlockSpec((ATTN_STEP, HEAD), lambda h, n: (cur(n), col(h)))
    tail = lambda col: pl.BlockSpec((bt, HEAD), lambda h, n: (tail_at(n), col(h)))
    late = pl.BlockSpec((ATTN_STEP, HEAD), lambda h, n: (prev(n), h))
    blk = 8 * _nbytes((ATTN_STEP, HEAD), F32) + 2 * _nbytes((bt, HEAD), F32)
    res, got = _pcall(
        body, name=name, grid=(N_HEADS, nsb + 1),
        in_specs=[big(lambda h: grp * N_HEADS + h), tail(lambda h: h), big(lambda h: h),
                  tail(lambda h: vcol + h), big(lambda h: vcol + h),
                  big(lambda h: h), big(lambda h: h)],
        out_specs=[big(lambda h: h), late, late],
        out_shape=[_sds((S, KV_W), out_dtype)] * 3,
        scratch_shapes=[pltpu.VMEM((2, ATTN_STEP, HEAD), F32), pltpu.VMEM((2, ATTN_STEP, HEAD), F32)],
        params=_cparams(2, blk, 4 * _nbytes((ATTN_STEP, HEAD), F32)),
        args=[qn, kn, kn, proj, proj, dattn, stats], plan=plan)
    return res if plan is None else (res, got)


def _window_sum(v, n_doublings, back):
    rows = v.shape[0]
    step = 1
    for _ in range(n_doublings):
        v = v + pltpu.roll(v, step if back else rows - step, 0)
        step *= 2
    return v


def _pool_out_proj(name, mix, proj, pool_w, pool_scale, w_out3, x, *, tm):
    S, D = x.shape
    ucol = (IN_W - POOL_W) // POOL_W
    hpt = tm // POOL_HALO

    def body(at_ref, u_ref, uh_ref, pw_ref, ps_ref, w_ref, x_ref, pooled_ref, dp_ref, h_ref):
        i = pl.program_id(0)
        halo = jnp.where(i == 0, 0.0, uh_ref[...])
        t = lax.broadcasted_iota(jnp.int32, (tm + POOL_HALO, HEAD), 0) + (i * tm - POOL_HALO)
        for g, w in enumerate(POOL_WINDOWS):
            sl = slice(g * HEAD, (g + 1) * HEAD)
            ub = jnp.concatenate([halo[:, sl], u_ref[:, sl]], axis=0)
            cnt = jnp.minimum(t + 1, w).astype(F32)
            d = (_window_sum(ub, g + 1, True) / cnt - ub)[POOL_HALO:]
            db = _bf(d)
            dp_ref[:, sl] = db
            pooled_ref[:, sl] = _bf(_dot(db, pw_ref[g]) * ps_ref[:, sl])
        h_ref[...] = x_ref[...] + _dot(at_ref[...], w_ref[:KV_W, :]) + _dot(pooled_ref[...], w_ref[KV_W:, :])

    row = pl.BlockSpec((tm, D), lambda i: (i, 0))
    blk = (3 * _nbytes((tm, POOL_W), F32) + 3 * _nbytes((tm, POOL_W), BF16) + _nbytes((2 * KV_W, D), BF16)
           + 3 * _nbytes((tm, D), F32))
    return pl.pallas_call(
        body, name=name, grid=(S // tm,),
        in_specs=[pl.BlockSpec((tm, KV_W), lambda i: (i, 0)),
                  pl.BlockSpec((tm, POOL_W), lambda i: (i, ucol)),
                  pl.BlockSpec((POOL_HALO, POOL_W), lambda i: (jnp.maximum(i * hpt - 1, 0), ucol)),
                  pl.BlockSpec((len(POOL_WINDOWS), HEAD, HEAD), lambda i: (0, 0, 0)),
                  pl.BlockSpec((1, POOL_W), lambda i: (0, 0)),
                  pl.BlockSpec((None, 2 * KV_W, D), lambda i: (0, 0, 0)), row],
        out_specs=[pl.BlockSpec((tm, POOL_W), lambda i: (i, 1)), pl.BlockSpec((tm, POOL_W), lambda i: (i, 0)), row],
        out_shape=[_sds((S, 2 * KV_W), BF16), _sds((S, POOL_W), BF16), _sds((S, D), F32)],
        input_output_aliases={0: 0},
        compiler_params=_cparams(1, blk))(*_pin([mix, proj, proj, pool_w, pool_scale, w_out3, x]))


def _mix_bwd(name, dh, w_out3, mix, lse, dpool, pool_w, pool_scale, *, tm, plan=None):
    S, D = dh.shape
    hpt = tm // POOL_HALO
    last_halo = S // POOL_HALO - 1
    n_tiles = S // tm

    def body(dh_ref, dhn_ref, w_ref, at_ref, l_ref, dp_ref, pw_ref, ps_ref, da_ref, st_ref, du_ref, gw_ref, gs_ref):
        i = pl.program_id(0)

        @pl.when(i == 0)
        def _():
            gw_ref[...] = jnp.zeros_like(gw_ref)
            gs_ref[...] = jnp.zeros_like(gs_ref)

        dm = _dot_nt(_bf(dh_ref[...]), w_ref[...])
        ahead = _dot_nt(_bf(dhn_ref[...]), w_ref[KV_W:, :])
        lane = lax.broadcasted_iota(jnp.int32, (tm, HEAD), 1)
        for h in range(N_HEADS):
            sl = slice(h * HEAD, (h + 1) * HEAD)
            da = dm[:, sl]
            da_ref[:, sl] = da
            delta = jnp.sum(da * at_ref[:, sl].astype(F32), axis=-1, keepdims=True)
            st_ref[:, sl] = jnp.where(lane < STAT_LANES, l_ref[:, sl], delta)
        halo = jnp.where(i == n_tiles - 1, 0.0, ahead)
        t = lax.broadcasted_iota(jnp.int32, (tm + POOL_HALO, HEAD), 0) + i * tm
        for g, w in enumerate(POOL_WINDOWS):
            sl = slice(g * HEAD, (g + 1) * HEAD)
            dy = jnp.concatenate([dm[:, KV_W + g * HEAD:KV_W + (g + 1) * HEAD], halo[:, sl]], axis=0)
            dys = _bf(dy * ps_ref[:, sl])
            dd = _dot_nt(dys, pw_ref[g])
            cnt = jnp.minimum(t + 1, w).astype(F32)
            du_ref[:, sl] = (_window_sum(dd / cnt, g + 1, False) - dd)[:tm]
            db = dp_ref[:, sl]
            gw_ref[g] += _dot_tn(db, dys[:tm])
            gs_ref[:, sl] += jnp.sum(dy[:tm] * _dot(db, pw_ref[g]), axis=0, keepdims=True)

    tile = pl.BlockSpec((tm, KV_W), lambda i: (i, 0))
    blk = 2 * _nbytes((tm, D), F32) + _nbytes((2 * KV_W, D), BF16) + 6 * _nbytes((tm, KV_W), F32)
    res, got = _pcall(
        body, name=name, grid=(n_tiles,),
        in_specs=[pl.BlockSpec((tm, D), lambda i: (i, 0)),
                  pl.BlockSpec((POOL_HALO, D), lambda i: (jnp.minimum((i + 1) * hpt, last_halo), 0)),
                  pl.BlockSpec((None, 2 * KV_W, D), lambda i: (0, 0, 0)),
                  tile, tile, tile,
                  pl.BlockSpec((len(POOL_WINDOWS), HEAD, HEAD), lambda i: (0, 0, 0)),
                  pl.BlockSpec((1, POOL_W), lambda i: (0, 0))],
        out_specs=[tile, tile, tile,
                   pl.BlockSpec((len(POOL_WINDOWS), HEAD, HEAD), lambda i: (0, 0, 0)),
                   pl.BlockSpec((1, POOL_W), lambda i: (0, 0))],
        out_shape=[_sds((S, KV_W), F32)] * 3 + [
            _sds((len(POOL_WINDOWS), HEAD, HEAD), F32), _sds((1, POOL_W), F32)],
        params=_cparams(1, blk), args=[dh, dh, w_out3, mix, lse, dpool, pool_w, pool_scale], plan=plan)
    return res if plan is None else (res, got)


QKV_BWD_BUFFERS = 3


def _qkv_bwd(name, dqs, dks, dvs, du, proj, ctab, stab, qg, kg, *, tm, plan=None):
    S = proj.shape[0]
    width = Q_W + KV_W

    def step(qg_ref, kg_ref, gq_ref, gk_ref,
             dq0, dq1, dq2, dk0, dk1, dk2, dv0, dv1, dv2, du_ref, p_ref, c_ref, s_ref, dp_ref):
        lane = lax.broadcasted_iota(jnp.int32, (tm, HEAD), 1)
        cv = c_ref[...]
        sv = s_ref[...]

        def back(dy, t, g, scale):
            dy = dy * scale
            dn = dy * cv + _swap_halves(dy * sv, lane, True)
            dt, xh = _norm_bwd(dn, t, g)
            return dt, jnp.sum(dn * xh, axis=0, keepdims=True)

        dqr = (dq0, dq1, dq2)
        gq = jnp.zeros((1, HEAD), F32)
        for j in range(Q_W // HEAD):
            grp, h = divmod(j, N_HEADS)
            dt, gj = back(dqr[grp][:, h * HEAD:(h + 1) * HEAD].astype(F32), p_ref[:, j * HEAD:(j + 1) * HEAD],
                          qg_ref[...], SCALE)
            dp_ref[:, j * HEAD:(j + 1) * HEAD] = _bf(dt)
            gq = gq + gj
        gq_ref[...] += gq
        gk = jnp.zeros((1, HEAD), F32)
        for h in range(N_HEADS):
            sl = slice(h * HEAD, (h + 1) * HEAD)
            dt, gj = back(dk0[:, sl] + dk1[:, sl] + dk2[:, sl], p_ref[:, Q_W + h * HEAD:Q_W + (h + 1) * HEAD],
                          kg_ref[...], 1.0)
            dp_ref[:, Q_W + h * HEAD:Q_W + (h + 1) * HEAD] = _bf(dt)
            gk = gk + gj
        gk_ref[...] += gk
        dp_ref[:, width:width + KV_W] = _bf(dv0[...] + dv1[...] + dv2[...])
        dp_ref[:, width + KV_W:] = _bf(du_ref[...])

    streams = [*dqs, *dks, *dvs, du, proj, ctab, stab]
    n_str = len(streams)
    deep = dict(pipeline_mode=pl.Buffered(QKV_BWD_BUFFERS))
    tile = pl.BlockSpec((tm, KV_W), lambda i: (i, 0), **deep)
    rot = pl.BlockSpec((tm, HEAD), lambda i: (i, 0), **deep)
    pi = len(plan.ins) if plan is not None else 0
    po = len(plan.out_shapes) if plan is not None else 0

    def outer(*refs):
        str_refs, (qg_ref, kg_ref), pins = refs[:n_str], refs[n_str:n_str + 2], refs[n_str + 2:n_str + 2 + pi]
        rest = refs[n_str + 2 + pi:]
        dp_hbm, gq_ref, gk_ref = rest[:3]
        pouts, psems = rest[3:3 + po], rest[3 + po:]
        if plan is not None:
            plan.begin(pins, pouts, psems)
        gq_ref[...] = jnp.zeros_like(gq_ref)
        gk_ref[...] = jnp.zeros_like(gk_ref)
        pltpu.emit_pipeline(
            functools.partial(step, qg_ref, kg_ref, gq_ref, gk_ref), grid=(S // tm,),
            in_specs=[tile] * 10 + [pl.BlockSpec((tm, width), lambda i: (i, 0), **deep), rot, rot],
            out_specs=[pl.BlockSpec((tm, IN_W), lambda i: (i, 0))])(*str_refs, dp_hbm)
        if plan is not None:
            plan.finish(pins, pouts, psems)

    whole = pl.BlockSpec(memory_space=pltpu.VMEM)
    stream_bytes = 10 * _nbytes((tm, KV_W), F32) + _nbytes((tm, width), F32) + 2 * _nbytes((tm, HEAD), F32)
    limit = QKV_BWD_BUFFERS * stream_bytes + 2 * _nbytes((tm, IN_W), BF16) + 4 * _nbytes((tm, HEAD), F32) * 16
    res = pl.pallas_call(
        outer, name=name,
        in_specs=[ANY] * n_str + [whole, whole] + [ANY] * pi,
        out_specs=[ANY, whole, whole] + [ANY] * po,
        out_shape=[_sds((S, IN_W), BF16), _sds((1, HEAD), F32), _sds((1, HEAD), F32)]
                  + (list(plan.out_shapes) if plan is not None else []),
        scratch_shapes=list(plan.sem_shapes) if plan is not None else [],
        input_output_aliases={n_str + 2 + a: 3 + b for a, b in plan.aliases.items()} if plan is not None else {},
        compiler_params=pltpu.CompilerParams(vmem_limit_bytes=int(min(VMEM_LIMIT_MAX, max(VMEM_LIMIT_MIN, limit)))),
    )(*_pin(streams), qg, kg, *(plan.ins if plan is not None else []))
    res = list(res)
    return res[:3] if plan is None else (res[:3], res[3:])


def _cross_heads(q_ref, kv_ref, qg, kg, h):
    sl = slice(h * HEAD, (h + 1) * HEAD)
    qr = q_ref[:, sl]
    kr = kv_ref[:, sl]
    qh = qr * _rstd(qr) * qg * SCALE
    kh = kr * _rstd(kr) * kg
    vh = kv_ref[:, X_W + h * HEAD:X_W + (h + 1) * HEAD]
    return qr, _bf(qh), _bf(kh), _bf(vh)


def _cross_fwd(name, h_in, g_in, w_cq3, kv, qg, kg, w_co3, *, tm):
    S, D = h_in.shape
    M = kv.shape[0]
    P, _, C = w_co3.shape

    def body(h_ref, g_ref, wcq_ref, kv_ref, qg_ref, kg_ref, wco_ref, q_ref, hn_ref, o_ref, ho_ref):
        hv = h_ref[...]
        ab = _bf(hv * _rstd(hv) * g_ref[...])
        hn_ref[...] = ab
        q_ref[...] = _dot(ab, wcq_ref[...])
        for h in range(N_HEADS):
            _, qh, kh, vh = _cross_heads(q_ref, kv_ref, qg_ref[...], kg_ref[...], h)
            s = _dot_nt(qh, kh)
            p = jnp.exp(s - jnp.max(s, axis=-1, keepdims=True))
            l = jnp.sum(p, axis=-1, keepdims=True)
            o_ref[:, h * HEAD:(h + 1) * HEAD] = _bf(_dot(_bf(p), vh) / l)
        ob = o_ref[...]
        for p in range(P):
            ho_ref[:, p * C:(p + 1) * C] = hv[:, p * C:(p + 1) * C] + _dot(ob, wco_ref[p])

    vec = pl.BlockSpec((1, HEAD), lambda i: (0, 0))
    row = pl.BlockSpec((tm, D), lambda i: (i, 0))
    qtile = pl.BlockSpec((tm, X_W), lambda i: (i, 0))
    blk = (3 * _nbytes((tm, D), F32) + 2 * _nbytes((tm, X_W), F32) + _nbytes((M, 2 * X_W), F32)
           + 4 * _nbytes((tm, M), F32) + _nbytes((D, X_W), BF16) + _nbytes(w_co3.shape, BF16))
    return pl.pallas_call(
        body, name=name, grid=(S // tm,),
        in_specs=[row, pl.BlockSpec((1, D), lambda i: (0, 0)), pl.BlockSpec((None, D, X_W), lambda i: (0, 0, 0)),
                  pl.BlockSpec((M, 2 * X_W), lambda i: (0, 0)), vec, vec,
                  pl.BlockSpec(w_co3.shape, lambda i: (0, 0, 0))],
        out_specs=[qtile, row, qtile, row],
        out_shape=[_sds((S, X_W), F32), _sds((S, D), BF16), _sds((S, X_W), BF16), _sds((S, D), F32)],
        compiler_params=_cparams(1, blk))(*_pin([h_in, g_in, w_cq3, kv, qg, kg, w_co3]))


def _cross_bwd(name, dh, w_co3, qraw, kv, qg, kg, w_cq3, h_in, g_in, *, tm, plan=None):
    S, D = dh.shape
    M = kv.shape[0]

    def body(dh_ref, wco_ref, q_ref, kv_ref, qg_ref, kg_ref, wcq_ref, h_ref, g_ref,
             dq_ref, dk_ref, dv_ref, gq_ref, dx_ref, gn_ref):
        @pl.when(pl.program_id(0) == 0)
        def _():
            dk_ref[...] = jnp.zeros_like(dk_ref)
            dv_ref[...] = jnp.zeros_like(dv_ref)
            gq_ref[...] = jnp.zeros_like(gq_ref)
            gn_ref[...] = jnp.zeros_like(gn_ref)

        dhv = dh_ref[...]
        do_all = _nt_pieces(dh_ref, wco_ref)
        gq = jnp.zeros((1, HEAD), F32)
        for h in range(N_HEADS):
            sl = slice(h * HEAD, (h + 1) * HEAD)
            qr, qh, kh, vh = _cross_heads(q_ref, kv_ref, qg_ref[...], kg_ref[...], h)
            doh = _bf(do_all[:, sl])
            s = _dot_nt(qh, kh)
            p = jnp.exp(s - jnp.max(s, axis=-1, keepdims=True))
            p = p / jnp.sum(p, axis=-1, keepdims=True)
            pb = _bf(p)
            dp = _dot_nt(doh, vh)
            ds = _bf(p * (dp - jnp.sum(dp * p, axis=-1, keepdims=True)))
            dv_ref[:, sl] += _dot_tn(pb, doh)
            dk_ref[:, sl] += _dot_tn(ds, qh)
            dn = _dot(ds, kh) * SCALE
            dt, xh = _norm_bwd(dn, qr, qg_ref[...])
            dq_ref[:, sl] = _bf(dt)
            gq = gq + jnp.sum(dn * xh, axis=0, keepdims=True)
        gq_ref[...] += gq
        dhn = _dot_nt(dq_ref[...], wcq_ref[...])
        dx, xh = _norm_bwd(dhn, h_ref[...], g_ref[...])
        dx_ref[...] = dx + dhv
        gn_ref[...] += jnp.sum(dhn * xh, axis=0, keepdims=True)

    vec = pl.BlockSpec((1, HEAD), lambda i: (0, 0))
    acc = pl.BlockSpec((M, X_W), lambda i: (0, 0))
    row = pl.BlockSpec((tm, D), lambda i: (i, 0))
    wide = pl.BlockSpec((1, D), lambda i: (0, 0))
    qtile = pl.BlockSpec((tm, X_W), lambda i: (i, 0))
    blk = (3 * _nbytes((tm, X_W), F32) + 3 * _nbytes((M, 2 * X_W), F32) + 6 * _nbytes((tm, M), F32)
           + 4 * _nbytes((tm, D), F32) + _nbytes(w_co3.shape, BF16) + _nbytes((D, X_W), BF16))
    res, got = _pcall(
        body, name=name, grid=(S // tm,),
        in_specs=[row, pl.BlockSpec(w_co3.shape, lambda i: (0, 0, 0)), qtile,
                  pl.BlockSpec((M, 2 * X_W), lambda i: (0, 0)), vec, vec,
                  pl.BlockSpec((None, D, X_W), lambda i: (0, 0, 0)), row, wide],
        out_specs=[qtile, acc, acc, vec, row, wide],
        out_shape=[_sds((S, X_W), BF16), _sds((M, X_W), F32), _sds((M, X_W), F32), _sds((1, HEAD), F32),
                   _sds((S, D), F32), _sds((1, D), F32)],
        params=_cparams(1, blk), args=[dh, w_co3, qraw, kv, qg, kg, w_cq3, h_in, g_in], plan=plan)
    return res if plan is None else (res, got)


def _cross_kv_bwd(name, dkn, dv, kv, kg):
    M = kv.shape[0]

    def body(dk_ref, dv_ref, kv_ref, kg_ref, o_ref, g_ref):
        gk = jnp.zeros((1, HEAD), F32)
        for h in range(N_HEADS):
            sl = slice(h * HEAD, (h + 1) * HEAD)
            dn = dk_ref[:, sl]
            dt, xh = _norm_bwd(dn, kv_ref[:, sl], kg_ref[...])
            o_ref[:, sl] = _bf(dt)
            gk = gk + jnp.sum(dn * xh, axis=0, keepdims=True)
        o_ref[:, X_W:] = _bf(dv_ref[...])
        g_ref[...] = gk

    full = lambda shape: pl.BlockSpec(shape, lambda i: (0,) * len(shape))
    return pl.pallas_call(
        body, name=name, grid=(1,),
        in_specs=[full((M, X_W)), full((M, X_W)), full((M, 2 * X_W)), full((1, HEAD))],
        out_specs=[full((M, 2 * X_W)), full((1, HEAD))],
        out_shape=[_sds((M, 2 * X_W), BF16), _sds((1, HEAD), F32)],
        compiler_params=_cparams(1, 6 * _nbytes((M, 2 * X_W), F32)))(dkn, dv, kv, kg)


def _rope_tables(positions):
    inv_freq = ROPE_THETA ** (-jnp.arange(0, ROT_DIM, 2, dtype=F32) / ROT_DIM)
    ang = positions.astype(F32)[:, None] * inv_freq
    cos, sin = jnp.cos(ang), jnp.sin(ang)
    S = positions.shape[0]
    ctab = jnp.concatenate([cos, cos, jnp.ones((S, HEAD - ROT_DIM), F32)], axis=-1)
    stab = jnp.concatenate([-sin, sin, jnp.zeros((S, HEAD - ROT_DIM), F32)], axis=-1)
    return ctab, stab


GATHER_BEHIND_IN_PROJ = ("w_out", "w_cq", "w_ckv", "w_co")
FFN_WEIGHTS = ("w_gate_up", "w_down")


def _hosted(fn, *args, plan=None, **kw):
    if plan is None:
        return fn(*args, **kw), []
    return fn(*args, plan=plan, **kw)


def _local_step(x, mem, positions, target, wb, sm, *, tm=512, place=None):
    S, D = x.shape
    M = mem.shape[0]
    dist = place is not None
    wb = dict(wb)
    ctab, stab = _rope_tables(positions)
    pool_w_b = _bf(sm["pool_w"])

    def gather(names):
        return _GatherPlan([wb[k] for k in names]) if dist else None

    if dist:
        wb["w_in"], = _run_plan("gather_w_in", gather(["w_in"]))
    w_in = wb["w_in"]
    behind_in_proj = GATHER_BEHIND_IN_PROJ + ("w_gate_up",)
    (proj, xn1, qn, kn), got = _hosted(_in_proj, "in_proj", x, w_in, sm["mix_norm_g"], ctab, stab,
                                       sm["q_norm_g"], sm["k_norm_g"], tm=tm, plan=gather(behind_in_proj))
    wb.update(zip(behind_in_proj, got))
    (mix, lse), got = _hosted(_attn_fwd, "attn_fwd", qn, kn, proj, plan=gather(["w_down"]))
    wb.update(zip(["w_down"], got))
    w_out = wb["w_out"].reshape(1, 2 * KV_W, D)
    w_cq = wb["w_cq"].reshape(1, D, X_W)
    w_ckv = wb["w_ckv"].reshape(1, D, 2 * X_W)
    w_co = wb["w_co"]
    w_gu = wb["w_gate_up"]
    w_down = wb["w_down"].reshape(1, D_FF, D)
    cin = w_in.shape[2]
    cco = w_co.shape[2]
    cgu = w_gu.shape[2]
    mix, dpool, h1 = _pool_out_proj("pool_out_proj", mix, proj, pool_w_b, sm["pool_scale"], w_out, x, tm=tm)
    kv, mem_n = _mm_nn("ckv_proj", mem, w_ckv, tm=M, norm_g=sm["mem_norm_g"])
    cq_raw, hn2, xo, h2 = _cross_fwd("cross_fwd", h1, sm["cross_norm_g"], w_cq, kv, sm["cq_norm_g"],
                                     sm["ck_norm_g"], w_co, tm=tm)
    act, gu, hn3, dy, lsum = _ffn_fwd("ffn_fwd", h2, w_gu, sm["ffn_norm_g"], w_down, target, tm=tm // 2)
    loss = 0.5 * jnp.sum(lsum) / D

    ts = 2 * tm
    nS = S // ts
    g_down = _mm_tn("g_w_down", act, pl.BlockSpec((ts, cgu), lambda r, c, s: (s, r)), (ts, cgu),
                    dy, pl.BlockSpec((ts, D), lambda r, c, s: (s, 0)), (ts, D),
                    (1, D_FF, D), pl.BlockSpec((None, cgu, D), lambda r, c, s: (0, r, 0)), (cgu, D),
                    (D_FF // cgu, 1, nS))
    dgu, dh2, g_ffn_norm = _ffn_bwd("ffn_bwd", dy, w_down, gu, w_gu, h2, sm["ffn_norm_g"], tm=tm // 2)
    g_gu = _mm_tn("g_w_gate_up", hn3, pl.BlockSpec((ts, D), lambda r, c, s: (s, 0)), (ts, D),
                  dgu, pl.BlockSpec((None, ts, cgu), lambda r, c, s: (c // 2, s, c % 2)), (ts, cgu),
                  (4, D, cgu), pl.BlockSpec((None, D, cgu), lambda r, c, s: (c, 0, 0)), (D, cgu),
                  (1, 4, nS))

    g_co = _mm_tn_wide("g_w_co", xo, dh2, N_CHIPS, ts=ts)
    full = {"w_gate_up": g_gu, "w_down": g_down.reshape(N_CHIPS, D_FF // N_CHIPS, D)}
    sums = {}

    def swap(names):
        return _SwapPlan([full[k] for k in names]) if dist else None

    def add_halves(names, from_sibling):
        for k, t in zip(names, from_sibling):
            sums[k] = _add_core_halves(f"add_halves_{k}", full[k], t, place[1])

    def exchange(names):
        return _ExchangePlan([sums[k][1] for k in names]) if dist else None

    def sum_chips(names, from_chips):
        return [_sum_chips(f"sum_chips_{k}", sums[k][0], t, place[1]) for k, t in zip(names, from_chips)]

    (dcq, dkn, dvm, g_cq_norm, dh1, g_cross_norm), got = _hosted(
        _cross_bwd, "cross_bwd", dh2, w_co, cq_raw, kv, sm["cq_norm_g"], sm["ck_norm_g"], w_cq, h1,
        sm["cross_norm_g"], tm=tm, plan=swap(FFN_WEIGHTS))
    add_halves(FFN_WEIGHTS, got)
    dkv, g_ck_norm = _cross_kv_bwd("cross_kv_bwd", dkn, dvm, kv, sm["ck_norm_g"])
    g_cq = _mm_tn("g_w_cq", hn2, pl.BlockSpec((ts, D), lambda r, c, s: (s, 0)), (ts, D),
                  dcq, pl.BlockSpec((ts, X_W), lambda r, c, s: (s, 0)), (ts, X_W),
                  (1, D, X_W), pl.BlockSpec((None, D, X_W), lambda r, c, s: (0, 0, 0)), (D, X_W), (1, 1, nS))
    _, g_mem_norm = _mm_nt_normbwd("ckv_proj_bwd", dkv, w_ckv, mem, sm["mem_norm_g"], None, tm=M)
    g_ckv = _mm_tn("g_w_ckv", mem_n, pl.BlockSpec((M, D), lambda r, c, s: (0, 0)), (M, D),
                   dkv, pl.BlockSpec((M, 2 * X_W), lambda r, c, s: (0, 0)), (M, 2 * X_W),
                   (1, D, 2 * X_W), pl.BlockSpec((None, D, 2 * X_W), lambda r, c, s: (0, 0, 0)), (D, 2 * X_W),
                   (1, 1, 1))

    g_out = _mm_tn("g_w_out", mix, pl.BlockSpec((ts, 2 * KV_W), lambda r, c, s: (s, 0)), (ts, 2 * KV_W),
                   dh1, pl.BlockSpec((ts, D), lambda r, c, s: (s, 0)), (ts, D),
                   (1, 2 * KV_W, D), pl.BlockSpec((None, 2 * KV_W, D), lambda r, c, s: (0, 0, 0)), (2 * KV_W, D),
                   (1, 1, nS))
    full.update({
        "w_out": g_out.reshape(N_CHIPS, 2 * KV_W // N_CHIPS, D),
        "w_cq": g_cq.reshape(N_CHIPS, D // N_CHIPS, X_W),
        "w_ckv": g_ckv.reshape(N_CHIPS, D // N_CHIPS, 2 * X_W),
        "w_co": g_co,
    })
    mixer = GATHER_BEHIND_IN_PROJ
    (dattn, stats, du, g_pool_w, g_pool_scale), got = _hosted(
        _mix_bwd, "mix_bwd", dh1, w_out, mix, lse, dpool, pool_w_b, sm["pool_scale"], tm=tm, plan=swap(mixer))
    add_halves(mixer, got)
    behind_attn = (None, mixer, FFN_WEIGHTS)
    halves = {}
    dqs, dks, dvs = [], [], []
    for grp in range(len(DILATIONS)):
        names = behind_attn[grp]
        (dq, dk, dv), got = _hosted(_attn_bwd, f"attn_bwd{grp}", qn, kn, proj, dattn, stats, grp,
                                    plan=exchange(names) if names else None)
        if dist and names:
            halves.update(zip(names, sum_chips(names, got)))
        dqs.append(dq)
        dks.append(dk)
        dvs.append(dv)
    joined = mixer + FFN_WEIGHTS
    (dproj, g_q_norm, g_k_norm), got = _hosted(
        _qkv_bwd, "qkv_bwd", dqs, dks, dvs, du, proj, ctab, stab, sm["q_norm_g"], sm["k_norm_g"], tm=tm,
        plan=_JoinPlan([halves[k] for k in joined]) if dist else None)
    shards = dict(zip(joined, got))
    dx, g_mix_norm = _mm_nt_normbwd("in_proj_bwd", dproj, w_in, x, sm["mix_norm_g"], dh1, tm=tm)
    small = {
        "mix_norm_g": g_mix_norm, "q_norm_g": g_q_norm, "k_norm_g": g_k_norm, "pool_w": g_pool_w,
        "pool_scale": g_pool_scale, "cross_norm_g": g_cross_norm, "mem_norm_g": g_mem_norm,
        "cq_norm_g": g_cq_norm, "ck_norm_g": g_ck_norm, "ffn_norm_g": g_ffn_norm,
    }
    full["w_in"], got = _hosted(_mm_tn_wide, "g_w_in", xn1, dproj, N_CHIPS, ts=tm,
                                plan=_AllPushPlan(_pack_small(small)) if dist else None)
    if dist:
        small = _unpack_small(_sum_slots("sum_small", got[0]), sm)
        add_halves(["w_in"], _run_plan("swap_w_in", swap(["w_in"])))
        half, = sum_chips(["w_in"], _run_plan("exchange_w_in", exchange(["w_in"])))
        shards["w_in"], = _run_plan("join_w_in", _JoinPlan([half]))
    big = shards if dist else full
    return loss, dx, big, small


BIG = ("w_in", "w_out", "w_cq", "w_ckv", "w_co", "w_gate_up", "w_down")
SMALL = ("mix_norm_g", "q_norm_g", "k_norm_g", "pool_w", "pool_scale", "cross_norm_g", "mem_norm_g",
         "cq_norm_g", "ck_norm_g", "ffn_norm_g")
WEIGHTS = ("mix_norm_g", "w_in", "q_norm_g", "k_norm_g", "pool_w", "pool_scale", "w_out", "cross_norm_g",
           "mem_norm_g", "w_cq", "w_ckv", "cq_norm_g", "ck_norm_g", "w_co", "ffn_norm_g", "w_gate_up", "w_down")


def _cast_piece(name, w, k_arr):
    R, C = w.shape
    hr = R // 2

    def body(k_ref, w_ref, o_ref):
        o_ref[...] = _bf(w_ref[...])

    return pl.pallas_call(
        body, name=name,
        grid_spec=pltpu.PrefetchScalarGridSpec(
            num_scalar_prefetch=1, grid=(2,),
            in_specs=[pl.BlockSpec((hr, C), lambda i, k: (i, 0))],
            out_specs=pl.BlockSpec((None, hr, C), lambda i, k: (k[0], i, 0))),
        out_shape=_sds((N_CHIPS, R, C), BF16),
        compiler_params=_cparams(1, 2 * _nbytes((hr, C), F32)))(k_arr, *_pin([w]))


def _add_core_halves(name, g, t, kc_arr):
    P, R, C = g.shape
    hr = R // 2

    def body(kc_ref, g_ref, t_ref, o_ref, ob_ref):
        tot = g_ref[...] + t_ref[...]
        ob_ref[...] = _bf(tot)

        @pl.when(pl.program_id(0) == kc_ref[0])
        def _():
            o_ref[...] = tot

    piece = pl.BlockSpec((None, hr, C), lambda p, kc: (p, 0, 0))
    return pl.pallas_call(
        body, name=name,
        grid_spec=pltpu.PrefetchScalarGridSpec(
            num_scalar_prefetch=1, grid=(P,),
            in_specs=[pl.BlockSpec((None, hr, C), lambda p, kc: (p, kc[1], 0)), piece],
            out_specs=[pl.BlockSpec((hr, C), lambda p, kc: (0, 0)), piece]),
        out_shape=[_sds((hr, C), F32), _sds((P, hr, C), BF16)],
        compiler_params=_cparams(1, 4 * _nbytes((hr, C), F32)))(kc_arr, *_pin([g, t]))


def _sum_chips(name, own, got, kc_arr):
    hr, C = own.shape

    def body(kc_ref, o_ref, g_ref, r_ref):
        r_ref[...] = ((o_ref[...] + g_ref[0].astype(F32)) + g_ref[1].astype(F32)) + g_ref[2].astype(F32)

    return pl.pallas_call(
        body, name=name,
        grid_spec=pltpu.PrefetchScalarGridSpec(
            num_scalar_prefetch=1, grid=(1,),
            in_specs=[pl.BlockSpec((hr, C), lambda i, kc: (0, 0)),
                      pl.BlockSpec((N_CHIPS - 1, hr, C), lambda i, kc: (0, 0, 0))],
            out_specs=pl.BlockSpec((hr, C), lambda i, kc: (kc[1], 0))),
        out_shape=_sds((2 * hr, C), F32),
        compiler_params=_cparams(1, 5 * _nbytes((hr, C), F32)))(kc_arr, *_pin([own, got]))


N_DEV = 8


class _AllPushPlan(_Plan):
    def __init__(self, v):
        self.ins = [v]
        self.out_shapes = [jax.ShapeDtypeStruct((N_DEV,) + v.shape, v.dtype)]
        self.sem_shapes = [pltpu.SemaphoreType.DMA((N_DEV - 1,)), pltpu.SemaphoreType.DMA((N_DEV - 1,)),
                           pltpu.SemaphoreType.DMA]

    def copies(self, ins, outs, sems):
        send, recv, own = sems
        x, y, c, _ = _place()
        slot = outs[0].at[4 * x + 2 * y + c]
        cps = [pltpu.make_async_copy(ins[0], slot, own)]
        flips = [(dx, dy, dc) for dx in (0, 1) for dy in (0, 1) for dc in (0, 1)][1:]
        for q, (dx, dy, dc) in enumerate(flips):
            to = (x + dx - 2 * x * dx, y + dy - 2 * y * dy, c + dc - 2 * c * dc)
            cps.append(pltpu.make_async_remote_copy(src_ref=ins[0], dst_ref=slot, send_sem=send.at[q],
                                                    recv_sem=recv.at[q], device_id=to, device_id_type=MESH))
        return cps


def _sum_slots(name, slots):
    n, R, C = slots.shape

    def body(s_ref, o_ref):
        acc = s_ref[0]
        for d in range(1, n):
            acc = acc + s_ref[d]
        o_ref[...] = acc

    return pl.pallas_call(
        body, name=name, grid=(1,),
        in_specs=[pl.BlockSpec((n, R, C), lambda i: (0, 0, 0))], out_specs=pl.BlockSpec((R, C), lambda i: (0, 0)),
        out_shape=_sds((R, C), F32),
        compiler_params=_cparams(1, _nbytes((n + 1, R, C), F32)))(*_pin([slots]))


def _adamw(name, w, g, m, v, *, tr):
    R, C = w.shape

    def body(w_ref, g_ref, m_ref, v_ref, d_ref, nm_ref, nv_ref):
        gv = g_ref[...]
        nm = ADAM_B1 * m_ref[...] + (1.0 - ADAM_B1) * gv
        nv = ADAM_B2 * v_ref[...] + (1.0 - ADAM_B2) * (gv * gv)
        m_hat = nm / (1.0 - ADAM_B1 ** ADAM_STEP)
        v_hat = nv / (1.0 - ADAM_B2 ** ADAM_STEP)
        d_ref[...] = -ADAM_LR * (m_hat / (jnp.sqrt(v_hat) + ADAM_EPS) + ADAM_WD * w_ref[...])
        nm_ref[...] = nm
        nv_ref[...] = nv

    tile = pl.BlockSpec((tr, C), lambda i: (i, 0))
    return pl.pallas_call(
        body, name=name, grid=(R // tr,), in_specs=[tile] * 4, out_specs=[tile] * 3,
        out_shape=[_sds((R, C), F32)] * 3,
        compiler_params=_cparams(1, 7 * _nbytes((tr, C), F32)))(*_pin([w, g, m, v]))


def _pack_small(d):
    parts = []
    for name in SMALL:
        a = d[name].reshape(-1, HEAD)
        pad = (-a.shape[0]) % 8
        parts.append(jnp.pad(a, ((0, pad), (0, 0))))
    return jnp.concatenate(parts, axis=0)


def _unpack_small(packed, like):
    out = {}
    row = 0
    for name in SMALL:
        shape = like[name].shape
        rows = like[name].size // HEAD
        out[name] = packed[row:row + rows].reshape(shape)
        row += rows + (-rows) % 8
    return out


def kernel(x, mem, positions, mix_norm_g, w_in, q_norm_g, k_norm_g, pool_w, pool_scale, w_out, cross_norm_g, mem_norm_g, w_cq, w_ckv, cq_norm_g, ck_norm_g, w_co, ffn_norm_g, w_gate_up, w_down, loss_target, m_mix_norm_g, m_w_in, m_q_norm_g, m_k_norm_g, m_pool_w, m_pool_scale, m_w_out, m_cross_norm_g, m_mem_norm_g, m_w_cq, m_w_ckv, m_cq_norm_g, m_ck_norm_g, m_w_co, m_ffn_norm_g, m_w_gate_up, m_w_down, v_mix_norm_g, v_w_in, v_q_norm_g, v_k_norm_g, v_pool_w, v_pool_scale, v_w_out, v_cross_norm_g, v_mem_norm_g, v_w_cq, v_w_ckv, v_cq_norm_g, v_ck_norm_g, v_w_co, v_ffn_norm_g, v_w_gate_up, v_w_down):
    w = dict(mix_norm_g=mix_norm_g, w_in=w_in, q_norm_g=q_norm_g, k_norm_g=k_norm_g, pool_w=pool_w,
             pool_scale=pool_scale, w_out=w_out, cross_norm_g=cross_norm_g, mem_norm_g=mem_norm_g, w_cq=w_cq,
             w_ckv=w_ckv, cq_norm_g=cq_norm_g, ck_norm_g=ck_norm_g, w_co=w_co, ffn_norm_g=ffn_norm_g,
             w_gate_up=w_gate_up, w_down=w_down)
    m = dict(mix_norm_g=m_mix_norm_g, w_in=m_w_in, q_norm_g=m_q_norm_g, k_norm_g=m_k_norm_g, pool_w=m_pool_w,
             pool_scale=m_pool_scale, w_out=m_w_out, cross_norm_g=m_cross_norm_g, mem_norm_g=m_mem_norm_g,
             w_cq=m_w_cq, w_ckv=m_w_ckv, cq_norm_g=m_cq_norm_g, ck_norm_g=m_ck_norm_g, w_co=m_w_co,
             ffn_norm_g=m_ffn_norm_g, w_gate_up=m_w_gate_up, w_down=m_w_down)
    v = dict(mix_norm_g=v_mix_norm_g, w_in=v_w_in, q_norm_g=v_q_norm_g, k_norm_g=v_k_norm_g, pool_w=v_pool_w,
             pool_scale=v_pool_scale, w_out=v_w_out, cross_norm_g=v_cross_norm_g, mem_norm_g=v_mem_norm_g,
             w_cq=v_w_cq, w_ckv=v_w_ckv, cq_norm_g=v_cq_norm_g, ck_norm_g=v_ck_norm_g, w_co=v_w_co,
             ffn_norm_g=v_ffn_norm_g, w_gate_up=v_w_gate_up, w_down=v_w_down)

    c_arr = lax.axis_index("c").astype(jnp.int32).reshape(1)
    k_arr = (2 * lax.axis_index("x") + lax.axis_index("y")).astype(jnp.int32).reshape(1)
    kc_arr = jnp.concatenate([k_arr, c_arr])
    wb = {k: _cast_piece(f"cast_{k}", w[k][0], k_arr) for k in BIG}
    sm = {k: (w[k][0] if k == "pool_w" else w[k]) for k in SMALL}
    loss_part, dx, gshard, gsm = _local_step(x[0], mem[0], positions[0], loss_target[0], wb, sm,
                                             place=(c_arr, kc_arr))
    loss = lax.psum(loss_part, ("x", "y", "c"))

    grads, deltas, new_m, new_v = {}, {}, {}, {}
    for k in BIG:
        shard = w[k][0]
        tr = shard.shape[0] // 4
        d, nm, nv = _adamw(f"adamw_{k}", shard, gshard[k], m[k][0], v[k][0], tr=tr)
        grads[k], deltas[k], new_m[k], new_v[k] = gshard[k][None], d[None], nm[None], nv[None]
    smw = {k: (w[k][0] if k == "pool_w" else w[k]) for k in SMALL}
    smm = {k: (m[k][0] if k == "pool_w" else m[k]) for k in SMALL}
    smv = {k: (v[k][0] if k == "pool_w" else v[k]) for k in SMALL}
    pw, pg, pm, pv = _pack_small(smw), _pack_small(gsm), _pack_small(smm), _pack_small(smv)
    d, nm, nv = _adamw("adamw_small", pw, pg, pm, pv, tr=pw.shape[0])
    for dst, packed in ((deltas, d), (new_m, nm), (new_v, nv)):
        un = _unpack_small(packed, sm)
        for k in SMALL:
            dst[k] = un[k].reshape(w[k].shape)
    for k in SMALL:
        grads[k] = gsm[k].reshape(w[k].shape)

    return (loss, dx[None], *[grads[k] for k in WEIGHTS], *[deltas[k] for k in WEIGHTS],
            *[new_m[k] for k in WEIGHTS], *[new_v[k] for k in WEIGHTS])
```
